```python
import math
import jax, jax.numpy as jnp
from jax import lax
import numpy as np

D_MODEL = 1024
BATCH = 8
SEQ = 2048
DEPTH = 2

SSD_HEADS = 16
SSD_HEAD_DIM = 64
SSD_INNER = SSD_HEADS * SSD_HEAD_DIM
SSD_GROUPS = 2
SSD_STATE = 128
SSD_CONV = 4
SSD_CHUNK = 64
GDN_HEADS = 4
GDN_HEAD_DIM = 128
GDN_WIDTH = GDN_HEADS * GDN_HEAD_DIM
GDN_CONV = 4
GDN_CHUNK = 64
GLA_HEADS = 4
GLA_KEY_DIM = 64
GLA_VAL_DIM = 128
GLA_K = GLA_HEADS * GLA_KEY_DIM
GLA_V = GLA_HEADS * GLA_VAL_DIM
GLA_GATE_RANK = 16
GLA_GATE_NORMALIZER = 16.0
GLA_CHUNK = 16
FFN_DIM = 2816
FFN_CONV = 3
N_BRANCH = 3
ALPHA = (2 * DEPTH) ** 0.25
BETA = (8 * DEPTH) ** -0.25
LN_EPS = 1e-5
RMS_EPS = 1e-6
DT_MIN = 0.001
DT_MAX = 0.1

SPLITS = (
    SSD_INNER,
    SSD_INNER + 2 * SSD_GROUPS * SSD_STATE,
    SSD_HEADS,
    3 * GDN_WIDTH,
    GDN_HEADS,
    GDN_HEADS,
    GDN_WIDTH,
    2 * GLA_K + GLA_V,
    GLA_GATE_RANK,
    GLA_V,
    N_BRANCH * D_MODEL,
)
IN_DIM = int(sum(SPLITS))
SPLIT_IDX = tuple(int(i) for i in np.cumsum(SPLITS)[:-1])

kernel_name = "hybrid_ssd_gdn_gla_deepnorm"


def causal_dwconv(x, w, b=None):
    K = w.shape[0]
    T = x.shape[1]
    xp = jnp.pad(x, ((0, 0), (K - 1, 0), (0, 0)))
    y = xp[:, 0:T] * w[0]
    for k in range(1, K):
        y = y + xp[:, k:k + T] * w[k]
    if b is not None:
        y = y + b
    return y


def layer_norm(x, g, b):
    xf = x.astype(jnp.float32)
    mu = jnp.mean(xf, -1, keepdims=True)
    var = jnp.mean(jnp.square(xf - mu), -1, keepdims=True)
    return ((xf - mu) * lax.rsqrt(var + LN_EPS) * g + b).astype(x.dtype)


def rms_norm(x, w):
    return x * lax.rsqrt(jnp.mean(x * x, -1, keepdims=True) + RMS_EPS) * w


def l2norm(x):
    return x * lax.rsqrt(jnp.sum(x * x, -1, keepdims=True) + RMS_EPS)


def ssd_mixer(z, xbc, dt_raw, conv_w, conv_b, dt_bias, a_log, d_skip, norm_w):
    f32 = jnp.float32
    Bsz, T, _ = z.shape
    L = SSD_CHUNK
    nc = T // L
    G = SSD_GROUPS
    R = SSD_HEADS // G
    P = SSD_HEAD_DIM
    N = SSD_STATE
    xbc = jax.nn.silu(causal_dwconv(xbc, conv_w, conv_b)).astype(f32)
    xs, Bm, Cm = jnp.split(xbc, [SSD_INNER, SSD_INNER + G * N], axis=-1)
    xs = xs.reshape(Bsz, nc, L, G, R, P)
    Bm = Bm.reshape(Bsz, nc, L, G, N)
    Cm = Cm.reshape(Bsz, nc, L, G, N)
    dt = jax.nn.softplus(dt_raw.astype(f32) + dt_bias.astype(f32)).reshape(Bsz, nc, L, G, R)
    A = -jnp.exp(a_log.astype(f32)).reshape(G, R)
    a_cs = jnp.cumsum(dt * A, axis=2)
    causal = jnp.tril(jnp.ones((L, L), bool))[:, :, None, None]
    seg = a_cs[:, :, :, None] - a_cs[:, :, None]
    decay = jnp.exp(jnp.where(causal, seg, -jnp.inf))
    CB = jnp.einsum('bclgn,bcsgn->bclsg', Cm, Bm)
    W = CB[..., None] * decay * dt[:, :, None]
    y = jnp.einsum('bclsgr,bcsgrp->bclgrp', W, xs)
    dec_end = jnp.exp(a_cs[:, :, -1:] - a_cs)
    states = jnp.einsum('bclgn,bclgr,bclgrp->bcgrpn', Bm, dec_end * dt, xs)
    chunk_decay = jnp.exp(a_cs[:, :, -1])

    def step(S, inp):
        st, cd = inp
        return S * cd[..., None, None] + st, S

    S0 = jnp.zeros((Bsz, G, R, P, N), f32)
    _, S_prev = lax.scan(step, S0, (jnp.moveaxis(states, 1, 0), jnp.moveaxis(chunk_decay, 1, 0)))
    S_prev = jnp.moveaxis(S_prev, 0, 1)
    y = y + jnp.einsum('bclgn,bcgrpn,bclgr->bclgrp', Cm, S_prev, jnp.exp(a_cs))
    y = y + d_skip.astype(f32).reshape(G, R)[:, :, None] * xs
    y = y.reshape(Bsz, T, SSD_INNER) * jax.nn.silu(z.astype(f32))
    y = rms_norm(y.reshape(Bsz, T, G, SSD_INNER // G), 1.0).reshape(Bsz, T, SSD_INNER)
    return (y * norm_w).astype(z.dtype)


def gdn_mixer(qkv, a_raw, b_raw, gate, conv_w, a_log, dt_bias, norm_w):
    f32 = jnp.float32
    Bsz, T, _ = qkv.shape
    H, Dh, L = GDN_HEADS, GDN_HEAD_DIM, GDN_CHUNK
    nc = T // L
    qkv = jax.nn.silu(causal_dwconv(qkv, conv_w)).astype(f32)
    q, k, v = jnp.split(qkv, 3, axis=-1)

    def heads(t):
        return t.reshape(Bsz, nc, L, H, Dh).transpose(0, 1, 3, 2, 4)

    def per_head(t):
        return t.reshape(Bsz, nc, L, H).transpose(0, 1, 3, 2)

    q = l2norm(heads(q)) * (Dh ** -0.5)
    k = l2norm(heads(k))
    v = heads(v)
    beta = per_head(jax.nn.sigmoid(b_raw.astype(f32)))
    g = -jnp.exp(a_log.astype(f32)) * jax.nn.softplus(a_raw.astype(f32) + dt_bias.astype(f32))
    g_cs = jnp.cumsum(per_head(g), -1)
    incl = jnp.tril(jnp.ones((L, L), bool))
    strict = jnp.tril(jnp.ones((L, L), bool), -1)
    gamma = jnp.exp(jnp.where(incl, g_cs[..., :, None] - g_cs[..., None, :], -jnp.inf))
    kb = k * beta[..., None]
    A = jnp.where(strict, jnp.einsum('bchld,bchsd->bchls', kb, k) * gamma, 0.0)
    eye = jnp.eye(L, dtype=f32)
    Tm = lax.linalg.triangular_solve(eye + A, jnp.broadcast_to(eye, A.shape),
                                     left_side=True, lower=True)
    u = jnp.einsum('bchls,bchsd->bchld', Tm, v * beta[..., None])
    w = jnp.einsum('bchls,bchsd->bchld', Tm, kb * jnp.exp(g_cs)[..., None])
    attn = jnp.where(incl, jnp.einsum('bchld,bchsd->bchls', q, k) * gamma, 0.0)
    q_dec = q * jnp.exp(g_cs)[..., None]
    k_dec = k * jnp.exp(g_cs[..., -1:] - g_cs)[..., None]
    chunk_decay = jnp.exp(g_cs[..., -1])

    def step(S, inp):
        u_c, w_c, qd_c, kd_c, a_c, cd_c = inp
        v_new = u_c - jnp.einsum('bhld,bhdv->bhlv', w_c, S)
        o = jnp.einsum('bhld,bhdv->bhlv', qd_c, S) + jnp.einsum('bhls,bhsv->bhlv', a_c, v_new)
        S = S * cd_c[..., None, None] + jnp.einsum('bhld,bhlv->bhdv', kd_c, v_new)
        return S, o

    seq_in = tuple(jnp.moveaxis(t, 1, 0) for t in (u, w, q_dec, k_dec, attn, chunk_decay))
    _, o = lax.scan(step, jnp.zeros((Bsz, H, Dh, Dh), f32), seq_in)
    o = o.transpose(1, 0, 3, 2, 4).reshape(Bsz, T, H, Dh)
    o = rms_norm(o, norm_w.astype(f32)) * jax.nn.silu(gate.astype(f32).reshape(Bsz, T, H, Dh))
    return o.reshape(Bsz, T, GDN_WIDTH).astype(gate.dtype)


def gla_mixer(qkv, g_lr, r, gate_w2, gate_b, norm_w):
    f32 = jnp.float32
    Bsz, T, _ = qkv.shape
    H, K, V, L = GLA_HEADS, GLA_KEY_DIM, GLA_VAL_DIM, GLA_CHUNK
    nc = T // L
    q, k, v = jnp.split(qkv.astype(f32), [GLA_K, 2 * GLA_K], axis=-1)
    gk = jax.nn.log_sigmoid(g_lr.astype(f32) @ gate_w2.astype(f32) + gate_b) / GLA_GATE_NORMALIZER
    q = q.reshape(Bsz, nc, L, H, K) * (K ** -0.5)
    k = k.reshape(Bsz, nc, L, H, K)
    v = v.reshape(Bsz, nc, L, H, V)
    b_cs = jnp.cumsum(gk.reshape(Bsz, nc, L, H, K), axis=2)
    q_e = q * jnp.exp(b_cs)
    k_e = k * jnp.exp(-b_cs)
    incl = jnp.tril(jnp.ones((L, L), bool))
    A = jnp.where(incl, jnp.einsum('bclhk,bcshk->bchls', q_e, k_e), 0.0)
    o = jnp.einsum('bchls,bcshv->bclhv', A, v)
    k_d = k * jnp.exp(b_cs[:, :, -1:] - b_cs)
    states = jnp.einsum('bclhk,bclhv->bchkv', k_d, v)
    chunk_decay = jnp.exp(b_cs[:, :, -1])

    def step(S, inp):
        st, cd = inp
        return S * cd[..., None] + st, S

    _, S_prev = lax.scan(step, jnp.zeros((Bsz, H, K, V), f32),
                         (jnp.moveaxis(states, 1, 0), jnp.moveaxis(chunk_decay, 1, 0)))
    S_prev = jnp.moveaxis(S_prev, 0, 1)
    o = o + jnp.einsum('bclhk,bchkv->bclhv', q_e, S_prev)
    o = o.reshape(Bsz, T, H, V)
    o = rms_norm(o, norm_w.astype(f32)) * jax.nn.silu(r.astype(f32).reshape(Bsz, T, H, V))
    return o.reshape(Bsz, T, GLA_V).astype(r.dtype)


def conv_glu_ffn(x, w_up, conv_w, conv_b, w_down):
    h = causal_dwconv(x @ w_up, conv_w, conv_b)
    g, u = jnp.split(h, 2, axis=-1)
    return (jax.nn.silu(g) * u) @ w_down


def _fwd_setup_inputs(seed: int = 0) -> dict:
    key = jax.random.key(seed)
    keys = jax.random.split(key, 32)
    counter = iter(range(32))

    def nrm(shape, scale):
        return jax.random.normal(keys[next(counter)], shape, jnp.float32) * scale

    def unif(shape, lo, hi):
        return jax.random.uniform(keys[next(counter)], shape, jnp.float32, lo, hi)

    def dt_bias(n):
        dt = jnp.exp(unif((DEPTH, n), math.log(DT_MIN), math.log(DT_MAX)))
        return dt + jnp.log(-jnp.expm1(-dt))

    Dm = D_MODEL
    ssd_conv_ch = SSD_INNER + 2 * SSD_GROUPS * SSD_STATE
    return {
        "x": nrm((BATCH, SEQ, Dm), 1.0),
        "w_in": nrm((DEPTH, Dm, IN_DIM), Dm ** -0.5),
        "ssd_conv_w": nrm((DEPTH, SSD_CONV, ssd_conv_ch), SSD_CONV ** -0.5),
        "ssd_conv_b": nrm((DEPTH, ssd_conv_ch), 0.02),
        "ssd_dt_bias": dt_bias(SSD_HEADS),
        "ssd_a_log": jnp.log(unif((DEPTH, SSD_HEADS), 1.0, 16.0)),
        "ssd_d": 1.0 + nrm((DEPTH, SSD_HEADS), 0.02),
        "ssd_norm_w": 1.0 + nrm((DEPTH, SSD_INNER), 0.02),
        "gdn_conv_w": nrm((DEPTH, GDN_CONV, 3 * GDN_WIDTH), GDN_CONV ** -0.5),
        "gdn_a_log": jnp.log(unif((DEPTH, GDN_HEADS), 1.0, 16.0)),
        "gdn_dt_bias": dt_bias(GDN_HEADS),
        "gdn_norm_w": 1.0 + nrm((DEPTH, GDN_HEAD_DIM), 0.02),
        "gla_gate_w2": nrm((DEPTH, GLA_GATE_RANK, GLA_K), GLA_GATE_RANK ** -0.5),
        "gla_gate_b": nrm((DEPTH, GLA_K), 0.1),
        "gla_norm_w": 1.0 + nrm((DEPTH, GLA_VAL_DIM), 0.02),
        "w_br_ssd": nrm((DEPTH, SSD_INNER, Dm), BETA * SSD_INNER ** -0.5),
        "w_br_gdn": nrm((DEPTH, GDN_WIDTH, Dm), BETA * GDN_WIDTH ** -0.5),
        "w_br_gla": nrm((DEPTH, GLA_V, Dm), BETA * GLA_V ** -0.5),
        "w_out": nrm((DEPTH, Dm, Dm), BETA * Dm ** -0.5),
        "ln1_g": 1.0 + nrm((DEPTH, Dm), 0.02),
        "ln1_b": nrm((DEPTH, Dm), 0.02),
        "ffn_w_up": nrm((DEPTH, Dm, 2 * FFN_DIM), Dm ** -0.5),
        "ffn_conv_w": nrm((DEPTH, FFN_CONV, 2 * FFN_DIM), FFN_CONV ** -0.5),
        "ffn_conv_b": nrm((DEPTH, 2 * FFN_DIM), 0.02),
        "ffn_w_down": nrm((DEPTH, FFN_DIM, Dm), BETA * FFN_DIM ** -0.5),
        "ln2_g": 1.0 + nrm((DEPTH, Dm), 0.02),
        "ln2_b": nrm((DEPTH, Dm), 0.02),
    }


def _fwd_reference(x, w_in, ssd_conv_w, ssd_conv_b, ssd_dt_bias, ssd_a_log, ssd_d, ssd_norm_w,
              gdn_conv_w, gdn_a_log, gdn_dt_bias, gdn_norm_w, gla_gate_w2, gla_gate_b, gla_norm_w,
              w_br_ssd, w_br_gdn, w_br_gla, w_out, ln1_g, ln1_b,
              ffn_w_up, ffn_conv_w, ffn_conv_b, ffn_w_down, ln2_g, ln2_b):
    Bsz, T, Dm = x.shape
    for l in range(DEPTH):
        h = x @ w_in[l]
        (z, xbc, dt_raw, gdn_qkv, gdn_a, gdn_b, gdn_g,
         gla_qkv, gla_glr, gla_r, gate_logits) = jnp.split(h, SPLIT_IDX, axis=-1)
        y_ssd = ssd_mixer(z, xbc, dt_raw, ssd_conv_w[l], ssd_conv_b[l], ssd_dt_bias[l],
                          ssd_a_log[l], ssd_d[l], ssd_norm_w[l]) @ w_br_ssd[l]
        y_gdn = gdn_mixer(gdn_qkv, gdn_a, gdn_b, gdn_g, gdn_conv_w[l], gdn_a_log[l],
                          gdn_dt_bias[l], gdn_norm_w[l]) @ w_br_gdn[l]
        y_gla = gla_mixer(gla_qkv, gla_glr, gla_r, gla_gate_w2[l], gla_gate_b[l],
                          gla_norm_w[l]) @ w_br_gla[l]
        gates = jax.nn.sigmoid(gate_logits).reshape(Bsz, T, N_BRANCH, Dm)
        mix = gates[:, :, 0] * y_ssd + gates[:, :, 1] * y_gdn + gates[:, :, 2] * y_gla
        x = layer_norm(ALPHA * x + mix @ w_out[l], ln1_g[l], ln1_b[l])
        x = layer_norm(ALPHA * x + conv_glu_ffn(x, ffn_w_up[l], ffn_conv_w[l], ffn_conv_b[l],
                                                ffn_w_down[l]), ln2_g[l], ln2_b[l])
    return x


import jax as _jax
import jax.numpy as _jnp

TWIN_FORMAT = 'train_step'
FWD_PARAMS = ['x', 'w_in', 'ssd_conv_w', 'ssd_conv_b', 'ssd_dt_bias', 'ssd_a_log', 'ssd_d', 'ssd_norm_w', 'gdn_conv_w', 'gdn_a_log', 'gdn_dt_bias', 'gdn_norm_w', 'gla_gate_w2', 'gla_gate_b', 'gla_norm_w', 'w_br_ssd', 'w_br_gdn', 'w_br_gla', 'w_out', 'ln1_g', 'ln1_b', 'ffn_w_up', 'ffn_conv_w', 'ffn_conv_b', 'ffn_w_down', 'ln2_g', 'ln2_b']
TWIN_WEIGHTS = ['w_in', 'ssd_conv_w', 'ssd_conv_b', 'ssd_dt_bias', 'ssd_a_log', 'ssd_d', 'ssd_norm_w', 'gdn_conv_w', 'gdn_a_log', 'gdn_dt_bias', 'gdn_norm_w', 'gla_gate_w2', 'gla_gate_b', 'gla_norm_w', 'w_br_ssd', 'w_br_gdn', 'w_br_gla', 'w_out', 'ln1_g', 'ln1_b', 'ffn_w_up', 'ffn_conv_w', 'ffn_conv_b', 'ffn_w_down', 'ln2_g', 'ln2_b']
TWIN_DIFF_INPUT = 'x'
TWIN_INPUTS = ['x', 'w_in', 'ssd_conv_w', 'ssd_conv_b', 'ssd_dt_bias', 'ssd_a_log', 'ssd_d', 'ssd_norm_w', 'gdn_conv_w', 'gdn_a_log', 'gdn_dt_bias', 'gdn_norm_w', 'gla_gate_w2', 'gla_gate_b', 'gla_norm_w', 'w_br_ssd', 'w_br_gdn', 'w_br_gla', 'w_out', 'ln1_g', 'ln1_b', 'ffn_w_up', 'ffn_conv_w', 'ffn_conv_b', 'ffn_w_down', 'ln2_g', 'ln2_b', 'loss_target', 'm_w_in', 'm_ssd_conv_w', 'm_ssd_conv_b', 'm_ssd_dt_bias', 'm_ssd_a_log', 'm_ssd_d', 'm_ssd_norm_w', 'm_gdn_conv_w', 'm_gdn_a_log', 'm_gdn_dt_bias', 'm_gdn_norm_w', 'm_gla_gate_w2', 'm_gla_gate_b', 'm_gla_norm_w', 'm_w_br_ssd', 'm_w_br_gdn', 'm_w_br_gla', 'm_w_out', 'm_ln1_g', 'm_ln1_b', 'm_ffn_w_up', 'm_ffn_conv_w', 'm_ffn_conv_b', 'm_ffn_w_down', 'm_ln2_g', 'm_ln2_b', 'v_w_in', 'v_ssd_conv_w', 'v_ssd_conv_b', 'v_ssd_dt_bias', 'v_ssd_a_log', 'v_ssd_d', 'v_ssd_norm_w', 'v_gdn_conv_w', 'v_gdn_a_log', 'v_gdn_dt_bias', 'v_gdn_norm_w', 'v_gla_gate_w2', 'v_gla_gate_b', 'v_gla_norm_w', 'v_w_br_ssd', 'v_w_br_gdn', 'v_w_br_gla', 'v_w_out', 'v_ln1_g', 'v_ln1_b', 'v_ffn_w_up', 'v_ffn_conv_w', 'v_ffn_conv_b', 'v_ffn_w_down', 'v_ln2_g', 'v_ln2_b']
TWIN_OUTPUTS = ['loss', 'grad_x', 'grad_w_in', 'grad_ssd_conv_w', 'grad_ssd_conv_b', 'grad_ssd_dt_bias', 'grad_ssd_a_log', 'grad_ssd_d', 'grad_ssd_norm_w', 'grad_gdn_conv_w', 'grad_gdn_a_log', 'grad_gdn_dt_bias', 'grad_gdn_norm_w', 'grad_gla_gate_w2', 'grad_gla_gate_b', 'grad_gla_norm_w', 'grad_w_br_ssd', 'grad_w_br_gdn', 'grad_w_br_gla', 'grad_w_out', 'grad_ln1_g', 'grad_ln1_b', 'grad_ffn_w_up', 'grad_ffn_conv_w', 'grad_ffn_conv_b', 'grad_ffn_w_down', 'grad_ln2_g', 'grad_ln2_b', 'delta_w_in', 'delta_ssd_conv_w', 'delta_ssd_conv_b', 'delta_ssd_dt_bias', 'delta_ssd_a_log', 'delta_ssd_d', 'delta_ssd_norm_w', 'delta_gdn_conv_w', 'delta_gdn_a_log', 'delta_gdn_dt_bias', 'delta_gdn_norm_w', 'delta_gla_gate_w2', 'delta_gla_gate_b', 'delta_gla_norm_w', 'delta_w_br_ssd', 'delta_w_br_gdn', 'delta_w_br_gla', 'delta_w_out', 'delta_ln1_g', 'delta_ln1_b', 'delta_ffn_w_up', 'delta_ffn_conv_w', 'delta_ffn_conv_b', 'delta_ffn_w_down', 'delta_ln2_g', 'delta_ln2_b', 'new_m_w_in', 'new_m_ssd_conv_w', 'new_m_ssd_conv_b', 'new_m_ssd_dt_bias', 'new_m_ssd_a_log', 'new_m_ssd_d', 'new_m_ssd_norm_w', 'new_m_gdn_conv_w', 'new_m_gdn_a_log', 'new_m_gdn_dt_bias', 'new_m_gdn_norm_w', 'new_m_gla_gate_w2', 'new_m_gla_gate_b', 'new_m_gla_norm_w', 'new_m_w_br_ssd', 'new_m_w_br_gdn', 'new_m_w_br_gla', 'new_m_w_out', 'new_m_ln1_g', 'new_m_ln1_b', 'new_m_ffn_w_up', 'new_m_ffn_conv_w', 'new_m_ffn_conv_b', 'new_m_ffn_w_down', 'new_m_ln2_g', 'new_m_ln2_b', 'new_v_w_in', 'new_v_ssd_conv_w', 'new_v_ssd_conv_b', 'new_v_ssd_dt_bias', 'new_v_ssd_a_log', 'new_v_ssd_d', 'new_v_ssd_norm_w', 'new_v_gdn_conv_w', 'new_v_gdn_a_log', 'new_v_gdn_dt_bias', 'new_v_gdn_norm_w', 'new_v_gla_gate_w2', 'new_v_gla_gate_b', 'new_v_gla_norm_w', 'new_v_w_br_ssd', 'new_v_w_br_gdn', 'new_v_w_br_gla', 'new_v_w_out', 'new_v_ln1_g', 'new_v_ln1_b', 'new_v_ffn_w_up', 'new_v_ffn_conv_w', 'new_v_ffn_conv_b', 'new_v_ffn_w_down', 'new_v_ln2_g', 'new_v_ln2_b']
TWIN_LEAF_KINDS = {'loss': 'loss', 'grad_x': 'grad_x', 'grad_w_in': 'grad_w', 'grad_ssd_conv_w': 'grad_w', 'grad_ssd_conv_b': 'grad_w', 'grad_ssd_dt_bias': 'grad_w', 'grad_ssd_a_log': 'grad_w', 'grad_ssd_d': 'grad_w', 'grad_ssd_norm_w': 'grad_w', 'grad_gdn_conv_w': 'grad_w', 'grad_gdn_a_log': 'grad_w', 'grad_gdn_dt_bias': 'grad_w', 'grad_gdn_norm_w': 'grad_w', 'grad_gla_gate_w2': 'grad_w', 'grad_gla_gate_b': 'grad_w', 'grad_gla_norm_w': 'grad_w', 'grad_w_br_ssd': 'grad_w', 'grad_w_br_gdn': 'grad_w', 'grad_w_br_gla': 'grad_w', 'grad_w_out': 'grad_w', 'grad_ln1_g': 'grad_w', 'grad_ln1_b': 'grad_w', 'grad_ffn_w_up': 'grad_w', 'grad_ffn_conv_w': 'grad_w', 'grad_ffn_conv_b': 'grad_w', 'grad_ffn_w_down': 'grad_w', 'grad_ln2_g': 'grad_w', 'grad_ln2_b': 'grad_w', 'delta_w_in': 'delta_w', 'delta_ssd_conv_w': 'delta_w', 'delta_ssd_conv_b': 'delta_w', 'delta_ssd_dt_bias': 'delta_w', 'delta_ssd_a_log': 'delta_w', 'delta_ssd_d': 'delta_w', 'delta_ssd_norm_w': 'delta_w', 'delta_gdn_conv_w': 'delta_w', 'delta_gdn_a_log': 'delta_w', 'delta_gdn_dt_bias': 'delta_w', 'delta_gdn_norm_w': 'delta_w', 'delta_gla_gate_w2': 'delta_w', 'delta_gla_gate_b': 'delta_w', 'delta_gla_norm_w': 'delta_w', 'delta_w_br_ssd': 'delta_w', 'delta_w_br_gdn': 'delta_w', 'delta_w_br_gla': 'delta_w', 'delta_w_out': 'delta_w', 'delta_ln1_g': 'delta_w', 'delta_ln1_b': 'delta_w', 'delta_ffn_w_up': 'delta_w', 'delta_ffn_conv_w': 'delta_w', 'delta_ffn_conv_b': 'delta_w', 'delta_ffn_w_down': 'delta_w', 'delta_ln2_g': 'delta_w', 'delta_ln2_b': 'delta_w', 'new_m_w_in': 'new_m', 'new_m_ssd_conv_w': 'new_m', 'new_m_ssd_conv_b': 'new_m', 'new_m_ssd_dt_bias': 'new_m', 'new_m_ssd_a_log': 'new_m', 'new_m_ssd_d': 'new_m', 'new_m_ssd_norm_w': 'new_m', 'new_m_gdn_conv_w': 'new_m', 'new_m_gdn_a_log': 'new_m', 'new_m_gdn_dt_bias': 'new_m', 'new_m_gdn_norm_w': 'new_m', 'new_m_gla_gate_w2': 'new_m', 'new_m_gla_gate_b': 'new_m', 'new_m_gla_norm_w': 'new_m', 'new_m_w_br_ssd': 'new_m', 'new_m_w_br_gdn': 'new_m', 'new_m_w_br_gla': 'new_m', 'new_m_w_out': 'new_m', 'new_m_ln1_g': 'new_m', 'new_m_ln1_b': 'new_m', 'new_m_ffn_w_up': 'new_m', 'new_m_ffn_conv_w': 'new_m', 'new_m_ffn_conv_b': 'new_m', 'new_m_ffn_w_down': 'new_m', 'new_m_ln2_g': 'new_m', 'new_m_ln2_b': 'new_m', 'new_v_w_in': 'new_v', 'new_v_ssd_conv_w': 'new_v', 'new_v_ssd_conv_b': 'new_v', 'new_v_ssd_dt_bias': 'new_v', 'new_v_ssd_a_log': 'new_v', 'new_v_ssd_d': 'new_v', 'new_v_ssd_norm_w': 'new_v', 'new_v_gdn_conv_w': 'new_v', 'new_v_gdn_a_log': 'new_v', 'new_v_gdn_dt_bias': 'new_v', 'new_v_gdn_norm_w': 'new_v', 'new_v_gla_gate_w2': 'new_v', 'new_v_gla_gate_b': 'new_v', 'new_v_gla_norm_w': 'new_v', 'new_v_w_br_ssd': 'new_v', 'new_v_w_br_gdn': 'new_v', 'new_v_w_br_gla': 'new_v', 'new_v_w_out': 'new_v', 'new_v_ln1_g': 'new_v', 'new_v_ln1_b': 'new_v', 'new_v_ffn_w_up': 'new_v', 'new_v_ffn_conv_w': 'new_v', 'new_v_ffn_conv_b': 'new_v', 'new_v_ffn_w_down': 'new_v', 'new_v_ln2_g': 'new_v', 'new_v_ln2_b': 'new_v'}


def _forward(args):
    return _fwd_reference(*[args[k] for k in FWD_PARAMS])


def _output_shape():
    out = _jax.eval_shape(lambda: _forward(_fwd_setup_inputs(0)))
    return out.shape, out.dtype

N_MICROBATCH = 1
ADAM_LR = 0.001
ADAM_B1 = 0.9
ADAM_B2 = 0.999
ADAM_EPS = 1e-08
ADAM_WD = 0.01
ADAM_STEP = 10
PER_EXAMPLE_BATCH_AXIS = {'x': 0, 'loss_target': 0}
SHARED_INPUTS = []
_WEIGHT_DTYPES = {'w_in': _jnp.float32, 'ssd_conv_w': _jnp.float32, 'ssd_conv_b': _jnp.float32, 'ssd_dt_bias': _jnp.float32, 'ssd_a_log': _jnp.float32, 'ssd_d': _jnp.float32, 'ssd_norm_w': _jnp.float32, 'gdn_conv_w': _jnp.float32, 'gdn_a_log': _jnp.float32, 'gdn_dt_bias': _jnp.float32, 'gdn_norm_w': _jnp.float32, 'gla_gate_w2': _jnp.float32, 'gla_gate_b': _jnp.float32, 'gla_norm_w': _jnp.float32, 'w_br_ssd': _jnp.float32, 'w_br_gdn': _jnp.float32, 'w_br_gla': _jnp.float32, 'w_out': _jnp.float32, 'ln1_g': _jnp.float32, 'ln1_b': _jnp.float32, 'ffn_w_up': _jnp.float32, 'ffn_conv_w': _jnp.float32, 'ffn_conv_b': _jnp.float32, 'ffn_w_down': _jnp.float32, 'ln2_g': _jnp.float32, 'ln2_b': _jnp.float32}
MOMENT_SCALE = {'w_in': 9.195109e-03, 'ssd_conv_w': 1.060248e-02, 'ssd_conv_b': 1.657536e-02, 'ssd_dt_bias': 2.230939e-02, 'ssd_a_log': 5.691146e-02, 'ssd_d': 1.036189e-01, 'ssd_norm_w': 1.318297e-02, 'gdn_conv_w': 8.188403e-03, 'gdn_a_log': 5.001753e-02, 'gdn_dt_bias': 4.871427e-02, 'gdn_norm_w': 2.159427e-02, 'gla_gate_w2': 1.665634e-03, 'gla_gate_b': 6.468012e-03, 'gla_norm_w': 2.150755e-02, 'w_br_ssd': 2.585408e-02, 'w_br_gdn': 1.515630e-02, 'w_br_gla': 1.482719e-02, 'w_out': 3.329961e-02, 'ln1_g': 5.428480e-01, 'ln1_b': 2.657415e-01, 'ffn_w_up': 1.664316e-02, 'ffn_conv_w': 1.678703e-02, 'ffn_conv_b': 1.764335e-02, 'ffn_w_down': 5.423689e-02, 'ln2_g': 1.132773e+01, 'ln2_b': 5.240088e-01}


def _to_microbatches(a, axis):
    t = _jnp.moveaxis(a, axis, 0)
    t = t.reshape((N_MICROBATCH, t.shape[0] // N_MICROBATCH) + t.shape[1:])
    return _jnp.moveaxis(t, 1, axis + 1)


def setup_inputs(seed: int = 0) -> dict:
    inp = _fwd_setup_inputs(seed)
    key = _jax.random.fold_in(_jax.random.key(seed), 7919)
    shape, _ = _output_shape()
    out = dict(inp)
    out["loss_target"] = _jax.random.normal(_jax.random.fold_in(key, 0), shape, _jnp.float32)
    for i, name in enumerate(TWIN_WEIGHTS):
        w = inp[name].astype(_jnp.float32)
        if MOMENT_SCALE is None:
            s = _jnp.sqrt(_jnp.mean(_jnp.square(w)) + 1e-30)
        else:
            s = MOMENT_SCALE[name]
        km, kv = _jax.random.split(_jax.random.fold_in(key, i + 1))
        out[name] = w
        out["m_" + name] = s * _jax.random.normal(km, w.shape, _jnp.float32)
        out["v_" + name] = (s * s) * _jax.random.uniform(kv, w.shape, _jnp.float32, 0.5, 1.5)
    if N_MICROBATCH > 1:
        for name, axis in PER_EXAMPLE_BATCH_AXIS.items():
            out[name] = _to_microbatches(out[name], axis)
    return {'x': out['x'], 'w_in': out['w_in'], 'ssd_conv_w': out['ssd_conv_w'], 'ssd_conv_b': out['ssd_conv_b'], 'ssd_dt_bias': out['ssd_dt_bias'], 'ssd_a_log': out['ssd_a_log'], 'ssd_d': out['ssd_d'], 'ssd_norm_w': out['ssd_norm_w'], 'gdn_conv_w': out['gdn_conv_w'], 'gdn_a_log': out['gdn_a_log'], 'gdn_dt_bias': out['gdn_dt_bias'], 'gdn_norm_w': out['gdn_norm_w'], 'gla_gate_w2': out['gla_gate_w2'], 'gla_gate_b': out['gla_gate_b'], 'gla_norm_w': out['gla_norm_w'], 'w_br_ssd': out['w_br_ssd'], 'w_br_gdn': out['w_br_gdn'], 'w_br_gla': out['w_br_gla'], 'w_out': out['w_out'], 'ln1_g': out['ln1_g'], 'ln1_b': out['ln1_b'], 'ffn_w_up': out['ffn_w_up'], 'ffn_conv_w': out['ffn_conv_w'], 'ffn_conv_b': out['ffn_conv_b'], 'ffn_w_down': out['ffn_w_down'], 'ln2_g': out['ln2_g'], 'ln2_b': out['ln2_b'], 'loss_target': out['loss_target'], 'm_w_in': out['m_w_in'], 'm_ssd_conv_w': out['m_ssd_conv_w'], 'm_ssd_conv_b': out['m_ssd_conv_b'], 'm_ssd_dt_bias': out['m_ssd_dt_bias'], 'm_ssd_a_log': out['m_ssd_a_log'], 'm_ssd_d': out['m_ssd_d'], 'm_ssd_norm_w': out['m_ssd_norm_w'], 'm_gdn_conv_w': out['m_gdn_conv_w'], 'm_gdn_a_log': out['m_gdn_a_log'], 'm_gdn_dt_bias': out['m_gdn_dt_bias'], 'm_gdn_norm_w': out['m_gdn_norm_w'], 'm_gla_gate_w2': out['m_gla_gate_w2'], 'm_gla_gate_b': out['m_gla_gate_b'], 'm_gla_norm_w': out['m_gla_norm_w'], 'm_w_br_ssd': out['m_w_br_ssd'], 'm_w_br_gdn': out['m_w_br_gdn'], 'm_w_br_gla': out['m_w_br_gla'], 'm_w_out': out['m_w_out'], 'm_ln1_g': out['m_ln1_g'], 'm_ln1_b': out['m_ln1_b'], 'm_ffn_w_up': out['m_ffn_w_up'], 'm_ffn_conv_w': out['m_ffn_conv_w'], 'm_ffn_conv_b': out['m_ffn_conv_b'], 'm_ffn_w_down': out['m_ffn_w_down'], 'm_ln2_g': out['m_ln2_g'], 'm_ln2_b': out['m_ln2_b'], 'v_w_in': out['v_w_in'], 'v_ssd_conv_w': out['v_ssd_conv_w'], 'v_ssd_conv_b': out['v_ssd_conv_b'], 'v_ssd_dt_bias': out['v_ssd_dt_bias'], 'v_ssd_a_log': out['v_ssd_a_log'], 'v_ssd_d': out['v_ssd_d'], 'v_ssd_norm_w': out['v_ssd_norm_w'], 'v_gdn_conv_w': out['v_gdn_conv_w'], 'v_gdn_a_log': out['v_gdn_a_log'], 'v_gdn_dt_bias': out['v_gdn_dt_bias'], 'v_gdn_norm_w': out['v_gdn_norm_w'], 'v_gla_gate_w2': out['v_gla_gate_w2'], 'v_gla_gate_b': out['v_gla_gate_b'], 'v_gla_norm_w': out['v_gla_norm_w'], 'v_w_br_ssd': out['v_w_br_ssd'], 'v_w_br_gdn': out['v_w_br_gdn'], 'v_w_br_gla': out['v_w_br_gla'], 'v_w_out': out['v_w_out'], 'v_ln1_g': out['v_ln1_g'], 'v_ln1_b': out['v_ln1_b'], 'v_ffn_w_up': out['v_ffn_w_up'], 'v_ffn_conv_w': out['v_ffn_conv_w'], 'v_ffn_conv_b': out['v_ffn_conv_b'], 'v_ffn_w_down': out['v_ffn_w_down'], 'v_ln2_g': out['v_ln2_g'], 'v_ln2_b': out['v_ln2_b']}


def _loss(weights, diff, rest, loss_target):
    with _jax.named_scope("forward"):
        args = {**rest, TWIN_DIFF_INPUT: diff, **{k: w.astype(_WEIGHT_DTYPES[k]) for k, w in weights.items()}}
        y = _forward(args)
    with _jax.named_scope("loss_head"):
        err = _jnp.square(y.astype(_jnp.float32) - loss_target)
        return 0.5 * _jnp.sum(_jnp.mean(err, axis=-1)) if err.ndim else 0.5 * err


def _adamw(w, g, m, v):
    m = ADAM_B1 * m + (1.0 - ADAM_B1) * g
    v = ADAM_B2 * v + (1.0 - ADAM_B2) * _jnp.square(g)
    m_hat = m / (1.0 - ADAM_B1 ** ADAM_STEP)
    v_hat = v / (1.0 - ADAM_B2 ** ADAM_STEP)
    delta = -ADAM_LR * (m_hat / (_jnp.sqrt(v_hat) + ADAM_EPS) + ADAM_WD * w)
    return delta, m, v


def reference(x, w_in, ssd_conv_w, ssd_conv_b, ssd_dt_bias, ssd_a_log, ssd_d, ssd_norm_w, gdn_conv_w, gdn_a_log, gdn_dt_bias, gdn_norm_w, gla_gate_w2, gla_gate_b, gla_norm_w, w_br_ssd, w_br_gdn, w_br_gla, w_out, ln1_g, ln1_b, ffn_w_up, ffn_conv_w, ffn_conv_b, ffn_w_down, ln2_g, ln2_b, loss_target, m_w_in, m_ssd_conv_w, m_ssd_conv_b, m_ssd_dt_bias, m_ssd_a_log, m_ssd_d, m_ssd_norm_w, m_gdn_conv_w, m_gdn_a_log, m_gdn_dt_bias, m_gdn_norm_w, m_gla_gate_w2, m_gla_gate_b, m_gla_norm_w, m_w_br_ssd, m_w_br_gdn, m_w_br_gla, m_w_out, m_ln1_g, m_ln1_b, m_ffn_w_up, m_ffn_conv_w, m_ffn_conv_b, m_ffn_w_down, m_ln2_g, m_ln2_b, v_w_in, v_ssd_conv_w, v_ssd_conv_b, v_ssd_dt_bias, v_ssd_a_log, v_ssd_d, v_ssd_norm_w, v_gdn_conv_w, v_gdn_a_log, v_gdn_dt_bias, v_gdn_norm_w, v_gla_gate_w2, v_gla_gate_b, v_gla_norm_w, v_w_br_ssd, v_w_br_gdn, v_w_br_gla, v_w_out, v_ln1_g, v_ln1_b, v_ffn_w_up, v_ffn_conv_w, v_ffn_conv_b, v_ffn_w_down, v_ln2_g, v_ln2_b):
    given = dict(x=x, w_in=w_in, ssd_conv_w=ssd_conv_w, ssd_conv_b=ssd_conv_b, ssd_dt_bias=ssd_dt_bias, ssd_a_log=ssd_a_log, ssd_d=ssd_d, ssd_norm_w=ssd_norm_w, gdn_conv_w=gdn_conv_w, gdn_a_log=gdn_a_log, gdn_dt_bias=gdn_dt_bias, gdn_norm_w=gdn_norm_w, gla_gate_w2=gla_gate_w2, gla_gate_b=gla_gate_b, gla_norm_w=gla_norm_w, w_br_ssd=w_br_ssd, w_br_gdn=w_br_gdn, w_br_gla=w_br_gla, w_out=w_out, ln1_g=ln1_g, ln1_b=ln1_b, ffn_w_up=ffn_w_up, ffn_conv_w=ffn_conv_w, ffn_conv_b=ffn_conv_b, ffn_w_down=ffn_w_down, ln2_g=ln2_g, ln2_b=ln2_b, loss_target=loss_target, m_w_in=m_w_in, m_ssd_conv_w=m_ssd_conv_w, m_ssd_conv_b=m_ssd_conv_b, m_ssd_dt_bias=m_ssd_dt_bias, m_ssd_a_log=m_ssd_a_log, m_ssd_d=m_ssd_d, m_ssd_norm_w=m_ssd_norm_w, m_gdn_conv_w=m_gdn_conv_w, m_gdn_a_log=m_gdn_a_log, m_gdn_dt_bias=m_gdn_dt_bias, m_gdn_norm_w=m_gdn_norm_w, m_gla_gate_w2=m_gla_gate_w2, m_gla_gate_b=m_gla_gate_b, m_gla_norm_w=m_gla_norm_w, m_w_br_ssd=m_w_br_ssd, m_w_br_gdn=m_w_br_gdn, m_w_br_gla=m_w_br_gla, m_w_out=m_w_out, m_ln1_g=m_ln1_g, m_ln1_b=m_ln1_b, m_ffn_w_up=m_ffn_w_up, m_ffn_conv_w=m_ffn_conv_w, m_ffn_conv_b=m_ffn_conv_b, m_ffn_w_down=m_ffn_w_down, m_ln2_g=m_ln2_g, m_ln2_b=m_ln2_b, v_w_in=v_w_in, v_ssd_conv_w=v_ssd_conv_w, v_ssd_conv_b=v_ssd_conv_b, v_ssd_dt_bias=v_ssd_dt_bias, v_ssd_a_log=v_ssd_a_log, v_ssd_d=v_ssd_d, v_ssd_norm_w=v_ssd_norm_w, v_gdn_conv_w=v_gdn_conv_w, v_gdn_a_log=v_gdn_a_log, v_gdn_dt_bias=v_gdn_dt_bias, v_gdn_norm_w=v_gdn_norm_w, v_gla_gate_w2=v_gla_gate_w2, v_gla_gate_b=v_gla_gate_b, v_gla_norm_w=v_gla_norm_w, v_w_br_ssd=v_w_br_ssd, v_w_br_gdn=v_w_br_gdn, v_w_br_gla=v_w_br_gla, v_w_out=v_w_out, v_ln1_g=v_ln1_g, v_ln1_b=v_ln1_b, v_ffn_w_up=v_ffn_w_up, v_ffn_conv_w=v_ffn_conv_w, v_ffn_conv_b=v_ffn_conv_b, v_ffn_w_down=v_ffn_w_down, v_ln2_g=v_ln2_g, v_ln2_b=v_ln2_b)
    weights = {n: given[n] for n in TWIN_WEIGHTS}
    shared = {n: given[n] for n in SHARED_INPUTS}
    per_example = {n: given[n] for n in ['x']}
    grad_fn = _jax.value_and_grad(_loss, argnums=(0, 1))

    def one_microbatch(ex, loss_target):
        ex = dict(ex)
        diff = ex.pop(TWIN_DIFF_INPUT)
        return grad_fn(weights, diff, {**shared, **ex}, loss_target)

    if N_MICROBATCH == 1:
        loss, (grad_w, grad_x) = one_microbatch(per_example, given["loss_target"])
    else:
        def body(carry, xs):
            loss_sum, grad_sum = carry
            l_k, (gw_k, gx_k) = one_microbatch(xs[0], xs[1])
            with _jax.named_scope("update"):
                return (loss_sum + l_k, _jax.tree.map(_jnp.add, grad_sum, gw_k)), gx_k

        init = (_jnp.zeros((), _jnp.float32), _jax.tree.map(_jnp.zeros_like, weights))
        (loss, grad_w), grad_x = _jax.lax.scan(body, init, (per_example, given["loss_target"]))
    with _jax.named_scope("update"):
        delta_w, new_m, new_v = {}, {}, {}
        for n in TWIN_WEIGHTS:
            delta_w[n], new_m[n], new_v[n] = _adamw(weights[n], grad_w[n], given["m_" + n], given["v_" + n])
    return (loss, grad_x, *[grad_w[n] for n in TWIN_WEIGHTS], *[delta_w[n] for n in TWIN_WEIGHTS],
            *[new_m[n] for n in TWIN_WEIGHTS], *[new_v[n] for n in TWIN_WEIGHTS])
```

```python
import functools
import math

import jax
import jax.numpy as jnp
from jax import lax
from jax.experimental import pallas as pl
from jax.experimental.pallas import tpu as pltpu

F32 = jnp.float32
MXU_DTYPE = jnp.bfloat16
HI = lax.Precision.HIGHEST

N_DEV = 8
D_MODEL = 1024
DEPTH = 2
SSD_HEADS, SSD_HEAD_DIM, SSD_INNER, SSD_GROUPS, SSD_STATE, SSD_CHUNK = 16, 64, 1024, 2, 128, 64
SSD_XBC = SSD_INNER + 2 * SSD_GROUPS * SSD_STATE
GDN_HEADS, GDN_HEAD_DIM, GDN_WIDTH, GDN_CHUNK = 4, 128, 512, 64
GLA_HEADS, GLA_KEY_DIM, GLA_VAL_DIM, GLA_K, GLA_V, GLA_RANK, GLA_CHUNK = 4, 64, 128, 256, 512, 16, 16
GLA_BLOCK = 128
GLA_NORMALIZER = 16.0
FFN_DIM = 2816
FFN_SHARD = 2 * FFN_DIM // 8
FFN_SHARD_PAD = 768
FFN_UP_PAD = 8 * FFN_SHARD_PAD
FFN_PAD = FFN_UP_PAD // 2
ALPHA = (2 * DEPTH) ** 0.25
LN_EPS = 1e-5
RMS_EPS = 1e-6
ADAM_LR, ADAM_B1, ADAM_B2, ADAM_EPS, ADAM_WD, ADAM_STEP = 0.001, 0.9, 0.999, 1e-08, 0.01, 10
LANES = 128
NEG_BIG = -1e30
VMEM_LIMIT = 56 * 1024 * 1024

IN_SPLITS = (("z", 1024), ("xbc", 1536), ("dt", 16), ("gqkv", 1536), ("ga", 4), ("gb", 4), ("gg", 512),
             ("lqkv", 1024), ("lglr", 16), ("lr", 512), ("gates", 3072))
IN_DIM = sum(w for _, w in IN_SPLITS)
PAD_SEGS = (("gates", 0, 3072, (("gates", 0),)), ("xbc", 3072, 1536, (("xbc", 0),)),
            ("gqkv", 4608, 1536, (("gqkv", 0),)), ("z", 6144, 1024, (("z", 0),)),
            ("lqkv", 7168, 1024, (("lqkv", 0),)), ("gg", 8192, 512, (("gg", 0),)), ("lr", 8704, 512, (("lr", 0),)),
            ("dt", 9216, 128, (("dt", 0),)), ("gab", 9344, 128, (("ga", 0), ("gb", 4))), ("lglr", 9472, 128, (("lglr", 0),)))
IN_PAD = 9728
SEG = {name: (off, width) for name, off, width, _ in PAD_SEGS}

BIG = ("w_in", "w_br_ssd", "w_br_gdn", "w_br_gla", "w_out", "ffn_w_up", "ffn_w_down")
COL_SHARDED = ("w_in", "w_br_gdn", "w_br_gla", "ffn_w_up")
SMALL_SHARDED = ("ssd_conv_w", "gdn_conv_w", "gla_gate_w2", "ffn_conv_w")
WEIGHTS = ("w_in", "ssd_conv_w", "ssd_conv_b", "ssd_dt_bias", "ssd_a_log", "ssd_d", "ssd_norm_w", "gdn_conv_w",
           "gdn_a_log", "gdn_dt_bias", "gdn_norm_w", "gla_gate_w2", "gla_gate_b", "gla_norm_w", "w_br_ssd", "w_br_gdn",
           "w_br_gla", "w_out", "ln1_g", "ln1_b", "ffn_w_up", "ffn_conv_w", "ffn_conv_b", "ffn_w_down", "ln2_g", "ln2_b")
SMALL = tuple(n for n in WEIGHTS if n not in BIG)
FLAT_W = 512


def _cparams(sem=None):
    kw = dict(vmem_limit_bytes=VMEM_LIMIT)
    if sem is not None:
        kw["dimension_semantics"] = sem
    return pltpu.CompilerParams(**kw)


_DIMS = {"nn": (((1,), (0,)), ((), ())), "nt": (((1,), (1,)), ((), ())), "tn": (((0,), (0,)), ((), ()))}


def _dot(a, b, dims="nn"):
    if MXU_DTYPE == F32:
        return lax.dot_general(a.astype(F32), b.astype(F32), _DIMS[dims], precision=HI, preferred_element_type=F32)
    return lax.dot_general(a.astype(MXU_DTYPE), b.astype(MXU_DTYPE), _DIMS[dims], preferred_element_type=F32)


def _dot_hi(a, b, dims="nn"):
    return lax.dot_general(a.astype(F32), b.astype(F32), _DIMS[dims], precision=HI, preferred_element_type=F32)


def _iota2(shape, axis):
    return lax.broadcasted_iota(jnp.int32, shape, axis)


def _tril(n, strict=False):
    r, c = _iota2((n, n), 0), _iota2((n, n), 1)
    return (r > c) if strict else (r >= c)


@jax.custom_vjp
def _unit_lower_inverse(a):
    n = a.shape[0]
    eye = (_iota2((n, n), 0) == _iota2((n, n), 1)).astype(F32)
    x = eye - a
    p = a
    k = 2
    while k < n:
        p = _dot_hi(p, p)
        x = x + _dot_hi(x, p)
        k *= 2
    return x


def _unit_lower_inverse_fwd(a):
    t = _unit_lower_inverse(a)
    return t, t


def _unit_lower_inverse_bwd(t, dt):
    return (-_dot_hi(_dot_hi(t, dt, "tn"), t, "nt"),)


_unit_lower_inverse.defvjp(_unit_lower_inverse_fwd, _unit_lower_inverse_bwd)


def _ssd_chunk(xs_, ps_, s_t):
    xbc, dtraw, z = xs_
    dt_bias, a_log, d_skip, norm_w = ps_
    L = xbc.shape[0]
    H, P, N, G = SSD_HEADS, SSD_HEAD_DIM, SSD_STATE, SSD_GROUPS
    W = SSD_INNER // G
    xs = xbc[:, :SSD_INNER]
    bm = xbc[:, SSD_INNER:SSD_INNER + G * N]
    cm = xbc[:, SSD_INNER + G * N:]
    dt = jax.nn.softplus(dtraw[:, :H] + dt_bias)
    a = dt * (-jnp.exp(a_log))
    causal = _tril(L)
    a_cs = _dot_hi(causal.astype(F32), a)
    expand = (_iota2((H, SSD_INNER), 1) // P == _iota2((H, SSD_INNER), 0)).astype(F32)
    wide = _dot_hi(jnp.concatenate([a_cs, dt, jnp.broadcast_to(d_skip, (L, H))], axis=0), expand)
    a_cs_x, dt_x, d_x = wide[:L], wide[L:2 * L], wide[2 * L:]
    a_end_x = a_cs_x[L - 1:L, :]
    a_cs_t, dt_t = a_cs.T, dt.T
    ys = []
    for g in range(G):
        cb = _dot(cm[:, g * N:(g + 1) * N], bm[:, g * N:(g + 1) * N], "nt")
        for r in range(H // G):
            h = g * (H // G) + r
            seg = a_cs[:, h:h + 1] - a_cs_t[h:h + 1, :]
            w = cb * jnp.exp(jnp.where(causal, seg, NEG_BIG)) * dt_t[h:h + 1, :]
            ys.append(_dot(w, xs[:, h * P:(h + 1) * P]))
    y = jnp.concatenate(ys, axis=1)
    y_in = jnp.concatenate([_dot(cm[:, g * N:(g + 1) * N], s_t[:, g * W:(g + 1) * W]) for g in range(G)], axis=1)
    y = y + y_in * jnp.exp(a_cs_x) + d_x * xs
    xw = xs * (jnp.exp(a_end_x - a_cs_x) * dt_x)
    st = jnp.concatenate([_dot(bm[:, g * N:(g + 1) * N], xw[:, g * W:(g + 1) * W], "tn") for g in range(G)], axis=1)
    s_new = s_t * jnp.exp(a_end_x) + st
    yg = y * jax.nn.silu(z)
    outs = []
    for g in range(G):
        part = yg[:, g * W:(g + 1) * W]
        outs.append(part * lax.rsqrt(jnp.mean(part * part, axis=1, keepdims=True) + RMS_EPS))
    return (jnp.concatenate(outs, axis=1) * norm_w,), s_new


def _gdn_chunk(xs_, ps_, s):
    qkv, ab, gate = xs_
    a_log, dt_bias, norm_w = ps_
    L = qkv.shape[0]
    H, D = GDN_HEADS, GDN_HEAD_DIM
    g_all = -jnp.exp(a_log) * jax.nn.softplus(ab + dt_bias)
    incl, strict = _tril(L), _tril(L, strict=True)
    g_cs = _dot_hi(incl.astype(F32), g_all)
    g_cs_t = g_cs.T
    beta_all = jax.nn.sigmoid(ab)
    outs, s_out = [], []
    for h in range(H):
        q = qkv[:, h * D:(h + 1) * D]
        k = qkv[:, GDN_WIDTH + h * D:GDN_WIDTH + (h + 1) * D]
        v = qkv[:, 2 * GDN_WIDTH + h * D:2 * GDN_WIDTH + (h + 1) * D]
        q = q * lax.rsqrt(jnp.sum(q * q, axis=1, keepdims=True) + RMS_EPS) * (D ** -0.5)
        k = k * lax.rsqrt(jnp.sum(k * k, axis=1, keepdims=True) + RMS_EPS)
        beta = beta_all[:, H + h:H + h + 1]
        col = g_cs[:, h:h + 1]
        gamma = jnp.exp(jnp.where(incl, col - g_cs_t[h:h + 1, :], NEG_BIG))
        kb = k * beta
        a_mat = jnp.where(strict, _dot(kb, k, "nt") * gamma, 0.0)
        t_mat = _unit_lower_inverse(a_mat)
        u = _dot(t_mat, v * beta)
        w = _dot(t_mat, kb * jnp.exp(col))
        attn = jnp.where(incl, _dot(q, k, "nt") * gamma, 0.0)
        g_end = col[L - 1:L, :]
        s_h = s[h * D:(h + 1) * D, :]
        v_new = u - _dot(w, s_h)
        o = _dot(q * jnp.exp(col), s_h) + _dot(attn, v_new)
        s_out.append(s_h * jnp.exp(g_end) + _dot(k * jnp.exp(g_end - col), v_new, "tn"))
        o = o * lax.rsqrt(jnp.mean(o * o, axis=1, keepdims=True) + RMS_EPS) * norm_w
        outs.append(o * jax.nn.silu(gate[:, h * D:(h + 1) * D]))
    return (jnp.concatenate(outs, axis=1),), jnp.concatenate(s_out, axis=0)


def _gla_block(xs_, ps_, s_t):
    qkv, glr, r = xs_
    w2, gate_b, norm_w = ps_
    B = qkv.shape[0]
    H, K, V, C = GLA_HEADS, GLA_KEY_DIM, GLA_VAL_DIM, GLA_CHUNK
    q = qkv[:, :GLA_K] * (K ** -0.5)
    k = qkv[:, GLA_K:2 * GLA_K]
    v = qkv[:, 2 * GLA_K:]
    gk = jax.nn.log_sigmoid(_dot(glr, w2) + gate_b) / GLA_NORMALIZER
    row, col = _iota2((B, B), 0), _iota2((B, B), 1)
    same = (row // C) == (col // C)
    mask = same & (row >= col)
    b_cs = _dot_hi(mask.astype(F32), gk)
    b_end = _dot_hi((col == (row // C) * C + (C - 1)).astype(F32), b_cs)
    q_e = q * jnp.exp(b_cs)
    k_e = k * jnp.exp(-b_cs)
    k_d = k * jnp.exp(b_end - b_cs)
    intra = []
    for h in range(H):
        a_mat = jnp.where(mask, _dot(q_e[:, h * K:(h + 1) * K], k_e[:, h * K:(h + 1) * K], "nt"), 0.0)
        intra.append(_dot(a_mat, v[:, h * V:(h + 1) * V]))
    o = jnp.concatenate(intra, axis=1)
    inter = []
    for j in range(B // C):
        sl = slice(j * C, (j + 1) * C)
        inter.append(jnp.concatenate(
            [_dot(q_e[sl, h * K:(h + 1) * K], s_t[:, h * K:(h + 1) * K], "nt") for h in range(H)], axis=1))
        st = jnp.concatenate([_dot(v[sl, h * V:(h + 1) * V], k_d[sl, h * K:(h + 1) * K], "tn") for h in range(H)], axis=1)
        s_t = s_t * jnp.exp(b_end[j * C:j * C + 1, :]) + st
    o = o + jnp.concatenate(inter, axis=0)
    outs = []
    for h in range(H):
        oh = o[:, h * V:(h + 1) * V]
        oh = oh * lax.rsqrt(jnp.mean(oh * oh, axis=1, keepdims=True) + RMS_EPS) * norm_w
        outs.append(oh * jax.nn.silu(r[:, h * V:(h + 1) * V]))
    return (jnp.concatenate(outs, axis=1),), s_t


def _merge_fn(xs_, ps_):
    gates, y_ssd, y_gdn, y_gla = xs_
    d = D_MODEL
    return (jax.nn.sigmoid(gates[:, :d]) * y_ssd + jax.nn.sigmoid(gates[:, d:2 * d]) * y_gdn
            + jax.nn.sigmoid(gates[:, 2 * d:]) * y_gla,)


def _ln_fn(xs_, ps_):
    x, r = xs_
    g, b = ps_
    t = ALPHA * x + r
    mu = jnp.mean(t, axis=1, keepdims=True)
    var = jnp.mean(jnp.square(t - mu), axis=1, keepdims=True)
    return ((t - mu) * lax.rsqrt(var + LN_EPS) * g + b,)


def _row_spec(rows, width, colblk, n, reverse):
    if reverse:
        return pl.BlockSpec((rows, width), lambda c: (n - 1 - c, colblk))
    return pl.BlockSpec((rows, width), lambda c: (c, colblk))


def _full_spec(shape):
    zeros = (0,) * len(shape)
    return pl.BlockSpec(shape, lambda c: zeros)


def _chain_fwd(name, fn, n, blocked, full, out_defs, state_shape=None):
    nb, nf, no = len(blocked), len(full), len(out_defs)

    def body(*refs):
        xs = [r[...].astype(F32) for r in refs[:nb]]
        ps = [r[...] for r in refs[nb:nb + nf]]
        o_refs = refs[nb + nf:nb + nf + no]
        if state_shape is None:
            outs = fn(xs, ps)
        else:
            sprev_ref, s_ref = refs[nb + nf + no:]

            @pl.when(pl.program_id(0) == 0)
            def _():
                s_ref[...] = jnp.zeros_like(s_ref)

            s = s_ref[...]
            sprev_ref[0] = s
            outs, s_new = fn(xs, ps, s)
            s_ref[...] = s_new
        for r, o in zip(o_refs, outs):
            r[...] = o.astype(r.dtype)

    in_specs = [_row_spec(rows, width, cb, n, False) for _, rows, width, cb in blocked]
    in_specs += [_full_spec(a.shape) for a in full]
    out_specs = [_row_spec(rows, width, 0, n, False) for rows, width, _ in out_defs]
    out_shape = [jax.ShapeDtypeStruct((n * rows, width), dt) for rows, width, dt in out_defs]
    scratch = []
    if state_shape is not None:
        out_specs.append(pl.BlockSpec((1,) + state_shape, lambda c: (c, 0, 0)))
        out_shape.append(jax.ShapeDtypeStruct((n,) + state_shape, F32))
        scratch.append(pltpu.VMEM(state_shape, F32))
    return pl.pallas_call(body, name=name, grid=(n,), in_specs=in_specs, out_specs=out_specs, out_shape=out_shape,
                          scratch_shapes=scratch, compiler_params=_cparams(("arbitrary",)))(
        *[a for a, _, _, _ in blocked], *full)


def _chain_bwd(name, fn, n, blocked, full, douts, sprev=None, dx_dtypes=None):
    nb, nf, nd = len(blocked), len(full), len(douts)
    has_state = sprev is not None
    dx_dtypes = dx_dtypes or [F32] * nb

    def body(*refs):
        pos = 0
        b_refs = refs[pos:pos + nb]; pos += nb
        f_refs = refs[pos:pos + nf]; pos += nf
        d_refs = refs[pos:pos + nd]; pos += nd
        if has_state:
            sprev_ref = refs[pos]; pos += 1
        dx_refs = refs[pos:pos + nb]; pos += nb
        dp_refs = refs[pos:pos + nf]; pos += nf
        if has_state:
            ds_ref = refs[pos]

        @pl.when(pl.program_id(0) == 0)
        def _():
            for r in dp_refs:
                r[...] = jnp.zeros_like(r)
            if has_state:
                ds_ref[...] = jnp.zeros_like(ds_ref)

        xs = [r[...].astype(F32) for r in b_refs]
        ps = [r[...] for r in f_refs]
        dys = tuple(r[...].astype(F32) for r in d_refs)
        if has_state:
            _, vjp = jax.vjp(fn, xs, ps, sprev_ref[0])
            dxs, dps, ds = vjp((dys, ds_ref[...]))
            ds_ref[...] = ds
        else:
            _, vjp = jax.vjp(fn, xs, ps)
            dxs, dps = vjp(dys)
        for r, d in zip(dx_refs, dxs):
            r[...] = d.astype(r.dtype)
        for r, d in zip(dp_refs, dps):
            r[...] += d

    in_specs = [_row_spec(rows, width, cb, n, True) for _, rows, width, cb in blocked]
    in_specs += [_full_spec(a.shape) for a in full]
    in_specs += [_row_spec(rows, width, 0, n, True) for _, rows, width in douts]
    args = [a for a, _, _, _ in blocked] + list(full) + [a for a, _, _ in douts]
    scratch = []
    if has_state:
        st_shape = sprev.shape[1:]
        in_specs.append(pl.BlockSpec((1,) + st_shape, lambda c: (n - 1 - c, 0, 0)))
        args.append(sprev)
        scratch.append(pltpu.VMEM(st_shape, F32))
    out_specs = [_row_spec(rows, width, 0, n, True) for _, rows, width, _ in blocked]
    out_specs += [_full_spec(a.shape) for a in full]
    out_shape = [jax.ShapeDtypeStruct((n * rows, width), dt) for (_, rows, width, _), dt in zip(blocked, dx_dtypes)]
    out_shape += [jax.ShapeDtypeStruct(a.shape, F32) for a in full]
    res = pl.pallas_call(body, name=name, grid=(n,), in_specs=in_specs, out_specs=out_specs, out_shape=out_shape,
                         scratch_shapes=scratch, compiler_params=_cparams(("arbitrary",)))(*args)
    return res[:nb], res[nb:]


def _tile(n, target, unit):
    if n <= target:
        return n
    best = None
    for t in range(unit, target + 1, unit):
        if n % t == 0:
            best = t
    assert best is not None, (n, target, unit)
    return best


def _mm(name, a, b, dims="nn", out_dtype=F32, tm=2048, tn=512, tk=2048):
    if dims == "nn":
        (M, K), (_, N) = a.shape, b.shape
    elif dims == "nt":
        (M, K), (N, _) = a.shape, b.shape
    else:
        (K, M), (_, N) = a.shape, b.shape
    tm, tn, tk = _tile(M, tm, LANES), _tile(N, tn, LANES), _tile(K, tk, LANES)
    nk = K // tk

    def body(a_ref, b_ref, o_ref, acc_ref):
        part = _dot(a_ref[...], b_ref[...], dims)
        if nk == 1:
            o_ref[...] = part.astype(o_ref.dtype)
            return

        @pl.when(pl.program_id(2) == 0)
        def _():
            acc_ref[...] = part

        @pl.when(pl.program_id(2) > 0)
        def _():
            acc_ref[...] += part

        @pl.when(pl.program_id(2) == nk - 1)
        def _():
            o_ref[...] = acc_ref[...].astype(o_ref.dtype)

    if dims == "tn":
        a_spec = pl.BlockSpec((tk, tm), lambda j, i, k: (k, i))
    else:
        a_spec = pl.BlockSpec((tm, tk), lambda j, i, k: (i, k))
    if dims == "nt":
        b_spec = pl.BlockSpec((tn, tk), lambda j, i, k: (j, k))
    else:
        b_spec = pl.BlockSpec((tk, tn), lambda j, i, k: (k, j))
    return pl.pallas_call(
        body, name=name, grid=(N // tn, M // tm, nk), in_specs=[a_spec, b_spec],
        out_specs=pl.BlockSpec((tm, tn), lambda j, i, k: (i, j)), out_shape=jax.ShapeDtypeStruct((M, N), out_dtype),
        scratch_shapes=[pltpu.VMEM((tm, tn) if nk > 1 else (8, LANES), F32)],
        compiler_params=_cparams(("parallel", "parallel", "arbitrary")))(a, b)


CONV_CB = 256


def _shift_down(x, k):
    if k == 0:
        return x
    return jnp.where(_iota2(x.shape, 0) >= k, pltpu.roll(x, k, 0), 0.0)


def _shift_up(x, k):
    if k == 0:
        return x
    t = x.shape[0]
    return jnp.where(_iota2(x.shape, 0) < t - k, pltpu.roll(x, t - k, 0), 0.0)


def _conv_pre(x, w, b):
    kk = w.shape[0]
    pre = x * w[kk - 1:kk, :]
    for k in range(kk - 1):
        pre = pre + _shift_down(x, kk - 1 - k) * w[k:k + 1, :]
    return pre if b is None else pre + b


def _conv_bwd_pre(x, w, dpre, dw_ref, db_ref):
    kk = w.shape[0]
    dx = dpre * w[kk - 1:kk, :]
    dw_ref[kk - 1:kk, :] = jnp.sum(dpre * x, axis=0, keepdims=True)
    for k in range(kk - 1):
        dx = dx + _shift_up(dpre, kk - 1 - k) * w[k:k + 1, :]
        dw_ref[k:k + 1, :] = jnp.sum(dpre * _shift_down(x, kk - 1 - k), axis=0, keepdims=True)
    if db_ref is not None:
        db_ref[...] = jnp.sum(dpre, axis=0, keepdims=True)
    return dx


def _dsilu(pre):
    sg = jax.nn.sigmoid(pre)
    return sg * (1.0 + pre * (1.0 - sg))


def _conv_silu_fwd(name, src, col0, w, b):
    T = src.shape[0]
    kk, C = w.shape
    cb = CONV_CB
    off = col0 // cb

    def body(*refs):
        x_ref, w_ref = refs[:2]
        b_val = refs[2][...] if b is not None else None
        refs[-1][...] = jax.nn.silu(_conv_pre(x_ref[...], w_ref[...], b_val))

    in_specs = [pl.BlockSpec((T, cb), lambda j: (0, off + j)), pl.BlockSpec((kk, cb), lambda j: (0, j))]
    args = [src, w]
    if b is not None:
        in_specs.append(pl.BlockSpec((1, cb), lambda j: (0, j)))
        args.append(b)
    return pl.pallas_call(body, name=name, grid=(C // cb,), in_specs=in_specs,
                          out_specs=pl.BlockSpec((T, cb), lambda j: (0, j)), out_shape=jax.ShapeDtypeStruct((T, C), F32),
                          compiler_params=_cparams(("parallel",)))(*args)


def _conv_silu_bwd(name, src, col0, w, b, dy, dx_dtype):
    T = src.shape[0]
    kk, C = w.shape
    cb = CONV_CB
    off = col0 // cb
    has_b = b is not None

    def body(*refs):
        x_ref, w_ref = refs[:2]
        pos = 2
        b_val = None
        if has_b:
            b_val = refs[pos][...]; pos += 1
        dy_ref = refs[pos]; pos += 1
        dx_ref, dw_ref = refs[pos], refs[pos + 1]
        db_ref = refs[pos + 2] if has_b else None
        x, wv = x_ref[...], w_ref[...]
        dpre = dy_ref[...] * _dsilu(_conv_pre(x, wv, b_val))
        dx_ref[...] = _conv_bwd_pre(x, wv, dpre, dw_ref, db_ref).astype(dx_ref.dtype)

    in_specs = [pl.BlockSpec((T, cb), lambda j: (0, off + j)), pl.BlockSpec((kk, cb), lambda j: (0, j))]
    args = [src, w]
    if has_b:
        in_specs.append(pl.BlockSpec((1, cb), lambda j: (0, j)))
        args.append(b)
    in_specs.append(pl.BlockSpec((T, cb), lambda j: (0, j)))
    args.append(dy)
    out_specs = [pl.BlockSpec((T, cb), lambda j: (0, j)), pl.BlockSpec((kk, cb), lambda j: (0, j))]
    out_shape = [jax.ShapeDtypeStruct((T, C), dx_dtype), jax.ShapeDtypeStruct((kk, C), F32)]
    if has_b:
        out_specs.append(pl.BlockSpec((1, cb), lambda j: (0, j)))
        out_shape.append(jax.ShapeDtypeStruct((1, C), F32))
    return pl.pallas_call(body, name=name, grid=(C // cb,), in_specs=in_specs, out_specs=out_specs, out_shape=out_shape,
                          compiler_params=_cparams(("parallel",)))(*args)


def _ffn_glu_fwd(name, up, w, b, out_dtype=F32):
    T = up.shape[0]
    kk = w.shape[0]
    cb = CONV_CB
    width = up.shape[1] // 2
    nblk = width // cb

    def body(g_ref, u_ref, wg_ref, wu_ref, bg_ref, bu_ref, o_ref):
        g = _conv_pre(g_ref[...], wg_ref[...], bg_ref[...])
        u = _conv_pre(u_ref[...], wu_ref[...], bu_ref[...])
        o_ref[...] = (jax.nn.silu(g) * u).astype(o_ref.dtype)

    lo, hi = (lambda j: (0, j)), (lambda j: (0, nblk + j))
    in_specs = [pl.BlockSpec((T, cb), lo), pl.BlockSpec((T, cb), hi), pl.BlockSpec((kk, cb), lo), pl.BlockSpec((kk, cb), hi),
                pl.BlockSpec((1, cb), lo), pl.BlockSpec((1, cb), hi)]
    return pl.pallas_call(body, name=name, grid=(nblk,), in_specs=in_specs, out_specs=pl.BlockSpec((T, cb), lo),
                          out_shape=jax.ShapeDtypeStruct((T, width), out_dtype),
                          compiler_params=_cparams(("parallel",)))(up, up, w, w, b, b)


def _ffn_glu_bwd(name, up, w, b, dact, dx_dtype):
    T = up.shape[0]
    kk = w.shape[0]
    cb = CONV_CB
    width = up.shape[1] // 2
    nblk = width // cb

    def body(g_ref, u_ref, wg_ref, wu_ref, bg_ref, bu_ref, d_ref, dg_ref, du_ref, dwg_ref, dwu_ref, dbg_ref, dbu_ref):
        xg, xu, wg, wu = g_ref[...], u_ref[...], wg_ref[...], wu_ref[...]
        g = _conv_pre(xg, wg, bg_ref[...])
        u = _conv_pre(xu, wu, bu_ref[...])
        d = d_ref[...].astype(F32)
        dg_ref[...] = _conv_bwd_pre(xg, wg, d * u * _dsilu(g), dwg_ref, dbg_ref).astype(dg_ref.dtype)
        du_ref[...] = _conv_bwd_pre(xu, wu, d * jax.nn.silu(g), dwu_ref, dbu_ref).astype(du_ref.dtype)

    lo, hi = (lambda j: (0, j)), (lambda j: (0, nblk + j))
    in_specs = [pl.BlockSpec((T, cb), lo), pl.BlockSpec((T, cb), hi), pl.BlockSpec((kk, cb), lo), pl.BlockSpec((kk, cb), hi),
                pl.BlockSpec((1, cb), lo), pl.BlockSpec((1, cb), hi), pl.BlockSpec((T, cb), lo)]
    out_specs = [pl.BlockSpec((T, cb), lo)] * 2 + [pl.BlockSpec((kk, cb), lo)] * 2 + [pl.BlockSpec((1, cb), lo)] * 2
    out_shape = ([jax.ShapeDtypeStruct((T, width), dx_dtype)] * 2 + [jax.ShapeDtypeStruct((kk, width), F32)] * 2
                 + [jax.ShapeDtypeStruct((1, width), F32)] * 2)
    return pl.pallas_call(body, name=name, grid=(nblk,), in_specs=in_specs, out_specs=out_specs, out_shape=out_shape,
                          compiler_params=_cparams(("parallel",)))(up, up, w, w, b, b, dact)


def _loss_head(y, target):
    T, D = y.shape
    tb = _tile(T, 256, 8)

    def body(y_ref, t_ref, dy_ref, l_ref):
        @pl.when(pl.program_id(0) == 0)
        def _():
            l_ref[...] = jnp.zeros_like(l_ref)

        err = y_ref[...] - t_ref[...]
        dy_ref[...] = err * (1.0 / D)
        l_ref[...] += jnp.sum(err * err, axis=0, keepdims=True) * (0.5 / D)

    spec = pl.BlockSpec((tb, D), lambda i: (i, 0))
    return pl.pallas_call(body, name="loss_head", grid=(T // tb,), in_specs=[spec, spec],
                          out_specs=[spec, pl.BlockSpec((1, D), lambda i: (0, 0))],
                          out_shape=[jax.ShapeDtypeStruct((T, D), F32), jax.ShapeDtypeStruct((1, D), F32)],
                          compiler_params=_cparams(("arbitrary",)))(y, target)


def _adamw_math(w, g, m, v):
    m = ADAM_B1 * m + (1.0 - ADAM_B1) * g
    v = ADAM_B2 * v + (1.0 - ADAM_B2) * jnp.square(g)
    m_hat = m / (1.0 - ADAM_B1 ** ADAM_STEP)
    v_hat = v / (1.0 - ADAM_B2 ** ADAM_STEP)
    return -ADAM_LR * (m_hat / (jnp.sqrt(v_hat) + ADAM_EPS) + ADAM_WD * w), m, v


def _adamw(name, w, g, m, v):
    A, R, C = w.shape
    rb = _tile(R, max(8, (1 << 19) // (C * 4) // 8 * 8), 8)

    def body(w_ref, g_ref, m_ref, v_ref, d_ref, mo_ref, vo_ref):
        d, mn, vn = _adamw_math(w_ref[...], g_ref[...], m_ref[...], v_ref[...])
        d_ref[...] = d
        mo_ref[...] = mn
        vo_ref[...] = vn

    spec = pl.BlockSpec((1, rb, C), lambda a, r: (a, r, 0))
    return pl.pallas_call(body, name=name, grid=(A, R // rb), in_specs=[spec] * 4, out_specs=[spec] * 3,
                          out_shape=[jax.ShapeDtypeStruct(w.shape, F32)] * 3,
                          compiler_params=_cparams(("parallel", "parallel")))(w, g, m, v)


def _adamw_small(parts, w, m, v):
    def body(p_ref, w_ref, m_ref, v_ref, g_ref, d_ref, mo_ref, vo_ref):
        g = p_ref[0]
        for i in range(1, N_DEV):
            g = g + p_ref[i]
        d, mn, vn = _adamw_math(w_ref[...], g, m_ref[...], v_ref[...])
        g_ref[...] = g
        d_ref[...] = d
        mo_ref[...] = mn
        vo_ref[...] = vn

    return pl.pallas_call(body, name="adamw_small", out_shape=[jax.ShapeDtypeStruct(w.shape, F32)] * 4,
                          compiler_params=_cparams())(parts, w, m, v)


def _add_blocks(name, a, b, out_dtype=F32):
    n, R, W = a.shape
    rb = _tile(R, 512, 8)

    def body(a_ref, b_ref, o_ref):
        o_ref[...] = (a_ref[...].astype(F32) + b_ref[...].astype(F32)).astype(o_ref.dtype)

    spec = pl.BlockSpec((1, rb, W), lambda i, r: (i, r, 0))
    return pl.pallas_call(body, name=name, grid=(n, R // rb), in_specs=[spec, spec], out_specs=spec,
                          out_shape=jax.ShapeDtypeStruct(a.shape, out_dtype),
                          compiler_params=_cparams(("parallel", "parallel")))(a, b)


def _pair_add(name, g, other, c):
    _, R, W = g.shape
    rb = _tile(R, 256, 8)

    def body(c_ref, a_ref, b_ref, o_ref):
        o_ref[...] = a_ref[...] + b_ref[...]

    grid_spec = pltpu.PrefetchScalarGridSpec(
        num_scalar_prefetch=1, grid=(4, R // rb),
        in_specs=[pl.BlockSpec((1, rb, W), lambda p, r, c_ref: (2 * p + c_ref[0], r, 0)),
                  pl.BlockSpec((1, rb, W), lambda p, r, c_ref: (p, r, 0))],
        out_specs=pl.BlockSpec((1, rb, W), lambda p, r, c_ref: (p, r, 0)))
    return pl.pallas_call(body, name=name, grid_spec=grid_spec, out_shape=jax.ShapeDtypeStruct((4, R, W), F32),
                          compiler_params=_cparams(("parallel", "parallel")))(c.reshape(1).astype(jnp.int32), g, other)


def _sum4(name, partial, parts, chip):
    _, R, W = partial.shape
    rb = _tile(R, 256, 8)

    def body(s_ref, o_ref, p_ref, out_ref):
        out_ref[...] = ((o_ref[0] + p_ref[0]) + p_ref[1]) + p_ref[2]

    grid_spec = pltpu.PrefetchScalarGridSpec(
        num_scalar_prefetch=1, grid=(R // rb,),
        in_specs=[pl.BlockSpec((1, rb, W), lambda r, s_ref: (s_ref[0], r, 0)),
                  pl.BlockSpec((3, rb, W), lambda r, s_ref: (0, r, 0))],
        out_specs=pl.BlockSpec((rb, W), lambda r, s_ref: (r, 0)))
    return pl.pallas_call(body, name=name, grid_spec=grid_spec, out_shape=jax.ShapeDtypeStruct((R, W), F32),
                          compiler_params=_cparams(("parallel",)))(chip.reshape(1).astype(jnp.int32), partial, parts)


MESH = pl.DeviceIdType.MESH
ANY = pl.BlockSpec(memory_space=pl.ANY)


def _place():
    return lax.axis_index("x"), lax.axis_index("y"), lax.axis_index("c")


def _other_chips(x, y):
    return [(1 - x, y), (x, 1 - y), (1 - x, 1 - y)]


def _all_gather(name, blocks):
    n = len(blocks)

    def body(*refs):
        x_refs, out_refs = refs[:n], refs[n:2 * n]
        send_sems, recv_sems, local_sems = refs[2 * n:]
        x, y, c = _place()
        me, sibling = (x, y, c), (x, y, 1 - c)
        chips = _other_chips(x, y)

        def slot(a, px, py, pc):
            return out_refs[a].at[4 * px + 2 * py + pc]

        def copy(a, k, blk, to, src=None):
            return pltpu.make_async_remote_copy(src_ref=slot(a, *blk) if src is None else src, dst_ref=slot(a, *blk),
                                                send_sem=send_sems.at[a, k], recv_sem=recv_sems.at[a, k],
                                                device_id=to, device_id_type=MESH)

        mine = [pltpu.make_async_copy(x_refs[a], slot(a, *me), local_sems.at[a]) for a in range(n)]
        for cp in mine:
            cp.start()
        first = []
        for j, chip in enumerate(chips):
            first += [copy(a, 1 + j, me, (*chip, c), src=x_refs[a]) for a in range(n)]
        first += [copy(a, 0, me, sibling, src=x_refs[a]) for a in range(n)]
        for cp in first:
            cp.start()
        passed = []
        for j, chip in enumerate(chips):
            for a in range(n):
                copy(a, 1 + j, (*chip, c), me).wait_recv()
                passed.append(copy(a, 4 + j, (*chip, c), sibling))
                passed[-1].start()
        for a in range(n):
            copy(a, 0, sibling, me).wait_recv()
        for j, chip in enumerate(chips):
            for a in range(n):
                copy(a, 4 + j, (*chip, 1 - c), me).wait_recv()
        for cp in first + passed:
            cp.wait_send()
        for cp in mine:
            cp.wait()

    return pl.pallas_call(body, name=name, in_specs=[ANY] * n, out_specs=[ANY] * n,
                          out_shape=[jax.ShapeDtypeStruct((N_DEV,) + b.shape, b.dtype) for b in blocks],
                          scratch_shapes=[pltpu.SemaphoreType.DMA((n, 7)), pltpu.SemaphoreType.DMA((n, 7)),
                                          pltpu.SemaphoreType.DMA((n,))])(*blocks)


def _swap_with_sibling(name, gs):
    n = len(gs)

    def body(*refs):
        g_refs, out_refs = refs[:n], refs[n:2 * n]
        send_sems, recv_sems = refs[2 * n:]
        x, y, c = _place()
        copies = []
        for a in range(n):
            for p in range(4):
                copies.append(pltpu.make_async_remote_copy(
                    src_ref=g_refs[a].at[2 * p + (1 - c)], dst_ref=out_refs[a].at[p], send_sem=send_sems.at[a, p],
                    recv_sem=recv_sems.at[a, p], device_id=(x, y, 1 - c), device_id_type=MESH))
        for cp in copies:
            cp.start()
        for cp in copies:
            cp.wait_recv()
        for cp in copies:
            cp.wait_send()

    return pl.pallas_call(body, name=name, in_specs=[ANY] * n, out_specs=[ANY] * n,
                          out_shape=[jax.ShapeDtypeStruct((4,) + g.shape[1:], g.dtype) for g in gs],
                          scratch_shapes=[pltpu.SemaphoreType.DMA((n, 4)), pltpu.SemaphoreType.DMA((n, 4))])(*gs)


def _exchange_chips(name, ps):
    n = len(ps)

    def body(*refs):
        p_refs, out_refs = refs[:n], refs[n:2 * n]
        send_sems, recv_sems = refs[2 * n:]
        x, y, c = _place()
        copies = []
        for a in range(n):
            for j, (px, py) in enumerate(_other_chips(x, y)):
                copies.append(pltpu.make_async_remote_copy(
                    src_ref=p_refs[a].at[2 * px + py], dst_ref=out_refs[a].at[j], send_sem=send_sems.at[a, j],
                    recv_sem=recv_sems.at[a, j], device_id=(px, py, c), device_id_type=MESH))
        for cp in copies:
            cp.start()
        for cp in copies:
            cp.wait_recv()
        for cp in copies:
            cp.wait_send()

    return pl.pallas_call(body, name=name, in_specs=[ANY] * n, out_specs=[ANY] * n,
                          out_shape=[jax.ShapeDtypeStruct((3,) + p.shape[1:], p.dtype) for p in ps],
                          scratch_shapes=[pltpu.SemaphoreType.DMA((n, 3)), pltpu.SemaphoreType.DMA((n, 3))])(*ps)


def _reduce_scatter(tag, gs):
    x, y, c = _place()
    from_sibling = _swap_with_sibling(f"rs_swap_{tag}", gs)
    partial = [_pair_add(f"rs_add_{tag}_{i}", g, o, c) for i, (g, o) in enumerate(zip(gs, from_sibling))]
    got = _exchange_chips(f"rs_chips_{tag}", partial)
    return [_sum4(f"rs_sum_{tag}_{i}", p, q, 2 * x + y) for i, (p, q) in enumerate(zip(partial, got))]


def _flat_rows(n_elems):
    return -(-n_elems // (FLAT_W * 16)) * 16


def _pack(arrays, dtype):
    flat = jnp.concatenate([a.reshape(-1).astype(dtype) for a in arrays])
    rows = _flat_rows(flat.shape[0])
    flat = jnp.pad(flat, (0, rows * FLAT_W - flat.shape[0]))
    return flat.reshape(rows, FLAT_W)


def _unpack(flat, shapes, lead=()):
    flat = flat.reshape(lead + (-1,))
    out, pos = [], 0
    for s in shapes:
        n = math.prod(s)
        out.append(flat[..., pos:pos + n].reshape(lead + tuple(s)))
        pos += n
    return out


def _shard_to_send(name, shard):
    if name == "ffn_w_up":
        shard = jnp.pad(shard, ((0, 0), (0, FFN_SHARD_PAD - FFN_SHARD)))
    return shard.astype(MXU_DTYPE)


def _whole_from_gathered(name, g):
    if name == "w_in":
        return _pad_in_proj(jnp.concatenate([g[d] for d in range(N_DEV)], axis=1))
    if name in ("w_br_gdn", "w_br_gla", "ffn_w_up"):
        return jnp.transpose(g, (1, 0, 2)).reshape(g.shape[1], N_DEV * g.shape[2])
    if name == "ffn_w_down":
        blocks = g.reshape(N_DEV // 2, FFN_SHARD, g.shape[2])
        blocks = jnp.pad(blocks, ((0, 0), (0, FFN_SHARD_PAD - FFN_SHARD), (0, 0)))
        return blocks.reshape(FFN_PAD, g.shape[2])
    return g.reshape(N_DEV * g.shape[1], g.shape[2])


def _slots_from_whole(name, gw):
    if name == "w_in":
        full = _unpad_in_proj(gw)
        cs = IN_DIM // N_DEV
        return jnp.stack([full[:, d * cs:(d + 1) * cs] for d in range(N_DEV)])
    if name in ("w_br_gdn", "w_br_gla"):
        return jnp.transpose(gw.reshape(gw.shape[0], N_DEV, gw.shape[1] // N_DEV), (1, 0, 2))
    if name == "ffn_w_up":
        return jnp.transpose(gw.reshape(gw.shape[0], N_DEV, FFN_SHARD_PAD)[:, :, :FFN_SHARD], (1, 0, 2))
    if name == "ffn_w_down":
        blocks = gw.reshape(N_DEV // 2, FFN_SHARD_PAD, gw.shape[1])[:, :FFN_SHARD]
        return blocks.reshape(N_DEV, FFN_SHARD // 2, gw.shape[1])
    return gw.reshape(N_DEV, gw.shape[0] // N_DEV, gw.shape[1])


def _ffn_pad_cols(a):
    lead = a.shape[:-1]
    nblk = a.shape[-1] // FFN_SHARD
    a = a.reshape(lead + (nblk, FFN_SHARD))
    a = jnp.pad(a, [(0, 0)] * (len(lead) + 1) + [(0, FFN_SHARD_PAD - FFN_SHARD)])
    return a.reshape(lead + (nblk * FFN_SHARD_PAD,))


def _ffn_unpad_cols(a):
    lead = a.shape[:-1]
    nblk = a.shape[-1] // FFN_SHARD_PAD
    return a.reshape(lead + (nblk, FFN_SHARD_PAD))[..., :FFN_SHARD].reshape(lead + (nblk * FFN_SHARD,))


def _pad_in_proj(w):
    starts, pos = {}, 0
    for n, width in IN_SPLITS:
        starts[n] = (pos, width)
        pos += width
    cols, at = [], 0
    for _, off, width, pieces in PAD_SEGS:
        assert off == at
        used = 0
        for ref_name, lane in pieces:
            assert lane == used
            s, wd = starts[ref_name]
            cols.append(w[:, s:s + wd])
            used += wd
        if used < width:
            cols.append(jnp.zeros((w.shape[0], width - used), w.dtype))
        at += width
    if at < IN_PAD:
        cols.append(jnp.zeros((w.shape[0], IN_PAD - at), w.dtype))
    return jnp.concatenate(cols, axis=1)


def _unpad_in_proj(wp):
    where = {}
    for _, off, _, pieces in PAD_SEGS:
        for ref_name, lane in pieces:
            where[ref_name] = off + lane
    return jnp.concatenate([wp[:, where[n]:where[n] + width] for n, width in IN_SPLITS], axis=1)


def _lane_pad(a, width=LANES):
    return jnp.pad(a, ((0, 0), (0, width - a.shape[1])))


def _seg_blk(h, name, rows):
    off, width = SEG[name]
    return (h, rows, width, off // width)


def _ln_both(xs_, ps_):
    (y,) = _ln_fn(xs_, ps_)
    return (y, y)


def _layer_fwd(l, x, x_mx, W, sp):
    T = x.shape[0]
    n64, ngla, ntok = T // SSD_CHUNK, T // GLA_BLOCK, T // 256
    h = _mm(f"in_proj_{l}", x_mx, W["w_in_pad"])
    xbc = _conv_silu_fwd(f"ssd_conv_{l}", h, SEG["xbc"][0], sp["ssd_conv_w"], sp["ssd_conv_b"])
    gqkv = _conv_silu_fwd(f"gdn_conv_{l}", h, SEG["gqkv"][0], sp["gdn_conv_w"], None)

    ssd_in = [(xbc, SSD_CHUNK, SSD_XBC, 0), _seg_blk(h, "dt", SSD_CHUNK), _seg_blk(h, "z", SSD_CHUNK)]
    ssd_p = [sp["ssd_dt_bias"], sp["ssd_a_log"], sp["ssd_d"], sp["ssd_norm_w"]]
    o_ssd, ssd_states = _chain_fwd(f"ssd_fwd_{l}", _ssd_chunk, n64, ssd_in, ssd_p, [(SSD_CHUNK, SSD_INNER, MXU_DTYPE)],
                                   (SSD_STATE, SSD_INNER))
    gdn_in = [(gqkv, GDN_CHUNK, 3 * GDN_WIDTH, 0), _seg_blk(h, "gab", GDN_CHUNK), _seg_blk(h, "gg", GDN_CHUNK)]
    gdn_p = [_lane_pad(sp["gdn_a_log"]), _lane_pad(sp["gdn_dt_bias"]), sp["gdn_norm_w"]]
    o_gdn, gdn_states = _chain_fwd(f"gdn_fwd_{l}", _gdn_chunk, n64, gdn_in, gdn_p, [(GDN_CHUNK, GDN_WIDTH, MXU_DTYPE)],
                                   (GDN_WIDTH, GDN_HEAD_DIM))
    gla_in = [_seg_blk(h, "lqkv", GLA_BLOCK), _seg_blk(h, "lglr", GLA_BLOCK), _seg_blk(h, "lr", GLA_BLOCK)]
    gla_p = [jnp.pad(sp["gla_gate_w2"], ((0, LANES - GLA_RANK), (0, 0))), sp["gla_gate_b"], sp["gla_norm_w"]]
    o_gla, gla_states = _chain_fwd(f"gla_fwd_{l}", _gla_block, ngla, gla_in, gla_p, [(GLA_BLOCK, GLA_V, MXU_DTYPE)],
                                   (GLA_VAL_DIM, GLA_K))
    y_ssd = _mm(f"br_ssd_{l}", o_ssd, W["w_br_ssd"])
    y_gdn = _mm(f"br_gdn_{l}", o_gdn, W["w_br_gdn"])
    y_gla = _mm(f"br_gla_{l}", o_gla, W["w_br_gla"])
    merge_in = [_seg_blk(h, "gates", 256), (y_ssd, 256, D_MODEL, 0), (y_gdn, 256, D_MODEL, 0), (y_gla, 256, D_MODEL, 0)]
    (mix,) = _chain_fwd(f"merge_{l}", _merge_fn, ntok, merge_in, [], [(256, D_MODEL, MXU_DTYPE)])
    r1 = _mm(f"out_proj_{l}", mix, W["w_out"])
    ln1_p = [sp["ln1_g"], sp["ln1_b"]]
    both = [(256, D_MODEL, F32), (256, D_MODEL, MXU_DTYPE)]
    x1, x1_mx = _chain_fwd(f"ln1_{l}", _ln_both, ntok, [(x, 256, D_MODEL, 0), (r1, 256, D_MODEL, 0)], ln1_p, both)
    up = _mm(f"ffn_up_{l}", x1_mx, W["ffn_w_up"])
    act = _ffn_glu_fwd(f"ffn_glu_{l}", up, sp["ffn_conv_w_pad"], sp["ffn_conv_b_pad"], MXU_DTYPE)
    r2 = _mm(f"ffn_down_{l}", act, W["ffn_w_down"])
    ln2_p = [sp["ln2_g"], sp["ln2_b"]]
    x2, x2_mx = _chain_fwd(f"ln2_{l}", _ln_both, ntok, [(x1, 256, D_MODEL, 0), (r2, 256, D_MODEL, 0)], ln2_p, both)
    saved = dict(x=x, x_mx=x_mx, h=h, xbc=xbc, gqkv=gqkv, ssd_in=ssd_in, ssd_p=ssd_p, ssd_states=ssd_states, gdn_in=gdn_in,
                 gdn_p=gdn_p, gdn_states=gdn_states, gla_in=gla_in, gla_p=gla_p, gla_states=gla_states, o_ssd=o_ssd,
                 o_gdn=o_gdn, o_gla=o_gla, merge_in=merge_in, mix=mix, r1=r1, ln1_p=ln1_p, x1=x1, x1_mx=x1_mx, up=up, act=act,
                 r2=r2, ln2_p=ln2_p)
    return x2, x2_mx, saved


def _layer_bwd(l, dx2, W, sp, sv):
    T = dx2.shape[0]
    n64, ngla, ntok = T // SSD_CHUNK, T // GLA_BLOCK, T // 256
    bf = MXU_DTYPE
    gw, gs = {}, {}
    (dx1_a, dr2), (gs["ln2_g"], gs["ln2_b"]) = _chain_bwd(
        f"ln2_bwd_{l}", _ln_fn, ntok, [(sv["x1"], 256, D_MODEL, 0), (sv["r2"], 256, D_MODEL, 0)], sv["ln2_p"],
        [(dx2, 256, D_MODEL)], dx_dtypes=[F32, bf])
    gw["ffn_w_down"] = _mm(f"ffn_down_dw_{l}", sv["act"], dr2, "tn")
    dact = _mm(f"ffn_down_dx_{l}", dr2, W["ffn_w_down"], "nt")
    dg, du, dwg, dwu, dbg, dbu = _ffn_glu_bwd(f"ffn_glu_bwd_{l}", sv["up"], sp["ffn_conv_w_pad"], sp["ffn_conv_b_pad"], dact, bf)
    gs["ffn_conv_w"] = _ffn_unpad_cols(jnp.concatenate([dwg, dwu], axis=1))
    gs["ffn_conv_b"] = _ffn_unpad_cols(jnp.concatenate([dbg, dbu], axis=1))
    dup = jnp.concatenate([dg, du], axis=1)
    gw["ffn_w_up"] = _mm(f"ffn_up_dw_{l}", sv["x1_mx"], dup, "tn")
    dx1_b = _mm(f"ffn_up_dx_{l}", dup, W["ffn_w_up"], "nt", tn=1024, tk=1024)
    (dx_a, dr1), (gs["ln1_g"], gs["ln1_b"]) = _chain_bwd(
        f"ln1_bwd_{l}", _ln_sum_fn, ntok, [(sv["x"], 256, D_MODEL, 0), (sv["r1"], 256, D_MODEL, 0)], sv["ln1_p"],
        [(dx1_a, 256, D_MODEL), (dx1_b, 256, D_MODEL)], dx_dtypes=[F32, bf])
    gw["w_out"] = _mm(f"out_proj_dw_{l}", sv["mix"], dr1, "tn")
    dmix = _mm(f"out_proj_dx_{l}", dr1, W["w_out"], "nt")
    (dgates, dy_ssd, dy_gdn, dy_gla), _ = _chain_bwd(f"merge_bwd_{l}", _merge_fn, ntok, sv["merge_in"], [],
                                                     [(dmix, 256, D_MODEL)], dx_dtypes=[bf, bf, bf, bf])
    gw["w_br_ssd"] = _mm(f"br_ssd_dw_{l}", sv["o_ssd"], dy_ssd, "tn")
    gw["w_br_gdn"] = _mm(f"br_gdn_dw_{l}", sv["o_gdn"], dy_gdn, "tn")
    gw["w_br_gla"] = _mm(f"br_gla_dw_{l}", sv["o_gla"], dy_gla, "tn")
    do_ssd = _mm(f"br_ssd_dx_{l}", dy_ssd, W["w_br_ssd"], "nt")
    do_gdn = _mm(f"br_gdn_dx_{l}", dy_gdn, W["w_br_gdn"], "nt")
    do_gla = _mm(f"br_gla_dx_{l}", dy_gla, W["w_br_gla"], "nt")

    (dxbc, ddt, dz), dps = _chain_bwd(f"ssd_bwd_{l}", _ssd_chunk, n64, sv["ssd_in"], sv["ssd_p"],
                                      [(do_ssd, SSD_CHUNK, SSD_INNER)], sprev=sv["ssd_states"], dx_dtypes=[F32, bf, bf])
    gs["ssd_dt_bias"], gs["ssd_a_log"], gs["ssd_d"], gs["ssd_norm_w"] = dps
    (dgqkv, dgab, dgg), dps = _chain_bwd(f"gdn_bwd_{l}", _gdn_chunk, n64, sv["gdn_in"], sv["gdn_p"],
                                         [(do_gdn, GDN_CHUNK, GDN_WIDTH)], sprev=sv["gdn_states"], dx_dtypes=[F32, bf, bf])
    gs["gdn_a_log"], gs["gdn_dt_bias"], gs["gdn_norm_w"] = dps[0][:, :GDN_HEADS], dps[1][:, :GDN_HEADS], dps[2]
    (dlqkv, dlglr, dlr), dps = _chain_bwd(f"gla_bwd_{l}", _gla_block, ngla, sv["gla_in"], sv["gla_p"],
                                          [(do_gla, GLA_BLOCK, GLA_V)], sprev=sv["gla_states"], dx_dtypes=[bf, bf, bf])
    gs["gla_gate_w2"], gs["gla_gate_b"], gs["gla_norm_w"] = dps[0][:GLA_RANK], dps[1], dps[2]
    dxbc_pre, gs["ssd_conv_w"], gs["ssd_conv_b"] = _conv_silu_bwd(
        f"ssd_conv_bwd_{l}", sv["h"], SEG["xbc"][0], sp["ssd_conv_w"], sp["ssd_conv_b"], dxbc, bf)
    dgqkv_pre, gs["gdn_conv_w"] = _conv_silu_bwd(f"gdn_conv_bwd_{l}", sv["h"], SEG["gqkv"][0], sp["gdn_conv_w"], None, dgqkv, bf)
    pieces = dict(gates=dgates, xbc=dxbc_pre, gqkv=dgqkv_pre, z=dz, lqkv=dlqkv, gg=dgg, lr=dlr, dt=ddt, gab=dgab, lglr=dlglr)
    cols = [pieces[name] for name, _, _, _ in PAD_SEGS]
    cols.append(jnp.zeros((T, IN_PAD - PAD_SEGS[-1][1] - PAD_SEGS[-1][2]), bf))
    dh = jnp.concatenate(cols, axis=1)
    gw["w_in_pad"] = _mm(f"in_proj_dw_{l}", sv["x_mx"], dh, "tn")
    dx_b = _mm(f"in_proj_dx_{l}", dh, W["w_in_pad"], "nt", tn=1024, tk=1024)
    dx = _add_blocks(f"dx_add_{l}", dx_a[None], dx_b[None])[0]
    return dx, gw, gs


def _ln_sum_fn(xs_, ps_):
    (y,) = _ln_fn(xs_, ps_)
    return (y, y)


def _small_2d(name, a):
    return a.reshape(1, -1) if a.ndim == 1 else a


def kernel(x, w_in, ssd_conv_w, ssd_conv_b, ssd_dt_bias, ssd_a_log, ssd_d, ssd_norm_w, gdn_conv_w, gdn_a_log, gdn_dt_bias, gdn_norm_w, gla_gate_w2, gla_gate_b, gla_norm_w, w_br_ssd, w_br_gdn, w_br_gla, w_out, ln1_g, ln1_b, ffn_w_up, ffn_conv_w, ffn_conv_b, ffn_w_down, ln2_g, ln2_b, loss_target, m_w_in, m_ssd_conv_w, m_ssd_conv_b, m_ssd_dt_bias, m_ssd_a_log, m_ssd_d, m_ssd_norm_w, m_gdn_conv_w, m_gdn_a_log, m_gdn_dt_bias, m_gdn_norm_w, m_gla_gate_w2, m_gla_gate_b, m_gla_norm_w, m_w_br_ssd, m_w_br_gdn, m_w_br_gla, m_w_out, m_ln1_g, m_ln1_b, m_ffn_w_up, m_ffn_conv_w, m_ffn_conv_b, m_ffn_w_down, m_ln2_g, m_ln2_b, v_w_in, v_ssd_conv_w, v_ssd_conv_b, v_ssd_dt_bias, v_ssd_a_log, v_ssd_d, v_ssd_norm_w, v_gdn_conv_w, v_gdn_a_log, v_gdn_dt_bias, v_gdn_norm_w, v_gla_gate_w2, v_gla_gate_b, v_gla_norm_w, v_w_br_ssd, v_w_br_gdn, v_w_br_gla, v_w_out, v_ln1_g, v_ln1_b, v_ffn_w_up, v_ffn_conv_w, v_ffn_conv_b, v_ffn_w_down, v_ln2_g, v_ln2_b):
    args = locals()
    w = {n: args[n] for n in WEIGHTS}
    m = {n: args["m_" + n] for n in WEIGHTS}
    v = {n: args["v_" + n] for n in WEIGHTS}
    dev = 4 * lax.axis_index("x") + 2 * lax.axis_index("y") + lax.axis_index("c")
    xl = x[0]
    tgt = loss_target[0]

    sm_flat = _pack([w[n] for n in SMALL_SHARDED], F32)
    W = []
    for l in range(DEPTH):
        send = [_shard_to_send(n, w[n][l]) for n in BIG] + ([sm_flat] if l == 0 else [])
        got = _all_gather(f"gather_w_{l}", send)
        if l == 0:
            sm_gathered = got[-1]
        Wl = {n: _whole_from_gathered(n, g) for n, g in zip(BIG, got)}
        Wl["w_in_pad"] = Wl.pop("w_in")
        W.append(Wl)
    sm_shards = _unpack(sm_gathered, [w[n].shape for n in SMALL_SHARDED], lead=(N_DEV,))
    whole = dict(w)
    for n, s in zip(SMALL_SHARDED, sm_shards):
        whole[n] = jnp.transpose(s, (1, 2, 0, 3)).reshape(s.shape[1], s.shape[2], N_DEV * s.shape[3])
    SP = [{n: _small_2d(n, whole[n][l]) for n in SMALL} for l in range(DEPTH)]
    for sp in SP:
        sp["ffn_conv_w_pad"] = _ffn_pad_cols(sp["ffn_conv_w"])
        sp["ffn_conv_b_pad"] = _ffn_pad_cols(sp["ffn_conv_b"])

    saved = []
    act, act_mx = xl, xl.astype(MXU_DTYPE)
    for l in range(DEPTH):
        act, act_mx, sv = _layer_fwd(l, act, act_mx, W[l], SP[l])
        saved.append(sv)
    dy, loss_parts = _loss_head(act, tgt)
    loss = lax.psum(jnp.sum(loss_parts), ("x", "y", "c"))

    grads = {}
    GS, red = [None] * DEPTH, [None] * DEPTH
    for l in reversed(range(DEPTH)):
        dy, gw, GS[l] = _layer_bwd(l, dy, W[l], SP[l], saved[l])
        gw["w_in"] = gw.pop("w_in_pad")
        red[l] = _reduce_scatter(str(l), [_slots_from_whole(n, gw[n]) for n in BIG])
    grad_x = dy[None]
    for i, n in enumerate(BIG):
        grads[n] = jnp.stack([red[l][i] for l in range(DEPTH)])

    small_shapes = [whole[n].shape for n in SMALL]
    gs_flat = _pack([jnp.stack([GS[l][n].reshape(whole[n].shape[1:]) for l in range(DEPTH)]) for n in SMALL], F32)
    (gs_all,) = _all_gather("gather_small_grads", [gs_flat])

    def mine(n, a):
        if n in SMALL_SHARDED:
            cs = a.shape[-1] // N_DEV
            return lax.dynamic_slice_in_dim(a, dev * cs, cs, axis=a.ndim - 1)
        return a

    m_whole, v_whole = {}, {}
    for n in SMALL:
        if n in SMALL_SHARDED:
            cs = w[n].shape[-1]
            zeros = jnp.zeros(whole[n].shape, F32)
            m_whole[n] = lax.dynamic_update_slice_in_dim(zeros, m[n], dev * cs, axis=2)
            v_whole[n] = lax.dynamic_update_slice_in_dim(zeros, v[n], dev * cs, axis=2)
        else:
            m_whole[n], v_whole[n] = m[n], v[n]
    outs = _adamw_small(gs_all, _pack([whole[n] for n in SMALL], F32), _pack([m_whole[n] for n in SMALL], F32),
                        _pack([v_whole[n] for n in SMALL], F32))
    g_s, d_s, m_s, v_s = [_unpack(o, small_shapes) for o in outs]
    delta, new_m, new_v = {}, {}, {}
    for i, n in enumerate(SMALL):
        grads[n], delta[n], new_m[n], new_v[n] = mine(n, g_s[i]), mine(n, d_s[i]), mine(n, m_s[i]), mine(n, v_s[i])
    for n in BIG:
        delta[n], new_m[n], new_v[n] = _adamw(f"adamw_{n}", w[n], grads[n], m[n], v[n])

    return (loss, grad_x, *[grads[n] for n in WEIGHTS], *[delta[n] for n in WEIGHTS], *[new_m[n] for n in WEIGHTS],
            *[new_v[n] for n in WEIGHTS])
```

```python
import functools
import math

import jax
import jax.numpy as jnp
from jax import lax
from jax.experimental import pallas as pl
from jax.experimental.pallas import tpu as pltpu

F32 = jnp.float32
MXU_DTYPE = jnp.bfloat16
HI = lax.Precision.HIGHEST

N_DEV = 8
D_MODEL = 1024
DEPTH = 2
SSD_HEADS, SSD_HEAD_DIM, SSD_INNER, SSD_GROUPS, SSD_STATE, SSD_CHUNK = 16, 64, 1024, 2, 128, 64
SSD_XBC = SSD_INNER + 2 * SSD_GROUPS * SSD_STATE
GDN_HEADS, GDN_HEAD_DIM, GDN_WIDTH, GDN_CHUNK = 4, 128, 512, 64
GLA_HEADS, GLA_KEY_DIM, GLA_VAL_DIM, GLA_K, GLA_V, GLA_RANK, GLA_CHUNK = 4, 64, 128, 256, 512, 16, 16
GLA_BLOCK = 128
GLA_NORMALIZER = 16.0
FFN_DIM = 2816
FFN_SHARD = 2 * FFN_DIM // 8
FFN_SHARD_PAD = 768
FFN_UP_PAD = 8 * FFN_SHARD_PAD
FFN_PAD = FFN_UP_PAD // 2
ALPHA = (2 * DEPTH) ** 0.25
LN_EPS = 1e-5
RMS_EPS = 1e-6
ADAM_LR, ADAM_B1, ADAM_B2, ADAM_EPS, ADAM_WD, ADAM_STEP = 0.001, 0.9, 0.999, 1e-08, 0.01, 10
LANES = 128
NEG_BIG = -1e30
VMEM_LIMIT = 56 * 1024 * 1024

IN_SPLITS = (("z", 1024), ("xbc", 1536), ("dt", 16), ("gqkv", 1536), ("ga", 4), ("gb", 4), ("gg", 512),
             ("lqkv", 1024), ("lglr", 16), ("lr", 512), ("gates", 3072))
IN_DIM = sum(w for _, w in IN_SPLITS)
PAD_SEGS = (("gates", 0, 3072, (("gates", 0),)), ("xbc", 3072, 1536, (("xbc", 0),)),
            ("gqkv", 4608, 1536, (("gqkv", 0),)), ("z", 6144, 1024, (("z", 0),)),
            ("lqkv", 7168, 1024, (("lqkv", 0),)), ("gg", 8192, 512, (("gg", 0),)), ("lr", 8704, 512, (("lr", 0),)),
            ("dt", 9216, 128, (("dt", 0),)), ("gab", 9344, 128, (("ga", 0), ("gb", 4))), ("lglr", 9472, 128, (("lglr", 0),)))
IN_PAD = 9728
SEG = {name: (off, width) for name, off, width, _ in PAD_SEGS}

BIG = ("w_in", "w_br_ssd", "w_br_gdn", "w_br_gla", "w_out", "ffn_w_up", "ffn_w_down")
COL_SHARDED = ("w_in", "w_br_gdn", "w_br_gla", "ffn_w_up")
SMALL_SHARDED = ("ssd_conv_w", "gdn_conv_w", "gla_gate_w2", "ffn_conv_w")
WEIGHTS = ("w_in", "ssd_conv_w", "ssd_conv_b", "ssd_dt_bias", "ssd_a_log", "ssd_d", "ssd_norm_w", "gdn_conv_w",
           "gdn_a_log", "gdn_dt_bias", "gdn_norm_w", "gla_gate_w2", "gla_gate_b", "gla_norm_w", "w_br_ssd", "w_br_gdn",
           "w_br_gla", "w_out", "ln1_g", "ln1_b", "ffn_w_up", "ffn_conv_w", "ffn_conv_b", "ffn_w_down", "ln2_g", "ln2_b")
SMALL = tuple(n for n in WEIGHTS if n not in BIG)
FLAT_W = 512


def _cparams(sem=None):
    kw = dict(vmem_limit_bytes=VMEM_LIMIT)
    if sem is not None:
        kw["dimension_semantics"] = sem
    return pltpu.CompilerParams(**kw)


_DIMS = {"nn": (((1,), (0,)), ((), ())), "nt": (((1,), (1,)), ((), ())), "tn": (((0,), (0,)), ((), ()))}


def _dot(a, b, dims="nn"):
    if MXU_DTYPE == F32:
        return lax.dot_general(a.astype(F32), b.astype(F32), _DIMS[dims], precision=HI, preferred_element_type=F32)
    return lax.dot_general(a.astype(MXU_DTYPE), b.astype(MXU_DTYPE), _DIMS[dims], preferred_element_type=F32)


def _dot_hi(a, b, dims="nn"):
    return lax.dot_general(a.astype(F32), b.astype(F32), _DIMS[dims], precision=HI, preferred_element_type=F32)


def _iota2(shape, axis):
    return lax.broadcasted_iota(jnp.int32, shape, axis)


def _tril(n, strict=False):
    r, c = _iota2((n, n), 0), _iota2((n, n), 1)
    return (r > c) if strict else (r >= c)


@jax.custom_vjp
def _unit_lower_inverse(a):
    n = a.shape[0]
    eye = (_iota2((n, n), 0) == _iota2((n, n), 1)).astype(F32)
    x = eye - a
    p = a
    k = 2
    while k < n:
        p = _dot_hi(p, p)
        x = x + _dot_hi(x, p)
        k *= 2
    return x


def _unit_lower_inverse_fwd(a):
    t = _unit_lower_inverse(a)
    return t, t


def _unit_lower_inverse_bwd(t, dt):
    return (-_dot_hi(_dot_hi(t, dt, "tn"), t, "nt"),)


_unit_lower_inverse.defvjp(_unit_lower_inverse_fwd, _unit_lower_inverse_bwd)


def _ssd_chunk(xs_, ps_, s_t):
    xbc, dtraw, z = xs_
    dt_bias, a_log, d_skip, norm_w = ps_
    L = xbc.shape[0]
    H, P, N, G = SSD_HEADS, SSD_HEAD_DIM, SSD_STATE, SSD_GROUPS
    W = SSD_INNER // G
    xs = xbc[:, :SSD_INNER]
    bm = xbc[:, SSD_INNER:SSD_INNER + G * N]
    cm = xbc[:, SSD_INNER + G * N:]
    dt = jax.nn.softplus(dtraw[:, :H] + dt_bias)
    a = dt * (-jnp.exp(a_log))
    causal = _tril(L)
    a_cs = _dot_hi(causal.astype(F32), a)
    expand = (_iota2((H, SSD_INNER), 1) // P == _iota2((H, SSD_INNER), 0)).astype(F32)
    wide = _dot_hi(jnp.concatenate([a_cs, dt, jnp.broadcast_to(d_skip, (L, H))], axis=0), expand)
    a_cs_x, dt_x, d_x = wide[:L], wide[L:2 * L], wide[2 * L:]
    a_end_x = a_cs_x[L - 1:L, :]
    a_cs_t, dt_t = a_cs.T, dt.T
    ys = []
    for g in range(G):
        cb = _dot(cm[:, g * N:(g + 1) * N], bm[:, g * N:(g + 1) * N], "nt")
        for r in range(H // G):
            h = g * (H // G) + r
            seg = a_cs[:, h:h + 1] - a_cs_t[h:h + 1, :]
            w = cb * jnp.exp(jnp.where(causal, seg, NEG_BIG)) * dt_t[h:h + 1, :]
            ys.append(_dot(w, xs[:, h * P:(h + 1) * P]))
    y = jnp.concatenate(ys, axis=1)
    y_in = jnp.concatenate([_dot(cm[:, g * N:(g + 1) * N], s_t[:, g * W:(g + 1) * W]) for g in range(G)], axis=1)
    y = y + y_in * jnp.exp(a_cs_x) + d_x * xs
    xw = xs * (jnp.exp(a_end_x - a_cs_x) * dt_x)
    st = jnp.concatenate([_dot(bm[:, g * N:(g + 1) * N], xw[:, g * W:(g + 1) * W], "tn") for g in range(G)], axis=1)
    s_new = s_t * jnp.exp(a_end_x) + st
    yg = y * jax.nn.silu(z)
    outs = []
    for g in range(G):
        part = yg[:, g * W:(g + 1) * W]
        outs.append(part * lax.rsqrt(jnp.mean(part * part, axis=1, keepdims=True) + RMS_EPS))
    return (jnp.concatenate(outs, axis=1) * norm_w,), s_new


def _gdn_chunk(xs_, ps_, s):
    qkv, ab, gate = xs_
    a_log, dt_bias, norm_w = ps_
    L = qkv.shape[0]
    H, D = GDN_HEADS, GDN_HEAD_DIM
    g_all = -jnp.exp(a_log) * jax.nn.softplus(ab + dt_bias)
    incl, strict = _tril(L), _tril(L, strict=True)
    g_cs = _dot_hi(incl.astype(F32), g_all)
    g_cs_t = g_cs.T
    beta_all = jax.nn.sigmoid(ab)
    outs, s_out = [], []
    for h in range(H):
        q = qkv[:, h * D:(h + 1) * D]
        k = qkv[:, GDN_WIDTH + h * D:GDN_WIDTH + (h + 1) * D]
        v = qkv[:, 2 * GDN_WIDTH + h * D:2 * GDN_WIDTH + (h + 1) * D]
        q = q * lax.rsqrt(jnp.sum(q * q, axis=1, keepdims=True) + RMS_EPS) * (D ** -0.5)
        k = k * lax.rsqrt(jnp.sum(k * k, axis=1, keepdims=True) + RMS_EPS)
        beta = beta_all[:, H + h:H + h + 1]
        col = g_cs[:, h:h + 1]
        gamma = jnp.exp(jnp.where(incl, col - g_cs_t[h:h + 1, :], NEG_BIG))
        kb = k * beta
        a_mat = jnp.where(strict, _dot(kb, k, "nt") * gamma, 0.0)
        t_mat = _unit_lower_inverse(a_mat)
        u = _dot(t_mat, v * beta)
        w = _dot(t_mat, kb * jnp.exp(col))
        attn = jnp.where(incl, _dot(q, k, "nt") * gamma, 0.0)
        g_end = col[L - 1:L, :]
        s_h = s[h * D:(h + 1) * D, :]
        v_new = u - _dot(w, s_h)
        o = _dot(q * jnp.exp(col), s_h) + _dot(attn, v_new)
        s_out.append(s_h * jnp.exp(g_end) + _dot(k * jnp.exp(g_end - col), v_new, "tn"))
        o = o * lax.rsqrt(jnp.mean(o * o, axis=1, keepdims=True) + RMS_EPS) * norm_w
        outs.append(o * jax.nn.silu(gate[:, h * D:(h + 1) * D]))
    return (jnp.concatenate(outs, axis=1),), jnp.concatenate(s_out, axis=0)


def _gla_block(xs_, ps_, s_t):
    qkv, glr, r = xs_
    w2, gate_b, norm_w = ps_
    B = qkv.shape[0]
    H, K, V, C = GLA_HEADS, GLA_KEY_DIM, GLA_VAL_DIM, GLA_CHUNK
    q = qkv[:, :GLA_K] * (K ** -0.5)
    k = qkv[:, GLA_K:2 * GLA_K]
    v = qkv[:, 2 * GLA_K:]
    gk = jax.nn.log_sigmoid(_dot(glr, w2) + gate_b) / GLA_NORMALIZER
    row, col = _iota2((B, B), 0), _iota2((B, B), 1)
    same = (row // C) == (col // C)
    mask = same & (row >= col)
    b_cs = _dot_hi(mask.astype(F32), gk)
    b_end = _dot_hi((col == (row // C) * C + (C - 1)).astype(F32), b_cs)
    q_e = q * jnp.exp(b_cs)
    k_e = k * jnp.exp(-b_cs)
    k_d = k * jnp.exp(b_end - b_cs)
    intra = []
    for h in range(H):
        a_mat = jnp.where(mask, _dot(q_e[:, h * K:(h + 1) * K], k_e[:, h * K:(h + 1) * K], "nt"), 0.0)
        intra.append(_dot(a_mat, v[:, h * V:(h + 1) * V]))
    o = jnp.concatenate(intra, axis=1)
    inter = []
    for j in range(B // C):
        sl = slice(j * C, (j + 1) * C)
        inter.append(jnp.concatenate(
            [_dot(q_e[sl, h * K:(h + 1) * K], s_t[:, h * K:(h + 1) * K], "nt") for h in range(H)], axis=1))
        st = jnp.concatenate([_dot(v[sl, h * V:(h + 1) * V], k_d[sl, h * K:(h + 1) * K], "tn") for h in range(H)], axis=1)
        s_t = s_t * jnp.exp(b_end[j * C:j * C + 1, :]) + st
    o = o + jnp.concatenate(inter, axis=0)
    outs = []
    for h in range(H):
        oh = o[:, h * V:(h + 1) * V]
        oh = oh * lax.rsqrt(jnp.mean(oh * oh, axis=1, keepdims=True) + RMS_EPS) * norm_w
        outs.append(oh * jax.nn.silu(r[:, h * V:(h + 1) * V]))
    return (jnp.concatenate(outs, axis=1),), s_t


def _merge_fn(xs_, ps_):
    gates, y_ssd, y_gdn, y_gla = xs_
    d = D_MODEL
    return (jax.nn.sigmoid(gates[:, :d]) * y_ssd + jax.nn.sigmoid(gates[:, d:2 * d]) * y_gdn
            + jax.nn.sigmoid(gates[:, 2 * d:]) * y_gla,)


def _ln_fn(xs_, ps_):
    x, r = xs_
    g, b = ps_
    t = ALPHA * x + r
    mu = jnp.mean(t, axis=1, keepdims=True)
    var = jnp.mean(jnp.square(t - mu), axis=1, keepdims=True)
    return ((t - mu) * lax.rsqrt(var + LN_EPS) * g + b,)


def _row_spec(rows, width, colblk, n, reverse):
    if reverse:
        return pl.BlockSpec((rows, width), lambda c: (n - 1 - c, colblk))
    return pl.BlockSpec((rows, width), lambda c: (c, colblk))


def _full_spec(shape):
    zeros = (0,) * len(shape)
    return pl.BlockSpec(shape, lambda c: zeros)


def _chain_fwd(name, fn, n, blocked, full, out_defs, state_shape=None):
    nb, nf, no = len(blocked), len(full), len(out_defs)

    def body(*refs):
        xs = [r[...].astype(F32) for r in refs[:nb]]
        ps = [r[...] for r in refs[nb:nb + nf]]
        o_refs = refs[nb + nf:nb + nf + no]
        if state_shape is None:
            outs = fn(xs, ps)
        else:
            sprev_ref, s_ref = refs[nb + nf + no:]

            @pl.when(pl.program_id(0) == 0)
            def _():
                s_ref[...] = jnp.zeros_like(s_ref)

            s = s_ref[...]
            sprev_ref[0] = s
            outs, s_new = fn(xs, ps, s)
            s_ref[...] = s_new
        for r, o in zip(o_refs, outs):
            r[...] = o.astype(r.dtype)

    in_specs = [_row_spec(rows, width, cb, n, False) for _, rows, width, cb in blocked]
    in_specs += [_full_spec(a.shape) for a in full]
    out_specs = [_row_spec(rows, width, 0, n, False) for rows, width, _ in out_defs]
    out_shape = [jax.ShapeDtypeStruct((n * rows, width), dt) for rows, width, dt in out_defs]
    scratch = []
    if state_shape is not None:
        out_specs.append(pl.BlockSpec((1,) + state_shape, lambda c: (c, 0, 0)))
        out_shape.append(jax.ShapeDtypeStruct((n,) + state_shape, F32))
        scratch.append(pltpu.VMEM(state_shape, F32))
    return pl.pallas_call(body, name=name, grid=(n,), in_specs=in_specs, out_specs=out_specs, out_shape=out_shape,
                          scratch_shapes=scratch, compiler_params=_cparams(("arbitrary",)))(
        *[a for a, _, _, _ in blocked], *full)


def _chain_bwd(name, fn, n, blocked, full, douts, sprev=None, dx_dtypes=None):
    nb, nf, nd = len(blocked), len(full), len(douts)
    has_state = sprev is not None
    dx_dtypes = dx_dtypes or [F32] * nb

    def body(*refs):
        pos = 0
        b_refs = refs[pos:pos + nb]; pos += nb
        f_refs = refs[pos:pos + nf]; pos += nf
        d_refs = refs[pos:pos + nd]; pos += nd
        if has_state:
            sprev_ref = refs[pos]; pos += 1
        dx_refs = refs[pos:pos + nb]; pos += nb
        dp_refs = refs[pos:pos + nf]; pos += nf
        if has_state:
            ds_ref = refs[pos]

        @pl.when(pl.program_id(0) == 0)
        def _():
            for r in dp_refs:
                r[...] = jnp.zeros_like(r)
            if has_state:
                ds_ref[...] = jnp.zeros_like(ds_ref)

        xs = [r[...].astype(F32) for r in b_refs]
        ps = [r[...] for r in f_refs]
        dys = tuple(r[...].astype(F32) for r in d_refs)
        if has_state:
            _, vjp = jax.vjp(fn, xs, ps, sprev_ref[0])
            dxs, dps, ds = vjp((dys, ds_ref[...]))
            ds_ref[...] = ds
        else:
            _, vjp = jax.vjp(fn, xs, ps)
            dxs, dps = vjp(dys)
        for r, d in zip(dx_refs, dxs):
            r[...] = d.astype(r.dtype)
        for r, d in zip(dp_refs, dps):
            r[...] += d

    in_specs = [_row_spec(rows, width, cb, n, True) for _, rows, width, cb in blocked]
    in_specs += [_full_spec(a.shape) for a in full]
    in_specs += [_row_spec(rows, width, 0, n, True) for _, rows, width in douts]
    args = [a for a, _, _, _ in blocked] + list(full) + [a for a, _, _ in douts]
    scratch = []
    if has_state:
        st_shape = sprev.shape[1:]
        in_specs.append(pl.BlockSpec((1,) + st_shape, lambda c: (n - 1 - c, 0, 0)))
        args.append(sprev)
        scratch.append(pltpu.VMEM(st_shape, F32))
    out_specs = [_row_spec(rows, width, 0, n, True) for _, rows, width, _ in blocked]
    out_specs += [_full_spec(a.shape) for a in full]
    out_shape = [jax.ShapeDtypeStruct((n * rows, width), dt) for (_, rows, width, _), dt in zip(blocked, dx_dtypes)]
    out_shape += [jax.ShapeDtypeStruct(a.shape, F32) for a in full]
    res = pl.pallas_call(body, name=name, grid=(n,), in_specs=in_specs, out_specs=out_specs, out_shape=out_shape,
                         scratch_shapes=scratch, compiler_params=_cparams(("arbitrary",)))(*args)
    return res[:nb], res[nb:]


def _tile(n, target, unit):
    if n <= target:
        return n
    best = None
    for t in range(unit, target + 1, unit):
        if n % t == 0:
            best = t
    assert best is not None, (n, target, unit)
    return best


def _mm(name, a, b, dims="nn", out_dtype=F32, tm=2048, tn=512, tk=2048):
    if dims == "nn":
        (M, K), (_, N) = a.shape, b.shape
    elif dims == "nt":
        (M, K), (N, _) = a.shape, b.shape
    else:
        (K, M), (_, N) = a.shape, b.shape
    tm, tn, tk = _tile(M, tm, LANES), _tile(N, tn, LANES), _tile(K, tk, LANES)
    nk = K // tk

    def body(a_ref, b_ref, o_ref, acc_ref):
        part = _dot(a_ref[...], b_ref[...], dims)
        if nk == 1:
            o_ref[...] = part.astype(o_ref.dtype)
            return

        @pl.when(pl.program_id(2) == 0)
        def _():
            acc_ref[...] = part

        @pl.when(pl.program_id(2) > 0)
        def _():
            acc_ref[...] += part

        @pl.when(pl.program_id(2) == nk - 1)
        def _():
            o_ref[...] = acc_ref[...].astype(o_ref.dtype)

    if dims == "tn":
        a_spec = pl.BlockSpec((tk, tm), lambda j, i, k: (k, i))
    else:
        a_spec = pl.BlockSpec((tm, tk), lambda j, i, k: (i, k))
    if dims == "nt":
        b_spec = pl.BlockSpec((tn, tk), lambda j, i, k: (j, k))
    else:
        b_spec = pl.BlockSpec((tk, tn), lambda j, i, k: (k, j))
    return pl.pallas_call(
        body, name=name, grid=(N // tn, M // tm, nk), in_specs=[a_spec, b_spec],
        out_specs=pl.BlockSpec((tm, tn), lambda j, i, k: (i, j)), out_shape=jax.ShapeDtypeStruct((M, N), out_dtype),
        scratch_shapes=[pltpu.VMEM((tm, tn) if nk > 1 else (8, LANES), F32)],
        compiler_params=_cparams(("parallel", "parallel", "arbitrary")))(a, b)


CONV_CB = 256


def _shift_down(x, k):
    if k == 0:
        return x
    return jnp.where(_iota2(x.shape, 0) >= k, pltpu.roll(x, k, 0), 0.0)


def _shift_up(x, k):
    if k == 0:
        return x
    t = x.shape[0]
    return jnp.where(_iota2(x.shape, 0) < t - k, pltpu.roll(x, t - k, 0), 0.0)


def _conv_pre(x, w, b):
    kk = w.shape[0]
    pre = x * w[kk - 1:kk, :]
    for k in range(kk - 1):
        pre = pre + _shift_down(x, kk - 1 - k) * w[k:k + 1, :]
    return pre if b is None else pre + b


def _conv_bwd_pre(x, w, dpre, dw_ref, db_ref):
    kk = w.shape[0]
    dx = dpre * w[kk - 1:kk, :]
    dw_ref[kk - 1:kk, :] = jnp.sum(dpre * x, axis=0, keepdims=True)
    for k in range(kk - 1):
        dx = dx + _shift_up(dpre, kk - 1 - k) * w[k:k + 1, :]
        dw_ref[k:k + 1, :] = jnp.sum(dpre * _shift_down(x, kk - 1 - k), axis=0, keepdims=True)
    if db_ref is not None:
        db_ref[...] = jnp.sum(dpre, axis=0, keepdims=True)
    return dx


def _dsilu(pre):
    sg = jax.nn.sigmoid(pre)
    return sg * (1.0 + pre * (1.0 - sg))


def _conv_silu_fwd(name, src, col0, w, b):
    T = src.shape[0]
    kk, C = w.shape
    cb = CONV_CB
    off = col0 // cb

    def body(*refs):
        x_ref, w_ref = refs[:2]
        b_val = refs[2][...] if b is not None else None
        refs[-1][...] = jax.nn.silu(_conv_pre(x_ref[...], w_ref[...], b_val))

    in_specs = [pl.BlockSpec((T, cb), lambda j: (0, off + j)), pl.BlockSpec((kk, cb), lambda j: (0, j))]
    args = [src, w]
    if b is not None:
        in_specs.append(pl.BlockSpec((1, cb), lambda j: (0, j)))
        args.append(b)
    return pl.pallas_call(body, name=name, grid=(C // cb,), in_specs=in_specs,
                          out_specs=pl.BlockSpec((T, cb), lambda j: (0, j)), out_shape=jax.ShapeDtypeStruct((T, C), F32),
                          compiler_params=_cparams(("parallel",)))(*args)


def _conv_silu_bwd(name, src, col0, w, b, dy, dx_dtype):
    T = src.shape[0]
    kk, C = w.shape
    cb = CONV_CB
    off = col0 // cb
    has_b = b is not None

    def body(*refs):
        x_ref, w_ref = refs[:2]
        pos = 2
        b_val = None
        if has_b:
            b_val = refs[pos][...]; pos += 1
        dy_ref = refs[pos]; pos += 1
        dx_ref, dw_ref = refs[pos], refs[pos + 1]
        db_ref = refs[pos + 2] if has_b else None
        x, wv = x_ref[...], w_ref[...]
        dpre = dy_ref[...] * _dsilu(_conv_pre(x, wv, b_val))
        dx_ref[...] = _conv_bwd_pre(x, wv, dpre, dw_ref, db_ref).astype(dx_ref.dtype)

    in_specs = [pl.BlockSpec((T, cb), lambda j: (0, off + j)), pl.BlockSpec((kk, cb), lambda j: (0, j))]
    args = [src, w]
    if has_b:
        in_specs.append(pl.BlockSpec((1, cb), lambda j: (0, j)))
        args.append(b)
    in_specs.append(pl.BlockSpec((T, cb), lambda j: (0, j)))
    args.append(dy)
    out_specs = [pl.BlockSpec((T, cb), lambda j: (0, j)), pl.BlockSpec((kk, cb), lambda j: (0, j))]
    out_shape = [jax.ShapeDtypeStruct((T, C), dx_dtype), jax.ShapeDtypeStruct((kk, C), F32)]
    if has_b:
        out_specs.append(pl.BlockSpec((1, cb), lambda j: (0, j)))
        out_shape.append(jax.ShapeDtypeStruct((1, C), F32))
    return pl.pallas_call(body, name=name, grid=(C // cb,), in_specs=in_specs, out_specs=out_specs, out_shape=out_shape,
                          compiler_params=_cparams(("parallel",)))(*args)


def _ffn_glu_fwd(name, up, w, b, out_dtype=F32):
    T = up.shape[0]
    kk = w.shape[0]
    cb = CONV_CB
    width = up.shape[1] // 2
    nblk = width // cb

    def body(g_ref, u_ref, wg_ref, wu_ref, bg_ref, bu_ref, o_ref):
        g = _conv_pre(g_ref[...], wg_ref[...], bg_ref[...])
        u = _conv_pre(u_ref[...], wu_ref[...], bu_ref[...])
        o_ref[...] = (jax.nn.silu(g) * u).astype(o_ref.dtype)

    lo, hi = (lambda j: (0, j)), (lambda j: (0, nblk + j))
    in_specs = [pl.BlockSpec((T, cb), lo), pl.BlockSpec((T, cb), hi), pl.BlockSpec((kk, cb), lo), pl.BlockSpec((kk, cb), hi),
                pl.BlockSpec((1, cb), lo), pl.BlockSpec((1, cb), hi)]
    return pl.pallas_call(body, name=name, grid=(nblk,), in_specs=in_specs, out_specs=pl.BlockSpec((T, cb), lo),
                          out_shape=jax.ShapeDtypeStruct((T, width), out_dtype),
                          compiler_params=_cparams(("parallel",)))(up, up, w, w, b, b)


def _ffn_glu_bwd(name, up, w, b, dact, dx_dtype):
    T = up.shape[0]
    kk = w.shape[0]
    cb = CONV_CB
    width = up.shape[1] // 2
    nblk = width // cb

    def body(g_ref, u_ref, wg_ref, wu_ref, bg_ref, bu_ref, d_ref, dg_ref, du_ref, dwg_ref, dwu_ref, dbg_ref, dbu_ref):
        xg, xu, wg, wu = g_ref[...], u_ref[...], wg_ref[...], wu_ref[...]
        g = _conv_pre(xg, wg, bg_ref[...])
        u = _conv_pre(xu, wu, bu_ref[...])
        d = d_ref[...].astype(F32)
        dg_ref[...] = _conv_bwd_pre(xg, wg, d * u * _dsilu(g), dwg_ref, dbg_ref).astype(dg_ref.dtype)
        du_ref[...] = _conv_bwd_pre(xu, wu, d * jax.nn.silu(g), dwu_ref, dbu_ref).astype(du_ref.dtype)

    lo, hi = (lambda j: (0, j)), (lambda j: (0, nblk + j))
    in_specs = [pl.BlockSpec((T, cb), lo), pl.BlockSpec((T, cb), hi), pl.BlockSpec((kk, cb), lo), pl.BlockSpec((kk, cb), hi),
                pl.BlockSpec((1, cb), lo), pl.BlockSpec((1, cb), hi), pl.BlockSpec((T, cb), lo)]
    out_specs = [pl.BlockSpec((T, cb), lo)] * 2 + [pl.BlockSpec((kk, cb), lo)] * 2 + [pl.BlockSpec((1, cb), lo)] * 2
    out_shape = ([jax.ShapeDtypeStruct((T, width), dx_dtype)] * 2 + [jax.ShapeDtypeStruct((kk, width), F32)] * 2
                 + [jax.ShapeDtypeStruct((1, width), F32)] * 2)
    return pl.pallas_call(body, name=name, grid=(nblk,), in_specs=in_specs, out_specs=out_specs, out_shape=out_shape,
                          compiler_params=_cparams(("parallel",)))(up, up, w, w, b, b, dact)


def _loss_head(y, target):
    T, D = y.shape
    tb = _tile(T, 256, 8)

    def body(y_ref, t_ref, dy_ref, l_ref):
        @pl.when(pl.program_id(0) == 0)
        def _():
            l_ref[...] = jnp.zeros_like(l_ref)

        err = y_ref[...] - t_ref[...]
        dy_ref[...] = err * (1.0 / D)
        l_ref[...] += jnp.sum(err * err, axis=0, keepdims=True) * (0.5 / D)

    spec = pl.BlockSpec((tb, D), lambda i: (i, 0))
    return pl.pallas_call(body, name="loss_head", grid=(T // tb,), in_specs=[spec, spec],
                          out_specs=[spec, pl.BlockSpec((1, D), lambda i: (0, 0))],
                          out_shape=[jax.ShapeDtypeStruct((T, D), F32), jax.ShapeDtypeStruct((1, D), F32)],
                          compiler_params=_cparams(("arbitrary",)))(y, target)


def _adamw_math(w, g, m, v):
    m = ADAM_B1 * m + (1.0 - ADAM_B1) * g
    v = ADAM_B2 * v + (1.0 - ADAM_B2) * jnp.square(g)
    m_hat = m / (1.0 - ADAM_B1 ** ADAM_STEP)
    v_hat = v / (1.0 - ADAM_B2 ** ADAM_STEP)
    return -ADAM_LR * (m_hat / (jnp.sqrt(v_hat) + ADAM_EPS) + ADAM_WD * w), m, v


def _adamw(name, w, g, m, v):
    A, R, C = w.shape
    rb = _tile(R, max(8, (1 << 19) // (C * 4) // 8 * 8), 8)

    def body(w_ref, g_ref, m_ref, v_ref, d_ref, mo_ref, vo_ref):
        d, mn, vn = _adamw_math(w_ref[...], g_ref[...], m_ref[...], v_ref[...])
        d_ref[...] = d
        mo_ref[...] = mn
        vo_ref[...] = vn

    spec = pl.BlockSpec((1, rb, C), lambda a, r: (a, r, 0))
    return pl.pallas_call(body, name=name, grid=(A, R // rb), in_specs=[spec] * 4, out_specs=[spec] * 3,
                          out_shape=[jax.ShapeDtypeStruct(w.shape, F32)] * 3,
                          compiler_params=_cparams(("parallel", "parallel")))(w, g, m, v)


def _adamw_small(parts, w, m, v):
    def body(p_ref, w_ref, m_ref, v_ref, g_ref, d_ref, mo_ref, vo_ref):
        g = p_ref[0]
        for i in range(1, N_DEV):
            g = g + p_ref[i]
        d, mn, vn = _adamw_math(w_ref[...], g, m_ref[...], v_ref[...])
        g_ref[...] = g
        d_ref[...] = d
        mo_ref[...] = mn
        vo_ref[...] = vn

    return pl.pallas_call(body, name="adamw_small", out_shape=[jax.ShapeDtypeStruct(w.shape, F32)] * 4,
                          compiler_params=_cparams())(parts, w, m, v)


def _add_blocks(name, a, b, out_dtype=F32):
    n, R, W = a.shape
    rb = _tile(R, 512, 8)

    def body(a_ref, b_ref, o_ref):
        o_ref[...] = (a_ref[...].astype(F32) + b_ref[...].astype(F32)).astype(o_ref.dtype)

    spec = pl.BlockSpec((1, rb, W), lambda i, r: (i, r, 0))
    return pl.pallas_call(body, name=name, grid=(n, R // rb), in_specs=[spec, spec], out_specs=spec,
                          out_shape=jax.ShapeDtypeStruct(a.shape, out_dtype),
                          compiler_params=_cparams(("parallel", "parallel")))(a, b)


def _pair_add(name, g, other, c, chip):
    _, R, W = g.shape
    rb = _tile(R, 256, 8)

    def body(s_ref, a_ref, b_ref, send_ref, own_ref):
        s = a_ref[0] + b_ref[0]
        send_ref[0] = s.astype(send_ref.dtype)

        @pl.when(pl.program_id(1) == s_ref[1])
        def _():
            own_ref[...] = s

    grid_spec = pltpu.PrefetchScalarGridSpec(
        num_scalar_prefetch=1, grid=(R // rb, 4),
        in_specs=[pl.BlockSpec((1, rb, W), lambda r, p, s_ref: (2 * p + s_ref[0], r, 0)),
                  pl.BlockSpec((1, rb, W), lambda r, p, s_ref: (p, r, 0))],
        out_specs=[pl.BlockSpec((1, rb, W), lambda r, p, s_ref: (p, r, 0)),
                   pl.BlockSpec((rb, W), lambda r, p, s_ref: (r, 0))])
    scalars = jnp.stack([c, chip]).astype(jnp.int32)
    return pl.pallas_call(body, name=name, grid_spec=grid_spec,
                          out_shape=[jax.ShapeDtypeStruct((4, R, W), MXU_DTYPE), jax.ShapeDtypeStruct((R, W), F32)],
                          compiler_params=_cparams(("parallel", "arbitrary")))(scalars, g, other)


def _sum4(name, own, parts):
    R, W = own.shape
    rb = _tile(R, 256, 8)

    def body(o_ref, p_ref, out_ref):
        out_ref[...] = ((o_ref[...] + p_ref[0].astype(F32)) + p_ref[1].astype(F32)) + p_ref[2].astype(F32)

    return pl.pallas_call(body, name=name, grid=(R // rb,),
                          in_specs=[pl.BlockSpec((rb, W), lambda r: (r, 0)), pl.BlockSpec((3, rb, W), lambda r: (0, r, 0))],
                          out_specs=pl.BlockSpec((rb, W), lambda r: (r, 0)), out_shape=jax.ShapeDtypeStruct((R, W), F32),
                          compiler_params=_cparams(("parallel",)))(own, parts)


MESH = pl.DeviceIdType.MESH
ANY = pl.BlockSpec(memory_space=pl.ANY)


def _place():
    return lax.axis_index("x"), lax.axis_index("y"), lax.axis_index("c")


def _other_chips(x, y):
    return [(1 - x, y), (x, 1 - y), (1 - x, 1 - y)]


def _all_gather(name, blocks):
    n = len(blocks)

    def body(*refs):
        x_refs, out_refs = refs[:n], refs[n:2 * n]
        send_sems, recv_sems, local_sems = refs[2 * n:]
        x, y, c = _place()
        me, sibling = (x, y, c), (x, y, 1 - c)
        chips = _other_chips(x, y)

        def slot(a, px, py, pc):
            return out_refs[a].at[4 * px + 2 * py + pc]

        def copy(a, k, blk, to, src=None):
            return pltpu.make_async_remote_copy(src_ref=slot(a, *blk) if src is None else src, dst_ref=slot(a, *blk),
                                                send_sem=send_sems.at[a, k], recv_sem=recv_sems.at[a, k],
                                                device_id=to, device_id_type=MESH)

        mine = [pltpu.make_async_copy(x_refs[a], slot(a, *me), local_sems.at[a]) for a in range(n)]
        for cp in mine:
            cp.start()
        first = []
        for j, chip in enumerate(chips):
            first += [copy(a, 1 + j, me, (*chip, c), src=x_refs[a]) for a in range(n)]
        first += [copy(a, 0, me, sibling, src=x_refs[a]) for a in range(n)]
        for cp in first:
            cp.start()
        passed = []
        for j, chip in enumerate(chips):
            for a in range(n):
                copy(a, 1 + j, (*chip, c), me).wait_recv()
                passed.append(copy(a, 4 + j, (*chip, c), sibling))
                passed[-1].start()
        for a in range(n):
            copy(a, 0, sibling, me).wait_recv()
        for j, chip in enumerate(chips):
            for a in range(n):
                copy(a, 4 + j, (*chip, 1 - c), me).wait_recv()
        for cp in first + passed:
            cp.wait_send()
        for cp in mine:
            cp.wait()

    return pl.pallas_call(body, name=name, in_specs=[ANY] * n, out_specs=[ANY] * n,
                          out_shape=[jax.ShapeDtypeStruct((N_DEV,) + b.shape, b.dtype) for b in blocks],
                          scratch_shapes=[pltpu.SemaphoreType.DMA((n, 7)), pltpu.SemaphoreType.DMA((n, 7)),
                                          pltpu.SemaphoreType.DMA((n,))])(*blocks)


def _routes_to_sibling(x, y, c):
    return [(2 * p + (1 - c), p, (x, y, 1 - c)) for p in range(4)]


def _routes_to_chips(x, y, c):
    return [(2 * px + py, j, (px, py, c)) for j, (px, py) in enumerate(_other_chips(x, y))]


def _routes_block_to_chips(x, y, c):
    me = 4 * x + 2 * y + c
    return [(me, me, (px, py, c)) for px, py in _other_chips(x, y)]


def _routes_blocks_to_sibling(x, y, c):
    return [(4 * px + 2 * py + c, 4 * px + 2 * py + c, (x, y, 1 - c)) for px, py in [(x, y)] + _other_chips(x, y)]


def _route_copies(routes, src_refs, land_refs, send_sems, recv_sems):
    x, y, c = _place()
    copies = []
    for a, (src, land) in enumerate(zip(src_refs, land_refs)):
        plan = routes(x, y, c)
        for k, (s, d, target) in enumerate(plan):
            i = a * len(plan) + k
            copies.append(pltpu.make_async_remote_copy(src_ref=src.at[s], dst_ref=land.at[d], send_sem=send_sems.at[i],
                                                       recv_sem=recv_sems.at[i], device_id=target, device_id_type=MESH))
    return copies


def _exchange(name, routes, n_routes, srcs, land_slots):
    n = len(srcs)

    def body(*refs):
        copies = _route_copies(routes, refs[:n], refs[n:2 * n], refs[2 * n], refs[2 * n + 1])
        for cp in copies:
            cp.start()
        for cp in copies:
            cp.wait_recv()
        for cp in copies:
            cp.wait_send()

    return pl.pallas_call(body, name=name, in_specs=[ANY] * n, out_specs=[ANY] * n,
                          out_shape=[jax.ShapeDtypeStruct((land_slots,) + s.shape[1:], s.dtype) for s in srcs],
                          scratch_shapes=[pltpu.SemaphoreType.DMA((n * n_routes,)), pltpu.SemaphoreType.DMA((n * n_routes,))])(*srcs)


HBM_SPEC = pl.BlockSpec(memory_space=pltpu.HBM)
SEM_SPEC = pl.BlockSpec(memory_space=pltpu.SEMAPHORE)
DATAFLOW = pltpu.SideEffectType.DATAFLOW_SIDE_EFFECTING


def _exchange_start(name, routes, n_routes, srcs, lands):
    n = len(srcs)
    in_place = lands is None
    bufs = list(srcs) + ([] if in_place else list(lands))
    nb = len(bufs)

    def body(*refs):
        src_refs = refs[:n]
        land_refs = src_refs if in_place else refs[n:nb]
        send_sems, recv_sems = refs[nb], refs[nb + 1]
        token = refs[-1]
        for cp in _route_copies(routes, src_refs, land_refs, send_sems, recv_sems):
            cp.start()
        token[...] = jnp.zeros_like(token)

    sems = [pltpu.SemaphoreType.DMA((n * n_routes,)), pltpu.SemaphoreType.DMA((n * n_routes,))]
    out = pl.pallas_call(
        body, name=name, in_specs=[HBM_SPEC] * nb,
        out_shape=sems + [pltpu.HBM(b.shape, b.dtype) for b in bufs] + [jax.ShapeDtypeStruct((8, LANES), F32)],
        out_specs=[SEM_SPEC, SEM_SPEC] + [HBM_SPEC] * nb + [pl.BlockSpec(memory_space=pltpu.VMEM)],
        input_output_aliases={i: 2 + i for i in range(nb)},
        compiler_params=pltpu.CompilerParams(has_side_effects=DATAFLOW))(
        *[pltpu.with_memory_space_constraint(b, pltpu.HBM) for b in bufs])
    return (out[0], out[1], list(out[2:2 + nb])), out[-1]


def _exchange_wait(name, routes, n_routes, n, started, after):
    send_sems, recv_sems, bufs = started
    nb = len(bufs)
    in_place = nb == n

    def body(*refs):
        src_refs = refs[:n]
        land_refs = src_refs if in_place else refs[n:nb]
        for cp in _route_copies(routes, src_refs, land_refs, refs[nb], refs[nb + 1]):
            cp.wait_send()
            cp.wait_recv()

    out = pl.pallas_call(
        body, name=name, in_specs=[HBM_SPEC] * nb + [SEM_SPEC, SEM_SPEC, ANY],
        out_shape=[pltpu.HBM(b.shape, b.dtype) for b in bufs], out_specs=[HBM_SPEC] * nb,
        input_output_aliases={i: i for i in range(nb)},
        compiler_params=pltpu.CompilerParams(has_side_effects=DATAFLOW))(*bufs, send_sems, recv_sems, after)
    return list(out[:n]) if in_place else list(out[n:])


def _reduce_scatter_finish(tag, gs, from_sibling, exchange_chips):
    x, y, c = _place()
    sums = [_pair_add(f"rs_add_{tag}_{i}", g, o, c, 2 * x + y) for i, (g, o) in enumerate(zip(gs, from_sibling))]
    got = exchange_chips([s[0] for s in sums])
    return [_sum4(f"rs_sum_{tag}_{i}", s[1], q) for i, (s, q) in enumerate(zip(sums, got))]


def _reduce_scatter(tag, gs):
    from_sibling = _exchange(f"rs_swap_{tag}", _routes_to_sibling, 4, gs, 4)
    return _reduce_scatter_finish(tag, gs, from_sibling, lambda ps: _exchange(f"rs_chips_{tag}", _routes_to_chips, 3, ps, 3))


def _flat_rows(n_elems):
    return -(-n_elems // (FLAT_W * 16)) * 16


def _pack(arrays, dtype):
    flat = jnp.concatenate([a.reshape(-1).astype(dtype) for a in arrays])
    rows = _flat_rows(flat.shape[0])
    flat = jnp.pad(flat, (0, rows * FLAT_W - flat.shape[0]))
    return flat.reshape(rows, FLAT_W)


def _unpack(flat, shapes, lead=()):
    flat = flat.reshape(lead + (-1,))
    out, pos = [], 0
    for s in shapes:
        n = math.prod(s)
        out.append(flat[..., pos:pos + n].reshape(lead + tuple(s)))
        pos += n
    return out


def _shard_to_send(name, shard):
    if name == "ffn_w_up":
        shard = jnp.pad(shard, ((0, 0), (0, FFN_SHARD_PAD - FFN_SHARD)))
    return shard.astype(MXU_DTYPE)


def _whole_from_gathered(name, g):
    if name == "w_in":
        return _pad_in_proj(jnp.concatenate([g[d] for d in range(N_DEV)], axis=1))
    if name in ("w_br_gdn", "w_br_gla", "ffn_w_up"):
        return jnp.transpose(g, (1, 0, 2)).reshape(g.shape[1], N_DEV * g.shape[2])
    if name == "ffn_w_down":
        blocks = g.reshape(N_DEV // 2, FFN_SHARD, g.shape[2])
        blocks = jnp.pad(blocks, ((0, 0), (0, FFN_SHARD_PAD - FFN_SHARD), (0, 0)))
        return blocks.reshape(FFN_PAD, g.shape[2])
    return g.reshape(N_DEV * g.shape[1], g.shape[2])


def _slots_from_whole(name, gw):
    if name == "w_in":
        full = _unpad_in_proj(gw)
        cs = IN_DIM // N_DEV
        return jnp.stack([full[:, d * cs:(d + 1) * cs] for d in range(N_DEV)])
    if name in ("w_br_gdn", "w_br_gla"):
        return jnp.transpose(gw.reshape(gw.shape[0], N_DEV, gw.shape[1] // N_DEV), (1, 0, 2))
    if name == "ffn_w_up":
        return jnp.transpose(gw.reshape(gw.shape[0], N_DEV, FFN_SHARD_PAD)[:, :, :FFN_SHARD], (1, 0, 2))
    if name == "ffn_w_down":
        blocks = gw.reshape(N_DEV // 2, FFN_SHARD_PAD, gw.shape[1])[:, :FFN_SHARD]
        return blocks.reshape(N_DEV, FFN_SHARD // 2, gw.shape[1])
    return gw.reshape(N_DEV, gw.shape[0] // N_DEV, gw.shape[1])


def _ffn_pad_cols(a):
    lead = a.shape[:-1]
    nblk = a.shape[-1] // FFN_SHARD
    a = a.reshape(lead + (nblk, FFN_SHARD))
    a = jnp.pad(a, [(0, 0)] * (len(lead) + 1) + [(0, FFN_SHARD_PAD - FFN_SHARD)])
    return a.reshape(lead + (nblk * FFN_SHARD_PAD,))


def _ffn_unpad_cols(a):
    lead = a.shape[:-1]
    nblk = a.shape[-1] // FFN_SHARD_PAD
    return a.reshape(lead + (nblk, FFN_SHARD_PAD))[..., :FFN_SHARD].reshape(lead + (nblk * FFN_SHARD,))


def _pad_in_proj(w):
    starts, pos = {}, 0
    for n, width in IN_SPLITS:
        starts[n] = (pos, width)
        pos += width
    cols, at = [], 0
    for _, off, width, pieces in PAD_SEGS:
        assert off == at
        used = 0
        for ref_name, lane in pieces:
            assert lane == used
            s, wd = starts[ref_name]
            cols.append(w[:, s:s + wd])
            used += wd
        if used < width:
            cols.append(jnp.zeros((w.shape[0], width - used), w.dtype))
        at += width
    if at < IN_PAD:
        cols.append(jnp.zeros((w.shape[0], IN_PAD - at), w.dtype))
    return jnp.concatenate(cols, axis=1)


def _unpad_in_proj(wp):
    where = {}
    for _, off, _, pieces in PAD_SEGS:
        for ref_name, lane in pieces:
            where[ref_name] = off + lane
    return jnp.concatenate([wp[:, where[n]:where[n] + width] for n, width in IN_SPLITS], axis=1)


def _lane_pad(a, width=LANES):
    return jnp.pad(a, ((0, 0), (0, width - a.shape[1])))


def _seg_blk(h, name, rows):
    off, width = SEG[name]
    return (h, rows, width, off // width)


def _ln_both(xs_, ps_):
    (y,) = _ln_fn(xs_, ps_)
    return (y, y)


def _layer_fwd(l, x, x_mx, W, sp, mid=None):
    T = x.shape[0]
    n64, ngla, ntok = T // SSD_CHUNK, T // GLA_BLOCK, T // 256
    h = _mm(f"in_proj_{l}", x_mx, W["w_in_pad"])
    xbc = _conv_silu_fwd(f"ssd_conv_{l}", h, SEG["xbc"][0], sp["ssd_conv_w"], sp["ssd_conv_b"])
    gqkv = _conv_silu_fwd(f"gdn_conv_{l}", h, SEG["gqkv"][0], sp["gdn_conv_w"], None)

    ssd_in = [(xbc, SSD_CHUNK, SSD_XBC, 0), _seg_blk(h, "dt", SSD_CHUNK), _seg_blk(h, "z", SSD_CHUNK)]
    ssd_p = [sp["ssd_dt_bias"], sp["ssd_a_log"], sp["ssd_d"], sp["ssd_norm_w"]]
    o_ssd, ssd_states = _chain_fwd(f"ssd_fwd_{l}", _ssd_chunk, n64, ssd_in, ssd_p, [(SSD_CHUNK, SSD_INNER, MXU_DTYPE)],
                                   (SSD_STATE, SSD_INNER))
    gdn_in = [(gqkv, GDN_CHUNK, 3 * GDN_WIDTH, 0), _seg_blk(h, "gab", GDN_CHUNK), _seg_blk(h, "gg", GDN_CHUNK)]
    gdn_p = [_lane_pad(sp["gdn_a_log"]), _lane_pad(sp["gdn_dt_bias"]), sp["gdn_norm_w"]]
    o_gdn, gdn_states = _chain_fwd(f"gdn_fwd_{l}", _gdn_chunk, n64, gdn_in, gdn_p, [(GDN_CHUNK, GDN_WIDTH, MXU_DTYPE)],
                                   (GDN_WIDTH, GDN_HEAD_DIM))
    gla_in = [_seg_blk(h, "lqkv", GLA_BLOCK), _seg_blk(h, "lglr", GLA_BLOCK), _seg_blk(h, "lr", GLA_BLOCK)]
    gla_p = [jnp.pad(sp["gla_gate_w2"], ((0, LANES - GLA_RANK), (0, 0))), sp["gla_gate_b"], sp["gla_norm_w"]]
    o_gla, gla_states = _chain_fwd(f"gla_fwd_{l}", _gla_block, ngla, gla_in, gla_p, [(GLA_BLOCK, GLA_V, MXU_DTYPE)],
                                   (GLA_VAL_DIM, GLA_K))
    y_ssd = _mm(f"br_ssd_{l}", o_ssd, W["w_br_ssd"])
    y_gdn = _mm(f"br_gdn_{l}", o_gdn, W["w_br_gdn"])
    y_gla = _mm(f"br_gla_{l}", o_gla, W["w_br_gla"])
    merge_in = [_seg_blk(h, "gates", 256), (y_ssd, 256, D_MODEL, 0), (y_gdn, 256, D_MODEL, 0), (y_gla, 256, D_MODEL, 0)]
    (mix,) = _chain_fwd(f"merge_{l}", _merge_fn, ntok, merge_in, [], [(256, D_MODEL, MXU_DTYPE)])
    r1 = _mm(f"out_proj_{l}", mix, W["w_out"])
    ln1_p = [sp["ln1_g"], sp["ln1_b"]]
    if mid is not None:
        ln1_p[0] = ln1_p[0] + mid(o_gla)[0:1, 0:1]
    both = [(256, D_MODEL, F32), (256, D_MODEL, MXU_DTYPE)]
    x1, x1_mx = _chain_fwd(f"ln1_{l}", _ln_both, ntok, [(x, 256, D_MODEL, 0), (r1, 256, D_MODEL, 0)], ln1_p, both)
    up = _mm(f"ffn_up_{l}", x1_mx, W["ffn_w_up"])
    act = _ffn_glu_fwd(f"ffn_glu_{l}", up, sp["ffn_conv_w_pad"], sp["ffn_conv_b_pad"], MXU_DTYPE)
    r2 = _mm(f"ffn_down_{l}", act, W["ffn_w_down"])
    ln2_p = [sp["ln2_g"], sp["ln2_b"]]
    x2, x2_mx = _chain_fwd(f"ln2_{l}", _ln_both, ntok, [(x1, 256, D_MODEL, 0), (r2, 256, D_MODEL, 0)], ln2_p, both)
    saved = dict(x=x, x_mx=x_mx, h=h, xbc=xbc, gqkv=gqkv, ssd_in=ssd_in, ssd_p=ssd_p, ssd_states=ssd_states, gdn_in=gdn_in,
                 gdn_p=gdn_p, gdn_states=gdn_states, gla_in=gla_in, gla_p=gla_p, gla_states=gla_states, o_ssd=o_ssd,
                 o_gdn=o_gdn, o_gla=o_gla, merge_in=merge_in, mix=mix, r1=r1, ln1_p=ln1_p, x1=x1, x1_mx=x1_mx, up=up, act=act,
                 r2=r2, ln2_p=ln2_p)
    return x2, x2_mx, saved


def _layer_bwd(l, dx2, W, sp, sv, first=None, mid=None):
    T = dx2.shape[0]
    n64, ngla, ntok = T // SSD_CHUNK, T // GLA_BLOCK, T // 256
    bf = MXU_DTYPE
    gw, gs = {}, {}
    ln2_p = list(sv["ln2_p"])
    if first is not None:
        ln2_p[0] = ln2_p[0] + first[0:1, 0:1]
    (dx1_a, dr2), (gs["ln2_g"], gs["ln2_b"]) = _chain_bwd(
        f"ln2_bwd_{l}", _ln_fn, ntok, [(sv["x1"], 256, D_MODEL, 0), (sv["r2"], 256, D_MODEL, 0)], ln2_p,
        [(dx2, 256, D_MODEL)], dx_dtypes=[F32, bf])
    gw["ffn_w_down"] = _mm(f"ffn_down_dw_{l}", sv["act"], dr2, "tn")
    dact = _mm(f"ffn_down_dx_{l}", dr2, W["ffn_w_down"], "nt")
    dg, du, dwg, dwu, dbg, dbu = _ffn_glu_bwd(f"ffn_glu_bwd_{l}", sv["up"], sp["ffn_conv_w_pad"], sp["ffn_conv_b_pad"], dact, bf)
    gs["ffn_conv_w"] = _ffn_unpad_cols(jnp.concatenate([dwg, dwu], axis=1))
    gs["ffn_conv_b"] = _ffn_unpad_cols(jnp.concatenate([dbg, dbu], axis=1))
    dup = jnp.concatenate([dg, du], axis=1)
    gw["ffn_w_up"] = _mm(f"ffn_up_dw_{l}", sv["x1_mx"], dup, "tn")
    dx1_b = _mm(f"ffn_up_dx_{l}", dup, W["ffn_w_up"], "nt", tn=1024, tk=1024)
    ln1_p = list(sv["ln1_p"])
    if mid is not None:
        ln1_p[0] = ln1_p[0] + mid(dx1_b)[0:1, 0:1]
    (dx_a, dr1), (gs["ln1_g"], gs["ln1_b"]) = _chain_bwd(
        f"ln1_bwd_{l}", _ln_sum_fn, ntok, [(sv["x"], 256, D_MODEL, 0), (sv["r1"], 256, D_MODEL, 0)], ln1_p,
        [(dx1_a, 256, D_MODEL), (dx1_b, 256, D_MODEL)], dx_dtypes=[F32, bf])
    gw["w_out"] = _mm(f"out_proj_dw_{l}", sv["mix"], dr1, "tn")
    dmix = _mm(f"out_proj_dx_{l}", dr1, W["w_out"], "nt")
    (dgates, dy_ssd, dy_gdn, dy_gla), _ = _chain_bwd(f"merge_bwd_{l}", _merge_fn, ntok, sv["merge_in"], [],
                                                     [(dmix, 256, D_MODEL)], dx_dtypes=[bf, bf, bf, bf])
    gw["w_br_ssd"] = _mm(f"br_ssd_dw_{l}", sv["o_ssd"], dy_ssd, "tn")
    gw["w_br_gdn"] = _mm(f"br_gdn_dw_{l}", sv["o_gdn"], dy_gdn, "tn")
    gw["w_br_gla"] = _mm(f"br_gla_dw_{l}", sv["o_gla"], dy_gla, "tn")
    do_ssd = _mm(f"br_ssd_dx_{l}", dy_ssd, W["w_br_ssd"], "nt")
    do_gdn = _mm(f"br_gdn_dx_{l}", dy_gdn, W["w_br_gdn"], "nt")
    do_gla = _mm(f"br_gla_dx_{l}", dy_gla, W["w_br_gla"], "nt")

    (dxbc, ddt, dz), dps = _chain_bwd(f"ssd_bwd_{l}", _ssd_chunk, n64, sv["ssd_in"], sv["ssd_p"],
                                      [(do_ssd, SSD_CHUNK, SSD_INNER)], sprev=sv["ssd_states"], dx_dtypes=[F32, bf, bf])
    gs["ssd_dt_bias"], gs["ssd_a_log"], gs["ssd_d"], gs["ssd_norm_w"] = dps
    (dgqkv, dgab, dgg), dps = _chain_bwd(f"gdn_bwd_{l}", _gdn_chunk, n64, sv["gdn_in"], sv["gdn_p"],
                                         [(do_gdn, GDN_CHUNK, GDN_WIDTH)], sprev=sv["gdn_states"], dx_dtypes=[F32, bf, bf])
    gs["gdn_a_log"], gs["gdn_dt_bias"], gs["gdn_norm_w"] = dps[0][:, :GDN_HEADS], dps[1][:, :GDN_HEADS], dps[2]
    (dlqkv, dlglr, dlr), dps = _chain_bwd(f"gla_bwd_{l}", _gla_block, ngla, sv["gla_in"], sv["gla_p"],
                                          [(do_gla, GLA_BLOCK, GLA_V)], sprev=sv["gla_states"], dx_dtypes=[bf, bf, bf])
    gs["gla_gate_w2"], gs["gla_gate_b"], gs["gla_norm_w"] = dps[0][:GLA_RANK], dps[1], dps[2]
    dxbc_pre, gs["ssd_conv_w"], gs["ssd_conv_b"] = _conv_silu_bwd(
        f"ssd_conv_bwd_{l}", sv["h"], SEG["xbc"][0], sp["ssd_conv_w"], sp["ssd_conv_b"], dxbc, bf)
    dgqkv_pre, gs["gdn_conv_w"] = _conv_silu_bwd(f"gdn_conv_bwd_{l}", sv["h"], SEG["gqkv"][0], sp["gdn_conv_w"], None, dgqkv, bf)
    pieces = dict(gates=dgates, xbc=dxbc_pre, gqkv=dgqkv_pre, z=dz, lqkv=dlqkv, gg=dgg, lr=dlr, dt=ddt, gab=dgab, lglr=dlglr)
    cols = [pieces[name] for name, _, _, _ in PAD_SEGS]
    cols.append(jnp.zeros((T, IN_PAD - PAD_SEGS[-1][1] - PAD_SEGS[-1][2]), bf))
    dh = jnp.concatenate(cols, axis=1)
    gw["w_in_pad"] = _mm(f"in_proj_dw_{l}", sv["x_mx"], dh, "tn")
    dx_b = _mm(f"in_proj_dx_{l}", dh, W["w_in_pad"], "nt", tn=1024, tk=1024)
    dx = _add_blocks(f"dx_add_{l}", dx_a[None], dx_b[None])[0]
    return dx, gw, gs


def _ln_sum_fn(xs_, ps_):
    (y,) = _ln_fn(xs_, ps_)
    return (y, y)


def _small_2d(name, a):
    return a.reshape(1, -1) if a.ndim == 1 else a


def kernel(x, w_in, ssd_conv_w, ssd_conv_b, ssd_dt_bias, ssd_a_log, ssd_d, ssd_norm_w, gdn_conv_w, gdn_a_log, gdn_dt_bias, gdn_norm_w, gla_gate_w2, gla_gate_b, gla_norm_w, w_br_ssd, w_br_gdn, w_br_gla, w_out, ln1_g, ln1_b, ffn_w_up, ffn_conv_w, ffn_conv_b, ffn_w_down, ln2_g, ln2_b, loss_target, m_w_in, m_ssd_conv_w, m_ssd_conv_b, m_ssd_dt_bias, m_ssd_a_log, m_ssd_d, m_ssd_norm_w, m_gdn_conv_w, m_gdn_a_log, m_gdn_dt_bias, m_gdn_norm_w, m_gla_gate_w2, m_gla_gate_b, m_gla_norm_w, m_w_br_ssd, m_w_br_gdn, m_w_br_gla, m_w_out, m_ln1_g, m_ln1_b, m_ffn_w_up, m_ffn_conv_w, m_ffn_conv_b, m_ffn_w_down, m_ln2_g, m_ln2_b, v_w_in, v_ssd_conv_w, v_ssd_conv_b, v_ssd_dt_bias, v_ssd_a_log, v_ssd_d, v_ssd_norm_w, v_gdn_conv_w, v_gdn_a_log, v_gdn_dt_bias, v_gdn_norm_w, v_gla_gate_w2, v_gla_gate_b, v_gla_norm_w, v_w_br_ssd, v_w_br_gdn, v_w_br_gla, v_w_out, v_ln1_g, v_ln1_b, v_ffn_w_up, v_ffn_conv_w, v_ffn_conv_b, v_ffn_w_down, v_ln2_g, v_ln2_b):
    args = locals()
    w = {n: args[n] for n in WEIGHTS}
    m = {n: args["m_" + n] for n in WEIGHTS}
    v = {n: args["v_" + n] for n in WEIGHTS}
    dev = 4 * lax.axis_index("x") + 2 * lax.axis_index("y") + lax.axis_index("c")
    xl = x[0]
    tgt = loss_target[0]

    sm_flat = _pack([w[n] for n in SMALL_SHARDED], F32)
    nbig = len(BIG)

    def whole_weights(got):
        Wl = {n: _whole_from_gathered(n, g) for n, g in zip(BIG, got)}
        Wl["w_in_pad"] = Wl.pop("w_in")
        return Wl

    zones = [lax.dynamic_update_slice_in_dim(lax.empty((N_DEV,) + s.shape, s.dtype), s[None], dev, axis=0)
             for s in (_shard_to_send(n, w[n][1]) for n in BIG)]
    gather1, gather1_token = _exchange_start("gather_w_1_chips_start", _routes_block_to_chips, 3, zones, None)
    got0 = _all_gather("gather_w_0", [_shard_to_send(n, w[n][0]) for n in BIG] + [sm_flat])
    sm_gathered = got0[-1]
    W = [whole_weights(got0[:nbig]), None]
    sm_shards = _unpack(sm_gathered, [w[n].shape for n in SMALL_SHARDED], lead=(N_DEV,))
    whole = dict(w)
    for n, s in zip(SMALL_SHARDED, sm_shards):
        whole[n] = jnp.transpose(s, (1, 2, 0, 3)).reshape(s.shape[1], s.shape[2], N_DEV * s.shape[3])
    SP = [{n: _small_2d(n, whole[n][l]) for n in SMALL} for l in range(DEPTH)]
    for sp in SP:
        sp["ffn_conv_w_pad"] = _ffn_pad_cols(sp["ffn_conv_w"])
        sp["ffn_conv_b_pad"] = _ffn_pad_cols(sp["ffn_conv_b"])

    held = {}

    def gather1_mid(mixed):
        zones = _exchange_wait("gather_w_1_chips_wait", _routes_block_to_chips, 3, nbig, gather1, mixed)
        held["gather1"], token = _exchange_start("gather_w_1_sibling_start", _routes_blocks_to_sibling, 4, zones, None)
        return token

    saved = [None] * DEPTH
    act, act_mx, saved[0] = _layer_fwd(0, xl, (xl + gather1_token[0, 0]).astype(MXU_DTYPE), W[0], SP[0], mid=gather1_mid)
    W[1] = whole_weights(_exchange_wait("gather_w_1_sibling_wait", _routes_blocks_to_sibling, 4, nbig, held["gather1"], act))
    act, act_mx, saved[1] = _layer_fwd(1, act, act_mx, W[1], SP[1])
    dy, loss_parts = _loss_head(act, tgt)
    loss = lax.psum(jnp.sum(loss_parts), ("x", "y", "c"))

    def slots_of(gw):
        gw["w_in"] = gw.pop("w_in_pad")
        return [_slots_from_whole(n, gw[n]) for n in BIG]

    grads = {}
    GS, red = [None] * DEPTH, [None] * DEPTH
    dy, gw, GS[1] = _layer_bwd(1, dy, W[1], SP[1], saved[1])
    slots1 = slots_of(gw)
    swap1, swap1_token = _exchange_start("rs_swap_1_start", _routes_to_sibling, 4, slots1,
                                         [lax.empty((4,) + s.shape[1:], s.dtype) for s in slots1])

    def reduce1_mid(ffn_done):
        from_sibling = _exchange_wait("rs_swap_1_wait", _routes_to_sibling, 4, nbig, swap1, ffn_done)
        held["sums1"] = [_pair_add(f"rs_add_1_{i}", g, o, lax.axis_index("c"), 2 * lax.axis_index("x") + lax.axis_index("y"))
                         for i, (g, o) in enumerate(zip(slots1, from_sibling))]
        partials = [s[0] for s in held["sums1"]]
        held["chips1"], token = _exchange_start("rs_chips_1_start", _routes_to_chips, 3, partials,
                                                [lax.empty((3,) + p.shape[1:], p.dtype) for p in partials])
        return token

    dy, gw, GS[0] = _layer_bwd(0, dy, W[0], SP[0], saved[0], first=swap1_token, mid=reduce1_mid)
    got1 = _exchange_wait("rs_chips_1_wait", _routes_to_chips, 3, nbig, held["chips1"], dy)
    red[1] = [_sum4(f"rs_sum_1_{i}", s[1], q) for i, (s, q) in enumerate(zip(held["sums1"], got1))]
    red[0] = _reduce_scatter("0", slots_of(gw))
    grad_x = dy[None]
    for i, n in enumerate(BIG):
        grads[n] = jnp.stack([red[l][i] for l in range(DEPTH)])

    small_shapes = [whole[n].shape for n in SMALL]
    gs_flat = _pack([jnp.stack([GS[l][n].reshape(whole[n].shape[1:]) for l in range(DEPTH)]) for n in SMALL], F32)
    (gs_all,) = _all_gather("gather_small_grads", [gs_flat])

    def mine(n, a):
        if n in SMALL_SHARDED:
            cs = a.shape[-1] // N_DEV
            return lax.dynamic_slice_in_dim(a, dev * cs, cs, axis=a.ndim - 1)
        return a

    m_whole, v_whole = {}, {}
    for n in SMALL:
        if n in SMALL_SHARDED:
            cs = w[n].shape[-1]
            zeros = jnp.zeros(whole[n].shape, F32)
            m_whole[n] = lax.dynamic_update_slice_in_dim(zeros, m[n], dev * cs, axis=2)
            v_whole[n] = lax.dynamic_update_slice_in_dim(zeros, v[n], dev * cs, axis=2)
        else:
            m_whole[n], v_whole[n] = m[n], v[n]
    outs = _adamw_small(gs_all, _pack([whole[n] for n in SMALL], F32), _pack([m_whole[n] for n in SMALL], F32),
                        _pack([v_whole[n] for n in SMALL], F32))
    g_s, d_s, m_s, v_s = [_unpack(o, small_shapes) for o in outs]
    delta, new_m, new_v = {}, {}, {}
    for i, n in enumerate(SMALL):
        grads[n], delta[n], new_m[n], new_v[n] = mine(n, g_s[i]), mine(n, d_s[i]), mine(n, m_s[i]), mine(n, v_s[i])
    for n in BIG:
        delta[n], new_m[n], new_v[n] = _adamw(f"adamw_{n}", w[n], grads[n], m[n], v[n])

    return (loss, grad_x, *[grads[n] for n in WEIGHTS], *[delta[n] for n in WEIGHTS], *[new_m[n] for n in WEIGHTS],
            *[new_v[n] for n in WEIGHTS])
```

```python
import functools
import math

import jax
import jax.numpy as jnp
from jax import lax
from jax.experimental import pallas as pl
from jax.experimental.pallas import tpu as pltpu

F32 = jnp.float32
MXU_DTYPE = jnp.bfloat16
HI = lax.Precision.HIGHEST

N_DEV = 8
D_MODEL = 1024
DEPTH = 2
SSD_HEADS, SSD_HEAD_DIM, SSD_INNER, SSD_GROUPS, SSD_STATE, SSD_CHUNK = 16, 64, 1024, 2, 128, 64
SSD_XBC = SSD_INNER + 2 * SSD_GROUPS * SSD_STATE
GDN_HEADS, GDN_HEAD_DIM, GDN_WIDTH, GDN_CHUNK = 4, 128, 512, 64
GLA_HEADS, GLA_KEY_DIM, GLA_VAL_DIM, GLA_K, GLA_V, GLA_RANK, GLA_CHUNK = 4, 64, 128, 256, 512, 16, 16
GLA_BLOCK = 128
GLA_NORMALIZER = 16.0
FFN_DIM = 2816
FFN_SHARD = 2 * FFN_DIM // 8
FFN_SHARD_PAD = 768
FFN_UP_PAD = 8 * FFN_SHARD_PAD
FFN_PAD = FFN_UP_PAD // 2
ALPHA = (2 * DEPTH) ** 0.25
LN_EPS = 1e-5
RMS_EPS = 1e-6
ADAM_LR, ADAM_B1, ADAM_B2, ADAM_EPS, ADAM_WD, ADAM_STEP = 0.001, 0.9, 0.999, 1e-08, 0.01, 10
LANES = 128
NEG_BIG = -1e30
VMEM_LIMIT = 56 * 1024 * 1024

IN_SPLITS = (("z", 1024), ("xbc", 1536), ("dt", 16), ("gqkv", 1536), ("ga", 4), ("gb", 4), ("gg", 512),
             ("lqkv", 1024), ("lglr", 16), ("lr", 512), ("gates", 3072))
IN_DIM = sum(w for _, w in IN_SPLITS)
PAD_SEGS = (("gates", 0, 3072, (("gates", 0),)), ("xbc", 3072, 1536, (("xbc", 0),)),
            ("gqkv", 4608, 1536, (("gqkv", 0),)), ("z", 6144, 1024, (("z", 0),)),
            ("lqkv", 7168, 1024, (("lqkv", 0),)), ("gg", 8192, 512, (("gg", 0),)), ("lr", 8704, 512, (("lr", 0),)),
            ("dt", 9216, 128, (("dt", 0),)), ("gab", 9344, 128, (("ga", 0), ("gb", 4))), ("lglr", 9472, 128, (("lglr", 0),)))
IN_PAD = 9728
SEG = {name: (off, width) for name, off, width, _ in PAD_SEGS}

BIG = ("w_in", "w_br_ssd", "w_br_gdn", "w_br_gla", "w_out", "ffn_w_up", "ffn_w_down")
COL_SHARDED = ("w_in", "w_br_gdn", "w_br_gla", "ffn_w_up")
SMALL_SHARDED = ("ssd_conv_w", "gdn_conv_w", "gla_gate_w2", "ffn_conv_w")
WEIGHTS = ("w_in", "ssd_conv_w", "ssd_conv_b", "ssd_dt_bias", "ssd_a_log", "ssd_d", "ssd_norm_w", "gdn_conv_w",
           "gdn_a_log", "gdn_dt_bias", "gdn_norm_w", "gla_gate_w2", "gla_gate_b", "gla_norm_w", "w_br_ssd", "w_br_gdn",
           "w_br_gla", "w_out", "ln1_g", "ln1_b", "ffn_w_up", "ffn_conv_w", "ffn_conv_b", "ffn_w_down", "ln2_g", "ln2_b")
SMALL = tuple(n for n in WEIGHTS if n not in BIG)
FLAT_W = 512


def _cparams(sem=None):
    kw = dict(vmem_limit_bytes=VMEM_LIMIT)
    if sem is not None:
        kw["dimension_semantics"] = sem
    return pltpu.CompilerParams(**kw)


_DIMS = {"nn": (((1,), (0,)), ((), ())), "nt": (((1,), (1,)), ((), ())), "tn": (((0,), (0,)), ((), ()))}


def _dot(a, b, dims="nn"):
    if MXU_DTYPE == F32:
        return lax.dot_general(a.astype(F32), b.astype(F32), _DIMS[dims], precision=HI, preferred_element_type=F32)
    return lax.dot_general(a.astype(MXU_DTYPE), b.astype(MXU_DTYPE), _DIMS[dims], preferred_element_type=F32)


def _dot_hi(a, b, dims="nn"):
    return lax.dot_general(a.astype(F32), b.astype(F32), _DIMS[dims], precision=HI, preferred_element_type=F32)


def _iota2(shape, axis):
    return lax.broadcasted_iota(jnp.int32, shape, axis)


def _tril(n, strict=False):
    r, c = _iota2((n, n), 0), _iota2((n, n), 1)
    return (r > c) if strict else (r >= c)


def _raw_dot(a, b, dims):
    return lax.dot_general(a, b, _DIMS[dims], preferred_element_type=F32)


def _dot_x3(a, b, dims="nn"):
    if MXU_DTYPE == F32:
        return _dot_hi(a, b, dims)
    ah, bh = a.astype(jnp.bfloat16), b.astype(jnp.bfloat16)
    al, bl = (a - ah.astype(F32)).astype(jnp.bfloat16), (b - bh.astype(F32)).astype(jnp.bfloat16)
    return _raw_dot(ah, bh, dims) + (_raw_dot(ah, bl, dims) + _raw_dot(al, bh, dims))


def _exact_dot(mask, b, dims, mask_first):
    if MXU_DTYPE == F32:
        return _dot_hi(mask, b, dims) if mask_first else _dot_hi(b, mask, dims)
    m = mask.astype(jnp.bfloat16)
    b1 = b.astype(jnp.bfloat16)
    r1 = b - b1.astype(F32)
    b2 = r1.astype(jnp.bfloat16)
    b3 = (r1 - b2.astype(F32)).astype(jnp.bfloat16)
    if mask_first:
        return _raw_dot(m, b1, dims) + (_raw_dot(m, b2, dims) + _raw_dot(m, b3, dims))
    return _raw_dot(b1, m, dims) + (_raw_dot(b2, m, dims) + _raw_dot(b3, m, dims))


@jax.custom_vjp
def _mask_left(mask, b):
    return _exact_dot(mask, b, "nn", True)


_mask_left.defvjp(lambda mask, b: (_mask_left(mask, b), mask),
                  lambda mask, d: (jnp.zeros_like(mask), _exact_dot(mask, d, "tn", True)))


@jax.custom_vjp
def _mask_right(a, mask):
    return _exact_dot(mask, a, "nn", False)


_mask_right.defvjp(lambda a, mask: (_mask_right(a, mask), mask),
                   lambda mask, d: (_exact_dot(mask, d, "nt", False), jnp.zeros_like(mask)))


@jax.custom_vjp
def _unit_lower_inverses(mats):
    n = mats[0].shape[0]
    eye = (_iota2((n, n), 0) == _iota2((n, n), 1)).astype(F32)
    xs = [eye - a for a in mats]
    ps = list(mats)
    k = 2
    while k < n:
        ps = [_dot_x3(p, p) for p in ps]
        xs = [x + _dot_x3(x, p) for x, p in zip(xs, ps)]
        k *= 2
    return xs


def _unit_lower_inverses_fwd(mats):
    ts = _unit_lower_inverses(mats)
    return ts, ts


def _unit_lower_inverses_bwd(ts, dts):
    mids = [_dot_x3(t, d, "tn") for t, d in zip(ts, dts)]
    return ([-_dot_x3(m, t, "nt") for m, t in zip(mids, ts)],)


_unit_lower_inverses.defvjp(_unit_lower_inverses_fwd, _unit_lower_inverses_bwd)


def _ssd_chunk(xs_, ps_, s_t):
    xbc, dtraw, z = xs_
    dt_bias, a_log, d_skip, norm_w = ps_
    L = xbc.shape[0]
    H, P, N, G = SSD_HEADS, SSD_HEAD_DIM, SSD_STATE, SSD_GROUPS
    W = SSD_INNER // G
    xs = xbc[:, :SSD_INNER]
    bm = xbc[:, SSD_INNER:SSD_INNER + G * N]
    cm = xbc[:, SSD_INNER + G * N:]
    dt = jax.nn.softplus(dtraw[:, :H] + dt_bias)
    a = dt * (-jnp.exp(a_log))
    causal = _tril(L)
    a_cs = _mask_left(causal.astype(F32), a)
    expand = (_iota2((H, SSD_INNER), 1) // P == _iota2((H, SSD_INNER), 0)).astype(F32)
    wide = _mask_right(jnp.concatenate([a_cs, dt, jnp.broadcast_to(d_skip, (L, H))], axis=0), expand)
    a_cs_x, dt_x, d_x = wide[:L], wide[L:2 * L], wide[2 * L:]
    a_end_x = a_cs_x[L - 1:L, :]
    a_cs_t, dt_t = a_cs.T, dt.T
    cb = [_dot(cm[:, g * N:(g + 1) * N], bm[:, g * N:(g + 1) * N], "nt") for g in range(G)]
    decay = [jnp.exp(jnp.where(causal, a_cs[:, h:h + 1] - a_cs_t[h:h + 1, :], NEG_BIG)) * dt_t[h:h + 1, :] for h in range(H)]
    ws = [cb[h // (H // G)] * decay[h] for h in range(H)]
    y = jnp.concatenate([_dot(ws[h], xs[:, h * P:(h + 1) * P]) for h in range(H)], axis=1)
    y_in = jnp.concatenate([_dot(cm[:, g * N:(g + 1) * N], s_t[:, g * W:(g + 1) * W]) for g in range(G)], axis=1)
    y = y + y_in * jnp.exp(a_cs_x) + d_x * xs
    xw = xs * (jnp.exp(a_end_x - a_cs_x) * dt_x)
    st = jnp.concatenate([_dot(bm[:, g * N:(g + 1) * N], xw[:, g * W:(g + 1) * W], "tn") for g in range(G)], axis=1)
    s_new = s_t * jnp.exp(a_end_x) + st
    yg = y * jax.nn.silu(z)
    outs = []
    for g in range(G):
        part = yg[:, g * W:(g + 1) * W]
        outs.append(part * lax.rsqrt(jnp.mean(part * part, axis=1, keepdims=True) + RMS_EPS))
    return (jnp.concatenate(outs, axis=1) * norm_w,), s_new


GDN_PREP_CHUNKS = 4


def _gdn_prep(xs_, ps_):
    qkv, ab = xs_
    a_log, dt_bias = ps_
    B = qkv.shape[0]
    H, D, L = GDN_HEADS, GDN_HEAD_DIM, GDN_CHUNK
    g_all = -jnp.exp(a_log) * jax.nn.softplus(ab + dt_bias)
    row, col = _iota2((B, B), 0), _iota2((B, B), 1)
    g_cs = _mask_left((((row // L) == (col // L)) & (row >= col)).astype(F32), g_all)
    g_cs_t = g_cs.T
    beta_all = jax.nn.sigmoid(ab)
    incl, strict = _tril(L), _tril(L, strict=True)
    qs, ks, vs = [], [], []
    for h in range(H):
        q = qkv[:, h * D:(h + 1) * D]
        k = qkv[:, GDN_WIDTH + h * D:GDN_WIDTH + (h + 1) * D]
        qs.append(q * lax.rsqrt(jnp.sum(q * q, axis=1, keepdims=True) + RMS_EPS) * (D ** -0.5))
        ks.append(k * lax.rsqrt(jnp.sum(k * k, axis=1, keepdims=True) + RMS_EPS))
        vs.append(qkv[:, 2 * GDN_WIDTH + h * D:2 * GDN_WIDTH + (h + 1) * D])
    pairs = [(c, h) for c in range(B // L) for h in range(H)]
    rows = {c: slice(c * L, (c + 1) * L) for c in range(B // L)}
    q_ = {(c, h): qs[h][rows[c]] for c, h in pairs}
    k_ = {(c, h): ks[h][rows[c]] for c, h in pairs}
    col_ = {(c, h): g_cs[rows[c], h:h + 1] for c, h in pairs}
    beta_ = {(c, h): beta_all[rows[c], H + h:H + h + 1] for c, h in pairs}
    gamma = {p: jnp.exp(jnp.where(incl, col_[p] - g_cs_t[p[1]:p[1] + 1, rows[p[0]]], NEG_BIG)) for p in pairs}
    kb = {p: k_[p] * beta_[p] for p in pairs}
    a_mat = [jnp.where(strict, _dot(kb[p], k_[p], "nt") * gamma[p], 0.0) for p in pairs]
    attn = {p: jnp.where(incl, _dot(q_[p], k_[p], "nt") * gamma[p], 0.0) for p in pairs}
    t_mat = dict(zip(pairs, _unit_lower_inverses(a_mat)))
    u = {p: _dot(t_mat[p], vs[p[1]][rows[p[0]]] * beta_[p]) for p in pairs}
    w = {p: _dot(t_mat[p], kb[p] * jnp.exp(col_[p])) for p in pairs}
    qd = {p: q_[p] * jnp.exp(col_[p]) for p in pairs}
    kd = {p: k_[p] * jnp.exp(col_[p][L - 1:L, :] - col_[p]) for p in pairs}

    def whole(parts):
        return jnp.concatenate([jnp.concatenate([parts[(c, h)] for h in range(H)], axis=1) for c in range(B // L)], axis=0)

    return (whole(u), whole(w), whole(qd), whole(kd), whole(attn), g_cs)


def _gdn_scan(xs_, ps_, s):
    u, w, qd, kd, attn, g_cs, gate = xs_
    (norm_w,) = ps_
    L = u.shape[0]
    H, D = GDN_HEADS, GDN_HEAD_DIM
    heads = range(H)
    lanes = [slice(h * D, (h + 1) * D) for h in heads]
    s_h = [s[lanes[h], :] for h in heads]
    v_new = [u[:, lanes[h]] - _dot(w[:, lanes[h]], s_h[h]) for h in heads]
    o = [_dot(qd[:, lanes[h]], s_h[h]) + _dot(attn[:, h * L:(h + 1) * L], v_new[h]) for h in heads]
    decay = [jnp.exp(g_cs[L - 1:L, h:h + 1]) for h in heads]
    s_new = [s_h[h] * decay[h] + _dot(kd[:, lanes[h]], v_new[h], "tn") for h in heads]
    o = [o[h] * lax.rsqrt(jnp.mean(o[h] * o[h], axis=1, keepdims=True) + RMS_EPS) * norm_w * jax.nn.silu(gate[:, lanes[h]])
         for h in heads]
    return (jnp.concatenate(o, axis=1),), jnp.concatenate(s_new, axis=0)


def _gdn_forward(tag, gqkv, h, sp):
    T = gqkv.shape[0]
    blk = GDN_PREP_CHUNKS * GDN_CHUNK
    prep_in = [(gqkv, blk, 3 * GDN_WIDTH, 0), _seg_blk(h, "gab", blk)]
    prep_p = [_lane_pad(sp["gdn_a_log"]), _lane_pad(sp["gdn_dt_bias"])]
    mx = MXU_DTYPE
    prep = _chain_fwd(f"gdn_prep_{tag}", _gdn_prep, T // blk, prep_in, prep_p,
                      [(blk, GDN_WIDTH, F32), (blk, GDN_WIDTH, mx), (blk, GDN_WIDTH, mx), (blk, GDN_WIDTH, mx),
                       (blk, GDN_HEADS * GDN_CHUNK, mx), (blk, LANES, F32)])
    widths = [GDN_WIDTH] * 4 + [GDN_HEADS * GDN_CHUNK, LANES]
    scan_in = [(a, GDN_CHUNK, wd, 0) for a, wd in zip(prep, widths)] + [_seg_blk(h, "gg", GDN_CHUNK)]
    scan_p = [sp["gdn_norm_w"]]
    o, states = _chain_fwd(f"gdn_scan_{tag}", _gdn_scan, T // GDN_CHUNK, scan_in, scan_p, [(GDN_CHUNK, GDN_WIDTH, mx)],
                           (GDN_WIDTH, GDN_HEAD_DIM))
    return o, dict(prep_in=prep_in, prep_p=prep_p, scan_in=scan_in, scan_p=scan_p, states=states, widths=widths)


def _gdn_backward(tag, do, sv, dx_dtype):
    T = do.shape[0]
    blk = GDN_PREP_CHUNKS * GDN_CHUNK
    dscan, (dnorm,) = _chain_bwd(f"gdn_scan_bwd_{tag}", _gdn_scan, T // GDN_CHUNK, sv["scan_in"], sv["scan_p"],
                                 [(do, GDN_CHUNK, GDN_WIDTH)], sprev=sv["states"], dx_dtypes=[F32] * 6 + [dx_dtype])
    douts = [(d, blk, wd) for d, wd in zip(dscan[:6], sv["widths"])]
    (dgqkv, dgab), (da_log, ddt_bias) = _chain_bwd(f"gdn_prep_bwd_{tag}", _gdn_prep, T // blk, sv["prep_in"], sv["prep_p"],
                                                   douts, dx_dtypes=[F32, dx_dtype])
    return dgqkv, dgab, dscan[6], da_log[:, :GDN_HEADS], ddt_bias[:, :GDN_HEADS], dnorm


def _gla_block(xs_, ps_, s_t):
    qkv, glr, r = xs_
    w2, gate_b, norm_w = ps_
    B = qkv.shape[0]
    H, K, V, C = GLA_HEADS, GLA_KEY_DIM, GLA_VAL_DIM, GLA_CHUNK
    q = qkv[:, :GLA_K] * (K ** -0.5)
    k = qkv[:, GLA_K:2 * GLA_K]
    v = qkv[:, 2 * GLA_K:]
    gk = jax.nn.log_sigmoid(_dot(glr, w2) + gate_b) / GLA_NORMALIZER
    row, col = _iota2((B, B), 0), _iota2((B, B), 1)
    same = (row // C) == (col // C)
    mask = same & (row >= col)
    b_cs = _mask_left(mask.astype(F32), gk)
    b_end = _mask_left((col == (row // C) * C + (C - 1)).astype(F32), b_cs)
    q_e = q * jnp.exp(b_cs)
    k_e = k * jnp.exp(-b_cs)
    k_d = k * jnp.exp(b_end - b_cs)
    intra = []
    for h in range(H):
        a_mat = jnp.where(mask, _dot(q_e[:, h * K:(h + 1) * K], k_e[:, h * K:(h + 1) * K], "nt"), 0.0)
        intra.append(_dot(a_mat, v[:, h * V:(h + 1) * V]))
    o = jnp.concatenate(intra, axis=1)
    chunks = [slice(j * C, (j + 1) * C) for j in range(B // C)]
    fresh = [jnp.concatenate([_dot(v[sl, h * V:(h + 1) * V], k_d[sl, h * K:(h + 1) * K], "tn") for h in range(H)], axis=1)
             for sl in chunks]
    entering = []
    for j, sl in enumerate(chunks):
        entering.append(s_t)
        s_t = s_t * jnp.exp(b_end[j * C:j * C + 1, :]) + fresh[j]
    inter = [jnp.concatenate([_dot(q_e[sl, h * K:(h + 1) * K], entering[j][:, h * K:(h + 1) * K], "nt") for h in range(H)],
                             axis=1) for j, sl in enumerate(chunks)]
    o = o + jnp.concatenate(inter, axis=0)
    outs = []
    for h in range(H):
        oh = o[:, h * V:(h + 1) * V]
        oh = oh * lax.rsqrt(jnp.mean(oh * oh, axis=1, keepdims=True) + RMS_EPS) * norm_w
        outs.append(oh * jax.nn.silu(r[:, h * V:(h + 1) * V]))
    return (jnp.concatenate(outs, axis=1),), s_t


def _merge_fn(xs_, ps_):
    gates, y_ssd, y_gdn, y_gla = xs_
    d = D_MODEL
    return (jax.nn.sigmoid(gates[:, :d]) * y_ssd + jax.nn.sigmoid(gates[:, d:2 * d]) * y_gdn
            + jax.nn.sigmoid(gates[:, 2 * d:]) * y_gla,)


def _ln_fn(xs_, ps_):
    x, r = xs_
    g, b = ps_
    t = ALPHA * x + r
    mu = jnp.mean(t, axis=1, keepdims=True)
    var = jnp.mean(jnp.square(t - mu), axis=1, keepdims=True)
    return ((t - mu) * lax.rsqrt(var + LN_EPS) * g + b,)


def _row_spec(rows, width, colblk, n, reverse):
    if reverse:
        return pl.BlockSpec((rows, width), lambda c: (n - 1 - c, colblk))
    return pl.BlockSpec((rows, width), lambda c: (c, colblk))


def _full_spec(shape):
    zeros = (0,) * len(shape)
    return pl.BlockSpec(shape, lambda c: zeros)


def _chain_fwd(name, fn, n, blocked, full, out_defs, state_shape=None):
    nb, nf, no = len(blocked), len(full), len(out_defs)

    def body(*refs):
        xs = [r[...].astype(F32) for r in refs[:nb]]
        ps = [r[...] for r in refs[nb:nb + nf]]
        o_refs = refs[nb + nf:nb + nf + no]
        if state_shape is None:
            outs = fn(xs, ps)
        else:
            sprev_ref, s_ref = refs[nb + nf + no:]

            @pl.when(pl.program_id(0) == 0)
            def _():
                s_ref[...] = jnp.zeros_like(s_ref)

            s = s_ref[...]
            sprev_ref[0] = s
            outs, s_new = fn(xs, ps, s)
            s_ref[...] = s_new
        for r, o in zip(o_refs, outs):
            r[...] = o.astype(r.dtype)

    in_specs = [_row_spec(rows, width, cb, n, False) for _, rows, width, cb in blocked]
    in_specs += [_full_spec(a.shape) for a in full]
    out_specs = [_row_spec(rows, width, 0, n, False) for rows, width, _ in out_defs]
    out_shape = [jax.ShapeDtypeStruct((n * rows, width), dt) for rows, width, dt in out_defs]
    scratch = []
    if state_shape is not None:
        out_specs.append(pl.BlockSpec((1,) + state_shape, lambda c: (c, 0, 0)))
        out_shape.append(jax.ShapeDtypeStruct((n,) + state_shape, F32))
        scratch.append(pltpu.VMEM(state_shape, F32))
    return pl.pallas_call(body, name=name, grid=(n,), in_specs=in_specs, out_specs=out_specs, out_shape=out_shape,
                          scratch_shapes=scratch, compiler_params=_cparams(("arbitrary",)))(
        *[a for a, _, _, _ in blocked], *full)


def _chain_bwd(name, fn, n, blocked, full, douts, sprev=None, dx_dtypes=None):
    nb, nf, nd = len(blocked), len(full), len(douts)
    has_state = sprev is not None
    dx_dtypes = dx_dtypes or [F32] * nb

    def body(*refs):
        pos = 0
        b_refs = refs[pos:pos + nb]; pos += nb
        f_refs = refs[pos:pos + nf]; pos += nf
        d_refs = refs[pos:pos + nd]; pos += nd
        if has_state:
            sprev_ref = refs[pos]; pos += 1
        dx_refs = refs[pos:pos + nb]; pos += nb
        dp_refs = refs[pos:pos + nf]; pos += nf
        if has_state:
            ds_ref = refs[pos]

        @pl.when(pl.program_id(0) == 0)
        def _():
            for r in dp_refs:
                r[...] = jnp.zeros_like(r)
            if has_state:
                ds_ref[...] = jnp.zeros_like(ds_ref)

        xs = [r[...].astype(F32) for r in b_refs]
        ps = [r[...] for r in f_refs]
        dys = tuple(r[...].astype(F32) for r in d_refs)
        if has_state:
            _, vjp = jax.vjp(fn, xs, ps, sprev_ref[0])
            dxs, dps, ds = vjp((dys, ds_ref[...]))
            ds_ref[...] = ds
        else:
            _, vjp = jax.vjp(fn, xs, ps)
            dxs, dps = vjp(dys)
        for r, d in zip(dx_refs, dxs):
            r[...] = d.astype(r.dtype)
        for r, d in zip(dp_refs, dps):
            r[...] += d

    in_specs = [_row_spec(rows, width, cb, n, True) for _, rows, width, cb in blocked]
    in_specs += [_full_spec(a.shape) for a in full]
    in_specs += [_row_spec(rows, width, 0, n, True) for _, rows, width in douts]
    args = [a for a, _, _, _ in blocked] + list(full) + [a for a, _, _ in douts]
    scratch = []
    if has_state:
        st_shape = sprev.shape[1:]
        in_specs.append(pl.BlockSpec((1,) + st_shape, lambda c: (n - 1 - c, 0, 0)))
        args.append(sprev)
        scratch.append(pltpu.VMEM(st_shape, F32))
    out_specs = [_row_spec(rows, width, 0, n, True) for _, rows, width, _ in blocked]
    out_specs += [_full_spec(a.shape) for a in full]
    out_shape = [jax.ShapeDtypeStruct((n * rows, width), dt) for (_, rows, width, _), dt in zip(blocked, dx_dtypes)]
    out_shape += [jax.ShapeDtypeStruct(a.shape, F32) for a in full]
    res = pl.pallas_call(body, name=name, grid=(n,), in_specs=in_specs, out_specs=out_specs, out_shape=out_shape,
                         scratch_shapes=scratch, compiler_params=_cparams(("arbitrary",)))(*args)
    return res[:nb], res[nb:]


def _tile(n, target, unit):
    if n <= target:
        return n
    best = None
    for t in range(unit, target + 1, unit):
        if n % t == 0:
            best = t
    assert best is not None, (n, target, unit)
    return best


def _mm(name, a, b, dims="nn", out_dtype=F32, tm=2048, tn=512, tk=2048):
    if dims == "nn":
        (M, K), (_, N) = a.shape, b.shape
    elif dims == "nt":
        (M, K), (N, _) = a.shape, b.shape
    else:
        (K, M), (_, N) = a.shape, b.shape
    tm, tn, tk = _tile(M, tm, LANES), _tile(N, tn, LANES), _tile(K, tk, LANES)
    nk = K // tk

    def body(a_ref, b_ref, o_ref, acc_ref):
        part = _dot(a_ref[...], b_ref[...], dims)
        if nk == 1:
            o_ref[...] = part.astype(o_ref.dtype)
            return

        @pl.when(pl.program_id(2) == 0)
        def _():
            acc_ref[...] = part

        @pl.when(pl.program_id(2) > 0)
        def _():
            acc_ref[...] += part

        @pl.when(pl.program_id(2) == nk - 1)
        def _():
            o_ref[...] = acc_ref[...].astype(o_ref.dtype)

    if dims == "tn":
        a_spec = pl.BlockSpec((tk, tm), lambda j, i, k: (k, i))
    else:
        a_spec = pl.BlockSpec((tm, tk), lambda j, i, k: (i, k))
    if dims == "nt":
        b_spec = pl.BlockSpec((tn, tk), lambda j, i, k: (j, k))
    else:
        b_spec = pl.BlockSpec((tk, tn), lambda j, i, k: (k, j))
    return pl.pallas_call(
        body, name=name, grid=(N // tn, M // tm, nk), in_specs=[a_spec, b_spec],
        out_specs=pl.BlockSpec((tm, tn), lambda j, i, k: (i, j)), out_shape=jax.ShapeDtypeStruct((M, N), out_dtype),
        scratch_shapes=[pltpu.VMEM((tm, tn) if nk > 1 else (8, LANES), F32)],
        compiler_params=_cparams(("parallel", "parallel", "arbitrary")))(a, b)


CONV_CB = 256


def _shift_down(x, k):
    if k == 0:
        return x
    return jnp.where(_iota2(x.shape, 0) >= k, pltpu.roll(x, k, 0), 0.0)


def _shift_up(x, k):
    if k == 0:
        return x
    t = x.shape[0]
    return jnp.where(_iota2(x.shape, 0) < t - k, pltpu.roll(x, t - k, 0), 0.0)


def _conv_pre(x, w, b):
    kk = w.shape[0]
    pre = x * w[kk - 1:kk, :]
    for k in range(kk - 1):
        pre = pre + _shift_down(x, kk - 1 - k) * w[k:k + 1, :]
    return pre if b is None else pre + b


def _conv_bwd_pre(x, w, dpre, dw_ref, db_ref):
    kk = w.shape[0]
    dx = dpre * w[kk - 1:kk, :]
    dw_ref[kk - 1:kk, :] = jnp.sum(dpre * x, axis=0, keepdims=True)
    for k in range(kk - 1):
        dx = dx + _shift_up(dpre, kk - 1 - k) * w[k:k + 1, :]
        dw_ref[k:k + 1, :] = jnp.sum(dpre * _shift_down(x, kk - 1 - k), axis=0, keepdims=True)
    if db_ref is not None:
        db_ref[...] = jnp.sum(dpre, axis=0, keepdims=True)
    return dx


def _dsilu(pre):
    sg = jax.nn.sigmoid(pre)
    return sg * (1.0 + pre * (1.0 - sg))


def _conv_silu_fwd(name, src, col0, w, b):
    T = src.shape[0]
    kk, C = w.shape
    cb = CONV_CB
    off = col0 // cb

    def body(*refs):
        x_ref, w_ref = refs[:2]
        b_val = refs[2][...] if b is not None else None
        refs[-1][...] = jax.nn.silu(_conv_pre(x_ref[...], w_ref[...], b_val))

    in_specs = [pl.BlockSpec((T, cb), lambda j: (0, off + j)), pl.BlockSpec((kk, cb), lambda j: (0, j))]
    args = [src, w]
    if b is not None:
        in_specs.append(pl.BlockSpec((1, cb), lambda j: (0, j)))
        args.append(b)
    return pl.pallas_call(body, name=name, grid=(C // cb,), in_specs=in_specs,
                          out_specs=pl.BlockSpec((T, cb), lambda j: (0, j)), out_shape=jax.ShapeDtypeStruct((T, C), F32),
                          compiler_params=_cparams(("parallel",)))(*args)


def _conv_silu_bwd(name, src, col0, w, b, dy, dx_dtype):
    T = src.shape[0]
    kk, C = w.shape
    cb = CONV_CB
    off = col0 // cb
    has_b = b is not None

    def body(*refs):
        x_ref, w_ref = refs[:2]
        pos = 2
        b_val = None
        if has_b:
            b_val = refs[pos][...]; pos += 1
        dy_ref = refs[pos]; pos += 1
        dx_ref, dw_ref = refs[pos], refs[pos + 1]
        db_ref = refs[pos + 2] if has_b else None
        x, wv = x_ref[...], w_ref[...]
        dpre = dy_ref[...] * _dsilu(_conv_pre(x, wv, b_val))
        dx_ref[...] = _conv_bwd_pre(x, wv, dpre, dw_ref, db_ref).astype(dx_ref.dtype)

    in_specs = [pl.BlockSpec((T, cb), lambda j: (0, off + j)), pl.BlockSpec((kk, cb), lambda j: (0, j))]
    args = [src, w]
    if has_b:
        in_specs.append(pl.BlockSpec((1, cb), lambda j: (0, j)))
        args.append(b)
    in_specs.append(pl.BlockSpec((T, cb), lambda j: (0, j)))
    args.append(dy)
    out_specs = [pl.BlockSpec((T, cb), lambda j: (0, j)), pl.BlockSpec((kk, cb), lambda j: (0, j))]
    out_shape = [jax.ShapeDtypeStruct((T, C), dx_dtype), jax.ShapeDtypeStruct((kk, C), F32)]
    if has_b:
        out_specs.append(pl.BlockSpec((1, cb), lambda j: (0, j)))
        out_shape.append(jax.ShapeDtypeStruct((1, C), F32))
    return pl.pallas_call(body, name=name, grid=(C // cb,), in_specs=in_specs, out_specs=out_specs, out_shape=out_shape,
                          compiler_params=_cparams(("parallel",)))(*args)


def _ffn_glu_fwd(name, up, w, b, out_dtype=F32):
    T = up.shape[0]
    kk = w.shape[0]
    cb = CONV_CB
    width = up.shape[1] // 2
    nblk = width // cb

    def body(g_ref, u_ref, wg_ref, wu_ref, bg_ref, bu_ref, o_ref):
        g = _conv_pre(g_ref[...], wg_ref[...], bg_ref[...])
        u = _conv_pre(u_ref[...], wu_ref[...], bu_ref[...])
        o_ref[...] = (jax.nn.silu(g) * u).astype(o_ref.dtype)

    lo, hi = (lambda j: (0, j)), (lambda j: (0, nblk + j))
    in_specs = [pl.BlockSpec((T, cb), lo), pl.BlockSpec((T, cb), hi), pl.BlockSpec((kk, cb), lo), pl.BlockSpec((kk, cb), hi),
                pl.BlockSpec((1, cb), lo), pl.BlockSpec((1, cb), hi)]
    return pl.pallas_call(body, name=name, grid=(nblk,), in_specs=in_specs, out_specs=pl.BlockSpec((T, cb), lo),
                          out_shape=jax.ShapeDtypeStruct((T, width), out_dtype),
                          compiler_params=_cparams(("parallel",)))(up, up, w, w, b, b)


def _ffn_glu_bwd(name, up, w, b, dact, dx_dtype):
    T = up.shape[0]
    kk = w.shape[0]
    cb = CONV_CB
    width = up.shape[1] // 2
    nblk = width // cb

    def body(g_ref, u_ref, wg_ref, wu_ref, bg_ref, bu_ref, d_ref, dg_ref, du_ref, dwg_ref, dwu_ref, dbg_ref, dbu_ref):
        xg, xu, wg, wu = g_ref[...], u_ref[...], wg_ref[...], wu_ref[...]
        g = _conv_pre(xg, wg, bg_ref[...])
        u = _conv_pre(xu, wu, bu_ref[...])
        d = d_ref[...].astype(F32)
        dg_ref[...] = _conv_bwd_pre(xg, wg, d * u * _dsilu(g), dwg_ref, dbg_ref).astype(dg_ref.dtype)
        du_ref[...] = _conv_bwd_pre(xu, wu, d * jax.nn.silu(g), dwu_ref, dbu_ref).astype(du_ref.dtype)

    lo, hi = (lambda j: (0, j)), (lambda j: (0, nblk + j))
    in_specs = [pl.BlockSpec((T, cb), lo), pl.BlockSpec((T, cb), hi), pl.BlockSpec((kk, cb), lo), pl.BlockSpec((kk, cb), hi),
                pl.BlockSpec((1, cb), lo), pl.BlockSpec((1, cb), hi), pl.BlockSpec((T, cb), lo)]
    out_specs = [pl.BlockSpec((T, cb), lo)] * 2 + [pl.BlockSpec((kk, cb), lo)] * 2 + [pl.BlockSpec((1, cb), lo)] * 2
    out_shape = ([jax.ShapeDtypeStruct((T, width), dx_dtype)] * 2 + [jax.ShapeDtypeStruct((kk, width), F32)] * 2
                 + [jax.ShapeDtypeStruct((1, width), F32)] * 2)
    return pl.pallas_call(body, name=name, grid=(nblk,), in_specs=in_specs, out_specs=out_specs, out_shape=out_shape,
                          compiler_params=_cparams(("parallel",)))(up, up, w, w, b, b, dact)


def _loss_head(y, target):
    T, D = y.shape
    tb = _tile(T, 256, 8)

    def body(y_ref, t_ref, dy_ref, l_ref):
        @pl.when(pl.program_id(0) == 0)
        def _():
            l_ref[...] = jnp.zeros_like(l_ref)

        err = y_ref[...] - t_ref[...]
        dy_ref[...] = err * (1.0 / D)
        l_ref[...] += jnp.sum(err * err, axis=0, keepdims=True) * (0.5 / D)

    spec = pl.BlockSpec((tb, D), lambda i: (i, 0))
    return pl.pallas_call(body, name="loss_head", grid=(T // tb,), in_specs=[spec, spec],
                          out_specs=[spec, pl.BlockSpec((1, D), lambda i: (0, 0))],
                          out_shape=[jax.ShapeDtypeStruct((T, D), F32), jax.ShapeDtypeStruct((1, D), F32)],
                          compiler_params=_cparams(("arbitrary",)))(y, target)


def _adamw_math(w, g, m, v):
    m = ADAM_B1 * m + (1.0 - ADAM_B1) * g
    v = ADAM_B2 * v + (1.0 - ADAM_B2) * jnp.square(g)
    m_hat = m / (1.0 - ADAM_B1 ** ADAM_STEP)
    v_hat = v / (1.0 - ADAM_B2 ** ADAM_STEP)
    return -ADAM_LR * (m_hat / (jnp.sqrt(v_hat) + ADAM_EPS) + ADAM_WD * w), m, v


def _adamw(name, w, g, m, v):
    A, R, C = w.shape
    rb = _tile(R, max(8, (1 << 19) // (C * 4) // 8 * 8), 8)

    def body(w_ref, g_ref, m_ref, v_ref, d_ref, mo_ref, vo_ref):
        d, mn, vn = _adamw_math(w_ref[...], g_ref[...], m_ref[...], v_ref[...])
        d_ref[...] = d
        mo_ref[...] = mn
        vo_ref[...] = vn

    spec = pl.BlockSpec((1, rb, C), lambda a, r: (a, r, 0))
    return pl.pallas_call(body, name=name, grid=(A, R // rb), in_specs=[spec] * 4, out_specs=[spec] * 3,
                          out_shape=[jax.ShapeDtypeStruct(w.shape, F32)] * 3,
                          compiler_params=_cparams(("parallel", "parallel")))(w, g, m, v)


def _adamw_small(parts, w, m, v):
    def body(p_ref, w_ref, m_ref, v_ref, g_ref, d_ref, mo_ref, vo_ref):
        g = p_ref[0]
        for i in range(1, N_DEV):
            g = g + p_ref[i]
        d, mn, vn = _adamw_math(w_ref[...], g, m_ref[...], v_ref[...])
        g_ref[...] = g
        d_ref[...] = d
        mo_ref[...] = mn
        vo_ref[...] = vn

    return pl.pallas_call(body, name="adamw_small", out_shape=[jax.ShapeDtypeStruct(w.shape, F32)] * 4,
                          compiler_params=_cparams())(parts, w, m, v)


def _add_blocks(name, a, b, out_dtype=F32):
    n, R, W = a.shape
    rb = _tile(R, 512, 8)

    def body(a_ref, b_ref, o_ref):
        o_ref[...] = (a_ref[...].astype(F32) + b_ref[...].astype(F32)).astype(o_ref.dtype)

    spec = pl.BlockSpec((1, rb, W), lambda i, r: (i, r, 0))
    return pl.pallas_call(body, name=name, grid=(n, R // rb), in_specs=[spec, spec], out_specs=spec,
                          out_shape=jax.ShapeDtypeStruct(a.shape, out_dtype),
                          compiler_params=_cparams(("parallel", "parallel")))(a, b)


def _pair_add(name, g, other, c, chip):
    _, R, W = g.shape
    rb = _tile(R, 256, 8)

    def body(s_ref, a_ref, b_ref, send_ref, own_ref):
        s = a_ref[0] + b_ref[0]
        send_ref[0] = s.astype(send_ref.dtype)

        @pl.when(pl.program_id(1) == s_ref[1])
        def _():
            own_ref[...] = s

    grid_spec = pltpu.PrefetchScalarGridSpec(
        num_scalar_prefetch=1, grid=(R // rb, 4),
        in_specs=[pl.BlockSpec((1, rb, W), lambda r, p, s_ref: (2 * p + s_ref[0], r, 0)),
                  pl.BlockSpec((1, rb, W), lambda r, p, s_ref: (p, r, 0))],
        out_specs=[pl.BlockSpec((1, rb, W), lambda r, p, s_ref: (p, r, 0)),
                   pl.BlockSpec((rb, W), lambda r, p, s_ref: (r, 0))])
    scalars = jnp.stack([c, chip]).astype(jnp.int32)
    return pl.pallas_call(body, name=name, grid_spec=grid_spec,
                          out_shape=[jax.ShapeDtypeStruct((4, R, W), MXU_DTYPE), jax.ShapeDtypeStruct((R, W), F32)],
                          compiler_params=_cparams(("parallel", "arbitrary")))(scalars, g, other)


def _sum4(name, own, parts):
    R, W = own.shape
    rb = _tile(R, 256, 8)

    def body(o_ref, p_ref, out_ref):
        out_ref[...] = ((o_ref[...] + p_ref[0].astype(F32)) + p_ref[1].astype(F32)) + p_ref[2].astype(F32)

    return pl.pallas_call(body, name=name, grid=(R // rb,),
                          in_specs=[pl.BlockSpec((rb, W), lambda r: (r, 0)), pl.BlockSpec((3, rb, W), lambda r: (0, r, 0))],
                          out_specs=pl.BlockSpec((rb, W), lambda r: (r, 0)), out_shape=jax.ShapeDtypeStruct((R, W), F32),
                          compiler_params=_cparams(("parallel",)))(own, parts)


MESH = pl.DeviceIdType.MESH
ANY = pl.BlockSpec(memory_space=pl.ANY)


def _place():
    return lax.axis_index("x"), lax.axis_index("y"), lax.axis_index("c")


def _other_chips(x, y):
    return [(1 - x, y), (x, 1 - y), (1 - x, 1 - y)]


def _all_gather(name, blocks):
    n = len(blocks)

    def body(*refs):
        x_refs, out_refs = refs[:n], refs[n:2 * n]
        send_sems, recv_sems, local_sems = refs[2 * n:]
        x, y, c = _place()
        me, sibling = (x, y, c), (x, y, 1 - c)
        chips = _other_chips(x, y)

        def slot(a, px, py, pc):
            return out_refs[a].at[4 * px + 2 * py + pc]

        def copy(a, k, blk, to, src=None):
            return pltpu.make_async_remote_copy(src_ref=slot(a, *blk) if src is None else src, dst_ref=slot(a, *blk),
                                                send_sem=send_sems.at[a, k], recv_sem=recv_sems.at[a, k],
                                                device_id=to, device_id_type=MESH)

        mine = [pltpu.make_async_copy(x_refs[a], slot(a, *me), local_sems.at[a]) for a in range(n)]
        for cp in mine:
            cp.start()
        first = []
        for j, chip in enumerate(chips):
            first += [copy(a, 1 + j, me, (*chip, c), src=x_refs[a]) for a in range(n)]
        first += [copy(a, 0, me, sibling, src=x_refs[a]) for a in range(n)]
        for cp in first:
            cp.start()
        passed = []
        for j, chip in enumerate(chips):
            for a in range(n):
                copy(a, 1 + j, (*chip, c), me).wait_recv()
                passed.append(copy(a, 4 + j, (*chip, c), sibling))
                passed[-1].start()
        for a in range(n):
            copy(a, 0, sibling, me).wait_recv()
        for j, chip in enumerate(chips):
            for a in range(n):
                copy(a, 4 + j, (*chip, 1 - c), me).wait_recv()
        for cp in first + passed:
            cp.wait_send()
        for cp in mine:
            cp.wait()

    return pl.pallas_call(body, name=name, in_specs=[ANY] * n, out_specs=[ANY] * n,
                          out_shape=[jax.ShapeDtypeStruct((N_DEV,) + b.shape, b.dtype) for b in blocks],
                          scratch_shapes=[pltpu.SemaphoreType.DMA((n, 7)), pltpu.SemaphoreType.DMA((n, 7)),
                                          pltpu.SemaphoreType.DMA((n,))])(*blocks)


def _routes_to_sibling(x, y, c):
    return [(2 * p + (1 - c), p, (x, y, 1 - c)) for p in range(4)]


def _routes_to_chips(x, y, c):
    return [(2 * px + py, j, (px, py, c)) for j, (px, py) in enumerate(_other_chips(x, y))]


def _routes_block_to_chips(x, y, c):
    me = 4 * x + 2 * y + c
    return [(me, me, (px, py, c)) for px, py in _other_chips(x, y)]


def _routes_blocks_to_sibling(x, y, c):
    return [(4 * px + 2 * py + c, 4 * px + 2 * py + c, (x, y, 1 - c)) for px, py in [(x, y)] + _other_chips(x, y)]


def _route_copies(routes, src_refs, land_refs, send_sems, recv_sems):
    x, y, c = _place()
    copies = []
    for a, (src, land) in enumerate(zip(src_refs, land_refs)):
        plan = routes(x, y, c)
        for k, (s, d, target) in enumerate(plan):
            i = a * len(plan) + k
            copies.append(pltpu.make_async_remote_copy(src_ref=src.at[s], dst_ref=land.at[d], send_sem=send_sems.at[i],
                                                       recv_sem=recv_sems.at[i], device_id=target, device_id_type=MESH))
    return copies


def _exchange(name, routes, n_routes, srcs, land_slots):
    n = len(srcs)

    def body(*refs):
        copies = _route_copies(routes, refs[:n], refs[n:2 * n], refs[2 * n], refs[2 * n + 1])
        for cp in copies:
            cp.start()
        for cp in copies:
            cp.wait_recv()
        for cp in copies:
            cp.wait_send()

    return pl.pallas_call(body, name=name, in_specs=[ANY] * n, out_specs=[ANY] * n,
                          out_shape=[jax.ShapeDtypeStruct((land_slots,) + s.shape[1:], s.dtype) for s in srcs],
                          scratch_shapes=[pltpu.SemaphoreType.DMA((n * n_routes,)), pltpu.SemaphoreType.DMA((n * n_routes,))])(*srcs)


HBM_SPEC = pl.BlockSpec(memory_space=pltpu.HBM)
SEM_SPEC = pl.BlockSpec(memory_space=pltpu.SEMAPHORE)
DATAFLOW = pltpu.SideEffectType.DATAFLOW_SIDE_EFFECTING


def _exchange_start(name, routes, n_routes, srcs, lands):
    n = len(srcs)
    in_place = lands is None
    bufs = list(srcs) + ([] if in_place else list(lands))
    nb = len(bufs)

    def body(*refs):
        src_refs = refs[:n]
        land_refs = src_refs if in_place else refs[n:nb]
        send_sems, recv_sems = refs[nb], refs[nb + 1]
        token = refs[-1]
        for cp in _route_copies(routes, src_refs, land_refs, send_sems, recv_sems):
            cp.start()
        token[...] = jnp.zeros_like(token)

    sems = [pltpu.SemaphoreType.DMA((n * n_routes,)), pltpu.SemaphoreType.DMA((n * n_routes,))]
    out = pl.pallas_call(
        body, name=name, in_specs=[HBM_SPEC] * nb,
        out_shape=sems + [pltpu.HBM(b.shape, b.dtype) for b in bufs] + [jax.ShapeDtypeStruct((8, LANES), F32)],
        out_specs=[SEM_SPEC, SEM_SPEC] + [HBM_SPEC] * nb + [pl.BlockSpec(memory_space=pltpu.VMEM)],
        input_output_aliases={i: 2 + i for i in range(nb)},
        compiler_params=pltpu.CompilerParams(has_side_effects=DATAFLOW))(
        *[pltpu.with_memory_space_constraint(b, pltpu.HBM) for b in bufs])
    return (out[0], out[1], list(out[2:2 + nb])), out[-1]


def _exchange_wait(name, routes, n_routes, n, started, after):
    send_sems, recv_sems, bufs = started
    nb = len(bufs)
    in_place = nb == n

    def body(*refs):
        src_refs = refs[:n]
        land_refs = src_refs if in_place else refs[n:nb]
        for cp in _route_copies(routes, src_refs, land_refs, refs[nb], refs[nb + 1]):
            cp.wait_send()
            cp.wait_recv()

    out = pl.pallas_call(
        body, name=name, in_specs=[HBM_SPEC] * nb + [SEM_SPEC, SEM_SPEC, ANY],
        out_shape=[pltpu.HBM(b.shape, b.dtype) for b in bufs], out_specs=[HBM_SPEC] * nb,
        input_output_aliases={i: i for i in range(nb)},
        compiler_params=pltpu.CompilerParams(has_side_effects=DATAFLOW))(*bufs, send_sems, recv_sems, after)
    return list(out[:n]) if in_place else list(out[n:])


def _reduce_scatter_finish(tag, gs, from_sibling, exchange_chips):
    x, y, c = _place()
    sums = [_pair_add(f"rs_add_{tag}_{i}", g, o, c, 2 * x + y) for i, (g, o) in enumerate(zip(gs, from_sibling))]
    got = exchange_chips([s[0] for s in sums])
    return [_sum4(f"rs_sum_{tag}_{i}", s[1], q) for i, (s, q) in enumerate(zip(sums, got))]


def _reduce_scatter(tag, gs):
    from_sibling = _exchange(f"rs_swap_{tag}", _routes_to_sibling, 4, gs, 4)
    return _reduce_scatter_finish(tag, gs, from_sibling, lambda ps: _exchange(f"rs_chips_{tag}", _routes_to_chips, 3, ps, 3))


def _flat_rows(n_elems):
    return -(-n_elems // (FLAT_W * 16)) * 16


def _pack(arrays, dtype):
    flat = jnp.concatenate([a.reshape(-1).astype(dtype) for a in arrays])
    rows = _flat_rows(flat.shape[0])
    flat = jnp.pad(flat, (0, rows * FLAT_W - flat.shape[0]))
    return flat.reshape(rows, FLAT_W)


def _unpack(flat, shapes, lead=()):
    flat = flat.reshape(lead + (-1,))
    out, pos = [], 0
    for s in shapes:
        n = math.prod(s)
        out.append(flat[..., pos:pos + n].reshape(lead + tuple(s)))
        pos += n
    return out


def _shard_to_send(name, shard):
    if name == "ffn_w_up":
        shard = jnp.pad(shard, ((0, 0), (0, FFN_SHARD_PAD - FFN_SHARD)))
    return shard.astype(MXU_DTYPE)


def _whole_from_gathered(name, g):
    if name == "w_in":
        return _pad_in_proj(jnp.concatenate([g[d] for d in range(N_DEV)], axis=1))
    if name in ("w_br_gdn", "w_br_gla", "ffn_w_up"):
        return jnp.transpose(g, (1, 0, 2)).reshape(g.shape[1], N_DEV * g.shape[2])
    if name == "ffn_w_down":
        blocks = g.reshape(N_DEV // 2, FFN_SHARD, g.shape[2])
        blocks = jnp.pad(blocks, ((0, 0), (0, FFN_SHARD_PAD - FFN_SHARD), (0, 0)))
        return blocks.reshape(FFN_PAD, g.shape[2])
    return g.reshape(N_DEV * g.shape[1], g.shape[2])


def _slots_from_whole(name, gw):
    if name == "w_in":
        full = _unpad_in_proj(gw)
        cs = IN_DIM // N_DEV
        return jnp.stack([full[:, d * cs:(d + 1) * cs] for d in range(N_DEV)])
    if name in ("w_br_gdn", "w_br_gla"):
        return jnp.transpose(gw.reshape(gw.shape[0], N_DEV, gw.shape[1] // N_DEV), (1, 0, 2))
    if name == "ffn_w_up":
        return jnp.transpose(gw.reshape(gw.shape[0], N_DEV, FFN_SHARD_PAD)[:, :, :FFN_SHARD], (1, 0, 2))
    if name == "ffn_w_down":
        blocks = gw.reshape(N_DEV // 2, FFN_SHARD_PAD, gw.shape[1])[:, :FFN_SHARD]
        return blocks.reshape(N_DEV, FFN_SHARD // 2, gw.shape[1])
    return gw.reshape(N_DEV, gw.shape[0] // N_DEV, gw.shape[1])


def _ffn_pad_cols(a):
    lead = a.shape[:-1]
    nblk = a.shape[-1] // FFN_SHARD
    a = a.reshape(lead + (nblk, FFN_SHARD))
    a = jnp.pad(a, [(0, 0)] * (len(lead) + 1) + [(0, FFN_SHARD_PAD - FFN_SHARD)])
    return a.reshape(lead + (nblk * FFN_SHARD_PAD,))


def _ffn_unpad_cols(a):
    lead = a.shape[:-1]
    nblk = a.shape[-1] // FFN_SHARD_PAD
    return a.reshape(lead + (nblk, FFN_SHARD_PAD))[..., :FFN_SHARD].reshape(lead + (nblk * FFN_SHARD,))


def _pad_in_proj(w):
    starts, pos = {}, 0
    for n, width in IN_SPLITS:
        starts[n] = (pos, width)
        pos += width
    cols, at = [], 0
    for _, off, width, pieces in PAD_SEGS:
        assert off == at
        used = 0
        for ref_name, lane in pieces:
            assert lane == used
            s, wd = starts[ref_name]
            cols.append(w[:, s:s + wd])
            used += wd
        if used < width:
            cols.append(jnp.zeros((w.shape[0], width - used), w.dtype))
        at += width
    if at < IN_PAD:
        cols.append(jnp.zeros((w.shape[0], IN_PAD - at), w.dtype))
    return jnp.concatenate(cols, axis=1)


def _unpad_in_proj(wp):
    where = {}
    for _, off, _, pieces in PAD_SEGS:
        for ref_name, lane in pieces:
            where[ref_name] = off + lane
    return jnp.concatenate([wp[:, where[n]:where[n] + width] for n, width in IN_SPLITS], axis=1)


def _lane_pad(a, width=LANES):
    return jnp.pad(a, ((0, 0), (0, width - a.shape[1])))


def _seg_blk(h, name, rows):
    off, width = SEG[name]
    return (h, rows, width, off // width)


def _ln_both(xs_, ps_):
    (y,) = _ln_fn(xs_, ps_)
    return (y, y)


def _layer_fwd(l, x, x_mx, W, sp, mid=None):
    T = x.shape[0]
    n64, ngla, ntok = T // SSD_CHUNK, T // GLA_BLOCK, T // 256
    h = _mm(f"in_proj_{l}", x_mx, W["w_in_pad"])
    xbc = _conv_silu_fwd(f"ssd_conv_{l}", h, SEG["xbc"][0], sp["ssd_conv_w"], sp["ssd_conv_b"])
    gqkv = _conv_silu_fwd(f"gdn_conv_{l}", h, SEG["gqkv"][0], sp["gdn_conv_w"], None)

    ssd_in = [(xbc, SSD_CHUNK, SSD_XBC, 0), _seg_blk(h, "dt", SSD_CHUNK), _seg_blk(h, "z", SSD_CHUNK)]
    ssd_p = [sp["ssd_dt_bias"], sp["ssd_a_log"], sp["ssd_d"], sp["ssd_norm_w"]]
    o_ssd, ssd_states = _chain_fwd(f"ssd_fwd_{l}", _ssd_chunk, n64, ssd_in, ssd_p, [(SSD_CHUNK, SSD_INNER, MXU_DTYPE)],
                                   (SSD_STATE, SSD_INNER))
    o_gdn, gdn_saved = _gdn_forward(str(l), gqkv, h, sp)
    gla_in = [_seg_blk(h, "lqkv", GLA_BLOCK), _seg_blk(h, "lglr", GLA_BLOCK), _seg_blk(h, "lr", GLA_BLOCK)]
    gla_p = [jnp.pad(sp["gla_gate_w2"], ((0, LANES - GLA_RANK), (0, 0))), sp["gla_gate_b"], sp["gla_norm_w"]]
    o_gla, gla_states = _chain_fwd(f"gla_fwd_{l}", _gla_block, ngla, gla_in, gla_p, [(GLA_BLOCK, GLA_V, MXU_DTYPE)],
                                   (GLA_VAL_DIM, GLA_K))
    y_ssd = _mm(f"br_ssd_{l}", o_ssd, W["w_br_ssd"])
    y_gdn = _mm(f"br_gdn_{l}", o_gdn, W["w_br_gdn"])
    y_gla = _mm(f"br_gla_{l}", o_gla, W["w_br_gla"])
    merge_in = [_seg_blk(h, "gates", 256), (y_ssd, 256, D_MODEL, 0), (y_gdn, 256, D_MODEL, 0), (y_gla, 256, D_MODEL, 0)]
    (mix,) = _chain_fwd(f"merge_{l}", _merge_fn, ntok, merge_in, [], [(256, D_MODEL, MXU_DTYPE)])
    r1 = _mm(f"out_proj_{l}", mix, W["w_out"])
    ln1_p = [sp["ln1_g"], sp["ln1_b"]]
    if mid is not None:
        ln1_p[0] = ln1_p[0] + mid(o_gla)[0:1, 0:1]
    both = [(256, D_MODEL, F32), (256, D_MODEL, MXU_DTYPE)]
    x1, x1_mx = _chain_fwd(f"ln1_{l}", _ln_both, ntok, [(x, 256, D_MODEL, 0), (r1, 256, D_MODEL, 0)], ln1_p, both)
    up = _mm(f"ffn_up_{l}", x1_mx, W["ffn_w_up"])
    act = _ffn_glu_fwd(f"ffn_glu_{l}", up, sp["ffn_conv_w_pad"], sp["ffn_conv_b_pad"], MXU_DTYPE)
    r2 = _mm(f"ffn_down_{l}", act, W["ffn_w_down"])
    ln2_p = [sp["ln2_g"], sp["ln2_b"]]
    x2, x2_mx = _chain_fwd(f"ln2_{l}", _ln_both, ntok, [(x1, 256, D_MODEL, 0), (r2, 256, D_MODEL, 0)], ln2_p, both)
    saved = dict(x=x, x_mx=x_mx, h=h, xbc=xbc, gqkv=gqkv, ssd_in=ssd_in, ssd_p=ssd_p, ssd_states=ssd_states,
                 gdn=gdn_saved, gla_in=gla_in, gla_p=gla_p, gla_states=gla_states, o_ssd=o_ssd,
                 o_gdn=o_gdn, o_gla=o_gla, merge_in=merge_in, mix=mix, r1=r1, ln1_p=ln1_p, x1=x1, x1_mx=x1_mx, up=up, act=act,
                 r2=r2, ln2_p=ln2_p)
    return x2, x2_mx, saved


def _layer_bwd(l, dx2, W, sp, sv, first=None, mid=None):
    T = dx2.shape[0]
    n64, ngla, ntok = T // SSD_CHUNK, T // GLA_BLOCK, T // 256
    bf = MXU_DTYPE
    gw, gs = {}, {}
    ln2_p = list(sv["ln2_p"])
    if first is not None:
        ln2_p[0] = ln2_p[0] + first[0:1, 0:1]
    (dx1_a, dr2), (gs["ln2_g"], gs["ln2_b"]) = _chain_bwd(
        f"ln2_bwd_{l}", _ln_fn, ntok, [(sv["x1"], 256, D_MODEL, 0), (sv["r2"], 256, D_MODEL, 0)], ln2_p,
        [(dx2, 256, D_MODEL)], dx_dtypes=[F32, bf])
    gw["ffn_w_down"] = _mm(f"ffn_down_dw_{l}", sv["act"], dr2, "tn")
    dact = _mm(f"ffn_down_dx_{l}", dr2, W["ffn_w_down"], "nt")
    dg, du, dwg, dwu, dbg, dbu = _ffn_glu_bwd(f"ffn_glu_bwd_{l}", sv["up"], sp["ffn_conv_w_pad"], sp["ffn_conv_b_pad"], dact, bf)
    gs["ffn_conv_w"] = _ffn_unpad_cols(jnp.concatenate([dwg, dwu], axis=1))
    gs["ffn_conv_b"] = _ffn_unpad_cols(jnp.concatenate([dbg, dbu], axis=1))
    dup = jnp.concatenate([dg, du], axis=1)
    gw["ffn_w_up"] = _mm(f"ffn_up_dw_{l}", sv["x1_mx"], dup, "tn")
    dx1_b = _mm(f"ffn_up_dx_{l}", dup, W["ffn_w_up"], "nt", tn=1024, tk=1024)
    ln1_p = list(sv["ln1_p"])
    if mid is not None:
        ln1_p[0] = ln1_p[0] + mid(dx1_b)[0:1, 0:1]
    (dx_a, dr1), (gs["ln1_g"], gs["ln1_b"]) = _chain_bwd(
        f"ln1_bwd_{l}", _ln_sum_fn, ntok, [(sv["x"], 256, D_MODEL, 0), (sv["r1"], 256, D_MODEL, 0)], ln1_p,
        [(dx1_a, 256, D_MODEL), (dx1_b, 256, D_MODEL)], dx_dtypes=[F32, bf])
    gw["w_out"] = _mm(f"out_proj_dw_{l}", sv["mix"], dr1, "tn")
    dmix = _mm(f"out_proj_dx_{l}", dr1, W["w_out"], "nt")
    (dgates, dy_ssd, dy_gdn, dy_gla), _ = _chain_bwd(f"merge_bwd_{l}", _merge_fn, ntok, sv["merge_in"], [],
                                                     [(dmix, 256, D_MODEL)], dx_dtypes=[bf, bf, bf, bf])
    gw["w_br_ssd"] = _mm(f"br_ssd_dw_{l}", sv["o_ssd"], dy_ssd, "tn")
    gw["w_br_gdn"] = _mm(f"br_gdn_dw_{l}", sv["o_gdn"], dy_gdn, "tn")
    gw["w_br_gla"] = _mm(f"br_gla_dw_{l}", sv["o_gla"], dy_gla, "tn")
    do_ssd = _mm(f"br_ssd_dx_{l}", dy_ssd, W["w_br_ssd"], "nt")
    do_gdn = _mm(f"br_gdn_dx_{l}", dy_gdn, W["w_br_gdn"], "nt")
    do_gla = _mm(f"br_gla_dx_{l}", dy_gla, W["w_br_gla"], "nt")

    (dxbc, ddt, dz), dps = _chain_bwd(f"ssd_bwd_{l}", _ssd_chunk, n64, sv["ssd_in"], sv["ssd_p"],
                                      [(do_ssd, SSD_CHUNK, SSD_INNER)], sprev=sv["ssd_states"], dx_dtypes=[F32, bf, bf])
    gs["ssd_dt_bias"], gs["ssd_a_log"], gs["ssd_d"], gs["ssd_norm_w"] = dps
    dgqkv, dgab, dgg, gs["gdn_a_log"], gs["gdn_dt_bias"], gs["gdn_norm_w"] = _gdn_backward(str(l), do_gdn, sv["gdn"], bf)
    (dlqkv, dlglr, dlr), dps = _chain_bwd(f"gla_bwd_{l}", _gla_block, ngla, sv["gla_in"], sv["gla_p"],
                                          [(do_gla, GLA_BLOCK, GLA_V)], sprev=sv["gla_states"], dx_dtypes=[bf, bf, bf])
    gs["gla_gate_w2"], gs["gla_gate_b"], gs["gla_norm_w"] = dps[0][:GLA_RANK], dps[1], dps[2]
    dxbc_pre, gs["ssd_conv_w"], gs["ssd_conv_b"] = _conv_silu_bwd(
        f"ssd_conv_bwd_{l}", sv["h"], SEG["xbc"][0], sp["ssd_conv_w"], sp["ssd_conv_b"], dxbc, bf)
    dgqkv_pre, gs["gdn_conv_w"] = _conv_silu_bwd(f"gdn_conv_bwd_{l}", sv["h"], SEG["gqkv"][0], sp["gdn_conv_w"], None, dgqkv, bf)
    pieces = dict(gates=dgates, xbc=dxbc_pre, gqkv=dgqkv_pre, z=dz, lqkv=dlqkv, gg=dgg, lr=dlr, dt=ddt, gab=dgab, lglr=dlglr)
    cols = [pieces[name] for name, _, _, _ in PAD_SEGS]
    cols.append(jnp.zeros((T, IN_PAD - PAD_SEGS[-1][1] - PAD_SEGS[-1][2]), bf))
    dh = jnp.concatenate(cols, axis=1)
    gw["w_in_pad"] = _mm(f"in_proj_dw_{l}", sv["x_mx"], dh, "tn")
    dx_b = _mm(f"in_proj_dx_{l}", dh, W["w_in_pad"], "nt", tn=1024, tk=1024)
    dx = _add_blocks(f"dx_add_{l}", dx_a[None], dx_b[None])[0]
    return dx, gw, gs


def _ln_sum_fn(xs_, ps_):
    (y,) = _ln_fn(xs_, ps_)
    return (y, y)


def _small_2d(name, a):
    return a.reshape(1, -1) if a.ndim == 1 else a


def kernel(x, w_in, ssd_conv_w, ssd_conv_b, ssd_dt_bias, ssd_a_log, ssd_d, ssd_norm_w, gdn_conv_w, gdn_a_log, gdn_dt_bias, gdn_norm_w, gla_gate_w2, gla_gate_b, gla_norm_w, w_br_ssd, w_br_gdn, w_br_gla, w_out, ln1_g, ln1_b, ffn_w_up, ffn_conv_w, ffn_conv_b, ffn_w_down, ln2_g, ln2_b, loss_target, m_w_in, m_ssd_conv_w, m_ssd_conv_b, m_ssd_dt_bias, m_ssd_a_log, m_ssd_d, m_ssd_norm_w, m_gdn_conv_w, m_gdn_a_log, m_gdn_dt_bias, m_gdn_norm_w, m_gla_gate_w2, m_gla_gate_b, m_gla_norm_w, m_w_br_ssd, m_w_br_gdn, m_w_br_gla, m_w_out, m_ln1_g, m_ln1_b, m_ffn_w_up, m_ffn_conv_w, m_ffn_conv_b, m_ffn_w_down, m_ln2_g, m_ln2_b, v_w_in, v_ssd_conv_w, v_ssd_conv_b, v_ssd_dt_bias, v_ssd_a_log, v_ssd_d, v_ssd_norm_w, v_gdn_conv_w, v_gdn_a_log, v_gdn_dt_bias, v_gdn_norm_w, v_gla_gate_w2, v_gla_gate_b, v_gla_norm_w, v_w_br_ssd, v_w_br_gdn, v_w_br_gla, v_w_out, v_ln1_g, v_ln1_b, v_ffn_w_up, v_ffn_conv_w, v_ffn_conv_b, v_ffn_w_down, v_ln2_g, v_ln2_b):
    args = locals()
    w = {n: args[n] for n in WEIGHTS}
    m = {n: args["m_" + n] for n in WEIGHTS}
    v = {n: args["v_" + n] for n in WEIGHTS}
    dev = 4 * lax.axis_index("x") + 2 * lax.axis_index("y") + lax.axis_index("c")
    xl = x[0]
    tgt = loss_target[0]

    sm_flat = _pack([w[n] for n in SMALL_SHARDED], F32)
    nbig = len(BIG)

    def whole_weights(got):
        Wl = {n: _whole_from_gathered(n, g) for n, g in zip(BIG, got)}
        Wl["w_in_pad"] = Wl.pop("w_in")
        return Wl

    zones = [lax.dynamic_update_slice_in_dim(lax.empty((N_DEV,) + s.shape, s.dtype), s[None], dev, axis=0)
             for s in (_shard_to_send(n, w[n][1]) for n in BIG)]
    gather1, gather1_token = _exchange_start("gather_w_1_chips_start", _routes_block_to_chips, 3, zones, None)
    got0 = _all_gather("gather_w_0", [_shard_to_send(n, w[n][0]) for n in BIG] + [sm_flat])
    sm_gathered = got0[-1]
    W = [whole_weights(got0[:nbig]), None]
    sm_shards = _unpack(sm_gathered, [w[n].shape for n in SMALL_SHARDED], lead=(N_DEV,))
    whole = dict(w)
    for n, s in zip(SMALL_SHARDED, sm_shards):
        whole[n] = jnp.transpose(s, (1, 2, 0, 3)).reshape(s.shape[1], s.shape[2], N_DEV * s.shape[3])
    SP = [{n: _small_2d(n, whole[n][l]) for n in SMALL} for l in range(DEPTH)]
    for sp in SP:
        sp["ffn_conv_w_pad"] = _ffn_pad_cols(sp["ffn_conv_w"])
        sp["ffn_conv_b_pad"] = _ffn_pad_cols(sp["ffn_conv_b"])

    held = {}

    def gather1_mid(mixed):
        zones = _exchange_wait("gather_w_1_chips_wait", _routes_block_to_chips, 3, nbig, gather1, mixed)
        held["gather1"], token = _exchange_start("gather_w_1_sibling_start", _routes_blocks_to_sibling, 4, zones, None)
        return token

    saved = [None] * DEPTH
    act, act_mx, saved[0] = _layer_fwd(0, xl, (xl + gather1_token[0, 0]).astype(MXU_DTYPE), W[0], SP[0], mid=gather1_mid)
    W[1] = whole_weights(_exchange_wait("gather_w_1_sibling_wait", _routes_blocks_to_sibling, 4, nbig, held["gather1"], act))
    act, act_mx, saved[1] = _layer_fwd(1, act, act_mx, W[1], SP[1])
    dy, loss_parts = _loss_head(act, tgt)
    loss = lax.psum(jnp.sum(loss_parts), ("x", "y", "c"))

    def slots_of(gw):
        gw["w_in"] = gw.pop("w_in_pad")
        return [_slots_from_whole(n, gw[n]) for n in BIG]

    grads = {}
    GS, red = [None] * DEPTH, [None] * DEPTH
    dy, gw, GS[1] = _layer_bwd(1, dy, W[1], SP[1], saved[1])
    slots1 = slots_of(gw)
    swap1, swap1_token = _exchange_start("rs_swap_1_start", _routes_to_sibling, 4, slots1,
                                         [lax.empty((4,) + s.shape[1:], s.dtype) for s in slots1])

    def reduce1_mid(ffn_done):
        from_sibling = _exchange_wait("rs_swap_1_wait", _routes_to_sibling, 4, nbig, swap1, ffn_done)
        held["sums1"] = [_pair_add(f"rs_add_1_{i}", g, o, lax.axis_index("c"), 2 * lax.axis_index("x") + lax.axis_index("y"))
                         for i, (g, o) in enumerate(zip(slots1, from_sibling))]
        partials = [s[0] for s in held["sums1"]]
        held["chips1"], token = _exchange_start("rs_chips_1_start", _routes_to_chips, 3, partials,
                                                [lax.empty((3,) + p.shape[1:], p.dtype) for p in partials])
        return token

    dy, gw, GS[0] = _layer_bwd(0, dy, W[0], SP[0], saved[0], first=swap1_token, mid=reduce1_mid)
    got1 = _exchange_wait("rs_chips_1_wait", _routes_to_chips, 3, nbig, held["chips1"], dy)
    red[1] = [_sum4(f"rs_sum_1_{i}", s[1], q) for i, (s, q) in enumerate(zip(held["sums1"], got1))]
    red[0] = _reduce_scatter("0", slots_of(gw))
    grad_x = dy[None]
    for i, n in enumerate(BIG):
        grads[n] = jnp.stack([red[l][i] for l in range(DEPTH)])

    small_shapes = [whole[n].shape for n in SMALL]
    gs_flat = _pack([jnp.stack([GS[l][n].reshape(whole[n].shape[1:]) for l in range(DEPTH)]) for n in SMALL], F32)
    (gs_all,) = _all_gather("gather_small_grads", [gs_flat])

    def mine(n, a):
        if n in SMALL_SHARDED:
            cs = a.shape[-1] // N_DEV
            return lax.dynamic_slice_in_dim(a, dev * cs, cs, axis=a.ndim - 1)
        return a

    m_whole, v_whole = {}, {}
    for n in SMALL:
        if n in SMALL_SHARDED:
            cs = w[n].shape[-1]
            zeros = jnp.zeros(whole[n].shape, F32)
            m_whole[n] = lax.dynamic_update_slice_in_dim(zeros, m[n], dev * cs, axis=2)
            v_whole[n] = lax.dynamic_update_slice_in_dim(zeros, v[n], dev * cs, axis=2)
        else:
            m_whole[n], v_whole[n] = m[n], v[n]
    outs = _adamw_small(gs_all, _pack([whole[n] for n in SMALL], F32), _pack([m_whole[n] for n in SMALL], F32),
                        _pack([v_whole[n] for n in SMALL], F32))
    g_s, d_s, m_s, v_s = [_unpack(o, small_shapes) for o in outs]
    delta, new_m, new_v = {}, {}, {}
    for i, n in enumerate(SMALL):
        grads[n], delta[n], new_m[n], new_v[n] = mine(n, g_s[i]), mine(n, d_s[i]), mine(n, m_s[i]), mine(n, v_s[i])
    for n in BIG:
        delta[n], new_m[n], new_v[n] = _adamw(f"adamw_{n}", w[n], grads[n], m[n], v[n])

    return (loss, grad_x, *[grads[n] for n in WEIGHTS], *[delta[n] for n in WEIGHTS], *[new_m[n] for n in WEIGHTS],
            *[new_v[n] for n in WEIGHTS])
```

```python
import functools
import math

import jax
import jax.numpy as jnp
from jax import lax
from jax.experimental import pallas as pl
from jax.experimental.pallas import tpu as pltpu

F32 = jnp.float32
MXU_DTYPE = jnp.bfloat16
HI = lax.Precision.HIGHEST

N_DEV = 8
D_MODEL = 1024
DEPTH = 2
SSD_HEADS, SSD_HEAD_DIM, SSD_INNER, SSD_GROUPS, SSD_STATE, SSD_CHUNK = 16, 64, 1024, 2, 128, 64
SSD_XBC = SSD_INNER + 2 * SSD_GROUPS * SSD_STATE
GDN_HEADS, GDN_HEAD_DIM, GDN_WIDTH, GDN_CHUNK = 4, 128, 512, 64
GLA_HEADS, GLA_KEY_DIM, GLA_VAL_DIM, GLA_K, GLA_V, GLA_RANK, GLA_CHUNK = 4, 64, 128, 256, 512, 16, 16
GLA_BLOCK = 128
GLA_NORMALIZER = 16.0
FFN_DIM = 2816
FFN_HALF = FFN_DIM // 8
FFN_HALF_PAD = 384
FFN_UP_PAD = 16 * FFN_HALF_PAD
FFN_PAD = FFN_UP_PAD // 2
ALPHA = (2 * DEPTH) ** 0.25
LN_EPS = 1e-5
RMS_EPS = 1e-6
ADAM_LR, ADAM_B1, ADAM_B2, ADAM_EPS, ADAM_WD, ADAM_STEP = 0.001, 0.9, 0.999, 1e-08, 0.01, 10
LANES = 128
NEG_BIG = -1e30
VMEM_LIMIT = 56 * 1024 * 1024

IN_SPLITS = (("z", 1024), ("xbc", 1536), ("dt", 16), ("gqkv", 1536), ("ga", 4), ("gb", 4), ("gg", 512),
             ("lqkv", 1024), ("lglr", 16), ("lr", 512), ("gates", 3072))
IN_DIM = sum(w for _, w in IN_SPLITS)
PAD_SEGS = (("gates", 0, 3072, (("gates", 0),)), ("xbc", 3072, 1536, (("xbc", 0),)),
            ("gqkv", 4608, 1536, (("gqkv", 0),)), ("z", 6144, 1024, (("z", 0),)),
            ("lqkv", 7168, 1024, (("lqkv", 0),)), ("gg", 8192, 512, (("gg", 0),)), ("lr", 8704, 512, (("lr", 0),)),
            ("dt", 9216, 128, (("dt", 0),)), ("gab", 9344, 128, (("ga", 0), ("gb", 4))), ("lglr", 9472, 128, (("lglr", 0),)))
IN_PAD = 9728
SEG = {name: (off, width) for name, off, width, _ in PAD_SEGS}

BIG = ("w_in", "w_br_ssd", "w_br_gdn", "w_br_gla", "w_out", "ffn_w_up", "ffn_w_down")
COL_SHARDED = ("w_in", "w_br_gdn", "w_br_gla", "ffn_w_up")
SMALL_SHARDED = ("ssd_conv_w", "gdn_conv_w", "gla_gate_w2", "ffn_conv_w")
WEIGHTS = ("w_in", "ssd_conv_w", "ssd_conv_b", "ssd_dt_bias", "ssd_a_log", "ssd_d", "ssd_norm_w", "gdn_conv_w",
           "gdn_a_log", "gdn_dt_bias", "gdn_norm_w", "gla_gate_w2", "gla_gate_b", "gla_norm_w", "w_br_ssd", "w_br_gdn",
           "w_br_gla", "w_out", "ln1_g", "ln1_b", "ffn_w_up", "ffn_conv_w", "ffn_conv_b", "ffn_w_down", "ln2_g", "ln2_b")
SMALL = tuple(n for n in WEIGHTS if n not in BIG)
FLAT_W = 512


def _cparams(sem=None):
    kw = dict(vmem_limit_bytes=VMEM_LIMIT)
    if sem is not None:
        kw["dimension_semantics"] = sem
    return pltpu.CompilerParams(**kw)


_DIMS = {"nn": (((1,), (0,)), ((), ())), "nt": (((1,), (1,)), ((), ())), "tn": (((0,), (0,)), ((), ()))}


def _dot(a, b, dims="nn"):
    if MXU_DTYPE == F32:
        return lax.dot_general(a.astype(F32), b.astype(F32), _DIMS[dims], precision=HI, preferred_element_type=F32)
    return lax.dot_general(a.astype(MXU_DTYPE), b.astype(MXU_DTYPE), _DIMS[dims], preferred_element_type=F32)


def _dot_hi(a, b, dims="nn"):
    return lax.dot_general(a.astype(F32), b.astype(F32), _DIMS[dims], precision=HI, preferred_element_type=F32)


def _iota2(shape, axis):
    return lax.broadcasted_iota(jnp.int32, shape, axis)


def _tril(n, strict=False):
    r, c = _iota2((n, n), 0), _iota2((n, n), 1)
    return (r > c) if strict else (r >= c)


def _raw_dot(a, b, dims):
    return lax.dot_general(a, b, _DIMS[dims], preferred_element_type=F32)


def _dot_x3(a, b, dims="nn"):
    if MXU_DTYPE == F32:
        return _dot_hi(a, b, dims)
    ah, bh = a.astype(jnp.bfloat16), b.astype(jnp.bfloat16)
    al, bl = (a - ah.astype(F32)).astype(jnp.bfloat16), (b - bh.astype(F32)).astype(jnp.bfloat16)
    return _raw_dot(ah, bh, dims) + (_raw_dot(ah, bl, dims) + _raw_dot(al, bh, dims))


def _exact_dot(mask, b, dims, mask_first):
    if MXU_DTYPE == F32:
        return _dot_hi(mask, b, dims) if mask_first else _dot_hi(b, mask, dims)
    m = mask.astype(jnp.bfloat16)
    b1 = b.astype(jnp.bfloat16)
    r1 = b - b1.astype(F32)
    b2 = r1.astype(jnp.bfloat16)
    b3 = (r1 - b2.astype(F32)).astype(jnp.bfloat16)
    if mask_first:
        return _raw_dot(m, b1, dims) + (_raw_dot(m, b2, dims) + _raw_dot(m, b3, dims))
    return _raw_dot(b1, m, dims) + (_raw_dot(b2, m, dims) + _raw_dot(b3, m, dims))


@jax.custom_vjp
def _mask_left(mask, b):
    return _exact_dot(mask, b, "nn", True)


_mask_left.defvjp(lambda mask, b: (_mask_left(mask, b), mask),
                  lambda mask, d: (jnp.zeros_like(mask), _exact_dot(mask, d, "tn", True)))


@jax.custom_vjp
def _mask_right(a, mask):
    return _exact_dot(mask, a, "nn", False)


_mask_right.defvjp(lambda a, mask: (_mask_right(a, mask), mask),
                   lambda mask, d: (_exact_dot(mask, d, "nt", False), jnp.zeros_like(mask)))


@jax.custom_vjp
def _unit_lower_inverses(mats):
    n = mats[0].shape[0]
    eye = (_iota2((n, n), 0) == _iota2((n, n), 1)).astype(F32)
    xs = [eye - a for a in mats]
    ps = list(mats)
    k = 2
    while k < n:
        ps = [_dot_x3(p, p) for p in ps]
        xs = [x + _dot_x3(x, p) for x, p in zip(xs, ps)]
        k *= 2
    return xs


def _unit_lower_inverses_fwd(mats):
    ts = _unit_lower_inverses(mats)
    return ts, ts


def _unit_lower_inverses_bwd(ts, dts):
    mids = [_dot_x3(t, d, "tn") for t, d in zip(ts, dts)]
    return ([-_dot_x3(m, t, "nt") for m, t in zip(mids, ts)],)


_unit_lower_inverses.defvjp(_unit_lower_inverses_fwd, _unit_lower_inverses_bwd)


def _ssd_chunk(xs_, ps_, s_t):
    xbc, dtraw, z = xs_
    dt_bias, a_log, d_skip, norm_w = ps_
    L = xbc.shape[0]
    H, P, N, G = SSD_HEADS, SSD_HEAD_DIM, SSD_STATE, SSD_GROUPS
    W = SSD_INNER // G
    xs = xbc[:, :SSD_INNER]
    bm = xbc[:, SSD_INNER:SSD_INNER + G * N]
    cm = xbc[:, SSD_INNER + G * N:]
    dt = jax.nn.softplus(dtraw[:, :H] + dt_bias)
    a = dt * (-jnp.exp(a_log))
    causal = _tril(L)
    a_cs = _mask_left(causal.astype(F32), a)
    expand = (_iota2((H, SSD_INNER), 1) // P == _iota2((H, SSD_INNER), 0)).astype(F32)
    wide = _mask_right(jnp.concatenate([a_cs, dt, jnp.broadcast_to(d_skip, (L, H))], axis=0), expand)
    a_cs_x, dt_x, d_x = wide[:L], wide[L:2 * L], wide[2 * L:]
    a_end_x = a_cs_x[L - 1:L, :]
    a_cs_t, dt_t = a_cs.T, dt.T
    cb = [_dot(cm[:, g * N:(g + 1) * N], bm[:, g * N:(g + 1) * N], "nt") for g in range(G)]
    decay = [jnp.exp(jnp.where(causal, a_cs[:, h:h + 1] - a_cs_t[h:h + 1, :], NEG_BIG)) * dt_t[h:h + 1, :] for h in range(H)]
    ws = [cb[h // (H // G)] * decay[h] for h in range(H)]
    y = jnp.concatenate([_dot(ws[h], xs[:, h * P:(h + 1) * P]) for h in range(H)], axis=1)
    y_in = jnp.concatenate([_dot(cm[:, g * N:(g + 1) * N], s_t[:, g * W:(g + 1) * W]) for g in range(G)], axis=1)
    y = y + y_in * jnp.exp(a_cs_x) + d_x * xs
    xw = xs * (jnp.exp(a_end_x - a_cs_x) * dt_x)
    st = jnp.concatenate([_dot(bm[:, g * N:(g + 1) * N], xw[:, g * W:(g + 1) * W], "tn") for g in range(G)], axis=1)
    s_new = s_t * jnp.exp(a_end_x) + st
    yg = y * jax.nn.silu(z)
    outs = []
    for g in range(G):
        part = yg[:, g * W:(g + 1) * W]
        outs.append(part * lax.rsqrt(jnp.mean(part * part, axis=1, keepdims=True) + RMS_EPS))
    return (jnp.concatenate(outs, axis=1) * norm_w,), s_new


GDN_PREP_CHUNKS = 4


def _gdn_prep(xs_, ps_):
    qkv, ab = xs_
    a_log, dt_bias = ps_
    B = qkv.shape[0]
    H, D, L = GDN_HEADS, GDN_HEAD_DIM, GDN_CHUNK
    g_all = -jnp.exp(a_log) * jax.nn.softplus(ab + dt_bias)
    row, col = _iota2((B, B), 0), _iota2((B, B), 1)
    g_cs = _mask_left((((row // L) == (col // L)) & (row >= col)).astype(F32), g_all)
    g_cs_t = g_cs.T
    beta_all = jax.nn.sigmoid(ab)
    incl, strict = _tril(L), _tril(L, strict=True)
    qs, ks, vs = [], [], []
    for h in range(H):
        q = qkv[:, h * D:(h + 1) * D]
        k = qkv[:, GDN_WIDTH + h * D:GDN_WIDTH + (h + 1) * D]
        qs.append(q * lax.rsqrt(jnp.sum(q * q, axis=1, keepdims=True) + RMS_EPS) * (D ** -0.5))
        ks.append(k * lax.rsqrt(jnp.sum(k * k, axis=1, keepdims=True) + RMS_EPS))
        vs.append(qkv[:, 2 * GDN_WIDTH + h * D:2 * GDN_WIDTH + (h + 1) * D])
    pairs = [(c, h) for c in range(B // L) for h in range(H)]
    rows = {c: slice(c * L, (c + 1) * L) for c in range(B // L)}
    q_ = {(c, h): qs[h][rows[c]] for c, h in pairs}
    k_ = {(c, h): ks[h][rows[c]] for c, h in pairs}
    col_ = {(c, h): g_cs[rows[c], h:h + 1] for c, h in pairs}
    beta_ = {(c, h): beta_all[rows[c], H + h:H + h + 1] for c, h in pairs}
    gamma = {p: jnp.exp(jnp.where(incl, col_[p] - g_cs_t[p[1]:p[1] + 1, rows[p[0]]], NEG_BIG)) for p in pairs}
    kb = {p: k_[p] * beta_[p] for p in pairs}
    a_mat = [jnp.where(strict, _dot(kb[p], k_[p], "nt") * gamma[p], 0.0) for p in pairs]
    attn = {p: jnp.where(incl, _dot(q_[p], k_[p], "nt") * gamma[p], 0.0) for p in pairs}
    t_mat = dict(zip(pairs, _unit_lower_inverses(a_mat)))
    u = {p: _dot(t_mat[p], vs[p[1]][rows[p[0]]] * beta_[p]) for p in pairs}
    w = {p: _dot(t_mat[p], kb[p] * jnp.exp(col_[p])) for p in pairs}
    qd = {p: q_[p] * jnp.exp(col_[p]) for p in pairs}
    kd = {p: k_[p] * jnp.exp(col_[p][L - 1:L, :] - col_[p]) for p in pairs}

    def whole(parts):
        return jnp.concatenate([jnp.concatenate([parts[(c, h)] for h in range(H)], axis=1) for c in range(B // L)], axis=0)

    return (whole(u), whole(w), whole(qd), whole(kd), whole(attn), g_cs)


def _gdn_scan(xs_, ps_, s):
    u, w, qd, kd, attn, g_cs, gate = xs_
    (norm_w,) = ps_
    L = u.shape[0]
    H, D = GDN_HEADS, GDN_HEAD_DIM
    heads = range(H)
    lanes = [slice(h * D, (h + 1) * D) for h in heads]
    s_h = [s[lanes[h], :] for h in heads]
    v_new = [u[:, lanes[h]] - _dot(w[:, lanes[h]], s_h[h]) for h in heads]
    o = [_dot(qd[:, lanes[h]], s_h[h]) + _dot(attn[:, h * L:(h + 1) * L], v_new[h]) for h in heads]
    decay = [jnp.exp(g_cs[L - 1:L, h:h + 1]) for h in heads]
    s_new = [s_h[h] * decay[h] + _dot(kd[:, lanes[h]], v_new[h], "tn") for h in heads]
    o = [o[h] * lax.rsqrt(jnp.mean(o[h] * o[h], axis=1, keepdims=True) + RMS_EPS) * norm_w * jax.nn.silu(gate[:, lanes[h]])
         for h in heads]
    return (jnp.concatenate(o, axis=1),), jnp.concatenate(s_new, axis=0)


def _gdn_forward(tag, gqkv, h, sp):
    T = gqkv.shape[0]
    blk = GDN_PREP_CHUNKS * GDN_CHUNK
    prep_in = [(gqkv, blk, 3 * GDN_WIDTH, 0), _seg_blk(h, "gab", blk)]
    prep_p = [_lane_pad(sp["gdn_a_log"]), _lane_pad(sp["gdn_dt_bias"])]
    mx = MXU_DTYPE
    prep = _chain_fwd(f"gdn_prep_{tag}", _gdn_prep, T // blk, prep_in, prep_p,
                      [(blk, GDN_WIDTH, F32), (blk, GDN_WIDTH, mx), (blk, GDN_WIDTH, mx), (blk, GDN_WIDTH, mx),
                       (blk, GDN_HEADS * GDN_CHUNK, mx), (blk, LANES, F32)])
    widths = [GDN_WIDTH] * 4 + [GDN_HEADS * GDN_CHUNK, LANES]
    scan_in = [(a, GDN_CHUNK, wd, 0) for a, wd in zip(prep, widths)] + [_seg_blk(h, "gg", GDN_CHUNK)]
    scan_p = [sp["gdn_norm_w"]]
    o, states = _chain_fwd(f"gdn_scan_{tag}", _gdn_scan, T // GDN_CHUNK, scan_in, scan_p, [(GDN_CHUNK, GDN_WIDTH, mx)],
                           (GDN_WIDTH, GDN_HEAD_DIM))
    return o, dict(prep_in=prep_in, prep_p=prep_p, scan_in=scan_in, scan_p=scan_p, states=states, widths=widths)


def _gdn_backward(tag, do, sv, dx_dtype):
    T = do.shape[0]
    blk = GDN_PREP_CHUNKS * GDN_CHUNK
    dscan, (dnorm,) = _chain_bwd(f"gdn_scan_bwd_{tag}", _gdn_scan, T // GDN_CHUNK, sv["scan_in"], sv["scan_p"],
                                 [(do, GDN_CHUNK, GDN_WIDTH)], sprev=sv["states"], dx_dtypes=[F32] * 6 + [dx_dtype])
    douts = [(d, blk, wd) for d, wd in zip(dscan[:6], sv["widths"])]
    (dgqkv, dgab), (da_log, ddt_bias) = _chain_bwd(f"gdn_prep_bwd_{tag}", _gdn_prep, T // blk, sv["prep_in"], sv["prep_p"],
                                                   douts, dx_dtypes=[F32, dx_dtype])
    return dgqkv, dgab, dscan[6], da_log[:, :GDN_HEADS], ddt_bias[:, :GDN_HEADS], dnorm


def _gla_block(xs_, ps_, s_t):
    qkv, glr, r = xs_
    w2, gate_b, norm_w = ps_
    B = qkv.shape[0]
    H, K, V, C = GLA_HEADS, GLA_KEY_DIM, GLA_VAL_DIM, GLA_CHUNK
    q = qkv[:, :GLA_K] * (K ** -0.5)
    k = qkv[:, GLA_K:2 * GLA_K]
    v = qkv[:, 2 * GLA_K:]
    gk = jax.nn.log_sigmoid(_dot(glr, w2) + gate_b) / GLA_NORMALIZER
    row, col = _iota2((B, B), 0), _iota2((B, B), 1)
    same = (row // C) == (col // C)
    mask = same & (row >= col)
    b_cs = _mask_left(mask.astype(F32), gk)
    b_end = _mask_left((col == (row // C) * C + (C - 1)).astype(F32), b_cs)
    q_e = q * jnp.exp(b_cs)
    k_e = k * jnp.exp(-b_cs)
    k_d = k * jnp.exp(b_end - b_cs)
    intra = []
    for h in range(H):
        a_mat = jnp.where(mask, _dot(q_e[:, h * K:(h + 1) * K], k_e[:, h * K:(h + 1) * K], "nt"), 0.0)
        intra.append(_dot(a_mat, v[:, h * V:(h + 1) * V]))
    o = jnp.concatenate(intra, axis=1)
    chunks = [slice(j * C, (j + 1) * C) for j in range(B // C)]
    fresh = [jnp.concatenate([_dot(v[sl, h * V:(h + 1) * V], k_d[sl, h * K:(h + 1) * K], "tn") for h in range(H)], axis=1)
             for sl in chunks]
    entering = []
    for j, sl in enumerate(chunks):
        entering.append(s_t)
        s_t = s_t * jnp.exp(b_end[j * C:j * C + 1, :]) + fresh[j]
    inter = [jnp.concatenate([_dot(q_e[sl, h * K:(h + 1) * K], entering[j][:, h * K:(h + 1) * K], "nt") for h in range(H)],
                             axis=1) for j, sl in enumerate(chunks)]
    o = o + jnp.concatenate(inter, axis=0)
    outs = []
    for h in range(H):
        oh = o[:, h * V:(h + 1) * V]
        oh = oh * lax.rsqrt(jnp.mean(oh * oh, axis=1, keepdims=True) + RMS_EPS) * norm_w
        outs.append(oh * jax.nn.silu(r[:, h * V:(h + 1) * V]))
    return (jnp.concatenate(outs, axis=1),), s_t


def _merge_fn(xs_, ps_):
    gates, y_ssd, y_gdn, y_gla = xs_
    d = D_MODEL
    return (jax.nn.sigmoid(gates[:, :d]) * y_ssd + jax.nn.sigmoid(gates[:, d:2 * d]) * y_gdn
            + jax.nn.sigmoid(gates[:, 2 * d:]) * y_gla,)


def _ln_fn(xs_, ps_):
    x, r = xs_
    g, b = ps_
    t = ALPHA * x + r
    mu = jnp.mean(t, axis=1, keepdims=True)
    var = jnp.mean(jnp.square(t - mu), axis=1, keepdims=True)
    return ((t - mu) * lax.rsqrt(var + LN_EPS) * g + b,)


def _row_spec(rows, width, colblk, n, reverse):
    if reverse:
        return pl.BlockSpec((rows, width), lambda c: (n - 1 - c, colblk))
    return pl.BlockSpec((rows, width), lambda c: (c, colblk))


def _full_spec(shape):
    zeros = (0,) * len(shape)
    return pl.BlockSpec(shape, lambda c: zeros)


def _chain_fwd(name, fn, n, blocked, full, out_defs, state_shape=None):
    nb, nf, no = len(blocked), len(full), len(out_defs)

    def body(*refs):
        xs = [r[...].astype(F32) for r in refs[:nb]]
        ps = [r[...] for r in refs[nb:nb + nf]]
        o_refs = refs[nb + nf:nb + nf + no]
        if state_shape is None:
            outs = fn(xs, ps)
        else:
            sprev_ref, s_ref = refs[nb + nf + no:]

            @pl.when(pl.program_id(0) == 0)
            def _():
                s_ref[...] = jnp.zeros_like(s_ref)

            s = s_ref[...]
            sprev_ref[0] = s
            outs, s_new = fn(xs, ps, s)
            s_ref[...] = s_new
        for r, o in zip(o_refs, outs):
            r[...] = o.astype(r.dtype)

    in_specs = [_row_spec(rows, width, cb, n, False) for _, rows, width, cb in blocked]
    in_specs += [_full_spec(a.shape) for a in full]
    out_specs = [_row_spec(rows, width, 0, n, False) for rows, width, _ in out_defs]
    out_shape = [jax.ShapeDtypeStruct((n * rows, width), dt) for rows, width, dt in out_defs]
    scratch = []
    if state_shape is not None:
        out_specs.append(pl.BlockSpec((1,) + state_shape, lambda c: (c, 0, 0)))
        out_shape.append(jax.ShapeDtypeStruct((n,) + state_shape, F32))
        scratch.append(pltpu.VMEM(state_shape, F32))
    return pl.pallas_call(body, name=name, grid=(n,), in_specs=in_specs, out_specs=out_specs, out_shape=out_shape,
                          scratch_shapes=scratch, compiler_params=_cparams(("arbitrary",)))(
        *[a for a, _, _, _ in blocked], *full)


def _chain_bwd(name, fn, n, blocked, full, douts, sprev=None, dx_dtypes=None):
    nb, nf, nd = len(blocked), len(full), len(douts)
    has_state = sprev is not None
    dx_dtypes = dx_dtypes or [F32] * nb

    def body(*refs):
        pos = 0
        b_refs = refs[pos:pos + nb]; pos += nb
        f_refs = refs[pos:pos + nf]; pos += nf
        d_refs = refs[pos:pos + nd]; pos += nd
        if has_state:
            sprev_ref = refs[pos]; pos += 1
        dx_refs = refs[pos:pos + nb]; pos += nb
        dp_refs = refs[pos:pos + nf]; pos += nf
        if has_state:
            ds_ref = refs[pos]

        @pl.when(pl.program_id(0) == 0)
        def _():
            for r in dp_refs:
                r[...] = jnp.zeros_like(r)
            if has_state:
                ds_ref[...] = jnp.zeros_like(ds_ref)

        xs = [r[...].astype(F32) for r in b_refs]
        ps = [r[...] for r in f_refs]
        dys = tuple(r[...].astype(F32) for r in d_refs)
        if has_state:
            _, vjp = jax.vjp(fn, xs, ps, sprev_ref[0])
            dxs, dps, ds = vjp((dys, ds_ref[...]))
            ds_ref[...] = ds
        else:
            _, vjp = jax.vjp(fn, xs, ps)
            dxs, dps = vjp(dys)
        for r, d in zip(dx_refs, dxs):
            r[...] = d.astype(r.dtype)
        for r, d in zip(dp_refs, dps):
            r[...] += d

    in_specs = [_row_spec(rows, width, cb, n, True) for _, rows, width, cb in blocked]
    in_specs += [_full_spec(a.shape) for a in full]
    in_specs += [_row_spec(rows, width, 0, n, True) for _, rows, width in douts]
    args = [a for a, _, _, _ in blocked] + list(full) + [a for a, _, _ in douts]
    scratch = []
    if has_state:
        st_shape = sprev.shape[1:]
        in_specs.append(pl.BlockSpec((1,) + st_shape, lambda c: (n - 1 - c, 0, 0)))
        args.append(sprev)
        scratch.append(pltpu.VMEM(st_shape, F32))
    out_specs = [_row_spec(rows, width, 0, n, True) for _, rows, width, _ in blocked]
    out_specs += [_full_spec(a.shape) for a in full]
    out_shape = [jax.ShapeDtypeStruct((n * rows, width), dt) for (_, rows, width, _), dt in zip(blocked, dx_dtypes)]
    out_shape += [jax.ShapeDtypeStruct(a.shape, F32) for a in full]
    res = pl.pallas_call(body, name=name, grid=(n,), in_specs=in_specs, out_specs=out_specs, out_shape=out_shape,
                         scratch_shapes=scratch, compiler_params=_cparams(("arbitrary",)))(*args)
    return res[:nb], res[nb:]


def _tile(n, target, unit):
    if n <= target:
        return n
    best = None
    for t in range(unit, target + 1, unit):
        if n % t == 0:
            best = t
    assert best is not None, (n, target, unit)
    return best


def _mm(name, a, b, dims="nn", out_dtype=F32, tm=2048, tn=512, tk=2048):
    if dims == "nn":
        (M, K), (_, N) = a.shape, b.shape
    elif dims == "nt":
        (M, K), (N, _) = a.shape, b.shape
    else:
        (K, M), (_, N) = a.shape, b.shape
    tm, tn, tk = _tile(M, tm, LANES), _tile(N, tn, LANES), _tile(K, tk, LANES)
    nk = K // tk

    def body(a_ref, b_ref, o_ref, acc_ref):
        part = _dot(a_ref[...], b_ref[...], dims)
        if nk == 1:
            o_ref[...] = part.astype(o_ref.dtype)
            return

        @pl.when(pl.program_id(2) == 0)
        def _():
            acc_ref[...] = part

        @pl.when(pl.program_id(2) > 0)
        def _():
            acc_ref[...] += part

        @pl.when(pl.program_id(2) == nk - 1)
        def _():
            o_ref[...] = acc_ref[...].astype(o_ref.dtype)

    if dims == "tn":
        a_spec = pl.BlockSpec((tk, tm), lambda j, i, k: (k, i))
    else:
        a_spec = pl.BlockSpec((tm, tk), lambda j, i, k: (i, k))
    if dims == "nt":
        b_spec = pl.BlockSpec((tn, tk), lambda j, i, k: (j, k))
    else:
        b_spec = pl.BlockSpec((tk, tn), lambda j, i, k: (k, j))
    return pl.pallas_call(
        body, name=name, grid=(N // tn, M // tm, nk), in_specs=[a_spec, b_spec],
        out_specs=pl.BlockSpec((tm, tn), lambda j, i, k: (i, j)), out_shape=jax.ShapeDtypeStruct((M, N), out_dtype),
        scratch_shapes=[pltpu.VMEM((tm, tn) if nk > 1 else (8, LANES), F32)],
        compiler_params=_cparams(("parallel", "parallel", "arbitrary")))(a, b)


CONV_CB = 256


def _shift_down(x, k):
    if k == 0:
        return x
    return jnp.where(_iota2(x.shape, 0) >= k, pltpu.roll(x, k, 0), 0.0)


def _shift_up(x, k):
    if k == 0:
        return x
    t = x.shape[0]
    return jnp.where(_iota2(x.shape, 0) < t - k, pltpu.roll(x, t - k, 0), 0.0)


def _conv_pre(x, w, b):
    kk = w.shape[0]
    pre = x * w[kk - 1:kk, :]
    for k in range(kk - 1):
        pre = pre + _shift_down(x, kk - 1 - k) * w[k:k + 1, :]
    return pre if b is None else pre + b


def _conv_bwd_pre(x, w, dpre, dw_ref, db_ref):
    kk = w.shape[0]
    dx = dpre * w[kk - 1:kk, :]
    dw_ref[kk - 1:kk, :] = jnp.sum(dpre * x, axis=0, keepdims=True)
    for k in range(kk - 1):
        dx = dx + _shift_up(dpre, kk - 1 - k) * w[k:k + 1, :]
        dw_ref[k:k + 1, :] = jnp.sum(dpre * _shift_down(x, kk - 1 - k), axis=0, keepdims=True)
    if db_ref is not None:
        db_ref[...] = jnp.sum(dpre, axis=0, keepdims=True)
    return dx


def _dsilu(pre):
    sg = jax.nn.sigmoid(pre)
    return sg * (1.0 + pre * (1.0 - sg))


def _conv_silu_fwd(name, src, col0, w, b):
    T = src.shape[0]
    kk, C = w.shape
    cb = CONV_CB
    off = col0 // cb

    def body(*refs):
        x_ref, w_ref = refs[:2]
        b_val = refs[2][...] if b is not None else None
        refs[-1][...] = jax.nn.silu(_conv_pre(x_ref[...], w_ref[...], b_val))

    in_specs = [pl.BlockSpec((T, cb), lambda j: (0, off + j)), pl.BlockSpec((kk, cb), lambda j: (0, j))]
    args = [src, w]
    if b is not None:
        in_specs.append(pl.BlockSpec((1, cb), lambda j: (0, j)))
        args.append(b)
    return pl.pallas_call(body, name=name, grid=(C // cb,), in_specs=in_specs,
                          out_specs=pl.BlockSpec((T, cb), lambda j: (0, j)), out_shape=jax.ShapeDtypeStruct((T, C), F32),
                          compiler_params=_cparams(("parallel",)))(*args)


def _conv_silu_bwd(name, src, col0, w, b, dy, dx_dtype):
    T = src.shape[0]
    kk, C = w.shape
    cb = CONV_CB
    off = col0 // cb
    has_b = b is not None

    def body(*refs):
        x_ref, w_ref = refs[:2]
        pos = 2
        b_val = None
        if has_b:
            b_val = refs[pos][...]; pos += 1
        dy_ref = refs[pos]; pos += 1
        dx_ref, dw_ref = refs[pos], refs[pos + 1]
        db_ref = refs[pos + 2] if has_b else None
        x, wv = x_ref[...], w_ref[...]
        dpre = dy_ref[...] * _dsilu(_conv_pre(x, wv, b_val))
        dx_ref[...] = _conv_bwd_pre(x, wv, dpre, dw_ref, db_ref).astype(dx_ref.dtype)

    in_specs = [pl.BlockSpec((T, cb), lambda j: (0, off + j)), pl.BlockSpec((kk, cb), lambda j: (0, j))]
    args = [src, w]
    if has_b:
        in_specs.append(pl.BlockSpec((1, cb), lambda j: (0, j)))
        args.append(b)
    in_specs.append(pl.BlockSpec((T, cb), lambda j: (0, j)))
    args.append(dy)
    out_specs = [pl.BlockSpec((T, cb), lambda j: (0, j)), pl.BlockSpec((kk, cb), lambda j: (0, j))]
    out_shape = [jax.ShapeDtypeStruct((T, C), dx_dtype), jax.ShapeDtypeStruct((kk, C), F32)]
    if has_b:
        out_specs.append(pl.BlockSpec((1, cb), lambda j: (0, j)))
        out_shape.append(jax.ShapeDtypeStruct((1, C), F32))
    return pl.pallas_call(body, name=name, grid=(C // cb,), in_specs=in_specs, out_specs=out_specs, out_shape=out_shape,
                          compiler_params=_cparams(("parallel",)))(*args)


def _ffn_glu_fwd(name, up, w, b, out_dtype=F32):
    T = up.shape[0]
    kk = w.shape[0]
    cb = CONV_CB
    width = up.shape[1] // 2
    nblk = width // cb

    def body(g_ref, u_ref, wg_ref, wu_ref, bg_ref, bu_ref, o_ref):
        g = _conv_pre(g_ref[...], wg_ref[...], bg_ref[...])
        u = _conv_pre(u_ref[...], wu_ref[...], bu_ref[...])
        o_ref[...] = (jax.nn.silu(g) * u).astype(o_ref.dtype)

    lo, hi = (lambda j: (0, j)), (lambda j: (0, nblk + j))
    in_specs = [pl.BlockSpec((T, cb), lo), pl.BlockSpec((T, cb), hi), pl.BlockSpec((kk, cb), lo), pl.BlockSpec((kk, cb), hi),
                pl.BlockSpec((1, cb), lo), pl.BlockSpec((1, cb), hi)]
    return pl.pallas_call(body, name=name, grid=(nblk,), in_specs=in_specs, out_specs=pl.BlockSpec((T, cb), lo),
                          out_shape=jax.ShapeDtypeStruct((T, width), out_dtype),
                          compiler_params=_cparams(("parallel",)))(up, up, w, w, b, b)


def _ffn_glu_bwd(name, up, w, b, dact, dx_dtype):
    T = up.shape[0]
    kk = w.shape[0]
    cb = CONV_CB
    width = up.shape[1] // 2
    nblk = width // cb

    def body(g_ref, u_ref, wg_ref, wu_ref, bg_ref, bu_ref, d_ref, dg_ref, du_ref, dwg_ref, dwu_ref, dbg_ref, dbu_ref):
        xg, xu, wg, wu = g_ref[...], u_ref[...], wg_ref[...], wu_ref[...]
        g = _conv_pre(xg, wg, bg_ref[...])
        u = _conv_pre(xu, wu, bu_ref[...])
        d = d_ref[...].astype(F32)
        dg_ref[...] = _conv_bwd_pre(xg, wg, d * u * _dsilu(g), dwg_ref, dbg_ref).astype(dg_ref.dtype)
        du_ref[...] = _conv_bwd_pre(xu, wu, d * jax.nn.silu(g), dwu_ref, dbu_ref).astype(du_ref.dtype)

    lo, hi = (lambda j: (0, j)), (lambda j: (0, nblk + j))
    in_specs = [pl.BlockSpec((T, cb), lo), pl.BlockSpec((T, cb), hi), pl.BlockSpec((kk, cb), lo), pl.BlockSpec((kk, cb), hi),
                pl.BlockSpec((1, cb), lo), pl.BlockSpec((1, cb), hi), pl.BlockSpec((T, cb), lo)]
    out_specs = [pl.BlockSpec((T, cb), lo)] * 2 + [pl.BlockSpec((kk, cb), lo)] * 2 + [pl.BlockSpec((1, cb), lo)] * 2
    out_shape = ([jax.ShapeDtypeStruct((T, width), dx_dtype)] * 2 + [jax.ShapeDtypeStruct((kk, width), F32)] * 2
                 + [jax.ShapeDtypeStruct((1, width), F32)] * 2)
    return pl.pallas_call(body, name=name, grid=(nblk,), in_specs=in_specs, out_specs=out_specs, out_shape=out_shape,
                          compiler_params=_cparams(("parallel",)))(up, up, w, w, b, b, dact)


def _loss_head(y, target):
    T, D = y.shape
    tb = _tile(T, 256, 8)

    def body(y_ref, t_ref, dy_ref, l_ref):
        @pl.when(pl.program_id(0) == 0)
        def _():
            l_ref[...] = jnp.zeros_like(l_ref)

        err = y_ref[...] - t_ref[...]
        dy_ref[...] = err * (1.0 / D)
        l_ref[...] += jnp.sum(err * err, axis=0, keepdims=True) * (0.5 / D)

    spec = pl.BlockSpec((tb, D), lambda i: (i, 0))
    return pl.pallas_call(body, name="loss_head", grid=(T // tb,), in_specs=[spec, spec],
                          out_specs=[spec, pl.BlockSpec((1, D), lambda i: (0, 0))],
                          out_shape=[jax.ShapeDtypeStruct((T, D), F32), jax.ShapeDtypeStruct((1, D), F32)],
                          compiler_params=_cparams(("arbitrary",)))(y, target)


def _adamw_math(w, g, m, v):
    m = ADAM_B1 * m + (1.0 - ADAM_B1) * g
    v = ADAM_B2 * v + (1.0 - ADAM_B2) * jnp.square(g)
    m_hat = m / (1.0 - ADAM_B1 ** ADAM_STEP)
    v_hat = v / (1.0 - ADAM_B2 ** ADAM_STEP)
    return -ADAM_LR * (m_hat / (jnp.sqrt(v_hat) + ADAM_EPS) + ADAM_WD * w), m, v


def _adamw(name, w, g, m, v):
    A, R, C = w.shape
    if C % LANES == 0:
        rb, cb = _slab(R, C)
    else:
        rb, cb = _tile(R, max(8, SLAB_BYTES // 2 // (C * 4) // 8 * 8), 8), C

    def body(w_ref, g_ref, m_ref, v_ref, d_ref, mo_ref, vo_ref):
        d, mn, vn = _adamw_math(w_ref[...], g_ref[...], m_ref[...], v_ref[...])
        d_ref[...] = d
        mo_ref[...] = mn
        vo_ref[...] = vn

    spec = pl.BlockSpec((1, rb, cb), lambda a, r, q: (a, r, q))
    return pl.pallas_call(body, name=name, grid=(A, R // rb, C // cb), in_specs=[spec] * 4, out_specs=[spec] * 3,
                          out_shape=[jax.ShapeDtypeStruct(w.shape, F32)] * 3,
                          compiler_params=_cparams(("parallel", "parallel", "parallel")))(w, g, m, v)


def _adamw_small(parts, w, m, v):
    def body(p_ref, w_ref, m_ref, v_ref, g_ref, d_ref, mo_ref, vo_ref):
        g = p_ref[0]
        for i in range(1, N_DEV):
            g = g + p_ref[i]
        d, mn, vn = _adamw_math(w_ref[...], g, m_ref[...], v_ref[...])
        g_ref[...] = g
        d_ref[...] = d
        mo_ref[...] = mn
        vo_ref[...] = vn

    return pl.pallas_call(body, name="adamw_small", out_shape=[jax.ShapeDtypeStruct(w.shape, F32)] * 4,
                          compiler_params=_cparams())(parts, w, m, v)


def _add_blocks(name, a, b, out_dtype=F32):
    n, R, W = a.shape
    rb = _tile(R, 512, 8)

    def body(a_ref, b_ref, o_ref):
        o_ref[...] = (a_ref[...].astype(F32) + b_ref[...].astype(F32)).astype(o_ref.dtype)

    spec = pl.BlockSpec((1, rb, W), lambda i, r: (i, r, 0))
    return pl.pallas_call(body, name=name, grid=(n, R // rb), in_specs=[spec, spec], out_specs=spec,
                          out_shape=jax.ShapeDtypeStruct(a.shape, out_dtype),
                          compiler_params=_cparams(("parallel", "parallel")))(a, b)


SLAB_BYTES = 1 << 20


def _slab(R, W):
    if R % 16 == 0:
        return _tile(R, max(16, SLAB_BYTES // (4 * W) // 16 * 16), 16), W
    assert W % LANES == 0, (R, W)
    return R, _tile(W, max(LANES, SLAB_BYTES // (4 * R) // LANES * LANES), LANES)


def _pair_add(name, g, other, c, chip):
    _, R, W = g.shape
    rb, cb = _slab(R, W)

    def body(s_ref, a_ref, b_ref, send_ref, own_ref):
        s = a_ref[0] + b_ref[0]
        send_ref[0] = s.astype(send_ref.dtype)

        @pl.when(pl.program_id(2) == s_ref[1])
        def _():
            own_ref[...] = s

    grid_spec = pltpu.PrefetchScalarGridSpec(
        num_scalar_prefetch=1, grid=(R // rb, W // cb, 4),
        in_specs=[pl.BlockSpec((1, rb, cb), lambda r, q, p, s_ref: (2 * p + s_ref[0], r, q)),
                  pl.BlockSpec((1, rb, cb), lambda r, q, p, s_ref: (p, r, q))],
        out_specs=[pl.BlockSpec((1, rb, cb), lambda r, q, p, s_ref: (p, r, q)),
                   pl.BlockSpec((rb, cb), lambda r, q, p, s_ref: (r, q))])
    scalars = jnp.stack([c, chip]).astype(jnp.int32)
    return pl.pallas_call(body, name=name, grid_spec=grid_spec,
                          out_shape=[jax.ShapeDtypeStruct((4, R, W), MXU_DTYPE), jax.ShapeDtypeStruct((R, W), F32)],
                          compiler_params=_cparams(("parallel", "parallel", "arbitrary")))(scalars, g, other)


def _sum4(name, own, parts):
    R, W = own.shape
    rb, cb = _slab(R, W)

    def body(o_ref, p_ref, out_ref):
        out_ref[...] = ((o_ref[...] + p_ref[0].astype(F32)) + p_ref[1].astype(F32)) + p_ref[2].astype(F32)

    return pl.pallas_call(body, name=name, grid=(R // rb, W // cb),
                          in_specs=[pl.BlockSpec((rb, cb), lambda r, q: (r, q)), pl.BlockSpec((3, rb, cb), lambda r, q: (0, r, q))],
                          out_specs=pl.BlockSpec((rb, cb), lambda r, q: (r, q)), out_shape=jax.ShapeDtypeStruct((R, W), F32),
                          compiler_params=_cparams(("parallel", "parallel")))(own, parts)


MESH = pl.DeviceIdType.MESH
ANY = pl.BlockSpec(memory_space=pl.ANY)


def _place():
    return lax.axis_index("x"), lax.axis_index("y"), lax.axis_index("c")


def _other_chips(x, y):
    return [(1 - x, y), (x, 1 - y), (1 - x, 1 - y)]


def _all_gather(name, blocks):
    n = len(blocks)

    def body(*refs):
        x_refs, out_refs = refs[:n], refs[n:2 * n]
        send_sems, recv_sems, local_sems = refs[2 * n:]
        x, y, c = _place()
        me, sibling = (x, y, c), (x, y, 1 - c)
        chips = _other_chips(x, y)

        def slot(a, px, py, pc):
            return out_refs[a].at[4 * px + 2 * py + pc]

        def copy(a, k, blk, to, src=None):
            return pltpu.make_async_remote_copy(src_ref=slot(a, *blk) if src is None else src, dst_ref=slot(a, *blk),
                                                send_sem=send_sems.at[a, k], recv_sem=recv_sems.at[a, k],
                                                device_id=to, device_id_type=MESH)

        mine = [pltpu.make_async_copy(x_refs[a], slot(a, *me), local_sems.at[a]) for a in range(n)]
        for cp in mine:
            cp.start()
        first = []
        for j, chip in enumerate(chips):
            first += [copy(a, 1 + j, me, (*chip, c), src=x_refs[a]) for a in range(n)]
        first += [copy(a, 0, me, sibling, src=x_refs[a]) for a in range(n)]
        for cp in first:
            cp.start()
        passed = []
        for j, chip in enumerate(chips):
            for a in range(n):
                copy(a, 1 + j, (*chip, c), me).wait_recv()
                passed.append(copy(a, 4 + j, (*chip, c), sibling))
                passed[-1].start()
        for a in range(n):
            copy(a, 0, sibling, me).wait_recv()
        for j, chip in enumerate(chips):
            for a in range(n):
                copy(a, 4 + j, (*chip, 1 - c), me).wait_recv()
        for cp in first + passed:
            cp.wait_send()
        for cp in mine:
            cp.wait()

    return pl.pallas_call(body, name=name, in_specs=[ANY] * n, out_specs=[ANY] * n,
                          out_shape=[jax.ShapeDtypeStruct((N_DEV,) + b.shape, b.dtype) for b in blocks],
                          scratch_shapes=[pltpu.SemaphoreType.DMA((n, 7)), pltpu.SemaphoreType.DMA((n, 7)),
                                          pltpu.SemaphoreType.DMA((n,))])(*blocks)


def _routes_to_sibling(x, y, c):
    return [(2 * p + (1 - c), p, (x, y, 1 - c)) for p in range(4)]


def _routes_to_chips(x, y, c):
    return [(2 * px + py, j, (px, py, c)) for j, (px, py) in enumerate(_other_chips(x, y))]


def _routes_block_to_chips(x, y, c):
    me = 4 * x + 2 * y + c
    return [(me, me, (px, py, c)) for px, py in _other_chips(x, y)]


def _routes_blocks_to_sibling(x, y, c):
    return [(4 * px + 2 * py + c, 4 * px + 2 * py + c, (x, y, 1 - c)) for px, py in [(x, y)] + _other_chips(x, y)]


def _route_copies(routes, src_refs, land_refs, send_sems, recv_sems):
    x, y, c = _place()
    copies = []
    for a, (src, land) in enumerate(zip(src_refs, land_refs)):
        plan = routes(x, y, c)
        for k, (s, d, target) in enumerate(plan):
            i = a * len(plan) + k
            copies.append(pltpu.make_async_remote_copy(src_ref=src.at[s], dst_ref=land.at[d], send_sem=send_sems.at[i],
                                                       recv_sem=recv_sems.at[i], device_id=target, device_id_type=MESH))
    return copies


def _exchange(name, routes, n_routes, srcs, land_slots):
    n = len(srcs)

    def body(*refs):
        copies = _route_copies(routes, refs[:n], refs[n:2 * n], refs[2 * n], refs[2 * n + 1])
        for cp in copies:
            cp.start()
        for cp in copies:
            cp.wait_recv()
        for cp in copies:
            cp.wait_send()

    return pl.pallas_call(body, name=name, in_specs=[ANY] * n, out_specs=[ANY] * n,
                          out_shape=[jax.ShapeDtypeStruct((land_slots,) + s.shape[1:], s.dtype) for s in srcs],
                          scratch_shapes=[pltpu.SemaphoreType.DMA((n * n_routes,)), pltpu.SemaphoreType.DMA((n * n_routes,))])(*srcs)


HBM_SPEC = pl.BlockSpec(memory_space=pltpu.HBM)
SEM_SPEC = pl.BlockSpec(memory_space=pltpu.SEMAPHORE)
DATAFLOW = pltpu.SideEffectType.DATAFLOW_SIDE_EFFECTING


def _exchange_start(name, routes, n_routes, srcs, lands):
    n = len(srcs)
    in_place = lands is None
    bufs = list(srcs) + ([] if in_place else list(lands))
    nb = len(bufs)

    def body(*refs):
        src_refs = refs[:n]
        land_refs = src_refs if in_place else refs[n:nb]
        send_sems, recv_sems = refs[nb], refs[nb + 1]
        token = refs[-1]
        for cp in _route_copies(routes, src_refs, land_refs, send_sems, recv_sems):
            cp.start()
        token[...] = jnp.zeros_like(token)

    sems = [pltpu.SemaphoreType.DMA((n * n_routes,)), pltpu.SemaphoreType.DMA((n * n_routes,))]
    out = pl.pallas_call(
        body, name=name, in_specs=[HBM_SPEC] * nb,
        out_shape=sems + [pltpu.HBM(b.shape, b.dtype) for b in bufs] + [jax.ShapeDtypeStruct((8, LANES), F32)],
        out_specs=[SEM_SPEC, SEM_SPEC] + [HBM_SPEC] * nb + [pl.BlockSpec(memory_space=pltpu.VMEM)],
        input_output_aliases={i: 2 + i for i in range(nb)},
        compiler_params=pltpu.CompilerParams(has_side_effects=DATAFLOW))(
        *[pltpu.with_memory_space_constraint(b, pltpu.HBM) for b in bufs])
    return (out[0], out[1], list(out[2:2 + nb])), out[-1]


def _exchange_wait(name, routes, n_routes, n, started, after):
    send_sems, recv_sems, bufs = started
    nb = len(bufs)
    in_place = nb == n

    def body(*refs):
        src_refs = refs[:n]
        land_refs = src_refs if in_place else refs[n:nb]
        for cp in _route_copies(routes, src_refs, land_refs, refs[nb], refs[nb + 1]):
            cp.wait_send()
            cp.wait_recv()

    out = pl.pallas_call(
        body, name=name, in_specs=[HBM_SPEC] * nb + [SEM_SPEC, SEM_SPEC, ANY],
        out_shape=[pltpu.HBM(b.shape, b.dtype) for b in bufs], out_specs=[HBM_SPEC] * nb,
        input_output_aliases={i: i for i in range(nb)},
        compiler_params=pltpu.CompilerParams(has_side_effects=DATAFLOW))(*bufs, send_sems, recv_sems, after)
    return list(out[:n]) if in_place else list(out[n:])


def _reduce_scatter_finish(tag, gs, from_sibling, exchange_chips):
    x, y, c = _place()
    sums = [_pair_add(f"rs_add_{tag}_{i}", g, o, c, 2 * x + y) for i, (g, o) in enumerate(zip(gs, from_sibling))]
    got = exchange_chips([s[0] for s in sums])
    return [_sum4(f"rs_sum_{tag}_{i}", s[1], q) for i, (s, q) in enumerate(zip(sums, got))]


def _reduce_scatter(tag, gs):
    from_sibling = _exchange(f"rs_swap_{tag}", _routes_to_sibling, 4, gs, 4)
    return _reduce_scatter_finish(tag, gs, from_sibling, lambda ps: _exchange(f"rs_chips_{tag}", _routes_to_chips, 3, ps, 3))


def _flat_rows(n_elems):
    return -(-n_elems // (FLAT_W * 16)) * 16


def _pack(arrays, dtype):
    flat = jnp.concatenate([a.reshape(-1).astype(dtype) for a in arrays])
    rows = _flat_rows(flat.shape[0])
    flat = jnp.pad(flat, (0, rows * FLAT_W - flat.shape[0]))
    return flat.reshape(rows, FLAT_W)


def _unpack(flat, shapes, lead=()):
    flat = flat.reshape(lead + (-1,))
    out, pos = [], 0
    for s in shapes:
        n = math.prod(s)
        out.append(flat[..., pos:pos + n].reshape(lead + tuple(s)))
        pos += n
    return out


def _ffn_pad_rows(a):
    n = a.shape[0] // FFN_HALF
    a = jnp.pad(a.reshape(n, FFN_HALF, a.shape[1]), ((0, 0), (0, FFN_HALF_PAD - FFN_HALF), (0, 0)))
    return a.reshape(n * FFN_HALF_PAD, a.shape[2])


def _ffn_unpad_rows(a):
    n = a.shape[0] // FFN_HALF_PAD
    return a.reshape(n, FFN_HALF_PAD, a.shape[1])[:, :FFN_HALF].reshape(n * FFN_HALF, a.shape[1])


def _ffn_pad_cols(a):
    n = a.shape[1] // FFN_HALF
    a = jnp.pad(a.reshape(a.shape[0], n, FFN_HALF), ((0, 0), (0, 0), (0, FFN_HALF_PAD - FFN_HALF)))
    return a.reshape(a.shape[0], n * FFN_HALF_PAD)


def _ffn_unpad_cols(a):
    n = a.shape[1] // FFN_HALF_PAD
    return a.reshape(a.shape[0], n, FFN_HALF_PAD)[:, :, :FFN_HALF].reshape(a.shape[0], n * FFN_HALF)


def _shard_to_send(name, shard):
    if name == "w_in":
        shard = shard.T
    elif name == "ffn_w_up":
        shard = _ffn_pad_rows(shard.T)
    return shard.astype(MXU_DTYPE)


def _whole_from_gathered(name, g):
    if name == "w_in":
        return _pad_in_proj_rows(g.reshape(IN_DIM, g.shape[2]))
    if name in ("w_br_gdn", "w_br_gla"):
        return jnp.transpose(g, (1, 0, 2)).reshape(g.shape[1], N_DEV * g.shape[2])
    if name == "ffn_w_down":
        return jnp.pad(g, ((0, 0), (0, FFN_HALF_PAD - FFN_HALF), (0, 0))).reshape(FFN_PAD, g.shape[2])
    return g.reshape(N_DEV * g.shape[1], g.shape[2])


def _slots_from_whole(name, gw):
    if name == "w_in":
        return _unpad_in_proj_rows(gw).reshape(N_DEV, IN_DIM // N_DEV, gw.shape[1])
    if name in ("w_br_gdn", "w_br_gla"):
        return jnp.transpose(gw.reshape(gw.shape[0], N_DEV, gw.shape[1] // N_DEV), (1, 0, 2))
    return gw.reshape(N_DEV, gw.shape[0] // N_DEV, gw.shape[1])


def _shard_from_slot(name, s):
    if name == "ffn_w_up":
        return _ffn_unpad_rows(s)
    if name == "ffn_w_down":
        return s[:FFN_HALF]
    return s


def _in_proj_pieces():
    starts, pos = {}, 0
    for n, width in IN_SPLITS:
        starts[n] = (pos, width)
        pos += width
    return [(starts[ref][0], off + lane, starts[ref][1]) for _, off, _, pieces in PAD_SEGS for ref, lane in pieces]


def _pad_in_proj_rows(w):
    rows, at = [], 0
    for src, dst, n in sorted(_in_proj_pieces(), key=lambda p: p[1]):
        if dst > at:
            rows.append(jnp.zeros((dst - at, w.shape[1]), w.dtype))
        rows.append(w[src:src + n])
        at = dst + n
    rows.append(jnp.zeros((IN_PAD - at, w.shape[1]), w.dtype))
    return jnp.concatenate(rows, axis=0)


def _unpad_in_proj_rows(wp):
    return jnp.concatenate([wp[dst:dst + n] for _, dst, n in sorted(_in_proj_pieces())], axis=0)


def _lane_pad(a, width=LANES):
    return jnp.pad(a, ((0, 0), (0, width - a.shape[1])))


def _seg_blk(h, name, rows):
    off, width = SEG[name]
    return (h, rows, width, off // width)


def _ln_both(xs_, ps_):
    (y,) = _ln_fn(xs_, ps_)
    return (y, y)


def _layer_fwd(l, x, x_mx, W, sp, mid=None):
    T = x.shape[0]
    n64, ngla, ntok = T // SSD_CHUNK, T // GLA_BLOCK, T // 256
    h = _mm(f"in_proj_{l}", x_mx, W["w_in"], "nt")
    xbc = _conv_silu_fwd(f"ssd_conv_{l}", h, SEG["xbc"][0], sp["ssd_conv_w"], sp["ssd_conv_b"])
    gqkv = _conv_silu_fwd(f"gdn_conv_{l}", h, SEG["gqkv"][0], sp["gdn_conv_w"], None)

    ssd_in = [(xbc, SSD_CHUNK, SSD_XBC, 0), _seg_blk(h, "dt", SSD_CHUNK), _seg_blk(h, "z", SSD_CHUNK)]
    ssd_p = [sp["ssd_dt_bias"], sp["ssd_a_log"], sp["ssd_d"], sp["ssd_norm_w"]]
    o_ssd, ssd_states = _chain_fwd(f"ssd_fwd_{l}", _ssd_chunk, n64, ssd_in, ssd_p, [(SSD_CHUNK, SSD_INNER, MXU_DTYPE)],
                                   (SSD_STATE, SSD_INNER))
    o_gdn, gdn_saved = _gdn_forward(str(l), gqkv, h, sp)
    gla_in = [_seg_blk(h, "lqkv", GLA_BLOCK), _seg_blk(h, "lglr", GLA_BLOCK), _seg_blk(h, "lr", GLA_BLOCK)]
    gla_p = [jnp.pad(sp["gla_gate_w2"], ((0, LANES - GLA_RANK), (0, 0))), sp["gla_gate_b"], sp["gla_norm_w"]]
    o_gla, gla_states = _chain_fwd(f"gla_fwd_{l}", _gla_block, ngla, gla_in, gla_p, [(GLA_BLOCK, GLA_V, MXU_DTYPE)],
                                   (GLA_VAL_DIM, GLA_K))
    y_ssd = _mm(f"br_ssd_{l}", o_ssd, W["w_br_ssd"])
    y_gdn = _mm(f"br_gdn_{l}", o_gdn, W["w_br_gdn"])
    y_gla = _mm(f"br_gla_{l}", o_gla, W["w_br_gla"])
    merge_in = [_seg_blk(h, "gates", 256), (y_ssd, 256, D_MODEL, 0), (y_gdn, 256, D_MODEL, 0), (y_gla, 256, D_MODEL, 0)]
    (mix,) = _chain_fwd(f"merge_{l}", _merge_fn, ntok, merge_in, [], [(256, D_MODEL, MXU_DTYPE)])
    r1 = _mm(f"out_proj_{l}", mix, W["w_out"])
    ln1_p = [sp["ln1_g"], sp["ln1_b"]]
    if mid is not None:
        ln1_p[0] = ln1_p[0] + mid(o_gla)[0:1, 0:1]
    both = [(256, D_MODEL, F32), (256, D_MODEL, MXU_DTYPE)]
    x1, x1_mx = _chain_fwd(f"ln1_{l}", _ln_both, ntok, [(x, 256, D_MODEL, 0), (r1, 256, D_MODEL, 0)], ln1_p, both)
    up = _mm(f"ffn_up_{l}", x1_mx, W["ffn_w_up"], "nt")
    act = _ffn_glu_fwd(f"ffn_glu_{l}", up, sp["ffn_conv_w_pad"], sp["ffn_conv_b_pad"], MXU_DTYPE)
    r2 = _mm(f"ffn_down_{l}", act, W["ffn_w_down"])
    ln2_p = [sp["ln2_g"], sp["ln2_b"]]
    x2, x2_mx = _chain_fwd(f"ln2_{l}", _ln_both, ntok, [(x1, 256, D_MODEL, 0), (r2, 256, D_MODEL, 0)], ln2_p, both)
    saved = dict(x=x, x_mx=x_mx, h=h, xbc=xbc, gqkv=gqkv, ssd_in=ssd_in, ssd_p=ssd_p, ssd_states=ssd_states,
                 gdn=gdn_saved, gla_in=gla_in, gla_p=gla_p, gla_states=gla_states, o_ssd=o_ssd,
                 o_gdn=o_gdn, o_gla=o_gla, merge_in=merge_in, mix=mix, r1=r1, ln1_p=ln1_p, x1=x1, x1_mx=x1_mx, up=up, act=act,
                 r2=r2, ln2_p=ln2_p)
    return x2, x2_mx, saved


def _layer_bwd(l, dx2, W, sp, sv, first=None, mid=None):
    T = dx2.shape[0]
    n64, ngla, ntok = T // SSD_CHUNK, T // GLA_BLOCK, T // 256
    bf = MXU_DTYPE
    gw, gs = {}, {}
    ln2_p = list(sv["ln2_p"])
    if first is not None:
        ln2_p[0] = ln2_p[0] + first[0:1, 0:1]
    (dx1_a, dr2), (gs["ln2_g"], gs["ln2_b"]) = _chain_bwd(
        f"ln2_bwd_{l}", _ln_fn, ntok, [(sv["x1"], 256, D_MODEL, 0), (sv["r2"], 256, D_MODEL, 0)], ln2_p,
        [(dx2, 256, D_MODEL)], dx_dtypes=[F32, bf])
    gw["ffn_w_down"] = _mm(f"ffn_down_dw_{l}", sv["act"], dr2, "tn")
    dact = _mm(f"ffn_down_dx_{l}", dr2, W["ffn_w_down"], "nt")
    dg, du, dwg, dwu, dbg, dbu = _ffn_glu_bwd(f"ffn_glu_bwd_{l}", sv["up"], sp["ffn_conv_w_pad"], sp["ffn_conv_b_pad"], dact, bf)
    gs["ffn_conv_w"] = _ffn_unpad_cols(jnp.concatenate([dwg, dwu], axis=1))
    gs["ffn_conv_b"] = _ffn_unpad_cols(jnp.concatenate([dbg, dbu], axis=1))
    dup = jnp.concatenate([dg, du], axis=1)
    gw["ffn_w_up"] = _mm(f"ffn_up_dw_{l}", dup, sv["x1_mx"], "tn", tn=1024)
    dx1_b = _mm(f"ffn_up_dx_{l}", dup, W["ffn_w_up"], "nn", tn=1024, tk=1024)
    ln1_p = list(sv["ln1_p"])
    if mid is not None:
        ln1_p[0] = ln1_p[0] + mid(dx1_b)[0:1, 0:1]
    (dx_a, dr1), (gs["ln1_g"], gs["ln1_b"]) = _chain_bwd(
        f"ln1_bwd_{l}", _ln_sum_fn, ntok, [(sv["x"], 256, D_MODEL, 0), (sv["r1"], 256, D_MODEL, 0)], ln1_p,
        [(dx1_a, 256, D_MODEL), (dx1_b, 256, D_MODEL)], dx_dtypes=[F32, bf])
    gw["w_out"] = _mm(f"out_proj_dw_{l}", sv["mix"], dr1, "tn")
    dmix = _mm(f"out_proj_dx_{l}", dr1, W["w_out"], "nt")
    (dgates, dy_ssd, dy_gdn, dy_gla), _ = _chain_bwd(f"merge_bwd_{l}", _merge_fn, ntok, sv["merge_in"], [],
                                                     [(dmix, 256, D_MODEL)], dx_dtypes=[bf, bf, bf, bf])
    gw["w_br_ssd"] = _mm(f"br_ssd_dw_{l}", sv["o_ssd"], dy_ssd, "tn")
    gw["w_br_gdn"] = _mm(f"br_gdn_dw_{l}", sv["o_gdn"], dy_gdn, "tn")
    gw["w_br_gla"] = _mm(f"br_gla_dw_{l}", sv["o_gla"], dy_gla, "tn")
    do_ssd = _mm(f"br_ssd_dx_{l}", dy_ssd, W["w_br_ssd"], "nt")
    do_gdn = _mm(f"br_gdn_dx_{l}", dy_gdn, W["w_br_gdn"], "nt")
    do_gla = _mm(f"br_gla_dx_{l}", dy_gla, W["w_br_gla"], "nt")

    (dxbc, ddt, dz), dps = _chain_bwd(f"ssd_bwd_{l}", _ssd_chunk, n64, sv["ssd_in"], sv["ssd_p"],
                                      [(do_ssd, SSD_CHUNK, SSD_INNER)], sprev=sv["ssd_states"], dx_dtypes=[F32, bf, bf])
    gs["ssd_dt_bias"], gs["ssd_a_log"], gs["ssd_d"], gs["ssd_norm_w"] = dps
    dgqkv, dgab, dgg, gs["gdn_a_log"], gs["gdn_dt_bias"], gs["gdn_norm_w"] = _gdn_backward(str(l), do_gdn, sv["gdn"], bf)
    (dlqkv, dlglr, dlr), dps = _chain_bwd(f"gla_bwd_{l}", _gla_block, ngla, sv["gla_in"], sv["gla_p"],
                                          [(do_gla, GLA_BLOCK, GLA_V)], sprev=sv["gla_states"], dx_dtypes=[bf, bf, bf])
    gs["gla_gate_w2"], gs["gla_gate_b"], gs["gla_norm_w"] = dps[0][:GLA_RANK], dps[1], dps[2]
    dxbc_pre, gs["ssd_conv_w"], gs["ssd_conv_b"] = _conv_silu_bwd(
        f"ssd_conv_bwd_{l}", sv["h"], SEG["xbc"][0], sp["ssd_conv_w"], sp["ssd_conv_b"], dxbc, bf)
    dgqkv_pre, gs["gdn_conv_w"] = _conv_silu_bwd(f"gdn_conv_bwd_{l}", sv["h"], SEG["gqkv"][0], sp["gdn_conv_w"], None, dgqkv, bf)
    pieces = dict(gates=dgates, xbc=dxbc_pre, gqkv=dgqkv_pre, z=dz, lqkv=dlqkv, gg=dgg, lr=dlr, dt=ddt, gab=dgab, lglr=dlglr)
    cols = [pieces[name] for name, _, _, _ in PAD_SEGS]
    cols.append(jnp.zeros((T, IN_PAD - PAD_SEGS[-1][1] - PAD_SEGS[-1][2]), bf))
    dh = jnp.concatenate(cols, axis=1)
    gw["w_in"] = _mm(f"in_proj_dw_{l}", dh, sv["x_mx"], "tn", tn=1024)
    dx_b = _mm(f"in_proj_dx_{l}", dh, W["w_in"], "nn", tn=1024, tk=1024)
    dx = _add_blocks(f"dx_add_{l}", dx_a[None], dx_b[None])[0]
    return dx, gw, gs


def _ln_sum_fn(xs_, ps_):
    (y,) = _ln_fn(xs_, ps_)
    return (y, y)


def _small_2d(name, a):
    return a.reshape(1, -1) if a.ndim == 1 else a


def kernel(x, w_in, ssd_conv_w, ssd_conv_b, ssd_dt_bias, ssd_a_log, ssd_d, ssd_norm_w, gdn_conv_w, gdn_a_log, gdn_dt_bias, gdn_norm_w, gla_gate_w2, gla_gate_b, gla_norm_w, w_br_ssd, w_br_gdn, w_br_gla, w_out, ln1_g, ln1_b, ffn_w_up, ffn_conv_w, ffn_conv_b, ffn_w_down, ln2_g, ln2_b, loss_target, m_w_in, m_ssd_conv_w, m_ssd_conv_b, m_ssd_dt_bias, m_ssd_a_log, m_ssd_d, m_ssd_norm_w, m_gdn_conv_w, m_gdn_a_log, m_gdn_dt_bias, m_gdn_norm_w, m_gla_gate_w2, m_gla_gate_b, m_gla_norm_w, m_w_br_ssd, m_w_br_gdn, m_w_br_gla, m_w_out, m_ln1_g, m_ln1_b, m_ffn_w_up, m_ffn_conv_w, m_ffn_conv_b, m_ffn_w_down, m_ln2_g, m_ln2_b, v_w_in, v_ssd_conv_w, v_ssd_conv_b, v_ssd_dt_bias, v_ssd_a_log, v_ssd_d, v_ssd_norm_w, v_gdn_conv_w, v_gdn_a_log, v_gdn_dt_bias, v_gdn_norm_w, v_gla_gate_w2, v_gla_gate_b, v_gla_norm_w, v_w_br_ssd, v_w_br_gdn, v_w_br_gla, v_w_out, v_ln1_g, v_ln1_b, v_ffn_w_up, v_ffn_conv_w, v_ffn_conv_b, v_ffn_w_down, v_ln2_g, v_ln2_b):
    args = locals()
    w = {n: args[n] for n in WEIGHTS}
    m = {n: args["m_" + n] for n in WEIGHTS}
    v = {n: args["v_" + n] for n in WEIGHTS}
    dev = 4 * lax.axis_index("x") + 2 * lax.axis_index("y") + lax.axis_index("c")
    xl = x[0]
    tgt = loss_target[0]

    nbig = len(BIG)

    def whole_weights(got):
        return {n: _whole_from_gathered(n, g) for n, g in zip(BIG, got)}

    zones = [lax.dynamic_update_slice_in_dim(lax.empty((N_DEV,) + s.shape, s.dtype), s[None], dev, axis=0)
             for s in (_shard_to_send(n, w[n][1]) for n in BIG)]
    gather1, gather1_token = _exchange_start("gather_w_1_chips_start", _routes_block_to_chips, 3, zones, None)
    got0 = _all_gather("gather_w_0", [_shard_to_send(n, w[n][0]) for n in BIG] + [w[n] for n in SMALL_SHARDED])
    W = [whole_weights(got0[:nbig]), None]
    whole = dict(w)
    for n, s in zip(SMALL_SHARDED, got0[nbig:]):
        whole[n] = jnp.transpose(s, (1, 2, 0, 3)).reshape(s.shape[1], s.shape[2], N_DEV * s.shape[3])
    SP = [{n: _small_2d(n, whole[n][l]) for n in SMALL} for l in range(DEPTH)]
    for sp in SP:
        sp["ffn_conv_w_pad"] = _ffn_pad_cols(sp["ffn_conv_w"])
        sp["ffn_conv_b_pad"] = _ffn_pad_cols(sp["ffn_conv_b"])

    held = {}

    def gather1_mid(mixed):
        zones = _exchange_wait("gather_w_1_chips_wait", _routes_block_to_chips, 3, nbig, gather1, mixed)
        held["gather1"], token = _exchange_start("gather_w_1_sibling_start", _routes_blocks_to_sibling, 4, zones, None)
        return token

    saved = [None] * DEPTH
    act, act_mx, saved[0] = _layer_fwd(0, xl, (xl + gather1_token[0, 0]).astype(MXU_DTYPE), W[0], SP[0], mid=gather1_mid)
    W[1] = whole_weights(_exchange_wait("gather_w_1_sibling_wait", _routes_blocks_to_sibling, 4, nbig, held["gather1"], act))
    act, act_mx, saved[1] = _layer_fwd(1, act, act_mx, W[1], SP[1])
    dy, loss_parts = _loss_head(act, tgt)
    loss = lax.psum(jnp.sum(loss_parts), ("x", "y", "c"))

    def slots_of(gw):
        return [_slots_from_whole(n, gw[n]) for n in BIG]

    grads = {}
    GS, red = [None] * DEPTH, [None] * DEPTH
    dy, gw, GS[1] = _layer_bwd(1, dy, W[1], SP[1], saved[1])
    slots1 = slots_of(gw)
    swap1, swap1_token = _exchange_start("rs_swap_1_start", _routes_to_sibling, 4, slots1,
                                         [lax.empty((4,) + s.shape[1:], s.dtype) for s in slots1])

    def reduce1_mid(ffn_done):
        from_sibling = _exchange_wait("rs_swap_1_wait", _routes_to_sibling, 4, nbig, swap1, ffn_done)
        held["sums1"] = [_pair_add(f"rs_add_1_{i}", g, o, lax.axis_index("c"), 2 * lax.axis_index("x") + lax.axis_index("y"))
                         for i, (g, o) in enumerate(zip(slots1, from_sibling))]
        partials = [s[0] for s in held["sums1"]]
        held["chips1"], token = _exchange_start("rs_chips_1_start", _routes_to_chips, 3, partials,
                                                [lax.empty((3,) + p.shape[1:], p.dtype) for p in partials])
        return token

    dy, gw, GS[0] = _layer_bwd(0, dy, W[0], SP[0], saved[0], first=swap1_token, mid=reduce1_mid)
    got1 = _exchange_wait("rs_chips_1_wait", _routes_to_chips, 3, nbig, held["chips1"], dy)
    red[1] = [_sum4(f"rs_sum_1_{i}", s[1], q) for i, (s, q) in enumerate(zip(held["sums1"], got1))]
    red[0] = _reduce_scatter("0", slots_of(gw))
    grad_x = dy[None]
    kept_t = ("w_in", "ffn_w_up")
    grads_k = {n: jnp.stack([_shard_from_slot(n, red[l][i]) for l in range(DEPTH)]) for i, n in enumerate(BIG)}

    small_shapes = [whole[n].shape for n in SMALL]
    gs_flat = _pack([jnp.stack([GS[l][n].reshape(whole[n].shape[1:]) for l in range(DEPTH)]) for n in SMALL], F32)
    (gs_all,) = _all_gather("gather_small_grads", [gs_flat])

    def mine(n, a):
        if n in SMALL_SHARDED:
            cs = a.shape[-1] // N_DEV
            return lax.dynamic_slice_in_dim(a, dev * cs, cs, axis=a.ndim - 1)
        return a

    m_whole, v_whole = {}, {}
    for n in SMALL:
        if n in SMALL_SHARDED:
            cs = w[n].shape[-1]
            zeros = jnp.zeros(whole[n].shape, F32)
            m_whole[n] = lax.dynamic_update_slice_in_dim(zeros, m[n], dev * cs, axis=2)
            v_whole[n] = lax.dynamic_update_slice_in_dim(zeros, v[n], dev * cs, axis=2)
        else:
            m_whole[n], v_whole[n] = m[n], v[n]
    outs = _adamw_small(gs_all, _pack([whole[n] for n in SMALL], F32), _pack([m_whole[n] for n in SMALL], F32),
                        _pack([v_whole[n] for n in SMALL], F32))
    g_s, d_s, m_s, v_s = [_unpack(o, small_shapes) for o in outs]
    delta, new_m, new_v = {}, {}, {}
    for i, n in enumerate(SMALL):
        grads[n], delta[n], new_m[n], new_v[n] = mine(n, g_s[i]), mine(n, d_s[i]), mine(n, m_s[i]), mine(n, v_s[i])
    for n in BIG:
        view = (lambda a: jnp.transpose(a, (0, 2, 1))) if n in kept_t else (lambda a: a)
        outs = _adamw(f"adamw_{n}", view(w[n]), grads_k[n], view(m[n]), view(v[n]))
        grads[n], delta[n], new_m[n], new_v[n] = view(grads_k[n]), view(outs[0]), view(outs[1]), view(outs[2])

    return (loss, grad_x, *[grads[n] for n in WEIGHTS], *[delta[n] for n in WEIGHTS], *[new_m[n] for n in WEIGHTS],
            *[new_v[n] for n in WEIGHTS])
```

```python
import functools
import math

import jax
import jax.numpy as jnp
from jax import lax
from jax.experimental import pallas as pl
from jax.experimental.pallas import tpu as pltpu

F32 = jnp.float32
MXU_DTYPE = jnp.bfloat16
HI = lax.Precision.HIGHEST

N_DEV = 8
D_MODEL = 1024
DEPTH = 2
SSD_HEADS, SSD_HEAD_DIM, SSD_INNER, SSD_GROUPS, SSD_STATE, SSD_CHUNK = 16, 64, 1024, 2, 128, 64
SSD_XBC = SSD_INNER + 2 * SSD_GROUPS * SSD_STATE
GDN_HEADS, GDN_HEAD_DIM, GDN_WIDTH, GDN_CHUNK = 4, 128, 512, 64
GLA_HEADS, GLA_KEY_DIM, GLA_VAL_DIM, GLA_K, GLA_V, GLA_RANK, GLA_CHUNK = 4, 64, 128, 256, 512, 16, 16
GLA_BLOCK = 128
GLA_NORMALIZER = 16.0
FFN_DIM = 2816
FFN_HALF = FFN_DIM // 8
FFN_HALF_PAD = 384
FFN_UP_PAD = 16 * FFN_HALF_PAD
FFN_PAD = FFN_UP_PAD // 2
ALPHA = (2 * DEPTH) ** 0.25
LN_EPS = 1e-5
RMS_EPS = 1e-6
ADAM_LR, ADAM_B1, ADAM_B2, ADAM_EPS, ADAM_WD, ADAM_STEP = 0.001, 0.9, 0.999, 1e-08, 0.01, 10
LANES = 128
NEG_BIG = -1e30
VMEM_LIMIT = 56 * 1024 * 1024

IN_SPLITS = (("z", 1024), ("xbc", 1536), ("dt", 16), ("gqkv", 1536), ("ga", 4), ("gb", 4), ("gg", 512),
             ("lqkv", 1024), ("lglr", 16), ("lr", 512), ("gates", 3072))
IN_DIM = sum(w for _, w in IN_SPLITS)
PAD_SEGS = (("gates", 0, 3072, (("gates", 0),)), ("xbc", 3072, 1536, (("xbc", 0),)),
            ("gqkv", 4608, 1536, (("gqkv", 0),)), ("z", 6144, 1024, (("z", 0),)),
            ("lqkv", 7168, 1024, (("lqkv", 0),)), ("gg", 8192, 512, (("gg", 0),)), ("lr", 8704, 512, (("lr", 0),)),
            ("dt", 9216, 128, (("dt", 0),)), ("gab", 9344, 128, (("ga", 0), ("gb", 4))), ("lglr", 9472, 128, (("lglr", 0),)))
IN_PAD = 9728
SEG = {name: (off, width) for name, off, width, _ in PAD_SEGS}

BIG = ("w_in", "w_br_ssd", "w_br_gdn", "w_br_gla", "w_out", "ffn_w_up", "ffn_w_down")
COL_SHARDED = ("w_in", "w_br_gdn", "w_br_gla", "ffn_w_up")
SMALL_SHARDED = ("ssd_conv_w", "gdn_conv_w", "gla_gate_w2", "ffn_conv_w")
WEIGHTS = ("w_in", "ssd_conv_w", "ssd_conv_b", "ssd_dt_bias", "ssd_a_log", "ssd_d", "ssd_norm_w", "gdn_conv_w",
           "gdn_a_log", "gdn_dt_bias", "gdn_norm_w", "gla_gate_w2", "gla_gate_b", "gla_norm_w", "w_br_ssd", "w_br_gdn",
           "w_br_gla", "w_out", "ln1_g", "ln1_b", "ffn_w_up", "ffn_conv_w", "ffn_conv_b", "ffn_w_down", "ln2_g", "ln2_b")
SMALL = tuple(n for n in WEIGHTS if n not in BIG)
FLAT_W = 512


def _cparams(sem=None):
    kw = dict(vmem_limit_bytes=VMEM_LIMIT)
    if sem is not None:
        kw["dimension_semantics"] = sem
    return pltpu.CompilerParams(**kw)


_DIMS = {"nn": (((1,), (0,)), ((), ())), "nt": (((1,), (1,)), ((), ())), "tn": (((0,), (0,)), ((), ()))}


def _dot(a, b, dims="nn"):
    if MXU_DTYPE == F32:
        return lax.dot_general(a.astype(F32), b.astype(F32), _DIMS[dims], precision=HI, preferred_element_type=F32)
    return lax.dot_general(a.astype(MXU_DTYPE), b.astype(MXU_DTYPE), _DIMS[dims], preferred_element_type=F32)


def _dot_hi(a, b, dims="nn"):
    return lax.dot_general(a.astype(F32), b.astype(F32), _DIMS[dims], precision=HI, preferred_element_type=F32)


def _iota2(shape, axis):
    return lax.broadcasted_iota(jnp.int32, shape, axis)


def _tril(n, strict=False):
    r, c = _iota2((n, n), 0), _iota2((n, n), 1)
    return (r > c) if strict else (r >= c)


def _raw_dot(a, b, dims):
    return lax.dot_general(a, b, _DIMS[dims], preferred_element_type=F32)


def _dot_x3(a, b, dims="nn"):
    if MXU_DTYPE == F32:
        return _dot_hi(a, b, dims)
    ah, bh = a.astype(jnp.bfloat16), b.astype(jnp.bfloat16)
    al, bl = (a - ah.astype(F32)).astype(jnp.bfloat16), (b - bh.astype(F32)).astype(jnp.bfloat16)
    return _raw_dot(ah, bh, dims) + (_raw_dot(ah, bl, dims) + _raw_dot(al, bh, dims))


def _exact_dot(mask, b, dims, mask_first):
    if MXU_DTYPE == F32:
        return _dot_hi(mask, b, dims) if mask_first else _dot_hi(b, mask, dims)
    m = mask.astype(jnp.bfloat16)
    b1 = b.astype(jnp.bfloat16)
    r1 = b - b1.astype(F32)
    b2 = r1.astype(jnp.bfloat16)
    b3 = (r1 - b2.astype(F32)).astype(jnp.bfloat16)
    if mask_first:
        return _raw_dot(m, b1, dims) + (_raw_dot(m, b2, dims) + _raw_dot(m, b3, dims))
    return _raw_dot(b1, m, dims) + (_raw_dot(b2, m, dims) + _raw_dot(b3, m, dims))


@jax.custom_vjp
def _mask_left(mask, b):
    return _exact_dot(mask, b, "nn", True)


_mask_left.defvjp(lambda mask, b: (_mask_left(mask, b), mask),
                  lambda mask, d: (jnp.zeros_like(mask), _exact_dot(mask, d, "tn", True)))


@jax.custom_vjp
def _mask_right(a, mask):
    return _exact_dot(mask, a, "nn", False)


_mask_right.defvjp(lambda a, mask: (_mask_right(a, mask), mask),
                   lambda mask, d: (_exact_dot(mask, d, "nt", False), jnp.zeros_like(mask)))


@jax.custom_vjp
def _unit_lower_inverses(mats):
    n = mats[0].shape[0]
    eye = (_iota2((n, n), 0) == _iota2((n, n), 1)).astype(F32)
    xs = [eye - a for a in mats]
    ps = list(mats)
    k = 2
    while k < n:
        ps = [_dot_x3(p, p) for p in ps]
        xs = [x + _dot_x3(x, p) for x, p in zip(xs, ps)]
        k *= 2
    return xs


def _unit_lower_inverses_fwd(mats):
    ts = _unit_lower_inverses(mats)
    return ts, ts


def _unit_lower_inverses_bwd(ts, dts):
    mids = [_dot_x3(t, d, "tn") for t, d in zip(ts, dts)]
    return ([-_dot_x3(m, t, "nt") for m, t in zip(mids, ts)],)


_unit_lower_inverses.defvjp(_unit_lower_inverses_fwd, _unit_lower_inverses_bwd)


def _ssd_chunk(xs_, ps_, s_t):
    xbc, dtraw, z = xs_
    dt_bias, a_log, d_skip, norm_w = ps_
    L = xbc.shape[0]
    H, P, N, G = SSD_HEADS, SSD_HEAD_DIM, SSD_STATE, SSD_GROUPS
    W = SSD_INNER // G
    xs = xbc[:, :SSD_INNER]
    bm = xbc[:, SSD_INNER:SSD_INNER + G * N]
    cm = xbc[:, SSD_INNER + G * N:]
    dt = jax.nn.softplus(dtraw[:, :H] + dt_bias)
    a = dt * (-jnp.exp(a_log))
    causal = _tril(L)
    a_cs = _mask_left(causal.astype(F32), a)
    expand = (_iota2((H, SSD_INNER), 1) // P == _iota2((H, SSD_INNER), 0)).astype(F32)
    wide = _mask_right(jnp.concatenate([a_cs, dt, jnp.broadcast_to(d_skip, (L, H))], axis=0), expand)
    a_cs_x, dt_x, d_x = wide[:L], wide[L:2 * L], wide[2 * L:]
    a_end_x = a_cs_x[L - 1:L, :]
    a_cs_t, dt_t = a_cs.T, dt.T
    cb = [_dot(cm[:, g * N:(g + 1) * N], bm[:, g * N:(g + 1) * N], "nt") for g in range(G)]
    decay = [jnp.exp(jnp.where(causal, a_cs[:, h:h + 1] - a_cs_t[h:h + 1, :], NEG_BIG)) * dt_t[h:h + 1, :] for h in range(H)]
    ws = [cb[h // (H // G)] * decay[h] for h in range(H)]
    y = jnp.concatenate([_dot(ws[h], xs[:, h * P:(h + 1) * P]) for h in range(H)], axis=1)
    y_in = jnp.concatenate([_dot(cm[:, g * N:(g + 1) * N], s_t[:, g * W:(g + 1) * W]) for g in range(G)], axis=1)
    y = y + y_in * jnp.exp(a_cs_x) + d_x * xs
    xw = xs * (jnp.exp(a_end_x - a_cs_x) * dt_x)
    st = jnp.concatenate([_dot(bm[:, g * N:(g + 1) * N], xw[:, g * W:(g + 1) * W], "tn") for g in range(G)], axis=1)
    s_new = s_t * jnp.exp(a_end_x) + st
    yg = y * jax.nn.silu(z)
    outs = []
    for g in range(G):
        part = yg[:, g * W:(g + 1) * W]
        outs.append(part * lax.rsqrt(jnp.mean(part * part, axis=1, keepdims=True) + RMS_EPS))
    return (jnp.concatenate(outs, axis=1) * norm_w,), s_new


GDN_PREP_CHUNKS = 4


def _gdn_prep(xs_, ps_):
    qkv, ab = xs_
    a_log, dt_bias = ps_
    B = qkv.shape[0]
    H, D, L = GDN_HEADS, GDN_HEAD_DIM, GDN_CHUNK
    g_all = -jnp.exp(a_log) * jax.nn.softplus(ab + dt_bias)
    row, col = _iota2((B, B), 0), _iota2((B, B), 1)
    g_cs = _mask_left((((row // L) == (col // L)) & (row >= col)).astype(F32), g_all)
    g_cs_t = g_cs.T
    beta_all = jax.nn.sigmoid(ab)
    incl, strict = _tril(L), _tril(L, strict=True)
    qs, ks, vs = [], [], []
    for h in range(H):
        q = qkv[:, h * D:(h + 1) * D]
        k = qkv[:, GDN_WIDTH + h * D:GDN_WIDTH + (h + 1) * D]
        qs.append(q * lax.rsqrt(jnp.sum(q * q, axis=1, keepdims=True) + RMS_EPS) * (D ** -0.5))
        ks.append(k * lax.rsqrt(jnp.sum(k * k, axis=1, keepdims=True) + RMS_EPS))
        vs.append(qkv[:, 2 * GDN_WIDTH + h * D:2 * GDN_WIDTH + (h + 1) * D])
    pairs = [(c, h) for c in range(B // L) for h in range(H)]
    rows = {c: slice(c * L, (c + 1) * L) for c in range(B // L)}
    q_ = {(c, h): qs[h][rows[c]] for c, h in pairs}
    k_ = {(c, h): ks[h][rows[c]] for c, h in pairs}
    col_ = {(c, h): g_cs[rows[c], h:h + 1] for c, h in pairs}
    beta_ = {(c, h): beta_all[rows[c], H + h:H + h + 1] for c, h in pairs}
    gamma = {p: jnp.exp(jnp.where(incl, col_[p] - g_cs_t[p[1]:p[1] + 1, rows[p[0]]], NEG_BIG)) for p in pairs}
    kb = {p: k_[p] * beta_[p] for p in pairs}
    a_mat = [jnp.where(strict, _dot(kb[p], k_[p], "nt") * gamma[p], 0.0) for p in pairs]
    attn = {p: jnp.where(incl, _dot(q_[p], k_[p], "nt") * gamma[p], 0.0) for p in pairs}
    t_mat = dict(zip(pairs, _unit_lower_inverses(a_mat)))
    u = {p: _dot(t_mat[p], vs[p[1]][rows[p[0]]] * beta_[p]) for p in pairs}
    w = {p: _dot(t_mat[p], kb[p] * jnp.exp(col_[p])) for p in pairs}
    qd = {p: q_[p] * jnp.exp(col_[p]) for p in pairs}
    kd = {p: k_[p] * jnp.exp(col_[p][L - 1:L, :] - col_[p]) for p in pairs}

    def whole(parts):
        return jnp.concatenate([jnp.concatenate([parts[(c, h)] for h in range(H)], axis=1) for c in range(B // L)], axis=0)

    return (whole(u), whole(w), whole(qd), whole(kd), whole(attn), g_cs)


def _gdn_scan(xs_, ps_, s):
    u, w, qd, kd, attn, g_cs, gate = xs_
    (norm_w,) = ps_
    L = u.shape[0]
    H, D = GDN_HEADS, GDN_HEAD_DIM
    heads = range(H)
    lanes = [slice(h * D, (h + 1) * D) for h in heads]
    s_h = [s[lanes[h], :] for h in heads]
    v_new = [u[:, lanes[h]] - _dot(w[:, lanes[h]], s_h[h]) for h in heads]
    o = [_dot(qd[:, lanes[h]], s_h[h]) + _dot(attn[:, h * L:(h + 1) * L], v_new[h]) for h in heads]
    decay = [jnp.exp(g_cs[L - 1:L, h:h + 1]) for h in heads]
    s_new = [s_h[h] * decay[h] + _dot(kd[:, lanes[h]], v_new[h], "tn") for h in heads]
    o = [o[h] * lax.rsqrt(jnp.mean(o[h] * o[h], axis=1, keepdims=True) + RMS_EPS) * norm_w * jax.nn.silu(gate[:, lanes[h]])
         for h in heads]
    return (jnp.concatenate(o, axis=1),), jnp.concatenate(s_new, axis=0)


def _gdn_forward(tag, gqkv, h, sp):
    T = gqkv.shape[0]
    blk = GDN_PREP_CHUNKS * GDN_CHUNK
    prep_in = [(gqkv, blk, 3 * GDN_WIDTH, 0), _seg_blk(h, "gab", blk)]
    prep_p = [_lane_pad(sp["gdn_a_log"]), _lane_pad(sp["gdn_dt_bias"])]
    mx = MXU_DTYPE
    prep = _chain_fwd(f"gdn_prep_{tag}", _gdn_prep, T // blk, prep_in, prep_p,
                      [(blk, GDN_WIDTH, F32), (blk, GDN_WIDTH, mx), (blk, GDN_WIDTH, mx), (blk, GDN_WIDTH, mx),
                       (blk, GDN_HEADS * GDN_CHUNK, mx), (blk, LANES, F32)])
    widths = [GDN_WIDTH] * 4 + [GDN_HEADS * GDN_CHUNK, LANES]
    scan_in = [(a, GDN_CHUNK, wd, 0) for a, wd in zip(prep, widths)] + [_seg_blk(h, "gg", GDN_CHUNK)]
    scan_p = [sp["gdn_norm_w"]]
    o, states = _chain_fwd(f"gdn_scan_{tag}", _gdn_scan, T // GDN_CHUNK, scan_in, scan_p, [(GDN_CHUNK, GDN_WIDTH, mx)],
                           (GDN_WIDTH, GDN_HEAD_DIM))
    return o, dict(prep_in=prep_in, prep_p=prep_p, scan_in=scan_in, scan_p=scan_p, states=states, widths=widths)


def _gdn_backward(tag, do, sv, dx_dtype):
    T = do.shape[0]
    blk = GDN_PREP_CHUNKS * GDN_CHUNK
    dscan, (dnorm,) = _chain_bwd(f"gdn_scan_bwd_{tag}", _gdn_scan, T // GDN_CHUNK, sv["scan_in"], sv["scan_p"],
                                 [(do, GDN_CHUNK, GDN_WIDTH)], sprev=sv["states"], dx_dtypes=[F32] * 6 + [dx_dtype])
    douts = [(d, blk, wd) for d, wd in zip(dscan[:6], sv["widths"])]
    (dgqkv, dgab), (da_log, ddt_bias) = _chain_bwd(f"gdn_prep_bwd_{tag}", _gdn_prep, T // blk, sv["prep_in"], sv["prep_p"],
                                                   douts, dx_dtypes=[F32, dx_dtype])
    return dgqkv, dgab, dscan[6], da_log[:, :GDN_HEADS], ddt_bias[:, :GDN_HEADS], dnorm


def _gla_block(xs_, ps_, s_t):
    qkv, glr, r = xs_
    w2, gate_b, norm_w = ps_
    B = qkv.shape[0]
    H, K, V, C = GLA_HEADS, GLA_KEY_DIM, GLA_VAL_DIM, GLA_CHUNK
    q = qkv[:, :GLA_K] * (K ** -0.5)
    k = qkv[:, GLA_K:2 * GLA_K]
    v = qkv[:, 2 * GLA_K:]
    gk = jax.nn.log_sigmoid(_dot(glr, w2) + gate_b) / GLA_NORMALIZER
    row, col = _iota2((B, B), 0), _iota2((B, B), 1)
    same = (row // C) == (col // C)
    mask = same & (row >= col)
    b_cs = _mask_left(mask.astype(F32), gk)
    b_end = _mask_left((col == (row // C) * C + (C - 1)).astype(F32), b_cs)
    q_e = q * jnp.exp(b_cs)
    k_e = k * jnp.exp(-b_cs)
    k_d = k * jnp.exp(b_end - b_cs)
    intra = []
    for h in range(H):
        a_mat = jnp.where(mask, _dot(q_e[:, h * K:(h + 1) * K], k_e[:, h * K:(h + 1) * K], "nt"), 0.0)
        intra.append(_dot(a_mat, v[:, h * V:(h + 1) * V]))
    o = jnp.concatenate(intra, axis=1)
    chunks = [slice(j * C, (j + 1) * C) for j in range(B // C)]
    fresh = [jnp.concatenate([_dot(v[sl, h * V:(h + 1) * V], k_d[sl, h * K:(h + 1) * K], "tn") for h in range(H)], axis=1)
             for sl in chunks]
    entering = []
    for j, sl in enumerate(chunks):
        entering.append(s_t)
        s_t = s_t * jnp.exp(b_end[j * C:j * C + 1, :]) + fresh[j]
    inter = [jnp.concatenate([_dot(q_e[sl, h * K:(h + 1) * K], entering[j][:, h * K:(h + 1) * K], "nt") for h in range(H)],
                             axis=1) for j, sl in enumerate(chunks)]
    o = o + jnp.concatenate(inter, axis=0)
    outs = []
    for h in range(H):
        oh = o[:, h * V:(h + 1) * V]
        oh = oh * lax.rsqrt(jnp.mean(oh * oh, axis=1, keepdims=True) + RMS_EPS) * norm_w
        outs.append(oh * jax.nn.silu(r[:, h * V:(h + 1) * V]))
    return (jnp.concatenate(outs, axis=1),), s_t


def _merge_fn(xs_, ps_):
    gates, y_ssd, y_gdn, y_gla = xs_
    d = D_MODEL
    return (jax.nn.sigmoid(gates[:, :d]) * y_ssd + jax.nn.sigmoid(gates[:, d:2 * d]) * y_gdn
            + jax.nn.sigmoid(gates[:, 2 * d:]) * y_gla,)


def _ln_fn(xs_, ps_):
    x, r = xs_
    g, b = ps_
    t = ALPHA * x + r
    mu = jnp.mean(t, axis=1, keepdims=True)
    var = jnp.mean(jnp.square(t - mu), axis=1, keepdims=True)
    return ((t - mu) * lax.rsqrt(var + LN_EPS) * g + b,)


def _row_spec(rows, width, colblk, n, reverse):
    if reverse:
        return pl.BlockSpec((rows, width), lambda c: (n - 1 - c, colblk))
    return pl.BlockSpec((rows, width), lambda c: (c, colblk))


def _full_spec(shape):
    zeros = (0,) * len(shape)
    return pl.BlockSpec(shape, lambda c: zeros)


def _chain_fwd(name, fn, n, blocked, full, out_defs, state_shape=None):
    nb, nf, no = len(blocked), len(full), len(out_defs)

    def body(*refs):
        xs = [r[...].astype(F32) for r in refs[:nb]]
        ps = [r[...] for r in refs[nb:nb + nf]]
        o_refs = refs[nb + nf:nb + nf + no]
        if state_shape is None:
            outs = fn(xs, ps)
        else:
            sprev_ref, s_ref = refs[nb + nf + no:]

            @pl.when(pl.program_id(0) == 0)
            def _():
                s_ref[...] = jnp.zeros_like(s_ref)

            s = s_ref[...]
            sprev_ref[0] = s
            outs, s_new = fn(xs, ps, s)
            s_ref[...] = s_new
        for r, o in zip(o_refs, outs):
            r[...] = o.astype(r.dtype)

    in_specs = [_row_spec(rows, width, cb, n, False) for _, rows, width, cb in blocked]
    in_specs += [_full_spec(a.shape) for a in full]
    out_specs = [_row_spec(rows, width, 0, n, False) for rows, width, _ in out_defs]
    out_shape = [jax.ShapeDtypeStruct((n * rows, width), dt) for rows, width, dt in out_defs]
    scratch = []
    if state_shape is not None:
        out_specs.append(pl.BlockSpec((1,) + state_shape, lambda c: (c, 0, 0)))
        out_shape.append(jax.ShapeDtypeStruct((n,) + state_shape, F32))
        scratch.append(pltpu.VMEM(state_shape, F32))
    return pl.pallas_call(body, name=name, grid=(n,), in_specs=in_specs, out_specs=out_specs, out_shape=out_shape,
                          scratch_shapes=scratch, compiler_params=_cparams(("arbitrary",)))(
        *[a for a, _, _, _ in blocked], *full)


def _chain_bwd(name, fn, n, blocked, full, douts, sprev=None, dx_dtypes=None):
    nb, nf, nd = len(blocked), len(full), len(douts)
    has_state = sprev is not None
    dx_dtypes = dx_dtypes or [F32] * nb

    def body(*refs):
        pos = 0
        b_refs = refs[pos:pos + nb]; pos += nb
        f_refs = refs[pos:pos + nf]; pos += nf
        d_refs = refs[pos:pos + nd]; pos += nd
        if has_state:
            sprev_ref = refs[pos]; pos += 1
        dx_refs = refs[pos:pos + nb]; pos += nb
        dp_refs = refs[pos:pos + nf]; pos += nf
        if has_state:
            ds_ref = refs[pos]

        @pl.when(pl.program_id(0) == 0)
        def _():
            for r in dp_refs:
                r[...] = jnp.zeros_like(r)
            if has_state:
                ds_ref[...] = jnp.zeros_like(ds_ref)

        xs = [r[...].astype(F32) for r in b_refs]
        ps = [r[...] for r in f_refs]
        dys = tuple(r[...].astype(F32) for r in d_refs)
        if has_state:
            _, vjp = jax.vjp(fn, xs, ps, sprev_ref[0])
            dxs, dps, ds = vjp((dys, ds_ref[...]))
            ds_ref[...] = ds
        else:
            _, vjp = jax.vjp(fn, xs, ps)
            dxs, dps = vjp(dys)
        for r, d in zip(dx_refs, dxs):
            r[...] = d.astype(r.dtype)
        for r, d in zip(dp_refs, dps):
            r[...] += d

    in_specs = [_row_spec(rows, width, cb, n, True) for _, rows, width, cb in blocked]
    in_specs += [_full_spec(a.shape) for a in full]
    in_specs += [_row_spec(rows, width, 0, n, True) for _, rows, width in douts]
    args = [a for a, _, _, _ in blocked] + list(full) + [a for a, _, _ in douts]
    scratch = []
    if has_state:
        st_shape = sprev.shape[1:]
        in_specs.append(pl.BlockSpec((1,) + st_shape, lambda c: (n - 1 - c, 0, 0)))
        args.append(sprev)
        scratch.append(pltpu.VMEM(st_shape, F32))
    out_specs = [_row_spec(rows, width, 0, n, True) for _, rows, width, _ in blocked]
    out_specs += [_full_spec(a.shape) for a in full]
    out_shape = [jax.ShapeDtypeStruct((n * rows, width), dt) for (_, rows, width, _), dt in zip(blocked, dx_dtypes)]
    out_shape += [jax.ShapeDtypeStruct(a.shape, F32) for a in full]
    res = pl.pallas_call(body, name=name, grid=(n,), in_specs=in_specs, out_specs=out_specs, out_shape=out_shape,
                         scratch_shapes=scratch, compiler_params=_cparams(("arbitrary",)))(*args)
    return res[:nb], res[nb:]


def _tile(n, target, unit):
    if n <= target:
        return n
    best = None
    for t in range(unit, target + 1, unit):
        if n % t == 0:
            best = t
    assert best is not None, (n, target, unit)
    return best


def _mm(name, a, b, dims="nn", out_dtype=F32, tm=2048, tn=512, tk=2048):
    if dims == "nn":
        (M, K), (_, N) = a.shape, b.shape
    elif dims == "nt":
        (M, K), (N, _) = a.shape, b.shape
    else:
        (K, M), (_, N) = a.shape, b.shape
    tm, tn, tk = _tile(M, tm, LANES), _tile(N, tn, LANES), _tile(K, tk, LANES)
    nk = K // tk

    def body(a_ref, b_ref, o_ref, acc_ref):
        part = _dot(a_ref[...], b_ref[...], dims)
        if nk == 1:
            o_ref[...] = part.astype(o_ref.dtype)
            return

        @pl.when(pl.program_id(2) == 0)
        def _():
            acc_ref[...] = part

        @pl.when(pl.program_id(2) > 0)
        def _():
            acc_ref[...] += part

        @pl.when(pl.program_id(2) == nk - 1)
        def _():
            o_ref[...] = acc_ref[...].astype(o_ref.dtype)

    if dims == "tn":
        a_spec = pl.BlockSpec((tk, tm), lambda j, i, k: (k, i))
    else:
        a_spec = pl.BlockSpec((tm, tk), lambda j, i, k: (i, k))
    if dims == "nt":
        b_spec = pl.BlockSpec((tn, tk), lambda j, i, k: (j, k))
    else:
        b_spec = pl.BlockSpec((tk, tn), lambda j, i, k: (k, j))
    return pl.pallas_call(
        body, name=name, grid=(N // tn, M // tm, nk), in_specs=[a_spec, b_spec],
        out_specs=pl.BlockSpec((tm, tn), lambda j, i, k: (i, j)), out_shape=jax.ShapeDtypeStruct((M, N), out_dtype),
        scratch_shapes=[pltpu.VMEM((tm, tn) if nk > 1 else (8, LANES), F32)],
        compiler_params=_cparams(("parallel", "parallel", "arbitrary")))(a, b)


CONV_CB = 256


def _shift_down(x, k):
    if k == 0:
        return x
    return jnp.where(_iota2(x.shape, 0) >= k, pltpu.roll(x, k, 0), 0.0)


def _shift_up(x, k):
    if k == 0:
        return x
    t = x.shape[0]
    return jnp.where(_iota2(x.shape, 0) < t - k, pltpu.roll(x, t - k, 0), 0.0)


def _conv_pre(x, w, b):
    kk = w.shape[0]
    pre = x * w[kk - 1:kk, :]
    for k in range(kk - 1):
        pre = pre + _shift_down(x, kk - 1 - k) * w[k:k + 1, :]
    return pre if b is None else pre + b


def _conv_bwd_pre(x, w, dpre, dw_ref, db_ref):
    kk = w.shape[0]
    dx = dpre * w[kk - 1:kk, :]
    dw_ref[kk - 1:kk, :] = jnp.sum(dpre * x, axis=0, keepdims=True)
    for k in range(kk - 1):
        dx = dx + _shift_up(dpre, kk - 1 - k) * w[k:k + 1, :]
        dw_ref[k:k + 1, :] = jnp.sum(dpre * _shift_down(x, kk - 1 - k), axis=0, keepdims=True)
    if db_ref is not None:
        db_ref[...] = jnp.sum(dpre, axis=0, keepdims=True)
    return dx


def _dsilu(pre):
    sg = jax.nn.sigmoid(pre)
    return sg * (1.0 + pre * (1.0 - sg))


def _conv_silu_fwd(name, src, col0, w, b):
    T = src.shape[0]
    kk, C = w.shape
    cb = CONV_CB
    off = col0 // cb

    def body(*refs):
        x_ref, w_ref = refs[:2]
        b_val = refs[2][...] if b is not None else None
        refs[-1][...] = jax.nn.silu(_conv_pre(x_ref[...], w_ref[...], b_val))

    in_specs = [pl.BlockSpec((T, cb), lambda j: (0, off + j)), pl.BlockSpec((kk, cb), lambda j: (0, j))]
    args = [src, w]
    if b is not None:
        in_specs.append(pl.BlockSpec((1, cb), lambda j: (0, j)))
        args.append(b)
    return pl.pallas_call(body, name=name, grid=(C // cb,), in_specs=in_specs,
                          out_specs=pl.BlockSpec((T, cb), lambda j: (0, j)), out_shape=jax.ShapeDtypeStruct((T, C), F32),
                          compiler_params=_cparams(("parallel",)))(*args)


def _conv_silu_bwd(name, src, col0, w, b, dy, dx_dtype):
    T = src.shape[0]
    kk, C = w.shape
    cb = CONV_CB
    off = col0 // cb
    has_b = b is not None

    def body(*refs):
        x_ref, w_ref = refs[:2]
        pos = 2
        b_val = None
        if has_b:
            b_val = refs[pos][...]; pos += 1
        dy_ref = refs[pos]; pos += 1
        dx_ref, dw_ref = refs[pos], refs[pos + 1]
        db_ref = refs[pos + 2] if has_b else None
        x, wv = x_ref[...], w_ref[...]
        dpre = dy_ref[...] * _dsilu(_conv_pre(x, wv, b_val))
        dx_ref[...] = _conv_bwd_pre(x, wv, dpre, dw_ref, db_ref).astype(dx_ref.dtype)

    in_specs = [pl.BlockSpec((T, cb), lambda j: (0, off + j)), pl.BlockSpec((kk, cb), lambda j: (0, j))]
    args = [src, w]
    if has_b:
        in_specs.append(pl.BlockSpec((1, cb), lambda j: (0, j)))
        args.append(b)
    in_specs.append(pl.BlockSpec((T, cb), lambda j: (0, j)))
    args.append(dy)
    out_specs = [pl.BlockSpec((T, cb), lambda j: (0, j)), pl.BlockSpec((kk, cb), lambda j: (0, j))]
    out_shape = [jax.ShapeDtypeStruct((T, C), dx_dtype), jax.ShapeDtypeStruct((kk, C), F32)]
    if has_b:
        out_specs.append(pl.BlockSpec((1, cb), lambda j: (0, j)))
        out_shape.append(jax.ShapeDtypeStruct((1, C), F32))
    return pl.pallas_call(body, name=name, grid=(C // cb,), in_specs=in_specs, out_specs=out_specs, out_shape=out_shape,
                          compiler_params=_cparams(("parallel",)))(*args)


def _ffn_glu_fwd(name, up, w, b, out_dtype=F32):
    T = up.shape[0]
    kk = w.shape[0]
    cb = CONV_CB
    width = up.shape[1] // 2
    nblk = width // cb

    def body(g_ref, u_ref, wg_ref, wu_ref, bg_ref, bu_ref, o_ref):
        g = _conv_pre(g_ref[...], wg_ref[...], bg_ref[...])
        u = _conv_pre(u_ref[...], wu_ref[...], bu_ref[...])
        o_ref[...] = (jax.nn.silu(g) * u).astype(o_ref.dtype)

    lo, hi = (lambda j: (0, j)), (lambda j: (0, nblk + j))
    in_specs = [pl.BlockSpec((T, cb), lo), pl.BlockSpec((T, cb), hi), pl.BlockSpec((kk, cb), lo), pl.BlockSpec((kk, cb), hi),
                pl.BlockSpec((1, cb), lo), pl.BlockSpec((1, cb), hi)]
    return pl.pallas_call(body, name=name, grid=(nblk,), in_specs=in_specs, out_specs=pl.BlockSpec((T, cb), lo),
                          out_shape=jax.ShapeDtypeStruct((T, width), out_dtype),
                          compiler_params=_cparams(("parallel",)))(up, up, w, w, b, b)


def _ffn_glu_bwd(name, up, w, b, dact, dx_dtype):
    T = up.shape[0]
    kk = w.shape[0]
    cb = CONV_CB
    width = up.shape[1] // 2
    nblk = width // cb

    def body(g_ref, u_ref, wg_ref, wu_ref, bg_ref, bu_ref, d_ref, dg_ref, du_ref, dwg_ref, dwu_ref, dbg_ref, dbu_ref):
        xg, xu, wg, wu = g_ref[...], u_ref[...], wg_ref[...], wu_ref[...]
        g = _conv_pre(xg, wg, bg_ref[...])
        u = _conv_pre(xu, wu, bu_ref[...])
        d = d_ref[...].astype(F32)
        dg_ref[...] = _conv_bwd_pre(xg, wg, d * u * _dsilu(g), dwg_ref, dbg_ref).astype(dg_ref.dtype)
        du_ref[...] = _conv_bwd_pre(xu, wu, d * jax.nn.silu(g), dwu_ref, dbu_ref).astype(du_ref.dtype)

    lo, hi = (lambda j: (0, j)), (lambda j: (0, nblk + j))
    in_specs = [pl.BlockSpec((T, cb), lo), pl.BlockSpec((T, cb), hi), pl.BlockSpec((kk, cb), lo), pl.BlockSpec((kk, cb), hi),
                pl.BlockSpec((1, cb), lo), pl.BlockSpec((1, cb), hi), pl.BlockSpec((T, cb), lo)]
    out_specs = [pl.BlockSpec((T, cb), lo)] * 2 + [pl.BlockSpec((kk, cb), lo)] * 2 + [pl.BlockSpec((1, cb), lo)] * 2
    out_shape = ([jax.ShapeDtypeStruct((T, width), dx_dtype)] * 2 + [jax.ShapeDtypeStruct((kk, width), F32)] * 2
                 + [jax.ShapeDtypeStruct((1, width), F32)] * 2)
    return pl.pallas_call(body, name=name, grid=(nblk,), in_specs=in_specs, out_specs=out_specs, out_shape=out_shape,
                          compiler_params=_cparams(("parallel",)))(up, up, w, w, b, b, dact)


def _loss_head(y, target):
    T, D = y.shape
    tb = _tile(T, 256, 8)

    def body(y_ref, t_ref, dy_ref, l_ref):
        @pl.when(pl.program_id(0) == 0)
        def _():
            l_ref[...] = jnp.zeros_like(l_ref)

        err = y_ref[...] - t_ref[...]
        dy_ref[...] = err * (1.0 / D)
        l_ref[...] += jnp.sum(err * err, axis=0, keepdims=True) * (0.5 / D)

    spec = pl.BlockSpec((tb, D), lambda i: (i, 0))
    return pl.pallas_call(body, name="loss_head", grid=(T // tb,), in_specs=[spec, spec],
                          out_specs=[spec, pl.BlockSpec((1, D), lambda i: (0, 0))],
                          out_shape=[jax.ShapeDtypeStruct((T, D), F32), jax.ShapeDtypeStruct((1, D), F32)],
                          compiler_params=_cparams(("arbitrary",)))(y, target)


def _adamw_math(w, g, m, v):
    m = ADAM_B1 * m + (1.0 - ADAM_B1) * g
    v = ADAM_B2 * v + (1.0 - ADAM_B2) * jnp.square(g)
    m_hat = m / (1.0 - ADAM_B1 ** ADAM_STEP)
    v_hat = v / (1.0 - ADAM_B2 ** ADAM_STEP)
    return -ADAM_LR * (m_hat / (jnp.sqrt(v_hat) + ADAM_EPS) + ADAM_WD * w), m, v


def _adamw(name, w, g, m, v):
    A, R, C = w.shape
    if C % LANES == 0:
        rb, cb = _slab(R, C)
    else:
        rb, cb = _tile(R, max(8, SLAB_BYTES // 2 // (C * 4) // 8 * 8), 8), C

    def body(w_ref, g_ref, m_ref, v_ref, d_ref, mo_ref, vo_ref):
        d, mn, vn = _adamw_math(w_ref[...], g_ref[...], m_ref[...], v_ref[...])
        d_ref[...] = d
        mo_ref[...] = mn
        vo_ref[...] = vn

    spec = pl.BlockSpec((1, rb, cb), lambda a, r, q: (a, r, q))
    return pl.pallas_call(body, name=name, grid=(A, R // rb, C // cb), in_specs=[spec] * 4, out_specs=[spec] * 3,
                          out_shape=[jax.ShapeDtypeStruct(w.shape, F32)] * 3,
                          compiler_params=_cparams(("parallel", "parallel", "parallel")))(w, g, m, v)


def _adamw_small(parts, w, m, v):
    def body(p_ref, w_ref, m_ref, v_ref, g_ref, d_ref, mo_ref, vo_ref):
        g = p_ref[0]
        for i in range(1, N_DEV):
            g = g + p_ref[i]
        d, mn, vn = _adamw_math(w_ref[...], g, m_ref[...], v_ref[...])
        g_ref[...] = g
        d_ref[...] = d
        mo_ref[...] = mn
        vo_ref[...] = vn

    return pl.pallas_call(body, name="adamw_small", out_shape=[jax.ShapeDtypeStruct(w.shape, F32)] * 4,
                          compiler_params=_cparams())(parts, w, m, v)


def _add_blocks(name, a, b, out_dtype=F32):
    n, R, W = a.shape
    rb = _tile(R, 512, 8)

    def body(a_ref, b_ref, o_ref):
        o_ref[...] = (a_ref[...].astype(F32) + b_ref[...].astype(F32)).astype(o_ref.dtype)

    spec = pl.BlockSpec((1, rb, W), lambda i, r: (i, r, 0))
    return pl.pallas_call(body, name=name, grid=(n, R // rb), in_specs=[spec, spec], out_specs=spec,
                          out_shape=jax.ShapeDtypeStruct(a.shape, out_dtype),
                          compiler_params=_cparams(("parallel", "parallel")))(a, b)


SLAB_BYTES = 1 << 20


def _slab(R, W):
    if R % 16 == 0:
        return _tile(R, max(16, SLAB_BYTES // (4 * W) // 16 * 16), 16), W
    assert W % LANES == 0, (R, W)
    return R, _tile(W, max(LANES, SLAB_BYTES // (4 * R) // LANES * LANES), LANES)


def _pair_add(name, g, other, c, chip):
    _, R, W = g.shape
    rb, cb = _slab(R, W)

    def body(s_ref, a_ref, b_ref, send_ref, own_ref):
        s = a_ref[0] + b_ref[0]
        send_ref[0] = s.astype(send_ref.dtype)

        @pl.when(pl.program_id(2) == s_ref[1])
        def _():
            own_ref[...] = s

    grid_spec = pltpu.PrefetchScalarGridSpec(
        num_scalar_prefetch=1, grid=(R // rb, W // cb, 4),
        in_specs=[pl.BlockSpec((1, rb, cb), lambda r, q, p, s_ref: (2 * p + s_ref[0], r, q)),
                  pl.BlockSpec((1, rb, cb), lambda r, q, p, s_ref: (p, r, q))],
        out_specs=[pl.BlockSpec((1, rb, cb), lambda r, q, p, s_ref: (p, r, q)),
                   pl.BlockSpec((rb, cb), lambda r, q, p, s_ref: (r, q))])
    scalars = jnp.stack([c, chip]).astype(jnp.int32)
    return pl.pallas_call(body, name=name, grid_spec=grid_spec,
                          out_shape=[jax.ShapeDtypeStruct((4, R, W), MXU_DTYPE), jax.ShapeDtypeStruct((R, W), F32)],
                          compiler_params=_cparams(("parallel", "parallel", "arbitrary")))(scalars, g, other)


def _sum4(name, own, parts):
    R, W = own.shape
    rb, cb = _slab(R, W)

    def body(o_ref, p_ref, out_ref):
        out_ref[...] = ((o_ref[...] + p_ref[0].astype(F32)) + p_ref[1].astype(F32)) + p_ref[2].astype(F32)

    return pl.pallas_call(body, name=name, grid=(R // rb, W // cb),
                          in_specs=[pl.BlockSpec((rb, cb), lambda r, q: (r, q)), pl.BlockSpec((3, rb, cb), lambda r, q: (0, r, q))],
                          out_specs=pl.BlockSpec((rb, cb), lambda r, q: (r, q)), out_shape=jax.ShapeDtypeStruct((R, W), F32),
                          compiler_params=_cparams(("parallel", "parallel")))(own, parts)


MESH = pl.DeviceIdType.MESH
ANY = pl.BlockSpec(memory_space=pl.ANY)


def _place():
    return lax.axis_index("x"), lax.axis_index("y"), lax.axis_index("c")


def _other_chips(x, y):
    return [(1 - x, y), (x, 1 - y), (1 - x, 1 - y)]


def _all_gather(name, blocks):
    n = len(blocks)

    def body(*refs):
        x_refs, out_refs = refs[:n], refs[n:2 * n]
        send_sems, recv_sems, local_sems = refs[2 * n:]
        x, y, c = _place()
        me, sibling = (x, y, c), (x, y, 1 - c)
        chips = _other_chips(x, y)

        def slot(a, px, py, pc):
            return out_refs[a].at[4 * px + 2 * py + pc]

        def copy(a, k, blk, to, src=None):
            return pltpu.make_async_remote_copy(src_ref=slot(a, *blk) if src is None else src, dst_ref=slot(a, *blk),
                                                send_sem=send_sems.at[a, k], recv_sem=recv_sems.at[a, k],
                                                device_id=to, device_id_type=MESH)

        mine = [pltpu.make_async_copy(x_refs[a], slot(a, *me), local_sems.at[a]) for a in range(n)]
        for cp in mine:
            cp.start()
        first = []
        for j, chip in enumerate(chips):
            first += [copy(a, 1 + j, me, (*chip, c), src=x_refs[a]) for a in range(n)]
        first += [copy(a, 0, me, sibling, src=x_refs[a]) for a in range(n)]
        for cp in first:
            cp.start()
        passed = []
        for j, chip in enumerate(chips):
            for a in range(n):
                copy(a, 1 + j, (*chip, c), me).wait_recv()
                passed.append(copy(a, 4 + j, (*chip, c), sibling))
                passed[-1].start()
        for a in range(n):
            copy(a, 0, sibling, me).wait_recv()
        for j, chip in enumerate(chips):
            for a in range(n):
                copy(a, 4 + j, (*chip, 1 - c), me).wait_recv()
        for cp in first + passed:
            cp.wait_send()
        for cp in mine:
            cp.wait()

    return pl.pallas_call(body, name=name, in_specs=[ANY] * n, out_specs=[ANY] * n,
                          out_shape=[jax.ShapeDtypeStruct((N_DEV,) + b.shape, b.dtype) for b in blocks],
                          scratch_shapes=[pltpu.SemaphoreType.DMA((n, 7)), pltpu.SemaphoreType.DMA((n, 7)),
                                          pltpu.SemaphoreType.DMA((n,))])(*blocks)


def _routes_to_sibling(x, y, c):
    return [(2 * p + (1 - c), p, (x, y, 1 - c)) for p in range(4)]


def _routes_to_chips(x, y, c):
    return [(2 * px + py, j, (px, py, c)) for j, (px, py) in enumerate(_other_chips(x, y))]


def _routes_block_to_chips(x, y, c):
    me = 4 * x + 2 * y + c
    return [(me, me, (px, py, c)) for px, py in _other_chips(x, y)]


def _routes_blocks_to_sibling(x, y, c):
    return [(4 * px + 2 * py + c, 4 * px + 2 * py + c, (x, y, 1 - c)) for px, py in [(x, y)] + _other_chips(x, y)]


def _route_copies(routes, src_refs, land_refs, send_sems, recv_sems):
    x, y, c = _place()
    copies = []
    for a, (src, land) in enumerate(zip(src_refs, land_refs)):
        plan = routes(x, y, c)
        for k, (s, d, target) in enumerate(plan):
            i = a * len(plan) + k
            copies.append(pltpu.make_async_remote_copy(src_ref=src.at[s], dst_ref=land.at[d], send_sem=send_sems.at[i],
                                                       recv_sem=recv_sems.at[i], device_id=target, device_id_type=MESH))
    return copies


def _exchange(name, routes, n_routes, srcs, land_slots):
    n = len(srcs)

    def body(*refs):
        copies = _route_copies(routes, refs[:n], refs[n:2 * n], refs[2 * n], refs[2 * n + 1])
        for cp in copies:
            cp.start()
        for cp in copies:
            cp.wait_recv()
        for cp in copies:
            cp.wait_send()

    return pl.pallas_call(body, name=name, in_specs=[ANY] * n, out_specs=[ANY] * n,
                          out_shape=[jax.ShapeDtypeStruct((land_slots,) + s.shape[1:], s.dtype) for s in srcs],
                          scratch_shapes=[pltpu.SemaphoreType.DMA((n * n_routes,)), pltpu.SemaphoreType.DMA((n * n_routes,))])(*srcs)


HBM_SPEC = pl.BlockSpec(memory_space=pltpu.HBM)
SEM_SPEC = pl.BlockSpec(memory_space=pltpu.SEMAPHORE)
DATAFLOW = pltpu.SideEffectType.DATAFLOW_SIDE_EFFECTING


def _exchange_start(name, routes, n_routes, srcs, lands):
    n = len(srcs)
    in_place = lands is None
    bufs = list(srcs) + ([] if in_place else list(lands))
    nb = len(bufs)

    def body(*refs):
        src_refs = refs[:n]
        land_refs = src_refs if in_place else refs[n:nb]
        send_sems, recv_sems = refs[nb], refs[nb + 1]
        token = refs[-1]
        for cp in _route_copies(routes, src_refs, land_refs, send_sems, recv_sems):
            cp.start()
        token[...] = jnp.zeros_like(token)

    sems = [pltpu.SemaphoreType.DMA((n * n_routes,)), pltpu.SemaphoreType.DMA((n * n_routes,))]
    out = pl.pallas_call(
        body, name=name, in_specs=[HBM_SPEC] * nb,
        out_shape=sems + [pltpu.HBM(b.shape, b.dtype) for b in bufs] + [jax.ShapeDtypeStruct((8, LANES), F32)],
        out_specs=[SEM_SPEC, SEM_SPEC] + [HBM_SPEC] * nb + [pl.BlockSpec(memory_space=pltpu.VMEM)],
        input_output_aliases={i: 2 + i for i in range(nb)},
        compiler_params=pltpu.CompilerParams(has_side_effects=DATAFLOW))(
        *[pltpu.with_memory_space_constraint(b, pltpu.HBM) for b in bufs])
    return (out[0], out[1], list(out[2:2 + nb])), out[-1]


def _exchange_wait(name, routes, n_routes, n, started, after):
    send_sems, recv_sems, bufs = started
    nb = len(bufs)
    in_place = nb == n

    def body(*refs):
        src_refs = refs[:n]
        land_refs = src_refs if in_place else refs[n:nb]
        for cp in _route_copies(routes, src_refs, land_refs, refs[nb], refs[nb + 1]):
            cp.wait_send()
            cp.wait_recv()

    out = pl.pallas_call(
        body, name=name, in_specs=[HBM_SPEC] * nb + [SEM_SPEC, SEM_SPEC, ANY],
        out_shape=[pltpu.HBM(b.shape, b.dtype) for b in bufs], out_specs=[HBM_SPEC] * nb,
        input_output_aliases={i: i for i in range(nb)},
        compiler_params=pltpu.CompilerParams(has_side_effects=DATAFLOW))(*bufs, send_sems, recv_sems, after)
    return list(out[:n]) if in_place else list(out[n:])


def _pair_sums(tag, gs, from_sibling):
    x, y, c = _place()
    return [_pair_add(f"rs_add_{tag}_{i}", g, o, c, 2 * x + y) for i, (g, o) in enumerate(zip(gs, from_sibling))]


def _reduce_scatter(tag, gs):
    sums = _pair_sums(tag, gs, _exchange(f"rs_swap_{tag}", _routes_to_sibling, 4, gs, 4))
    got = _exchange(f"rs_chips_{tag}", _routes_to_chips, 3, [s[0] for s in sums], 3)
    return [_sum4(f"rs_sum_{tag}_{i}", s[1], q) for i, (s, q) in enumerate(zip(sums, got))]


def _reduce_scatter_begin(tag, gs):
    lands = [lax.empty((4,) + g.shape[1:], g.dtype) for g in gs]
    swap, token = _exchange_start(f"rs_swap_{tag}_start", _routes_to_sibling, 4, gs, lands)
    return dict(tag=tag, gs=gs, swap=swap), token


def _reduce_scatter_middle(state, after):
    tag, gs = state["tag"], state["gs"]
    from_sibling = _exchange_wait(f"rs_swap_{tag}_wait", _routes_to_sibling, 4, len(gs), state["swap"], after)
    state["sums"] = _pair_sums(tag, gs, from_sibling)
    partials = [s[0] for s in state["sums"]]
    lands = [lax.empty((3,) + p.shape[1:], p.dtype) for p in partials]
    state["chips"], token = _exchange_start(f"rs_chips_{tag}_start", _routes_to_chips, 3, partials, lands)
    return token


def _reduce_scatter_end(state, after):
    tag = state["tag"]
    got = _exchange_wait(f"rs_chips_{tag}_wait", _routes_to_chips, 3, len(state["gs"]), state["chips"], after)
    return [_sum4(f"rs_sum_{tag}_{i}", s[1], q) for i, (s, q) in enumerate(zip(state["sums"], got))]


def _all_gather_begin(tag, blocks):
    dev = 4 * lax.axis_index("x") + 2 * lax.axis_index("y") + lax.axis_index("c")
    zones = [lax.dynamic_update_slice_in_dim(lax.empty((N_DEV,) + b.shape, b.dtype), b[None], dev, axis=0) for b in blocks]
    chips, token = _exchange_start(f"gather_{tag}_chips_start", _routes_block_to_chips, 3, zones, None)
    return dict(tag=tag, n=len(blocks), chips=chips), token


def _all_gather_middle(state, after):
    tag, n = state["tag"], state["n"]
    zones = _exchange_wait(f"gather_{tag}_chips_wait", _routes_block_to_chips, 3, n, state["chips"], after)
    state["sibling"], token = _exchange_start(f"gather_{tag}_sibling_start", _routes_blocks_to_sibling, 4, zones, None)
    return token


def _all_gather_end(state, after):
    return _exchange_wait(f"gather_{state['tag']}_sibling_wait", _routes_blocks_to_sibling, 4, state["n"], state["sibling"], after)


def _flat_rows(n_elems):
    return -(-n_elems // (FLAT_W * 16)) * 16


def _pack(arrays, dtype):
    flat = jnp.concatenate([a.reshape(-1).astype(dtype) for a in arrays])
    rows = _flat_rows(flat.shape[0])
    flat = jnp.pad(flat, (0, rows * FLAT_W - flat.shape[0]))
    return flat.reshape(rows, FLAT_W)


def _unpack(flat, shapes, lead=()):
    flat = flat.reshape(lead + (-1,))
    out, pos = [], 0
    for s in shapes:
        n = math.prod(s)
        out.append(flat[..., pos:pos + n].reshape(lead + tuple(s)))
        pos += n
    return out


def _ffn_pad_rows(a):
    n = a.shape[0] // FFN_HALF
    a = jnp.pad(a.reshape(n, FFN_HALF, a.shape[1]), ((0, 0), (0, FFN_HALF_PAD - FFN_HALF), (0, 0)))
    return a.reshape(n * FFN_HALF_PAD, a.shape[2])


def _ffn_unpad_rows(a):
    n = a.shape[0] // FFN_HALF_PAD
    return a.reshape(n, FFN_HALF_PAD, a.shape[1])[:, :FFN_HALF].reshape(n * FFN_HALF, a.shape[1])


def _ffn_pad_cols(a):
    n = a.shape[1] // FFN_HALF
    a = jnp.pad(a.reshape(a.shape[0], n, FFN_HALF), ((0, 0), (0, 0), (0, FFN_HALF_PAD - FFN_HALF)))
    return a.reshape(a.shape[0], n * FFN_HALF_PAD)


def _ffn_unpad_cols(a):
    n = a.shape[1] // FFN_HALF_PAD
    return a.reshape(a.shape[0], n, FFN_HALF_PAD)[:, :, :FFN_HALF].reshape(a.shape[0], n * FFN_HALF)


def _shard_to_send(name, shard):
    if name == "w_in":
        shard = shard.T
    elif name == "ffn_w_up":
        shard = _ffn_pad_rows(shard.T)
    return shard.astype(MXU_DTYPE)


def _whole_from_gathered(name, g):
    if name == "w_in":
        return _pad_in_proj_rows(g.reshape(IN_DIM, g.shape[2]))
    if name in ("w_br_gdn", "w_br_gla"):
        return jnp.transpose(g, (1, 0, 2)).reshape(g.shape[1], N_DEV * g.shape[2])
    if name == "ffn_w_down":
        return jnp.pad(g, ((0, 0), (0, FFN_HALF_PAD - FFN_HALF), (0, 0))).reshape(FFN_PAD, g.shape[2])
    return g.reshape(N_DEV * g.shape[1], g.shape[2])


def _slots_from_whole(name, gw):
    if name == "w_in":
        return _unpad_in_proj_rows(gw).reshape(N_DEV, IN_DIM // N_DEV, gw.shape[1])
    if name in ("w_br_gdn", "w_br_gla"):
        return jnp.transpose(gw.reshape(gw.shape[0], N_DEV, gw.shape[1] // N_DEV), (1, 0, 2))
    return gw.reshape(N_DEV, gw.shape[0] // N_DEV, gw.shape[1])


def _shard_from_slot(name, s):
    if name == "ffn_w_up":
        return _ffn_unpad_rows(s)
    if name == "ffn_w_down":
        return s[:FFN_HALF]
    return s


def _in_proj_pieces():
    starts, pos = {}, 0
    for n, width in IN_SPLITS:
        starts[n] = (pos, width)
        pos += width
    return [(starts[ref][0], off + lane, starts[ref][1]) for _, off, _, pieces in PAD_SEGS for ref, lane in pieces]


def _pad_in_proj_rows(w):
    rows, at = [], 0
    for src, dst, n in sorted(_in_proj_pieces(), key=lambda p: p[1]):
        if dst > at:
            rows.append(jnp.zeros((dst - at, w.shape[1]), w.dtype))
        rows.append(w[src:src + n])
        at = dst + n
    rows.append(jnp.zeros((IN_PAD - at, w.shape[1]), w.dtype))
    return jnp.concatenate(rows, axis=0)


def _unpad_in_proj_rows(wp):
    return jnp.concatenate([wp[dst:dst + n] for _, dst, n in sorted(_in_proj_pieces())], axis=0)


def _lane_pad(a, width=LANES):
    return jnp.pad(a, ((0, 0), (0, width - a.shape[1])))


def _seg_blk(h, name, rows):
    off, width = SEG[name]
    return (h, rows, width, off // width)


def _ln_both(xs_, ps_):
    (y,) = _ln_fn(xs_, ps_)
    return (y, y)


def _behind(param, hooks, stage, *seen):
    if hooks is None or stage not in hooks:
        return param
    token = hooks[stage](*seen)
    return param if token is None else param + token[0:1, 0:1]


def _layer_fwd(l, x, x_mx, W, sp, hooks=None):
    T = x.shape[0]
    n64, ngla, ntok = T // SSD_CHUNK, T // GLA_BLOCK, T // 256
    h = _mm(f"in_proj_{l}", x_mx, W["w_in"], "nt")
    xbc = _conv_silu_fwd(f"ssd_conv_{l}", h, SEG["xbc"][0], sp["ssd_conv_w"], sp["ssd_conv_b"])
    gqkv = _conv_silu_fwd(f"gdn_conv_{l}", h, SEG["gqkv"][0], sp["gdn_conv_w"], None)

    ssd_in = [(xbc, SSD_CHUNK, SSD_XBC, 0), _seg_blk(h, "dt", SSD_CHUNK), _seg_blk(h, "z", SSD_CHUNK)]
    ssd_p = [_behind(sp["ssd_dt_bias"], hooks, "projected", h), sp["ssd_a_log"], sp["ssd_d"], sp["ssd_norm_w"]]
    o_ssd, ssd_states = _chain_fwd(f"ssd_fwd_{l}", _ssd_chunk, n64, ssd_in, ssd_p, [(SSD_CHUNK, SSD_INNER, MXU_DTYPE)],
                                   (SSD_STATE, SSD_INNER))
    o_gdn, gdn_saved = _gdn_forward(str(l), gqkv, h, sp)
    gla_in = [_seg_blk(h, "lqkv", GLA_BLOCK), _seg_blk(h, "lglr", GLA_BLOCK), _seg_blk(h, "lr", GLA_BLOCK)]
    gla_p = [jnp.pad(sp["gla_gate_w2"], ((0, LANES - GLA_RANK), (0, 0))), sp["gla_gate_b"], sp["gla_norm_w"]]
    o_gla, gla_states = _chain_fwd(f"gla_fwd_{l}", _gla_block, ngla, gla_in, gla_p, [(GLA_BLOCK, GLA_V, MXU_DTYPE)],
                                   (GLA_VAL_DIM, GLA_K))
    ln1_p = [_behind(sp["ln1_g"], hooks, "mixed", o_gla), sp["ln1_b"]]
    y_ssd = _mm(f"br_ssd_{l}", o_ssd, W["w_br_ssd"])
    y_gdn = _mm(f"br_gdn_{l}", o_gdn, W["w_br_gdn"])
    y_gla = _mm(f"br_gla_{l}", o_gla, W["w_br_gla"])
    merge_in = [_seg_blk(h, "gates", 256), (y_ssd, 256, D_MODEL, 0), (y_gdn, 256, D_MODEL, 0), (y_gla, 256, D_MODEL, 0)]
    (mix,) = _chain_fwd(f"merge_{l}", _merge_fn, ntok, merge_in, [], [(256, D_MODEL, MXU_DTYPE)])
    r1 = _mm(f"out_proj_{l}", mix, W["w_out"])
    both = [(256, D_MODEL, F32), (256, D_MODEL, MXU_DTYPE)]
    x1, x1_mx = _chain_fwd(f"ln1_{l}", _ln_both, ntok, [(x, 256, D_MODEL, 0), (r1, 256, D_MODEL, 0)], ln1_p, both)
    up = _mm(f"ffn_up_{l}", x1_mx, W["ffn_w_up"], "nt")
    ln2_p = [_behind(sp["ln2_g"], hooks, "ffn_up", up), sp["ln2_b"]]
    act = _ffn_glu_fwd(f"ffn_glu_{l}", up, sp["ffn_conv_w_pad"], sp["ffn_conv_b_pad"], MXU_DTYPE)
    r2 = _mm(f"ffn_down_{l}", act, W["ffn_w_down"])
    x2, x2_mx = _chain_fwd(f"ln2_{l}", _ln_both, ntok, [(x1, 256, D_MODEL, 0), (r2, 256, D_MODEL, 0)], ln2_p, both)
    saved = dict(x=x, x_mx=x_mx, h=h, xbc=xbc, gqkv=gqkv, ssd_in=ssd_in, ssd_p=ssd_p, ssd_states=ssd_states,
                 gdn=gdn_saved, gla_in=gla_in, gla_p=gla_p, gla_states=gla_states, o_ssd=o_ssd,
                 o_gdn=o_gdn, o_gla=o_gla, merge_in=merge_in, mix=mix, r1=r1, ln1_p=ln1_p, x1=x1, x1_mx=x1_mx, up=up, act=act,
                 r2=r2, ln2_p=ln2_p)
    return x2, x2_mx, saved


def _layer_bwd(l, dx2, W, sp, sv, hooks=None):
    T = dx2.shape[0]
    n64, ngla, ntok = T // SSD_CHUNK, T // GLA_BLOCK, T // 256
    bf = MXU_DTYPE
    gw, gs = {}, {}
    ln2_p = [_behind(sv["ln2_p"][0], hooks, "start"), sv["ln2_p"][1]]
    (dx1_a, dr2), (gs["ln2_g"], gs["ln2_b"]) = _chain_bwd(
        f"ln2_bwd_{l}", _ln_fn, ntok, [(sv["x1"], 256, D_MODEL, 0), (sv["r2"], 256, D_MODEL, 0)], ln2_p,
        [(dx2, 256, D_MODEL)], dx_dtypes=[F32, bf])
    gw["ffn_w_down"] = _mm(f"ffn_down_dw_{l}", sv["act"], dr2, "tn")
    dact = _mm(f"ffn_down_dx_{l}", dr2, W["ffn_w_down"], "nt")
    dg, du, dwg, dwu, dbg, dbu = _ffn_glu_bwd(f"ffn_glu_bwd_{l}", sv["up"], sp["ffn_conv_w_pad"], sp["ffn_conv_b_pad"], dact, bf)
    gs["ffn_conv_w"] = _ffn_unpad_cols(jnp.concatenate([dwg, dwu], axis=1))
    gs["ffn_conv_b"] = _ffn_unpad_cols(jnp.concatenate([dbg, dbu], axis=1))
    dup = jnp.concatenate([dg, du], axis=1)
    gw["ffn_w_up"] = _mm(f"ffn_up_dw_{l}", dup, sv["x1_mx"], "tn", tn=1024)
    dx1_b = _mm(f"ffn_up_dx_{l}", dup, W["ffn_w_up"], "nn", tn=1024, tk=1024)
    ln1_p = [_behind(sv["ln1_p"][0], hooks, "ffn", dx1_b), sv["ln1_p"][1]]
    (dx_a, dr1), (gs["ln1_g"], gs["ln1_b"]) = _chain_bwd(
        f"ln1_bwd_{l}", _ln_sum_fn, ntok, [(sv["x"], 256, D_MODEL, 0), (sv["r1"], 256, D_MODEL, 0)], ln1_p,
        [(dx1_a, 256, D_MODEL), (dx1_b, 256, D_MODEL)], dx_dtypes=[F32, bf])
    gw["w_out"] = _mm(f"out_proj_dw_{l}", sv["mix"], dr1, "tn")
    dmix = _mm(f"out_proj_dx_{l}", dr1, W["w_out"], "nt")
    (dgates, dy_ssd, dy_gdn, dy_gla), _ = _chain_bwd(f"merge_bwd_{l}", _merge_fn, ntok, sv["merge_in"], [],
                                                     [(dmix, 256, D_MODEL)], dx_dtypes=[bf, bf, bf, bf])
    gw["w_br_ssd"] = _mm(f"br_ssd_dw_{l}", sv["o_ssd"], dy_ssd, "tn")
    gw["w_br_gdn"] = _mm(f"br_gdn_dw_{l}", sv["o_gdn"], dy_gdn, "tn")
    gw["w_br_gla"] = _mm(f"br_gla_dw_{l}", sv["o_gla"], dy_gla, "tn")
    do_ssd = _mm(f"br_ssd_dx_{l}", dy_ssd, W["w_br_ssd"], "nt")
    do_gdn = _mm(f"br_gdn_dx_{l}", dy_gdn, W["w_br_gdn"], "nt")
    do_gla = _mm(f"br_gla_dx_{l}", dy_gla, W["w_br_gla"], "nt")

    ssd_p = [_behind(sv["ssd_p"][0], hooks, "branches", do_gla, gw)] + list(sv["ssd_p"][1:])
    (dxbc, ddt, dz), dps = _chain_bwd(f"ssd_bwd_{l}", _ssd_chunk, n64, sv["ssd_in"], ssd_p,
                                      [(do_ssd, SSD_CHUNK, SSD_INNER)], sprev=sv["ssd_states"], dx_dtypes=[F32, bf, bf])
    gs["ssd_dt_bias"], gs["ssd_a_log"], gs["ssd_d"], gs["ssd_norm_w"] = dps
    gdn_sv = dict(sv["gdn"], scan_p=[_behind(sv["gdn"]["scan_p"][0], hooks, "ssd", dz)])
    dgqkv, dgab, dgg, gs["gdn_a_log"], gs["gdn_dt_bias"], gs["gdn_norm_w"] = _gdn_backward(str(l), do_gdn, gdn_sv, bf)
    (dlqkv, dlglr, dlr), dps = _chain_bwd(f"gla_bwd_{l}", _gla_block, ngla, sv["gla_in"], sv["gla_p"],
                                          [(do_gla, GLA_BLOCK, GLA_V)], sprev=sv["gla_states"], dx_dtypes=[bf, bf, bf])
    gs["gla_gate_w2"], gs["gla_gate_b"], gs["gla_norm_w"] = dps[0][:GLA_RANK], dps[1], dps[2]
    dxbc_pre, gs["ssd_conv_w"], gs["ssd_conv_b"] = _conv_silu_bwd(
        f"ssd_conv_bwd_{l}", sv["h"], SEG["xbc"][0], sp["ssd_conv_w"], sp["ssd_conv_b"], dxbc, bf)
    dgqkv_pre, gs["gdn_conv_w"] = _conv_silu_bwd(f"gdn_conv_bwd_{l}", sv["h"], SEG["gqkv"][0], sp["gdn_conv_w"], None, dgqkv, bf)
    pieces = dict(gates=dgates, xbc=dxbc_pre, gqkv=dgqkv_pre, z=dz, lqkv=dlqkv, gg=dgg, lr=dlr, dt=ddt, gab=dgab, lglr=dlglr)
    cols = [pieces[name] for name, _, _, _ in PAD_SEGS]
    cols.append(jnp.zeros((T, IN_PAD - PAD_SEGS[-1][1] - PAD_SEGS[-1][2]), bf))
    dh = jnp.concatenate(cols, axis=1)
    gw["w_in"] = _mm(f"in_proj_dw_{l}", dh, sv["x_mx"], "tn", tn=1024)
    dx_b = _mm(f"in_proj_dx_{l}", dh, W["w_in"], "nn", tn=1024, tk=1024)
    dx = _add_blocks(f"dx_add_{l}", dx_a[None], dx_b[None])[0]
    return dx, gw, gs


def _ln_sum_fn(xs_, ps_):
    (y,) = _ln_fn(xs_, ps_)
    return (y, y)


def _small_2d(name, a):
    return a.reshape(1, -1) if a.ndim == 1 else a


def kernel(x, w_in, ssd_conv_w, ssd_conv_b, ssd_dt_bias, ssd_a_log, ssd_d, ssd_norm_w, gdn_conv_w, gdn_a_log, gdn_dt_bias, gdn_norm_w, gla_gate_w2, gla_gate_b, gla_norm_w, w_br_ssd, w_br_gdn, w_br_gla, w_out, ln1_g, ln1_b, ffn_w_up, ffn_conv_w, ffn_conv_b, ffn_w_down, ln2_g, ln2_b, loss_target, m_w_in, m_ssd_conv_w, m_ssd_conv_b, m_ssd_dt_bias, m_ssd_a_log, m_ssd_d, m_ssd_norm_w, m_gdn_conv_w, m_gdn_a_log, m_gdn_dt_bias, m_gdn_norm_w, m_gla_gate_w2, m_gla_gate_b, m_gla_norm_w, m_w_br_ssd, m_w_br_gdn, m_w_br_gla, m_w_out, m_ln1_g, m_ln1_b, m_ffn_w_up, m_ffn_conv_w, m_ffn_conv_b, m_ffn_w_down, m_ln2_g, m_ln2_b, v_w_in, v_ssd_conv_w, v_ssd_conv_b, v_ssd_dt_bias, v_ssd_a_log, v_ssd_d, v_ssd_norm_w, v_gdn_conv_w, v_gdn_a_log, v_gdn_dt_bias, v_gdn_norm_w, v_gla_gate_w2, v_gla_gate_b, v_gla_norm_w, v_w_br_ssd, v_w_br_gdn, v_w_br_gla, v_w_out, v_ln1_g, v_ln1_b, v_ffn_w_up, v_ffn_conv_w, v_ffn_conv_b, v_ffn_w_down, v_ln2_g, v_ln2_b):
    args = locals()
    w = {n: args[n] for n in WEIGHTS}
    m = {n: args["m_" + n] for n in WEIGHTS}
    v = {n: args["v_" + n] for n in WEIGHTS}
    dev = 4 * lax.axis_index("x") + 2 * lax.axis_index("y") + lax.axis_index("c")
    xl = x[0]
    tgt = loss_target[0]

    late = BIG[1:]

    def send(names, l):
        return [_shard_to_send(n, w[n][l]) for n in names]

    def whole_weights(names, got):
        return {n: _whole_from_gathered(n, g) for n, g in zip(names, got)}

    gather0, token0 = _all_gather_begin("w_0", send(late, 0))
    gather1, token1 = _all_gather_begin("w_1", send(BIG, 1))
    got0 = _all_gather("gather_first", send(BIG[:1], 0) + [w[n] for n in SMALL_SHARDED])
    W = [whole_weights(BIG[:1], got0[:1]), None]
    whole = dict(w)
    for n, s in zip(SMALL_SHARDED, got0[1:]):
        whole[n] = jnp.transpose(s, (1, 2, 0, 3)).reshape(s.shape[1], s.shape[2], N_DEV * s.shape[3])
    SP = [{n: _small_2d(n, whole[n][l]) for n in SMALL} for l in range(DEPTH)]
    for sp in SP:
        sp["ffn_conv_w_pad"] = _ffn_pad_cols(sp["ffn_conv_w"])
        sp["ffn_conv_b_pad"] = _ffn_pad_cols(sp["ffn_conv_b"])

    def late_weights_arrive(mixed):
        W[0].update(whole_weights(late, _all_gather_end(gather0, mixed)))

    fwd_hooks = {"projected": lambda h: _all_gather_middle(gather0, h), "mixed": late_weights_arrive,
                 "ffn_up": lambda up: _all_gather_middle(gather1, up)}
    saved = [None] * DEPTH
    x_mx = (xl + (token0[0, 0] + token1[0, 0])).astype(MXU_DTYPE)
    act, act_mx, saved[0] = _layer_fwd(0, xl, x_mx, W[0], SP[0], hooks=fwd_hooks)
    W[1] = whole_weights(BIG, _all_gather_end(gather1, act))
    act, act_mx, saved[1] = _layer_fwd(1, act, act_mx, W[1], SP[1])
    dy, loss_parts = _loss_head(act, tgt)
    loss = lax.psum(jnp.sum(loss_parts), ("x", "y", "c"))

    def slots_of(names, gw):
        return [_slots_from_whole(n, gw[n]) for n in names]

    grads = {}
    GS = [None] * DEPTH
    dy, gw, GS[1] = _layer_bwd(1, dy, W[1], SP[1], saved[1])
    reduce1, reduce1_token = _reduce_scatter_begin("1", slots_of(BIG, gw))
    held = {}

    def late_grads_leave(seen, gw0):
        held["reduce0"], token = _reduce_scatter_begin("0", slots_of(late, gw0))
        return token

    bwd_hooks = {"start": lambda: reduce1_token, "ffn": lambda seen: _reduce_scatter_middle(reduce1, seen),
                 "branches": late_grads_leave, "ssd": lambda seen: _reduce_scatter_middle(held["reduce0"], seen)}
    dy, gw, GS[0] = _layer_bwd(0, dy, W[0], SP[0], saved[0], hooks=bwd_hooks)
    red1 = _reduce_scatter_end(reduce1, dy)
    red0 = _reduce_scatter("first", slots_of(BIG[:1], gw)) + _reduce_scatter_end(held["reduce0"], dy)
    grad_x = dy[None]
    kept_t = ("w_in", "ffn_w_up")
    grads_k = {n: jnp.stack([_shard_from_slot(n, red0[i]), _shard_from_slot(n, red1[i])]) for i, n in enumerate(BIG)}

    small_shapes = [whole[n].shape for n in SMALL]
    gs_flat = _pack([jnp.stack([GS[l][n].reshape(whole[n].shape[1:]) for l in range(DEPTH)]) for n in SMALL], F32)
    (gs_all,) = _all_gather("gather_small_grads", [gs_flat])

    def mine(n, a):
        if n in SMALL_SHARDED:
            cs = a.shape[-1] // N_DEV
            return lax.dynamic_slice_in_dim(a, dev * cs, cs, axis=a.ndim - 1)
        return a

    m_whole, v_whole = {}, {}
    for n in SMALL:
        if n in SMALL_SHARDED:
            cs = w[n].shape[-1]
            zeros = jnp.zeros(whole[n].shape, F32)
            m_whole[n] = lax.dynamic_update_slice_in_dim(zeros, m[n], dev * cs, axis=2)
            v_whole[n] = lax.dynamic_update_slice_in_dim(zeros, v[n], dev * cs, axis=2)
        else:
            m_whole[n], v_whole[n] = m[n], v[n]
    outs = _adamw_small(gs_all, _pack([whole[n] for n in SMALL], F32), _pack([m_whole[n] for n in SMALL], F32),
                        _pack([v_whole[n] for n in SMALL], F32))
    g_s, d_s, m_s, v_s = [_unpack(o, small_shapes) for o in outs]
    delta, new_m, new_v = {}, {}, {}
    for i, n in enumerate(SMALL):
        grads[n], delta[n], new_m[n], new_v[n] = mine(n, g_s[i]), mine(n, d_s[i]), mine(n, m_s[i]), mine(n, v_s[i])
    for n in BIG:
        view = (lambda a: jnp.transpose(a, (0, 2, 1))) if n in kept_t else (lambda a: a)
        outs = _adamw(f"adamw_{n}", view(w[n]), grads_k[n], view(m[n]), view(v[n]))
        grads[n], delta[n], new_m[n], new_v[n] = view(grads_k[n]), view(outs[0]), view(outs[1]), view(outs[2])

    return (loss, grad_x, *[grads[n] for n in WEIGHTS], *[delta[n] for n in WEIGHTS], *[new_m[n] for n in WEIGHTS],
            *[new_v[n] for n in WEIGHTS])
```

```python
import functools
import math

import jax
import jax.numpy as jnp
from jax import lax
from jax.experimental import pallas as pl
from jax.experimental.pallas import tpu as pltpu

F32 = jnp.float32
MXU_DTYPE = jnp.bfloat16
HI = lax.Precision.HIGHEST

N_DEV = 8
D_MODEL = 1024
DEPTH = 2
SSD_HEADS, SSD_HEAD_DIM, SSD_INNER, SSD_GROUPS, SSD_STATE, SSD_CHUNK = 16, 64, 1024, 2, 128, 64
SSD_XBC = SSD_INNER + 2 * SSD_GROUPS * SSD_STATE
GDN_HEADS, GDN_HEAD_DIM, GDN_WIDTH, GDN_CHUNK = 4, 128, 512, 64
GLA_HEADS, GLA_KEY_DIM, GLA_VAL_DIM, GLA_K, GLA_V, GLA_RANK, GLA_CHUNK = 4, 64, 128, 256, 512, 16, 16
GLA_BLOCK = 128
GLA_NORMALIZER = 16.0
FFN_DIM = 2816
FFN_HALF = FFN_DIM // 8
FFN_HALF_PAD = 384
FFN_UP_PAD = 16 * FFN_HALF_PAD
FFN_PAD = FFN_UP_PAD // 2
ALPHA = (2 * DEPTH) ** 0.25
LN_EPS = 1e-5
RMS_EPS = 1e-6
ADAM_LR, ADAM_B1, ADAM_B2, ADAM_EPS, ADAM_WD, ADAM_STEP = 0.001, 0.9, 0.999, 1e-08, 0.01, 10
LANES = 128
NEG_BIG = -1e30
VMEM_LIMIT = 56 * 1024 * 1024

IN_SPLITS = (("z", 1024), ("xbc", 1536), ("dt", 16), ("gqkv", 1536), ("ga", 4), ("gb", 4), ("gg", 512),
             ("lqkv", 1024), ("lglr", 16), ("lr", 512), ("gates", 3072))
IN_DIM = sum(w for _, w in IN_SPLITS)
PAD_SEGS = (("gates", 0, 3072, (("gates", 0),)), ("xbc", 3072, 1536, (("xbc", 0),)),
            ("gqkv", 4608, 1536, (("gqkv", 0),)), ("z", 6144, 1024, (("z", 0),)),
            ("lqkv", 7168, 1024, (("lqkv", 0),)), ("gg", 8192, 512, (("gg", 0),)), ("lr", 8704, 512, (("lr", 0),)),
            ("dt", 9216, 128, (("dt", 0),)), ("gab", 9344, 128, (("ga", 0), ("gb", 4))), ("lglr", 9472, 128, (("lglr", 0),)))
IN_PAD = 9728
SEG = {name: (off, width) for name, off, width, _ in PAD_SEGS}

BIG = ("w_in", "w_br_ssd", "w_br_gdn", "w_br_gla", "w_out", "ffn_w_up", "ffn_w_down")
COL_SHARDED = ("w_in", "w_br_gdn", "w_br_gla", "ffn_w_up")
SMALL_SHARDED = ("ssd_conv_w", "gdn_conv_w", "gla_gate_w2", "ffn_conv_w")
WEIGHTS = ("w_in", "ssd_conv_w", "ssd_conv_b", "ssd_dt_bias", "ssd_a_log", "ssd_d", "ssd_norm_w", "gdn_conv_w",
           "gdn_a_log", "gdn_dt_bias", "gdn_norm_w", "gla_gate_w2", "gla_gate_b", "gla_norm_w", "w_br_ssd", "w_br_gdn",
           "w_br_gla", "w_out", "ln1_g", "ln1_b", "ffn_w_up", "ffn_conv_w", "ffn_conv_b", "ffn_w_down", "ln2_g", "ln2_b")
SMALL = tuple(n for n in WEIGHTS if n not in BIG)
FLAT_W = 512


def _cparams(sem=None):
    kw = dict(vmem_limit_bytes=VMEM_LIMIT)
    if sem is not None:
        kw["dimension_semantics"] = sem
    return pltpu.CompilerParams(**kw)


_DIMS = {"nn": (((1,), (0,)), ((), ())), "nt": (((1,), (1,)), ((), ())), "tn": (((0,), (0,)), ((), ()))}


def _dot(a, b, dims="nn"):
    if MXU_DTYPE == F32:
        return lax.dot_general(a.astype(F32), b.astype(F32), _DIMS[dims], precision=HI, preferred_element_type=F32)
    return lax.dot_general(a.astype(MXU_DTYPE), b.astype(MXU_DTYPE), _DIMS[dims], preferred_element_type=F32)


def _dot_hi(a, b, dims="nn"):
    return lax.dot_general(a.astype(F32), b.astype(F32), _DIMS[dims], precision=HI, preferred_element_type=F32)


def _iota2(shape, axis):
    return lax.broadcasted_iota(jnp.int32, shape, axis)


def _tril(n, strict=False):
    r, c = _iota2((n, n), 0), _iota2((n, n), 1)
    return (r > c) if strict else (r >= c)


def _raw_dot(a, b, dims):
    return lax.dot_general(a, b, _DIMS[dims], preferred_element_type=F32)


def _dot_x3(a, b, dims="nn"):
    if MXU_DTYPE == F32:
        return _dot_hi(a, b, dims)
    ah, bh = a.astype(jnp.bfloat16), b.astype(jnp.bfloat16)
    al, bl = (a - ah.astype(F32)).astype(jnp.bfloat16), (b - bh.astype(F32)).astype(jnp.bfloat16)
    return _raw_dot(ah, bh, dims) + (_raw_dot(ah, bl, dims) + _raw_dot(al, bh, dims))


def _exact_dot(mask, b, dims, mask_first):
    if MXU_DTYPE == F32:
        return _dot_hi(mask, b, dims) if mask_first else _dot_hi(b, mask, dims)
    m = mask.astype(jnp.bfloat16)
    b1 = b.astype(jnp.bfloat16)
    r1 = b - b1.astype(F32)
    b2 = r1.astype(jnp.bfloat16)
    b3 = (r1 - b2.astype(F32)).astype(jnp.bfloat16)
    if mask_first:
        return _raw_dot(m, b1, dims) + (_raw_dot(m, b2, dims) + _raw_dot(m, b3, dims))
    return _raw_dot(b1, m, dims) + (_raw_dot(b2, m, dims) + _raw_dot(b3, m, dims))


@jax.custom_vjp
def _mask_left(mask, b):
    return _exact_dot(mask, b, "nn", True)


_mask_left.defvjp(lambda mask, b: (_mask_left(mask, b), mask),
                  lambda mask, d: (jnp.zeros_like(mask), _exact_dot(mask, d, "tn", True)))


@jax.custom_vjp
def _mask_right(a, mask):
    return _exact_dot(mask, a, "nn", False)


_mask_right.defvjp(lambda a, mask: (_mask_right(a, mask), mask),
                   lambda mask, d: (_exact_dot(mask, d, "nt", False), jnp.zeros_like(mask)))


@jax.custom_vjp
def _unit_lower_inverses(mats):
    n = mats[0].shape[0]
    eye = (_iota2((n, n), 0) == _iota2((n, n), 1)).astype(F32)
    xs = [eye - a for a in mats]
    ps = list(mats)
    k = 2
    while k < n:
        ps = [_dot_x3(p, p) for p in ps]
        xs = [x + _dot_x3(x, p) for x, p in zip(xs, ps)]
        k *= 2
    return xs


def _unit_lower_inverses_fwd(mats):
    ts = _unit_lower_inverses(mats)
    return ts, ts


def _unit_lower_inverses_bwd(ts, dts):
    mids = [_dot_x3(t, d, "tn") for t, d in zip(ts, dts)]
    return ([-_dot_x3(m, t, "nt") for m, t in zip(mids, ts)],)


_unit_lower_inverses.defvjp(_unit_lower_inverses_fwd, _unit_lower_inverses_bwd)


def _ssd_chunk(xs_, ps_, s_t):
    xbc, dtraw, z = xs_
    dt_bias, a_log, d_skip, norm_w = ps_
    L = xbc.shape[0]
    H, P, N, G = SSD_HEADS, SSD_HEAD_DIM, SSD_STATE, SSD_GROUPS
    W = SSD_INNER // G
    xs = xbc[:, :SSD_INNER]
    bm = xbc[:, SSD_INNER:SSD_INNER + G * N]
    cm = xbc[:, SSD_INNER + G * N:]
    dt = jax.nn.softplus(dtraw[:, :H] + dt_bias)
    a = dt * (-jnp.exp(a_log))
    causal = _tril(L)
    a_cs = _mask_left(causal.astype(F32), a)
    expand = (_iota2((H, SSD_INNER), 1) // P == _iota2((H, SSD_INNER), 0)).astype(F32)
    wide = _mask_right(jnp.concatenate([a_cs, dt, jnp.broadcast_to(d_skip, (L, H))], axis=0), expand)
    a_cs_x, dt_x, d_x = wide[:L], wide[L:2 * L], wide[2 * L:]
    a_end_x = a_cs_x[L - 1:L, :]
    a_cs_t, dt_t = a_cs.T, dt.T
    cb = [_dot(cm[:, g * N:(g + 1) * N], bm[:, g * N:(g + 1) * N], "nt") for g in range(G)]
    decay = [jnp.exp(jnp.where(causal, a_cs[:, h:h + 1] - a_cs_t[h:h + 1, :], NEG_BIG)) * dt_t[h:h + 1, :] for h in range(H)]
    ws = [cb[h // (H // G)] * decay[h] for h in range(H)]
    y = jnp.concatenate([_dot(ws[h], xs[:, h * P:(h + 1) * P]) for h in range(H)], axis=1)
    y_in = jnp.concatenate([_dot(cm[:, g * N:(g + 1) * N], s_t[:, g * W:(g + 1) * W]) for g in range(G)], axis=1)
    y = y + y_in * jnp.exp(a_cs_x) + d_x * xs
    xw = xs * (jnp.exp(a_end_x - a_cs_x) * dt_x)
    st = jnp.concatenate([_dot(bm[:, g * N:(g + 1) * N], xw[:, g * W:(g + 1) * W], "tn") for g in range(G)], axis=1)
    s_new = s_t * jnp.exp(a_end_x) + st
    yg = y * jax.nn.silu(z)
    outs = []
    for g in range(G):
        part = yg[:, g * W:(g + 1) * W]
        outs.append(part * lax.rsqrt(jnp.mean(part * part, axis=1, keepdims=True) + RMS_EPS))
    return (jnp.concatenate(outs, axis=1) * norm_w,), s_new


GDN_PREP_CHUNKS = 4


def _gdn_prep(xs_, ps_):
    qkv, ab = xs_
    a_log, dt_bias = ps_
    B = qkv.shape[0]
    H, D, L = GDN_HEADS, GDN_HEAD_DIM, GDN_CHUNK
    g_all = -jnp.exp(a_log) * jax.nn.softplus(ab + dt_bias)
    row, col = _iota2((B, B), 0), _iota2((B, B), 1)
    g_cs = _mask_left((((row // L) == (col // L)) & (row >= col)).astype(F32), g_all)
    g_cs_t = g_cs.T
    beta_all = jax.nn.sigmoid(ab)
    incl, strict = _tril(L), _tril(L, strict=True)
    qs, ks, vs = [], [], []
    for h in range(H):
        q = qkv[:, h * D:(h + 1) * D]
        k = qkv[:, GDN_WIDTH + h * D:GDN_WIDTH + (h + 1) * D]
        qs.append(q * lax.rsqrt(jnp.sum(q * q, axis=1, keepdims=True) + RMS_EPS) * (D ** -0.5))
        ks.append(k * lax.rsqrt(jnp.sum(k * k, axis=1, keepdims=True) + RMS_EPS))
        vs.append(qkv[:, 2 * GDN_WIDTH + h * D:2 * GDN_WIDTH + (h + 1) * D])
    pairs = [(c, h) for c in range(B // L) for h in range(H)]
    rows = {c: slice(c * L, (c + 1) * L) for c in range(B // L)}
    q_ = {(c, h): qs[h][rows[c]] for c, h in pairs}
    k_ = {(c, h): ks[h][rows[c]] for c, h in pairs}
    col_ = {(c, h): g_cs[rows[c], h:h + 1] for c, h in pairs}
    beta_ = {(c, h): beta_all[rows[c], H + h:H + h + 1] for c, h in pairs}
    gamma = {p: jnp.exp(jnp.where(incl, col_[p] - g_cs_t[p[1]:p[1] + 1, rows[p[0]]], NEG_BIG)) for p in pairs}
    kb = {p: k_[p] * beta_[p] for p in pairs}
    a_mat = [jnp.where(strict, _dot(kb[p], k_[p], "nt") * gamma[p], 0.0) for p in pairs]
    attn = {p: jnp.where(incl, _dot(q_[p], k_[p], "nt") * gamma[p], 0.0) for p in pairs}
    t_mat = dict(zip(pairs, _unit_lower_inverses(a_mat)))
    u = {p: _dot(t_mat[p], vs[p[1]][rows[p[0]]] * beta_[p]) for p in pairs}
    w = {p: _dot(t_mat[p], kb[p] * jnp.exp(col_[p])) for p in pairs}
    qd = {p: q_[p] * jnp.exp(col_[p]) for p in pairs}
    kd = {p: k_[p] * jnp.exp(col_[p][L - 1:L, :] - col_[p]) for p in pairs}

    def whole(parts):
        return jnp.concatenate([jnp.concatenate([parts[(c, h)] for h in range(H)], axis=1) for c in range(B // L)], axis=0)

    return (whole(u), whole(w), whole(qd), whole(kd), whole(attn), g_cs)


def _gdn_scan(xs_, ps_, s):
    u, w, qd, kd, attn, g_cs, gate = xs_
    (norm_w,) = ps_
    L = u.shape[0]
    H, D = GDN_HEADS, GDN_HEAD_DIM
    heads = range(H)
    lanes = [slice(h * D, (h + 1) * D) for h in heads]
    s_h = [s[lanes[h], :] for h in heads]
    v_new = [u[:, lanes[h]] - _dot(w[:, lanes[h]], s_h[h]) for h in heads]
    o = [_dot(qd[:, lanes[h]], s_h[h]) + _dot(attn[:, h * L:(h + 1) * L], v_new[h]) for h in heads]
    decay = [jnp.exp(g_cs[L - 1:L, h:h + 1]) for h in heads]
    s_new = [s_h[h] * decay[h] + _dot(kd[:, lanes[h]], v_new[h], "tn") for h in heads]
    o = [o[h] * lax.rsqrt(jnp.mean(o[h] * o[h], axis=1, keepdims=True) + RMS_EPS) * norm_w * jax.nn.silu(gate[:, lanes[h]])
         for h in heads]
    return (jnp.concatenate(o, axis=1),), jnp.concatenate(s_new, axis=0)


def _gdn_forward(tag, gqkv, h, sp):
    T = gqkv.shape[0]
    blk = GDN_PREP_CHUNKS * GDN_CHUNK
    prep_in = [(gqkv, blk, 3 * GDN_WIDTH, 0), _seg_blk(h, "gab", blk)]
    prep_p = [_lane_pad(sp["gdn_a_log"]), _lane_pad(sp["gdn_dt_bias"])]
    mx = MXU_DTYPE
    prep = _chain_fwd(f"gdn_prep_{tag}", _gdn_prep, T // blk, prep_in, prep_p,
                      [(blk, GDN_WIDTH, F32), (blk, GDN_WIDTH, mx), (blk, GDN_WIDTH, mx), (blk, GDN_WIDTH, mx),
                       (blk, GDN_HEADS * GDN_CHUNK, mx), (blk, LANES, F32)])
    widths = [GDN_WIDTH] * 4 + [GDN_HEADS * GDN_CHUNK, LANES]
    scan_in = [(a, GDN_CHUNK, wd, 0) for a, wd in zip(prep, widths)] + [_seg_blk(h, "gg", GDN_CHUNK)]
    scan_p = [sp["gdn_norm_w"]]
    o, states = _chain_fwd(f"gdn_scan_{tag}", _gdn_scan, T // GDN_CHUNK, scan_in, scan_p, [(GDN_CHUNK, GDN_WIDTH, mx)],
                           (GDN_WIDTH, GDN_HEAD_DIM))
    return o, dict(prep_in=prep_in, prep_p=prep_p, scan_in=scan_in, scan_p=scan_p, states=states, widths=widths)


def _gdn_backward(tag, do, sv, dx_dtype):
    T = do.shape[0]
    blk = GDN_PREP_CHUNKS * GDN_CHUNK
    dscan, (dnorm,) = _chain_bwd(f"gdn_scan_bwd_{tag}", _gdn_scan, T // GDN_CHUNK, sv["scan_in"], sv["scan_p"],
                                 [(do, GDN_CHUNK, GDN_WIDTH)], sprev=sv["states"], dx_dtypes=[F32] * 6 + [dx_dtype])
    douts = [(d, blk, wd) for d, wd in zip(dscan[:6], sv["widths"])]
    (dgqkv, dgab), (da_log, ddt_bias) = _chain_bwd(f"gdn_prep_bwd_{tag}", _gdn_prep, T // blk, sv["prep_in"], sv["prep_p"],
                                                   douts, dx_dtypes=[F32, dx_dtype])
    return dgqkv, dgab, dscan[6], da_log[:, :GDN_HEADS], ddt_bias[:, :GDN_HEADS], dnorm


def _gla_block(xs_, ps_, s_t):
    qkv, glr, r = xs_
    w2, gate_b, norm_w = ps_
    B = qkv.shape[0]
    H, K, V, C = GLA_HEADS, GLA_KEY_DIM, GLA_VAL_DIM, GLA_CHUNK
    q = qkv[:, :GLA_K] * (K ** -0.5)
    k = qkv[:, GLA_K:2 * GLA_K]
    v = qkv[:, 2 * GLA_K:]
    gk = jax.nn.log_sigmoid(_dot(glr, w2) + gate_b) / GLA_NORMALIZER
    row, col = _iota2((B, B), 0), _iota2((B, B), 1)
    same = (row // C) == (col // C)
    mask = same & (row >= col)
    b_cs = _mask_left(mask.astype(F32), gk)
    b_end = _mask_left((col == (row // C) * C + (C - 1)).astype(F32), b_cs)
    q_e = q * jnp.exp(b_cs)
    k_e = k * jnp.exp(-b_cs)
    k_d = k * jnp.exp(b_end - b_cs)
    intra = []
    for h in range(H):
        a_mat = jnp.where(mask, _dot(q_e[:, h * K:(h + 1) * K], k_e[:, h * K:(h + 1) * K], "nt"), 0.0)
        intra.append(_dot(a_mat, v[:, h * V:(h + 1) * V]))
    o = jnp.concatenate(intra, axis=1)
    chunks = [slice(j * C, (j + 1) * C) for j in range(B // C)]
    fresh = [jnp.concatenate([_dot(v[sl, h * V:(h + 1) * V], k_d[sl, h * K:(h + 1) * K], "tn") for h in range(H)], axis=1)
             for sl in chunks]
    entering = []
    for j, sl in enumerate(chunks):
        entering.append(s_t)
        s_t = s_t * jnp.exp(b_end[j * C:j * C + 1, :]) + fresh[j]
    inter = [jnp.concatenate([_dot(q_e[sl, h * K:(h + 1) * K], entering[j][:, h * K:(h + 1) * K], "nt") for h in range(H)],
                             axis=1) for j, sl in enumerate(chunks)]
    o = o + jnp.concatenate(inter, axis=0)
    outs = []
    for h in range(H):
        oh = o[:, h * V:(h + 1) * V]
        oh = oh * lax.rsqrt(jnp.mean(oh * oh, axis=1, keepdims=True) + RMS_EPS) * norm_w
        outs.append(oh * jax.nn.silu(r[:, h * V:(h + 1) * V]))
    return (jnp.concatenate(outs, axis=1),), s_t


def _merge_fn(xs_, ps_):
    gates, y_ssd, y_gdn, y_gla = xs_
    d = D_MODEL
    return (jax.nn.sigmoid(gates[:, :d]) * y_ssd + jax.nn.sigmoid(gates[:, d:2 * d]) * y_gdn
            + jax.nn.sigmoid(gates[:, 2 * d:]) * y_gla,)


def _ln_fn(xs_, ps_):
    x, r = xs_
    g, b = ps_
    t = ALPHA * x + r
    mu = jnp.mean(t, axis=1, keepdims=True)
    var = jnp.mean(jnp.square(t - mu), axis=1, keepdims=True)
    return ((t - mu) * lax.rsqrt(var + LN_EPS) * g + b,)


def _row_spec(rows, width, colblk, n, reverse):
    if reverse:
        return pl.BlockSpec((rows, width), lambda c: (n - 1 - c, colblk))
    return pl.BlockSpec((rows, width), lambda c: (c, colblk))


def _full_spec(shape):
    zeros = (0,) * len(shape)
    return pl.BlockSpec(shape, lambda c: zeros)


def _chain_fwd(name, fn, n, blocked, full, out_defs, state_shape=None):
    nb, nf, no = len(blocked), len(full), len(out_defs)

    def body(*refs):
        xs = [r[...].astype(F32) for r in refs[:nb]]
        ps = [r[...] for r in refs[nb:nb + nf]]
        o_refs = refs[nb + nf:nb + nf + no]
        if state_shape is None:
            outs = fn(xs, ps)
        else:
            sprev_ref, s_ref = refs[nb + nf + no:]

            @pl.when(pl.program_id(0) == 0)
            def _():
                s_ref[...] = jnp.zeros_like(s_ref)

            s = s_ref[...]
            sprev_ref[0] = s
            outs, s_new = fn(xs, ps, s)
            s_ref[...] = s_new
        for r, o in zip(o_refs, outs):
            r[...] = o.astype(r.dtype)

    in_specs = [_row_spec(rows, width, cb, n, False) for _, rows, width, cb in blocked]
    in_specs += [_full_spec(a.shape) for a in full]
    out_specs = [_row_spec(rows, width, 0, n, False) for rows, width, _ in out_defs]
    out_shape = [jax.ShapeDtypeStruct((n * rows, width), dt) for rows, width, dt in out_defs]
    scratch = []
    if state_shape is not None:
        out_specs.append(pl.BlockSpec((1,) + state_shape, lambda c: (c, 0, 0)))
        out_shape.append(jax.ShapeDtypeStruct((n,) + state_shape, F32))
        scratch.append(pltpu.VMEM(state_shape, F32))
    return pl.pallas_call(body, name=name, grid=(n,), in_specs=in_specs, out_specs=out_specs, out_shape=out_shape,
                          scratch_shapes=scratch, compiler_params=_cparams(("arbitrary",)))(
        *[a for a, _, _, _ in blocked], *full)


def _chain_bwd(name, fn, n, blocked, full, douts, sprev=None, dx_dtypes=None):
    nb, nf, nd = len(blocked), len(full), len(douts)
    has_state = sprev is not None
    dx_dtypes = dx_dtypes or [F32] * nb

    def body(*refs):
        pos = 0
        b_refs = refs[pos:pos + nb]; pos += nb
        f_refs = refs[pos:pos + nf]; pos += nf
        d_refs = refs[pos:pos + nd]; pos += nd
        if has_state:
            sprev_ref = refs[pos]; pos += 1
        dx_refs = refs[pos:pos + nb]; pos += nb
        dp_refs = refs[pos:pos + nf]; pos += nf
        if has_state:
            ds_ref = refs[pos]

        @pl.when(pl.program_id(0) == 0)
        def _():
            for r in dp_refs:
                r[...] = jnp.zeros_like(r)
            if has_state:
                ds_ref[...] = jnp.zeros_like(ds_ref)

        xs = [r[...].astype(F32) for r in b_refs]
        ps = [r[...] for r in f_refs]
        dys = tuple(r[...].astype(F32) for r in d_refs)
        if has_state:
            _, vjp = jax.vjp(fn, xs, ps, sprev_ref[0])
            dxs, dps, ds = vjp((dys, ds_ref[...]))
            ds_ref[...] = ds
        else:
            _, vjp = jax.vjp(fn, xs, ps)
            dxs, dps = vjp(dys)
        for r, d in zip(dx_refs, dxs):
            r[...] = d.astype(r.dtype)
        for r, d in zip(dp_refs, dps):
            r[...] += d

    in_specs = [_row_spec(rows, width, cb, n, True) for _, rows, width, cb in blocked]
    in_specs += [_full_spec(a.shape) for a in full]
    in_specs += [_row_spec(rows, width, 0, n, True) for _, rows, width in douts]
    args = [a for a, _, _, _ in blocked] + list(full) + [a for a, _, _ in douts]
    scratch = []
    if has_state:
        st_shape = sprev.shape[1:]
        in_specs.append(pl.BlockSpec((1,) + st_shape, lambda c: (n - 1 - c, 0, 0)))
        args.append(sprev)
        scratch.append(pltpu.VMEM(st_shape, F32))
    out_specs = [_row_spec(rows, width, 0, n, True) for _, rows, width, _ in blocked]
    out_specs += [_full_spec(a.shape) for a in full]
    out_shape = [jax.ShapeDtypeStruct((n * rows, width), dt) for (_, rows, width, _), dt in zip(blocked, dx_dtypes)]
    out_shape += [jax.ShapeDtypeStruct(a.shape, F32) for a in full]
    res = pl.pallas_call(body, name=name, grid=(n,), in_specs=in_specs, out_specs=out_specs, out_shape=out_shape,
                         scratch_shapes=scratch, compiler_params=_cparams(("arbitrary",)))(*args)
    return res[:nb], res[nb:]


def _tile(n, target, unit):
    if n <= target:
        return n
    best = None
    for t in range(unit, target + 1, unit):
        if n % t == 0:
            best = t
    assert best is not None, (n, target, unit)
    return best


def _mm(name, a, b, dims="nn", out_dtype=F32, tm=2048, tn=512, tk=2048):
    if dims == "nn":
        (M, K), (_, N) = a.shape, b.shape
    elif dims == "nt":
        (M, K), (N, _) = a.shape, b.shape
    else:
        (K, M), (_, N) = a.shape, b.shape
    tm, tn, tk = _tile(M, tm, LANES), _tile(N, tn, LANES), _tile(K, tk, LANES)
    nk = K // tk

    def body(a_ref, b_ref, o_ref, acc_ref):
        part = _dot(a_ref[...], b_ref[...], dims)
        if nk == 1:
            o_ref[...] = part.astype(o_ref.dtype)
            return

        @pl.when(pl.program_id(2) == 0)
        def _():
            acc_ref[...] = part

        @pl.when(pl.program_id(2) > 0)
        def _():
            acc_ref[...] += part

        @pl.when(pl.program_id(2) == nk - 1)
        def _():
            o_ref[...] = acc_ref[...].astype(o_ref.dtype)

    if dims == "tn":
        a_spec = pl.BlockSpec((tk, tm), lambda j, i, k: (k, i))
    else:
        a_spec = pl.BlockSpec((tm, tk), lambda j, i, k: (i, k))
    if dims == "nt":
        b_spec = pl.BlockSpec((tn, tk), lambda j, i, k: (j, k))
    else:
        b_spec = pl.BlockSpec((tk, tn), lambda j, i, k: (k, j))
    return pl.pallas_call(
        body, name=name, grid=(N // tn, M // tm, nk), in_specs=[a_spec, b_spec],
        out_specs=pl.BlockSpec((tm, tn), lambda j, i, k: (i, j)), out_shape=jax.ShapeDtypeStruct((M, N), out_dtype),
        scratch_shapes=[pltpu.VMEM((tm, tn) if nk > 1 else (8, LANES), F32)],
        compiler_params=_cparams(("parallel", "parallel", "arbitrary")))(a, b)


CONV_CB = 256


def _shift_down(x, k):
    if k == 0:
        return x
    return jnp.where(_iota2(x.shape, 0) >= k, pltpu.roll(x, k, 0), 0.0)


def _shift_up(x, k):
    if k == 0:
        return x
    t = x.shape[0]
    return jnp.where(_iota2(x.shape, 0) < t - k, pltpu.roll(x, t - k, 0), 0.0)


def _conv_pre(x, w, b):
    kk = w.shape[0]
    pre = x * w[kk - 1:kk, :]
    for k in range(kk - 1):
        pre = pre + _shift_down(x, kk - 1 - k) * w[k:k + 1, :]
    return pre if b is None else pre + b


def _conv_bwd_pre(x, w, dpre, dw_ref, db_ref):
    kk = w.shape[0]
    dx = dpre * w[kk - 1:kk, :]
    dw_ref[kk - 1:kk, :] = jnp.sum(dpre * x, axis=0, keepdims=True)
    for k in range(kk - 1):
        dx = dx + _shift_up(dpre, kk - 1 - k) * w[k:k + 1, :]
        dw_ref[k:k + 1, :] = jnp.sum(dpre * _shift_down(x, kk - 1 - k), axis=0, keepdims=True)
    if db_ref is not None:
        db_ref[...] = jnp.sum(dpre, axis=0, keepdims=True)
    return dx


def _dsilu(pre):
    sg = jax.nn.sigmoid(pre)
    return sg * (1.0 + pre * (1.0 - sg))


def _conv_silu_fwd(name, src, col0, w, b):
    T = src.shape[0]
    kk, C = w.shape
    cb = CONV_CB
    off = col0 // cb

    def body(*refs):
        x_ref, w_ref = refs[:2]
        b_val = refs[2][...] if b is not None else None
        refs[-1][...] = jax.nn.silu(_conv_pre(x_ref[...], w_ref[...], b_val))

    in_specs = [pl.BlockSpec((T, cb), lambda j: (0, off + j)), pl.BlockSpec((kk, cb), lambda j: (0, j))]
    args = [src, w]
    if b is not None:
        in_specs.append(pl.BlockSpec((1, cb), lambda j: (0, j)))
        args.append(b)
    return pl.pallas_call(body, name=name, grid=(C // cb,), in_specs=in_specs,
                          out_specs=pl.BlockSpec((T, cb), lambda j: (0, j)), out_shape=jax.ShapeDtypeStruct((T, C), F32),
                          compiler_params=_cparams(("parallel",)))(*args)


def _conv_silu_bwd(name, src, col0, w, b, dy, dx_dtype):
    T = src.shape[0]
    kk, C = w.shape
    cb = CONV_CB
    off = col0 // cb
    has_b = b is not None

    def body(*refs):
        x_ref, w_ref = refs[:2]
        pos = 2
        b_val = None
        if has_b:
            b_val = refs[pos][...]; pos += 1
        dy_ref = refs[pos]; pos += 1
        dx_ref, dw_ref = refs[pos], refs[pos + 1]
        db_ref = refs[pos + 2] if has_b else None
        x, wv = x_ref[...], w_ref[...]
        dpre = dy_ref[...] * _dsilu(_conv_pre(x, wv, b_val))
        dx_ref[...] = _conv_bwd_pre(x, wv, dpre, dw_ref, db_ref).astype(dx_ref.dtype)

    in_specs = [pl.BlockSpec((T, cb), lambda j: (0, off + j)), pl.BlockSpec((kk, cb), lambda j: (0, j))]
    args = [src, w]
    if has_b:
        in_specs.append(pl.BlockSpec((1, cb), lambda j: (0, j)))
        args.append(b)
    in_specs.append(pl.BlockSpec((T, cb), lambda j: (0, j)))
    args.append(dy)
    out_specs = [pl.BlockSpec((T, cb), lambda j: (0, j)), pl.BlockSpec((kk, cb), lambda j: (0, j))]
    out_shape = [jax.ShapeDtypeStruct((T, C), dx_dtype), jax.ShapeDtypeStruct((kk, C), F32)]
    if has_b:
        out_specs.append(pl.BlockSpec((1, cb), lambda j: (0, j)))
        out_shape.append(jax.ShapeDtypeStruct((1, C), F32))
    return pl.pallas_call(body, name=name, grid=(C // cb,), in_specs=in_specs, out_specs=out_specs, out_shape=out_shape,
                          compiler_params=_cparams(("parallel",)))(*args)


def _ffn_glu_fwd(name, up, w, b, out_dtype=F32):
    T = up.shape[0]
    kk = w.shape[0]
    cb = CONV_CB
    width = up.shape[1] // 2
    nblk = width // cb

    def body(g_ref, u_ref, wg_ref, wu_ref, bg_ref, bu_ref, o_ref):
        g = _conv_pre(g_ref[...], wg_ref[...], bg_ref[...])
        u = _conv_pre(u_ref[...], wu_ref[...], bu_ref[...])
        o_ref[...] = (jax.nn.silu(g) * u).astype(o_ref.dtype)

    lo, hi = (lambda j: (0, j)), (lambda j: (0, nblk + j))
    in_specs = [pl.BlockSpec((T, cb), lo), pl.BlockSpec((T, cb), hi), pl.BlockSpec((kk, cb), lo), pl.BlockSpec((kk, cb), hi),
                pl.BlockSpec((1, cb), lo), pl.BlockSpec((1, cb), hi)]
    return pl.pallas_call(body, name=name, grid=(nblk,), in_specs=in_specs, out_specs=pl.BlockSpec((T, cb), lo),
                          out_shape=jax.ShapeDtypeStruct((T, width), out_dtype),
                          compiler_params=_cparams(("parallel",)))(up, up, w, w, b, b)


def _ffn_glu_bwd(name, up, w, b, dact, dx_dtype):
    T = up.shape[0]
    kk = w.shape[0]
    cb = CONV_CB
    width = up.shape[1] // 2
    nblk = width // cb

    def body(g_ref, u_ref, wg_ref, wu_ref, bg_ref, bu_ref, d_ref, dg_ref, du_ref, dwg_ref, dwu_ref, dbg_ref, dbu_ref):
        xg, xu, wg, wu = g_ref[...], u_ref[...], wg_ref[...], wu_ref[...]
        g = _conv_pre(xg, wg, bg_ref[...])
        u = _conv_pre(xu, wu, bu_ref[...])
        d = d_ref[...].astype(F32)
        dg_ref[...] = _conv_bwd_pre(xg, wg, d * u * _dsilu(g), dwg_ref, dbg_ref).astype(dg_ref.dtype)
        du_ref[...] = _conv_bwd_pre(xu, wu, d * jax.nn.silu(g), dwu_ref, dbu_ref).astype(du_ref.dtype)

    lo, hi = (lambda j: (0, j)), (lambda j: (0, nblk + j))
    in_specs = [pl.BlockSpec((T, cb), lo), pl.BlockSpec((T, cb), hi), pl.BlockSpec((kk, cb), lo), pl.BlockSpec((kk, cb), hi),
                pl.BlockSpec((1, cb), lo), pl.BlockSpec((1, cb), hi), pl.BlockSpec((T, cb), lo)]
    out_specs = [pl.BlockSpec((T, cb), lo)] * 2 + [pl.BlockSpec((kk, cb), lo)] * 2 + [pl.BlockSpec((1, cb), lo)] * 2
    out_shape = ([jax.ShapeDtypeStruct((T, width), dx_dtype)] * 2 + [jax.ShapeDtypeStruct((kk, width), F32)] * 2
                 + [jax.ShapeDtypeStruct((1, width), F32)] * 2)
    return pl.pallas_call(body, name=name, grid=(nblk,), in_specs=in_specs, out_specs=out_specs, out_shape=out_shape,
                          compiler_params=_cparams(("parallel",)))(up, up, w, w, b, b, dact)


def _loss_head(y, target):
    T, D = y.shape
    tb = _tile(T, 256, 8)

    def body(y_ref, t_ref, dy_ref, l_ref):
        @pl.when(pl.program_id(0) == 0)
        def _():
            l_ref[...] = jnp.zeros_like(l_ref)

        err = y_ref[...] - t_ref[...]
        dy_ref[...] = err * (1.0 / D)
        l_ref[...] += jnp.sum(err * err, axis=0, keepdims=True) * (0.5 / D)

    spec = pl.BlockSpec((tb, D), lambda i: (i, 0))
    return pl.pallas_call(body, name="loss_head", grid=(T // tb,), in_specs=[spec, spec],
                          out_specs=[spec, pl.BlockSpec((1, D), lambda i: (0, 0))],
                          out_shape=[jax.ShapeDtypeStruct((T, D), F32), jax.ShapeDtypeStruct((1, D), F32)],
                          compiler_params=_cparams(("arbitrary",)))(y, target)


def _adamw_math(w, g, m, v):
    m = ADAM_B1 * m + (1.0 - ADAM_B1) * g
    v = ADAM_B2 * v + (1.0 - ADAM_B2) * jnp.square(g)
    m_hat = m / (1.0 - ADAM_B1 ** ADAM_STEP)
    v_hat = v / (1.0 - ADAM_B2 ** ADAM_STEP)
    return -ADAM_LR * (m_hat / (jnp.sqrt(v_hat) + ADAM_EPS) + ADAM_WD * w), m, v


def _adamw(name, w, g, m, v):
    A, R, C = w.shape
    if C % LANES == 0:
        rb, cb = _slab(R, C)
    else:
        rb, cb = _tile(R, max(8, SLAB_BYTES // 2 // (C * 4) // 8 * 8), 8), C

    def body(w_ref, g_ref, m_ref, v_ref, d_ref, mo_ref, vo_ref):
        d, mn, vn = _adamw_math(w_ref[...], g_ref[...], m_ref[...], v_ref[...])
        d_ref[...] = d
        mo_ref[...] = mn
        vo_ref[...] = vn

    spec = pl.BlockSpec((1, rb, cb), lambda a, r, q: (a, r, q))
    return pl.pallas_call(body, name=name, grid=(A, R // rb, C // cb), in_specs=[spec] * 4, out_specs=[spec] * 3,
                          out_shape=[jax.ShapeDtypeStruct(w.shape, F32)] * 3,
                          compiler_params=_cparams(("parallel", "parallel", "parallel")))(w, g, m, v)


def _adamw_small(parts, w, m, v):
    def body(p_ref, w_ref, m_ref, v_ref, g_ref, d_ref, mo_ref, vo_ref):
        g = p_ref[0]
        for i in range(1, N_DEV):
            g = g + p_ref[i]
        d, mn, vn = _adamw_math(w_ref[...], g, m_ref[...], v_ref[...])
        g_ref[...] = g
        d_ref[...] = d
        mo_ref[...] = mn
        vo_ref[...] = vn

    return pl.pallas_call(body, name="adamw_small", out_shape=[jax.ShapeDtypeStruct(w.shape, F32)] * 4,
                          compiler_params=_cparams())(parts, w, m, v)


def _add_blocks(name, a, b, out_dtype=F32):
    n, R, W = a.shape
    rb = _tile(R, 512, 8)

    def body(a_ref, b_ref, o_ref):
        o_ref[...] = (a_ref[...].astype(F32) + b_ref[...].astype(F32)).astype(o_ref.dtype)

    spec = pl.BlockSpec((1, rb, W), lambda i, r: (i, r, 0))
    return pl.pallas_call(body, name=name, grid=(n, R // rb), in_specs=[spec, spec], out_specs=spec,
                          out_shape=jax.ShapeDtypeStruct(a.shape, out_dtype),
                          compiler_params=_cparams(("parallel", "parallel")))(a, b)


SLAB_BYTES = 1 << 20


def _slab(R, W):
    if R % 16 == 0:
        return _tile(R, max(16, SLAB_BYTES // (4 * W) // 16 * 16), 16), W
    assert W % LANES == 0, (R, W)
    return R, _tile(W, max(LANES, SLAB_BYTES // (4 * R) // LANES * LANES), LANES)


def _pair_add(name, g, other, c, chip):
    _, R, W = g.shape
    rb, cb = _slab(R, W)

    def body(s_ref, a_ref, b_ref, send_ref, own_ref):
        s = a_ref[0] + b_ref[0]
        send_ref[0] = s.astype(send_ref.dtype)

        @pl.when(pl.program_id(2) == s_ref[1])
        def _():
            own_ref[...] = s

    grid_spec = pltpu.PrefetchScalarGridSpec(
        num_scalar_prefetch=1, grid=(R // rb, W // cb, 4),
        in_specs=[pl.BlockSpec((1, rb, cb), lambda r, q, p, s_ref: (2 * p + s_ref[0], r, q)),
                  pl.BlockSpec((1, rb, cb), lambda r, q, p, s_ref: (p, r, q))],
        out_specs=[pl.BlockSpec((1, rb, cb), lambda r, q, p, s_ref: (p, r, q)),
                   pl.BlockSpec((rb, cb), lambda r, q, p, s_ref: (r, q))])
    scalars = jnp.stack([c, chip]).astype(jnp.int32)
    return pl.pallas_call(body, name=name, grid_spec=grid_spec,
                          out_shape=[jax.ShapeDtypeStruct((4, R, W), MXU_DTYPE), jax.ShapeDtypeStruct((R, W), F32)],
                          compiler_params=_cparams(("parallel", "parallel", "arbitrary")))(scalars, g, other)


def _sum4(name, own, parts):
    R, W = own.shape
    rb, cb = _slab(R, W)

    def body(o_ref, p_ref, out_ref):
        out_ref[...] = ((o_ref[...] + p_ref[0].astype(F32)) + p_ref[1].astype(F32)) + p_ref[2].astype(F32)

    return pl.pallas_call(body, name=name, grid=(R // rb, W // cb),
                          in_specs=[pl.BlockSpec((rb, cb), lambda r, q: (r, q)), pl.BlockSpec((3, rb, cb), lambda r, q: (0, r, q))],
                          out_specs=pl.BlockSpec((rb, cb), lambda r, q: (r, q)), out_shape=jax.ShapeDtypeStruct((R, W), F32),
                          compiler_params=_cparams(("parallel", "parallel")))(own, parts)


MESH = pl.DeviceIdType.MESH
ANY = pl.BlockSpec(memory_space=pl.ANY)


def _place():
    return lax.axis_index("x"), lax.axis_index("y"), lax.axis_index("c")


def _other_chips(x, y):
    return [(1 - x, y), (x, 1 - y), (1 - x, 1 - y)]


def _all_gather(name, blocks):
    n = len(blocks)

    def body(*refs):
        x_refs, out_refs = refs[:n], refs[n:2 * n]
        send_sems, recv_sems, local_sems = refs[2 * n:]
        x, y, c = _place()
        me, sibling = (x, y, c), (x, y, 1 - c)
        chips = _other_chips(x, y)

        def slot(a, px, py, pc):
            return out_refs[a].at[4 * px + 2 * py + pc]

        def copy(a, k, blk, to, src=None):
            return pltpu.make_async_remote_copy(src_ref=slot(a, *blk) if src is None else src, dst_ref=slot(a, *blk),
                                                send_sem=send_sems.at[a, k], recv_sem=recv_sems.at[a, k],
                                                device_id=to, device_id_type=MESH)

        mine = [pltpu.make_async_copy(x_refs[a], slot(a, *me), local_sems.at[a]) for a in range(n)]
        for cp in mine:
            cp.start()
        first = []
        for j, chip in enumerate(chips):
            first += [copy(a, 1 + j, me, (*chip, c), src=x_refs[a]) for a in range(n)]
        first += [copy(a, 0, me, sibling, src=x_refs[a]) for a in range(n)]
        for cp in first:
            cp.start()
        passed = []
        for j, chip in enumerate(chips):
            for a in range(n):
                copy(a, 1 + j, (*chip, c), me).wait_recv()
                passed.append(copy(a, 4 + j, (*chip, c), sibling))
                passed[-1].start()
        for a in range(n):
            copy(a, 0, sibling, me).wait_recv()
        for j, chip in enumerate(chips):
            for a in range(n):
                copy(a, 4 + j, (*chip, 1 - c), me).wait_recv()
        for cp in first + passed:
            cp.wait_send()
        for cp in mine:
            cp.wait()

    return pl.pallas_call(body, name=name, in_specs=[ANY] * n, out_specs=[ANY] * n,
                          out_shape=[jax.ShapeDtypeStruct((N_DEV,) + b.shape, b.dtype) for b in blocks],
                          scratch_shapes=[pltpu.SemaphoreType.DMA((n, 7)), pltpu.SemaphoreType.DMA((n, 7)),
                                          pltpu.SemaphoreType.DMA((n,))])(*blocks)


def _routes_to_sibling(x, y, c):
    return [(2 * p + (1 - c), p, (x, y, 1 - c)) for p in range(4)]


def _routes_to_chips(x, y, c):
    return [(2 * px + py, j, (px, py, c)) for j, (px, py) in enumerate(_other_chips(x, y))]


def _routes_block_to_chips(x, y, c):
    me = 4 * x + 2 * y + c
    return [(me, me, (px, py, c)) for px, py in _other_chips(x, y)]


def _routes_blocks_to_sibling(x, y, c):
    return [(4 * px + 2 * py + c, 4 * px + 2 * py + c, (x, y, 1 - c)) for px, py in [(x, y)] + _other_chips(x, y)]


def _route_copies(routes, src_refs, land_refs, send_sems, recv_sems):
    x, y, c = _place()
    copies = []
    for a, (src, land) in enumerate(zip(src_refs, land_refs)):
        plan = routes(x, y, c)
        for k, (s, d, target) in enumerate(plan):
            i = a * len(plan) + k
            copies.append(pltpu.make_async_remote_copy(src_ref=src.at[s], dst_ref=land.at[d], send_sem=send_sems.at[i],
                                                       recv_sem=recv_sems.at[i], device_id=target, device_id_type=MESH))
    return copies


def _exchange(name, routes, n_routes, srcs, land_slots):
    n = len(srcs)

    def body(*refs):
        copies = _route_copies(routes, refs[:n], refs[n:2 * n], refs[2 * n], refs[2 * n + 1])
        for cp in copies:
            cp.start()
        for cp in copies:
            cp.wait_recv()
        for cp in copies:
            cp.wait_send()

    return pl.pallas_call(body, name=name, in_specs=[ANY] * n, out_specs=[ANY] * n,
                          out_shape=[jax.ShapeDtypeStruct((land_slots,) + s.shape[1:], s.dtype) for s in srcs],
                          scratch_shapes=[pltpu.SemaphoreType.DMA((n * n_routes,)), pltpu.SemaphoreType.DMA((n * n_routes,))])(*srcs)


HBM_SPEC = pl.BlockSpec(memory_space=pltpu.HBM)
SEM_SPEC = pl.BlockSpec(memory_space=pltpu.SEMAPHORE)
DATAFLOW = pltpu.SideEffectType.DATAFLOW_SIDE_EFFECTING


def _exchange_start(name, routes, n_routes, srcs, lands, after=None):
    n = len(srcs)
    in_place = lands is None
    bufs = list(srcs) + ([] if in_place else list(lands))
    nb = len(bufs)
    extra = [] if after is None else [after]

    def body(*refs):
        src_refs = refs[:n]
        land_refs = src_refs if in_place else refs[n:nb]
        send_sems, recv_sems = refs[nb + len(extra)], refs[nb + len(extra) + 1]
        token = refs[-1]
        for cp in _route_copies(routes, src_refs, land_refs, send_sems, recv_sems):
            cp.start()
        token[...] = jnp.zeros_like(token)

    sems = [pltpu.SemaphoreType.DMA((n * n_routes,)), pltpu.SemaphoreType.DMA((n * n_routes,))]
    out = pl.pallas_call(
        body, name=name, in_specs=[HBM_SPEC] * nb + [ANY] * len(extra),
        out_shape=sems + [pltpu.HBM(b.shape, b.dtype) for b in bufs] + [jax.ShapeDtypeStruct((8, LANES), F32)],
        out_specs=[SEM_SPEC, SEM_SPEC] + [HBM_SPEC] * nb + [pl.BlockSpec(memory_space=pltpu.VMEM)],
        input_output_aliases={i: 2 + i for i in range(nb)},
        compiler_params=pltpu.CompilerParams(has_side_effects=DATAFLOW))(
        *[pltpu.with_memory_space_constraint(b, pltpu.HBM) for b in bufs], *extra)
    return (out[0], out[1], list(out[2:2 + nb])), out[-1]


def _exchange_wait(name, routes, n_routes, n, started, after):
    send_sems, recv_sems, bufs = started
    nb = len(bufs)
    in_place = nb == n

    def body(*refs):
        src_refs = refs[:n]
        land_refs = src_refs if in_place else refs[n:nb]
        for cp in _route_copies(routes, src_refs, land_refs, refs[nb], refs[nb + 1]):
            cp.wait_send()
            cp.wait_recv()

    out = pl.pallas_call(
        body, name=name, in_specs=[HBM_SPEC] * nb + [SEM_SPEC, SEM_SPEC, ANY],
        out_shape=[pltpu.HBM(b.shape, b.dtype) for b in bufs], out_specs=[HBM_SPEC] * nb,
        input_output_aliases={i: i for i in range(nb)},
        compiler_params=pltpu.CompilerParams(has_side_effects=DATAFLOW))(*bufs, send_sems, recv_sems, after)
    return list(out[:n]) if in_place else list(out[n:])


def _pair_sums(tag, gs, from_sibling):
    x, y, c = _place()
    return [_pair_add(f"rs_add_{tag}_{i}", g, o, c, 2 * x + y) for i, (g, o) in enumerate(zip(gs, from_sibling))]


def _reduce_scatter(tag, gs):
    sums = _pair_sums(tag, gs, _exchange(f"rs_swap_{tag}", _routes_to_sibling, 4, gs, 4))
    got = _exchange(f"rs_chips_{tag}", _routes_to_chips, 3, [s[0] for s in sums], 3)
    return [_sum4(f"rs_sum_{tag}_{i}", s[1], q) for i, (s, q) in enumerate(zip(sums, got))]


def _reduce_scatter_begin(tag, gs):
    lands = [lax.empty((4,) + g.shape[1:], g.dtype) for g in gs]
    swap, token = _exchange_start(f"rs_swap_{tag}_start", _routes_to_sibling, 4, gs, lands)
    return dict(tag=tag, gs=gs, swap=swap), token


def _reduce_scatter_middle(state, after):
    tag, gs = state["tag"], state["gs"]
    from_sibling = _exchange_wait(f"rs_swap_{tag}_wait", _routes_to_sibling, 4, len(gs), state["swap"], after)
    state["sums"] = _pair_sums(tag, gs, from_sibling)
    partials = [s[0] for s in state["sums"]]
    lands = [lax.empty((3,) + p.shape[1:], p.dtype) for p in partials]
    state["chips"], token = _exchange_start(f"rs_chips_{tag}_start", _routes_to_chips, 3, partials, lands)
    return token


def _reduce_scatter_end(state, after):
    tag = state["tag"]
    got = _exchange_wait(f"rs_chips_{tag}_wait", _routes_to_chips, 3, len(state["gs"]), state["chips"], after)
    return [_sum4(f"rs_sum_{tag}_{i}", s[1], q) for i, (s, q) in enumerate(zip(state["sums"], got))]


def _all_gather_begin(tag, blocks, after):
    dev = 4 * lax.axis_index("x") + 2 * lax.axis_index("y") + lax.axis_index("c")
    zones = [lax.dynamic_update_slice_in_dim(lax.empty((N_DEV,) + b.shape, b.dtype), b[None], dev, axis=0) for b in blocks]
    chips, token = _exchange_start(f"gather_{tag}_chips_start", _routes_block_to_chips, 3, zones, None, after)
    return dict(tag=tag, n=len(blocks), chips=chips), token


def _all_gather_middle(state, after):
    tag, n = state["tag"], state["n"]
    zones = _exchange_wait(f"gather_{tag}_chips_wait", _routes_block_to_chips, 3, n, state["chips"], after)
    state["sibling"], token = _exchange_start(f"gather_{tag}_sibling_start", _routes_blocks_to_sibling, 4, zones, None)
    return token


def _all_gather_end(state, after):
    return _exchange_wait(f"gather_{state['tag']}_sibling_wait", _routes_blocks_to_sibling, 4, state["n"], state["sibling"], after)


def _flat_rows(n_elems):
    return -(-n_elems // (FLAT_W * 16)) * 16


def _pack(arrays, dtype):
    flat = jnp.concatenate([a.reshape(-1).astype(dtype) for a in arrays])
    rows = _flat_rows(flat.shape[0])
    flat = jnp.pad(flat, (0, rows * FLAT_W - flat.shape[0]))
    return flat.reshape(rows, FLAT_W)


def _unpack(flat, shapes, lead=()):
    flat = flat.reshape(lead + (-1,))
    out, pos = [], 0
    for s in shapes:
        n = math.prod(s)
        out.append(flat[..., pos:pos + n].reshape(lead + tuple(s)))
        pos += n
    return out


def _ffn_pad_rows(a):
    n = a.shape[0] // FFN_HALF
    a = jnp.pad(a.reshape(n, FFN_HALF, a.shape[1]), ((0, 0), (0, FFN_HALF_PAD - FFN_HALF), (0, 0)))
    return a.reshape(n * FFN_HALF_PAD, a.shape[2])


def _ffn_unpad_rows(a):
    n = a.shape[0] // FFN_HALF_PAD
    return a.reshape(n, FFN_HALF_PAD, a.shape[1])[:, :FFN_HALF].reshape(n * FFN_HALF, a.shape[1])


def _ffn_pad_cols(a):
    n = a.shape[1] // FFN_HALF
    a = jnp.pad(a.reshape(a.shape[0], n, FFN_HALF), ((0, 0), (0, 0), (0, FFN_HALF_PAD - FFN_HALF)))
    return a.reshape(a.shape[0], n * FFN_HALF_PAD)


def _ffn_unpad_cols(a):
    n = a.shape[1] // FFN_HALF_PAD
    return a.reshape(a.shape[0], n, FFN_HALF_PAD)[:, :, :FFN_HALF].reshape(a.shape[0], n * FFN_HALF)


def _shard_to_send(name, shard):
    if name == "w_in":
        shard = shard.T
    elif name == "ffn_w_up":
        shard = _ffn_pad_rows(shard.T)
    return shard.astype(MXU_DTYPE)


def _whole_from_gathered(name, g):
    if name == "w_in":
        return _pad_in_proj_rows(g.reshape(IN_DIM, g.shape[2]))
    if name in ("w_br_gdn", "w_br_gla"):
        return jnp.transpose(g, (1, 0, 2)).reshape(g.shape[1], N_DEV * g.shape[2])
    if name == "ffn_w_down":
        return jnp.pad(g, ((0, 0), (0, FFN_HALF_PAD - FFN_HALF), (0, 0))).reshape(FFN_PAD, g.shape[2])
    return g.reshape(N_DEV * g.shape[1], g.shape[2])


def _slots_from_whole(name, gw):
    if name == "w_in":
        return _unpad_in_proj_rows(gw).reshape(N_DEV, IN_DIM // N_DEV, gw.shape[1])
    if name in ("w_br_gdn", "w_br_gla"):
        return jnp.transpose(gw.reshape(gw.shape[0], N_DEV, gw.shape[1] // N_DEV), (1, 0, 2))
    return gw.reshape(N_DEV, gw.shape[0] // N_DEV, gw.shape[1])


def _shard_from_slot(name, s):
    if name == "ffn_w_up":
        return _ffn_unpad_rows(s)
    if name == "ffn_w_down":
        return s[:FFN_HALF]
    return s


def _in_proj_pieces():
    starts, pos = {}, 0
    for n, width in IN_SPLITS:
        starts[n] = (pos, width)
        pos += width
    return [(starts[ref][0], off + lane, starts[ref][1]) for _, off, _, pieces in PAD_SEGS for ref, lane in pieces]


def _pad_in_proj_rows(w):
    rows, at = [], 0
    for src, dst, n in sorted(_in_proj_pieces(), key=lambda p: p[1]):
        if dst > at:
            rows.append(jnp.zeros((dst - at, w.shape[1]), w.dtype))
        rows.append(w[src:src + n])
        at = dst + n
    rows.append(jnp.zeros((IN_PAD - at, w.shape[1]), w.dtype))
    return jnp.concatenate(rows, axis=0)


def _unpad_in_proj_rows(wp):
    return jnp.concatenate([wp[dst:dst + n] for _, dst, n in sorted(_in_proj_pieces())], axis=0)


def _lane_pad(a, width=LANES):
    return jnp.pad(a, ((0, 0), (0, width - a.shape[1])))


def _seg_blk(h, name, rows):
    off, width = SEG[name]
    return (h, rows, width, off // width)


def _ln_both(xs_, ps_):
    (y,) = _ln_fn(xs_, ps_)
    return (y, y)


def _behind(param, hooks, stage, *seen):
    if hooks is None or stage not in hooks:
        return param
    token = hooks[stage](*seen)
    return param if token is None else param + token[0:1, 0:1]


def _layer_fwd(l, x, x_mx, W, sp, hooks=None):
    T = x.shape[0]
    n64, ngla, ntok = T // SSD_CHUNK, T // GLA_BLOCK, T // 256
    h = _mm(f"in_proj_{l}", x_mx, W["w_in"], "nt")
    xbc = _conv_silu_fwd(f"ssd_conv_{l}", h, SEG["xbc"][0], sp["ssd_conv_w"], sp["ssd_conv_b"])
    gqkv = _conv_silu_fwd(f"gdn_conv_{l}", h, SEG["gqkv"][0], sp["gdn_conv_w"], None)

    ssd_in = [(xbc, SSD_CHUNK, SSD_XBC, 0), _seg_blk(h, "dt", SSD_CHUNK), _seg_blk(h, "z", SSD_CHUNK)]
    ssd_p = [_behind(sp["ssd_dt_bias"], hooks, "projected", h), sp["ssd_a_log"], sp["ssd_d"], sp["ssd_norm_w"]]
    o_ssd, ssd_states = _chain_fwd(f"ssd_fwd_{l}", _ssd_chunk, n64, ssd_in, ssd_p, [(SSD_CHUNK, SSD_INNER, MXU_DTYPE)],
                                   (SSD_STATE, SSD_INNER))
    o_gdn, gdn_saved = _gdn_forward(str(l), gqkv, h, sp)
    gla_in = [_seg_blk(h, "lqkv", GLA_BLOCK), _seg_blk(h, "lglr", GLA_BLOCK), _seg_blk(h, "lr", GLA_BLOCK)]
    gla_p = [jnp.pad(sp["gla_gate_w2"], ((0, LANES - GLA_RANK), (0, 0))), sp["gla_gate_b"], sp["gla_norm_w"]]
    o_gla, gla_states = _chain_fwd(f"gla_fwd_{l}", _gla_block, ngla, gla_in, gla_p, [(GLA_BLOCK, GLA_V, MXU_DTYPE)],
                                   (GLA_VAL_DIM, GLA_K))
    ln1_p = [_behind(sp["ln1_g"], hooks, "mixed", o_gla), sp["ln1_b"]]
    y_ssd = _mm(f"br_ssd_{l}", o_ssd, W["w_br_ssd"])
    y_gdn = _mm(f"br_gdn_{l}", o_gdn, W["w_br_gdn"])
    y_gla = _mm(f"br_gla_{l}", o_gla, W["w_br_gla"])
    merge_in = [_seg_blk(h, "gates", 256), (y_ssd, 256, D_MODEL, 0), (y_gdn, 256, D_MODEL, 0), (y_gla, 256, D_MODEL, 0)]
    (mix,) = _chain_fwd(f"merge_{l}", _merge_fn, ntok, merge_in, [], [(256, D_MODEL, MXU_DTYPE)])
    r1 = _mm(f"out_proj_{l}", mix, W["w_out"])
    both = [(256, D_MODEL, F32), (256, D_MODEL, MXU_DTYPE)]
    x1, x1_mx = _chain_fwd(f"ln1_{l}", _ln_both, ntok, [(x, 256, D_MODEL, 0), (r1, 256, D_MODEL, 0)], ln1_p, both)
    up = _mm(f"ffn_up_{l}", x1_mx, W["ffn_w_up"], "nt")
    ln2_p = [_behind(sp["ln2_g"], hooks, "ffn_up", up), sp["ln2_b"]]
    act = _ffn_glu_fwd(f"ffn_glu_{l}", up, sp["ffn_conv_w_pad"], sp["ffn_conv_b_pad"], MXU_DTYPE)
    r2 = _mm(f"ffn_down_{l}", act, W["ffn_w_down"])
    x2, x2_mx = _chain_fwd(f"ln2_{l}", _ln_both, ntok, [(x1, 256, D_MODEL, 0), (r2, 256, D_MODEL, 0)], ln2_p, both)
    saved = dict(x=x, x_mx=x_mx, h=h, xbc=xbc, gqkv=gqkv, ssd_in=ssd_in, ssd_p=ssd_p, ssd_states=ssd_states,
                 gdn=gdn_saved, gla_in=gla_in, gla_p=gla_p, gla_states=gla_states, o_ssd=o_ssd,
                 o_gdn=o_gdn, o_gla=o_gla, merge_in=merge_in, mix=mix, r1=r1, ln1_p=ln1_p, x1=x1, x1_mx=x1_mx, up=up, act=act,
                 r2=r2, ln2_p=ln2_p)
    return x2, x2_mx, saved


def _layer_bwd(l, dx2, W, sp, sv, hooks=None):
    T = dx2.shape[0]
    n64, ngla, ntok = T // SSD_CHUNK, T // GLA_BLOCK, T // 256
    bf = MXU_DTYPE
    gw, gs = {}, {}
    ln2_p = [_behind(sv["ln2_p"][0], hooks, "start"), sv["ln2_p"][1]]
    (dx1_a, dr2), (gs["ln2_g"], gs["ln2_b"]) = _chain_bwd(
        f"ln2_bwd_{l}", _ln_fn, ntok, [(sv["x1"], 256, D_MODEL, 0), (sv["r2"], 256, D_MODEL, 0)], ln2_p,
        [(dx2, 256, D_MODEL)], dx_dtypes=[F32, bf])
    gw["ffn_w_down"] = _mm(f"ffn_down_dw_{l}", sv["act"], dr2, "tn")
    dact = _mm(f"ffn_down_dx_{l}", dr2, W["ffn_w_down"], "nt")
    dg, du, dwg, dwu, dbg, dbu = _ffn_glu_bwd(f"ffn_glu_bwd_{l}", sv["up"], sp["ffn_conv_w_pad"], sp["ffn_conv_b_pad"], dact, bf)
    gs["ffn_conv_w"] = _ffn_unpad_cols(jnp.concatenate([dwg, dwu], axis=1))
    gs["ffn_conv_b"] = _ffn_unpad_cols(jnp.concatenate([dbg, dbu], axis=1))
    dup = jnp.concatenate([dg, du], axis=1)
    gw["ffn_w_up"] = _mm(f"ffn_up_dw_{l}", dup, sv["x1_mx"], "tn", tn=1024)
    dx1_b = _mm(f"ffn_up_dx_{l}", dup, W["ffn_w_up"], "nn", tn=1024, tk=1024)
    ln1_p = [_behind(sv["ln1_p"][0], hooks, "ffn", dx1_b), sv["ln1_p"][1]]
    (dx_a, dr1), (gs["ln1_g"], gs["ln1_b"]) = _chain_bwd(
        f"ln1_bwd_{l}", _ln_sum_fn, ntok, [(sv["x"], 256, D_MODEL, 0), (sv["r1"], 256, D_MODEL, 0)], ln1_p,
        [(dx1_a, 256, D_MODEL), (dx1_b, 256, D_MODEL)], dx_dtypes=[F32, bf])
    gw["w_out"] = _mm(f"out_proj_dw_{l}", sv["mix"], dr1, "tn")
    dmix = _mm(f"out_proj_dx_{l}", dr1, W["w_out"], "nt")
    (dgates, dy_ssd, dy_gdn, dy_gla), _ = _chain_bwd(f"merge_bwd_{l}", _merge_fn, ntok, sv["merge_in"], [],
                                                     [(dmix, 256, D_MODEL)], dx_dtypes=[bf, bf, bf, bf])
    gw["w_br_ssd"] = _mm(f"br_ssd_dw_{l}", sv["o_ssd"], dy_ssd, "tn")
    gw["w_br_gdn"] = _mm(f"br_gdn_dw_{l}", sv["o_gdn"], dy_gdn, "tn")
    gw["w_br_gla"] = _mm(f"br_gla_dw_{l}", sv["o_gla"], dy_gla, "tn")
    do_ssd = _mm(f"br_ssd_dx_{l}", dy_ssd, W["w_br_ssd"], "nt")
    do_gdn = _mm(f"br_gdn_dx_{l}", dy_gdn, W["w_br_gdn"], "nt")
    do_gla = _mm(f"br_gla_dx_{l}", dy_gla, W["w_br_gla"], "nt")

    ssd_p = [_behind(sv["ssd_p"][0], hooks, "branches", do_gla, gw)] + list(sv["ssd_p"][1:])
    (dxbc, ddt, dz), dps = _chain_bwd(f"ssd_bwd_{l}", _ssd_chunk, n64, sv["ssd_in"], ssd_p,
                                      [(do_ssd, SSD_CHUNK, SSD_INNER)], sprev=sv["ssd_states"], dx_dtypes=[F32, bf, bf])
    gs["ssd_dt_bias"], gs["ssd_a_log"], gs["ssd_d"], gs["ssd_norm_w"] = dps
    gdn_sv = dict(sv["gdn"], scan_p=[_behind(sv["gdn"]["scan_p"][0], hooks, "ssd", dz)])
    dgqkv, dgab, dgg, gs["gdn_a_log"], gs["gdn_dt_bias"], gs["gdn_norm_w"] = _gdn_backward(str(l), do_gdn, gdn_sv, bf)
    (dlqkv, dlglr, dlr), dps = _chain_bwd(f"gla_bwd_{l}", _gla_block, ngla, sv["gla_in"], sv["gla_p"],
                                          [(do_gla, GLA_BLOCK, GLA_V)], sprev=sv["gla_states"], dx_dtypes=[bf, bf, bf])
    gs["gla_gate_w2"], gs["gla_gate_b"], gs["gla_norm_w"] = dps[0][:GLA_RANK], dps[1], dps[2]
    dxbc_pre, gs["ssd_conv_w"], gs["ssd_conv_b"] = _conv_silu_bwd(
        f"ssd_conv_bwd_{l}", sv["h"], SEG["xbc"][0], sp["ssd_conv_w"], sp["ssd_conv_b"], dxbc, bf)
    dgqkv_pre, gs["gdn_conv_w"] = _conv_silu_bwd(f"gdn_conv_bwd_{l}", sv["h"], SEG["gqkv"][0], sp["gdn_conv_w"], None, dgqkv, bf)
    pieces = dict(gates=dgates, xbc=dxbc_pre, gqkv=dgqkv_pre, z=dz, lqkv=dlqkv, gg=dgg, lr=dlr, dt=ddt, gab=dgab, lglr=dlglr)
    cols = [pieces[name] for name, _, _, _ in PAD_SEGS]
    cols.append(jnp.zeros((T, IN_PAD - PAD_SEGS[-1][1] - PAD_SEGS[-1][2]), bf))
    dh = jnp.concatenate(cols, axis=1)
    gw["w_in"] = _mm(f"in_proj_dw_{l}", dh, sv["x_mx"], "tn", tn=1024)
    dx_b = _mm(f"in_proj_dx_{l}", dh, W["w_in"], "nn", tn=1024, tk=1024)
    dx = _add_blocks(f"dx_add_{l}", dx_a[None], dx_b[None])[0]
    return dx, gw, gs


def _ln_sum_fn(xs_, ps_):
    (y,) = _ln_fn(xs_, ps_)
    return (y, y)


def _small_2d(name, a):
    return a.reshape(1, -1) if a.ndim == 1 else a


def kernel(x, w_in, ssd_conv_w, ssd_conv_b, ssd_dt_bias, ssd_a_log, ssd_d, ssd_norm_w, gdn_conv_w, gdn_a_log, gdn_dt_bias, gdn_norm_w, gla_gate_w2, gla_gate_b, gla_norm_w, w_br_ssd, w_br_gdn, w_br_gla, w_out, ln1_g, ln1_b, ffn_w_up, ffn_conv_w, ffn_conv_b, ffn_w_down, ln2_g, ln2_b, loss_target, m_w_in, m_ssd_conv_w, m_ssd_conv_b, m_ssd_dt_bias, m_ssd_a_log, m_ssd_d, m_ssd_norm_w, m_gdn_conv_w, m_gdn_a_log, m_gdn_dt_bias, m_gdn_norm_w, m_gla_gate_w2, m_gla_gate_b, m_gla_norm_w, m_w_br_ssd, m_w_br_gdn, m_w_br_gla, m_w_out, m_ln1_g, m_ln1_b, m_ffn_w_up, m_ffn_conv_w, m_ffn_conv_b, m_ffn_w_down, m_ln2_g, m_ln2_b, v_w_in, v_ssd_conv_w, v_ssd_conv_b, v_ssd_dt_bias, v_ssd_a_log, v_ssd_d, v_ssd_norm_w, v_gdn_conv_w, v_gdn_a_log, v_gdn_dt_bias, v_gdn_norm_w, v_gla_gate_w2, v_gla_gate_b, v_gla_norm_w, v_w_br_ssd, v_w_br_gdn, v_w_br_gla, v_w_out, v_ln1_g, v_ln1_b, v_ffn_w_up, v_ffn_conv_w, v_ffn_conv_b, v_ffn_w_down, v_ln2_g, v_ln2_b):
    args = locals()
    w = {n: args[n] for n in WEIGHTS}
    m = {n: args["m_" + n] for n in WEIGHTS}
    v = {n: args["v_" + n] for n in WEIGHTS}
    dev = 4 * lax.axis_index("x") + 2 * lax.axis_index("y") + lax.axis_index("c")
    xl = x[0]
    tgt = loss_target[0]

    late = BIG[1:]

    def send(names, l):
        return [_shard_to_send(n, w[n][l]) for n in names]

    def whole_weights(names, got):
        return {n: _whole_from_gathered(n, g) for n, g in zip(names, got)}

    got0 = _all_gather("gather_first", send(BIG[:1], 0) + [w[n] for n in SMALL_SHARDED])
    gather0, token0 = _all_gather_begin("w_0", send(late, 0), got0[0])
    gather1, token1 = _all_gather_begin("w_1", send(BIG, 1), got0[0])
    W = [whole_weights(BIG[:1], got0[:1]), None]
    whole = dict(w)
    for n, s in zip(SMALL_SHARDED, got0[1:]):
        whole[n] = jnp.transpose(s, (1, 2, 0, 3)).reshape(s.shape[1], s.shape[2], N_DEV * s.shape[3])
    SP = [{n: _small_2d(n, whole[n][l]) for n in SMALL} for l in range(DEPTH)]
    for sp in SP:
        sp["ffn_conv_w_pad"] = _ffn_pad_cols(sp["ffn_conv_w"])
        sp["ffn_conv_b_pad"] = _ffn_pad_cols(sp["ffn_conv_b"])

    def late_weights_arrive(mixed):
        W[0].update(whole_weights(late, _all_gather_end(gather0, mixed)))

    fwd_hooks = {"projected": lambda h: _all_gather_middle(gather0, h), "mixed": late_weights_arrive,
                 "ffn_up": lambda up: _all_gather_middle(gather1, up)}
    saved = [None] * DEPTH
    x_mx = (xl + (token0[0, 0] + token1[0, 0])).astype(MXU_DTYPE)
    act, act_mx, saved[0] = _layer_fwd(0, xl, x_mx, W[0], SP[0], hooks=fwd_hooks)
    W[1] = whole_weights(BIG, _all_gather_end(gather1, act))
    act, act_mx, saved[1] = _layer_fwd(1, act, act_mx, W[1], SP[1])
    dy, loss_parts = _loss_head(act, tgt)
    loss = lax.psum(jnp.sum(loss_parts), ("x", "y", "c"))

    def slots_of(names, gw):
        return [_slots_from_whole(n, gw[n]) for n in names]

    grads = {}
    GS = [None] * DEPTH
    dy, gw, GS[1] = _layer_bwd(1, dy, W[1], SP[1], saved[1])
    reduce1, reduce1_token = _reduce_scatter_begin("1", slots_of(BIG, gw))
    held = {}

    def late_grads_leave(seen, gw0):
        held["reduce0"], token = _reduce_scatter_begin("0", slots_of(late, gw0))
        return token

    bwd_hooks = {"start": lambda: reduce1_token, "ffn": lambda seen: _reduce_scatter_middle(reduce1, seen),
                 "branches": late_grads_leave, "ssd": lambda seen: _reduce_scatter_middle(held["reduce0"], seen)}
    dy, gw, GS[0] = _layer_bwd(0, dy, W[0], SP[0], saved[0], hooks=bwd_hooks)
    red1 = _reduce_scatter_end(reduce1, dy)
    red0 = _reduce_scatter("first", slots_of(BIG[:1], gw)) + _reduce_scatter_end(held["reduce0"], dy)
    grad_x = dy[None]
    kept_t = ("w_in", "ffn_w_up")
    grads_k = {n: jnp.stack([_shard_from_slot(n, red0[i]), _shard_from_slot(n, red1[i])]) for i, n in enumerate(BIG)}

    small_shapes = [whole[n].shape for n in SMALL]
    gs_flat = _pack([jnp.stack([GS[l][n].reshape(whole[n].shape[1:]) for l in range(DEPTH)]) for n in SMALL], F32)
    (gs_all,) = _all_gather("gather_small_grads", [gs_flat])

    def mine(n, a):
        if n in SMALL_SHARDED:
            cs = a.shape[-1] // N_DEV
            return lax.dynamic_slice_in_dim(a, dev * cs, cs, axis=a.ndim - 1)
        return a

    m_whole, v_whole = {}, {}
    for n in SMALL:
        if n in SMALL_SHARDED:
            cs = w[n].shape[-1]
            zeros = jnp.zeros(whole[n].shape, F32)
            m_whole[n] = lax.dynamic_update_slice_in_dim(zeros, m[n], dev * cs, axis=2)
            v_whole[n] = lax.dynamic_update_slice_in_dim(zeros, v[n], dev * cs, axis=2)
        else:
            m_whole[n], v_whole[n] = m[n], v[n]
    outs = _adamw_small(gs_all, _pack([whole[n] for n in SMALL], F32), _pack([m_whole[n] for n in SMALL], F32),
                        _pack([v_whole[n] for n in SMALL], F32))
    g_s, d_s, m_s, v_s = [_unpack(o, small_shapes) for o in outs]
    delta, new_m, new_v = {}, {}, {}
    for i, n in enumerate(SMALL):
        grads[n], delta[n], new_m[n], new_v[n] = mine(n, g_s[i]), mine(n, d_s[i]), mine(n, m_s[i]), mine(n, v_s[i])
    for n in BIG:
        view = (lambda a: jnp.transpose(a, (0, 2, 1))) if n in kept_t else (lambda a: a)
        outs = _adamw(f"adamw_{n}", view(w[n]), grads_k[n], view(m[n]), view(v[n]))
        grads[n], delta[n], new_m[n], new_v[n] = view(grads_k[n]), view(outs[0]), view(outs[1]), view(outs[2])

    return (loss, grad_x, *[grads[n] for n in WEIGHTS], *[delta[n] for n in WEIGHTS], *[new_m[n] for n in WEIGHTS],
            *[new_v[n] for n in WEIGHTS])
```

```python
import functools
import math

import jax
import jax.numpy as jnp
from jax import lax
from jax.experimental import pallas as pl
from jax.experimental.pallas import tpu as pltpu

F32 = jnp.float32
MXU_DTYPE = jnp.bfloat16
HI = lax.Precision.HIGHEST

N_DEV = 8
D_MODEL = 1024
DEPTH = 2
SSD_HEADS, SSD_HEAD_DIM, SSD_INNER, SSD_GROUPS, SSD_STATE, SSD_CHUNK = 16, 64, 1024, 2, 128, 64
SSD_XBC = SSD_INNER + 2 * SSD_GROUPS * SSD_STATE
GDN_HEADS, GDN_HEAD_DIM, GDN_WIDTH, GDN_CHUNK = 4, 128, 512, 64
GLA_HEADS, GLA_KEY_DIM, GLA_VAL_DIM, GLA_K, GLA_V, GLA_RANK, GLA_CHUNK = 4, 64, 128, 256, 512, 16, 16
GLA_BLOCK = 128
GLA_NORMALIZER = 16.0
FFN_DIM = 2816
FFN_HALF = FFN_DIM // 8
FFN_HALF_PAD = 384
FFN_UP_PAD = 16 * FFN_HALF_PAD
FFN_PAD = FFN_UP_PAD // 2
ALPHA = (2 * DEPTH) ** 0.25
LN_EPS = 1e-5
RMS_EPS = 1e-6
ADAM_LR, ADAM_B1, ADAM_B2, ADAM_EPS, ADAM_WD, ADAM_STEP = 0.001, 0.9, 0.999, 1e-08, 0.01, 10
LANES = 128
NEG_BIG = -1e30
VMEM_LIMIT = 56 * 1024 * 1024

IN_SPLITS = (("z", 1024), ("xbc", 1536), ("dt", 16), ("gqkv", 1536), ("ga", 4), ("gb", 4), ("gg", 512),
             ("lqkv", 1024), ("lglr", 16), ("lr", 512), ("gates", 3072))
IN_DIM = sum(w for _, w in IN_SPLITS)
PAD_SEGS = (("gates", 0, 3072, (("gates", 0),)), ("xbc", 3072, 1536, (("xbc", 0),)),
            ("gqkv", 4608, 1536, (("gqkv", 0),)), ("z", 6144, 1024, (("z", 0),)),
            ("lqkv", 7168, 1024, (("lqkv", 0),)), ("gg", 8192, 512, (("gg", 0),)), ("lr", 8704, 512, (("lr", 0),)),
            ("dt", 9216, 128, (("dt", 0),)), ("gab", 9344, 128, (("ga", 0), ("gb", 4))), ("lglr", 9472, 128, (("lglr", 0),)))
IN_PAD = 9728
SEG = {name: (off, width) for name, off, width, _ in PAD_SEGS}

BIG = ("w_in", "w_br_ssd", "w_br_gdn", "w_br_gla", "w_out", "ffn_w_up", "ffn_w_down")
COL_SHARDED = ("w_in", "w_br_gdn", "w_br_gla", "ffn_w_up")
SMALL_SHARDED = ("ssd_conv_w", "gdn_conv_w", "gla_gate_w2", "ffn_conv_w")
WEIGHTS = ("w_in", "ssd_conv_w", "ssd_conv_b", "ssd_dt_bias", "ssd_a_log", "ssd_d", "ssd_norm_w", "gdn_conv_w",
           "gdn_a_log", "gdn_dt_bias", "gdn_norm_w", "gla_gate_w2", "gla_gate_b", "gla_norm_w", "w_br_ssd", "w_br_gdn",
           "w_br_gla", "w_out", "ln1_g", "ln1_b", "ffn_w_up", "ffn_conv_w", "ffn_conv_b", "ffn_w_down", "ln2_g", "ln2_b")
SMALL = tuple(n for n in WEIGHTS if n not in BIG)
FLAT_W = 512


def _cparams(sem=None):
    kw = dict(vmem_limit_bytes=VMEM_LIMIT)
    if sem is not None:
        kw["dimension_semantics"] = sem
    return pltpu.CompilerParams(**kw)


_DIMS = {"nn": (((1,), (0,)), ((), ())), "nt": (((1,), (1,)), ((), ())), "tn": (((0,), (0,)), ((), ()))}


def _dot(a, b, dims="nn"):
    if MXU_DTYPE == F32:
        return lax.dot_general(a.astype(F32), b.astype(F32), _DIMS[dims], precision=HI, preferred_element_type=F32)
    return lax.dot_general(a.astype(MXU_DTYPE), b.astype(MXU_DTYPE), _DIMS[dims], preferred_element_type=F32)


def _dot_hi(a, b, dims="nn"):
    return lax.dot_general(a.astype(F32), b.astype(F32), _DIMS[dims], precision=HI, preferred_element_type=F32)


def _iota2(shape, axis):
    return lax.broadcasted_iota(jnp.int32, shape, axis)


def _tril(n, strict=False):
    r, c = _iota2((n, n), 0), _iota2((n, n), 1)
    return (r > c) if strict else (r >= c)


def _raw_dot(a, b, dims):
    return lax.dot_general(a, b, _DIMS[dims], preferred_element_type=F32)


def _dot_x3(a, b, dims="nn"):
    if MXU_DTYPE == F32:
        return _dot_hi(a, b, dims)
    ah, bh = a.astype(jnp.bfloat16), b.astype(jnp.bfloat16)
    al, bl = (a - ah.astype(F32)).astype(jnp.bfloat16), (b - bh.astype(F32)).astype(jnp.bfloat16)
    return _raw_dot(ah, bh, dims) + (_raw_dot(ah, bl, dims) + _raw_dot(al, bh, dims))


def _exact_dot(mask, b, dims, mask_first):
    if MXU_DTYPE == F32:
        return _dot_hi(mask, b, dims) if mask_first else _dot_hi(b, mask, dims)
    m = mask.astype(jnp.bfloat16)
    b1 = b.astype(jnp.bfloat16)
    r1 = b - b1.astype(F32)
    b2 = r1.astype(jnp.bfloat16)
    b3 = (r1 - b2.astype(F32)).astype(jnp.bfloat16)
    if mask_first:
        return _raw_dot(m, b1, dims) + (_raw_dot(m, b2, dims) + _raw_dot(m, b3, dims))
    return _raw_dot(b1, m, dims) + (_raw_dot(b2, m, dims) + _raw_dot(b3, m, dims))


@jax.custom_vjp
def _mask_left(mask, b):
    return _exact_dot(mask, b, "nn", True)


_mask_left.defvjp(lambda mask, b: (_mask_left(mask, b), mask),
                  lambda mask, d: (jnp.zeros_like(mask), _exact_dot(mask, d, "tn", True)))


@jax.custom_vjp
def _mask_right(a, mask):
    return _exact_dot(mask, a, "nn", False)


_mask_right.defvjp(lambda a, mask: (_mask_right(a, mask), mask),
                   lambda mask, d: (_exact_dot(mask, d, "nt", False), jnp.zeros_like(mask)))


@jax.custom_vjp
def _unit_lower_inverses(mats):
    n = mats[0].shape[0]
    eye = (_iota2((n, n), 0) == _iota2((n, n), 1)).astype(F32)
    xs = [eye - a for a in mats]
    ps = list(mats)
    k = 2
    while k < n:
        ps = [_dot_x3(p, p) for p in ps]
        xs = [x + _dot_x3(x, p) for x, p in zip(xs, ps)]
        k *= 2
    return xs


def _unit_lower_inverses_fwd(mats):
    ts = _unit_lower_inverses(mats)
    return ts, ts


def _unit_lower_inverses_bwd(ts, dts):
    mids = [_dot_x3(t, d, "tn") for t, d in zip(ts, dts)]
    return ([-_dot_x3(m, t, "nt") for m, t in zip(mids, ts)],)


_unit_lower_inverses.defvjp(_unit_lower_inverses_fwd, _unit_lower_inverses_bwd)


def _ssd_chunk(xs_, ps_, s_t):
    xbc, dtraw, z = xs_
    dt_bias, a_log, d_skip, norm_w = ps_
    L = xbc.shape[0]
    H, P, N, G = SSD_HEADS, SSD_HEAD_DIM, SSD_STATE, SSD_GROUPS
    W = SSD_INNER // G
    xs = xbc[:, :SSD_INNER]
    bm = xbc[:, SSD_INNER:SSD_INNER + G * N]
    cm = xbc[:, SSD_INNER + G * N:]
    dt = jax.nn.softplus(dtraw[:, :H] + dt_bias)
    a = dt * (-jnp.exp(a_log))
    causal = _tril(L)
    a_cs = _mask_left(causal.astype(F32), a)
    expand = (_iota2((H, SSD_INNER), 1) // P == _iota2((H, SSD_INNER), 0)).astype(F32)
    wide = _mask_right(jnp.concatenate([a_cs, dt, jnp.broadcast_to(d_skip, (L, H))], axis=0), expand)
    a_cs_x, dt_x, d_x = wide[:L], wide[L:2 * L], wide[2 * L:]
    a_end_x = a_cs_x[L - 1:L, :]
    a_cs_t, dt_t = a_cs.T, dt.T
    cb = [_dot(cm[:, g * N:(g + 1) * N], bm[:, g * N:(g + 1) * N], "nt") for g in range(G)]
    decay = [jnp.exp(jnp.where(causal, a_cs[:, h:h + 1] - a_cs_t[h:h + 1, :], NEG_BIG)) * dt_t[h:h + 1, :] for h in range(H)]
    ws = [cb[h // (H // G)] * decay[h] for h in range(H)]
    y = jnp.concatenate([_dot(ws[h], xs[:, h * P:(h + 1) * P]) for h in range(H)], axis=1)
    y_in = jnp.concatenate([_dot(cm[:, g * N:(g + 1) * N], s_t[:, g * W:(g + 1) * W]) for g in range(G)], axis=1)
    y = y + y_in * jnp.exp(a_cs_x) + d_x * xs
    xw = xs * (jnp.exp(a_end_x - a_cs_x) * dt_x)
    st = jnp.concatenate([_dot(bm[:, g * N:(g + 1) * N], xw[:, g * W:(g + 1) * W], "tn") for g in range(G)], axis=1)
    s_new = s_t * jnp.exp(a_end_x) + st
    yg = y * jax.nn.silu(z)
    outs = []
    for g in range(G):
        part = yg[:, g * W:(g + 1) * W]
        outs.append(part * lax.rsqrt(jnp.mean(part * part, axis=1, keepdims=True) + RMS_EPS))
    return (jnp.concatenate(outs, axis=1) * norm_w,), s_new


GDN_PREP_CHUNKS = 4


def _gdn_prep(xs_, ps_):
    qkv, ab = xs_
    a_log, dt_bias = ps_
    B = qkv.shape[0]
    H, D, L = GDN_HEADS, GDN_HEAD_DIM, GDN_CHUNK
    g_all = -jnp.exp(a_log) * jax.nn.softplus(ab + dt_bias)
    row, col = _iota2((B, B), 0), _iota2((B, B), 1)
    g_cs = _mask_left((((row // L) == (col // L)) & (row >= col)).astype(F32), g_all)
    g_cs_t = g_cs.T
    beta_all = jax.nn.sigmoid(ab)
    incl, strict = _tril(L), _tril(L, strict=True)
    qs, ks, vs = [], [], []
    for h in range(H):
        q = qkv[:, h * D:(h + 1) * D]
        k = qkv[:, GDN_WIDTH + h * D:GDN_WIDTH + (h + 1) * D]
        qs.append(q * lax.rsqrt(jnp.sum(q * q, axis=1, keepdims=True) + RMS_EPS) * (D ** -0.5))
        ks.append(k * lax.rsqrt(jnp.sum(k * k, axis=1, keepdims=True) + RMS_EPS))
        vs.append(qkv[:, 2 * GDN_WIDTH + h * D:2 * GDN_WIDTH + (h + 1) * D])
    pairs = [(c, h) for c in range(B // L) for h in range(H)]
    rows = {c: slice(c * L, (c + 1) * L) for c in range(B // L)}
    q_ = {(c, h): qs[h][rows[c]] for c, h in pairs}
    k_ = {(c, h): ks[h][rows[c]] for c, h in pairs}
    col_ = {(c, h): g_cs[rows[c], h:h + 1] for c, h in pairs}
    beta_ = {(c, h): beta_all[rows[c], H + h:H + h + 1] for c, h in pairs}
    gamma = {p: jnp.exp(jnp.where(incl, col_[p] - g_cs_t[p[1]:p[1] + 1, rows[p[0]]], NEG_BIG)) for p in pairs}
    kb = {p: k_[p] * beta_[p] for p in pairs}
    a_mat = [jnp.where(strict, _dot(kb[p], k_[p], "nt") * gamma[p], 0.0) for p in pairs]
    attn = {p: jnp.where(incl, _dot(q_[p], k_[p], "nt") * gamma[p], 0.0) for p in pairs}
    t_mat = dict(zip(pairs, _unit_lower_inverses(a_mat)))
    u = {p: _dot(t_mat[p], vs[p[1]][rows[p[0]]] * beta_[p]) for p in pairs}
    w = {p: _dot(t_mat[p], kb[p] * jnp.exp(col_[p])) for p in pairs}
    qd = {p: q_[p] * jnp.exp(col_[p]) for p in pairs}
    kd = {p: k_[p] * jnp.exp(col_[p][L - 1:L, :] - col_[p]) for p in pairs}

    def whole(parts):
        return jnp.concatenate([jnp.concatenate([parts[(c, h)] for h in range(H)], axis=1) for c in range(B // L)], axis=0)

    return (whole(u), whole(w), whole(qd), whole(kd), whole(attn), g_cs)


def _gdn_scan(xs_, ps_, s):
    u, w, qd, kd, attn, g_cs, gate = xs_
    (norm_w,) = ps_
    L = u.shape[0]
    H, D = GDN_HEADS, GDN_HEAD_DIM
    heads = range(H)
    lanes = [slice(h * D, (h + 1) * D) for h in heads]
    s_h = [s[lanes[h], :] for h in heads]
    v_new = [u[:, lanes[h]] - _dot(w[:, lanes[h]], s_h[h]) for h in heads]
    o = [_dot(qd[:, lanes[h]], s_h[h]) + _dot(attn[:, h * L:(h + 1) * L], v_new[h]) for h in heads]
    decay = [jnp.exp(g_cs[L - 1:L, h:h + 1]) for h in heads]
    s_new = [s_h[h] * decay[h] + _dot(kd[:, lanes[h]], v_new[h], "tn") for h in heads]
    o = [o[h] * lax.rsqrt(jnp.mean(o[h] * o[h], axis=1, keepdims=True) + RMS_EPS) * norm_w * jax.nn.silu(gate[:, lanes[h]])
         for h in heads]
    return (jnp.concatenate(o, axis=1),), jnp.concatenate(s_new, axis=0)


def _gdn_forward(tag, gqkv, h, sp):
    T = gqkv.shape[0]
    blk = GDN_PREP_CHUNKS * GDN_CHUNK
    prep_in = [(gqkv, blk, 3 * GDN_WIDTH, 0), _seg_blk(h, "gab", blk)]
    prep_p = [_lane_pad(sp["gdn_a_log"]), _lane_pad(sp["gdn_dt_bias"])]
    mx = MXU_DTYPE
    prep = _chain_fwd(f"gdn_prep_{tag}", _gdn_prep, T // blk, prep_in, prep_p,
                      [(blk, GDN_WIDTH, F32), (blk, GDN_WIDTH, mx), (blk, GDN_WIDTH, mx), (blk, GDN_WIDTH, mx),
                       (blk, GDN_HEADS * GDN_CHUNK, mx), (blk, LANES, F32)])
    widths = [GDN_WIDTH] * 4 + [GDN_HEADS * GDN_CHUNK, LANES]
    scan_in = [(a, GDN_CHUNK, wd, 0) for a, wd in zip(prep, widths)] + [_seg_blk(h, "gg", GDN_CHUNK)]
    scan_p = [sp["gdn_norm_w"]]
    o, states = _chain_fwd(f"gdn_scan_{tag}", _gdn_scan, T // GDN_CHUNK, scan_in, scan_p, [(GDN_CHUNK, GDN_WIDTH, mx)],
                           (GDN_WIDTH, GDN_HEAD_DIM))
    return o, dict(prep_in=prep_in, prep_p=prep_p, scan_in=scan_in, scan_p=scan_p, states=states, widths=widths)


def _gdn_backward(tag, do, sv, dx_dtype):
    T = do.shape[0]
    blk = GDN_PREP_CHUNKS * GDN_CHUNK
    dscan, (dnorm,) = _chain_bwd(f"gdn_scan_bwd_{tag}", _gdn_scan, T // GDN_CHUNK, sv["scan_in"], sv["scan_p"],
                                 [(do, GDN_CHUNK, GDN_WIDTH)], sprev=sv["states"], dx_dtypes=[F32] * 6 + [dx_dtype])
    douts = [(d, blk, wd) for d, wd in zip(dscan[:6], sv["widths"])]
    (dgqkv, dgab), (da_log, ddt_bias) = _chain_bwd(f"gdn_prep_bwd_{tag}", _gdn_prep, T // blk, sv["prep_in"], sv["prep_p"],
                                                   douts, dx_dtypes=[F32, dx_dtype])
    return dgqkv, dgab, dscan[6], da_log[:, :GDN_HEADS], ddt_bias[:, :GDN_HEADS], dnorm


def _gla_block(xs_, ps_, s_t):
    qkv, glr, r = xs_
    w2, gate_b, norm_w = ps_
    B = qkv.shape[0]
    H, K, V, C = GLA_HEADS, GLA_KEY_DIM, GLA_VAL_DIM, GLA_CHUNK
    q = qkv[:, :GLA_K] * (K ** -0.5)
    k = qkv[:, GLA_K:2 * GLA_K]
    v = qkv[:, 2 * GLA_K:]
    gk = jax.nn.log_sigmoid(_dot(glr, w2) + gate_b) / GLA_NORMALIZER
    row, col = _iota2((B, B), 0), _iota2((B, B), 1)
    same = (row // C) == (col // C)
    mask = same & (row >= col)
    b_cs = _mask_left(mask.astype(F32), gk)
    b_end = _mask_left((col == (row // C) * C + (C - 1)).astype(F32), b_cs)
    q_e = q * jnp.exp(b_cs)
    k_e = k * jnp.exp(-b_cs)
    k_d = k * jnp.exp(b_end - b_cs)
    intra = []
    for h in range(H):
        a_mat = jnp.where(mask, _dot(q_e[:, h * K:(h + 1) * K], k_e[:, h * K:(h + 1) * K], "nt"), 0.0)
        intra.append(_dot(a_mat, v[:, h * V:(h + 1) * V]))
    o = jnp.concatenate(intra, axis=1)
    chunks = [slice(j * C, (j + 1) * C) for j in range(B // C)]
    fresh = [jnp.concatenate([_dot(v[sl, h * V:(h + 1) * V], k_d[sl, h * K:(h + 1) * K], "tn") for h in range(H)], axis=1)
             for sl in chunks]
    entering = []
    for j, sl in enumerate(chunks):
        entering.append(s_t)
        s_t = s_t * jnp.exp(b_end[j * C:j * C + 1, :]) + fresh[j]
    inter = [jnp.concatenate([_dot(q_e[sl, h * K:(h + 1) * K], entering[j][:, h * K:(h + 1) * K], "nt") for h in range(H)],
                             axis=1) for j, sl in enumerate(chunks)]
    o = o + jnp.concatenate(inter, axis=0)
    outs = []
    for h in range(H):
        oh = o[:, h * V:(h + 1) * V]
        oh = oh * lax.rsqrt(jnp.mean(oh * oh, axis=1, keepdims=True) + RMS_EPS) * norm_w
        outs.append(oh * jax.nn.silu(r[:, h * V:(h + 1) * V]))
    return (jnp.concatenate(outs, axis=1),), s_t


def _merge_fn(xs_, ps_):
    gates, y_ssd, y_gdn, y_gla = xs_
    d = D_MODEL
    return (jax.nn.sigmoid(gates[:, :d]) * y_ssd + jax.nn.sigmoid(gates[:, d:2 * d]) * y_gdn
            + jax.nn.sigmoid(gates[:, 2 * d:]) * y_gla,)


def _ln_fn(xs_, ps_):
    x, r = xs_
    g, b = ps_
    t = ALPHA * x + r
    mu = jnp.mean(t, axis=1, keepdims=True)
    var = jnp.mean(jnp.square(t - mu), axis=1, keepdims=True)
    return ((t - mu) * lax.rsqrt(var + LN_EPS) * g + b,)


def _row_spec(rows, width, colblk, n, reverse):
    if reverse:
        return pl.BlockSpec((rows, width), lambda c: (n - 1 - c, colblk))
    return pl.BlockSpec((rows, width), lambda c: (c, colblk))


def _full_spec(shape):
    zeros = (0,) * len(shape)
    return pl.BlockSpec(shape, lambda c: zeros)


def _chain_fwd(name, fn, n, blocked, full, out_defs, state_shape=None):
    nb, nf, no = len(blocked), len(full), len(out_defs)

    def body(*refs):
        xs = [r[...].astype(F32) for r in refs[:nb]]
        ps = [r[...] for r in refs[nb:nb + nf]]
        o_refs = refs[nb + nf:nb + nf + no]
        if state_shape is None:
            outs = fn(xs, ps)
        else:
            sprev_ref, s_ref = refs[nb + nf + no:]

            @pl.when(pl.program_id(0) == 0)
            def _():
                s_ref[...] = jnp.zeros_like(s_ref)

            s = s_ref[...]
            sprev_ref[0] = s
            outs, s_new = fn(xs, ps, s)
            s_ref[...] = s_new
        for r, o in zip(o_refs, outs):
            r[...] = o.astype(r.dtype)

    in_specs = [_row_spec(rows, width, cb, n, False) for _, rows, width, cb in blocked]
    in_specs += [_full_spec(a.shape) for a in full]
    out_specs = [_row_spec(rows, width, 0, n, False) for rows, width, _ in out_defs]
    out_shape = [jax.ShapeDtypeStruct((n * rows, width), dt) for rows, width, dt in out_defs]
    scratch = []
    if state_shape is not None:
        out_specs.append(pl.BlockSpec((1,) + state_shape, lambda c: (c, 0, 0)))
        out_shape.append(jax.ShapeDtypeStruct((n,) + state_shape, F32))
        scratch.append(pltpu.VMEM(state_shape, F32))
    return pl.pallas_call(body, name=name, grid=(n,), in_specs=in_specs, out_specs=out_specs, out_shape=out_shape,
                          scratch_shapes=scratch, compiler_params=_cparams(("arbitrary",)))(
        *[a for a, _, _, _ in blocked], *full)


def _chain_bwd(name, fn, n, blocked, full, douts, sprev=None, dx_dtypes=None):
    nb, nf, nd = len(blocked), len(full), len(douts)
    has_state = sprev is not None
    dx_dtypes = dx_dtypes or [F32] * nb

    def body(*refs):
        pos = 0
        b_refs = refs[pos:pos + nb]; pos += nb
        f_refs = refs[pos:pos + nf]; pos += nf
        d_refs = refs[pos:pos + nd]; pos += nd
        if has_state:
            sprev_ref = refs[pos]; pos += 1
        dx_refs = refs[pos:pos + nb]; pos += nb
        dp_refs = refs[pos:pos + nf]; pos += nf
        if has_state:
            ds_ref = refs[pos]

        @pl.when(pl.program_id(0) == 0)
        def _():
            for r in dp_refs:
                r[...] = jnp.zeros_like(r)
            if has_state:
                ds_ref[...] = jnp.zeros_like(ds_ref)

        xs = [r[...].astype(F32) for r in b_refs]
        ps = [r[...] for r in f_refs]
        dys = tuple(r[...].astype(F32) for r in d_refs)
        if has_state:
            _, vjp = jax.vjp(fn, xs, ps, sprev_ref[0])
            dxs, dps, ds = vjp((dys, ds_ref[...]))
            ds_ref[...] = ds
        else:
            _, vjp = jax.vjp(fn, xs, ps)
            dxs, dps = vjp(dys)
        for r, d in zip(dx_refs, dxs):
            r[...] = d.astype(r.dtype)
        for r, d in zip(dp_refs, dps):
            r[...] += d

    in_specs = [_row_spec(rows, width, cb, n, True) for _, rows, width, cb in blocked]
    in_specs += [_full_spec(a.shape) for a in full]
    in_specs += [_row_spec(rows, width, 0, n, True) for _, rows, width in douts]
    args = [a for a, _, _, _ in blocked] + list(full) + [a for a, _, _ in douts]
    scratch = []
    if has_state:
        st_shape = sprev.shape[1:]
        in_specs.append(pl.BlockSpec((1,) + st_shape, lambda c: (n - 1 - c, 0, 0)))
        args.append(sprev)
        scratch.append(pltpu.VMEM(st_shape, F32))
    out_specs = [_row_spec(rows, width, 0, n, True) for _, rows, width, _ in blocked]
    out_specs += [_full_spec(a.shape) for a in full]
    out_shape = [jax.ShapeDtypeStruct((n * rows, width), dt) for (_, rows, width, _), dt in zip(blocked, dx_dtypes)]
    out_shape += [jax.ShapeDtypeStruct(a.shape, F32) for a in full]
    res = pl.pallas_call(body, name=name, grid=(n,), in_specs=in_specs, out_specs=out_specs, out_shape=out_shape,
                         scratch_shapes=scratch, compiler_params=_cparams(("arbitrary",)))(*args)
    return res[:nb], res[nb:]


def _tile(n, target, unit):
    if n <= target:
        return n
    best = None
    for t in range(unit, target + 1, unit):
        if n % t == 0:
            best = t
    assert best is not None, (n, target, unit)
    return best


def _mm(name, a, b, dims="nn", out_dtype=F32, tm=2048, tn=512, tk=2048):
    if dims == "nn":
        (M, K), (_, N) = a.shape, b.shape
    elif dims == "nt":
        (M, K), (N, _) = a.shape, b.shape
    else:
        (K, M), (_, N) = a.shape, b.shape
    tm, tn, tk = _tile(M, tm, LANES), _tile(N, tn, LANES), _tile(K, tk, LANES)
    nk = K // tk

    def body(a_ref, b_ref, o_ref, acc_ref):
        part = _dot(a_ref[...], b_ref[...], dims)
        if nk == 1:
            o_ref[...] = part.astype(o_ref.dtype)
            return

        @pl.when(pl.program_id(2) == 0)
        def _():
            acc_ref[...] = part

        @pl.when(pl.program_id(2) > 0)
        def _():
            acc_ref[...] += part

        @pl.when(pl.program_id(2) == nk - 1)
        def _():
            o_ref[...] = acc_ref[...].astype(o_ref.dtype)

    if dims == "tn":
        a_spec = pl.BlockSpec((tk, tm), lambda j, i, k: (k, i))
    else:
        a_spec = pl.BlockSpec((tm, tk), lambda j, i, k: (i, k))
    if dims == "nt":
        b_spec = pl.BlockSpec((tn, tk), lambda j, i, k: (j, k))
    else:
        b_spec = pl.BlockSpec((tk, tn), lambda j, i, k: (k, j))
    return pl.pallas_call(
        body, name=name, grid=(N // tn, M // tm, nk), in_specs=[a_spec, b_spec],
        out_specs=pl.BlockSpec((tm, tn), lambda j, i, k: (i, j)), out_shape=jax.ShapeDtypeStruct((M, N), out_dtype),
        scratch_shapes=[pltpu.VMEM((tm, tn) if nk > 1 else (8, LANES), F32)],
        compiler_params=_cparams(("parallel", "parallel", "arbitrary")))(a, b)


CONV_CB = 256


def _shift_down(x, k):
    if k == 0:
        return x
    return jnp.where(_iota2(x.shape, 0) >= k, pltpu.roll(x, k, 0), 0.0)


def _shift_up(x, k):
    if k == 0:
        return x
    t = x.shape[0]
    return jnp.where(_iota2(x.shape, 0) < t - k, pltpu.roll(x, t - k, 0), 0.0)


def _conv_pre(x, w, b):
    kk = w.shape[0]
    pre = x * w[kk - 1:kk, :]
    for k in range(kk - 1):
        pre = pre + _shift_down(x, kk - 1 - k) * w[k:k + 1, :]
    return pre if b is None else pre + b


def _conv_bwd_pre(x, w, dpre, dw_ref, db_ref):
    kk = w.shape[0]
    dx = dpre * w[kk - 1:kk, :]
    dw_ref[kk - 1:kk, :] = jnp.sum(dpre * x, axis=0, keepdims=True)
    for k in range(kk - 1):
        dx = dx + _shift_up(dpre, kk - 1 - k) * w[k:k + 1, :]
        dw_ref[k:k + 1, :] = jnp.sum(dpre * _shift_down(x, kk - 1 - k), axis=0, keepdims=True)
    if db_ref is not None:
        db_ref[...] = jnp.sum(dpre, axis=0, keepdims=True)
    return dx


def _dsilu(pre):
    sg = jax.nn.sigmoid(pre)
    return sg * (1.0 + pre * (1.0 - sg))


def _conv_silu_fwd(name, src, col0, w, b):
    T = src.shape[0]
    kk, C = w.shape
    cb = CONV_CB
    off = col0 // cb

    def body(*refs):
        x_ref, w_ref = refs[:2]
        b_val = refs[2][...] if b is not None else None
        refs[-1][...] = jax.nn.silu(_conv_pre(x_ref[...], w_ref[...], b_val))

    in_specs = [pl.BlockSpec((T, cb), lambda j: (0, off + j)), pl.BlockSpec((kk, cb), lambda j: (0, j))]
    args = [src, w]
    if b is not None:
        in_specs.append(pl.BlockSpec((1, cb), lambda j: (0, j)))
        args.append(b)
    return pl.pallas_call(body, name=name, grid=(C // cb,), in_specs=in_specs,
                          out_specs=pl.BlockSpec((T, cb), lambda j: (0, j)), out_shape=jax.ShapeDtypeStruct((T, C), F32),
                          compiler_params=_cparams(("parallel",)))(*args)


def _conv_silu_bwd(name, src, col0, w, b, dy, dx_dtype):
    T = src.shape[0]
    kk, C = w.shape
    cb = CONV_CB
    off = col0 // cb
    has_b = b is not None

    def body(*refs):
        x_ref, w_ref = refs[:2]
        pos = 2
        b_val = None
        if has_b:
            b_val = refs[pos][...]; pos += 1
        dy_ref = refs[pos]; pos += 1
        dx_ref, dw_ref = refs[pos], refs[pos + 1]
        db_ref = refs[pos + 2] if has_b else None
        x, wv = x_ref[...], w_ref[...]
        dpre = dy_ref[...] * _dsilu(_conv_pre(x, wv, b_val))
        dx_ref[...] = _conv_bwd_pre(x, wv, dpre, dw_ref, db_ref).astype(dx_ref.dtype)

    in_specs = [pl.BlockSpec((T, cb), lambda j: (0, off + j)), pl.BlockSpec((kk, cb), lambda j: (0, j))]
    args = [src, w]
    if has_b:
        in_specs.append(pl.BlockSpec((1, cb), lambda j: (0, j)))
        args.append(b)
    in_specs.append(pl.BlockSpec((T, cb), lambda j: (0, j)))
    args.append(dy)
    out_specs = [pl.BlockSpec((T, cb), lambda j: (0, j)), pl.BlockSpec((kk, cb), lambda j: (0, j))]
    out_shape = [jax.ShapeDtypeStruct((T, C), dx_dtype), jax.ShapeDtypeStruct((kk, C), F32)]
    if has_b:
        out_specs.append(pl.BlockSpec((1, cb), lambda j: (0, j)))
        out_shape.append(jax.ShapeDtypeStruct((1, C), F32))
    return pl.pallas_call(body, name=name, grid=(C // cb,), in_specs=in_specs, out_specs=out_specs, out_shape=out_shape,
                          compiler_params=_cparams(("parallel",)))(*args)


def _ffn_glu_fwd(name, up, w, b, out_dtype=F32):
    T = up.shape[0]
    kk = w.shape[0]
    cb = CONV_CB
    width = up.shape[1] // 2
    nblk = width // cb

    def body(g_ref, u_ref, wg_ref, wu_ref, bg_ref, bu_ref, o_ref):
        g = _conv_pre(g_ref[...], wg_ref[...], bg_ref[...])
        u = _conv_pre(u_ref[...], wu_ref[...], bu_ref[...])
        o_ref[...] = (jax.nn.silu(g) * u).astype(o_ref.dtype)

    lo, hi = (lambda j: (0, j)), (lambda j: (0, nblk + j))
    in_specs = [pl.BlockSpec((T, cb), lo), pl.BlockSpec((T, cb), hi), pl.BlockSpec((kk, cb), lo), pl.BlockSpec((kk, cb), hi),
                pl.BlockSpec((1, cb), lo), pl.BlockSpec((1, cb), hi)]
    return pl.pallas_call(body, name=name, grid=(nblk,), in_specs=in_specs, out_specs=pl.BlockSpec((T, cb), lo),
                          out_shape=jax.ShapeDtypeStruct((T, width), out_dtype),
                          compiler_params=_cparams(("parallel",)))(up, up, w, w, b, b)


def _ffn_glu_bwd(name, up, w, b, dact, dx_dtype):
    T = up.shape[0]
    kk = w.shape[0]
    cb = CONV_CB
    width = up.shape[1] // 2
    nblk = width // cb

    def body(g_ref, u_ref, wg_ref, wu_ref, bg_ref, bu_ref, d_ref, dg_ref, du_ref, dwg_ref, dwu_ref, dbg_ref, dbu_ref):
        xg, xu, wg, wu = g_ref[...], u_ref[...], wg_ref[...], wu_ref[...]
        g = _conv_pre(xg, wg, bg_ref[...])
        u = _conv_pre(xu, wu, bu_ref[...])
        d = d_ref[...].astype(F32)
        dg_ref[...] = _conv_bwd_pre(xg, wg, d * u * _dsilu(g), dwg_ref, dbg_ref).astype(dg_ref.dtype)
        du_ref[...] = _conv_bwd_pre(xu, wu, d * jax.nn.silu(g), dwu_ref, dbu_ref).astype(du_ref.dtype)

    lo, hi = (lambda j: (0, j)), (lambda j: (0, nblk + j))
    in_specs = [pl.BlockSpec((T, cb), lo), pl.BlockSpec((T, cb), hi), pl.BlockSpec((kk, cb), lo), pl.BlockSpec((kk, cb), hi),
                pl.BlockSpec((1, cb), lo), pl.BlockSpec((1, cb), hi), pl.BlockSpec((T, cb), lo)]
    out_specs = [pl.BlockSpec((T, cb), lo)] * 2 + [pl.BlockSpec((kk, cb), lo)] * 2 + [pl.BlockSpec((1, cb), lo)] * 2
    out_shape = ([jax.ShapeDtypeStruct((T, width), dx_dtype)] * 2 + [jax.ShapeDtypeStruct((kk, width), F32)] * 2
                 + [jax.ShapeDtypeStruct((1, width), F32)] * 2)
    return pl.pallas_call(body, name=name, grid=(nblk,), in_specs=in_specs, out_specs=out_specs, out_shape=out_shape,
                          compiler_params=_cparams(("parallel",)))(up, up, w, w, b, b, dact)


def _loss_head(y, target):
    T, D = y.shape
    tb = _tile(T, 256, 8)

    def body(y_ref, t_ref, dy_ref, l_ref):
        @pl.when(pl.program_id(0) == 0)
        def _():
            l_ref[...] = jnp.zeros_like(l_ref)

        err = y_ref[...] - t_ref[...]
        dy_ref[...] = err * (1.0 / D)
        l_ref[...] += jnp.sum(err * err, axis=0, keepdims=True) * (0.5 / D)

    spec = pl.BlockSpec((tb, D), lambda i: (i, 0))
    return pl.pallas_call(body, name="loss_head", grid=(T // tb,), in_specs=[spec, spec],
                          out_specs=[spec, pl.BlockSpec((1, D), lambda i: (0, 0))],
                          out_shape=[jax.ShapeDtypeStruct((T, D), F32), jax.ShapeDtypeStruct((1, D), F32)],
                          compiler_params=_cparams(("arbitrary",)))(y, target)


def _adamw_math(w, g, m, v):
    m = ADAM_B1 * m + (1.0 - ADAM_B1) * g
    v = ADAM_B2 * v + (1.0 - ADAM_B2) * jnp.square(g)
    m_hat = m / (1.0 - ADAM_B1 ** ADAM_STEP)
    v_hat = v / (1.0 - ADAM_B2 ** ADAM_STEP)
    return -ADAM_LR * (m_hat / (jnp.sqrt(v_hat) + ADAM_EPS) + ADAM_WD * w), m, v


def _adamw(name, w, g, m, v):
    A, R, C = w.shape
    if C % LANES == 0:
        rb, cb = _slab(R, C)
    else:
        rb, cb = _tile(R, max(8, SLAB_BYTES // 2 // (C * 4) // 8 * 8), 8), C

    def body(w_ref, g_ref, m_ref, v_ref, d_ref, mo_ref, vo_ref):
        d, mn, vn = _adamw_math(w_ref[...], g_ref[...], m_ref[...], v_ref[...])
        d_ref[...] = d
        mo_ref[...] = mn
        vo_ref[...] = vn

    spec = pl.BlockSpec((1, rb, cb), lambda a, r, q: (a, r, q))
    return pl.pallas_call(body, name=name, grid=(A, R // rb, C // cb), in_specs=[spec] * 4, out_specs=[spec] * 3,
                          out_shape=[jax.ShapeDtypeStruct(w.shape, F32)] * 3,
                          compiler_params=_cparams(("parallel", "parallel", "parallel")))(w, g, m, v)


def _adamw_small(parts, w, m, v):
    def body(p_ref, w_ref, m_ref, v_ref, g_ref, d_ref, mo_ref, vo_ref):
        g = p_ref[0]
        for i in range(1, N_DEV):
            g = g + p_ref[i]
        d, mn, vn = _adamw_math(w_ref[...], g, m_ref[...], v_ref[...])
        g_ref[...] = g
        d_ref[...] = d
        mo_ref[...] = mn
        vo_ref[...] = vn

    return pl.pallas_call(body, name="adamw_small", out_shape=[jax.ShapeDtypeStruct(w.shape, F32)] * 4,
                          compiler_params=_cparams())(parts, w, m, v)


def _add_blocks(name, a, b, out_dtype=F32):
    n, R, W = a.shape
    rb = _tile(R, 512, 8)

    def body(a_ref, b_ref, o_ref):
        o_ref[...] = (a_ref[...].astype(F32) + b_ref[...].astype(F32)).astype(o_ref.dtype)

    spec = pl.BlockSpec((1, rb, W), lambda i, r: (i, r, 0))
    return pl.pallas_call(body, name=name, grid=(n, R // rb), in_specs=[spec, spec], out_specs=spec,
                          out_shape=jax.ShapeDtypeStruct(a.shape, out_dtype),
                          compiler_params=_cparams(("parallel", "parallel")))(a, b)


SLAB_BYTES = 1 << 20


def _slab(R, W):
    if R % 16 == 0:
        return _tile(R, max(16, SLAB_BYTES // (4 * W) // 16 * 16), 16), W
    assert W % LANES == 0, (R, W)
    return R, _tile(W, max(LANES, SLAB_BYTES // (4 * R) // LANES * LANES), LANES)


def _pair_add(name, g, other, c, chip):
    _, R, W = g.shape
    rb, cb = _slab(R, W)

    def body(s_ref, a_ref, b_ref, send_ref, own_ref):
        s = a_ref[0] + b_ref[0]
        send_ref[0] = s.astype(send_ref.dtype)

        @pl.when(pl.program_id(2) == s_ref[1])
        def _():
            own_ref[...] = s

    grid_spec = pltpu.PrefetchScalarGridSpec(
        num_scalar_prefetch=1, grid=(R // rb, W // cb, 4),
        in_specs=[pl.BlockSpec((1, rb, cb), lambda r, q, p, s_ref: (2 * p + s_ref[0], r, q)),
                  pl.BlockSpec((1, rb, cb), lambda r, q, p, s_ref: (p, r, q))],
        out_specs=[pl.BlockSpec((1, rb, cb), lambda r, q, p, s_ref: (p, r, q)),
                   pl.BlockSpec((rb, cb), lambda r, q, p, s_ref: (r, q))])
    scalars = jnp.stack([c, chip]).astype(jnp.int32)
    return pl.pallas_call(body, name=name, grid_spec=grid_spec,
                          out_shape=[jax.ShapeDtypeStruct((4, R, W), MXU_DTYPE), jax.ShapeDtypeStruct((R, W), F32)],
                          compiler_params=_cparams(("parallel", "parallel", "arbitrary")))(scalars, g, other)


def _sum4(name, own, parts):
    R, W = own.shape
    rb, cb = _slab(R, W)

    def body(o_ref, p_ref, out_ref):
        out_ref[...] = ((o_ref[...] + p_ref[0].astype(F32)) + p_ref[1].astype(F32)) + p_ref[2].astype(F32)

    return pl.pallas_call(body, name=name, grid=(R // rb, W // cb),
                          in_specs=[pl.BlockSpec((rb, cb), lambda r, q: (r, q)), pl.BlockSpec((3, rb, cb), lambda r, q: (0, r, q))],
                          out_specs=pl.BlockSpec((rb, cb), lambda r, q: (r, q)), out_shape=jax.ShapeDtypeStruct((R, W), F32),
                          compiler_params=_cparams(("parallel", "parallel")))(own, parts)


MESH = pl.DeviceIdType.MESH
ANY = pl.BlockSpec(memory_space=pl.ANY)


def _place():
    return lax.axis_index("x"), lax.axis_index("y"), lax.axis_index("c")


def _other_chips(x, y):
    return [(1 - x, y), (x, 1 - y), (1 - x, 1 - y)]


def _all_gather(name, blocks):
    n = len(blocks)

    def body(*refs):
        x_refs, out_refs = refs[:n], refs[n:2 * n]
        send_sems, recv_sems, local_sems = refs[2 * n:]
        x, y, c = _place()
        me, sibling = (x, y, c), (x, y, 1 - c)
        chips = _other_chips(x, y)

        def slot(a, px, py, pc):
            return out_refs[a].at[4 * px + 2 * py + pc]

        def copy(a, k, blk, to, src=None):
            return pltpu.make_async_remote_copy(src_ref=slot(a, *blk) if src is None else src, dst_ref=slot(a, *blk),
                                                send_sem=send_sems.at[a, k], recv_sem=recv_sems.at[a, k],
                                                device_id=to, device_id_type=MESH)

        mine = [pltpu.make_async_copy(x_refs[a], slot(a, *me), local_sems.at[a]) for a in range(n)]
        for cp in mine:
            cp.start()
        first = []
        for j, chip in enumerate(chips):
            first += [copy(a, 1 + j, me, (*chip, c), src=x_refs[a]) for a in range(n)]
        first += [copy(a, 0, me, sibling, src=x_refs[a]) for a in range(n)]
        for cp in first:
            cp.start()
        passed = []
        for j, chip in enumerate(chips):
            for a in range(n):
                copy(a, 1 + j, (*chip, c), me).wait_recv()
                passed.append(copy(a, 4 + j, (*chip, c), sibling))
                passed[-1].start()
        for a in range(n):
            copy(a, 0, sibling, me).wait_recv()
        for j, chip in enumerate(chips):
            for a in range(n):
                copy(a, 4 + j, (*chip, 1 - c), me).wait_recv()
        for cp in first + passed:
            cp.wait_send()
        for cp in mine:
            cp.wait()

    return pl.pallas_call(body, name=name, in_specs=[ANY] * n, out_specs=[ANY] * n,
                          out_shape=[jax.ShapeDtypeStruct((N_DEV,) + b.shape, b.dtype) for b in blocks],
                          scratch_shapes=[pltpu.SemaphoreType.DMA((n, 7)), pltpu.SemaphoreType.DMA((n, 7)),
                                          pltpu.SemaphoreType.DMA((n,))])(*blocks)


def _routes_to_sibling(x, y, c):
    return [(2 * p + (1 - c), p, (x, y, 1 - c)) for p in range(4)]


def _routes_to_chips(x, y, c):
    return [(2 * px + py, j, (px, py, c)) for j, (px, py) in enumerate(_other_chips(x, y))]


def _routes_block_to_chips(x, y, c):
    me = 4 * x + 2 * y + c
    return [(me, me, (px, py, c)) for px, py in _other_chips(x, y)]


def _routes_blocks_to_sibling(x, y, c):
    return [(4 * px + 2 * py + c, 4 * px + 2 * py + c, (x, y, 1 - c)) for px, py in [(x, y)] + _other_chips(x, y)]


def _route_copies(routes, src_refs, land_refs, send_sems, recv_sems):
    x, y, c = _place()
    copies = []
    for a, (src, land) in enumerate(zip(src_refs, land_refs)):
        plan = routes(x, y, c)
        for k, (s, d, target) in enumerate(plan):
            i = a * len(plan) + k
            copies.append(pltpu.make_async_remote_copy(src_ref=src.at[s], dst_ref=land.at[d], send_sem=send_sems.at[i],
                                                       recv_sem=recv_sems.at[i], device_id=target, device_id_type=MESH))
    return copies


def _exchange(name, routes, n_routes, srcs, land_slots):
    n = len(srcs)

    def body(*refs):
        copies = _route_copies(routes, refs[:n], refs[n:2 * n], refs[2 * n], refs[2 * n + 1])
        for cp in copies:
            cp.start()
        for cp in copies:
            cp.wait_recv()
        for cp in copies:
            cp.wait_send()

    return pl.pallas_call(body, name=name, in_specs=[ANY] * n, out_specs=[ANY] * n,
                          out_shape=[jax.ShapeDtypeStruct((land_slots,) + s.shape[1:], s.dtype) for s in srcs],
                          scratch_shapes=[pltpu.SemaphoreType.DMA((n * n_routes,)), pltpu.SemaphoreType.DMA((n * n_routes,))])(*srcs)


HBM_SPEC = pl.BlockSpec(memory_space=pltpu.HBM)
SEM_SPEC = pl.BlockSpec(memory_space=pltpu.SEMAPHORE)
DATAFLOW = pltpu.SideEffectType.DATAFLOW_SIDE_EFFECTING


def _exchange_start(name, routes, n_routes, srcs, lands, after=None):
    n = len(srcs)
    in_place = lands is None
    bufs = list(srcs) + ([] if in_place else list(lands))
    nb = len(bufs)
    extra = [] if after is None else [after]

    def body(*refs):
        src_refs = refs[:n]
        land_refs = src_refs if in_place else refs[n:nb]
        send_sems, recv_sems = refs[nb + len(extra)], refs[nb + len(extra) + 1]
        token = refs[-1]
        for cp in _route_copies(routes, src_refs, land_refs, send_sems, recv_sems):
            cp.start()
        token[...] = jnp.zeros_like(token)

    sems = [pltpu.SemaphoreType.DMA((n * n_routes,)), pltpu.SemaphoreType.DMA((n * n_routes,))]
    out = pl.pallas_call(
        body, name=name, in_specs=[HBM_SPEC] * nb + [ANY] * len(extra),
        out_shape=sems + [pltpu.HBM(b.shape, b.dtype) for b in bufs] + [jax.ShapeDtypeStruct((8, LANES), F32)],
        out_specs=[SEM_SPEC, SEM_SPEC] + [HBM_SPEC] * nb + [pl.BlockSpec(memory_space=pltpu.VMEM)],
        input_output_aliases={i: 2 + i for i in range(nb)},
        compiler_params=pltpu.CompilerParams(has_side_effects=DATAFLOW))(
        *[pltpu.with_memory_space_constraint(b, pltpu.HBM) for b in bufs], *extra)
    return (out[0], out[1], list(out[2:2 + nb])), out[-1]


def _exchange_wait(name, routes, n_routes, n, started, after):
    send_sems, recv_sems, bufs = started
    nb = len(bufs)
    in_place = nb == n

    def body(*refs):
        src_refs = refs[:n]
        land_refs = src_refs if in_place else refs[n:nb]
        for cp in _route_copies(routes, src_refs, land_refs, refs[nb], refs[nb + 1]):
            cp.wait_send()
            cp.wait_recv()

    out = pl.pallas_call(
        body, name=name, in_specs=[HBM_SPEC] * nb + [SEM_SPEC, SEM_SPEC, ANY],
        out_shape=[pltpu.HBM(b.shape, b.dtype) for b in bufs], out_specs=[HBM_SPEC] * nb,
        input_output_aliases={i: i for i in range(nb)},
        compiler_params=pltpu.CompilerParams(has_side_effects=DATAFLOW))(*bufs, send_sems, recv_sems, after)
    return list(out[:n]) if in_place else list(out[n:])


def _pair_sums(tag, gs, from_sibling):
    x, y, c = _place()
    return [_pair_add(f"rs_add_{tag}_{i}", g, o, c, 2 * x + y) for i, (g, o) in enumerate(zip(gs, from_sibling))]


def _reduce_scatter(tag, gs):
    sums = _pair_sums(tag, gs, _exchange(f"rs_swap_{tag}", _routes_to_sibling, 4, gs, 4))
    got = _exchange(f"rs_chips_{tag}", _routes_to_chips, 3, [s[0] for s in sums], 3)
    return [_sum4(f"rs_sum_{tag}_{i}", s[1], q) for i, (s, q) in enumerate(zip(sums, got))]


def _reduce_scatter_begin(tag, gs):
    lands = [lax.empty((4,) + g.shape[1:], g.dtype) for g in gs]
    swap, token = _exchange_start(f"rs_swap_{tag}_start", _routes_to_sibling, 4, gs, lands)
    return dict(tag=tag, gs=gs, swap=swap), token


def _reduce_scatter_middle(state, after):
    tag, gs = state["tag"], state["gs"]
    from_sibling = _exchange_wait(f"rs_swap_{tag}_wait", _routes_to_sibling, 4, len(gs), state["swap"], after)
    state["sums"] = _pair_sums(tag, gs, from_sibling)
    partials = [s[0] for s in state["sums"]]
    lands = [lax.empty((3,) + p.shape[1:], p.dtype) for p in partials]
    state["chips"], token = _exchange_start(f"rs_chips_{tag}_start", _routes_to_chips, 3, partials, lands)
    return token


def _reduce_scatter_end(state, after):
    tag = state["tag"]
    got = _exchange_wait(f"rs_chips_{tag}_wait", _routes_to_chips, 3, len(state["gs"]), state["chips"], after)
    return [_sum4(f"rs_sum_{tag}_{i}", s[1], q) for i, (s, q) in enumerate(zip(state["sums"], got))]


def _all_gather_begin(tag, blocks, after):
    dev = 4 * lax.axis_index("x") + 2 * lax.axis_index("y") + lax.axis_index("c")
    zones = [lax.dynamic_update_slice_in_dim(lax.empty((N_DEV,) + b.shape, b.dtype), b[None], dev, axis=0) for b in blocks]
    chips, token = _exchange_start(f"gather_{tag}_chips_start", _routes_block_to_chips, 3, zones, None, after)
    return dict(tag=tag, n=len(blocks), chips=chips), token


def _all_gather_middle(state, after):
    tag, n = state["tag"], state["n"]
    zones = _exchange_wait(f"gather_{tag}_chips_wait", _routes_block_to_chips, 3, n, state["chips"], after)
    state["sibling"], token = _exchange_start(f"gather_{tag}_sibling_start", _routes_blocks_to_sibling, 4, zones, None)
    return token


def _all_gather_end(state, after):
    return _exchange_wait(f"gather_{state['tag']}_sibling_wait", _routes_blocks_to_sibling, 4, state["n"], state["sibling"], after)


def _flat_rows(n_elems):
    return -(-n_elems // (FLAT_W * 16)) * 16


def _pack(arrays, dtype):
    flat = jnp.concatenate([a.reshape(-1).astype(dtype) for a in arrays])
    rows = _flat_rows(flat.shape[0])
    flat = jnp.pad(flat, (0, rows * FLAT_W - flat.shape[0]))
    return flat.reshape(rows, FLAT_W)


def _unpack(flat, shapes, lead=()):
    flat = flat.reshape(lead + (-1,))
    out, pos = [], 0
    for s in shapes:
        n = math.prod(s)
        out.append(flat[..., pos:pos + n].reshape(lead + tuple(s)))
        pos += n
    return out


def _ffn_pad_rows(a):
    n = a.shape[0] // FFN_HALF
    a = jnp.pad(a.reshape(n, FFN_HALF, a.shape[1]), ((0, 0), (0, FFN_HALF_PAD - FFN_HALF), (0, 0)))
    return a.reshape(n * FFN_HALF_PAD, a.shape[2])


def _ffn_unpad_rows(a):
    n = a.shape[0] // FFN_HALF_PAD
    return a.reshape(n, FFN_HALF_PAD, a.shape[1])[:, :FFN_HALF].reshape(n * FFN_HALF, a.shape[1])


def _ffn_pad_cols(a):
    n = a.shape[1] // FFN_HALF
    a = jnp.pad(a.reshape(a.shape[0], n, FFN_HALF), ((0, 0), (0, 0), (0, FFN_HALF_PAD - FFN_HALF)))
    return a.reshape(a.shape[0], n * FFN_HALF_PAD)


def _ffn_unpad_cols(a):
    n = a.shape[1] // FFN_HALF_PAD
    return a.reshape(a.shape[0], n, FFN_HALF_PAD)[:, :, :FFN_HALF].reshape(a.shape[0], n * FFN_HALF)


def _shard_to_send(name, shard):
    if name == "w_in":
        shard = shard.T
    elif name == "ffn_w_up":
        shard = _ffn_pad_rows(shard.T)
    return shard.astype(MXU_DTYPE)


def _whole_from_gathered(name, g):
    if name == "w_in":
        return _pad_in_proj_rows(g.reshape(IN_DIM, g.shape[2]))
    if name in ("w_br_gdn", "w_br_gla"):
        return jnp.transpose(g, (1, 0, 2)).reshape(g.shape[1], N_DEV * g.shape[2])
    if name == "ffn_w_down":
        return jnp.pad(g, ((0, 0), (0, FFN_HALF_PAD - FFN_HALF), (0, 0))).reshape(FFN_PAD, g.shape[2])
    return g.reshape(N_DEV * g.shape[1], g.shape[2])


def _slots_from_whole(name, gw):
    if name == "w_in":
        return _unpad_in_proj_rows(gw).reshape(N_DEV, IN_DIM // N_DEV, gw.shape[1])
    if name in ("w_br_gdn", "w_br_gla"):
        return jnp.transpose(gw.reshape(gw.shape[0], N_DEV, gw.shape[1] // N_DEV), (1, 0, 2))
    return gw.reshape(N_DEV, gw.shape[0] // N_DEV, gw.shape[1])


def _shard_from_slot(name, s):
    if name == "ffn_w_up":
        return _ffn_unpad_rows(s)
    if name == "ffn_w_down":
        return s[:FFN_HALF]
    return s


def _in_proj_pieces():
    starts, pos = {}, 0
    for n, width in IN_SPLITS:
        starts[n] = (pos, width)
        pos += width
    return [(starts[ref][0], off + lane, starts[ref][1]) for _, off, _, pieces in PAD_SEGS for ref, lane in pieces]


def _pad_in_proj_rows(w):
    rows, at = [], 0
    for src, dst, n in sorted(_in_proj_pieces(), key=lambda p: p[1]):
        if dst > at:
            rows.append(jnp.zeros((dst - at, w.shape[1]), w.dtype))
        rows.append(w[src:src + n])
        at = dst + n
    rows.append(jnp.zeros((IN_PAD - at, w.shape[1]), w.dtype))
    return jnp.concatenate(rows, axis=0)


def _unpad_in_proj_rows(wp):
    return jnp.concatenate([wp[dst:dst + n] for _, dst, n in sorted(_in_proj_pieces())], axis=0)


def _lane_pad(a, width=LANES):
    return jnp.pad(a, ((0, 0), (0, width - a.shape[1])))


def _seg_blk(h, name, rows):
    off, width = SEG[name]
    return (h, rows, width, off // width)


def _ln_both(xs_, ps_):
    (y,) = _ln_fn(xs_, ps_)
    return (y, y)


def _behind(param, hooks, stage, *seen):
    if hooks is None or stage not in hooks:
        return param
    token = hooks[stage](*seen)
    return param if token is None else param + token[0:1, 0:1]


def _layer_fwd(l, x, x_mx, W, sp, hooks=None):
    T = x.shape[0]
    n64, ngla, ntok = T // SSD_CHUNK, T // GLA_BLOCK, T // 256
    h = _mm(f"in_proj_{l}", x_mx, W["w_in"], "nt")
    xbc = _conv_silu_fwd(f"ssd_conv_{l}", h, SEG["xbc"][0], sp["ssd_conv_w"], sp["ssd_conv_b"])
    gqkv = _conv_silu_fwd(f"gdn_conv_{l}", h, SEG["gqkv"][0], sp["gdn_conv_w"], None)

    ssd_in = [(xbc, SSD_CHUNK, SSD_XBC, 0), _seg_blk(h, "dt", SSD_CHUNK), _seg_blk(h, "z", SSD_CHUNK)]
    ssd_p = [sp["ssd_dt_bias"], sp["ssd_a_log"], sp["ssd_d"], sp["ssd_norm_w"]]
    o_ssd, ssd_states = _chain_fwd(f"ssd_fwd_{l}", _ssd_chunk, n64, ssd_in, ssd_p, [(SSD_CHUNK, SSD_INNER, MXU_DTYPE)],
                                   (SSD_STATE, SSD_INNER))
    o_gdn, gdn_saved = _gdn_forward(str(l), gqkv, h, dict(sp, gdn_a_log=_behind(sp["gdn_a_log"], hooks, "ssd", o_ssd)))
    gla_in = [_seg_blk(h, "lqkv", GLA_BLOCK), _seg_blk(h, "lglr", GLA_BLOCK), _seg_blk(h, "lr", GLA_BLOCK)]
    gla_p = [jnp.pad(sp["gla_gate_w2"], ((0, LANES - GLA_RANK), (0, 0))), sp["gla_gate_b"], sp["gla_norm_w"]]
    o_gla, gla_states = _chain_fwd(f"gla_fwd_{l}", _gla_block, ngla, gla_in, gla_p, [(GLA_BLOCK, GLA_V, MXU_DTYPE)],
                                   (GLA_VAL_DIM, GLA_K))
    ln1_p = [_behind(sp["ln1_g"], hooks, "mixed", o_gla), sp["ln1_b"]]
    y_ssd = _mm(f"br_ssd_{l}", o_ssd, W["w_br_ssd"])
    y_gdn = _mm(f"br_gdn_{l}", o_gdn, W["w_br_gdn"])
    y_gla = _mm(f"br_gla_{l}", o_gla, W["w_br_gla"])
    merge_in = [_seg_blk(h, "gates", 256), (y_ssd, 256, D_MODEL, 0), (y_gdn, 256, D_MODEL, 0), (y_gla, 256, D_MODEL, 0)]
    (mix,) = _chain_fwd(f"merge_{l}", _merge_fn, ntok, merge_in, [], [(256, D_MODEL, MXU_DTYPE)])
    r1 = _mm(f"out_proj_{l}", mix, W["w_out"])
    both = [(256, D_MODEL, F32), (256, D_MODEL, MXU_DTYPE)]
    x1, x1_mx = _chain_fwd(f"ln1_{l}", _ln_both, ntok, [(x, 256, D_MODEL, 0), (r1, 256, D_MODEL, 0)], ln1_p, both)
    up = _mm(f"ffn_up_{l}", x1_mx, W["ffn_w_up"], "nt")
    act = _ffn_glu_fwd(f"ffn_glu_{l}", up, sp["ffn_conv_w_pad"], sp["ffn_conv_b_pad"], MXU_DTYPE)
    ln2_p = [_behind(sp["ln2_g"], hooks, "ffn_act", act), sp["ln2_b"]]
    r2 = _mm(f"ffn_down_{l}", act, W["ffn_w_down"])
    x2, x2_mx = _chain_fwd(f"ln2_{l}", _ln_both, ntok, [(x1, 256, D_MODEL, 0), (r2, 256, D_MODEL, 0)], ln2_p, both)
    saved = dict(x=x, x_mx=x_mx, h=h, xbc=xbc, gqkv=gqkv, ssd_in=ssd_in, ssd_p=ssd_p, ssd_states=ssd_states,
                 gdn=gdn_saved, gla_in=gla_in, gla_p=gla_p, gla_states=gla_states, o_ssd=o_ssd,
                 o_gdn=o_gdn, o_gla=o_gla, merge_in=merge_in, mix=mix, r1=r1, ln1_p=ln1_p, x1=x1, x1_mx=x1_mx, up=up, act=act,
                 r2=r2, ln2_p=ln2_p)
    return x2, x2_mx, saved


def _layer_bwd(l, dx2, W, sp, sv, hooks=None):
    T = dx2.shape[0]
    n64, ngla, ntok = T // SSD_CHUNK, T // GLA_BLOCK, T // 256
    bf = MXU_DTYPE
    gw, gs = {}, {}
    ln2_p = [_behind(sv["ln2_p"][0], hooks, "start"), sv["ln2_p"][1]]
    (dx1_a, dr2), (gs["ln2_g"], gs["ln2_b"]) = _chain_bwd(
        f"ln2_bwd_{l}", _ln_fn, ntok, [(sv["x1"], 256, D_MODEL, 0), (sv["r2"], 256, D_MODEL, 0)], ln2_p,
        [(dx2, 256, D_MODEL)], dx_dtypes=[F32, bf])
    gw["ffn_w_down"] = _mm(f"ffn_down_dw_{l}", sv["act"], dr2, "tn")
    dact = _mm(f"ffn_down_dx_{l}", dr2, W["ffn_w_down"], "nt")
    dg, du, dwg, dwu, dbg, dbu = _ffn_glu_bwd(f"ffn_glu_bwd_{l}", sv["up"], sp["ffn_conv_w_pad"], sp["ffn_conv_b_pad"], dact, bf)
    gs["ffn_conv_w"] = _ffn_unpad_cols(jnp.concatenate([dwg, dwu], axis=1))
    gs["ffn_conv_b"] = _ffn_unpad_cols(jnp.concatenate([dbg, dbu], axis=1))
    dup = jnp.concatenate([dg, du], axis=1)
    gw["ffn_w_up"] = _mm(f"ffn_up_dw_{l}", dup, sv["x1_mx"], "tn", tn=1024)
    dx1_b = _mm(f"ffn_up_dx_{l}", dup, W["ffn_w_up"], "nn", tn=1024, tk=1024)
    ln1_p = [_behind(sv["ln1_p"][0], hooks, "ffn", dx1_b), sv["ln1_p"][1]]
    (dx_a, dr1), (gs["ln1_g"], gs["ln1_b"]) = _chain_bwd(
        f"ln1_bwd_{l}", _ln_sum_fn, ntok, [(sv["x"], 256, D_MODEL, 0), (sv["r1"], 256, D_MODEL, 0)], ln1_p,
        [(dx1_a, 256, D_MODEL), (dx1_b, 256, D_MODEL)], dx_dtypes=[F32, bf])
    gw["w_out"] = _mm(f"out_proj_dw_{l}", sv["mix"], dr1, "tn")
    dmix = _mm(f"out_proj_dx_{l}", dr1, W["w_out"], "nt")
    (dgates, dy_ssd, dy_gdn, dy_gla), _ = _chain_bwd(f"merge_bwd_{l}", _merge_fn, ntok, sv["merge_in"], [],
                                                     [(dmix, 256, D_MODEL)], dx_dtypes=[bf, bf, bf, bf])
    gw["w_br_ssd"] = _mm(f"br_ssd_dw_{l}", sv["o_ssd"], dy_ssd, "tn")
    gw["w_br_gdn"] = _mm(f"br_gdn_dw_{l}", sv["o_gdn"], dy_gdn, "tn")
    gw["w_br_gla"] = _mm(f"br_gla_dw_{l}", sv["o_gla"], dy_gla, "tn")
    do_ssd = _mm(f"br_ssd_dx_{l}", dy_ssd, W["w_br_ssd"], "nt")
    do_gdn = _mm(f"br_gdn_dx_{l}", dy_gdn, W["w_br_gdn"], "nt")
    do_gla = _mm(f"br_gla_dx_{l}", dy_gla, W["w_br_gla"], "nt")

    ssd_p = [_behind(sv["ssd_p"][0], hooks, "branches", do_gla, gw)] + list(sv["ssd_p"][1:])
    (dxbc, ddt, dz), dps = _chain_bwd(f"ssd_bwd_{l}", _ssd_chunk, n64, sv["ssd_in"], ssd_p,
                                      [(do_ssd, SSD_CHUNK, SSD_INNER)], sprev=sv["ssd_states"], dx_dtypes=[F32, bf, bf])
    gs["ssd_dt_bias"], gs["ssd_a_log"], gs["ssd_d"], gs["ssd_norm_w"] = dps
    gdn_sv = dict(sv["gdn"], scan_p=[_behind(sv["gdn"]["scan_p"][0], hooks, "ssd", dz)])
    dgqkv, dgab, dgg, gs["gdn_a_log"], gs["gdn_dt_bias"], gs["gdn_norm_w"] = _gdn_backward(str(l), do_gdn, gdn_sv, bf)
    (dlqkv, dlglr, dlr), dps = _chain_bwd(f"gla_bwd_{l}", _gla_block, ngla, sv["gla_in"], sv["gla_p"],
                                          [(do_gla, GLA_BLOCK, GLA_V)], sprev=sv["gla_states"], dx_dtypes=[bf, bf, bf])
    gs["gla_gate_w2"], gs["gla_gate_b"], gs["gla_norm_w"] = dps[0][:GLA_RANK], dps[1], dps[2]
    dxbc_pre, gs["ssd_conv_w"], gs["ssd_conv_b"] = _conv_silu_bwd(
        f"ssd_conv_bwd_{l}", sv["h"], SEG["xbc"][0], sp["ssd_conv_w"], sp["ssd_conv_b"], dxbc, bf)
    dgqkv_pre, gs["gdn_conv_w"] = _conv_silu_bwd(f"gdn_conv_bwd_{l}", sv["h"], SEG["gqkv"][0], sp["gdn_conv_w"], None, dgqkv, bf)
    pieces = dict(gates=dgates, xbc=dxbc_pre, gqkv=dgqkv_pre, z=dz, lqkv=dlqkv, gg=dgg, lr=dlr, dt=ddt, gab=dgab, lglr=dlglr)
    cols = [pieces[name] for name, _, _, _ in PAD_SEGS]
    cols.append(jnp.zeros((T, IN_PAD - PAD_SEGS[-1][1] - PAD_SEGS[-1][2]), bf))
    dh = jnp.concatenate(cols, axis=1)
    gw["w_in"] = _mm(f"in_proj_dw_{l}", dh, sv["x_mx"], "tn", tn=1024)
    if hooks is not None and "w_in_grad" in hooks:
        hooks["w_in_grad"](gw)
    dx_b = _mm(f"in_proj_dx_{l}", dh, W["w_in"], "nn", tn=1024, tk=1024)
    dx = _add_blocks(f"dx_add_{l}", dx_a[None], dx_b[None])[0]
    return dx, gw, gs


def _ln_sum_fn(xs_, ps_):
    (y,) = _ln_fn(xs_, ps_)
    return (y, y)


def _small_2d(name, a):
    return a.reshape(1, -1) if a.ndim == 1 else a


def kernel(x, w_in, ssd_conv_w, ssd_conv_b, ssd_dt_bias, ssd_a_log, ssd_d, ssd_norm_w, gdn_conv_w, gdn_a_log, gdn_dt_bias, gdn_norm_w, gla_gate_w2, gla_gate_b, gla_norm_w, w_br_ssd, w_br_gdn, w_br_gla, w_out, ln1_g, ln1_b, ffn_w_up, ffn_conv_w, ffn_conv_b, ffn_w_down, ln2_g, ln2_b, loss_target, m_w_in, m_ssd_conv_w, m_ssd_conv_b, m_ssd_dt_bias, m_ssd_a_log, m_ssd_d, m_ssd_norm_w, m_gdn_conv_w, m_gdn_a_log, m_gdn_dt_bias, m_gdn_norm_w, m_gla_gate_w2, m_gla_gate_b, m_gla_norm_w, m_w_br_ssd, m_w_br_gdn, m_w_br_gla, m_w_out, m_ln1_g, m_ln1_b, m_ffn_w_up, m_ffn_conv_w, m_ffn_conv_b, m_ffn_w_down, m_ln2_g, m_ln2_b, v_w_in, v_ssd_conv_w, v_ssd_conv_b, v_ssd_dt_bias, v_ssd_a_log, v_ssd_d, v_ssd_norm_w, v_gdn_conv_w, v_gdn_a_log, v_gdn_dt_bias, v_gdn_norm_w, v_gla_gate_w2, v_gla_gate_b, v_gla_norm_w, v_w_br_ssd, v_w_br_gdn, v_w_br_gla, v_w_out, v_ln1_g, v_ln1_b, v_ffn_w_up, v_ffn_conv_w, v_ffn_conv_b, v_ffn_w_down, v_ln2_g, v_ln2_b):
    args = locals()
    w = {n: args[n] for n in WEIGHTS}
    m = {n: args["m_" + n] for n in WEIGHTS}
    v = {n: args["v_" + n] for n in WEIGHTS}
    dev = 4 * lax.axis_index("x") + 2 * lax.axis_index("y") + lax.axis_index("c")
    xl = x[0]
    tgt = loss_target[0]

    late = BIG[1:]

    def send(names, l):
        return [_shard_to_send(n, w[n][l]) for n in names]

    def whole_weights(names, got):
        return {n: _whole_from_gathered(n, g) for n, g in zip(names, got)}

    got0 = _all_gather("gather_first", send(BIG[:1], 0) + [w[n] for n in SMALL_SHARDED])
    gather0, token0 = _all_gather_begin("w_0", send(late, 0), got0[0])
    W = [whole_weights(BIG[:1], got0[:1]), None]
    whole = dict(w)
    for n, s in zip(SMALL_SHARDED, got0[1:]):
        whole[n] = jnp.transpose(s, (1, 2, 0, 3)).reshape(s.shape[1], s.shape[2], N_DEV * s.shape[3])
    SP = [{n: _small_2d(n, whole[n][l]) for n in SMALL} for l in range(DEPTH)]
    for sp in SP:
        sp["ffn_conv_w_pad"] = _ffn_pad_cols(sp["ffn_conv_w"])
        sp["ffn_conv_b_pad"] = _ffn_pad_cols(sp["ffn_conv_b"])

    held = {}

    def late_weights_cross(o_ssd):
        token = _all_gather_middle(gather0, o_ssd)
        held["gather1"], _ = _all_gather_begin("w_1", send(BIG, 1), o_ssd)
        return token

    def late_weights_arrive(mixed):
        W[0].update(whole_weights(late, _all_gather_end(gather0, mixed)))

    fwd_hooks = {"ssd": late_weights_cross, "mixed": late_weights_arrive,
                 "ffn_act": lambda act: _all_gather_middle(held["gather1"], act)}
    saved = [None] * DEPTH
    act, act_mx, saved[0] = _layer_fwd(0, xl, (xl + token0[0, 0]).astype(MXU_DTYPE), W[0], SP[0], hooks=fwd_hooks)
    W[1] = whole_weights(BIG, _all_gather_end(held["gather1"], act))
    act, act_mx, saved[1] = _layer_fwd(1, act, act_mx, W[1], SP[1])
    dy, loss_parts = _loss_head(act, tgt)
    loss = lax.psum(jnp.sum(loss_parts), ("x", "y", "c"))

    def slots_of(names, gw):
        return [_slots_from_whole(n, gw[n]) for n in names]

    grads = {}
    GS = [None] * DEPTH
    dy, gw, GS[1] = _layer_bwd(1, dy, W[1], SP[1], saved[1])
    reduce1, reduce1_token = _reduce_scatter_begin("1", slots_of(BIG, gw))

    def late_grads_leave(seen, gw0):
        held["reduce0"], token = _reduce_scatter_begin("0", slots_of(late, gw0))
        return token

    def w_in_grad_leaves(gw0):
        held["reduce_first"], _ = _reduce_scatter_begin("first", slots_of(BIG[:1], gw0))

    bwd_hooks = {"start": lambda: reduce1_token, "ffn": lambda seen: _reduce_scatter_middle(reduce1, seen),
                 "branches": late_grads_leave, "ssd": lambda seen: _reduce_scatter_middle(held["reduce0"], seen),
                 "w_in_grad": w_in_grad_leaves}
    dy, gw, GS[0] = _layer_bwd(0, dy, W[0], SP[0], saved[0], hooks=bwd_hooks)
    _reduce_scatter_middle(held["reduce_first"], dy)
    red1 = _reduce_scatter_end(reduce1, dy)
    red0_late = _reduce_scatter_end(held["reduce0"], dy)
    grad_x = dy[None]
    kept_t = ("w_in", "ffn_w_up")
    grads_k = {n: jnp.stack([_shard_from_slot(n, red0_late[i]), _shard_from_slot(n, red1[i + 1])]) for i, n in enumerate(late)}

    small_shapes = [whole[n].shape for n in SMALL]
    gs_flat = _pack([jnp.stack([GS[l][n].reshape(whole[n].shape[1:]) for l in range(DEPTH)]) for n in SMALL], F32)
    (gs_all,) = _all_gather("gather_small_grads", [gs_flat])

    def mine(n, a):
        if n in SMALL_SHARDED:
            cs = a.shape[-1] // N_DEV
            return lax.dynamic_slice_in_dim(a, dev * cs, cs, axis=a.ndim - 1)
        return a

    m_whole, v_whole = {}, {}
    for n in SMALL:
        if n in SMALL_SHARDED:
            cs = w[n].shape[-1]
            zeros = jnp.zeros(whole[n].shape, F32)
            m_whole[n] = lax.dynamic_update_slice_in_dim(zeros, m[n], dev * cs, axis=2)
            v_whole[n] = lax.dynamic_update_slice_in_dim(zeros, v[n], dev * cs, axis=2)
        else:
            m_whole[n], v_whole[n] = m[n], v[n]
    outs = _adamw_small(gs_all, _pack([whole[n] for n in SMALL], F32), _pack([m_whole[n] for n in SMALL], F32),
                        _pack([v_whole[n] for n in SMALL], F32))
    g_s, d_s, m_s, v_s = [_unpack(o, small_shapes) for o in outs]
    delta, new_m, new_v = {}, {}, {}
    for i, n in enumerate(SMALL):
        grads[n], delta[n], new_m[n], new_v[n] = mine(n, g_s[i]), mine(n, d_s[i]), mine(n, m_s[i]), mine(n, v_s[i])
    for n in late + BIG[:1]:
        if n == "w_in":
            (first0,) = _reduce_scatter_end(held["reduce_first"], delta["ffn_w_down"])
            grads_k[n] = jnp.stack([first0, red1[0]])
        view = (lambda a: jnp.transpose(a, (0, 2, 1))) if n in kept_t else (lambda a: a)
        outs = _adamw(f"adamw_{n}", view(w[n]), grads_k[n], view(m[n]), view(v[n]))
        grads[n], delta[n], new_m[n], new_v[n] = view(grads_k[n]), view(outs[0]), view(outs[1]), view(outs[2])

    return (loss, grad_x, *[grads[n] for n in WEIGHTS], *[delta[n] for n in WEIGHTS], *[new_m[n] for n in WEIGHTS],
            *[new_v[n] for n in WEIGHTS])
```

```python
import functools
import math

import jax
import jax.numpy as jnp
from jax import lax
from jax.experimental import pallas as pl
from jax.experimental.pallas import tpu as pltpu

F32 = jnp.float32
MXU_DTYPE = jnp.bfloat16
HI = lax.Precision.HIGHEST

N_DEV = 8
D_MODEL = 1024
DEPTH = 2
SSD_HEADS, SSD_HEAD_DIM, SSD_INNER, SSD_GROUPS, SSD_STATE, SSD_CHUNK = 16, 64, 1024, 2, 128, 64
SSD_XBC = SSD_INNER + 2 * SSD_GROUPS * SSD_STATE
GDN_HEADS, GDN_HEAD_DIM, GDN_WIDTH, GDN_CHUNK = 4, 128, 512, 64
GLA_HEADS, GLA_KEY_DIM, GLA_VAL_DIM, GLA_K, GLA_V, GLA_RANK, GLA_CHUNK = 4, 64, 128, 256, 512, 16, 16
GLA_BLOCK = 128
GLA_NORMALIZER = 16.0
FFN_DIM = 2816
FFN_HALF = FFN_DIM // 8
FFN_HALF_PAD = 384
FFN_UP_PAD = 16 * FFN_HALF_PAD
FFN_PAD = FFN_UP_PAD // 2
ALPHA = (2 * DEPTH) ** 0.25
LN_EPS = 1e-5
RMS_EPS = 1e-6
ADAM_LR, ADAM_B1, ADAM_B2, ADAM_EPS, ADAM_WD, ADAM_STEP = 0.001, 0.9, 0.999, 1e-08, 0.01, 10
LANES = 128
NEG_BIG = -1e30
VMEM_LIMIT = 56 * 1024 * 1024

IN_SPLITS = (("z", 1024), ("xbc", 1536), ("dt", 16), ("gqkv", 1536), ("ga", 4), ("gb", 4), ("gg", 512),
             ("lqkv", 1024), ("lglr", 16), ("lr", 512), ("gates", 3072))
IN_DIM = sum(w for _, w in IN_SPLITS)
PAD_SEGS = (("gates", 0, 3072, (("gates", 0),)), ("xbc", 3072, 1536, (("xbc", 0),)),
            ("gqkv", 4608, 1536, (("gqkv", 0),)), ("z", 6144, 1024, (("z", 0),)),
            ("lqkv", 7168, 1024, (("lqkv", 0),)), ("gg", 8192, 512, (("gg", 0),)), ("lr", 8704, 512, (("lr", 0),)),
            ("dt", 9216, 128, (("dt", 0),)), ("gab", 9344, 128, (("ga", 0), ("gb", 4))), ("lglr", 9472, 128, (("lglr", 0),)))
IN_PAD = 9728
SEG = {name: (off, width) for name, off, width, _ in PAD_SEGS}

BIG = ("w_in", "w_br_ssd", "w_br_gdn", "w_br_gla", "w_out", "ffn_w_up", "ffn_w_down")
COL_SHARDED = ("w_in", "w_br_gdn", "w_br_gla", "ffn_w_up")
SMALL_SHARDED = ("ssd_conv_w", "gdn_conv_w", "gla_gate_w2", "ffn_conv_w")
WEIGHTS = ("w_in", "ssd_conv_w", "ssd_conv_b", "ssd_dt_bias", "ssd_a_log", "ssd_d", "ssd_norm_w", "gdn_conv_w",
           "gdn_a_log", "gdn_dt_bias", "gdn_norm_w", "gla_gate_w2", "gla_gate_b", "gla_norm_w", "w_br_ssd", "w_br_gdn",
           "w_br_gla", "w_out", "ln1_g", "ln1_b", "ffn_w_up", "ffn_conv_w", "ffn_conv_b", "ffn_w_down", "ln2_g", "ln2_b")
SMALL = tuple(n for n in WEIGHTS if n not in BIG)
FLAT_W = 512


def _cparams(sem=None):
    kw = dict(vmem_limit_bytes=VMEM_LIMIT)
    if sem is not None:
        kw["dimension_semantics"] = sem
    return pltpu.CompilerParams(**kw)


_DIMS = {"nn": (((1,), (0,)), ((), ())), "nt": (((1,), (1,)), ((), ())), "tn": (((0,), (0,)), ((), ()))}


def _dot(a, b, dims="nn"):
    if MXU_DTYPE == F32:
        return lax.dot_general(a.astype(F32), b.astype(F32), _DIMS[dims], precision=HI, preferred_element_type=F32)
    return lax.dot_general(a.astype(MXU_DTYPE), b.astype(MXU_DTYPE), _DIMS[dims], preferred_element_type=F32)


def _dot_hi(a, b, dims="nn"):
    return lax.dot_general(a.astype(F32), b.astype(F32), _DIMS[dims], precision=HI, preferred_element_type=F32)


def _iota2(shape, axis):
    return lax.broadcasted_iota(jnp.int32, shape, axis)


def _tril(n, strict=False):
    r, c = _iota2((n, n), 0), _iota2((n, n), 1)
    return (r > c) if strict else (r >= c)


def _raw_dot(a, b, dims):
    return lax.dot_general(a, b, _DIMS[dims], preferred_element_type=F32)


def _dot_x3(a, b, dims="nn"):
    if MXU_DTYPE == F32:
        return _dot_hi(a, b, dims)
    ah, bh = a.astype(jnp.bfloat16), b.astype(jnp.bfloat16)
    al, bl = (a - ah.astype(F32)).astype(jnp.bfloat16), (b - bh.astype(F32)).astype(jnp.bfloat16)
    return _raw_dot(ah, bh, dims) + (_raw_dot(ah, bl, dims) + _raw_dot(al, bh, dims))


def _exact_dot(mask, b, dims, mask_first):
    if MXU_DTYPE == F32:
        return _dot_hi(mask, b, dims) if mask_first else _dot_hi(b, mask, dims)
    m = mask.astype(jnp.bfloat16)
    b1 = b.astype(jnp.bfloat16)
    r1 = b - b1.astype(F32)
    b2 = r1.astype(jnp.bfloat16)
    b3 = (r1 - b2.astype(F32)).astype(jnp.bfloat16)
    if mask_first:
        return _raw_dot(m, b1, dims) + (_raw_dot(m, b2, dims) + _raw_dot(m, b3, dims))
    return _raw_dot(b1, m, dims) + (_raw_dot(b2, m, dims) + _raw_dot(b3, m, dims))


@jax.custom_vjp
def _mask_left(mask, b):
    return _exact_dot(mask, b, "nn", True)


_mask_left.defvjp(lambda mask, b: (_mask_left(mask, b), mask),
                  lambda mask, d: (jnp.zeros_like(mask), _exact_dot(mask, d, "tn", True)))


@jax.custom_vjp
def _mask_right(a, mask):
    return _exact_dot(mask, a, "nn", False)


_mask_right.defvjp(lambda a, mask: (_mask_right(a, mask), mask),
                   lambda mask, d: (_exact_dot(mask, d, "nt", False), jnp.zeros_like(mask)))


@jax.custom_vjp
def _unit_lower_inverses(mats):
    n = mats[0].shape[0]
    eye = (_iota2((n, n), 0) == _iota2((n, n), 1)).astype(F32)
    xs = [eye - a for a in mats]
    ps = list(mats)
    k = 2
    while k < n:
        ps = [_dot_x3(p, p) for p in ps]
        xs = [x + _dot_x3(x, p) for x, p in zip(xs, ps)]
        k *= 2
    return xs


def _unit_lower_inverses_fwd(mats):
    ts = _unit_lower_inverses(mats)
    return ts, ts


def _unit_lower_inverses_bwd(ts, dts):
    mids = [_dot_x3(t, d, "tn") for t, d in zip(ts, dts)]
    return ([-_dot_x3(m, t, "nt") for m, t in zip(mids, ts)],)


_unit_lower_inverses.defvjp(_unit_lower_inverses_fwd, _unit_lower_inverses_bwd)


def _ssd_chunk(xs_, ps_, s_t):
    xbc, dtraw, z = xs_
    dt_bias, a_log, d_skip, norm_w = ps_
    L = xbc.shape[0]
    H, P, N, G = SSD_HEADS, SSD_HEAD_DIM, SSD_STATE, SSD_GROUPS
    W = SSD_INNER // G
    xs = xbc[:, :SSD_INNER]
    bm = xbc[:, SSD_INNER:SSD_INNER + G * N]
    cm = xbc[:, SSD_INNER + G * N:]
    dt = jax.nn.softplus(dtraw[:, :H] + dt_bias)
    a = dt * (-jnp.exp(a_log))
    causal = _tril(L)
    a_cs = _mask_left(causal.astype(F32), a)
    expand = (_iota2((H, SSD_INNER), 1) // P == _iota2((H, SSD_INNER), 0)).astype(F32)
    wide = _mask_right(jnp.concatenate([a_cs, dt, jnp.broadcast_to(d_skip, (L, H))], axis=0), expand)
    a_cs_x, dt_x, d_x = wide[:L], wide[L:2 * L], wide[2 * L:]
    a_end_x = a_cs_x[L - 1:L, :]
    a_cs_t, dt_t = a_cs.T, dt.T
    cb = [_dot(cm[:, g * N:(g + 1) * N], bm[:, g * N:(g + 1) * N], "nt") for g in range(G)]
    decay = [jnp.exp(jnp.where(causal, a_cs[:, h:h + 1] - a_cs_t[h:h + 1, :], NEG_BIG)) * dt_t[h:h + 1, :] for h in range(H)]
    ws = [cb[h // (H // G)] * decay[h] for h in range(H)]
    y = jnp.concatenate([_dot(ws[h], xs[:, h * P:(h + 1) * P]) for h in range(H)], axis=1)
    y_in = jnp.concatenate([_dot(cm[:, g * N:(g + 1) * N], s_t[:, g * W:(g + 1) * W]) for g in range(G)], axis=1)
    y = y + y_in * jnp.exp(a_cs_x) + d_x * xs
    xw = xs * (jnp.exp(a_end_x - a_cs_x) * dt_x)
    st = jnp.concatenate([_dot(bm[:, g * N:(g + 1) * N], xw[:, g * W:(g + 1) * W], "tn") for g in range(G)], axis=1)
    s_new = s_t * jnp.exp(a_end_x) + st
    yg = y * jax.nn.silu(z)
    outs = []
    for g in range(G):
        part = yg[:, g * W:(g + 1) * W]
        outs.append(part * lax.rsqrt(jnp.mean(part * part, axis=1, keepdims=True) + RMS_EPS))
    return (jnp.concatenate(outs, axis=1) * norm_w,), s_new


GDN_PREP_CHUNKS = 4


def _gdn_prep(xs_, ps_):
    qkv, ab = xs_
    a_log, dt_bias = ps_
    B = qkv.shape[0]
    H, D, L = GDN_HEADS, GDN_HEAD_DIM, GDN_CHUNK
    g_all = -jnp.exp(a_log) * jax.nn.softplus(ab + dt_bias)
    row, col = _iota2((B, B), 0), _iota2((B, B), 1)
    g_cs = _mask_left((((row // L) == (col // L)) & (row >= col)).astype(F32), g_all)
    g_cs_t = g_cs.T
    beta_all = jax.nn.sigmoid(ab)
    incl, strict = _tril(L), _tril(L, strict=True)
    qs, ks, vs = [], [], []
    for h in range(H):
        q = qkv[:, h * D:(h + 1) * D]
        k = qkv[:, GDN_WIDTH + h * D:GDN_WIDTH + (h + 1) * D]
        qs.append(q * lax.rsqrt(jnp.sum(q * q, axis=1, keepdims=True) + RMS_EPS) * (D ** -0.5))
        ks.append(k * lax.rsqrt(jnp.sum(k * k, axis=1, keepdims=True) + RMS_EPS))
        vs.append(qkv[:, 2 * GDN_WIDTH + h * D:2 * GDN_WIDTH + (h + 1) * D])
    pairs = [(c, h) for c in range(B // L) for h in range(H)]
    rows = {c: slice(c * L, (c + 1) * L) for c in range(B // L)}
    q_ = {(c, h): qs[h][rows[c]] for c, h in pairs}
    k_ = {(c, h): ks[h][rows[c]] for c, h in pairs}
    col_ = {(c, h): g_cs[rows[c], h:h + 1] for c, h in pairs}
    beta_ = {(c, h): beta_all[rows[c], H + h:H + h + 1] for c, h in pairs}
    gamma = {p: jnp.exp(jnp.where(incl, col_[p] - g_cs_t[p[1]:p[1] + 1, rows[p[0]]], NEG_BIG)) for p in pairs}
    kb = {p: k_[p] * beta_[p] for p in pairs}
    a_mat = [jnp.where(strict, _dot(kb[p], k_[p], "nt") * gamma[p], 0.0) for p in pairs]
    attn = {p: jnp.where(incl, _dot(q_[p], k_[p], "nt") * gamma[p], 0.0) for p in pairs}
    t_mat = dict(zip(pairs, _unit_lower_inverses(a_mat)))
    u = {p: _dot(t_mat[p], vs[p[1]][rows[p[0]]] * beta_[p]) for p in pairs}
    w = {p: _dot(t_mat[p], kb[p] * jnp.exp(col_[p])) for p in pairs}
    qd = {p: q_[p] * jnp.exp(col_[p]) for p in pairs}
    kd = {p: k_[p] * jnp.exp(col_[p][L - 1:L, :] - col_[p]) for p in pairs}

    def whole(parts):
        return jnp.concatenate([jnp.concatenate([parts[(c, h)] for h in range(H)], axis=1) for c in range(B // L)], axis=0)

    return (whole(u), whole(w), whole(qd), whole(kd), whole(attn), g_cs)


def _gdn_scan(xs_, ps_, s):
    u, w, qd, kd, attn, g_cs, gate = xs_
    (norm_w,) = ps_
    L = u.shape[0]
    H, D = GDN_HEADS, GDN_HEAD_DIM
    heads = range(H)
    lanes = [slice(h * D, (h + 1) * D) for h in heads]
    s_h = [s[lanes[h], :] for h in heads]
    v_new = [u[:, lanes[h]] - _dot(w[:, lanes[h]], s_h[h]) for h in heads]
    o = [_dot(qd[:, lanes[h]], s_h[h]) + _dot(attn[:, h * L:(h + 1) * L], v_new[h]) for h in heads]
    decay = [jnp.exp(g_cs[L - 1:L, h:h + 1]) for h in heads]
    s_new = [s_h[h] * decay[h] + _dot(kd[:, lanes[h]], v_new[h], "tn") for h in heads]
    o = [o[h] * lax.rsqrt(jnp.mean(o[h] * o[h], axis=1, keepdims=True) + RMS_EPS) * norm_w * jax.nn.silu(gate[:, lanes[h]])
         for h in heads]
    return (jnp.concatenate(o, axis=1),), jnp.concatenate(s_new, axis=0)


def _gdn_forward(tag, gqkv, h, sp):
    T = gqkv.shape[0]
    blk = GDN_PREP_CHUNKS * GDN_CHUNK
    prep_in = [(gqkv, blk, 3 * GDN_WIDTH, 0), _seg_blk(h, "gab", blk)]
    prep_p = [_lane_pad(sp["gdn_a_log"]), _lane_pad(sp["gdn_dt_bias"])]
    mx = MXU_DTYPE
    prep = _chain_fwd(f"gdn_prep_{tag}", _gdn_prep, T // blk, prep_in, prep_p,
                      [(blk, GDN_WIDTH, F32), (blk, GDN_WIDTH, mx), (blk, GDN_WIDTH, mx), (blk, GDN_WIDTH, mx),
                       (blk, GDN_HEADS * GDN_CHUNK, mx), (blk, LANES, F32)])
    widths = [GDN_WIDTH] * 4 + [GDN_HEADS * GDN_CHUNK, LANES]
    scan_in = [(a, GDN_CHUNK, wd, 0) for a, wd in zip(prep, widths)] + [_seg_blk(h, "gg", GDN_CHUNK)]
    scan_p = [sp["gdn_norm_w"]]
    o, states = _chain_fwd(f"gdn_scan_{tag}", _gdn_scan, T // GDN_CHUNK, scan_in, scan_p, [(GDN_CHUNK, GDN_WIDTH, mx)],
                           (GDN_WIDTH, GDN_HEAD_DIM))
    return o, dict(prep_in=prep_in, prep_p=prep_p, scan_in=scan_in, scan_p=scan_p, states=states, widths=widths)


def _gdn_backward(tag, do, sv, dx_dtype):
    T = do.shape[0]
    blk = GDN_PREP_CHUNKS * GDN_CHUNK
    dscan, (dnorm,) = _chain_bwd(f"gdn_scan_bwd_{tag}", _gdn_scan, T // GDN_CHUNK, sv["scan_in"], sv["scan_p"],
                                 [(do, GDN_CHUNK, GDN_WIDTH)], sprev=sv["states"], dx_dtypes=[F32] * 6 + [dx_dtype])
    douts = [(d, blk, wd) for d, wd in zip(dscan[:6], sv["widths"])]
    (dgqkv, dgab), (da_log, ddt_bias) = _chain_bwd(f"gdn_prep_bwd_{tag}", _gdn_prep, T // blk, sv["prep_in"], sv["prep_p"],
                                                   douts, dx_dtypes=[F32, dx_dtype])
    return dgqkv, dgab, dscan[6], da_log[:, :GDN_HEADS], ddt_bias[:, :GDN_HEADS], dnorm


def _gla_block(xs_, ps_, s_t):
    qkv, glr, r = xs_
    w2, gate_b, norm_w = ps_
    B = qkv.shape[0]
    H, K, V, C = GLA_HEADS, GLA_KEY_DIM, GLA_VAL_DIM, GLA_CHUNK
    q = qkv[:, :GLA_K] * (K ** -0.5)
    k = qkv[:, GLA_K:2 * GLA_K]
    v = qkv[:, 2 * GLA_K:]
    gk = jax.nn.log_sigmoid(_dot(glr, w2) + gate_b) / GLA_NORMALIZER
    row, col = _iota2((B, B), 0), _iota2((B, B), 1)
    same = (row // C) == (col // C)
    mask = same & (row >= col)
    b_cs = _mask_left(mask.astype(F32), gk)
    b_end = _mask_left((col == (row // C) * C + (C - 1)).astype(F32), b_cs)
    q_e = q * jnp.exp(b_cs)
    k_e = k * jnp.exp(-b_cs)
    k_d = k * jnp.exp(b_end - b_cs)
    intra = []
    for h in range(H):
        a_mat = jnp.where(mask, _dot(q_e[:, h * K:(h + 1) * K], k_e[:, h * K:(h + 1) * K], "nt"), 0.0)
        intra.append(_dot(a_mat, v[:, h * V:(h + 1) * V]))
    o = jnp.concatenate(intra, axis=1)
    chunks = [slice(j * C, (j + 1) * C) for j in range(B // C)]
    fresh = [jnp.concatenate([_dot(v[sl, h * V:(h + 1) * V], k_d[sl, h * K:(h + 1) * K], "tn") for h in range(H)], axis=1)
             for sl in chunks]
    entering = []
    for j, sl in enumerate(chunks):
        entering.append(s_t)
        s_t = s_t * jnp.exp(b_end[j * C:j * C + 1, :]) + fresh[j]
    inter = [jnp.concatenate([_dot(q_e[sl, h * K:(h + 1) * K], entering[j][:, h * K:(h + 1) * K], "nt") for h in range(H)],
                             axis=1) for j, sl in enumerate(chunks)]
    o = o + jnp.concatenate(inter, axis=0)
    outs = []
    for h in range(H):
        oh = o[:, h * V:(h + 1) * V]
        oh = oh * lax.rsqrt(jnp.mean(oh * oh, axis=1, keepdims=True) + RMS_EPS) * norm_w
        outs.append(oh * jax.nn.silu(r[:, h * V:(h + 1) * V]))
    return (jnp.concatenate(outs, axis=1),), s_t


def _merge_fn(xs_, ps_):
    gates, y_ssd, y_gdn, y_gla = xs_
    d = D_MODEL
    return (jax.nn.sigmoid(gates[:, :d]) * y_ssd + jax.nn.sigmoid(gates[:, d:2 * d]) * y_gdn
            + jax.nn.sigmoid(gates[:, 2 * d:]) * y_gla,)


def _ln_fn(xs_, ps_):
    x, r = xs_
    g, b = ps_
    t = ALPHA * x + r
    mu = jnp.mean(t, axis=1, keepdims=True)
    var = jnp.mean(jnp.square(t - mu), axis=1, keepdims=True)
    return ((t - mu) * lax.rsqrt(var + LN_EPS) * g + b,)


def _row_spec(rows, width, colblk, n, reverse):
    if reverse:
        return pl.BlockSpec((rows, width), lambda c: (n - 1 - c, colblk))
    return pl.BlockSpec((rows, width), lambda c: (c, colblk))


def _full_spec(shape):
    zeros = (0,) * len(shape)
    return pl.BlockSpec(shape, lambda c: zeros)


def _chain_fwd(name, fn, n, blocked, full, out_defs, state_shape=None):
    nb, nf, no = len(blocked), len(full), len(out_defs)

    def body(*refs):
        xs = [r[...].astype(F32) for r in refs[:nb]]
        ps = [r[...] for r in refs[nb:nb + nf]]
        o_refs = refs[nb + nf:nb + nf + no]
        if state_shape is None:
            outs = fn(xs, ps)
        else:
            sprev_ref, s_ref = refs[nb + nf + no:]

            @pl.when(pl.program_id(0) == 0)
            def _():
                s_ref[...] = jnp.zeros_like(s_ref)

            s = s_ref[...]
            sprev_ref[0] = s
            outs, s_new = fn(xs, ps, s)
            s_ref[...] = s_new
        for r, o in zip(o_refs, outs):
            r[...] = o.astype(r.dtype)

    in_specs = [_row_spec(rows, width, cb, n, False) for _, rows, width, cb in blocked]
    in_specs += [_full_spec(a.shape) for a in full]
    out_specs = [_row_spec(rows, width, 0, n, False) for rows, width, _ in out_defs]
    out_shape = [jax.ShapeDtypeStruct((n * rows, width), dt) for rows, width, dt in out_defs]
    scratch = []
    if state_shape is not None:
        out_specs.append(pl.BlockSpec((1,) + state_shape, lambda c: (c, 0, 0)))
        out_shape.append(jax.ShapeDtypeStruct((n,) + state_shape, F32))
        scratch.append(pltpu.VMEM(state_shape, F32))
    return pl.pallas_call(body, name=name, grid=(n,), in_specs=in_specs, out_specs=out_specs, out_shape=out_shape,
                          scratch_shapes=scratch, compiler_params=_cparams(("arbitrary",)))(
        *[a for a, _, _, _ in blocked], *full)


def _chain_bwd(name, fn, n, blocked, full, douts, sprev=None, dx_dtypes=None):
    nb, nf, nd = len(blocked), len(full), len(douts)
    has_state = sprev is not None
    dx_dtypes = dx_dtypes or [F32] * nb

    def body(*refs):
        pos = 0
        b_refs = refs[pos:pos + nb]; pos += nb
        f_refs = refs[pos:pos + nf]; pos += nf
        d_refs = refs[pos:pos + nd]; pos += nd
        if has_state:
            sprev_ref = refs[pos]; pos += 1
        dx_refs = refs[pos:pos + nb]; pos += nb
        dp_refs = refs[pos:pos + nf]; pos += nf
        if has_state:
            ds_ref = refs[pos]

        @pl.when(pl.program_id(0) == 0)
        def _():
            for r in dp_refs:
                r[...] = jnp.zeros_like(r)
            if has_state:
                ds_ref[...] = jnp.zeros_like(ds_ref)

        xs = [r[...].astype(F32) for r in b_refs]
        ps = [r[...] for r in f_refs]
        dys = tuple(r[...].astype(F32) for r in d_refs)
        if has_state:
            _, vjp = jax.vjp(fn, xs, ps, sprev_ref[0])
            dxs, dps, ds = vjp((dys, ds_ref[...]))
            ds_ref[...] = ds
        else:
            _, vjp = jax.vjp(fn, xs, ps)
            dxs, dps = vjp(dys)
        for r, d in zip(dx_refs, dxs):
            r[...] = d.astype(r.dtype)
        for r, d in zip(dp_refs, dps):
            r[...] += d

    in_specs = [_row_spec(rows, width, cb, n, True) for _, rows, width, cb in blocked]
    in_specs += [_full_spec(a.shape) for a in full]
    in_specs += [_row_spec(rows, width, 0, n, True) for _, rows, width in douts]
    args = [a for a, _, _, _ in blocked] + list(full) + [a for a, _, _ in douts]
    scratch = []
    if has_state:
        st_shape = sprev.shape[1:]
        in_specs.append(pl.BlockSpec((1,) + st_shape, lambda c: (n - 1 - c, 0, 0)))
        args.append(sprev)
        scratch.append(pltpu.VMEM(st_shape, F32))
    out_specs = [_row_spec(rows, width, 0, n, True) for _, rows, width, _ in blocked]
    out_specs += [_full_spec(a.shape) for a in full]
    out_shape = [jax.ShapeDtypeStruct((n * rows, width), dt) for (_, rows, width, _), dt in zip(blocked, dx_dtypes)]
    out_shape += [jax.ShapeDtypeStruct(a.shape, F32) for a in full]
    res = pl.pallas_call(body, name=name, grid=(n,), in_specs=in_specs, out_specs=out_specs, out_shape=out_shape,
                         scratch_shapes=scratch, compiler_params=_cparams(("arbitrary",)))(*args)
    return res[:nb], res[nb:]


def _tile(n, target, unit):
    if n <= target:
        return n
    best = None
    for t in range(unit, target + 1, unit):
        if n % t == 0:
            best = t
    assert best is not None, (n, target, unit)
    return best


def _mm(name, a, b, dims="nn", out_dtype=F32, tm=2048, tn=512, tk=2048):
    if dims == "nn":
        (M, K), (_, N) = a.shape, b.shape
    elif dims == "nt":
        (M, K), (N, _) = a.shape, b.shape
    else:
        (K, M), (_, N) = a.shape, b.shape
    tm, tn, tk = _tile(M, tm, LANES), _tile(N, tn, LANES), _tile(K, tk, LANES)
    nk = K // tk

    def body(a_ref, b_ref, o_ref, acc_ref):
        part = _dot(a_ref[...], b_ref[...], dims)
        if nk == 1:
            o_ref[...] = part.astype(o_ref.dtype)
            return

        @pl.when(pl.program_id(2) == 0)
        def _():
            acc_ref[...] = part

        @pl.when(pl.program_id(2) > 0)
        def _():
            acc_ref[...] += part

        @pl.when(pl.program_id(2) == nk - 1)
        def _():
            o_ref[...] = acc_ref[...].astype(o_ref.dtype)

    if dims == "tn":
        a_spec = pl.BlockSpec((tk, tm), lambda j, i, k: (k, i))
    else:
        a_spec = pl.BlockSpec((tm, tk), lambda j, i, k: (i, k))
    if dims == "nt":
        b_spec = pl.BlockSpec((tn, tk), lambda j, i, k: (j, k))
    else:
        b_spec = pl.BlockSpec((tk, tn), lambda j, i, k: (k, j))
    return pl.pallas_call(
        body, name=name, grid=(N // tn, M // tm, nk), in_specs=[a_spec, b_spec],
        out_specs=pl.BlockSpec((tm, tn), lambda j, i, k: (i, j)), out_shape=jax.ShapeDtypeStruct((M, N), out_dtype),
        scratch_shapes=[pltpu.VMEM((tm, tn) if nk > 1 else (8, LANES), F32)],
        compiler_params=_cparams(("parallel", "parallel", "arbitrary")))(a, b)


CONV_CB = 256


def _shift_down(x, k):
    if k == 0:
        return x
    return jnp.where(_iota2(x.shape, 0) >= k, pltpu.roll(x, k, 0), 0.0)


def _shift_up(x, k):
    if k == 0:
        return x
    t = x.shape[0]
    return jnp.where(_iota2(x.shape, 0) < t - k, pltpu.roll(x, t - k, 0), 0.0)


def _conv_pre(x, w, b):
    kk = w.shape[0]
    pre = x * w[kk - 1:kk, :]
    for k in range(kk - 1):
        pre = pre + _shift_down(x, kk - 1 - k) * w[k:k + 1, :]
    return pre if b is None else pre + b


def _conv_bwd_pre(x, w, dpre, dw_ref, db_ref):
    kk = w.shape[0]
    dx = dpre * w[kk - 1:kk, :]
    dw_ref[kk - 1:kk, :] = jnp.sum(dpre * x, axis=0, keepdims=True)
    for k in range(kk - 1):
        dx = dx + _shift_up(dpre, kk - 1 - k) * w[k:k + 1, :]
        dw_ref[k:k + 1, :] = jnp.sum(dpre * _shift_down(x, kk - 1 - k), axis=0, keepdims=True)
    if db_ref is not None:
        db_ref[...] = jnp.sum(dpre, axis=0, keepdims=True)
    return dx


def _dsilu(pre):
    sg = jax.nn.sigmoid(pre)
    return sg * (1.0 + pre * (1.0 - sg))


def _conv_silu_fwd(name, src, col0, w, b):
    T = src.shape[0]
    kk, C = w.shape
    cb = CONV_CB
    off = col0 // cb

    def body(*refs):
        x_ref, w_ref = refs[:2]
        b_val = refs[2][...] if b is not None else None
        refs[-1][...] = jax.nn.silu(_conv_pre(x_ref[...], w_ref[...], b_val))

    in_specs = [pl.BlockSpec((T, cb), lambda j: (0, off + j)), pl.BlockSpec((kk, cb), lambda j: (0, j))]
    args = [src, w]
    if b is not None:
        in_specs.append(pl.BlockSpec((1, cb), lambda j: (0, j)))
        args.append(b)
    return pl.pallas_call(body, name=name, grid=(C // cb,), in_specs=in_specs,
                          out_specs=pl.BlockSpec((T, cb), lambda j: (0, j)), out_shape=jax.ShapeDtypeStruct((T, C), F32),
                          compiler_params=_cparams(("parallel",)))(*args)


def _conv_silu_bwd(name, src, col0, w, b, dy, dx_dtype):
    T = src.shape[0]
    kk, C = w.shape
    cb = CONV_CB
    off = col0 // cb
    has_b = b is not None

    def body(*refs):
        x_ref, w_ref = refs[:2]
        pos = 2
        b_val = None
        if has_b:
            b_val = refs[pos][...]; pos += 1
        dy_ref = refs[pos]; pos += 1
        dx_ref, dw_ref = refs[pos], refs[pos + 1]
        db_ref = refs[pos + 2] if has_b else None
        x, wv = x_ref[...], w_ref[...]
        dpre = dy_ref[...] * _dsilu(_conv_pre(x, wv, b_val))
        dx_ref[...] = _conv_bwd_pre(x, wv, dpre, dw_ref, db_ref).astype(dx_ref.dtype)

    in_specs = [pl.BlockSpec((T, cb), lambda j: (0, off + j)), pl.BlockSpec((kk, cb), lambda j: (0, j))]
    args = [src, w]
    if has_b:
        in_specs.append(pl.BlockSpec((1, cb), lambda j: (0, j)))
        args.append(b)
    in_specs.append(pl.BlockSpec((T, cb), lambda j: (0, j)))
    args.append(dy)
    out_specs = [pl.BlockSpec((T, cb), lambda j: (0, j)), pl.BlockSpec((kk, cb), lambda j: (0, j))]
    out_shape = [jax.ShapeDtypeStruct((T, C), dx_dtype), jax.ShapeDtypeStruct((kk, C), F32)]
    if has_b:
        out_specs.append(pl.BlockSpec((1, cb), lambda j: (0, j)))
        out_shape.append(jax.ShapeDtypeStruct((1, C), F32))
    return pl.pallas_call(body, name=name, grid=(C // cb,), in_specs=in_specs, out_specs=out_specs, out_shape=out_shape,
                          compiler_params=_cparams(("parallel",)))(*args)


def _ffn_glu_fwd(name, up, w, b, out_dtype=F32):
    T = up.shape[0]
    kk = w.shape[0]
    cb = CONV_CB
    width = up.shape[1] // 2
    nblk = width // cb

    def body(g_ref, u_ref, wg_ref, wu_ref, bg_ref, bu_ref, o_ref):
        g = _conv_pre(g_ref[...], wg_ref[...], bg_ref[...])
        u = _conv_pre(u_ref[...], wu_ref[...], bu_ref[...])
        o_ref[...] = (jax.nn.silu(g) * u).astype(o_ref.dtype)

    lo, hi = (lambda j: (0, j)), (lambda j: (0, nblk + j))
    in_specs = [pl.BlockSpec((T, cb), lo), pl.BlockSpec((T, cb), hi), pl.BlockSpec((kk, cb), lo), pl.BlockSpec((kk, cb), hi),
                pl.BlockSpec((1, cb), lo), pl.BlockSpec((1, cb), hi)]
    return pl.pallas_call(body, name=name, grid=(nblk,), in_specs=in_specs, out_specs=pl.BlockSpec((T, cb), lo),
                          out_shape=jax.ShapeDtypeStruct((T, width), out_dtype),
                          compiler_params=_cparams(("parallel",)))(up, up, w, w, b, b)


def _ffn_glu_bwd(name, up, w, b, dact, dx_dtype):
    T = up.shape[0]
    kk = w.shape[0]
    cb = CONV_CB
    width = up.shape[1] // 2
    nblk = width // cb

    def body(g_ref, u_ref, wg_ref, wu_ref, bg_ref, bu_ref, d_ref, dg_ref, du_ref, dwg_ref, dwu_ref, dbg_ref, dbu_ref):
        xg, xu, wg, wu = g_ref[...], u_ref[...], wg_ref[...], wu_ref[...]
        g = _conv_pre(xg, wg, bg_ref[...])
        u = _conv_pre(xu, wu, bu_ref[...])
        d = d_ref[...].astype(F32)
        dg_ref[...] = _conv_bwd_pre(xg, wg, d * u * _dsilu(g), dwg_ref, dbg_ref).astype(dg_ref.dtype)
        du_ref[...] = _conv_bwd_pre(xu, wu, d * jax.nn.silu(g), dwu_ref, dbu_ref).astype(du_ref.dtype)

    lo, hi = (lambda j: (0, j)), (lambda j: (0, nblk + j))
    in_specs = [pl.BlockSpec((T, cb), lo), pl.BlockSpec((T, cb), hi), pl.BlockSpec((kk, cb), lo), pl.BlockSpec((kk, cb), hi),
                pl.BlockSpec((1, cb), lo), pl.BlockSpec((1, cb), hi), pl.BlockSpec((T, cb), lo)]
    out_specs = [pl.BlockSpec((T, cb), lo)] * 2 + [pl.BlockSpec((kk, cb), lo)] * 2 + [pl.BlockSpec((1, cb), lo)] * 2
    out_shape = ([jax.ShapeDtypeStruct((T, width), dx_dtype)] * 2 + [jax.ShapeDtypeStruct((kk, width), F32)] * 2
                 + [jax.ShapeDtypeStruct((1, width), F32)] * 2)
    return pl.pallas_call(body, name=name, grid=(nblk,), in_specs=in_specs, out_specs=out_specs, out_shape=out_shape,
                          compiler_params=_cparams(("parallel",)))(up, up, w, w, b, b, dact)


def _loss_head(y, target):
    T, D = y.shape
    tb = _tile(T, 256, 8)

    def body(y_ref, t_ref, dy_ref, l_ref):
        @pl.when(pl.program_id(0) == 0)
        def _():
            l_ref[...] = jnp.zeros_like(l_ref)

        err = y_ref[...] - t_ref[...]
        dy_ref[...] = err * (1.0 / D)
        l_ref[...] += jnp.sum(err * err, axis=0, keepdims=True) * (0.5 / D)

    spec = pl.BlockSpec((tb, D), lambda i: (i, 0))
    return pl.pallas_call(body, name="loss_head", grid=(T // tb,), in_specs=[spec, spec],
                          out_specs=[spec, pl.BlockSpec((1, D), lambda i: (0, 0))],
                          out_shape=[jax.ShapeDtypeStruct((T, D), F32), jax.ShapeDtypeStruct((1, D), F32)],
                          compiler_params=_cparams(("arbitrary",)))(y, target)


def _adamw_math(w, g, m, v):
    m = ADAM_B1 * m + (1.0 - ADAM_B1) * g
    v = ADAM_B2 * v + (1.0 - ADAM_B2) * jnp.square(g)
    m_hat = m / (1.0 - ADAM_B1 ** ADAM_STEP)
    v_hat = v / (1.0 - ADAM_B2 ** ADAM_STEP)
    return -ADAM_LR * (m_hat / (jnp.sqrt(v_hat) + ADAM_EPS) + ADAM_WD * w), m, v


def _adamw(name, w, g, m, v):
    A, R, C = w.shape
    if C % LANES == 0:
        rb, cb = _slab(R, C)
    else:
        rb, cb = _tile(R, max(8, SLAB_BYTES // 2 // (C * 4) // 8 * 8), 8), C

    def body(w_ref, g_ref, m_ref, v_ref, d_ref, mo_ref, vo_ref):
        d, mn, vn = _adamw_math(w_ref[...], g_ref[...], m_ref[...], v_ref[...])
        d_ref[...] = d
        mo_ref[...] = mn
        vo_ref[...] = vn

    spec = pl.BlockSpec((1, rb, cb), lambda a, r, q: (a, r, q))
    return pl.pallas_call(body, name=name, grid=(A, R // rb, C // cb), in_specs=[spec] * 4, out_specs=[spec] * 3,
                          out_shape=[jax.ShapeDtypeStruct(w.shape, F32)] * 3,
                          compiler_params=_cparams(("parallel", "parallel", "parallel")))(w, g, m, v)


def _adamw_small(parts, w, m, v):
    def body(p_ref, w_ref, m_ref, v_ref, g_ref, d_ref, mo_ref, vo_ref):
        g = p_ref[0]
        for i in range(1, N_DEV):
            g = g + p_ref[i]
        d, mn, vn = _adamw_math(w_ref[...], g, m_ref[...], v_ref[...])
        g_ref[...] = g
        d_ref[...] = d
        mo_ref[...] = mn
        vo_ref[...] = vn

    return pl.pallas_call(body, name="adamw_small", out_shape=[jax.ShapeDtypeStruct(w.shape, F32)] * 4,
                          compiler_params=_cparams())(parts, w, m, v)


def _add_blocks(name, a, b, out_dtype=F32):
    n, R, W = a.shape
    rb = _tile(R, 512, 8)

    def body(a_ref, b_ref, o_ref):
        o_ref[...] = (a_ref[...].astype(F32) + b_ref[...].astype(F32)).astype(o_ref.dtype)

    spec = pl.BlockSpec((1, rb, W), lambda i, r: (i, r, 0))
    return pl.pallas_call(body, name=name, grid=(n, R // rb), in_specs=[spec, spec], out_specs=spec,
                          out_shape=jax.ShapeDtypeStruct(a.shape, out_dtype),
                          compiler_params=_cparams(("parallel", "parallel")))(a, b)


SLAB_BYTES = 1 << 20


def _slab(R, W):
    if R % 16 == 0:
        return _tile(R, max(16, SLAB_BYTES // (4 * W) // 16 * 16), 16), W
    assert W % LANES == 0, (R, W)
    return R, _tile(W, max(LANES, SLAB_BYTES // (4 * R) // LANES * LANES), LANES)


def _pair_add(name, g, other, c, chip):
    _, R, W = g.shape
    rb, cb = _slab(R, W)

    def body(s_ref, a_ref, b_ref, send_ref, own_ref):
        s = a_ref[0] + b_ref[0]
        send_ref[0] = s.astype(send_ref.dtype)

        @pl.when(pl.program_id(2) == s_ref[1])
        def _():
            own_ref[...] = s

    grid_spec = pltpu.PrefetchScalarGridSpec(
        num_scalar_prefetch=1, grid=(R // rb, W // cb, 4),
        in_specs=[pl.BlockSpec((1, rb, cb), lambda r, q, p, s_ref: (2 * p + s_ref[0], r, q)),
                  pl.BlockSpec((1, rb, cb), lambda r, q, p, s_ref: (p, r, q))],
        out_specs=[pl.BlockSpec((1, rb, cb), lambda r, q, p, s_ref: (p, r, q)),
                   pl.BlockSpec((rb, cb), lambda r, q, p, s_ref: (r, q))])
    scalars = jnp.stack([c, chip]).astype(jnp.int32)
    return pl.pallas_call(body, name=name, grid_spec=grid_spec,
                          out_shape=[jax.ShapeDtypeStruct((4, R, W), MXU_DTYPE), jax.ShapeDtypeStruct((R, W), F32)],
                          compiler_params=_cparams(("parallel", "parallel", "arbitrary")))(scalars, g, other)


def _sum4(name, own, parts):
    R, W = own.shape
    rb, cb = _slab(R, W)

    def body(o_ref, p_ref, out_ref):
        out_ref[...] = ((o_ref[...] + p_ref[0].astype(F32)) + p_ref[1].astype(F32)) + p_ref[2].astype(F32)

    return pl.pallas_call(body, name=name, grid=(R // rb, W // cb),
                          in_specs=[pl.BlockSpec((rb, cb), lambda r, q: (r, q)), pl.BlockSpec((3, rb, cb), lambda r, q: (0, r, q))],
                          out_specs=pl.BlockSpec((rb, cb), lambda r, q: (r, q)), out_shape=jax.ShapeDtypeStruct((R, W), F32),
                          compiler_params=_cparams(("parallel", "parallel")))(own, parts)


MESH = pl.DeviceIdType.MESH
ANY = pl.BlockSpec(memory_space=pl.ANY)


def _place():
    return lax.axis_index("x"), lax.axis_index("y"), lax.axis_index("c")


def _other_chips(x, y):
    return [(1 - x, y), (x, 1 - y), (1 - x, 1 - y)]


def _all_gather(name, blocks):
    n = len(blocks)

    def body(*refs):
        x_refs, out_refs = refs[:n], refs[n:2 * n]
        send_sems, recv_sems, local_sems = refs[2 * n:]
        x, y, c = _place()
        me, sibling = (x, y, c), (x, y, 1 - c)
        chips = _other_chips(x, y)

        def slot(a, px, py, pc):
            return out_refs[a].at[4 * px + 2 * py + pc]

        def copy(a, k, blk, to, src=None):
            return pltpu.make_async_remote_copy(src_ref=slot(a, *blk) if src is None else src, dst_ref=slot(a, *blk),
                                                send_sem=send_sems.at[a, k], recv_sem=recv_sems.at[a, k],
                                                device_id=to, device_id_type=MESH)

        mine = [pltpu.make_async_copy(x_refs[a], slot(a, *me), local_sems.at[a]) for a in range(n)]
        for cp in mine:
            cp.start()
        first = []
        for j, chip in enumerate(chips):
            first += [copy(a, 1 + j, me, (*chip, c), src=x_refs[a]) for a in range(n)]
        first += [copy(a, 0, me, sibling, src=x_refs[a]) for a in range(n)]
        for cp in first:
            cp.start()
        passed = []
        for j, chip in enumerate(chips):
            for a in range(n):
                copy(a, 1 + j, (*chip, c), me).wait_recv()
                passed.append(copy(a, 4 + j, (*chip, c), sibling))
                passed[-1].start()
        for a in range(n):
            copy(a, 0, sibling, me).wait_recv()
        for j, chip in enumerate(chips):
            for a in range(n):
                copy(a, 4 + j, (*chip, 1 - c), me).wait_recv()
        for cp in first + passed:
            cp.wait_send()
        for cp in mine:
            cp.wait()

    return pl.pallas_call(body, name=name, in_specs=[ANY] * n, out_specs=[ANY] * n,
                          out_shape=[jax.ShapeDtypeStruct((N_DEV,) + b.shape, b.dtype) for b in blocks],
                          scratch_shapes=[pltpu.SemaphoreType.DMA((n, 7)), pltpu.SemaphoreType.DMA((n, 7)),
                                          pltpu.SemaphoreType.DMA((n,))])(*blocks)


def _routes_to_sibling(x, y, c):
    return [(2 * p + (1 - c), p, (x, y, 1 - c)) for p in range(4)]


def _routes_to_chips(x, y, c):
    return [(2 * px + py, j, (px, py, c)) for j, (px, py) in enumerate(_other_chips(x, y))]


def _routes_block_to_chips(x, y, c):
    me = 4 * x + 2 * y + c
    return [(me, me, (px, py, c)) for px, py in _other_chips(x, y)]


def _routes_blocks_to_sibling(x, y, c):
    return [(4 * px + 2 * py + c, 4 * px + 2 * py + c, (x, y, 1 - c)) for px, py in [(x, y)] + _other_chips(x, y)]


def _route_copies(routes, src_refs, land_refs, send_sems, recv_sems):
    x, y, c = _place()
    copies = []
    for a, (src, land) in enumerate(zip(src_refs, land_refs)):
        plan = routes(x, y, c)
        for k, (s, d, target) in enumerate(plan):
            i = a * len(plan) + k
            copies.append(pltpu.make_async_remote_copy(src_ref=src.at[s], dst_ref=land.at[d], send_sem=send_sems.at[i],
                                                       recv_sem=recv_sems.at[i], device_id=target, device_id_type=MESH))
    return copies


def _exchange(name, routes, n_routes, srcs, land_slots):
    n = len(srcs)

    def body(*refs):
        copies = _route_copies(routes, refs[:n], refs[n:2 * n], refs[2 * n], refs[2 * n + 1])
        for cp in copies:
            cp.start()
        for cp in copies:
            cp.wait_recv()
        for cp in copies:
            cp.wait_send()

    return pl.pallas_call(body, name=name, in_specs=[ANY] * n, out_specs=[ANY] * n,
                          out_shape=[jax.ShapeDtypeStruct((land_slots,) + s.shape[1:], s.dtype) for s in srcs],
                          scratch_shapes=[pltpu.SemaphoreType.DMA((n * n_routes,)), pltpu.SemaphoreType.DMA((n * n_routes,))])(*srcs)


HBM_SPEC = pl.BlockSpec(memory_space=pltpu.HBM)
SEM_SPEC = pl.BlockSpec(memory_space=pltpu.SEMAPHORE)
DATAFLOW = pltpu.SideEffectType.DATAFLOW_SIDE_EFFECTING


def _exchange_start(name, routes, n_routes, srcs, lands, after=None):
    n = len(srcs)
    in_place = lands is None
    bufs = list(srcs) + ([] if in_place else list(lands))
    nb = len(bufs)
    extra = [] if after is None else [after]

    def body(*refs):
        src_refs = refs[:n]
        land_refs = src_refs if in_place else refs[n:nb]
        send_sems, recv_sems = refs[nb + len(extra)], refs[nb + len(extra) + 1]
        token = refs[-1]
        for cp in _route_copies(routes, src_refs, land_refs, send_sems, recv_sems):
            cp.start()
        token[...] = jnp.zeros_like(token)

    sems = [pltpu.SemaphoreType.DMA((n * n_routes,)), pltpu.SemaphoreType.DMA((n * n_routes,))]
    out = pl.pallas_call(
        body, name=name, in_specs=[HBM_SPEC] * nb + [ANY] * len(extra),
        out_shape=sems + [pltpu.HBM(b.shape, b.dtype) for b in bufs] + [jax.ShapeDtypeStruct((8, LANES), F32)],
        out_specs=[SEM_SPEC, SEM_SPEC] + [HBM_SPEC] * nb + [pl.BlockSpec(memory_space=pltpu.VMEM)],
        input_output_aliases={i: 2 + i for i in range(nb)},
        compiler_params=pltpu.CompilerParams(has_side_effects=DATAFLOW))(
        *[pltpu.with_memory_space_constraint(b, pltpu.HBM) for b in bufs], *extra)
    return (out[0], out[1], list(out[2:2 + nb])), out[-1]


def _exchange_wait(name, routes, n_routes, n, started, after):
    send_sems, recv_sems, bufs = started
    nb = len(bufs)
    in_place = nb == n

    def body(*refs):
        src_refs = refs[:n]
        land_refs = src_refs if in_place else refs[n:nb]
        for cp in _route_copies(routes, src_refs, land_refs, refs[nb], refs[nb + 1]):
            cp.wait_send()
            cp.wait_recv()

    out = pl.pallas_call(
        body, name=name, in_specs=[HBM_SPEC] * nb + [SEM_SPEC, SEM_SPEC, ANY],
        out_shape=[pltpu.HBM(b.shape, b.dtype) for b in bufs], out_specs=[HBM_SPEC] * nb,
        input_output_aliases={i: i for i in range(nb)},
        compiler_params=pltpu.CompilerParams(has_side_effects=DATAFLOW))(*bufs, send_sems, recv_sems, after)
    return list(out[:n]) if in_place else list(out[n:])


def _pair_sums(tag, gs, from_sibling):
    x, y, c = _place()
    return [_pair_add(f"rs_add_{tag}_{i}", g, o, c, 2 * x + y) for i, (g, o) in enumerate(zip(gs, from_sibling))]


def _reduce_scatter(tag, gs):
    sums = _pair_sums(tag, gs, _exchange(f"rs_swap_{tag}", _routes_to_sibling, 4, gs, 4))
    got = _exchange(f"rs_chips_{tag}", _routes_to_chips, 3, [s[0] for s in sums], 3)
    return [_sum4(f"rs_sum_{tag}_{i}", s[1], q) for i, (s, q) in enumerate(zip(sums, got))]


def _reduce_scatter_begin(tag, gs):
    lands = [lax.empty((4,) + g.shape[1:], g.dtype) for g in gs]
    swap, token = _exchange_start(f"rs_swap_{tag}_start", _routes_to_sibling, 4, gs, lands)
    return dict(tag=tag, gs=gs, swap=swap), token


def _reduce_scatter_middle(state, after):
    tag, gs = state["tag"], state["gs"]
    from_sibling = _exchange_wait(f"rs_swap_{tag}_wait", _routes_to_sibling, 4, len(gs), state["swap"], after)
    state["sums"] = _pair_sums(tag, gs, from_sibling)
    partials = [s[0] for s in state["sums"]]
    lands = [lax.empty((3,) + p.shape[1:], p.dtype) for p in partials]
    state["chips"], token = _exchange_start(f"rs_chips_{tag}_start", _routes_to_chips, 3, partials, lands)
    return token


def _reduce_scatter_end(state, after):
    tag = state["tag"]
    got = _exchange_wait(f"rs_chips_{tag}_wait", _routes_to_chips, 3, len(state["gs"]), state["chips"], after)
    return [_sum4(f"rs_sum_{tag}_{i}", s[1], q) for i, (s, q) in enumerate(zip(state["sums"], got))]


def _all_gather_begin(tag, blocks, after):
    dev = 4 * lax.axis_index("x") + 2 * lax.axis_index("y") + lax.axis_index("c")
    zones = [lax.dynamic_update_slice_in_dim(lax.empty((N_DEV,) + b.shape, b.dtype), b[None], dev, axis=0) for b in blocks]
    chips, token = _exchange_start(f"gather_{tag}_chips_start", _routes_block_to_chips, 3, zones, None, after)
    return dict(tag=tag, n=len(blocks), chips=chips), token


def _all_gather_middle(state, after):
    tag, n = state["tag"], state["n"]
    zones = _exchange_wait(f"gather_{tag}_chips_wait", _routes_block_to_chips, 3, n, state["chips"], after)
    state["sibling"], token = _exchange_start(f"gather_{tag}_sibling_start", _routes_blocks_to_sibling, 4, zones, None)
    return token


def _all_gather_end(state, after):
    return _exchange_wait(f"gather_{state['tag']}_sibling_wait", _routes_blocks_to_sibling, 4, state["n"], state["sibling"], after)


def _flat_rows(n_elems):
    return -(-n_elems // (FLAT_W * 16)) * 16


def _pack(arrays, dtype):
    flat = jnp.concatenate([a.reshape(-1).astype(dtype) for a in arrays])
    rows = _flat_rows(flat.shape[0])
    flat = jnp.pad(flat, (0, rows * FLAT_W - flat.shape[0]))
    return flat.reshape(rows, FLAT_W)


def _unpack(flat, shapes, lead=()):
    flat = flat.reshape(lead + (-1,))
    out, pos = [], 0
    for s in shapes:
        n = math.prod(s)
        out.append(flat[..., pos:pos + n].reshape(lead + tuple(s)))
        pos += n
    return out


def _ffn_pad_rows(a):
    n = a.shape[0] // FFN_HALF
    a = jnp.pad(a.reshape(n, FFN_HALF, a.shape[1]), ((0, 0), (0, FFN_HALF_PAD - FFN_HALF), (0, 0)))
    return a.reshape(n * FFN_HALF_PAD, a.shape[2])


def _ffn_unpad_rows(a):
    n = a.shape[0] // FFN_HALF_PAD
    return a.reshape(n, FFN_HALF_PAD, a.shape[1])[:, :FFN_HALF].reshape(n * FFN_HALF, a.shape[1])


def _ffn_pad_cols(a):
    n = a.shape[1] // FFN_HALF
    a = jnp.pad(a.reshape(a.shape[0], n, FFN_HALF), ((0, 0), (0, 0), (0, FFN_HALF_PAD - FFN_HALF)))
    return a.reshape(a.shape[0], n * FFN_HALF_PAD)


def _ffn_unpad_cols(a):
    n = a.shape[1] // FFN_HALF_PAD
    return a.reshape(a.shape[0], n, FFN_HALF_PAD)[:, :, :FFN_HALF].reshape(a.shape[0], n * FFN_HALF)


def _shard_to_send(name, shard):
    if name == "w_in":
        shard = shard.T
    elif name == "ffn_w_up":
        shard = _ffn_pad_rows(shard.T)
    return shard.astype(MXU_DTYPE)


def _whole_from_gathered(name, g):
    if name == "w_in":
        return _pad_in_proj_rows(g.reshape(IN_DIM, g.shape[2]))
    if name in ("w_br_gdn", "w_br_gla"):
        return jnp.transpose(g, (1, 0, 2)).reshape(g.shape[1], N_DEV * g.shape[2])
    if name == "ffn_w_down":
        return jnp.pad(g, ((0, 0), (0, FFN_HALF_PAD - FFN_HALF), (0, 0))).reshape(FFN_PAD, g.shape[2])
    return g.reshape(N_DEV * g.shape[1], g.shape[2])


def _slots_from_whole(name, gw):
    if name == "w_in":
        return _unpad_in_proj_rows(gw).reshape(N_DEV, IN_DIM // N_DEV, gw.shape[1])
    if name in ("w_br_gdn", "w_br_gla"):
        return jnp.transpose(gw.reshape(gw.shape[0], N_DEV, gw.shape[1] // N_DEV), (1, 0, 2))
    return gw.reshape(N_DEV, gw.shape[0] // N_DEV, gw.shape[1])


def _shard_from_slot(name, s):
    if name == "ffn_w_up":
        return _ffn_unpad_rows(s)
    if name == "ffn_w_down":
        return s[:FFN_HALF]
    return s


def _in_proj_pieces():
    starts, pos = {}, 0
    for n, width in IN_SPLITS:
        starts[n] = (pos, width)
        pos += width
    return [(starts[ref][0], off + lane, starts[ref][1]) for _, off, _, pieces in PAD_SEGS for ref, lane in pieces]


def _pad_in_proj_rows(w):
    rows, at = [], 0
    for src, dst, n in sorted(_in_proj_pieces(), key=lambda p: p[1]):
        if dst > at:
            rows.append(jnp.zeros((dst - at, w.shape[1]), w.dtype))
        rows.append(w[src:src + n])
        at = dst + n
    rows.append(jnp.zeros((IN_PAD - at, w.shape[1]), w.dtype))
    return jnp.concatenate(rows, axis=0)


def _unpad_in_proj_rows(wp):
    return jnp.concatenate([wp[dst:dst + n] for _, dst, n in sorted(_in_proj_pieces())], axis=0)


def _lane_pad(a, width=LANES):
    return jnp.pad(a, ((0, 0), (0, width - a.shape[1])))


def _seg_blk(h, name, rows):
    off, width = SEG[name]
    return (h, rows, width, off // width)


def _ln_both(xs_, ps_):
    (y,) = _ln_fn(xs_, ps_)
    return (y, y)


def _behind(param, hooks, stage, *seen):
    if hooks is None or stage not in hooks:
        return param
    token = hooks[stage](*seen)
    return param if token is None else param + token[0:1, 0:1]


def _layer_fwd(l, x, x_mx, W, sp, hooks=None):
    T = x.shape[0]
    n64, ngla, ntok = T // SSD_CHUNK, T // GLA_BLOCK, T // 256
    h = _mm(f"in_proj_{l}", x_mx, W["w_in"], "nt")
    xbc = _conv_silu_fwd(f"ssd_conv_{l}", h, SEG["xbc"][0], sp["ssd_conv_w"], sp["ssd_conv_b"])
    gqkv = _conv_silu_fwd(f"gdn_conv_{l}", h, SEG["gqkv"][0], sp["gdn_conv_w"], None)

    ssd_in = [(xbc, SSD_CHUNK, SSD_XBC, 0), _seg_blk(h, "dt", SSD_CHUNK), _seg_blk(h, "z", SSD_CHUNK)]
    ssd_p = [sp["ssd_dt_bias"], sp["ssd_a_log"], sp["ssd_d"], sp["ssd_norm_w"]]
    o_ssd, ssd_states = _chain_fwd(f"ssd_fwd_{l}", _ssd_chunk, n64, ssd_in, ssd_p, [(SSD_CHUNK, SSD_INNER, MXU_DTYPE)],
                                   (SSD_STATE, SSD_INNER))
    o_gdn, gdn_saved = _gdn_forward(str(l), gqkv, h, dict(sp, gdn_a_log=_behind(sp["gdn_a_log"], hooks, "ssd", o_ssd)))
    gla_in = [_seg_blk(h, "lqkv", GLA_BLOCK), _seg_blk(h, "lglr", GLA_BLOCK), _seg_blk(h, "lr", GLA_BLOCK)]
    gla_p = [jnp.pad(sp["gla_gate_w2"], ((0, LANES - GLA_RANK), (0, 0))), sp["gla_gate_b"], sp["gla_norm_w"]]
    o_gla, gla_states = _chain_fwd(f"gla_fwd_{l}", _gla_block, ngla, gla_in, gla_p, [(GLA_BLOCK, GLA_V, MXU_DTYPE)],
                                   (GLA_VAL_DIM, GLA_K))
    ln1_p = [_behind(sp["ln1_g"], hooks, "mixed", o_gdn), sp["ln1_b"]]
    y_ssd = _mm(f"br_ssd_{l}", o_ssd, W["w_br_ssd"])
    y_gdn = _mm(f"br_gdn_{l}", o_gdn, W["w_br_gdn"])
    y_gla = _mm(f"br_gla_{l}", o_gla, W["w_br_gla"])
    merge_in = [_seg_blk(h, "gates", 256), (y_ssd, 256, D_MODEL, 0), (y_gdn, 256, D_MODEL, 0), (y_gla, 256, D_MODEL, 0)]
    (mix,) = _chain_fwd(f"merge_{l}", _merge_fn, ntok, merge_in, [], [(256, D_MODEL, MXU_DTYPE)])
    r1 = _mm(f"out_proj_{l}", mix, W["w_out"])
    both = [(256, D_MODEL, F32), (256, D_MODEL, MXU_DTYPE)]
    x1, x1_mx = _chain_fwd(f"ln1_{l}", _ln_both, ntok, [(x, 256, D_MODEL, 0), (r1, 256, D_MODEL, 0)], ln1_p, both)
    up = _mm(f"ffn_up_{l}", x1_mx, W["ffn_w_up"], "nt")
    act = _ffn_glu_fwd(f"ffn_glu_{l}", up, sp["ffn_conv_w_pad"], sp["ffn_conv_b_pad"], MXU_DTYPE)
    ln2_p = [_behind(sp["ln2_g"], hooks, "ffn_act", act), sp["ln2_b"]]
    r2 = _mm(f"ffn_down_{l}", act, W["ffn_w_down"])
    x2, x2_mx = _chain_fwd(f"ln2_{l}", _ln_both, ntok, [(x1, 256, D_MODEL, 0), (r2, 256, D_MODEL, 0)], ln2_p, both)
    saved = dict(x=x, x_mx=x_mx, h=h, xbc=xbc, gqkv=gqkv, ssd_in=ssd_in, ssd_p=ssd_p, ssd_states=ssd_states,
                 gdn=gdn_saved, gla_in=gla_in, gla_p=gla_p, gla_states=gla_states, o_ssd=o_ssd,
                 o_gdn=o_gdn, o_gla=o_gla, merge_in=merge_in, mix=mix, r1=r1, ln1_p=ln1_p, x1=x1, x1_mx=x1_mx, up=up, act=act,
                 r2=r2, ln2_p=ln2_p)
    return x2, x2_mx, saved


def _layer_bwd(l, dx2, W, sp, sv, hooks=None):
    T = dx2.shape[0]
    n64, ngla, ntok = T // SSD_CHUNK, T // GLA_BLOCK, T // 256
    bf = MXU_DTYPE
    gw, gs = {}, {}
    ln2_p = [_behind(sv["ln2_p"][0], hooks, "start"), sv["ln2_p"][1]]
    (dx1_a, dr2), (gs["ln2_g"], gs["ln2_b"]) = _chain_bwd(
        f"ln2_bwd_{l}", _ln_fn, ntok, [(sv["x1"], 256, D_MODEL, 0), (sv["r2"], 256, D_MODEL, 0)], ln2_p,
        [(dx2, 256, D_MODEL)], dx_dtypes=[F32, bf])
    gw["ffn_w_down"] = _mm(f"ffn_down_dw_{l}", sv["act"], dr2, "tn")
    dact = _mm(f"ffn_down_dx_{l}", dr2, W["ffn_w_down"], "nt")
    dg, du, dwg, dwu, dbg, dbu = _ffn_glu_bwd(f"ffn_glu_bwd_{l}", sv["up"], sp["ffn_conv_w_pad"], sp["ffn_conv_b_pad"], dact, bf)
    gs["ffn_conv_w"] = _ffn_unpad_cols(jnp.concatenate([dwg, dwu], axis=1))
    gs["ffn_conv_b"] = _ffn_unpad_cols(jnp.concatenate([dbg, dbu], axis=1))
    dup = jnp.concatenate([dg, du], axis=1)
    gw["ffn_w_up"] = _mm(f"ffn_up_dw_{l}", dup, sv["x1_mx"], "tn", tn=1024)
    dx1_b = _mm(f"ffn_up_dx_{l}", dup, W["ffn_w_up"], "nn", tn=1024, tk=1024)
    ln1_p = [_behind(sv["ln1_p"][0], hooks, "ffn", dx1_b), sv["ln1_p"][1]]
    (dx_a, dr1), (gs["ln1_g"], gs["ln1_b"]) = _chain_bwd(
        f"ln1_bwd_{l}", _ln_sum_fn, ntok, [(sv["x"], 256, D_MODEL, 0), (sv["r1"], 256, D_MODEL, 0)], ln1_p,
        [(dx1_a, 256, D_MODEL), (dx1_b, 256, D_MODEL)], dx_dtypes=[F32, bf])
    gw["w_out"] = _mm(f"out_proj_dw_{l}", sv["mix"], dr1, "tn")
    dmix = _mm(f"out_proj_dx_{l}", dr1, W["w_out"], "nt")
    (dgates, dy_ssd, dy_gdn, dy_gla), _ = _chain_bwd(f"merge_bwd_{l}", _merge_fn, ntok, sv["merge_in"], [],
                                                     [(dmix, 256, D_MODEL)], dx_dtypes=[bf, bf, bf, bf])
    gw["w_br_ssd"] = _mm(f"br_ssd_dw_{l}", sv["o_ssd"], dy_ssd, "tn")
    gw["w_br_gdn"] = _mm(f"br_gdn_dw_{l}", sv["o_gdn"], dy_gdn, "tn")
    gw["w_br_gla"] = _mm(f"br_gla_dw_{l}", sv["o_gla"], dy_gla, "tn")
    do_ssd = _mm(f"br_ssd_dx_{l}", dy_ssd, W["w_br_ssd"], "nt")
    do_gdn = _mm(f"br_gdn_dx_{l}", dy_gdn, W["w_br_gdn"], "nt")
    do_gla = _mm(f"br_gla_dx_{l}", dy_gla, W["w_br_gla"], "nt")

    ssd_p = [_behind(sv["ssd_p"][0], hooks, "branches", do_gla, gw)] + list(sv["ssd_p"][1:])
    (dxbc, ddt, dz), dps = _chain_bwd(f"ssd_bwd_{l}", _ssd_chunk, n64, sv["ssd_in"], ssd_p,
                                      [(do_ssd, SSD_CHUNK, SSD_INNER)], sprev=sv["ssd_states"], dx_dtypes=[F32, bf, bf])
    gs["ssd_dt_bias"], gs["ssd_a_log"], gs["ssd_d"], gs["ssd_norm_w"] = dps
    gdn_sv = dict(sv["gdn"], scan_p=[_behind(sv["gdn"]["scan_p"][0], hooks, "ssd", dz)])
    dgqkv, dgab, dgg, gs["gdn_a_log"], gs["gdn_dt_bias"], gs["gdn_norm_w"] = _gdn_backward(str(l), do_gdn, gdn_sv, bf)
    (dlqkv, dlglr, dlr), dps = _chain_bwd(f"gla_bwd_{l}", _gla_block, ngla, sv["gla_in"], sv["gla_p"],
                                          [(do_gla, GLA_BLOCK, GLA_V)], sprev=sv["gla_states"], dx_dtypes=[bf, bf, bf])
    gs["gla_gate_w2"], gs["gla_gate_b"], gs["gla_norm_w"] = dps[0][:GLA_RANK], dps[1], dps[2]
    dxbc_pre, gs["ssd_conv_w"], gs["ssd_conv_b"] = _conv_silu_bwd(
        f"ssd_conv_bwd_{l}", sv["h"], SEG["xbc"][0], sp["ssd_conv_w"], sp["ssd_conv_b"], dxbc, bf)
    dgqkv_pre, gs["gdn_conv_w"] = _conv_silu_bwd(f"gdn_conv_bwd_{l}", sv["h"], SEG["gqkv"][0], sp["gdn_conv_w"], None, dgqkv, bf)
    pieces = dict(gates=dgates, xbc=dxbc_pre, gqkv=dgqkv_pre, z=dz, lqkv=dlqkv, gg=dgg, lr=dlr, dt=ddt, gab=dgab, lglr=dlglr)
    cols = [pieces[name] for name, _, _, _ in PAD_SEGS]
    cols.append(jnp.zeros((T, IN_PAD - PAD_SEGS[-1][1] - PAD_SEGS[-1][2]), bf))
    dh = jnp.concatenate(cols, axis=1)
    gw["w_in"] = _mm(f"in_proj_dw_{l}", dh, sv["x_mx"], "tn", tn=1024)
    if hooks is not None and "w_in_grad" in hooks:
        hooks["w_in_grad"](gw)
    dx_b = _mm(f"in_proj_dx_{l}", dh, W["w_in"], "nn", tn=1024, tk=1024)
    dx = _add_blocks(f"dx_add_{l}", dx_a[None], dx_b[None])[0]
    return dx, gw, gs


def _ln_sum_fn(xs_, ps_):
    (y,) = _ln_fn(xs_, ps_)
    return (y, y)


def _small_2d(name, a):
    return a.reshape(1, -1) if a.ndim == 1 else a


def kernel(x, w_in, ssd_conv_w, ssd_conv_b, ssd_dt_bias, ssd_a_log, ssd_d, ssd_norm_w, gdn_conv_w, gdn_a_log, gdn_dt_bias, gdn_norm_w, gla_gate_w2, gla_gate_b, gla_norm_w, w_br_ssd, w_br_gdn, w_br_gla, w_out, ln1_g, ln1_b, ffn_w_up, ffn_conv_w, ffn_conv_b, ffn_w_down, ln2_g, ln2_b, loss_target, m_w_in, m_ssd_conv_w, m_ssd_conv_b, m_ssd_dt_bias, m_ssd_a_log, m_ssd_d, m_ssd_norm_w, m_gdn_conv_w, m_gdn_a_log, m_gdn_dt_bias, m_gdn_norm_w, m_gla_gate_w2, m_gla_gate_b, m_gla_norm_w, m_w_br_ssd, m_w_br_gdn, m_w_br_gla, m_w_out, m_ln1_g, m_ln1_b, m_ffn_w_up, m_ffn_conv_w, m_ffn_conv_b, m_ffn_w_down, m_ln2_g, m_ln2_b, v_w_in, v_ssd_conv_w, v_ssd_conv_b, v_ssd_dt_bias, v_ssd_a_log, v_ssd_d, v_ssd_norm_w, v_gdn_conv_w, v_gdn_a_log, v_gdn_dt_bias, v_gdn_norm_w, v_gla_gate_w2, v_gla_gate_b, v_gla_norm_w, v_w_br_ssd, v_w_br_gdn, v_w_br_gla, v_w_out, v_ln1_g, v_ln1_b, v_ffn_w_up, v_ffn_conv_w, v_ffn_conv_b, v_ffn_w_down, v_ln2_g, v_ln2_b):
    args = locals()
    w = {n: args[n] for n in WEIGHTS}
    m = {n: args["m_" + n] for n in WEIGHTS}
    v = {n: args["v_" + n] for n in WEIGHTS}
    dev = 4 * lax.axis_index("x") + 2 * lax.axis_index("y") + lax.axis_index("c")
    xl = x[0]
    tgt = loss_target[0]

    late = BIG[1:]

    def send(names, l):
        return [_shard_to_send(n, w[n][l]) for n in names]

    def whole_weights(names, got):
        return {n: _whole_from_gathered(n, g) for n, g in zip(names, got)}

    got0 = _all_gather("gather_first", send(BIG[:1], 0) + [w[n] for n in SMALL_SHARDED])
    gather0, token0 = _all_gather_begin("w_0", send(late, 0), got0[0])
    W = [whole_weights(BIG[:1], got0[:1]), None]
    whole = dict(w)
    for n, s in zip(SMALL_SHARDED, got0[1:]):
        whole[n] = jnp.transpose(s, (1, 2, 0, 3)).reshape(s.shape[1], s.shape[2], N_DEV * s.shape[3])
    SP = [{n: _small_2d(n, whole[n][l]) for n in SMALL} for l in range(DEPTH)]
    for sp in SP:
        sp["ffn_conv_w_pad"] = _ffn_pad_cols(sp["ffn_conv_w"])
        sp["ffn_conv_b_pad"] = _ffn_pad_cols(sp["ffn_conv_b"])

    held = {}

    def late_weights_cross(o_ssd):
        token = _all_gather_middle(gather0, o_ssd)
        held["gather1"], token1 = _all_gather_begin("w_1", send(BIG, 1), o_ssd)
        return token + token1

    def late_weights_arrive(mixed):
        W[0].update(whole_weights(late, _all_gather_end(gather0, mixed)))

    fwd_hooks = {"ssd": late_weights_cross, "mixed": late_weights_arrive,
                 "ffn_act": lambda act: _all_gather_middle(held["gather1"], act)}
    saved = [None] * DEPTH
    act, act_mx, saved[0] = _layer_fwd(0, xl, (xl + token0[0, 0]).astype(MXU_DTYPE), W[0], SP[0], hooks=fwd_hooks)
    W[1] = whole_weights(BIG, _all_gather_end(held["gather1"], act))
    act, act_mx, saved[1] = _layer_fwd(1, act, act_mx, W[1], SP[1])
    dy, loss_parts = _loss_head(act, tgt)
    loss = lax.psum(jnp.sum(loss_parts), ("x", "y", "c"))

    def slots_of(names, gw):
        return [_slots_from_whole(n, gw[n]) for n in names]

    grads = {}
    GS = [None] * DEPTH
    dy, gw, GS[1] = _layer_bwd(1, dy, W[1], SP[1], saved[1])
    reduce1, reduce1_token = _reduce_scatter_begin("1", slots_of(BIG, gw))

    def late_grads_leave(seen, gw0):
        held["reduce0"], token = _reduce_scatter_begin("0", slots_of(late, gw0))
        return token

    def w_in_grad_leaves(gw0):
        held["reduce_first"], _ = _reduce_scatter_begin("first", slots_of(BIG[:1], gw0))

    bwd_hooks = {"start": lambda: reduce1_token, "ffn": lambda seen: _reduce_scatter_middle(reduce1, seen),
                 "branches": late_grads_leave, "ssd": lambda seen: _reduce_scatter_middle(held["reduce0"], seen),
                 "w_in_grad": w_in_grad_leaves}
    dy, gw, GS[0] = _layer_bwd(0, dy, W[0], SP[0], saved[0], hooks=bwd_hooks)
    red1 = _reduce_scatter_end(reduce1, dy)
    red0_late = _reduce_scatter_end(held["reduce0"], dy)
    _reduce_scatter_middle(held["reduce_first"], red1[0])
    grad_x = dy[None]
    kept_t = ("w_in", "ffn_w_up")
    grads_k = {n: jnp.stack([_shard_from_slot(n, red0_late[i]), _shard_from_slot(n, red1[i + 1])]) for i, n in enumerate(late)}

    small_shapes = [whole[n].shape for n in SMALL]
    gs_flat = _pack([jnp.stack([GS[l][n].reshape(whole[n].shape[1:]) for l in range(DEPTH)]) for n in SMALL], F32)
    (gs_all,) = _all_gather("gather_small_grads", [gs_flat])

    def mine(n, a):
        if n in SMALL_SHARDED:
            cs = a.shape[-1] // N_DEV
            return lax.dynamic_slice_in_dim(a, dev * cs, cs, axis=a.ndim - 1)
        return a

    m_whole, v_whole = {}, {}
    for n in SMALL:
        if n in SMALL_SHARDED:
            cs = w[n].shape[-1]
            zeros = jnp.zeros(whole[n].shape, F32)
            m_whole[n] = lax.dynamic_update_slice_in_dim(zeros, m[n], dev * cs, axis=2)
            v_whole[n] = lax.dynamic_update_slice_in_dim(zeros, v[n], dev * cs, axis=2)
        else:
            m_whole[n], v_whole[n] = m[n], v[n]
    outs = _adamw_small(gs_all, _pack([whole[n] for n in SMALL], F32), _pack([m_whole[n] for n in SMALL], F32),
                        _pack([v_whole[n] for n in SMALL], F32))
    g_s, d_s, m_s, v_s = [_unpack(o, small_shapes) for o in outs]
    delta, new_m, new_v = {}, {}, {}
    for i, n in enumerate(SMALL):
        grads[n], delta[n], new_m[n], new_v[n] = mine(n, g_s[i]), mine(n, d_s[i]), mine(n, m_s[i]), mine(n, v_s[i])
    for n in late + BIG[:1]:
        if n == "w_in":
            done = sum(new_v[k].reshape(-1)[0:1] for k in late + SMALL[:1])
            (first0,) = _reduce_scatter_end(held["reduce_first"], done)
            grads_k[n] = jnp.stack([first0, red1[0]])
        view = (lambda a: jnp.transpose(a, (0, 2, 1))) if n in kept_t else (lambda a: a)
        outs = _adamw(f"adamw_{n}", view(w[n]), grads_k[n], view(m[n]), view(v[n]))
        grads[n], delta[n], new_m[n], new_v[n] = view(grads_k[n]), view(outs[0]), view(outs[1]), view(outs[2])

    return (loss, grad_x, *[grads[n] for n in WEIGHTS], *[delta[n] for n in WEIGHTS], *[new_m[n] for n in WEIGHTS],
            *[new_v[n] for n in WEIGHTS])
```

```python
import functools
import math

import jax
import jax.numpy as jnp
from jax import lax
from jax.experimental import pallas as pl
from jax.experimental.pallas import tpu as pltpu

F32 = jnp.float32
MXU_DTYPE = jnp.bfloat16
HI = lax.Precision.HIGHEST

N_DEV = 8
D_MODEL = 1024
DEPTH = 2
SSD_HEADS, SSD_HEAD_DIM, SSD_INNER, SSD_GROUPS, SSD_STATE, SSD_CHUNK = 16, 64, 1024, 2, 128, 64
SSD_XBC = SSD_INNER + 2 * SSD_GROUPS * SSD_STATE
GDN_HEADS, GDN_HEAD_DIM, GDN_WIDTH, GDN_CHUNK = 4, 128, 512, 64
GLA_HEADS, GLA_KEY_DIM, GLA_VAL_DIM, GLA_K, GLA_V, GLA_RANK, GLA_CHUNK = 4, 64, 128, 256, 512, 16, 16
GLA_BLOCK = 128
GLA_NORMALIZER = 16.0
FFN_DIM = 2816
FFN_HALF = FFN_DIM // 8
FFN_HALF_PAD = 384
FFN_UP_PAD = 16 * FFN_HALF_PAD
FFN_PAD = FFN_UP_PAD // 2
ALPHA = (2 * DEPTH) ** 0.25
LN_EPS = 1e-5
RMS_EPS = 1e-6
ADAM_LR, ADAM_B1, ADAM_B2, ADAM_EPS, ADAM_WD, ADAM_STEP = 0.001, 0.9, 0.999, 1e-08, 0.01, 10
LANES = 128
NEG_BIG = -1e30
VMEM_LIMIT = 56 * 1024 * 1024

IN_SPLITS = (("z", 1024), ("xbc", 1536), ("dt", 16), ("gqkv", 1536), ("ga", 4), ("gb", 4), ("gg", 512),
             ("lqkv", 1024), ("lglr", 16), ("lr", 512), ("gates", 3072))
IN_DIM = sum(w for _, w in IN_SPLITS)
PAD_SEGS = (("gates", 0, 3072, (("gates", 0),)), ("xbc", 3072, 1536, (("xbc", 0),)),
            ("gqkv", 4608, 1536, (("gqkv", 0),)), ("z", 6144, 1024, (("z", 0),)),
            ("lqkv", 7168, 1024, (("lqkv", 0),)), ("gg", 8192, 512, (("gg", 0),)), ("lr", 8704, 512, (("lr", 0),)),
            ("dt", 9216, 128, (("dt", 0),)), ("gab", 9344, 128, (("ga", 0), ("gb", 4))), ("lglr", 9472, 128, (("lglr", 0),)))
IN_PAD = 9728
SEG = {name: (off, width) for name, off, width, _ in PAD_SEGS}

BIG = ("w_in", "w_br_ssd", "w_br_gdn", "w_br_gla", "w_out", "ffn_w_up", "ffn_w_down")
COL_SHARDED = ("w_in", "w_br_gdn", "w_br_gla", "ffn_w_up")
SMALL_SHARDED = ("ssd_conv_w", "gdn_conv_w", "gla_gate_w2", "ffn_conv_w")
WEIGHTS = ("w_in", "ssd_conv_w", "ssd_conv_b", "ssd_dt_bias", "ssd_a_log", "ssd_d", "ssd_norm_w", "gdn_conv_w",
           "gdn_a_log", "gdn_dt_bias", "gdn_norm_w", "gla_gate_w2", "gla_gate_b", "gla_norm_w", "w_br_ssd", "w_br_gdn",
           "w_br_gla", "w_out", "ln1_g", "ln1_b", "ffn_w_up", "ffn_conv_w", "ffn_conv_b", "ffn_w_down", "ln2_g", "ln2_b")
SMALL = tuple(n for n in WEIGHTS if n not in BIG)
FLAT_W = 512


def _cparams(sem=None):
    kw = dict(vmem_limit_bytes=VMEM_LIMIT)
    if sem is not None:
        kw["dimension_semantics"] = sem
    return pltpu.CompilerParams(**kw)


_DIMS = {"nn": (((1,), (0,)), ((), ())), "nt": (((1,), (1,)), ((), ())), "tn": (((0,), (0,)), ((), ()))}


def _dot(a, b, dims="nn"):
    if MXU_DTYPE == F32:
        return lax.dot_general(a.astype(F32), b.astype(F32), _DIMS[dims], precision=HI, preferred_element_type=F32)
    return lax.dot_general(a.astype(MXU_DTYPE), b.astype(MXU_DTYPE), _DIMS[dims], preferred_element_type=F32)


def _dot_hi(a, b, dims="nn"):
    return lax.dot_general(a.astype(F32), b.astype(F32), _DIMS[dims], precision=HI, preferred_element_type=F32)


def _iota2(shape, axis):
    return lax.broadcasted_iota(jnp.int32, shape, axis)


def _tril(n, strict=False):
    r, c = _iota2((n, n), 0), _iota2((n, n), 1)
    return (r > c) if strict else (r >= c)


def _raw_dot(a, b, dims):
    return lax.dot_general(a, b, _DIMS[dims], preferred_element_type=F32)


def _dot_x3(a, b, dims="nn"):
    if MXU_DTYPE == F32:
        return _dot_hi(a, b, dims)
    ah, bh = a.astype(jnp.bfloat16), b.astype(jnp.bfloat16)
    al, bl = (a - ah.astype(F32)).astype(jnp.bfloat16), (b - bh.astype(F32)).astype(jnp.bfloat16)
    return _raw_dot(ah, bh, dims) + (_raw_dot(ah, bl, dims) + _raw_dot(al, bh, dims))


def _exact_dot(mask, b, dims, mask_first):
    if MXU_DTYPE == F32:
        return _dot_hi(mask, b, dims) if mask_first else _dot_hi(b, mask, dims)
    m = mask.astype(jnp.bfloat16)
    b1 = b.astype(jnp.bfloat16)
    r1 = b - b1.astype(F32)
    b2 = r1.astype(jnp.bfloat16)
    b3 = (r1 - b2.astype(F32)).astype(jnp.bfloat16)
    if mask_first:
        return _raw_dot(m, b1, dims) + (_raw_dot(m, b2, dims) + _raw_dot(m, b3, dims))
    return _raw_dot(b1, m, dims) + (_raw_dot(b2, m, dims) + _raw_dot(b3, m, dims))


@jax.custom_vjp
def _mask_left(mask, b):
    return _exact_dot(mask, b, "nn", True)


_mask_left.defvjp(lambda mask, b: (_mask_left(mask, b), mask),
                  lambda mask, d: (jnp.zeros_like(mask), _exact_dot(mask, d, "tn", True)))


@jax.custom_vjp
def _mask_right(a, mask):
    return _exact_dot(mask, a, "nn", False)


_mask_right.defvjp(lambda a, mask: (_mask_right(a, mask), mask),
                   lambda mask, d: (_exact_dot(mask, d, "nt", False), jnp.zeros_like(mask)))


@jax.custom_vjp
def _unit_lower_inverses(mats):
    n = mats[0].shape[0]
    eye = (_iota2((n, n), 0) == _iota2((n, n), 1)).astype(F32)
    xs = [eye - a for a in mats]
    ps = list(mats)
    k = 2
    while k < n:
        ps = [_dot_x3(p, p) for p in ps]
        xs = [x + _dot_x3(x, p) for x, p in zip(xs, ps)]
        k *= 2
    return xs


def _unit_lower_inverses_fwd(mats):
    ts = _unit_lower_inverses(mats)
    return ts, ts


def _unit_lower_inverses_bwd(ts, dts):
    mids = [_dot_x3(t, d, "tn") for t, d in zip(ts, dts)]
    return ([-_dot_x3(m, t, "nt") for m, t in zip(mids, ts)],)


_unit_lower_inverses.defvjp(_unit_lower_inverses_fwd, _unit_lower_inverses_bwd)


def _ssd_chunk(xs_, ps_, s_t):
    xbc, dtraw, z = xs_
    dt_bias, a_log, d_skip, norm_w = ps_
    L = xbc.shape[0]
    H, P, N, G = SSD_HEADS, SSD_HEAD_DIM, SSD_STATE, SSD_GROUPS
    W = SSD_INNER // G
    xs = xbc[:, :SSD_INNER]
    bm = xbc[:, SSD_INNER:SSD_INNER + G * N]
    cm = xbc[:, SSD_INNER + G * N:]
    dt = jax.nn.softplus(dtraw[:, :H] + dt_bias)
    a = dt * (-jnp.exp(a_log))
    causal = _tril(L)
    a_cs = _mask_left(causal.astype(F32), a)
    expand = (_iota2((H, SSD_INNER), 1) // P == _iota2((H, SSD_INNER), 0)).astype(F32)
    wide = _mask_right(jnp.concatenate([a_cs, dt, jnp.broadcast_to(d_skip, (L, H))], axis=0), expand)
    a_cs_x, dt_x, d_x = wide[:L], wide[L:2 * L], wide[2 * L:]
    a_end_x = a_cs_x[L - 1:L, :]
    a_cs_t, dt_t = a_cs.T, dt.T
    cb = [_dot(cm[:, g * N:(g + 1) * N], bm[:, g * N:(g + 1) * N], "nt") for g in range(G)]
    decay = [jnp.exp(jnp.where(causal, a_cs[:, h:h + 1] - a_cs_t[h:h + 1, :], NEG_BIG)) * dt_t[h:h + 1, :] for h in range(H)]
    ws = [cb[h // (H // G)] * decay[h] for h in range(H)]
    y = jnp.concatenate([_dot(ws[h], xs[:, h * P:(h + 1) * P]) for h in range(H)], axis=1)
    y_in = jnp.concatenate([_dot(cm[:, g * N:(g + 1) * N], s_t[:, g * W:(g + 1) * W]) for g in range(G)], axis=1)
    y = y + y_in * jnp.exp(a_cs_x) + d_x * xs
    xw = xs * (jnp.exp(a_end_x - a_cs_x) * dt_x)
    st = jnp.concatenate([_dot(bm[:, g * N:(g + 1) * N], xw[:, g * W:(g + 1) * W], "tn") for g in range(G)], axis=1)
    s_new = s_t * jnp.exp(a_end_x) + st
    yg = y * jax.nn.silu(z)
    outs = []
    for g in range(G):
        part = yg[:, g * W:(g + 1) * W]
        outs.append(part * lax.rsqrt(jnp.mean(part * part, axis=1, keepdims=True) + RMS_EPS))
    return (jnp.concatenate(outs, axis=1) * norm_w,), s_new


GDN_PREP_CHUNKS = 4


def _gdn_prep(xs_, ps_):
    qkv, ab = xs_
    a_log, dt_bias = ps_
    B = qkv.shape[0]
    H, D, L = GDN_HEADS, GDN_HEAD_DIM, GDN_CHUNK
    g_all = -jnp.exp(a_log) * jax.nn.softplus(ab + dt_bias)
    row, col = _iota2((B, B), 0), _iota2((B, B), 1)
    g_cs = _mask_left((((row // L) == (col // L)) & (row >= col)).astype(F32), g_all)
    g_cs_t = g_cs.T
    beta_all = jax.nn.sigmoid(ab)
    incl, strict = _tril(L), _tril(L, strict=True)
    qs, ks, vs = [], [], []
    for h in range(H):
        q = qkv[:, h * D:(h + 1) * D]
        k = qkv[:, GDN_WIDTH + h * D:GDN_WIDTH + (h + 1) * D]
        qs.append(q * lax.rsqrt(jnp.sum(q * q, axis=1, keepdims=True) + RMS_EPS) * (D ** -0.5))
        ks.append(k * lax.rsqrt(jnp.sum(k * k, axis=1, keepdims=True) + RMS_EPS))
        vs.append(qkv[:, 2 * GDN_WIDTH + h * D:2 * GDN_WIDTH + (h + 1) * D])
    pairs = [(c, h) for c in range(B // L) for h in range(H)]
    rows = {c: slice(c * L, (c + 1) * L) for c in range(B // L)}
    q_ = {(c, h): qs[h][rows[c]] for c, h in pairs}
    k_ = {(c, h): ks[h][rows[c]] for c, h in pairs}
    col_ = {(c, h): g_cs[rows[c], h:h + 1] for c, h in pairs}
    beta_ = {(c, h): beta_all[rows[c], H + h:H + h + 1] for c, h in pairs}
    gamma = {p: jnp.exp(jnp.where(incl, col_[p] - g_cs_t[p[1]:p[1] + 1, rows[p[0]]], NEG_BIG)) for p in pairs}
    kb = {p: k_[p] * beta_[p] for p in pairs}
    a_mat = [jnp.where(strict, _dot(kb[p], k_[p], "nt") * gamma[p], 0.0) for p in pairs]
    attn = {p: jnp.where(incl, _dot(q_[p], k_[p], "nt") * gamma[p], 0.0) for p in pairs}
    t_mat = dict(zip(pairs, _unit_lower_inverses(a_mat)))
    u = {p: _dot(t_mat[p], vs[p[1]][rows[p[0]]] * beta_[p]) for p in pairs}
    w = {p: _dot(t_mat[p], kb[p] * jnp.exp(col_[p])) for p in pairs}
    qd = {p: q_[p] * jnp.exp(col_[p]) for p in pairs}
    kd = {p: k_[p] * jnp.exp(col_[p][L - 1:L, :] - col_[p]) for p in pairs}

    def whole(parts):
        return jnp.concatenate([jnp.concatenate([parts[(c, h)] for h in range(H)], axis=1) for c in range(B // L)], axis=0)

    return (whole(u), whole(w), whole(qd), whole(kd), whole(attn), g_cs)


def _gdn_scan(xs_, ps_, s):
    u, w, qd, kd, attn, g_cs, gate = xs_
    (norm_w,) = ps_
    L = u.shape[0]
    H, D = GDN_HEADS, GDN_HEAD_DIM
    heads = range(H)
    lanes = [slice(h * D, (h + 1) * D) for h in heads]
    s_h = [s[lanes[h], :] for h in heads]
    v_new = [u[:, lanes[h]] - _dot(w[:, lanes[h]], s_h[h]) for h in heads]
    o = [_dot(qd[:, lanes[h]], s_h[h]) + _dot(attn[:, h * L:(h + 1) * L], v_new[h]) for h in heads]
    decay = [jnp.exp(g_cs[L - 1:L, h:h + 1]) for h in heads]
    s_new = [s_h[h] * decay[h] + _dot(kd[:, lanes[h]], v_new[h], "tn") for h in heads]
    o = [o[h] * lax.rsqrt(jnp.mean(o[h] * o[h], axis=1, keepdims=True) + RMS_EPS) * norm_w * jax.nn.silu(gate[:, lanes[h]])
         for h in heads]
    return (jnp.concatenate(o, axis=1),), jnp.concatenate(s_new, axis=0)


def _gdn_forward(tag, gqkv, h, sp):
    T = gqkv.shape[0]
    blk = GDN_PREP_CHUNKS * GDN_CHUNK
    prep_in = [(gqkv, blk, 3 * GDN_WIDTH, 0), _seg_blk(h, "gab", blk)]
    prep_p = [_lane_pad(sp["gdn_a_log"]), _lane_pad(sp["gdn_dt_bias"])]
    mx = MXU_DTYPE
    prep = _chain_fwd(f"gdn_prep_{tag}", _gdn_prep, T // blk, prep_in, prep_p,
                      [(blk, GDN_WIDTH, F32), (blk, GDN_WIDTH, mx), (blk, GDN_WIDTH, mx), (blk, GDN_WIDTH, mx),
                       (blk, GDN_HEADS * GDN_CHUNK, mx), (blk, LANES, F32)])
    widths = [GDN_WIDTH] * 4 + [GDN_HEADS * GDN_CHUNK, LANES]
    scan_in = [(a, GDN_CHUNK, wd, 0) for a, wd in zip(prep, widths)] + [_seg_blk(h, "gg", GDN_CHUNK)]
    scan_p = [sp["gdn_norm_w"]]
    o, states = _chain_fwd(f"gdn_scan_{tag}", _gdn_scan, T // GDN_CHUNK, scan_in, scan_p, [(GDN_CHUNK, GDN_WIDTH, mx)],
                           (GDN_WIDTH, GDN_HEAD_DIM))
    return o, dict(prep_in=prep_in, prep_p=prep_p, scan_in=scan_in, scan_p=scan_p, states=states, widths=widths)


def _gdn_backward(tag, do, sv, dx_dtype):
    T = do.shape[0]
    blk = GDN_PREP_CHUNKS * GDN_CHUNK
    dscan, (dnorm,) = _chain_bwd(f"gdn_scan_bwd_{tag}", _gdn_scan, T // GDN_CHUNK, sv["scan_in"], sv["scan_p"],
                                 [(do, GDN_CHUNK, GDN_WIDTH)], sprev=sv["states"], dx_dtypes=[F32] * 6 + [dx_dtype])
    douts = [(d, blk, wd) for d, wd in zip(dscan[:6], sv["widths"])]
    (dgqkv, dgab), (da_log, ddt_bias) = _chain_bwd(f"gdn_prep_bwd_{tag}", _gdn_prep, T // blk, sv["prep_in"], sv["prep_p"],
                                                   douts, dx_dtypes=[F32, dx_dtype])
    return dgqkv, dgab, dscan[6], da_log[:, :GDN_HEADS], ddt_bias[:, :GDN_HEADS], dnorm


def _gla_block(xs_, ps_, s_t):
    qkv, glr, r = xs_
    w2, gate_b, norm_w = ps_
    B = qkv.shape[0]
    H, K, V, C = GLA_HEADS, GLA_KEY_DIM, GLA_VAL_DIM, GLA_CHUNK
    q = qkv[:, :GLA_K] * (K ** -0.5)
    k = qkv[:, GLA_K:2 * GLA_K]
    v = qkv[:, 2 * GLA_K:]
    gk = jax.nn.log_sigmoid(_dot(glr, w2) + gate_b) / GLA_NORMALIZER
    row, col = _iota2((B, B), 0), _iota2((B, B), 1)
    same = (row // C) == (col // C)
    mask = same & (row >= col)
    b_cs = _mask_left(mask.astype(F32), gk)
    b_end = _mask_left((col == (row // C) * C + (C - 1)).astype(F32), b_cs)
    q_e = q * jnp.exp(b_cs)
    k_e = k * jnp.exp(-b_cs)
    k_d = k * jnp.exp(b_end - b_cs)
    intra = []
    for h in range(H):
        a_mat = jnp.where(mask, _dot(q_e[:, h * K:(h + 1) * K], k_e[:, h * K:(h + 1) * K], "nt"), 0.0)
        intra.append(_dot(a_mat, v[:, h * V:(h + 1) * V]))
    o = jnp.concatenate(intra, axis=1)
    chunks = [slice(j * C, (j + 1) * C) for j in range(B // C)]
    fresh = [jnp.concatenate([_dot(v[sl, h * V:(h + 1) * V], k_d[sl, h * K:(h + 1) * K], "tn") for h in range(H)], axis=1)
             for sl in chunks]
    entering = []
    for j, sl in enumerate(chunks):
        entering.append(s_t)
        s_t = s_t * jnp.exp(b_end[j * C:j * C + 1, :]) + fresh[j]
    inter = [jnp.concatenate([_dot(q_e[sl, h * K:(h + 1) * K], entering[j][:, h * K:(h + 1) * K], "nt") for h in range(H)],
                             axis=1) for j, sl in enumerate(chunks)]
    o = o + jnp.concatenate(inter, axis=0)
    outs = []
    for h in range(H):
        oh = o[:, h * V:(h + 1) * V]
        oh = oh * lax.rsqrt(jnp.mean(oh * oh, axis=1, keepdims=True) + RMS_EPS) * norm_w
        outs.append(oh * jax.nn.silu(r[:, h * V:(h + 1) * V]))
    return (jnp.concatenate(outs, axis=1),), s_t


def _merge_fn(xs_, ps_):
    gates, y_ssd, y_gdn, y_gla = xs_
    d = D_MODEL
    return (jax.nn.sigmoid(gates[:, :d]) * y_ssd + jax.nn.sigmoid(gates[:, d:2 * d]) * y_gdn
            + jax.nn.sigmoid(gates[:, 2 * d:]) * y_gla,)


def _ln_fn(xs_, ps_):
    x, r = xs_
    g, b = ps_
    t = ALPHA * x + r
    mu = jnp.mean(t, axis=1, keepdims=True)
    var = jnp.mean(jnp.square(t - mu), axis=1, keepdims=True)
    return ((t - mu) * lax.rsqrt(var + LN_EPS) * g + b,)


def _row_spec(rows, width, colblk, n, reverse):
    if reverse:
        return pl.BlockSpec((rows, width), lambda c: (n - 1 - c, colblk))
    return pl.BlockSpec((rows, width), lambda c: (c, colblk))


def _full_spec(shape):
    zeros = (0,) * len(shape)
    return pl.BlockSpec(shape, lambda c: zeros)


def _chain_fwd(name, fn, n, blocked, full, out_defs, state_shape=None):
    nb, nf, no = len(blocked), len(full), len(out_defs)

    def body(*refs):
        xs = [r[...].astype(F32) for r in refs[:nb]]
        ps = [r[...] for r in refs[nb:nb + nf]]
        o_refs = refs[nb + nf:nb + nf + no]
        if state_shape is None:
            outs = fn(xs, ps)
        else:
            sprev_ref, s_ref = refs[nb + nf + no:]

            @pl.when(pl.program_id(0) == 0)
            def _():
                s_ref[...] = jnp.zeros_like(s_ref)

            s = s_ref[...]
            sprev_ref[0] = s
            outs, s_new = fn(xs, ps, s)
            s_ref[...] = s_new
        for r, o in zip(o_refs, outs):
            r[...] = o.astype(r.dtype)

    in_specs = [_row_spec(rows, width, cb, n, False) for _, rows, width, cb in blocked]
    in_specs += [_full_spec(a.shape) for a in full]
    out_specs = [_row_spec(rows, width, 0, n, False) for rows, width, _ in out_defs]
    out_shape = [jax.ShapeDtypeStruct((n * rows, width), dt) for rows, width, dt in out_defs]
    scratch = []
    if state_shape is not None:
        out_specs.append(pl.BlockSpec((1,) + state_shape, lambda c: (c, 0, 0)))
        out_shape.append(jax.ShapeDtypeStruct((n,) + state_shape, F32))
        scratch.append(pltpu.VMEM(state_shape, F32))
    return pl.pallas_call(body, name=name, grid=(n,), in_specs=in_specs, out_specs=out_specs, out_shape=out_shape,
                          scratch_shapes=scratch, compiler_params=_cparams(("arbitrary",)))(
        *[a for a, _, _, _ in blocked], *full)


def _chain_bwd(name, fn, n, blocked, full, douts, sprev=None, dx_dtypes=None):
    nb, nf, nd = len(blocked), len(full), len(douts)
    has_state = sprev is not None
    dx_dtypes = dx_dtypes or [F32] * nb

    def body(*refs):
        pos = 0
        b_refs = refs[pos:pos + nb]; pos += nb
        f_refs = refs[pos:pos + nf]; pos += nf
        d_refs = refs[pos:pos + nd]; pos += nd
        if has_state:
            sprev_ref = refs[pos]; pos += 1
        dx_refs = refs[pos:pos + nb]; pos += nb
        dp_refs = refs[pos:pos + nf]; pos += nf
        if has_state:
            ds_ref = refs[pos]

        @pl.when(pl.program_id(0) == 0)
        def _():
            for r in dp_refs:
                r[...] = jnp.zeros_like(r)
            if has_state:
                ds_ref[...] = jnp.zeros_like(ds_ref)

        xs = [r[...].astype(F32) for r in b_refs]
        ps = [r[...] for r in f_refs]
        dys = tuple(r[...].astype(F32) for r in d_refs)
        if has_state:
            _, vjp = jax.vjp(fn, xs, ps, sprev_ref[0])
            dxs, dps, ds = vjp((dys, ds_ref[...]))
            ds_ref[...] = ds
        else:
            _, vjp = jax.vjp(fn, xs, ps)
            dxs, dps = vjp(dys)
        for r, d in zip(dx_refs, dxs):
            r[...] = d.astype(r.dtype)
        for r, d in zip(dp_refs, dps):
            r[...] += d

    in_specs = [_row_spec(rows, width, cb, n, True) for _, rows, width, cb in blocked]
    in_specs += [_full_spec(a.shape) for a in full]
    in_specs += [_row_spec(rows, width, 0, n, True) for _, rows, width in douts]
    args = [a for a, _, _, _ in blocked] + list(full) + [a for a, _, _ in douts]
    scratch = []
    if has_state:
        st_shape = sprev.shape[1:]
        in_specs.append(pl.BlockSpec((1,) + st_shape, lambda c: (n - 1 - c, 0, 0)))
        args.append(sprev)
        scratch.append(pltpu.VMEM(st_shape, F32))
    out_specs = [_row_spec(rows, width, 0, n, True) for _, rows, width, _ in blocked]
    out_specs += [_full_spec(a.shape) for a in full]
    out_shape = [jax.ShapeDtypeStruct((n * rows, width), dt) for (_, rows, width, _), dt in zip(blocked, dx_dtypes)]
    out_shape += [jax.ShapeDtypeStruct(a.shape, F32) for a in full]
    res = pl.pallas_call(body, name=name, grid=(n,), in_specs=in_specs, out_specs=out_specs, out_shape=out_shape,
                         scratch_shapes=scratch, compiler_params=_cparams(("arbitrary",)))(*args)
    return res[:nb], res[nb:]


def _tile(n, target, unit):
    if n <= target:
        return n
    best = None
    for t in range(unit, target + 1, unit):
        if n % t == 0:
            best = t
    assert best is not None, (n, target, unit)
    return best


def _mm(name, a, b, dims="nn", out_dtype=F32, tm=2048, tn=512, tk=2048, after=None):
    if dims == "nn":
        (M, K), (_, N) = a.shape, b.shape
    elif dims == "nt":
        (M, K), (N, _) = a.shape, b.shape
    else:
        (K, M), (_, N) = a.shape, b.shape
    tm, tn, tk = _tile(M, tm, LANES), _tile(N, tn, LANES), _tile(K, tk, LANES)
    nk = K // tk
    extra = [] if after is None else [after]

    def body(*refs):
        a_ref, b_ref = refs[:2]
        o_ref, acc_ref = refs[-2:]
        part = _dot(a_ref[...], b_ref[...], dims)
        if nk == 1:
            o_ref[...] = part.astype(o_ref.dtype)
            return
        k = pl.program_id(2)

        @pl.when(k == 0)
        def _():
            acc_ref[...] = part

        @pl.when((k > 0) & (k < nk - 1))
        def _():
            acc_ref[...] += part

        @pl.when(k == nk - 1)
        def _():
            o_ref[...] = (acc_ref[...] + part).astype(o_ref.dtype)

    if dims == "tn":
        a_spec = pl.BlockSpec((tk, tm), lambda j, i, k: (k, i))
    else:
        a_spec = pl.BlockSpec((tm, tk), lambda j, i, k: (i, k))
    if dims == "nt":
        b_spec = pl.BlockSpec((tn, tk), lambda j, i, k: (j, k))
    else:
        b_spec = pl.BlockSpec((tk, tn), lambda j, i, k: (k, j))
    return pl.pallas_call(
        body, name=name, grid=(N // tn, M // tm, nk), in_specs=[a_spec, b_spec] + [ANY] * len(extra),
        out_specs=pl.BlockSpec((tm, tn), lambda j, i, k: (i, j)), out_shape=jax.ShapeDtypeStruct((M, N), out_dtype),
        scratch_shapes=[pltpu.VMEM((tm, tn) if nk > 1 else (8, LANES), F32)],
        compiler_params=_cparams(("parallel", "parallel", "arbitrary")))(a, b, *extra)


CONV_CB = 256


def _shift_down(x, k):
    if k == 0:
        return x
    return jnp.where(_iota2(x.shape, 0) >= k, pltpu.roll(x, k, 0), 0.0)


def _shift_up(x, k):
    if k == 0:
        return x
    t = x.shape[0]
    return jnp.where(_iota2(x.shape, 0) < t - k, pltpu.roll(x, t - k, 0), 0.0)


def _conv_pre(x, w, b):
    kk = w.shape[0]
    pre = x * w[kk - 1:kk, :]
    for k in range(kk - 1):
        pre = pre + _shift_down(x, kk - 1 - k) * w[k:k + 1, :]
    return pre if b is None else pre + b


def _conv_bwd_pre(x, w, dpre, dw_ref, db_ref):
    kk = w.shape[0]
    dx = dpre * w[kk - 1:kk, :]
    dw_ref[kk - 1:kk, :] = jnp.sum(dpre * x, axis=0, keepdims=True)
    for k in range(kk - 1):
        dx = dx + _shift_up(dpre, kk - 1 - k) * w[k:k + 1, :]
        dw_ref[k:k + 1, :] = jnp.sum(dpre * _shift_down(x, kk - 1 - k), axis=0, keepdims=True)
    if db_ref is not None:
        db_ref[...] = jnp.sum(dpre, axis=0, keepdims=True)
    return dx


def _dsilu(pre):
    sg = jax.nn.sigmoid(pre)
    return sg * (1.0 + pre * (1.0 - sg))


def _conv_silu_fwd(name, src, col0, w, b):
    T = src.shape[0]
    kk, C = w.shape
    cb = CONV_CB
    off = col0 // cb

    def body(*refs):
        x_ref, w_ref = refs[:2]
        b_val = refs[2][...] if b is not None else None
        refs[-1][...] = jax.nn.silu(_conv_pre(x_ref[...], w_ref[...], b_val))

    in_specs = [pl.BlockSpec((T, cb), lambda j: (0, off + j)), pl.BlockSpec((kk, cb), lambda j: (0, j))]
    args = [src, w]
    if b is not None:
        in_specs.append(pl.BlockSpec((1, cb), lambda j: (0, j)))
        args.append(b)
    return pl.pallas_call(body, name=name, grid=(C // cb,), in_specs=in_specs,
                          out_specs=pl.BlockSpec((T, cb), lambda j: (0, j)), out_shape=jax.ShapeDtypeStruct((T, C), F32),
                          compiler_params=_cparams(("parallel",)))(*args)


def _conv_silu_bwd(name, src, col0, w, b, dy, dx_dtype):
    T = src.shape[0]
    kk, C = w.shape
    cb = CONV_CB
    off = col0 // cb
    has_b = b is not None

    def body(*refs):
        x_ref, w_ref = refs[:2]
        pos = 2
        b_val = None
        if has_b:
            b_val = refs[pos][...]; pos += 1
        dy_ref = refs[pos]; pos += 1
        dx_ref, dw_ref = refs[pos], refs[pos + 1]
        db_ref = refs[pos + 2] if has_b else None
        x, wv = x_ref[...], w_ref[...]
        dpre = dy_ref[...] * _dsilu(_conv_pre(x, wv, b_val))
        dx_ref[...] = _conv_bwd_pre(x, wv, dpre, dw_ref, db_ref).astype(dx_ref.dtype)

    in_specs = [pl.BlockSpec((T, cb), lambda j: (0, off + j)), pl.BlockSpec((kk, cb), lambda j: (0, j))]
    args = [src, w]
    if has_b:
        in_specs.append(pl.BlockSpec((1, cb), lambda j: (0, j)))
        args.append(b)
    in_specs.append(pl.BlockSpec((T, cb), lambda j: (0, j)))
    args.append(dy)
    out_specs = [pl.BlockSpec((T, cb), lambda j: (0, j)), pl.BlockSpec((kk, cb), lambda j: (0, j))]
    out_shape = [jax.ShapeDtypeStruct((T, C), dx_dtype), jax.ShapeDtypeStruct((kk, C), F32)]
    if has_b:
        out_specs.append(pl.BlockSpec((1, cb), lambda j: (0, j)))
        out_shape.append(jax.ShapeDtypeStruct((1, C), F32))
    return pl.pallas_call(body, name=name, grid=(C // cb,), in_specs=in_specs, out_specs=out_specs, out_shape=out_shape,
                          compiler_params=_cparams(("parallel",)))(*args)


def _ffn_glu_fwd(name, up, w, b, out_dtype=F32):
    T = up.shape[0]
    kk = w.shape[0]
    cb = CONV_CB
    width = up.shape[1] // 2
    nblk = width // cb

    def body(g_ref, u_ref, wg_ref, wu_ref, bg_ref, bu_ref, o_ref):
        g = _conv_pre(g_ref[...], wg_ref[...], bg_ref[...])
        u = _conv_pre(u_ref[...], wu_ref[...], bu_ref[...])
        o_ref[...] = (jax.nn.silu(g) * u).astype(o_ref.dtype)

    lo, hi = (lambda j: (0, j)), (lambda j: (0, nblk + j))
    in_specs = [pl.BlockSpec((T, cb), lo), pl.BlockSpec((T, cb), hi), pl.BlockSpec((kk, cb), lo), pl.BlockSpec((kk, cb), hi),
                pl.BlockSpec((1, cb), lo), pl.BlockSpec((1, cb), hi)]
    return pl.pallas_call(body, name=name, grid=(nblk,), in_specs=in_specs, out_specs=pl.BlockSpec((T, cb), lo),
                          out_shape=jax.ShapeDtypeStruct((T, width), out_dtype),
                          compiler_params=_cparams(("parallel",)))(up, up, w, w, b, b)


def _ffn_glu_bwd(name, up, w, b, dact, dx_dtype):
    T = up.shape[0]
    kk = w.shape[0]
    cb = CONV_CB
    width = up.shape[1] // 2
    nblk = width // cb

    def body(g_ref, u_ref, wg_ref, wu_ref, bg_ref, bu_ref, d_ref, dg_ref, du_ref, dwg_ref, dwu_ref, dbg_ref, dbu_ref):
        xg, xu, wg, wu = g_ref[...], u_ref[...], wg_ref[...], wu_ref[...]
        g = _conv_pre(xg, wg, bg_ref[...])
        u = _conv_pre(xu, wu, bu_ref[...])
        d = d_ref[...].astype(F32)
        dg_ref[...] = _conv_bwd_pre(xg, wg, d * u * _dsilu(g), dwg_ref, dbg_ref).astype(dg_ref.dtype)
        du_ref[...] = _conv_bwd_pre(xu, wu, d * jax.nn.silu(g), dwu_ref, dbu_ref).astype(du_ref.dtype)

    lo, hi = (lambda j: (0, j)), (lambda j: (0, nblk + j))
    in_specs = [pl.BlockSpec((T, cb), lo), pl.BlockSpec((T, cb), hi), pl.BlockSpec((kk, cb), lo), pl.BlockSpec((kk, cb), hi),
                pl.BlockSpec((1, cb), lo), pl.BlockSpec((1, cb), hi), pl.BlockSpec((T, cb), lo)]
    out_specs = [pl.BlockSpec((T, cb), lo)] * 2 + [pl.BlockSpec((kk, cb), lo)] * 2 + [pl.BlockSpec((1, cb), lo)] * 2
    out_shape = ([jax.ShapeDtypeStruct((T, width), dx_dtype)] * 2 + [jax.ShapeDtypeStruct((kk, width), F32)] * 2
                 + [jax.ShapeDtypeStruct((1, width), F32)] * 2)
    return pl.pallas_call(body, name=name, grid=(nblk,), in_specs=in_specs, out_specs=out_specs, out_shape=out_shape,
                          compiler_params=_cparams(("parallel",)))(up, up, w, w, b, b, dact)


def _loss_head(y, target):
    T, D = y.shape
    tb = _tile(T, 256, 8)

    def body(y_ref, t_ref, dy_ref, l_ref):
        @pl.when(pl.program_id(0) == 0)
        def _():
            l_ref[...] = jnp.zeros_like(l_ref)

        err = y_ref[...] - t_ref[...]
        dy_ref[...] = err * (1.0 / D)
        l_ref[...] += jnp.sum(err * err, axis=0, keepdims=True) * (0.5 / D)

    spec = pl.BlockSpec((tb, D), lambda i: (i, 0))
    return pl.pallas_call(body, name="loss_head", grid=(T // tb,), in_specs=[spec, spec],
                          out_specs=[spec, pl.BlockSpec((1, D), lambda i: (0, 0))],
                          out_shape=[jax.ShapeDtypeStruct((T, D), F32), jax.ShapeDtypeStruct((1, D), F32)],
                          compiler_params=_cparams(("arbitrary",)))(y, target)


def _adamw_math(w, g, m, v):
    m = ADAM_B1 * m + (1.0 - ADAM_B1) * g
    v = ADAM_B2 * v + (1.0 - ADAM_B2) * jnp.square(g)
    m_hat = m / (1.0 - ADAM_B1 ** ADAM_STEP)
    v_hat = v / (1.0 - ADAM_B2 ** ADAM_STEP)
    return -ADAM_LR * (m_hat / (jnp.sqrt(v_hat) + ADAM_EPS) + ADAM_WD * w), m, v


def _adamw(name, w, g, m, v):
    A, R, C = w.shape
    if C % LANES == 0:
        rb, cb = _slab(R, C)
    else:
        rb, cb = _tile(R, max(8, SLAB_BYTES // 2 // (C * 4) // 8 * 8), 8), C

    def body(w_ref, g_ref, m_ref, v_ref, d_ref, mo_ref, vo_ref):
        d, mn, vn = _adamw_math(w_ref[...], g_ref[...], m_ref[...], v_ref[...])
        d_ref[...] = d
        mo_ref[...] = mn
        vo_ref[...] = vn

    spec = pl.BlockSpec((1, rb, cb), lambda a, r, q: (a, r, q))
    return pl.pallas_call(body, name=name, grid=(A, R // rb, C // cb), in_specs=[spec] * 4, out_specs=[spec] * 3,
                          out_shape=[jax.ShapeDtypeStruct(w.shape, F32)] * 3,
                          compiler_params=_cparams(("parallel", "parallel", "parallel")))(w, g, m, v)


def _adamw_small(parts, w, m, v):
    def body(p_ref, w_ref, m_ref, v_ref, g_ref, d_ref, mo_ref, vo_ref):
        g = p_ref[0]
        for i in range(1, N_DEV):
            g = g + p_ref[i]
        d, mn, vn = _adamw_math(w_ref[...], g, m_ref[...], v_ref[...])
        g_ref[...] = g
        d_ref[...] = d
        mo_ref[...] = mn
        vo_ref[...] = vn

    return pl.pallas_call(body, name="adamw_small", out_shape=[jax.ShapeDtypeStruct(w.shape, F32)] * 4,
                          compiler_params=_cparams())(parts, w, m, v)


def _add_blocks(name, a, b, out_dtype=F32):
    n, R, W = a.shape
    rb = _tile(R, 512, 8)

    def body(a_ref, b_ref, o_ref):
        o_ref[...] = (a_ref[...].astype(F32) + b_ref[...].astype(F32)).astype(o_ref.dtype)

    spec = pl.BlockSpec((1, rb, W), lambda i, r: (i, r, 0))
    return pl.pallas_call(body, name=name, grid=(n, R // rb), in_specs=[spec, spec], out_specs=spec,
                          out_shape=jax.ShapeDtypeStruct(a.shape, out_dtype),
                          compiler_params=_cparams(("parallel", "parallel")))(a, b)


SLAB_BYTES = 5 << 19


def _slab(R, W):
    if R % 16 == 0:
        return _tile(R, max(16, SLAB_BYTES // (4 * W) // 16 * 16), 16), W
    assert W % LANES == 0, (R, W)
    return R, _tile(W, max(LANES, SLAB_BYTES // (4 * R) // LANES * LANES), LANES)


def _pair_add(name, g, other, c, chip):
    _, R, W = g.shape
    rb, cb = _slab(R, W)

    def body(s_ref, a_ref, b_ref, send_ref, own_ref):
        s = a_ref[0] + b_ref[0]
        send_ref[0] = s.astype(send_ref.dtype)

        @pl.when(pl.program_id(2) == s_ref[1])
        def _():
            own_ref[...] = s

    grid_spec = pltpu.PrefetchScalarGridSpec(
        num_scalar_prefetch=1, grid=(R // rb, W // cb, 4),
        in_specs=[pl.BlockSpec((1, rb, cb), lambda r, q, p, s_ref: (2 * p + s_ref[0], r, q)),
                  pl.BlockSpec((1, rb, cb), lambda r, q, p, s_ref: (p, r, q))],
        out_specs=[pl.BlockSpec((1, rb, cb), lambda r, q, p, s_ref: (p, r, q)),
                   pl.BlockSpec((rb, cb), lambda r, q, p, s_ref: (r, q))])
    scalars = jnp.stack([c, chip]).astype(jnp.int32)
    return pl.pallas_call(body, name=name, grid_spec=grid_spec,
                          out_shape=[jax.ShapeDtypeStruct((4, R, W), MXU_DTYPE), jax.ShapeDtypeStruct((R, W), F32)],
                          compiler_params=_cparams(("parallel", "parallel", "arbitrary")))(scalars, g, other)


def _sum4(name, own, parts):
    R, W = own.shape
    rb, cb = _slab(R, W)

    def body(o_ref, p_ref, out_ref):
        out_ref[...] = ((o_ref[...] + p_ref[0].astype(F32)) + p_ref[1].astype(F32)) + p_ref[2].astype(F32)

    return pl.pallas_call(body, name=name, grid=(R // rb, W // cb),
                          in_specs=[pl.BlockSpec((rb, cb), lambda r, q: (r, q)), pl.BlockSpec((3, rb, cb), lambda r, q: (0, r, q))],
                          out_specs=pl.BlockSpec((rb, cb), lambda r, q: (r, q)), out_shape=jax.ShapeDtypeStruct((R, W), F32),
                          compiler_params=_cparams(("parallel", "parallel")))(own, parts)


MESH = pl.DeviceIdType.MESH
ANY = pl.BlockSpec(memory_space=pl.ANY)


def _place():
    return lax.axis_index("x"), lax.axis_index("y"), lax.axis_index("c")


def _other_chips(x, y):
    return [(1 - x, y), (x, 1 - y), (1 - x, 1 - y)]


def _all_gather(name, blocks):
    n = len(blocks)

    def body(*refs):
        x_refs, out_refs = refs[:n], refs[n:2 * n]
        send_sems, recv_sems, local_sems = refs[2 * n:]
        x, y, c = _place()
        me, sibling = (x, y, c), (x, y, 1 - c)
        chips = _other_chips(x, y)

        def slot(a, px, py, pc):
            return out_refs[a].at[4 * px + 2 * py + pc]

        def copy(a, k, blk, to, src=None):
            return pltpu.make_async_remote_copy(src_ref=slot(a, *blk) if src is None else src, dst_ref=slot(a, *blk),
                                                send_sem=send_sems.at[a, k], recv_sem=recv_sems.at[a, k],
                                                device_id=to, device_id_type=MESH)

        mine = [pltpu.make_async_copy(x_refs[a], slot(a, *me), local_sems.at[a]) for a in range(n)]
        for cp in mine:
            cp.start()
        first = []
        for j, chip in enumerate(chips):
            first += [copy(a, 1 + j, me, (*chip, c), src=x_refs[a]) for a in range(n)]
        first += [copy(a, 0, me, sibling, src=x_refs[a]) for a in range(n)]
        for cp in first:
            cp.start()
        passed = []
        for j, chip in enumerate(chips):
            for a in range(n):
                copy(a, 1 + j, (*chip, c), me).wait_recv()
                passed.append(copy(a, 4 + j, (*chip, c), sibling))
                passed[-1].start()
        for a in range(n):
            copy(a, 0, sibling, me).wait_recv()
        for j, chip in enumerate(chips):
            for a in range(n):
                copy(a, 4 + j, (*chip, 1 - c), me).wait_recv()
        for cp in first + passed:
            cp.wait_send()
        for cp in mine:
            cp.wait()

    return pl.pallas_call(body, name=name, in_specs=[ANY] * n, out_specs=[ANY] * n,
                          out_shape=[jax.ShapeDtypeStruct((N_DEV,) + b.shape, b.dtype) for b in blocks],
                          scratch_shapes=[pltpu.SemaphoreType.DMA((n, 7)), pltpu.SemaphoreType.DMA((n, 7)),
                                          pltpu.SemaphoreType.DMA((n,))])(*blocks)


def _routes_to_sibling(x, y, c):
    return [(2 * p + (1 - c), p, (x, y, 1 - c)) for p in range(4)]


def _routes_to_chips(x, y, c):
    return [(2 * px + py, j, (px, py, c)) for j, (px, py) in enumerate(_other_chips(x, y))]


def _routes_block_to_chips(x, y, c):
    me = 4 * x + 2 * y + c
    return [(me, me, (px, py, c)) for px, py in _other_chips(x, y)]


def _routes_blocks_to_sibling(x, y, c):
    return [(4 * px + 2 * py + c, 4 * px + 2 * py + c, (x, y, 1 - c)) for px, py in [(x, y)] + _other_chips(x, y)]


def _route_copies(routes, src_refs, land_refs, send_sems, recv_sems):
    x, y, c = _place()
    copies = []
    for a, (src, land) in enumerate(zip(src_refs, land_refs)):
        plan = routes(x, y, c)
        for k, (s, d, target) in enumerate(plan):
            i = a * len(plan) + k
            copies.append(pltpu.make_async_remote_copy(src_ref=src.at[s], dst_ref=land.at[d], send_sem=send_sems.at[i],
                                                       recv_sem=recv_sems.at[i], device_id=target, device_id_type=MESH))
    return copies


def _exchange(name, routes, n_routes, srcs, land_slots):
    n = len(srcs)

    def body(*refs):
        copies = _route_copies(routes, refs[:n], refs[n:2 * n], refs[2 * n], refs[2 * n + 1])
        for cp in copies:
            cp.start()
        for cp in copies:
            cp.wait_recv()
        for cp in copies:
            cp.wait_send()

    return pl.pallas_call(body, name=name, in_specs=[ANY] * n, out_specs=[ANY] * n,
                          out_shape=[jax.ShapeDtypeStruct((land_slots,) + s.shape[1:], s.dtype) for s in srcs],
                          scratch_shapes=[pltpu.SemaphoreType.DMA((n * n_routes,)), pltpu.SemaphoreType.DMA((n * n_routes,))])(*srcs)


HBM_SPEC = pl.BlockSpec(memory_space=pltpu.HBM)
SEM_SPEC = pl.BlockSpec(memory_space=pltpu.SEMAPHORE)
DATAFLOW = pltpu.SideEffectType.DATAFLOW_SIDE_EFFECTING


def _exchange_start(name, routes, n_routes, srcs, lands, after=None):
    n = len(srcs)
    in_place = lands is None
    bufs = list(srcs) + ([] if in_place else list(lands))
    nb = len(bufs)
    extra = [] if after is None else [after]

    def body(*refs):
        src_refs = refs[:n]
        land_refs = src_refs if in_place else refs[n:nb]
        send_sems, recv_sems = refs[nb + len(extra)], refs[nb + len(extra) + 1]
        token = refs[-1]
        for cp in _route_copies(routes, src_refs, land_refs, send_sems, recv_sems):
            cp.start()
        token[...] = jnp.zeros_like(token)

    sems = [pltpu.SemaphoreType.DMA((n * n_routes,)), pltpu.SemaphoreType.DMA((n * n_routes,))]
    out = pl.pallas_call(
        body, name=name, in_specs=[HBM_SPEC] * nb + [ANY] * len(extra),
        out_shape=sems + [pltpu.HBM(b.shape, b.dtype) for b in bufs] + [jax.ShapeDtypeStruct((8, LANES), F32)],
        out_specs=[SEM_SPEC, SEM_SPEC] + [HBM_SPEC] * nb + [pl.BlockSpec(memory_space=pltpu.VMEM)],
        input_output_aliases={i: 2 + i for i in range(nb)},
        compiler_params=pltpu.CompilerParams(has_side_effects=DATAFLOW))(
        *[pltpu.with_memory_space_constraint(b, pltpu.HBM) for b in bufs], *extra)
    return (out[0], out[1], list(out[2:2 + nb])), out[-1]


def _exchange_wait(name, routes, n_routes, n, started, after):
    send_sems, recv_sems, bufs = started
    nb = len(bufs)
    in_place = nb == n

    def body(*refs):
        src_refs = refs[:n]
        land_refs = src_refs if in_place else refs[n:nb]
        for cp in _route_copies(routes, src_refs, land_refs, refs[nb], refs[nb + 1]):
            cp.wait_send()
            cp.wait_recv()

    out = pl.pallas_call(
        body, name=name, in_specs=[HBM_SPEC] * nb + [SEM_SPEC, SEM_SPEC, ANY],
        out_shape=[pltpu.HBM(b.shape, b.dtype) for b in bufs], out_specs=[HBM_SPEC] * nb,
        input_output_aliases={i: i for i in range(nb)},
        compiler_params=pltpu.CompilerParams(has_side_effects=DATAFLOW))(*bufs, send_sems, recv_sems, after)
    return list(out[:n]) if in_place else list(out[n:])


def _pair_sums(tag, gs, from_sibling):
    x, y, c = _place()
    return [_pair_add(f"rs_add_{tag}_{i}", g, o, c, 2 * x + y) for i, (g, o) in enumerate(zip(gs, from_sibling))]


def _reduce_scatter(tag, gs):
    sums = _pair_sums(tag, gs, _exchange(f"rs_swap_{tag}", _routes_to_sibling, 4, gs, 4))
    got = _exchange(f"rs_chips_{tag}", _routes_to_chips, 3, [s[0] for s in sums], 3)
    return [_sum4(f"rs_sum_{tag}_{i}", s[1], q) for i, (s, q) in enumerate(zip(sums, got))]


def _reduce_scatter_begin(tag, gs):
    lands = [lax.empty((4,) + g.shape[1:], g.dtype) for g in gs]
    swap, token = _exchange_start(f"rs_swap_{tag}_start", _routes_to_sibling, 4, gs, lands)
    return dict(tag=tag, gs=gs, swap=swap), token


def _reduce_scatter_middle(state, after):
    tag, gs = state["tag"], state["gs"]
    from_sibling = _exchange_wait(f"rs_swap_{tag}_wait", _routes_to_sibling, 4, len(gs), state["swap"], after)
    state["sums"] = _pair_sums(tag, gs, from_sibling)
    partials = [s[0] for s in state["sums"]]
    lands = [lax.empty((3,) + p.shape[1:], p.dtype) for p in partials]
    state["chips"], token = _exchange_start(f"rs_chips_{tag}_start", _routes_to_chips, 3, partials, lands)
    return token


def _reduce_scatter_end(state, after):
    tag = state["tag"]
    got = _exchange_wait(f"rs_chips_{tag}_wait", _routes_to_chips, 3, len(state["gs"]), state["chips"], after)
    return [_sum4(f"rs_sum_{tag}_{i}", s[1], q) for i, (s, q) in enumerate(zip(state["sums"], got))]


def _all_gather_begin(tag, blocks, after):
    dev = 4 * lax.axis_index("x") + 2 * lax.axis_index("y") + lax.axis_index("c")
    zones = [lax.dynamic_update_slice_in_dim(lax.empty((N_DEV,) + b.shape, b.dtype), b[None], dev, axis=0) for b in blocks]
    chips, token = _exchange_start(f"gather_{tag}_chips_start", _routes_block_to_chips, 3, zones, None, after)
    return dict(tag=tag, n=len(blocks), chips=chips), token


def _all_gather_middle(state, after):
    tag, n = state["tag"], state["n"]
    zones = _exchange_wait(f"gather_{tag}_chips_wait", _routes_block_to_chips, 3, n, state["chips"], after)
    state["sibling"], token = _exchange_start(f"gather_{tag}_sibling_start", _routes_blocks_to_sibling, 4, zones, None)
    return token


def _all_gather_end(state, after):
    return _exchange_wait(f"gather_{state['tag']}_sibling_wait", _routes_blocks_to_sibling, 4, state["n"], state["sibling"], after)


def _flat_rows(n_elems):
    return -(-n_elems // (FLAT_W * 16)) * 16


def _pack(arrays, dtype):
    flat = jnp.concatenate([a.reshape(-1).astype(dtype) for a in arrays])
    rows = _flat_rows(flat.shape[0])
    flat = jnp.pad(flat, (0, rows * FLAT_W - flat.shape[0]))
    return flat.reshape(rows, FLAT_W)


def _unpack(flat, shapes, lead=()):
    flat = flat.reshape(lead + (-1,))
    out, pos = [], 0
    for s in shapes:
        n = math.prod(s)
        out.append(flat[..., pos:pos + n].reshape(lead + tuple(s)))
        pos += n
    return out


def _ffn_pad_rows(a):
    n = a.shape[0] // FFN_HALF
    a = jnp.pad(a.reshape(n, FFN_HALF, a.shape[1]), ((0, 0), (0, FFN_HALF_PAD - FFN_HALF), (0, 0)))
    return a.reshape(n * FFN_HALF_PAD, a.shape[2])


def _ffn_unpad_rows(a):
    n = a.shape[0] // FFN_HALF_PAD
    return a.reshape(n, FFN_HALF_PAD, a.shape[1])[:, :FFN_HALF].reshape(n * FFN_HALF, a.shape[1])


def _ffn_pad_cols(a):
    n = a.shape[1] // FFN_HALF
    a = jnp.pad(a.reshape(a.shape[0], n, FFN_HALF), ((0, 0), (0, 0), (0, FFN_HALF_PAD - FFN_HALF)))
    return a.reshape(a.shape[0], n * FFN_HALF_PAD)


def _ffn_unpad_cols(a):
    n = a.shape[1] // FFN_HALF_PAD
    return a.reshape(a.shape[0], n, FFN_HALF_PAD)[:, :, :FFN_HALF].reshape(a.shape[0], n * FFN_HALF)


def _shard_to_send(name, shard):
    if name == "w_in":
        shard = shard.T
    elif name == "ffn_w_up":
        shard = _ffn_pad_rows(shard.T)
    return shard.astype(MXU_DTYPE)


def _whole_from_gathered(name, g):
    if name == "w_in":
        return _pad_in_proj_rows(g.reshape(IN_DIM, g.shape[2]))
    if name in ("w_br_gdn", "w_br_gla"):
        return jnp.transpose(g, (1, 0, 2)).reshape(g.shape[1], N_DEV * g.shape[2])
    if name == "ffn_w_down":
        return jnp.pad(g, ((0, 0), (0, FFN_HALF_PAD - FFN_HALF), (0, 0))).reshape(FFN_PAD, g.shape[2])
    return g.reshape(N_DEV * g.shape[1], g.shape[2])


def _slots_from_whole(name, gw):
    if name == "w_in":
        return _unpad_in_proj_rows(gw).reshape(N_DEV, IN_DIM // N_DEV, gw.shape[1])
    if name in ("w_br_gdn", "w_br_gla"):
        return jnp.transpose(gw.reshape(gw.shape[0], N_DEV, gw.shape[1] // N_DEV), (1, 0, 2))
    return gw.reshape(N_DEV, gw.shape[0] // N_DEV, gw.shape[1])


def _shard_from_slot(name, s):
    if name == "ffn_w_up":
        return _ffn_unpad_rows(s)
    if name == "ffn_w_down":
        return s[:FFN_HALF]
    return s


def _in_proj_pieces():
    starts, pos = {}, 0
    for n, width in IN_SPLITS:
        starts[n] = (pos, width)
        pos += width
    return [(starts[ref][0], off + lane, starts[ref][1]) for _, off, _, pieces in PAD_SEGS for ref, lane in pieces]


def _pad_in_proj_rows(w):
    rows, at = [], 0
    for src, dst, n in sorted(_in_proj_pieces(), key=lambda p: p[1]):
        if dst > at:
            rows.append(jnp.zeros((dst - at, w.shape[1]), w.dtype))
        rows.append(w[src:src + n])
        at = dst + n
    rows.append(jnp.zeros((IN_PAD - at, w.shape[1]), w.dtype))
    return jnp.concatenate(rows, axis=0)


def _unpad_in_proj_rows(wp):
    return jnp.concatenate([wp[dst:dst + n] for _, dst, n in sorted(_in_proj_pieces())], axis=0)


def _lane_pad(a, width=LANES):
    return jnp.pad(a, ((0, 0), (0, width - a.shape[1])))


def _seg_blk(h, name, rows):
    off, width = SEG[name]
    return (h, rows, width, off // width)


def _ln_both(xs_, ps_):
    (y,) = _ln_fn(xs_, ps_)
    return (y, y)


def _behind(param, hooks, stage, *seen):
    if hooks is None or stage not in hooks:
        return param
    token = hooks[stage](*seen)
    return param if token is None else param + token[0:1, 0:1]


def _layer_fwd(l, x, x_mx, W, sp, hooks=None):
    T = x.shape[0]
    n64, ngla, ntok = T // SSD_CHUNK, T // GLA_BLOCK, T // 256
    h = _mm(f"in_proj_{l}", x_mx, W["w_in"], "nt")
    xbc = _conv_silu_fwd(f"ssd_conv_{l}", h, SEG["xbc"][0], sp["ssd_conv_w"], sp["ssd_conv_b"])
    gqkv = _conv_silu_fwd(f"gdn_conv_{l}", h, SEG["gqkv"][0], sp["gdn_conv_w"], None)

    ssd_in = [(xbc, SSD_CHUNK, SSD_XBC, 0), _seg_blk(h, "dt", SSD_CHUNK), _seg_blk(h, "z", SSD_CHUNK)]
    ssd_p = [sp["ssd_dt_bias"], sp["ssd_a_log"], sp["ssd_d"], sp["ssd_norm_w"]]
    o_ssd, ssd_states = _chain_fwd(f"ssd_fwd_{l}", _ssd_chunk, n64, ssd_in, ssd_p, [(SSD_CHUNK, SSD_INNER, MXU_DTYPE)],
                                   (SSD_STATE, SSD_INNER))
    o_gdn, gdn_saved = _gdn_forward(str(l), gqkv, h, dict(sp, gdn_a_log=_behind(sp["gdn_a_log"], hooks, "ssd", o_ssd)))
    gla_in = [_seg_blk(h, "lqkv", GLA_BLOCK), _seg_blk(h, "lglr", GLA_BLOCK), _seg_blk(h, "lr", GLA_BLOCK)]
    gla_p = [jnp.pad(sp["gla_gate_w2"], ((0, LANES - GLA_RANK), (0, 0))), sp["gla_gate_b"], sp["gla_norm_w"]]
    o_gla, gla_states = _chain_fwd(f"gla_fwd_{l}", _gla_block, ngla, gla_in, gla_p, [(GLA_BLOCK, GLA_V, MXU_DTYPE)],
                                   (GLA_VAL_DIM, GLA_K))
    ln1_p = [_behind(sp["ln1_g"], hooks, "mixed", o_gdn), sp["ln1_b"]]
    y_ssd = _mm(f"br_ssd_{l}", o_ssd, W["w_br_ssd"])
    y_gdn = _mm(f"br_gdn_{l}", o_gdn, W["w_br_gdn"])
    y_gla = _mm(f"br_gla_{l}", o_gla, W["w_br_gla"])
    merge_in = [_seg_blk(h, "gates", 256), (y_ssd, 256, D_MODEL, 0), (y_gdn, 256, D_MODEL, 0), (y_gla, 256, D_MODEL, 0)]
    (mix,) = _chain_fwd(f"merge_{l}", _merge_fn, ntok, merge_in, [], [(256, D_MODEL, MXU_DTYPE)])
    r1 = _mm(f"out_proj_{l}", mix, W["w_out"])
    both = [(256, D_MODEL, F32), (256, D_MODEL, MXU_DTYPE)]
    x1, x1_mx = _chain_fwd(f"ln1_{l}", _ln_both, ntok, [(x, 256, D_MODEL, 0), (r1, 256, D_MODEL, 0)], ln1_p, both)
    up = _mm(f"ffn_up_{l}", x1_mx, W["ffn_w_up"], "nt")
    act = _ffn_glu_fwd(f"ffn_glu_{l}", up, sp["ffn_conv_w_pad"], sp["ffn_conv_b_pad"], MXU_DTYPE)
    ln2_p = [_behind(sp["ln2_g"], hooks, "ffn_act", act), sp["ln2_b"]]
    r2 = _mm(f"ffn_down_{l}", act, W["ffn_w_down"])
    x2, x2_mx = _chain_fwd(f"ln2_{l}", _ln_both, ntok, [(x1, 256, D_MODEL, 0), (r2, 256, D_MODEL, 0)], ln2_p, both)
    saved = dict(x=x, x_mx=x_mx, h=h, xbc=xbc, gqkv=gqkv, ssd_in=ssd_in, ssd_p=ssd_p, ssd_states=ssd_states,
                 gdn=gdn_saved, gla_in=gla_in, gla_p=gla_p, gla_states=gla_states, o_ssd=o_ssd,
                 o_gdn=o_gdn, o_gla=o_gla, merge_in=merge_in, mix=mix, r1=r1, ln1_p=ln1_p, x1=x1, x1_mx=x1_mx, up=up, act=act,
                 r2=r2, ln2_p=ln2_p)
    return x2, x2_mx, saved


def _layer_bwd(l, dx2, W, sp, sv, hooks=None):
    T = dx2.shape[0]
    n64, ngla, ntok = T // SSD_CHUNK, T // GLA_BLOCK, T // 256
    bf = MXU_DTYPE
    gw, gs = {}, {}
    ln2_p = [_behind(sv["ln2_p"][0], hooks, "start"), sv["ln2_p"][1]]
    (dx1_a, dr2), (gs["ln2_g"], gs["ln2_b"]) = _chain_bwd(
        f"ln2_bwd_{l}", _ln_fn, ntok, [(sv["x1"], 256, D_MODEL, 0), (sv["r2"], 256, D_MODEL, 0)], ln2_p,
        [(dx2, 256, D_MODEL)], dx_dtypes=[F32, bf])
    gw["ffn_w_down"] = _mm(f"ffn_down_dw_{l}", sv["act"], dr2, "tn")
    dact = _mm(f"ffn_down_dx_{l}", dr2, W["ffn_w_down"], "nt")
    dg, du, dwg, dwu, dbg, dbu = _ffn_glu_bwd(f"ffn_glu_bwd_{l}", sv["up"], sp["ffn_conv_w_pad"], sp["ffn_conv_b_pad"], dact, bf)
    gs["ffn_conv_w"] = _ffn_unpad_cols(jnp.concatenate([dwg, dwu], axis=1))
    gs["ffn_conv_b"] = _ffn_unpad_cols(jnp.concatenate([dbg, dbu], axis=1))
    dup = jnp.concatenate([dg, du], axis=1)
    gw["ffn_w_up"] = _mm(f"ffn_up_dw_{l}", dup, sv["x1_mx"], "tn", tn=1024)
    dx1_b = _mm(f"ffn_up_dx_{l}", dup, W["ffn_w_up"], "nn", tn=1024, tk=1024)
    ln1_p = [_behind(sv["ln1_p"][0], hooks, "ffn", dx1_b), sv["ln1_p"][1]]
    (dx_a, dr1), (gs["ln1_g"], gs["ln1_b"]) = _chain_bwd(
        f"ln1_bwd_{l}", _ln_sum_fn, ntok, [(sv["x"], 256, D_MODEL, 0), (sv["r1"], 256, D_MODEL, 0)], ln1_p,
        [(dx1_a, 256, D_MODEL), (dx1_b, 256, D_MODEL)], dx_dtypes=[F32, bf])
    gw["w_out"] = _mm(f"out_proj_dw_{l}", sv["mix"], dr1, "tn")
    dmix = _mm(f"out_proj_dx_{l}", dr1, W["w_out"], "nt")
    (dgates, dy_ssd, dy_gdn, dy_gla), _ = _chain_bwd(f"merge_bwd_{l}", _merge_fn, ntok, sv["merge_in"], [],
                                                     [(dmix, 256, D_MODEL)], dx_dtypes=[bf, bf, bf, bf])
    gw["w_br_ssd"] = _mm(f"br_ssd_dw_{l}", sv["o_ssd"], dy_ssd, "tn")
    gw["w_br_gdn"] = _mm(f"br_gdn_dw_{l}", sv["o_gdn"], dy_gdn, "tn")
    gw["w_br_gla"] = _mm(f"br_gla_dw_{l}", sv["o_gla"], dy_gla, "tn")
    do_ssd = _mm(f"br_ssd_dx_{l}", dy_ssd, W["w_br_ssd"], "nt")
    do_gdn = _mm(f"br_gdn_dx_{l}", dy_gdn, W["w_br_gdn"], "nt")
    do_gla = _mm(f"br_gla_dx_{l}", dy_gla, W["w_br_gla"], "nt")

    ssd_p = [_behind(sv["ssd_p"][0], hooks, "branches", do_gla, gw)] + list(sv["ssd_p"][1:])
    (dxbc, ddt, dz), dps = _chain_bwd(f"ssd_bwd_{l}", _ssd_chunk, n64, sv["ssd_in"], ssd_p,
                                      [(do_ssd, SSD_CHUNK, SSD_INNER)], sprev=sv["ssd_states"], dx_dtypes=[F32, bf, bf])
    gs["ssd_dt_bias"], gs["ssd_a_log"], gs["ssd_d"], gs["ssd_norm_w"] = dps
    gdn_sv = dict(sv["gdn"], scan_p=[_behind(sv["gdn"]["scan_p"][0], hooks, "ssd", dz)])
    dgqkv, dgab, dgg, gs["gdn_a_log"], gs["gdn_dt_bias"], gs["gdn_norm_w"] = _gdn_backward(str(l), do_gdn, gdn_sv, bf)
    (dlqkv, dlglr, dlr), dps = _chain_bwd(f"gla_bwd_{l}", _gla_block, ngla, sv["gla_in"], sv["gla_p"],
                                          [(do_gla, GLA_BLOCK, GLA_V)], sprev=sv["gla_states"], dx_dtypes=[bf, bf, bf])
    gs["gla_gate_w2"], gs["gla_gate_b"], gs["gla_norm_w"] = dps[0][:GLA_RANK], dps[1], dps[2]
    dxbc_pre, gs["ssd_conv_w"], gs["ssd_conv_b"] = _conv_silu_bwd(
        f"ssd_conv_bwd_{l}", sv["h"], SEG["xbc"][0], sp["ssd_conv_w"], sp["ssd_conv_b"], dxbc, bf)
    dgqkv_pre, gs["gdn_conv_w"] = _conv_silu_bwd(f"gdn_conv_bwd_{l}", sv["h"], SEG["gqkv"][0], sp["gdn_conv_w"], None, dgqkv, bf)
    pieces = dict(gates=dgates, xbc=dxbc_pre, gqkv=dgqkv_pre, z=dz, lqkv=dlqkv, gg=dgg, lr=dlr, dt=ddt, gab=dgab, lglr=dlglr)
    cols = [pieces[name] for name, _, _, _ in PAD_SEGS]
    cols.append(jnp.zeros((T, IN_PAD - PAD_SEGS[-1][1] - PAD_SEGS[-1][2]), bf))
    dh = jnp.concatenate(cols, axis=1)
    gw["w_in"] = _mm(f"in_proj_dw_{l}", dh, sv["x_mx"], "tn", tn=1024)
    behind = hooks["w_in_grad"](gw) if hooks is not None and "w_in_grad" in hooks else None
    dx_b = _mm(f"in_proj_dx_{l}", dh, W["w_in"], "nn", tm=1024, tn=1024, tk=IN_PAD // 4, after=behind)
    dx = _add_blocks(f"dx_add_{l}", dx_a[None], dx_b[None])[0]
    return dx, gw, gs


def _ln_sum_fn(xs_, ps_):
    (y,) = _ln_fn(xs_, ps_)
    return (y, y)


def _small_2d(name, a):
    return a.reshape(1, -1) if a.ndim == 1 else a


def kernel(x, w_in, ssd_conv_w, ssd_conv_b, ssd_dt_bias, ssd_a_log, ssd_d, ssd_norm_w, gdn_conv_w, gdn_a_log, gdn_dt_bias, gdn_norm_w, gla_gate_w2, gla_gate_b, gla_norm_w, w_br_ssd, w_br_gdn, w_br_gla, w_out, ln1_g, ln1_b, ffn_w_up, ffn_conv_w, ffn_conv_b, ffn_w_down, ln2_g, ln2_b, loss_target, m_w_in, m_ssd_conv_w, m_ssd_conv_b, m_ssd_dt_bias, m_ssd_a_log, m_ssd_d, m_ssd_norm_w, m_gdn_conv_w, m_gdn_a_log, m_gdn_dt_bias, m_gdn_norm_w, m_gla_gate_w2, m_gla_gate_b, m_gla_norm_w, m_w_br_ssd, m_w_br_gdn, m_w_br_gla, m_w_out, m_ln1_g, m_ln1_b, m_ffn_w_up, m_ffn_conv_w, m_ffn_conv_b, m_ffn_w_down, m_ln2_g, m_ln2_b, v_w_in, v_ssd_conv_w, v_ssd_conv_b, v_ssd_dt_bias, v_ssd_a_log, v_ssd_d, v_ssd_norm_w, v_gdn_conv_w, v_gdn_a_log, v_gdn_dt_bias, v_gdn_norm_w, v_gla_gate_w2, v_gla_gate_b, v_gla_norm_w, v_w_br_ssd, v_w_br_gdn, v_w_br_gla, v_w_out, v_ln1_g, v_ln1_b, v_ffn_w_up, v_ffn_conv_w, v_ffn_conv_b, v_ffn_w_down, v_ln2_g, v_ln2_b):
    args = locals()
    w = {n: args[n] for n in WEIGHTS}
    m = {n: args["m_" + n] for n in WEIGHTS}
    v = {n: args["v_" + n] for n in WEIGHTS}
    dev = 4 * lax.axis_index("x") + 2 * lax.axis_index("y") + lax.axis_index("c")
    xl = x[0]
    tgt = loss_target[0]

    late = BIG[1:]

    def send(names, l):
        return [_shard_to_send(n, w[n][l]) for n in names]

    def whole_weights(names, got):
        return {n: _whole_from_gathered(n, g) for n, g in zip(names, got)}

    got0 = _all_gather("gather_first", send(BIG[:1], 0) + [w[n] for n in SMALL_SHARDED])
    gather0, token0 = _all_gather_begin("w_0", send(late, 0), got0[0])
    W = [whole_weights(BIG[:1], got0[:1]), None]
    whole = dict(w)
    for n, s in zip(SMALL_SHARDED, got0[1:]):
        whole[n] = jnp.transpose(s, (1, 2, 0, 3)).reshape(s.shape[1], s.shape[2], N_DEV * s.shape[3])
    SP = [{n: _small_2d(n, whole[n][l]) for n in SMALL} for l in range(DEPTH)]
    for sp in SP:
        sp["ffn_conv_w_pad"] = _ffn_pad_cols(sp["ffn_conv_w"])
        sp["ffn_conv_b_pad"] = _ffn_pad_cols(sp["ffn_conv_b"])

    held = {}

    def late_weights_cross(o_ssd):
        token = _all_gather_middle(gather0, o_ssd)
        held["gather1"], token1 = _all_gather_begin("w_1", send(BIG, 1), o_ssd)
        return token + token1

    def late_weights_arrive(mixed):
        W[0].update(whole_weights(late, _all_gather_end(gather0, mixed)))

    fwd_hooks = {"ssd": late_weights_cross, "mixed": late_weights_arrive,
                 "ffn_act": lambda act: _all_gather_middle(held["gather1"], act)}
    saved = [None] * DEPTH
    act, act_mx, saved[0] = _layer_fwd(0, xl, (xl + token0[0, 0]).astype(MXU_DTYPE), W[0], SP[0], hooks=fwd_hooks)
    W[1] = whole_weights(BIG, _all_gather_end(held["gather1"], act))
    act, act_mx, saved[1] = _layer_fwd(1, act, act_mx, W[1], SP[1])
    dy, loss_parts = _loss_head(act, tgt)
    loss = lax.psum(jnp.sum(loss_parts), ("x", "y", "c"))

    def slots_of(names, gw):
        return [_slots_from_whole(n, gw[n]) for n in names]

    grads = {}
    GS = [None] * DEPTH
    dy, gw, GS[1] = _layer_bwd(1, dy, W[1], SP[1], saved[1])
    reduce1, reduce1_token = _reduce_scatter_begin("1", slots_of(BIG, gw))

    def late_grads_leave(seen, gw0):
        held["reduce0"], token = _reduce_scatter_begin("0", slots_of(late, gw0))
        return token

    def w_in_grad_leaves(gw0):
        held["reduce_first"], token = _reduce_scatter_begin("first", slots_of(BIG[:1], gw0))
        return token

    bwd_hooks = {"start": lambda: reduce1_token, "ffn": lambda seen: _reduce_scatter_middle(reduce1, seen),
                 "branches": late_grads_leave, "ssd": lambda seen: _reduce_scatter_middle(held["reduce0"], seen),
                 "w_in_grad": w_in_grad_leaves}
    dy, gw, GS[0] = _layer_bwd(0, dy, W[0], SP[0], saved[0], hooks=bwd_hooks)
    red1 = _reduce_scatter_end(reduce1, dy)
    red0_late = _reduce_scatter_end(held["reduce0"], dy)
    _reduce_scatter_middle(held["reduce_first"], red1[0])
    grad_x = dy[None]
    kept_t = ("w_in", "ffn_w_up")
    grads_k = {n: jnp.stack([_shard_from_slot(n, red0_late[i]), _shard_from_slot(n, red1[i + 1])]) for i, n in enumerate(late)}

    small_shapes = [whole[n].shape for n in SMALL]
    gs_flat = _pack([jnp.stack([GS[l][n].reshape(whole[n].shape[1:]) for l in range(DEPTH)]) for n in SMALL], F32)
    (gs_all,) = _all_gather("gather_small_grads", [gs_flat])

    def mine(n, a):
        if n in SMALL_SHARDED:
            cs = a.shape[-1] // N_DEV
            return lax.dynamic_slice_in_dim(a, dev * cs, cs, axis=a.ndim - 1)
        return a

    m_whole, v_whole = {}, {}
    for n in SMALL:
        if n in SMALL_SHARDED:
            cs = w[n].shape[-1]
            zeros = jnp.zeros(whole[n].shape, F32)
            m_whole[n] = lax.dynamic_update_slice_in_dim(zeros, m[n], dev * cs, axis=2)
            v_whole[n] = lax.dynamic_update_slice_in_dim(zeros, v[n], dev * cs, axis=2)
        else:
            m_whole[n], v_whole[n] = m[n], v[n]
    outs = _adamw_small(gs_all, _pack([whole[n] for n in SMALL], F32), _pack([m_whole[n] for n in SMALL], F32),
                        _pack([v_whole[n] for n in SMALL], F32))
    g_s, d_s, m_s, v_s = [_unpack(o, small_shapes) for o in outs]
    delta, new_m, new_v = {}, {}, {}
    for i, n in enumerate(SMALL):
        grads[n], delta[n], new_m[n], new_v[n] = mine(n, g_s[i]), mine(n, d_s[i]), mine(n, m_s[i]), mine(n, v_s[i])
    for n in late + BIG[:1]:
        if n == "w_in":
            done = sum(new_v[k].reshape(-1)[0:1] for k in late + SMALL[:1])
            (first0,) = _reduce_scatter_end(held["reduce_first"], done)
            grads_k[n] = jnp.stack([first0, red1[0]])
        view = (lambda a: jnp.transpose(a, (0, 2, 1))) if n in kept_t else (lambda a: a)
        outs = _adamw(f"adamw_{n}", view(w[n]), grads_k[n], view(m[n]), view(v[n]))
        grads[n], delta[n], new_m[n], new_v[n] = view(grads_k[n]), view(outs[0]), view(outs[1]), view(outs[2])

    return (loss, grad_x, *[grads[n] for n in WEIGHTS], *[delta[n] for n in WEIGHTS], *[new_m[n] for n in WEIGHTS],
            *[new_v[n] for n in WEIGHTS])
```

```python
import functools
import math

import jax
import jax.numpy as jnp
from jax import lax
from jax.experimental import pallas as pl
from jax.experimental.pallas import tpu as pltpu

F32 = jnp.float32
MXU_DTYPE = jnp.bfloat16
HI = lax.Precision.HIGHEST

N_DEV = 8
D_MODEL = 1024
DEPTH = 2
SSD_HEADS, SSD_HEAD_DIM, SSD_INNER, SSD_GROUPS, SSD_STATE, SSD_CHUNK = 16, 64, 1024, 2, 128, 64
SSD_XBC = SSD_INNER + 2 * SSD_GROUPS * SSD_STATE
GDN_HEADS, GDN_HEAD_DIM, GDN_WIDTH, GDN_CHUNK = 4, 128, 512, 64
GLA_HEADS, GLA_KEY_DIM, GLA_VAL_DIM, GLA_K, GLA_V, GLA_RANK, GLA_CHUNK = 4, 64, 128, 256, 512, 16, 16
GLA_BLOCK = 128
GLA_NORMALIZER = 16.0
FFN_DIM = 2816
FFN_HALF = FFN_DIM // 8
FFN_HALF_PAD = 384
FFN_UP_PAD = 16 * FFN_HALF_PAD
FFN_PAD = FFN_UP_PAD // 2
ALPHA = (2 * DEPTH) ** 0.25
LN_EPS = 1e-5
RMS_EPS = 1e-6
ADAM_LR, ADAM_B1, ADAM_B2, ADAM_EPS, ADAM_WD, ADAM_STEP = 0.001, 0.9, 0.999, 1e-08, 0.01, 10
LANES = 128
NEG_BIG = -1e30
VMEM_LIMIT = 56 * 1024 * 1024

IN_SPLITS = (("z", 1024), ("xbc", 1536), ("dt", 16), ("gqkv", 1536), ("ga", 4), ("gb", 4), ("gg", 512),
             ("lqkv", 1024), ("lglr", 16), ("lr", 512), ("gates", 3072))
IN_DIM = sum(w for _, w in IN_SPLITS)
PAD_SEGS = (("gates", 0, 3072, (("gates", 0),)), ("xbc", 3072, 1536, (("xbc", 0),)),
            ("gqkv", 4608, 1536, (("gqkv", 0),)), ("z", 6144, 1024, (("z", 0),)),
            ("lqkv", 7168, 1024, (("lqkv", 0),)), ("gg", 8192, 512, (("gg", 0),)), ("lr", 8704, 512, (("lr", 0),)),
            ("dt", 9216, 128, (("dt", 0),)), ("gab", 9344, 128, (("ga", 0), ("gb", 4))), ("lglr", 9472, 128, (("lglr", 0),)))
IN_PAD = 9728
SEG = {name: (off, width) for name, off, width, _ in PAD_SEGS}

BIG = ("w_in", "w_br_ssd", "w_br_gdn", "w_br_gla", "w_out", "ffn_w_up", "ffn_w_down")
COL_SHARDED = ("w_in", "w_br_gdn", "w_br_gla", "ffn_w_up")
SMALL_SHARDED = ("ssd_conv_w", "gdn_conv_w", "gla_gate_w2", "ffn_conv_w")
WEIGHTS = ("w_in", "ssd_conv_w", "ssd_conv_b", "ssd_dt_bias", "ssd_a_log", "ssd_d", "ssd_norm_w", "gdn_conv_w",
           "gdn_a_log", "gdn_dt_bias", "gdn_norm_w", "gla_gate_w2", "gla_gate_b", "gla_norm_w", "w_br_ssd", "w_br_gdn",
           "w_br_gla", "w_out", "ln1_g", "ln1_b", "ffn_w_up", "ffn_conv_w", "ffn_conv_b", "ffn_w_down", "ln2_g", "ln2_b")
SMALL = tuple(n for n in WEIGHTS if n not in BIG)
FLAT_W = 512


def _cparams(sem=None):
    kw = dict(vmem_limit_bytes=VMEM_LIMIT)
    if sem is not None:
        kw["dimension_semantics"] = sem
    return pltpu.CompilerParams(**kw)


_DIMS = {"nn": (((1,), (0,)), ((), ())), "nt": (((1,), (1,)), ((), ())), "tn": (((0,), (0,)), ((), ()))}


def _dot(a, b, dims="nn"):
    if MXU_DTYPE == F32:
        return lax.dot_general(a.astype(F32), b.astype(F32), _DIMS[dims], precision=HI, preferred_element_type=F32)
    return lax.dot_general(a.astype(MXU_DTYPE), b.astype(MXU_DTYPE), _DIMS[dims], preferred_element_type=F32)


def _dot_hi(a, b, dims="nn"):
    return lax.dot_general(a.astype(F32), b.astype(F32), _DIMS[dims], precision=HI, preferred_element_type=F32)


def _iota2(shape, axis):
    return lax.broadcasted_iota(jnp.int32, shape, axis)


def _tril(n, strict=False):
    r, c = _iota2((n, n), 0), _iota2((n, n), 1)
    return (r > c) if strict else (r >= c)


def _raw_dot(a, b, dims):
    return lax.dot_general(a, b, _DIMS[dims], preferred_element_type=F32)


def _dot_x3(a, b, dims="nn"):
    if MXU_DTYPE == F32:
        return _dot_hi(a, b, dims)
    ah, bh = a.astype(jnp.bfloat16), b.astype(jnp.bfloat16)
    al, bl = (a - ah.astype(F32)).astype(jnp.bfloat16), (b - bh.astype(F32)).astype(jnp.bfloat16)
    return _raw_dot(ah, bh, dims) + (_raw_dot(ah, bl, dims) + _raw_dot(al, bh, dims))


def _exact_dot(mask, b, dims, mask_first):
    if MXU_DTYPE == F32:
        return _dot_hi(mask, b, dims) if mask_first else _dot_hi(b, mask, dims)
    m = mask.astype(jnp.bfloat16)
    b1 = b.astype(jnp.bfloat16)
    r1 = b - b1.astype(F32)
    b2 = r1.astype(jnp.bfloat16)
    b3 = (r1 - b2.astype(F32)).astype(jnp.bfloat16)
    if mask_first:
        return _raw_dot(m, b1, dims) + (_raw_dot(m, b2, dims) + _raw_dot(m, b3, dims))
    return _raw_dot(b1, m, dims) + (_raw_dot(b2, m, dims) + _raw_dot(b3, m, dims))


@jax.custom_vjp
def _mask_left(mask, b):
    return _exact_dot(mask, b, "nn", True)


_mask_left.defvjp(lambda mask, b: (_mask_left(mask, b), mask),
                  lambda mask, d: (jnp.zeros_like(mask), _exact_dot(mask, d, "tn", True)))


@jax.custom_vjp
def _mask_right(a, mask):
    return _exact_dot(mask, a, "nn", False)


_mask_right.defvjp(lambda a, mask: (_mask_right(a, mask), mask),
                   lambda mask, d: (_exact_dot(mask, d, "nt", False), jnp.zeros_like(mask)))


@jax.custom_vjp
def _unit_lower_inverses(mats):
    n = mats[0].shape[0]
    eye = (_iota2((n, n), 0) == _iota2((n, n), 1)).astype(F32)
    xs = [eye - a for a in mats]
    ps = list(mats)
    k = 2
    while k < n:
        ps = [_dot_x3(p, p) for p in ps]
        xs = [x + _dot_x3(x, p) for x, p in zip(xs, ps)]
        k *= 2
    return xs


def _unit_lower_inverses_fwd(mats):
    ts = _unit_lower_inverses(mats)
    return ts, ts


def _unit_lower_inverses_bwd(ts, dts):
    mids = [_dot_x3(t, d, "tn") for t, d in zip(ts, dts)]
    return ([-_dot_x3(m, t, "nt") for m, t in zip(mids, ts)],)


_unit_lower_inverses.defvjp(_unit_lower_inverses_fwd, _unit_lower_inverses_bwd)


def _ssd_chunk(xs_, ps_, s_t):
    xbc, dtraw, z = xs_
    dt_bias, a_log, d_skip, norm_w = ps_
    L = xbc.shape[0]
    H, P, N, G = SSD_HEADS, SSD_HEAD_DIM, SSD_STATE, SSD_GROUPS
    W = SSD_INNER // G
    xs = xbc[:, :SSD_INNER]
    bm = xbc[:, SSD_INNER:SSD_INNER + G * N]
    cm = xbc[:, SSD_INNER + G * N:]
    dt = jax.nn.softplus(dtraw[:, :H] + dt_bias)
    a = dt * (-jnp.exp(a_log))
    causal = _tril(L)
    a_cs = _mask_left(causal.astype(F32), a)
    expand = (_iota2((H, SSD_INNER), 1) // P == _iota2((H, SSD_INNER), 0)).astype(F32)
    wide = _mask_right(jnp.concatenate([a_cs, dt, jnp.broadcast_to(d_skip, (L, H))], axis=0), expand)
    a_cs_x, dt_x, d_x = wide[:L], wide[L:2 * L], wide[2 * L:]
    a_end_x = a_cs_x[L - 1:L, :]
    a_cs_t, dt_t = a_cs.T, dt.T
    cb = [_dot(cm[:, g * N:(g + 1) * N], bm[:, g * N:(g + 1) * N], "nt") for g in range(G)]
    decay = [jnp.exp(jnp.where(causal, a_cs[:, h:h + 1] - a_cs_t[h:h + 1, :], NEG_BIG)) * dt_t[h:h + 1, :] for h in range(H)]
    ws = [cb[h // (H // G)] * decay[h] for h in range(H)]
    y = jnp.concatenate([_dot(ws[h], xs[:, h * P:(h + 1) * P]) for h in range(H)], axis=1)
    y_in = jnp.concatenate([_dot(cm[:, g * N:(g + 1) * N], s_t[:, g * W:(g + 1) * W]) for g in range(G)], axis=1)
    y = y + y_in * jnp.exp(a_cs_x) + d_x * xs
    xw = xs * (jnp.exp(a_end_x - a_cs_x) * dt_x)
    st = jnp.concatenate([_dot(bm[:, g * N:(g + 1) * N], xw[:, g * W:(g + 1) * W], "tn") for g in range(G)], axis=1)
    s_new = s_t * jnp.exp(a_end_x) + st
    yg = y * jax.nn.silu(z)
    outs = []
    for g in range(G):
        part = yg[:, g * W:(g + 1) * W]
        outs.append(part * lax.rsqrt(jnp.mean(part * part, axis=1, keepdims=True) + RMS_EPS))
    return (jnp.concatenate(outs, axis=1) * norm_w,), s_new


GDN_PREP_CHUNKS = 4


def _gdn_prep(xs_, ps_):
    qkv, ab = xs_
    a_log, dt_bias = ps_
    B = qkv.shape[0]
    H, D, L = GDN_HEADS, GDN_HEAD_DIM, GDN_CHUNK
    g_all = -jnp.exp(a_log) * jax.nn.softplus(ab + dt_bias)
    row, col = _iota2((B, B), 0), _iota2((B, B), 1)
    g_cs = _mask_left((((row // L) == (col // L)) & (row >= col)).astype(F32), g_all)
    g_cs_t = g_cs.T
    beta_all = jax.nn.sigmoid(ab)
    incl, strict = _tril(L), _tril(L, strict=True)
    qs, ks, vs = [], [], []
    for h in range(H):
        q = qkv[:, h * D:(h + 1) * D]
        k = qkv[:, GDN_WIDTH + h * D:GDN_WIDTH + (h + 1) * D]
        qs.append(q * lax.rsqrt(jnp.sum(q * q, axis=1, keepdims=True) + RMS_EPS) * (D ** -0.5))
        ks.append(k * lax.rsqrt(jnp.sum(k * k, axis=1, keepdims=True) + RMS_EPS))
        vs.append(qkv[:, 2 * GDN_WIDTH + h * D:2 * GDN_WIDTH + (h + 1) * D])
    pairs = [(c, h) for c in range(B // L) for h in range(H)]
    rows = {c: slice(c * L, (c + 1) * L) for c in range(B // L)}
    q_ = {(c, h): qs[h][rows[c]] for c, h in pairs}
    k_ = {(c, h): ks[h][rows[c]] for c, h in pairs}
    col_ = {(c, h): g_cs[rows[c], h:h + 1] for c, h in pairs}
    beta_ = {(c, h): beta_all[rows[c], H + h:H + h + 1] for c, h in pairs}
    gamma = {p: jnp.exp(jnp.where(incl, col_[p] - g_cs_t[p[1]:p[1] + 1, rows[p[0]]], NEG_BIG)) for p in pairs}
    kb = {p: k_[p] * beta_[p] for p in pairs}
    a_mat = [jnp.where(strict, _dot(kb[p], k_[p], "nt") * gamma[p], 0.0) for p in pairs]
    attn = {p: jnp.where(incl, _dot(q_[p], k_[p], "nt") * gamma[p], 0.0) for p in pairs}
    t_mat = dict(zip(pairs, _unit_lower_inverses(a_mat)))
    u = {p: _dot(t_mat[p], vs[p[1]][rows[p[0]]] * beta_[p]) for p in pairs}
    w = {p: _dot(t_mat[p], kb[p] * jnp.exp(col_[p])) for p in pairs}
    qd = {p: q_[p] * jnp.exp(col_[p]) for p in pairs}
    kd = {p: k_[p] * jnp.exp(col_[p][L - 1:L, :] - col_[p]) for p in pairs}

    def whole(parts):
        return jnp.concatenate([jnp.concatenate([parts[(c, h)] for h in range(H)], axis=1) for c in range(B // L)], axis=0)

    return (whole(u), whole(w), whole(qd), whole(kd), whole(attn), g_cs)


def _gdn_scan(xs_, ps_, s):
    u, w, qd, kd, attn, g_cs, gate = xs_
    (norm_w,) = ps_
    L = u.shape[0]
    H, D = GDN_HEADS, GDN_HEAD_DIM
    heads = range(H)
    lanes = [slice(h * D, (h + 1) * D) for h in heads]
    s_h = [s[lanes[h], :] for h in heads]
    v_new = [u[:, lanes[h]] - _dot(w[:, lanes[h]], s_h[h]) for h in heads]
    o = [_dot(qd[:, lanes[h]], s_h[h]) + _dot(attn[:, h * L:(h + 1) * L], v_new[h]) for h in heads]
    decay = [jnp.exp(g_cs[L - 1:L, h:h + 1]) for h in heads]
    s_new = [s_h[h] * decay[h] + _dot(kd[:, lanes[h]], v_new[h], "tn") for h in heads]
    o = [o[h] * lax.rsqrt(jnp.mean(o[h] * o[h], axis=1, keepdims=True) + RMS_EPS) * norm_w * jax.nn.silu(gate[:, lanes[h]])
         for h in heads]
    return (jnp.concatenate(o, axis=1),), jnp.concatenate(s_new, axis=0)


def _gdn_forward(tag, gqkv, h, sp):
    T = gqkv.shape[0]
    blk = GDN_PREP_CHUNKS * GDN_CHUNK
    prep_in = [(gqkv, blk, 3 * GDN_WIDTH, 0), _seg_blk(h, "gab", blk)]
    prep_p = [_lane_pad(sp["gdn_a_log"]), _lane_pad(sp["gdn_dt_bias"])]
    mx = MXU_DTYPE
    prep = _chain_fwd(f"gdn_prep_{tag}", _gdn_prep, T // blk, prep_in, prep_p,
                      [(blk, GDN_WIDTH, F32), (blk, GDN_WIDTH, mx), (blk, GDN_WIDTH, mx), (blk, GDN_WIDTH, mx),
                       (blk, GDN_HEADS * GDN_CHUNK, mx), (blk, LANES, F32)])
    widths = [GDN_WIDTH] * 4 + [GDN_HEADS * GDN_CHUNK, LANES]
    scan_in = [(a, GDN_CHUNK, wd, 0) for a, wd in zip(prep, widths)] + [_seg_blk(h, "gg", GDN_CHUNK)]
    scan_p = [sp["gdn_norm_w"]]
    o, states = _chain_fwd(f"gdn_scan_{tag}", _gdn_scan, T // GDN_CHUNK, scan_in, scan_p, [(GDN_CHUNK, GDN_WIDTH, mx)],
                           (GDN_WIDTH, GDN_HEAD_DIM))
    return o, dict(prep_in=prep_in, prep_p=prep_p, scan_in=scan_in, scan_p=scan_p, states=states, widths=widths)


def _gdn_backward(tag, do, sv, dx_dtype):
    T = do.shape[0]
    blk = GDN_PREP_CHUNKS * GDN_CHUNK
    dscan, (dnorm,) = _chain_bwd(f"gdn_scan_bwd_{tag}", _gdn_scan, T // GDN_CHUNK, sv["scan_in"], sv["scan_p"],
                                 [(do, GDN_CHUNK, GDN_WIDTH)], sprev=sv["states"], dx_dtypes=[F32] * 6 + [dx_dtype])
    douts = [(d, blk, wd) for d, wd in zip(dscan[:6], sv["widths"])]
    (dgqkv, dgab), (da_log, ddt_bias) = _chain_bwd(f"gdn_prep_bwd_{tag}", _gdn_prep, T // blk, sv["prep_in"], sv["prep_p"],
                                                   douts, dx_dtypes=[F32, dx_dtype])
    return dgqkv, dgab, dscan[6], da_log[:, :GDN_HEADS], ddt_bias[:, :GDN_HEADS], dnorm


def _gla_block(xs_, ps_, s_t):
    qkv, glr, r = xs_
    w2, gate_b, norm_w = ps_
    B = qkv.shape[0]
    H, K, V, C = GLA_HEADS, GLA_KEY_DIM, GLA_VAL_DIM, GLA_CHUNK
    q = qkv[:, :GLA_K] * (K ** -0.5)
    k = qkv[:, GLA_K:2 * GLA_K]
    v = qkv[:, 2 * GLA_K:]
    gk = jax.nn.log_sigmoid(_dot(glr, w2) + gate_b) / GLA_NORMALIZER
    row, col = _iota2((B, B), 0), _iota2((B, B), 1)
    same = (row // C) == (col // C)
    mask = same & (row >= col)
    b_cs = _mask_left(mask.astype(F32), gk)
    b_end = _mask_left((col == (row // C) * C + (C - 1)).astype(F32), b_cs)
    q_e = q * jnp.exp(b_cs)
    k_e = k * jnp.exp(-b_cs)
    k_d = k * jnp.exp(b_end - b_cs)
    intra = []
    for h in range(H):
        a_mat = jnp.where(mask, _dot(q_e[:, h * K:(h + 1) * K], k_e[:, h * K:(h + 1) * K], "nt"), 0.0)
        intra.append(_dot(a_mat, v[:, h * V:(h + 1) * V]))
    o = jnp.concatenate(intra, axis=1)
    chunks = [slice(j * C, (j + 1) * C) for j in range(B // C)]
    fresh = [jnp.concatenate([_dot(v[sl, h * V:(h + 1) * V], k_d[sl, h * K:(h + 1) * K], "tn") for h in range(H)], axis=1)
             for sl in chunks]
    entering = []
    for j, sl in enumerate(chunks):
        entering.append(s_t)
        s_t = s_t * jnp.exp(b_end[j * C:j * C + 1, :]) + fresh[j]
    inter = [jnp.concatenate([_dot(q_e[sl, h * K:(h + 1) * K], entering[j][:, h * K:(h + 1) * K], "nt") for h in range(H)],
                             axis=1) for j, sl in enumerate(chunks)]
    o = o + jnp.concatenate(inter, axis=0)
    outs = []
    for h in range(H):
        oh = o[:, h * V:(h + 1) * V]
        oh = oh * lax.rsqrt(jnp.mean(oh * oh, axis=1, keepdims=True) + RMS_EPS) * norm_w
        outs.append(oh * jax.nn.silu(r[:, h * V:(h + 1) * V]))
    return (jnp.concatenate(outs, axis=1),), s_t


def _merge_fn(xs_, ps_):
    gates, y_ssd, y_gdn, y_gla = xs_
    d = D_MODEL
    return (jax.nn.sigmoid(gates[:, :d]) * y_ssd + jax.nn.sigmoid(gates[:, d:2 * d]) * y_gdn
            + jax.nn.sigmoid(gates[:, 2 * d:]) * y_gla,)


def _ln_fn(xs_, ps_):
    x, r = xs_
    g, b = ps_
    t = ALPHA * x + r
    mu = jnp.mean(t, axis=1, keepdims=True)
    var = jnp.mean(jnp.square(t - mu), axis=1, keepdims=True)
    return ((t - mu) * lax.rsqrt(var + LN_EPS) * g + b,)


def _row_spec(rows, width, colblk, n, reverse):
    if reverse:
        return pl.BlockSpec((rows, width), lambda c: (n - 1 - c, colblk))
    return pl.BlockSpec((rows, width), lambda c: (c, colblk))


def _full_spec(shape):
    zeros = (0,) * len(shape)
    return pl.BlockSpec(shape, lambda c: zeros)


def _chain_fwd(name, fn, n, blocked, full, out_defs, state_shape=None):
    nb, nf, no = len(blocked), len(full), len(out_defs)

    def body(*refs):
        xs = [r[...].astype(F32) for r in refs[:nb]]
        ps = [r[...] for r in refs[nb:nb + nf]]
        o_refs = refs[nb + nf:nb + nf + no]
        if state_shape is None:
            outs = fn(xs, ps)
        else:
            sprev_ref, s_ref = refs[nb + nf + no:]

            @pl.when(pl.program_id(0) == 0)
            def _():
                s_ref[...] = jnp.zeros_like(s_ref)

            s = s_ref[...]
            sprev_ref[0] = s
            outs, s_new = fn(xs, ps, s)
            s_ref[...] = s_new
        for r, o in zip(o_refs, outs):
            r[...] = o.astype(r.dtype)

    in_specs = [_row_spec(rows, width, cb, n, False) for _, rows, width, cb in blocked]
    in_specs += [_full_spec(a.shape) for a in full]
    out_specs = [_row_spec(rows, width, 0, n, False) for rows, width, _ in out_defs]
    out_shape = [jax.ShapeDtypeStruct((n * rows, width), dt) for rows, width, dt in out_defs]
    scratch = []
    if state_shape is not None:
        out_specs.append(pl.BlockSpec((1,) + state_shape, lambda c: (c, 0, 0)))
        out_shape.append(jax.ShapeDtypeStruct((n,) + state_shape, F32))
        scratch.append(pltpu.VMEM(state_shape, F32))
    return pl.pallas_call(body, name=name, grid=(n,), in_specs=in_specs, out_specs=out_specs, out_shape=out_shape,
                          scratch_shapes=scratch, compiler_params=_cparams(("arbitrary",)))(
        *[a for a, _, _, _ in blocked], *full)


def _chain_bwd(name, fn, n, blocked, full, douts, sprev=None, dx_dtypes=None):
    nb, nf, nd = len(blocked), len(full), len(douts)
    has_state = sprev is not None
    dx_dtypes = dx_dtypes or [F32] * nb

    def body(*refs):
        pos = 0
        b_refs = refs[pos:pos + nb]; pos += nb
        f_refs = refs[pos:pos + nf]; pos += nf
        d_refs = refs[pos:pos + nd]; pos += nd
        if has_state:
            sprev_ref = refs[pos]; pos += 1
        dx_refs = refs[pos:pos + nb]; pos += nb
        dp_refs = refs[pos:pos + nf]; pos += nf
        if has_state:
            ds_ref = refs[pos]

        @pl.when(pl.program_id(0) == 0)
        def _():
            for r in dp_refs:
                r[...] = jnp.zeros_like(r)
            if has_state:
                ds_ref[...] = jnp.zeros_like(ds_ref)

        xs = [r[...].astype(F32) for r in b_refs]
        ps = [r[...] for r in f_refs]
        dys = tuple(r[...].astype(F32) for r in d_refs)
        if has_state:
            _, vjp = jax.vjp(fn, xs, ps, sprev_ref[0])
            dxs, dps, ds = vjp((dys, ds_ref[...]))
            ds_ref[...] = ds
        else:
            _, vjp = jax.vjp(fn, xs, ps)
            dxs, dps = vjp(dys)
        for r, d in zip(dx_refs, dxs):
            r[...] = d.astype(r.dtype)
        for r, d in zip(dp_refs, dps):
            r[...] += d

    in_specs = [_row_spec(rows, width, cb, n, True) for _, rows, width, cb in blocked]
    in_specs += [_full_spec(a.shape) for a in full]
    in_specs += [_row_spec(rows, width, 0, n, True) for _, rows, width in douts]
    args = [a for a, _, _, _ in blocked] + list(full) + [a for a, _, _ in douts]
    scratch = []
    if has_state:
        st_shape = sprev.shape[1:]
        in_specs.append(pl.BlockSpec((1,) + st_shape, lambda c: (n - 1 - c, 0, 0)))
        args.append(sprev)
        scratch.append(pltpu.VMEM(st_shape, F32))
    out_specs = [_row_spec(rows, width, 0, n, True) for _, rows, width, _ in blocked]
    out_specs += [_full_spec(a.shape) for a in full]
    out_shape = [jax.ShapeDtypeStruct((n * rows, width), dt) for (_, rows, width, _), dt in zip(blocked, dx_dtypes)]
    out_shape += [jax.ShapeDtypeStruct(a.shape, F32) for a in full]
    res = pl.pallas_call(body, name=name, grid=(n,), in_specs=in_specs, out_specs=out_specs, out_shape=out_shape,
                         scratch_shapes=scratch, compiler_params=_cparams(("arbitrary",)))(*args)
    return res[:nb], res[nb:]


def _tile(n, target, unit):
    if n <= target:
        return n
    best = None
    for t in range(unit, target + 1, unit):
        if n % t == 0:
            best = t
    assert best is not None, (n, target, unit)
    return best


def _mm(name, a, b, dims="nn", out_dtype=F32, tm=2048, tn=512, tk=2048, after=None):
    if dims == "nn":
        (M, K), (_, N) = a.shape, b.shape
    elif dims == "nt":
        (M, K), (N, _) = a.shape, b.shape
    else:
        (K, M), (_, N) = a.shape, b.shape
    tm, tn, tk = _tile(M, tm, LANES), _tile(N, tn, LANES), _tile(K, tk, LANES)
    nk = K // tk
    extra = [] if after is None else [after]

    def body(*refs):
        a_ref, b_ref = refs[:2]
        o_ref, acc_ref = refs[-2:]
        part = _dot(a_ref[...], b_ref[...], dims)
        if nk == 1:
            o_ref[...] = part.astype(o_ref.dtype)
            return
        k = pl.program_id(2)

        @pl.when(k == 0)
        def _():
            acc_ref[...] = part

        @pl.when((k > 0) & (k < nk - 1))
        def _():
            acc_ref[...] += part

        @pl.when(k == nk - 1)
        def _():
            o_ref[...] = (acc_ref[...] + part).astype(o_ref.dtype)

    if dims == "tn":
        a_spec = pl.BlockSpec((tk, tm), lambda j, i, k: (k, i))
    else:
        a_spec = pl.BlockSpec((tm, tk), lambda j, i, k: (i, k))
    if dims == "nt":
        b_spec = pl.BlockSpec((tn, tk), lambda j, i, k: (j, k))
    else:
        b_spec = pl.BlockSpec((tk, tn), lambda j, i, k: (k, j))
    return pl.pallas_call(
        body, name=name, grid=(N // tn, M // tm, nk), in_specs=[a_spec, b_spec] + [ANY] * len(extra),
        out_specs=pl.BlockSpec((tm, tn), lambda j, i, k: (i, j)), out_shape=jax.ShapeDtypeStruct((M, N), out_dtype),
        scratch_shapes=[pltpu.VMEM((tm, tn) if nk > 1 else (8, LANES), F32)],
        compiler_params=_cparams(("parallel", "parallel", "arbitrary")))(a, b, *extra)


CONV_CB = 256


def _shift_down(x, k):
    if k == 0:
        return x
    return jnp.where(_iota2(x.shape, 0) >= k, pltpu.roll(x, k, 0), 0.0)


def _shift_up(x, k):
    if k == 0:
        return x
    t = x.shape[0]
    return jnp.where(_iota2(x.shape, 0) < t - k, pltpu.roll(x, t - k, 0), 0.0)


def _conv_pre(x, w, b):
    kk = w.shape[0]
    pre = x * w[kk - 1:kk, :]
    for k in range(kk - 1):
        pre = pre + _shift_down(x, kk - 1 - k) * w[k:k + 1, :]
    return pre if b is None else pre + b


def _conv_bwd_pre(x, w, dpre, dw_ref, db_ref):
    kk = w.shape[0]
    dx = dpre * w[kk - 1:kk, :]
    dw_ref[kk - 1:kk, :] = jnp.sum(dpre * x, axis=0, keepdims=True)
    for k in range(kk - 1):
        dx = dx + _shift_up(dpre, kk - 1 - k) * w[k:k + 1, :]
        dw_ref[k:k + 1, :] = jnp.sum(dpre * _shift_down(x, kk - 1 - k), axis=0, keepdims=True)
    if db_ref is not None:
        db_ref[...] = jnp.sum(dpre, axis=0, keepdims=True)
    return dx


def _dsilu(pre):
    sg = jax.nn.sigmoid(pre)
    return sg * (1.0 + pre * (1.0 - sg))


def _conv_silu_fwd(name, src, col0, w, b):
    T = src.shape[0]
    kk, C = w.shape
    cb = CONV_CB
    off = col0 // cb

    def body(*refs):
        x_ref, w_ref = refs[:2]
        b_val = refs[2][...] if b is not None else None
        refs[-1][...] = jax.nn.silu(_conv_pre(x_ref[...], w_ref[...], b_val))

    in_specs = [pl.BlockSpec((T, cb), lambda j: (0, off + j)), pl.BlockSpec((kk, cb), lambda j: (0, j))]
    args = [src, w]
    if b is not None:
        in_specs.append(pl.BlockSpec((1, cb), lambda j: (0, j)))
        args.append(b)
    return pl.pallas_call(body, name=name, grid=(C // cb,), in_specs=in_specs,
                          out_specs=pl.BlockSpec((T, cb), lambda j: (0, j)), out_shape=jax.ShapeDtypeStruct((T, C), F32),
                          compiler_params=_cparams(("parallel",)))(*args)


def _conv_silu_bwd(name, src, col0, w, b, dy, dx_dtype):
    T = src.shape[0]
    kk, C = w.shape
    cb = CONV_CB
    off = col0 // cb
    has_b = b is not None

    def body(*refs):
        x_ref, w_ref = refs[:2]
        pos = 2
        b_val = None
        if has_b:
            b_val = refs[pos][...]; pos += 1
        dy_ref = refs[pos]; pos += 1
        dx_ref, dw_ref = refs[pos], refs[pos + 1]
        db_ref = refs[pos + 2] if has_b else None
        x, wv = x_ref[...], w_ref[...]
        dpre = dy_ref[...] * _dsilu(_conv_pre(x, wv, b_val))
        dx_ref[...] = _conv_bwd_pre(x, wv, dpre, dw_ref, db_ref).astype(dx_ref.dtype)

    in_specs = [pl.BlockSpec((T, cb), lambda j: (0, off + j)), pl.BlockSpec((kk, cb), lambda j: (0, j))]
    args = [src, w]
    if has_b:
        in_specs.append(pl.BlockSpec((1, cb), lambda j: (0, j)))
        args.append(b)
    in_specs.append(pl.BlockSpec((T, cb), lambda j: (0, j)))
    args.append(dy)
    out_specs = [pl.BlockSpec((T, cb), lambda j: (0, j)), pl.BlockSpec((kk, cb), lambda j: (0, j))]
    out_shape = [jax.ShapeDtypeStruct((T, C), dx_dtype), jax.ShapeDtypeStruct((kk, C), F32)]
    if has_b:
        out_specs.append(pl.BlockSpec((1, cb), lambda j: (0, j)))
        out_shape.append(jax.ShapeDtypeStruct((1, C), F32))
    return pl.pallas_call(body, name=name, grid=(C // cb,), in_specs=in_specs, out_specs=out_specs, out_shape=out_shape,
                          compiler_params=_cparams(("parallel",)))(*args)


def _ffn_glu_fwd(name, up, w, b, out_dtype=F32):
    T = up.shape[0]
    kk = w.shape[0]
    cb = CONV_CB
    width = up.shape[1] // 2
    nblk = width // cb

    def body(g_ref, u_ref, wg_ref, wu_ref, bg_ref, bu_ref, o_ref):
        g = _conv_pre(g_ref[...], wg_ref[...], bg_ref[...])
        u = _conv_pre(u_ref[...], wu_ref[...], bu_ref[...])
        o_ref[...] = (jax.nn.silu(g) * u).astype(o_ref.dtype)

    lo, hi = (lambda j: (0, j)), (lambda j: (0, nblk + j))
    in_specs = [pl.BlockSpec((T, cb), lo), pl.BlockSpec((T, cb), hi), pl.BlockSpec((kk, cb), lo), pl.BlockSpec((kk, cb), hi),
                pl.BlockSpec((1, cb), lo), pl.BlockSpec((1, cb), hi)]
    return pl.pallas_call(body, name=name, grid=(nblk,), in_specs=in_specs, out_specs=pl.BlockSpec((T, cb), lo),
                          out_shape=jax.ShapeDtypeStruct((T, width), out_dtype),
                          compiler_params=_cparams(("parallel",)))(up, up, w, w, b, b)


def _ffn_glu_bwd(name, up, w, b, dact, dx_dtype):
    T = up.shape[0]
    kk = w.shape[0]
    cb = CONV_CB
    width = up.shape[1] // 2
    nblk = width // cb

    def body(g_ref, u_ref, wg_ref, wu_ref, bg_ref, bu_ref, d_ref, dg_ref, du_ref, dwg_ref, dwu_ref, dbg_ref, dbu_ref):
        xg, xu, wg, wu = g_ref[...], u_ref[...], wg_ref[...], wu_ref[...]
        g = _conv_pre(xg, wg, bg_ref[...])
        u = _conv_pre(xu, wu, bu_ref[...])
        d = d_ref[...].astype(F32)
        dg_ref[...] = _conv_bwd_pre(xg, wg, d * u * _dsilu(g), dwg_ref, dbg_ref).astype(dg_ref.dtype)
        du_ref[...] = _conv_bwd_pre(xu, wu, d * jax.nn.silu(g), dwu_ref, dbu_ref).astype(du_ref.dtype)

    lo, hi = (lambda j: (0, j)), (lambda j: (0, nblk + j))
    in_specs = [pl.BlockSpec((T, cb), lo), pl.BlockSpec((T, cb), hi), pl.BlockSpec((kk, cb), lo), pl.BlockSpec((kk, cb), hi),
                pl.BlockSpec((1, cb), lo), pl.BlockSpec((1, cb), hi), pl.BlockSpec((T, cb), lo)]
    out_specs = [pl.BlockSpec((T, cb), lo)] * 2 + [pl.BlockSpec((kk, cb), lo)] * 2 + [pl.BlockSpec((1, cb), lo)] * 2
    out_shape = ([jax.ShapeDtypeStruct((T, width), dx_dtype)] * 2 + [jax.ShapeDtypeStruct((kk, width), F32)] * 2
                 + [jax.ShapeDtypeStruct((1, width), F32)] * 2)
    return pl.pallas_call(body, name=name, grid=(nblk,), in_specs=in_specs, out_specs=out_specs, out_shape=out_shape,
                          compiler_params=_cparams(("parallel",)))(up, up, w, w, b, b, dact)


def _loss_head(y, target):
    T, D = y.shape
    tb = _tile(T, 256, 8)

    def body(y_ref, t_ref, dy_ref, l_ref):
        @pl.when(pl.program_id(0) == 0)
        def _():
            l_ref[...] = jnp.zeros_like(l_ref)

        err = y_ref[...] - t_ref[...]
        dy_ref[...] = err * (1.0 / D)
        l_ref[...] += jnp.sum(err * err, axis=0, keepdims=True) * (0.5 / D)

    spec = pl.BlockSpec((tb, D), lambda i: (i, 0))
    return pl.pallas_call(body, name="loss_head", grid=(T // tb,), in_specs=[spec, spec],
                          out_specs=[spec, pl.BlockSpec((1, D), lambda i: (0, 0))],
                          out_shape=[jax.ShapeDtypeStruct((T, D), F32), jax.ShapeDtypeStruct((1, D), F32)],
                          compiler_params=_cparams(("arbitrary",)))(y, target)


def _adamw_math(w, g, m, v):
    m = ADAM_B1 * m + (1.0 - ADAM_B1) * g
    v = ADAM_B2 * v + (1.0 - ADAM_B2) * jnp.square(g)
    m_hat = m / (1.0 - ADAM_B1 ** ADAM_STEP)
    v_hat = v / (1.0 - ADAM_B2 ** ADAM_STEP)
    return -ADAM_LR * (m_hat / (jnp.sqrt(v_hat) + ADAM_EPS) + ADAM_WD * w), m, v


def _adamw(name, w, g, m, v):
    A, R, C = w.shape
    if C % LANES == 0:
        rb, cb = _slab(R, C)
    else:
        rb, cb = _tile(R, max(8, SLAB_BYTES // 2 // (C * 4) // 8 * 8), 8), C

    def body(w_ref, g_ref, m_ref, v_ref, d_ref, mo_ref, vo_ref):
        d, mn, vn = _adamw_math(w_ref[...], g_ref[...], m_ref[...], v_ref[...])
        d_ref[...] = d
        mo_ref[...] = mn
        vo_ref[...] = vn

    spec = pl.BlockSpec((1, rb, cb), lambda a, r, q: (a, r, q))
    return pl.pallas_call(body, name=name, grid=(A, R // rb, C // cb), in_specs=[spec] * 4, out_specs=[spec] * 3,
                          out_shape=[jax.ShapeDtypeStruct(w.shape, F32)] * 3,
                          compiler_params=_cparams(("parallel", "parallel", "parallel")))(w, g, m, v)


def _adamw_small(parts, w, m, v):
    def body(p_ref, w_ref, m_ref, v_ref, g_ref, d_ref, mo_ref, vo_ref):
        g = p_ref[0]
        for i in range(1, N_DEV):
            g = g + p_ref[i]
        d, mn, vn = _adamw_math(w_ref[...], g, m_ref[...], v_ref[...])
        g_ref[...] = g
        d_ref[...] = d
        mo_ref[...] = mn
        vo_ref[...] = vn

    return pl.pallas_call(body, name="adamw_small", out_shape=[jax.ShapeDtypeStruct(w.shape, F32)] * 4,
                          compiler_params=_cparams())(parts, w, m, v)


def _add_blocks(name, a, b, out_dtype=F32):
    n, R, W = a.shape
    rb = _tile(R, 512, 8)

    def body(a_ref, b_ref, o_ref):
        o_ref[...] = (a_ref[...].astype(F32) + b_ref[...].astype(F32)).astype(o_ref.dtype)

    spec = pl.BlockSpec((1, rb, W), lambda i, r: (i, r, 0))
    return pl.pallas_call(body, name=name, grid=(n, R // rb), in_specs=[spec, spec], out_specs=spec,
                          out_shape=jax.ShapeDtypeStruct(a.shape, out_dtype),
                          compiler_params=_cparams(("parallel", "parallel")))(a, b)


SLAB_BYTES = 5 << 19


def _slab(R, W):
    if R % 16 == 0:
        return _tile(R, max(16, SLAB_BYTES // (4 * W) // 16 * 16), 16), W
    assert W % LANES == 0, (R, W)
    return R, _tile(W, max(LANES, SLAB_BYTES // (4 * R) // LANES * LANES), LANES)


def _pair_add(name, g, other, c, chip):
    _, R, W = g.shape
    rb, cb = _slab(R, W)

    def body(s_ref, a_ref, b_ref, send_ref, own_ref):
        s = a_ref[0] + b_ref[0]
        send_ref[0] = s.astype(send_ref.dtype)

        @pl.when(pl.program_id(2) == s_ref[1])
        def _():
            own_ref[...] = s

    grid_spec = pltpu.PrefetchScalarGridSpec(
        num_scalar_prefetch=1, grid=(R // rb, W // cb, 4),
        in_specs=[pl.BlockSpec((1, rb, cb), lambda r, q, p, s_ref: (2 * p + s_ref[0], r, q)),
                  pl.BlockSpec((1, rb, cb), lambda r, q, p, s_ref: (p, r, q))],
        out_specs=[pl.BlockSpec((1, rb, cb), lambda r, q, p, s_ref: (p, r, q)),
                   pl.BlockSpec((rb, cb), lambda r, q, p, s_ref: (r, q))])
    scalars = jnp.stack([c, chip]).astype(jnp.int32)
    return pl.pallas_call(body, name=name, grid_spec=grid_spec,
                          out_shape=[jax.ShapeDtypeStruct((4, R, W), MXU_DTYPE), jax.ShapeDtypeStruct((R, W), F32)],
                          compiler_params=_cparams(("parallel", "parallel", "arbitrary")))(scalars, g, other)


def _sum4(name, own, parts):
    R, W = own.shape
    rb, cb = _slab(R, W)

    def body(o_ref, p_ref, out_ref):
        out_ref[...] = ((o_ref[...] + p_ref[0].astype(F32)) + p_ref[1].astype(F32)) + p_ref[2].astype(F32)

    return pl.pallas_call(body, name=name, grid=(R // rb, W // cb),
                          in_specs=[pl.BlockSpec((rb, cb), lambda r, q: (r, q)), pl.BlockSpec((3, rb, cb), lambda r, q: (0, r, q))],
                          out_specs=pl.BlockSpec((rb, cb), lambda r, q: (r, q)), out_shape=jax.ShapeDtypeStruct((R, W), F32),
                          compiler_params=_cparams(("parallel", "parallel")))(own, parts)


MESH = pl.DeviceIdType.MESH
ANY = pl.BlockSpec(memory_space=pl.ANY)


def _place():
    return lax.axis_index("x"), lax.axis_index("y"), lax.axis_index("c")


def _other_chips(x, y):
    return [(1 - x, y), (x, 1 - y), (1 - x, 1 - y)]


def _all_gather(name, blocks):
    n = len(blocks)

    def body(*refs):
        x_refs, out_refs = refs[:n], refs[n:2 * n]
        send_sems, recv_sems, local_sems = refs[2 * n:]
        x, y, c = _place()
        me, sibling = (x, y, c), (x, y, 1 - c)
        chips = _other_chips(x, y)

        def slot(a, px, py, pc):
            return out_refs[a].at[4 * px + 2 * py + pc]

        def copy(a, k, blk, to, src=None):
            return pltpu.make_async_remote_copy(src_ref=slot(a, *blk) if src is None else src, dst_ref=slot(a, *blk),
                                                send_sem=send_sems.at[a, k], recv_sem=recv_sems.at[a, k],
                                                device_id=to, device_id_type=MESH)

        mine = [pltpu.make_async_copy(x_refs[a], slot(a, *me), local_sems.at[a]) for a in range(n)]
        for cp in mine:
            cp.start()
        first = []
        for j, chip in enumerate(chips):
            first += [copy(a, 1 + j, me, (*chip, c), src=x_refs[a]) for a in range(n)]
        first += [copy(a, 0, me, sibling, src=x_refs[a]) for a in range(n)]
        for cp in first:
            cp.start()
        passed = []
        for j, chip in enumerate(chips):
            for a in range(n):
                copy(a, 1 + j, (*chip, c), me).wait_recv()
                passed.append(copy(a, 4 + j, (*chip, c), sibling))
                passed[-1].start()
        for a in range(n):
            copy(a, 0, sibling, me).wait_recv()
        for j, chip in enumerate(chips):
            for a in range(n):
                copy(a, 4 + j, (*chip, 1 - c), me).wait_recv()
        for cp in first + passed:
            cp.wait_send()
        for cp in mine:
            cp.wait()

    return pl.pallas_call(body, name=name, in_specs=[ANY] * n, out_specs=[ANY] * n,
                          out_shape=[jax.ShapeDtypeStruct((N_DEV,) + b.shape, b.dtype) for b in blocks],
                          scratch_shapes=[pltpu.SemaphoreType.DMA((n, 7)), pltpu.SemaphoreType.DMA((n, 7)),
                                          pltpu.SemaphoreType.DMA((n,))])(*blocks)


def _routes_to_sibling(x, y, c):
    return [(2 * p + (1 - c), p, (x, y, 1 - c)) for p in range(4)]


def _routes_to_chips(x, y, c):
    return [(2 * px + py, j, (px, py, c)) for j, (px, py) in enumerate(_other_chips(x, y))]


def _routes_block_to_chips(x, y, c):
    me = 4 * x + 2 * y + c
    return [(me, me, (px, py, c)) for px, py in _other_chips(x, y)]


def _routes_blocks_to_sibling(x, y, c):
    return [(4 * px + 2 * py + c, 4 * px + 2 * py + c, (x, y, 1 - c)) for px, py in [(x, y)] + _other_chips(x, y)]


def _route_copies(routes, src_refs, land_refs, send_sems, recv_sems):
    x, y, c = _place()
    copies = []
    for a, (src, land) in enumerate(zip(src_refs, land_refs)):
        plan = routes(x, y, c)
        for k, (s, d, target) in enumerate(plan):
            i = a * len(plan) + k
            copies.append(pltpu.make_async_remote_copy(src_ref=src.at[s], dst_ref=land.at[d], send_sem=send_sems.at[i],
                                                       recv_sem=recv_sems.at[i], device_id=target, device_id_type=MESH))
    return copies


def _exchange(name, routes, n_routes, srcs, land_slots):
    n = len(srcs)

    def body(*refs):
        copies = _route_copies(routes, refs[:n], refs[n:2 * n], refs[2 * n], refs[2 * n + 1])
        for cp in copies:
            cp.start()
        for cp in copies:
            cp.wait_recv()
        for cp in copies:
            cp.wait_send()

    return pl.pallas_call(body, name=name, in_specs=[ANY] * n, out_specs=[ANY] * n,
                          out_shape=[jax.ShapeDtypeStruct((land_slots,) + s.shape[1:], s.dtype) for s in srcs],
                          scratch_shapes=[pltpu.SemaphoreType.DMA((n * n_routes,)), pltpu.SemaphoreType.DMA((n * n_routes,))])(*srcs)


HBM_SPEC = pl.BlockSpec(memory_space=pltpu.HBM)
SEM_SPEC = pl.BlockSpec(memory_space=pltpu.SEMAPHORE)
DATAFLOW = pltpu.SideEffectType.DATAFLOW_SIDE_EFFECTING


def _exchange_start(name, routes, n_routes, srcs, lands, after=None):
    n = len(srcs)
    in_place = lands is None
    bufs = list(srcs) + ([] if in_place else list(lands))
    nb = len(bufs)
    extra = [] if after is None else [after]

    def body(*refs):
        src_refs = refs[:n]
        land_refs = src_refs if in_place else refs[n:nb]
        send_sems, recv_sems = refs[nb + len(extra)], refs[nb + len(extra) + 1]
        token = refs[-1]
        for cp in _route_copies(routes, src_refs, land_refs, send_sems, recv_sems):
            cp.start()
        token[...] = jnp.zeros_like(token)

    sems = [pltpu.SemaphoreType.DMA((n * n_routes,)), pltpu.SemaphoreType.DMA((n * n_routes,))]
    out = pl.pallas_call(
        body, name=name, in_specs=[HBM_SPEC] * nb + [ANY] * len(extra),
        out_shape=sems + [pltpu.HBM(b.shape, b.dtype) for b in bufs] + [jax.ShapeDtypeStruct((8, LANES), F32)],
        out_specs=[SEM_SPEC, SEM_SPEC] + [HBM_SPEC] * nb + [pl.BlockSpec(memory_space=pltpu.VMEM)],
        input_output_aliases={i: 2 + i for i in range(nb)},
        compiler_params=pltpu.CompilerParams(has_side_effects=DATAFLOW))(
        *[pltpu.with_memory_space_constraint(b, pltpu.HBM) for b in bufs], *extra)
    return (out[0], out[1], list(out[2:2 + nb])), out[-1]


def _exchange_wait(name, routes, n_routes, n, started, after):
    send_sems, recv_sems, bufs = started
    nb = len(bufs)
    in_place = nb == n

    def body(*refs):
        src_refs = refs[:n]
        land_refs = src_refs if in_place else refs[n:nb]
        for cp in _route_copies(routes, src_refs, land_refs, refs[nb], refs[nb + 1]):
            cp.wait_send()
            cp.wait_recv()

    out = pl.pallas_call(
        body, name=name, in_specs=[HBM_SPEC] * nb + [SEM_SPEC, SEM_SPEC, ANY],
        out_shape=[pltpu.HBM(b.shape, b.dtype) for b in bufs], out_specs=[HBM_SPEC] * nb,
        input_output_aliases={i: i for i in range(nb)},
        compiler_params=pltpu.CompilerParams(has_side_effects=DATAFLOW))(*bufs, send_sems, recv_sems, after)
    return list(out[:n]) if in_place else list(out[n:])


def _pair_sums(tag, gs, from_sibling):
    x, y, c = _place()
    return [_pair_add(f"rs_add_{tag}_{i}", g, o, c, 2 * x + y) for i, (g, o) in enumerate(zip(gs, from_sibling))]


def _reduce_scatter(tag, gs):
    sums = _pair_sums(tag, gs, _exchange(f"rs_swap_{tag}", _routes_to_sibling, 4, gs, 4))
    got = _exchange(f"rs_chips_{tag}", _routes_to_chips, 3, [s[0] for s in sums], 3)
    return [_sum4(f"rs_sum_{tag}_{i}", s[1], q) for i, (s, q) in enumerate(zip(sums, got))]


def _reduce_scatter_begin(tag, gs):
    lands = [lax.empty((4,) + g.shape[1:], g.dtype) for g in gs]
    swap, token = _exchange_start(f"rs_swap_{tag}_start", _routes_to_sibling, 4, gs, lands)
    return dict(tag=tag, gs=gs, swap=swap), token


def _reduce_scatter_middle(state, after):
    tag, gs = state["tag"], state["gs"]
    from_sibling = _exchange_wait(f"rs_swap_{tag}_wait", _routes_to_sibling, 4, len(gs), state["swap"], after)
    state["sums"] = _pair_sums(tag, gs, from_sibling)
    partials = [s[0] for s in state["sums"]]
    lands = [lax.empty((3,) + p.shape[1:], p.dtype) for p in partials]
    state["chips"], token = _exchange_start(f"rs_chips_{tag}_start", _routes_to_chips, 3, partials, lands)
    return token


def _reduce_scatter_end(state, after):
    tag = state["tag"]
    got = _exchange_wait(f"rs_chips_{tag}_wait", _routes_to_chips, 3, len(state["gs"]), state["chips"], after)
    return [_sum4(f"rs_sum_{tag}_{i}", s[1], q) for i, (s, q) in enumerate(zip(state["sums"], got))]


def _all_gather_begin(tag, blocks, after):
    dev = 4 * lax.axis_index("x") + 2 * lax.axis_index("y") + lax.axis_index("c")
    zones = [lax.dynamic_update_slice_in_dim(lax.empty((N_DEV,) + b.shape, b.dtype), b[None], dev, axis=0) for b in blocks]
    chips, token = _exchange_start(f"gather_{tag}_chips_start", _routes_block_to_chips, 3, zones, None, after)
    return dict(tag=tag, n=len(blocks), chips=chips), token


def _all_gather_middle(state, after):
    tag, n = state["tag"], state["n"]
    zones = _exchange_wait(f"gather_{tag}_chips_wait", _routes_block_to_chips, 3, n, state["chips"], after)
    state["sibling"], token = _exchange_start(f"gather_{tag}_sibling_start", _routes_blocks_to_sibling, 4, zones, None)
    return token


def _all_gather_end(state, after):
    return _exchange_wait(f"gather_{state['tag']}_sibling_wait", _routes_blocks_to_sibling, 4, state["n"], state["sibling"], after)


PACK_UNIT = 8 * LANES


def _packed_size(shape):
    return -(-math.prod(shape) // PACK_UNIT) * PACK_UNIT


def _pack(arrays, dtype):
    parts = []
    for a in arrays:
        flat = a.reshape(-1).astype(dtype)
        parts.append(jnp.pad(flat, (0, _packed_size(a.shape) - flat.shape[0])))
    return jnp.concatenate(parts).reshape(-1, LANES)


def _unpack(flat, shapes, lead=()):
    flat = flat.reshape(lead + (-1,))
    out, pos = [], 0
    for s in shapes:
        out.append(flat[..., pos:pos + math.prod(s)].reshape(lead + tuple(s)))
        pos += _packed_size(s)
    return out


def _ffn_pad_rows(a):
    n = a.shape[0] // FFN_HALF
    a = jnp.pad(a.reshape(n, FFN_HALF, a.shape[1]), ((0, 0), (0, FFN_HALF_PAD - FFN_HALF), (0, 0)))
    return a.reshape(n * FFN_HALF_PAD, a.shape[2])


def _ffn_unpad_rows(a):
    n = a.shape[0] // FFN_HALF_PAD
    return a.reshape(n, FFN_HALF_PAD, a.shape[1])[:, :FFN_HALF].reshape(n * FFN_HALF, a.shape[1])


def _ffn_pad_cols(a):
    n = a.shape[1] // FFN_HALF
    a = jnp.pad(a.reshape(a.shape[0], n, FFN_HALF), ((0, 0), (0, 0), (0, FFN_HALF_PAD - FFN_HALF)))
    return a.reshape(a.shape[0], n * FFN_HALF_PAD)


def _ffn_unpad_cols(a):
    n = a.shape[1] // FFN_HALF_PAD
    return a.reshape(a.shape[0], n, FFN_HALF_PAD)[:, :, :FFN_HALF].reshape(a.shape[0], n * FFN_HALF)


def _shard_to_send(name, shard):
    if name == "w_in":
        shard = shard.T
    elif name == "ffn_w_up":
        shard = _ffn_pad_rows(shard.T)
    return shard.astype(MXU_DTYPE)


def _whole_from_gathered(name, g):
    if name == "w_in":
        return _pad_in_proj_rows(g.reshape(IN_DIM, g.shape[2]))
    if name in ("w_br_gdn", "w_br_gla"):
        return jnp.transpose(g, (1, 0, 2)).reshape(g.shape[1], N_DEV * g.shape[2])
    if name == "ffn_w_down":
        return jnp.pad(g, ((0, 0), (0, FFN_HALF_PAD - FFN_HALF), (0, 0))).reshape(FFN_PAD, g.shape[2])
    return g.reshape(N_DEV * g.shape[1], g.shape[2])


def _slots_from_whole(name, gw):
    if name == "w_in":
        return _unpad_in_proj_rows(gw).reshape(N_DEV, IN_DIM // N_DEV, gw.shape[1])
    if name in ("w_br_gdn", "w_br_gla"):
        return jnp.transpose(gw.reshape(gw.shape[0], N_DEV, gw.shape[1] // N_DEV), (1, 0, 2))
    return gw.reshape(N_DEV, gw.shape[0] // N_DEV, gw.shape[1])


def _shard_from_slot(name, s):
    if name == "ffn_w_up":
        return _ffn_unpad_rows(s)
    if name == "ffn_w_down":
        return s[:FFN_HALF]
    return s


def _in_proj_pieces():
    starts, pos = {}, 0
    for n, width in IN_SPLITS:
        starts[n] = (pos, width)
        pos += width
    return [(starts[ref][0], off + lane, starts[ref][1]) for _, off, _, pieces in PAD_SEGS for ref, lane in pieces]


def _pad_in_proj_rows(w):
    rows, at = [], 0
    for src, dst, n in sorted(_in_proj_pieces(), key=lambda p: p[1]):
        if dst > at:
            rows.append(jnp.zeros((dst - at, w.shape[1]), w.dtype))
        rows.append(w[src:src + n])
        at = dst + n
    rows.append(jnp.zeros((IN_PAD - at, w.shape[1]), w.dtype))
    return jnp.concatenate(rows, axis=0)


def _unpad_in_proj_rows(wp):
    return jnp.concatenate([wp[dst:dst + n] for _, dst, n in sorted(_in_proj_pieces())], axis=0)


def _lane_pad(a, width=LANES):
    return jnp.pad(a, ((0, 0), (0, width - a.shape[1])))


def _seg_blk(h, name, rows):
    off, width = SEG[name]
    return (h, rows, width, off // width)


def _ln_both(xs_, ps_):
    (y,) = _ln_fn(xs_, ps_)
    return (y, y)


def _behind(param, hooks, stage, *seen):
    if hooks is None or stage not in hooks:
        return param
    token = hooks[stage](*seen)
    return param if token is None else param + token[0:1, 0:1]


def _layer_fwd(l, x, x_mx, W, sp, hooks=None):
    T = x.shape[0]
    n64, ngla, ntok = T // SSD_CHUNK, T // GLA_BLOCK, T // 256
    h = _mm(f"in_proj_{l}", x_mx, W["w_in"], "nt")
    xbc = _conv_silu_fwd(f"ssd_conv_{l}", h, SEG["xbc"][0], sp["ssd_conv_w"], sp["ssd_conv_b"])
    gqkv = _conv_silu_fwd(f"gdn_conv_{l}", h, SEG["gqkv"][0], sp["gdn_conv_w"], None)

    ssd_in = [(xbc, SSD_CHUNK, SSD_XBC, 0), _seg_blk(h, "dt", SSD_CHUNK), _seg_blk(h, "z", SSD_CHUNK)]
    ssd_p = [sp["ssd_dt_bias"], sp["ssd_a_log"], sp["ssd_d"], sp["ssd_norm_w"]]
    o_ssd, ssd_states = _chain_fwd(f"ssd_fwd_{l}", _ssd_chunk, n64, ssd_in, ssd_p, [(SSD_CHUNK, SSD_INNER, MXU_DTYPE)],
                                   (SSD_STATE, SSD_INNER))
    o_gdn, gdn_saved = _gdn_forward(str(l), gqkv, h, dict(sp, gdn_a_log=_behind(sp["gdn_a_log"], hooks, "ssd", o_ssd)))
    gla_in = [_seg_blk(h, "lqkv", GLA_BLOCK), _seg_blk(h, "lglr", GLA_BLOCK), _seg_blk(h, "lr", GLA_BLOCK)]
    gla_p = [jnp.pad(sp["gla_gate_w2"], ((0, LANES - GLA_RANK), (0, 0))), sp["gla_gate_b"], sp["gla_norm_w"]]
    o_gla, gla_states = _chain_fwd(f"gla_fwd_{l}", _gla_block, ngla, gla_in, gla_p, [(GLA_BLOCK, GLA_V, MXU_DTYPE)],
                                   (GLA_VAL_DIM, GLA_K))
    ln1_p = [_behind(sp["ln1_g"], hooks, "mixed", o_gdn), sp["ln1_b"]]
    y_ssd = _mm(f"br_ssd_{l}", o_ssd, W["w_br_ssd"])
    y_gdn = _mm(f"br_gdn_{l}", o_gdn, W["w_br_gdn"])
    y_gla = _mm(f"br_gla_{l}", o_gla, W["w_br_gla"])
    merge_in = [_seg_blk(h, "gates", 256), (y_ssd, 256, D_MODEL, 0), (y_gdn, 256, D_MODEL, 0), (y_gla, 256, D_MODEL, 0)]
    (mix,) = _chain_fwd(f"merge_{l}", _merge_fn, ntok, merge_in, [], [(256, D_MODEL, MXU_DTYPE)])
    r1 = _mm(f"out_proj_{l}", mix, W["w_out"])
    both = [(256, D_MODEL, F32), (256, D_MODEL, MXU_DTYPE)]
    x1, x1_mx = _chain_fwd(f"ln1_{l}", _ln_both, ntok, [(x, 256, D_MODEL, 0), (r1, 256, D_MODEL, 0)], ln1_p, both)
    up = _mm(f"ffn_up_{l}", x1_mx, W["ffn_w_up"], "nt")
    act = _ffn_glu_fwd(f"ffn_glu_{l}", up, sp["ffn_conv_w_pad"], sp["ffn_conv_b_pad"], MXU_DTYPE)
    ln2_p = [_behind(sp["ln2_g"], hooks, "ffn_act", act), sp["ln2_b"]]
    r2 = _mm(f"ffn_down_{l}", act, W["ffn_w_down"])
    x2, x2_mx = _chain_fwd(f"ln2_{l}", _ln_both, ntok, [(x1, 256, D_MODEL, 0), (r2, 256, D_MODEL, 0)], ln2_p, both)
    saved = dict(x=x, x_mx=x_mx, h=h, xbc=xbc, gqkv=gqkv, ssd_in=ssd_in, ssd_p=ssd_p, ssd_states=ssd_states,
                 gdn=gdn_saved, gla_in=gla_in, gla_p=gla_p, gla_states=gla_states, o_ssd=o_ssd,
                 o_gdn=o_gdn, o_gla=o_gla, merge_in=merge_in, mix=mix, r1=r1, ln1_p=ln1_p, x1=x1, x1_mx=x1_mx, up=up, act=act,
                 r2=r2, ln2_p=ln2_p)
    return x2, x2_mx, saved


def _layer_bwd(l, dx2, W, sp, sv, hooks=None):
    T = dx2.shape[0]
    n64, ngla, ntok = T // SSD_CHUNK, T // GLA_BLOCK, T // 256
    bf = MXU_DTYPE
    gw, gs = {}, {}
    ln2_p = [_behind(sv["ln2_p"][0], hooks, "start"), sv["ln2_p"][1]]
    (dx1_a, dr2), (gs["ln2_g"], gs["ln2_b"]) = _chain_bwd(
        f"ln2_bwd_{l}", _ln_fn, ntok, [(sv["x1"], 256, D_MODEL, 0), (sv["r2"], 256, D_MODEL, 0)], ln2_p,
        [(dx2, 256, D_MODEL)], dx_dtypes=[F32, bf])
    gw["ffn_w_down"] = _mm(f"ffn_down_dw_{l}", sv["act"], dr2, "tn")
    dact = _mm(f"ffn_down_dx_{l}", dr2, W["ffn_w_down"], "nt")
    dg, du, dwg, dwu, dbg, dbu = _ffn_glu_bwd(f"ffn_glu_bwd_{l}", sv["up"], sp["ffn_conv_w_pad"], sp["ffn_conv_b_pad"], dact, bf)
    gs["ffn_conv_w"] = _ffn_unpad_cols(jnp.concatenate([dwg, dwu], axis=1))
    gs["ffn_conv_b"] = _ffn_unpad_cols(jnp.concatenate([dbg, dbu], axis=1))
    dup = jnp.concatenate([dg, du], axis=1)
    gw["ffn_w_up"] = _mm(f"ffn_up_dw_{l}", dup, sv["x1_mx"], "tn", tn=1024)
    dx1_b = _mm(f"ffn_up_dx_{l}", dup, W["ffn_w_up"], "nn", tn=1024, tk=1024)
    ln1_p = [_behind(sv["ln1_p"][0], hooks, "ffn", dx1_b), sv["ln1_p"][1]]
    (dx_a, dr1), (gs["ln1_g"], gs["ln1_b"]) = _chain_bwd(
        f"ln1_bwd_{l}", _ln_sum_fn, ntok, [(sv["x"], 256, D_MODEL, 0), (sv["r1"], 256, D_MODEL, 0)], ln1_p,
        [(dx1_a, 256, D_MODEL), (dx1_b, 256, D_MODEL)], dx_dtypes=[F32, bf])
    gw["w_out"] = _mm(f"out_proj_dw_{l}", sv["mix"], dr1, "tn")
    dmix = _mm(f"out_proj_dx_{l}", dr1, W["w_out"], "nt")
    (dgates, dy_ssd, dy_gdn, dy_gla), _ = _chain_bwd(f"merge_bwd_{l}", _merge_fn, ntok, sv["merge_in"], [],
                                                     [(dmix, 256, D_MODEL)], dx_dtypes=[bf, bf, bf, bf])
    gw["w_br_ssd"] = _mm(f"br_ssd_dw_{l}", sv["o_ssd"], dy_ssd, "tn")
    gw["w_br_gdn"] = _mm(f"br_gdn_dw_{l}", sv["o_gdn"], dy_gdn, "tn")
    gw["w_br_gla"] = _mm(f"br_gla_dw_{l}", sv["o_gla"], dy_gla, "tn")
    do_ssd = _mm(f"br_ssd_dx_{l}", dy_ssd, W["w_br_ssd"], "nt")
    do_gdn = _mm(f"br_gdn_dx_{l}", dy_gdn, W["w_br_gdn"], "nt")
    do_gla = _mm(f"br_gla_dx_{l}", dy_gla, W["w_br_gla"], "nt")

    ssd_p = [_behind(sv["ssd_p"][0], hooks, "branches", do_gla, gw)] + list(sv["ssd_p"][1:])
    (dxbc, ddt, dz), dps = _chain_bwd(f"ssd_bwd_{l}", _ssd_chunk, n64, sv["ssd_in"], ssd_p,
                                      [(do_ssd, SSD_CHUNK, SSD_INNER)], sprev=sv["ssd_states"], dx_dtypes=[F32, bf, bf])
    gs["ssd_dt_bias"], gs["ssd_a_log"], gs["ssd_d"], gs["ssd_norm_w"] = dps
    gdn_sv = dict(sv["gdn"], scan_p=[_behind(sv["gdn"]["scan_p"][0], hooks, "ssd", dz)])
    dgqkv, dgab, dgg, gs["gdn_a_log"], gs["gdn_dt_bias"], gs["gdn_norm_w"] = _gdn_backward(str(l), do_gdn, gdn_sv, bf)
    (dlqkv, dlglr, dlr), dps = _chain_bwd(f"gla_bwd_{l}", _gla_block, ngla, sv["gla_in"], sv["gla_p"],
                                          [(do_gla, GLA_BLOCK, GLA_V)], sprev=sv["gla_states"], dx_dtypes=[bf, bf, bf])
    gs["gla_gate_w2"], gs["gla_gate_b"], gs["gla_norm_w"] = dps[0][:GLA_RANK], dps[1], dps[2]
    dxbc_pre, gs["ssd_conv_w"], gs["ssd_conv_b"] = _conv_silu_bwd(
        f"ssd_conv_bwd_{l}", sv["h"], SEG["xbc"][0], sp["ssd_conv_w"], sp["ssd_conv_b"], dxbc, bf)
    dgqkv_pre, gs["gdn_conv_w"] = _conv_silu_bwd(f"gdn_conv_bwd_{l}", sv["h"], SEG["gqkv"][0], sp["gdn_conv_w"], None, dgqkv, bf)
    pieces = dict(gates=dgates, xbc=dxbc_pre, gqkv=dgqkv_pre, z=dz, lqkv=dlqkv, gg=dgg, lr=dlr, dt=ddt, gab=dgab, lglr=dlglr)
    cols = [pieces[name] for name, _, _, _ in PAD_SEGS]
    cols.append(jnp.zeros((T, IN_PAD - PAD_SEGS[-1][1] - PAD_SEGS[-1][2]), bf))
    dh = jnp.concatenate(cols, axis=1)
    gw["w_in"] = _mm(f"in_proj_dw_{l}", dh, sv["x_mx"], "tn", tn=1024)
    behind = hooks["w_in_grad"](gw) if hooks is not None and "w_in_grad" in hooks else None
    dx_b = _mm(f"in_proj_dx_{l}", dh, W["w_in"], "nn", tm=1024, tn=1024, tk=IN_PAD // 4, after=behind)
    dx = _add_blocks(f"dx_add_{l}", dx_a[None], dx_b[None])[0]
    return dx, gw, gs


def _ln_sum_fn(xs_, ps_):
    (y,) = _ln_fn(xs_, ps_)
    return (y, y)


def _small_2d(name, a):
    return a.reshape(1, -1) if a.ndim == 1 else a


def kernel(x, w_in, ssd_conv_w, ssd_conv_b, ssd_dt_bias, ssd_a_log, ssd_d, ssd_norm_w, gdn_conv_w, gdn_a_log, gdn_dt_bias, gdn_norm_w, gla_gate_w2, gla_gate_b, gla_norm_w, w_br_ssd, w_br_gdn, w_br_gla, w_out, ln1_g, ln1_b, ffn_w_up, ffn_conv_w, ffn_conv_b, ffn_w_down, ln2_g, ln2_b, loss_target, m_w_in, m_ssd_conv_w, m_ssd_conv_b, m_ssd_dt_bias, m_ssd_a_log, m_ssd_d, m_ssd_norm_w, m_gdn_conv_w, m_gdn_a_log, m_gdn_dt_bias, m_gdn_norm_w, m_gla_gate_w2, m_gla_gate_b, m_gla_norm_w, m_w_br_ssd, m_w_br_gdn, m_w_br_gla, m_w_out, m_ln1_g, m_ln1_b, m_ffn_w_up, m_ffn_conv_w, m_ffn_conv_b, m_ffn_w_down, m_ln2_g, m_ln2_b, v_w_in, v_ssd_conv_w, v_ssd_conv_b, v_ssd_dt_bias, v_ssd_a_log, v_ssd_d, v_ssd_norm_w, v_gdn_conv_w, v_gdn_a_log, v_gdn_dt_bias, v_gdn_norm_w, v_gla_gate_w2, v_gla_gate_b, v_gla_norm_w, v_w_br_ssd, v_w_br_gdn, v_w_br_gla, v_w_out, v_ln1_g, v_ln1_b, v_ffn_w_up, v_ffn_conv_w, v_ffn_conv_b, v_ffn_w_down, v_ln2_g, v_ln2_b):
    args = locals()
    w = {n: args[n] for n in WEIGHTS}
    m = {n: args["m_" + n] for n in WEIGHTS}
    v = {n: args["v_" + n] for n in WEIGHTS}
    dev = 4 * lax.axis_index("x") + 2 * lax.axis_index("y") + lax.axis_index("c")
    xl = x[0]
    tgt = loss_target[0]

    late = BIG[1:]

    def send(names, l):
        return [_shard_to_send(n, w[n][l]) for n in names]

    def whole_weights(names, got):
        return {n: _whole_from_gathered(n, g) for n, g in zip(names, got)}

    got0 = _all_gather("gather_first", send(BIG[:1], 0) + [w[n] for n in SMALL_SHARDED])
    gather0, token0 = _all_gather_begin("w_0", send(late, 0), got0[0])
    W = [whole_weights(BIG[:1], got0[:1]), None]
    whole = dict(w)
    for n, s in zip(SMALL_SHARDED, got0[1:]):
        whole[n] = jnp.transpose(s, (1, 2, 0, 3)).reshape(s.shape[1], s.shape[2], N_DEV * s.shape[3])
    SP = [{n: _small_2d(n, whole[n][l]) for n in SMALL} for l in range(DEPTH)]
    for sp in SP:
        sp["ffn_conv_w_pad"] = _ffn_pad_cols(sp["ffn_conv_w"])
        sp["ffn_conv_b_pad"] = _ffn_pad_cols(sp["ffn_conv_b"])

    held = {}

    def late_weights_cross(o_ssd):
        token = _all_gather_middle(gather0, o_ssd)
        held["gather1"], token1 = _all_gather_begin("w_1", send(BIG, 1), o_ssd)
        return token + token1

    def late_weights_arrive(mixed):
        W[0].update(whole_weights(late, _all_gather_end(gather0, mixed)))

    fwd_hooks = {"ssd": late_weights_cross, "mixed": late_weights_arrive,
                 "ffn_act": lambda act: _all_gather_middle(held["gather1"], act)}
    saved = [None] * DEPTH
    act, act_mx, saved[0] = _layer_fwd(0, xl, (xl + token0[0, 0]).astype(MXU_DTYPE), W[0], SP[0], hooks=fwd_hooks)
    W[1] = whole_weights(BIG, _all_gather_end(held["gather1"], act))
    act, act_mx, saved[1] = _layer_fwd(1, act, act_mx, W[1], SP[1])
    dy, loss_parts = _loss_head(act, tgt)
    loss = lax.psum(jnp.sum(loss_parts), ("x", "y", "c"))

    def slots_of(names, gw):
        return [_slots_from_whole(n, gw[n]) for n in names]

    grads = {}
    GS = [None] * DEPTH
    dy, gw, GS[1] = _layer_bwd(1, dy, W[1], SP[1], saved[1])
    reduce1, reduce1_token = _reduce_scatter_begin("1", slots_of(BIG, gw))

    def late_grads_leave(seen, gw0):
        held["reduce0"], token = _reduce_scatter_begin("0", slots_of(late, gw0))
        return token

    def w_in_grad_leaves(gw0):
        held["reduce_first"], token = _reduce_scatter_begin("first", slots_of(BIG[:1], gw0))
        return token

    bwd_hooks = {"start": lambda: reduce1_token, "ffn": lambda seen: _reduce_scatter_middle(reduce1, seen),
                 "branches": late_grads_leave, "ssd": lambda seen: _reduce_scatter_middle(held["reduce0"], seen),
                 "w_in_grad": w_in_grad_leaves}
    dy, gw, GS[0] = _layer_bwd(0, dy, W[0], SP[0], saved[0], hooks=bwd_hooks)
    _reduce_scatter_middle(held["reduce_first"], dy)
    red1 = _reduce_scatter_end(reduce1, dy)
    red0_late = _reduce_scatter_end(held["reduce0"], dy)
    grad_x = dy[None]
    kept_t = ("w_in", "ffn_w_up")
    grads_k = {n: jnp.stack([_shard_from_slot(n, red0_late[i]), _shard_from_slot(n, red1[i + 1])]) for i, n in enumerate(late)}

    small_shapes = [whole[n].shape for n in SMALL]
    gs_flat = _pack([jnp.stack([GS[l][n].reshape(whole[n].shape[1:]) for l in range(DEPTH)]) for n in SMALL], F32)
    (gs_all,) = _all_gather("gather_small_grads", [gs_flat])

    def mine(n, a):
        if n in SMALL_SHARDED:
            cs = a.shape[-1] // N_DEV
            return lax.dynamic_slice_in_dim(a, dev * cs, cs, axis=a.ndim - 1)
        return a

    m_whole, v_whole = {}, {}
    for n in SMALL:
        reps = (1, 1, N_DEV) if n in SMALL_SHARDED else (1,) * m[n].ndim
        m_whole[n], v_whole[n] = jnp.tile(m[n], reps), jnp.tile(v[n], reps)
    outs = _adamw_small(gs_all, _pack([whole[n] for n in SMALL], F32), _pack([m_whole[n] for n in SMALL], F32),
                        _pack([v_whole[n] for n in SMALL], F32))
    g_s, d_s, m_s, v_s = [_unpack(o, small_shapes) for o in outs]
    delta, new_m, new_v = {}, {}, {}
    for i, n in enumerate(SMALL):
        grads[n], delta[n], new_m[n], new_v[n] = mine(n, g_s[i]), mine(n, d_s[i]), mine(n, m_s[i]), mine(n, v_s[i])
    for n in late + BIG[:1]:
        if n == "w_in":
            done = sum(new_v[k].reshape(-1)[0:1] for k in late + SMALL[:1])
            (first0,) = _reduce_scatter_end(held["reduce_first"], done)
            grads_k[n] = jnp.stack([first0, red1[0]])
        view = (lambda a: jnp.transpose(a, (0, 2, 1))) if n in kept_t else (lambda a: a)
        outs = _adamw(f"adamw_{n}", view(w[n]), grads_k[n], view(m[n]), view(v[n]))
        grads[n], delta[n], new_m[n], new_v[n] = view(grads_k[n]), view(outs[0]), view(outs[1]), view(outs[2])

    return (loss, grad_x, *[grads[n] for n in WEIGHTS], *[delta[n] for n in WEIGHTS], *[new_m[n] for n in WEIGHTS],
            *[new_v[n] for n in WEIGHTS])
```

```python
import functools
import math

import jax
import jax.numpy as jnp
from jax import lax
from jax.experimental import pallas as pl
from jax.experimental.pallas import tpu as pltpu

F32 = jnp.float32
MXU_DTYPE = jnp.bfloat16
HI = lax.Precision.HIGHEST

N_DEV = 8
D_MODEL = 1024
DEPTH = 2
SSD_HEADS, SSD_HEAD_DIM, SSD_INNER, SSD_GROUPS, SSD_STATE, SSD_CHUNK = 16, 64, 1024, 2, 128, 64
SSD_XBC = SSD_INNER + 2 * SSD_GROUPS * SSD_STATE
GDN_HEADS, GDN_HEAD_DIM, GDN_WIDTH, GDN_CHUNK = 4, 128, 512, 64
GLA_HEADS, GLA_KEY_DIM, GLA_VAL_DIM, GLA_K, GLA_V, GLA_RANK, GLA_CHUNK = 4, 64, 128, 256, 512, 16, 16
GLA_BLOCK = 128
GLA_NORMALIZER = 16.0
FFN_DIM = 2816
FFN_HALF = FFN_DIM // 8
FFN_HALF_PAD = 384
FFN_UP_PAD = 16 * FFN_HALF_PAD
FFN_PAD = FFN_UP_PAD // 2
ALPHA = (2 * DEPTH) ** 0.25
LN_EPS = 1e-5
RMS_EPS = 1e-6
ADAM_LR, ADAM_B1, ADAM_B2, ADAM_EPS, ADAM_WD, ADAM_STEP = 0.001, 0.9, 0.999, 1e-08, 0.01, 10
LANES = 128
NEG_BIG = -1e30
VMEM_LIMIT = 56 * 1024 * 1024

IN_SPLITS = (("z", 1024), ("xbc", 1536), ("dt", 16), ("gqkv", 1536), ("ga", 4), ("gb", 4), ("gg", 512),
             ("lqkv", 1024), ("lglr", 16), ("lr", 512), ("gates", 3072))
IN_DIM = sum(w for _, w in IN_SPLITS)
PAD_SEGS = (("gates", 0, 3072, (("gates", 0),)), ("xbc", 3072, 1536, (("xbc", 0),)),
            ("gqkv", 4608, 1536, (("gqkv", 0),)), ("z", 6144, 1024, (("z", 0),)),
            ("lqkv", 7168, 1024, (("lqkv", 0),)), ("gg", 8192, 512, (("gg", 0),)), ("lr", 8704, 512, (("lr", 0),)),
            ("dt", 9216, 128, (("dt", 0),)), ("gab", 9344, 128, (("ga", 0), ("gb", 4))), ("lglr", 9472, 128, (("lglr", 0),)))
IN_PAD = 9728
SEG = {name: (off, width) for name, off, width, _ in PAD_SEGS}

BIG = ("w_in", "w_br_ssd", "w_br_gdn", "w_br_gla", "w_out", "ffn_w_up", "ffn_w_down")
COL_SHARDED = ("w_in", "w_br_gdn", "w_br_gla", "ffn_w_up")
SMALL_SHARDED = ("ssd_conv_w", "gdn_conv_w", "gla_gate_w2", "ffn_conv_w")
WEIGHTS = ("w_in", "ssd_conv_w", "ssd_conv_b", "ssd_dt_bias", "ssd_a_log", "ssd_d", "ssd_norm_w", "gdn_conv_w",
           "gdn_a_log", "gdn_dt_bias", "gdn_norm_w", "gla_gate_w2", "gla_gate_b", "gla_norm_w", "w_br_ssd", "w_br_gdn",
           "w_br_gla", "w_out", "ln1_g", "ln1_b", "ffn_w_up", "ffn_conv_w", "ffn_conv_b", "ffn_w_down", "ln2_g", "ln2_b")
SMALL = tuple(n for n in WEIGHTS if n not in BIG)
FLAT_W = 512


def _cparams(sem=None):
    kw = dict(vmem_limit_bytes=VMEM_LIMIT)
    if sem is not None:
        kw["dimension_semantics"] = sem
    return pltpu.CompilerParams(**kw)


_DIMS = {"nn": (((1,), (0,)), ((), ())), "nt": (((1,), (1,)), ((), ())), "tn": (((0,), (0,)), ((), ()))}


def _dot(a, b, dims="nn"):
    if MXU_DTYPE == F32:
        return lax.dot_general(a.astype(F32), b.astype(F32), _DIMS[dims], precision=HI, preferred_element_type=F32)
    return lax.dot_general(a.astype(MXU_DTYPE), b.astype(MXU_DTYPE), _DIMS[dims], preferred_element_type=F32)


def _dot_hi(a, b, dims="nn"):
    return lax.dot_general(a.astype(F32), b.astype(F32), _DIMS[dims], precision=HI, preferred_element_type=F32)


def _iota2(shape, axis):
    return lax.broadcasted_iota(jnp.int32, shape, axis)


def _tril(n, strict=False):
    r, c = _iota2((n, n), 0), _iota2((n, n), 1)
    return (r > c) if strict else (r >= c)


def _raw_dot(a, b, dims):
    return lax.dot_general(a, b, _DIMS[dims], preferred_element_type=F32)


def _dot_x3(a, b, dims="nn"):
    if MXU_DTYPE == F32:
        return _dot_hi(a, b, dims)
    ah, bh = a.astype(jnp.bfloat16), b.astype(jnp.bfloat16)
    al, bl = (a - ah.astype(F32)).astype(jnp.bfloat16), (b - bh.astype(F32)).astype(jnp.bfloat16)
    return _raw_dot(ah, bh, dims) + (_raw_dot(ah, bl, dims) + _raw_dot(al, bh, dims))


def _exact_dot(mask, b, dims, mask_first):
    if MXU_DTYPE == F32:
        return _dot_hi(mask, b, dims) if mask_first else _dot_hi(b, mask, dims)
    m = mask.astype(jnp.bfloat16)
    b1 = b.astype(jnp.bfloat16)
    r1 = b - b1.astype(F32)
    b2 = r1.astype(jnp.bfloat16)
    b3 = (r1 - b2.astype(F32)).astype(jnp.bfloat16)
    if mask_first:
        return _raw_dot(m, b1, dims) + (_raw_dot(m, b2, dims) + _raw_dot(m, b3, dims))
    return _raw_dot(b1, m, dims) + (_raw_dot(b2, m, dims) + _raw_dot(b3, m, dims))


@jax.custom_vjp
def _mask_left(mask, b):
    return _exact_dot(mask, b, "nn", True)


_mask_left.defvjp(lambda mask, b: (_mask_left(mask, b), mask),
                  lambda mask, d: (jnp.zeros_like(mask), _exact_dot(mask, d, "tn", True)))


@jax.custom_vjp
def _mask_right(a, mask):
    return _exact_dot(mask, a, "nn", False)


_mask_right.defvjp(lambda a, mask: (_mask_right(a, mask), mask),
                   lambda mask, d: (_exact_dot(mask, d, "nt", False), jnp.zeros_like(mask)))


@jax.custom_vjp
def _unit_lower_inverses(mats):
    n = mats[0].shape[0]
    eye = (_iota2((n, n), 0) == _iota2((n, n), 1)).astype(F32)
    xs = [eye - a for a in mats]
    ps = list(mats)
    k = 2
    while k < n:
        ps = [_dot_x3(p, p) for p in ps]
        xs = [x + _dot_x3(x, p) for x, p in zip(xs, ps)]
        k *= 2
    return xs


def _unit_lower_inverses_fwd(mats):
    ts = _unit_lower_inverses(mats)
    return ts, ts


def _unit_lower_inverses_bwd(ts, dts):
    mids = [_dot_x3(t, d, "tn") for t, d in zip(ts, dts)]
    return ([-_dot_x3(m, t, "nt") for m, t in zip(mids, ts)],)


_unit_lower_inverses.defvjp(_unit_lower_inverses_fwd, _unit_lower_inverses_bwd)


def _ssd_chunk(xs_, ps_, s_t):
    xbc, dtraw, z = xs_
    dt_bias, a_log, d_skip, norm_w = ps_
    L = xbc.shape[0]
    H, P, N, G = SSD_HEADS, SSD_HEAD_DIM, SSD_STATE, SSD_GROUPS
    W = SSD_INNER // G
    xs = xbc[:, :SSD_INNER]
    bm = xbc[:, SSD_INNER:SSD_INNER + G * N]
    cm = xbc[:, SSD_INNER + G * N:]
    dt = jax.nn.softplus(dtraw[:, :H] + dt_bias)
    a = dt * (-jnp.exp(a_log))
    causal = _tril(L)
    a_cs = _mask_left(causal.astype(F32), a)
    expand = (_iota2((H, SSD_INNER), 1) // P == _iota2((H, SSD_INNER), 0)).astype(F32)
    wide = _mask_right(jnp.concatenate([a_cs, dt, jnp.broadcast_to(d_skip, (L, H))], axis=0), expand)
    a_cs_x, dt_x, d_x = wide[:L], wide[L:2 * L], wide[2 * L:]
    a_end_x = a_cs_x[L - 1:L, :]
    a_cs_t, dt_t = a_cs.T, dt.T
    cb = [_dot(cm[:, g * N:(g + 1) * N], bm[:, g * N:(g + 1) * N], "nt") for g in range(G)]
    decay = [jnp.exp(jnp.where(causal, a_cs[:, h:h + 1] - a_cs_t[h:h + 1, :], NEG_BIG)) * dt_t[h:h + 1, :] for h in range(H)]
    ws = [cb[h // (H // G)] * decay[h] for h in range(H)]
    y = jnp.concatenate([_dot(ws[h], xs[:, h * P:(h + 1) * P]) for h in range(H)], axis=1)
    y_in = jnp.concatenate([_dot(cm[:, g * N:(g + 1) * N], s_t[:, g * W:(g + 1) * W]) for g in range(G)], axis=1)
    y = y + y_in * jnp.exp(a_cs_x) + d_x * xs
    xw = xs * (jnp.exp(a_end_x - a_cs_x) * dt_x)
    st = jnp.concatenate([_dot(bm[:, g * N:(g + 1) * N], xw[:, g * W:(g + 1) * W], "tn") for g in range(G)], axis=1)
    s_new = s_t * jnp.exp(a_end_x) + st
    yg = y * jax.nn.silu(z)
    outs = []
    for g in range(G):
        part = yg[:, g * W:(g + 1) * W]
        outs.append(part * lax.rsqrt(jnp.mean(part * part, axis=1, keepdims=True) + RMS_EPS))
    return (jnp.concatenate(outs, axis=1) * norm_w,), s_new


GDN_PREP_CHUNKS = 4


def _gdn_prep(xs_, ps_):
    qkv, ab = xs_
    a_log, dt_bias = ps_
    B = qkv.shape[0]
    H, D, L = GDN_HEADS, GDN_HEAD_DIM, GDN_CHUNK
    g_all = -jnp.exp(a_log) * jax.nn.softplus(ab + dt_bias)
    row, col = _iota2((B, B), 0), _iota2((B, B), 1)
    g_cs = _mask_left((((row // L) == (col // L)) & (row >= col)).astype(F32), g_all)
    g_cs_t = g_cs.T
    beta_all = jax.nn.sigmoid(ab)
    incl, strict = _tril(L), _tril(L, strict=True)
    qs, ks, vs = [], [], []
    for h in range(H):
        q = qkv[:, h * D:(h + 1) * D]
        k = qkv[:, GDN_WIDTH + h * D:GDN_WIDTH + (h + 1) * D]
        qs.append(q * lax.rsqrt(jnp.sum(q * q, axis=1, keepdims=True) + RMS_EPS) * (D ** -0.5))
        ks.append(k * lax.rsqrt(jnp.sum(k * k, axis=1, keepdims=True) + RMS_EPS))
        vs.append(qkv[:, 2 * GDN_WIDTH + h * D:2 * GDN_WIDTH + (h + 1) * D])
    pairs = [(c, h) for c in range(B // L) for h in range(H)]
    rows = {c: slice(c * L, (c + 1) * L) for c in range(B // L)}
    q_ = {(c, h): qs[h][rows[c]] for c, h in pairs}
    k_ = {(c, h): ks[h][rows[c]] for c, h in pairs}
    col_ = {(c, h): g_cs[rows[c], h:h + 1] for c, h in pairs}
    beta_ = {(c, h): beta_all[rows[c], H + h:H + h + 1] for c, h in pairs}
    gamma = {p: jnp.exp(jnp.where(incl, col_[p] - g_cs_t[p[1]:p[1] + 1, rows[p[0]]], NEG_BIG)) for p in pairs}
    kb = {p: k_[p] * beta_[p] for p in pairs}
    a_mat = [jnp.where(strict, _dot(kb[p], k_[p], "nt") * gamma[p], 0.0) for p in pairs]
    attn = {p: jnp.where(incl, _dot(q_[p], k_[p], "nt") * gamma[p], 0.0) for p in pairs}
    t_mat = dict(zip(pairs, _unit_lower_inverses(a_mat)))
    u = {p: _dot(t_mat[p], vs[p[1]][rows[p[0]]] * beta_[p]) for p in pairs}
    w = {p: _dot(t_mat[p], kb[p] * jnp.exp(col_[p])) for p in pairs}
    qd = {p: q_[p] * jnp.exp(col_[p]) for p in pairs}
    kd = {p: k_[p] * jnp.exp(col_[p][L - 1:L, :] - col_[p]) for p in pairs}

    def whole(parts):
        return jnp.concatenate([jnp.concatenate([parts[(c, h)] for h in range(H)], axis=1) for c in range(B // L)], axis=0)

    return (whole(u), whole(w), whole(qd), whole(kd), whole(attn), g_cs)


def _gdn_scan(xs_, ps_, s):
    u, w, qd, kd, attn, g_cs, gate = xs_
    (norm_w,) = ps_
    L = u.shape[0]
    H, D = GDN_HEADS, GDN_HEAD_DIM
    heads = range(H)
    lanes = [slice(h * D, (h + 1) * D) for h in heads]
    s_h = [s[lanes[h], :] for h in heads]
    v_new = [u[:, lanes[h]] - _dot(w[:, lanes[h]], s_h[h]) for h in heads]
    o = [_dot(qd[:, lanes[h]], s_h[h]) + _dot(attn[:, h * L:(h + 1) * L], v_new[h]) for h in heads]
    decay = [jnp.exp(g_cs[L - 1:L, h:h + 1]) for h in heads]
    s_new = [s_h[h] * decay[h] + _dot(kd[:, lanes[h]], v_new[h], "tn") for h in heads]
    o = [o[h] * lax.rsqrt(jnp.mean(o[h] * o[h], axis=1, keepdims=True) + RMS_EPS) * norm_w * jax.nn.silu(gate[:, lanes[h]])
         for h in heads]
    return (jnp.concatenate(o, axis=1),), jnp.concatenate(s_new, axis=0)


def _gdn_forward(tag, gqkv, h, sp):
    T = gqkv.shape[0]
    blk = GDN_PREP_CHUNKS * GDN_CHUNK
    prep_in = [(gqkv, blk, 3 * GDN_WIDTH, 0), _seg_blk(h, "gab", blk)]
    prep_p = [_lane_pad(sp["gdn_a_log"]), _lane_pad(sp["gdn_dt_bias"])]
    mx = MXU_DTYPE
    prep = _chain_fwd(f"gdn_prep_{tag}", _gdn_prep, T // blk, prep_in, prep_p,
                      [(blk, GDN_WIDTH, F32), (blk, GDN_WIDTH, mx), (blk, GDN_WIDTH, mx), (blk, GDN_WIDTH, mx),
                       (blk, GDN_HEADS * GDN_CHUNK, mx), (blk, LANES, F32)])
    widths = [GDN_WIDTH] * 4 + [GDN_HEADS * GDN_CHUNK, LANES]
    scan_in = [(a, GDN_CHUNK, wd, 0) for a, wd in zip(prep, widths)] + [_seg_blk(h, "gg", GDN_CHUNK)]
    scan_p = [sp["gdn_norm_w"]]
    o, states = _chain_fwd(f"gdn_scan_{tag}", _gdn_scan, T // GDN_CHUNK, scan_in, scan_p, [(GDN_CHUNK, GDN_WIDTH, mx)],
                           (GDN_WIDTH, GDN_HEAD_DIM))
    return o, dict(prep_in=prep_in, prep_p=prep_p, scan_in=scan_in, scan_p=scan_p, states=states, widths=widths)


def _gdn_backward(tag, do, sv, dx_dtype):
    T = do.shape[0]
    blk = GDN_PREP_CHUNKS * GDN_CHUNK
    dscan, (dnorm,) = _chain_bwd(f"gdn_scan_bwd_{tag}", _gdn_scan, T // GDN_CHUNK, sv["scan_in"], sv["scan_p"],
                                 [(do, GDN_CHUNK, GDN_WIDTH)], sprev=sv["states"], dx_dtypes=[F32] * 6 + [dx_dtype])
    douts = [(d, blk, wd) for d, wd in zip(dscan[:6], sv["widths"])]
    (dgqkv, dgab), (da_log, ddt_bias) = _chain_bwd(f"gdn_prep_bwd_{tag}", _gdn_prep, T // blk, sv["prep_in"], sv["prep_p"],
                                                   douts, dx_dtypes=[F32, dx_dtype])
    return dgqkv, dgab, dscan[6], da_log[:, :GDN_HEADS], ddt_bias[:, :GDN_HEADS], dnorm


def _gla_block(xs_, ps_, s_t):
    qkv, glr, r = xs_
    w2, gate_b, norm_w = ps_
    B = qkv.shape[0]
    H, K, V, C = GLA_HEADS, GLA_KEY_DIM, GLA_VAL_DIM, GLA_CHUNK
    q = qkv[:, :GLA_K] * (K ** -0.5)
    k = qkv[:, GLA_K:2 * GLA_K]
    v = qkv[:, 2 * GLA_K:]
    gk = jax.nn.log_sigmoid(_dot(glr, w2) + gate_b) / GLA_NORMALIZER
    row, col = _iota2((B, B), 0), _iota2((B, B), 1)
    same = (row // C) == (col // C)
    mask = same & (row >= col)
    b_cs = _mask_left(mask.astype(F32), gk)
    b_end = _mask_left((col == (row // C) * C + (C - 1)).astype(F32), b_cs)
    q_e = q * jnp.exp(b_cs)
    k_e = k * jnp.exp(-b_cs)
    k_d = k * jnp.exp(b_end - b_cs)
    intra = []
    for h in range(H):
        a_mat = jnp.where(mask, _dot(q_e[:, h * K:(h + 1) * K], k_e[:, h * K:(h + 1) * K], "nt"), 0.0)
        intra.append(_dot(a_mat, v[:, h * V:(h + 1) * V]))
    o = jnp.concatenate(intra, axis=1)
    chunks = [slice(j * C, (j + 1) * C) for j in range(B // C)]
    fresh = [jnp.concatenate([_dot(v[sl, h * V:(h + 1) * V], k_d[sl, h * K:(h + 1) * K], "tn") for h in range(H)], axis=1)
             for sl in chunks]
    entering = []
    for j, sl in enumerate(chunks):
        entering.append(s_t)
        s_t = s_t * jnp.exp(b_end[j * C:j * C + 1, :]) + fresh[j]
    inter = [jnp.concatenate([_dot(q_e[sl, h * K:(h + 1) * K], entering[j][:, h * K:(h + 1) * K], "nt") for h in range(H)],
                             axis=1) for j, sl in enumerate(chunks)]
    o = o + jnp.concatenate(inter, axis=0)
    outs = []
    for h in range(H):
        oh = o[:, h * V:(h + 1) * V]
        oh = oh * lax.rsqrt(jnp.mean(oh * oh, axis=1, keepdims=True) + RMS_EPS) * norm_w
        outs.append(oh * jax.nn.silu(r[:, h * V:(h + 1) * V]))
    return (jnp.concatenate(outs, axis=1),), s_t


def _merge_fn(xs_, ps_):
    gates, y_ssd, y_gdn, y_gla = xs_
    d = D_MODEL
    return (jax.nn.sigmoid(gates[:, :d]) * y_ssd + jax.nn.sigmoid(gates[:, d:2 * d]) * y_gdn
            + jax.nn.sigmoid(gates[:, 2 * d:]) * y_gla,)


def _ln_fn(xs_, ps_):
    x, r = xs_
    g, b = ps_
    t = ALPHA * x + r
    mu = jnp.mean(t, axis=1, keepdims=True)
    var = jnp.mean(jnp.square(t - mu), axis=1, keepdims=True)
    return ((t - mu) * lax.rsqrt(var + LN_EPS) * g + b,)


def _row_spec(rows, width, colblk, n, reverse):
    if reverse:
        return pl.BlockSpec((rows, width), lambda c: (n - 1 - c, colblk))
    return pl.BlockSpec((rows, width), lambda c: (c, colblk))


def _full_spec(shape):
    zeros = (0,) * len(shape)
    return pl.BlockSpec(shape, lambda c: zeros)


def _chain_fwd(name, fn, n, blocked, full, out_defs, state_shape=None):
    nb, nf, no = len(blocked), len(full), len(out_defs)

    def body(*refs):
        xs = [r[...].astype(F32) for r in refs[:nb]]
        ps = [r[...] for r in refs[nb:nb + nf]]
        o_refs = refs[nb + nf:nb + nf + no]
        if state_shape is None:
            outs = fn(xs, ps)
        else:
            sprev_ref, s_ref = refs[nb + nf + no:]

            @pl.when(pl.program_id(0) == 0)
            def _():
                s_ref[...] = jnp.zeros_like(s_ref)

            s = s_ref[...]
            sprev_ref[0] = s
            outs, s_new = fn(xs, ps, s)
            s_ref[...] = s_new
        for r, o in zip(o_refs, outs):
            r[...] = o.astype(r.dtype)

    in_specs = [_row_spec(rows, width, cb, n, False) for _, rows, width, cb in blocked]
    in_specs += [_full_spec(a.shape) for a in full]
    out_specs = [_row_spec(rows, width, 0, n, False) for rows, width, _ in out_defs]
    out_shape = [jax.ShapeDtypeStruct((n * rows, width), dt) for rows, width, dt in out_defs]
    scratch = []
    if state_shape is not None:
        out_specs.append(pl.BlockSpec((1,) + state_shape, lambda c: (c, 0, 0)))
        out_shape.append(jax.ShapeDtypeStruct((n,) + state_shape, F32))
        scratch.append(pltpu.VMEM(state_shape, F32))
    return pl.pallas_call(body, name=name, grid=(n,), in_specs=in_specs, out_specs=out_specs, out_shape=out_shape,
                          scratch_shapes=scratch, compiler_params=_cparams(("arbitrary",)))(
        *[a for a, _, _, _ in blocked], *full)


def _chain_bwd(name, fn, n, blocked, full, douts, sprev=None, dx_dtypes=None):
    nb, nf, nd = len(blocked), len(full), len(douts)
    has_state = sprev is not None
    dx_dtypes = dx_dtypes or [F32] * nb

    def body(*refs):
        pos = 0
        b_refs = refs[pos:pos + nb]; pos += nb
        f_refs = refs[pos:pos + nf]; pos += nf
        d_refs = refs[pos:pos + nd]; pos += nd
        if has_state:
            sprev_ref = refs[pos]; pos += 1
        dx_refs = refs[pos:pos + nb]; pos += nb
        dp_refs = refs[pos:pos + nf]; pos += nf
        if has_state:
            ds_ref = refs[pos]

        @pl.when(pl.program_id(0) == 0)
        def _():
            for r in dp_refs:
                r[...] = jnp.zeros_like(r)
            if has_state:
                ds_ref[...] = jnp.zeros_like(ds_ref)

        xs = [r[...].astype(F32) for r in b_refs]
        ps = [r[...] for r in f_refs]
        dys = tuple(r[...].astype(F32) for r in d_refs)
        if has_state:
            _, vjp = jax.vjp(fn, xs, ps, sprev_ref[0])
            dxs, dps, ds = vjp((dys, ds_ref[...]))
            ds_ref[...] = ds
        else:
            _, vjp = jax.vjp(fn, xs, ps)
            dxs, dps = vjp(dys)
        for r, d in zip(dx_refs, dxs):
            r[...] = d.astype(r.dtype)
        for r, d in zip(dp_refs, dps):
            r[...] += d

    in_specs = [_row_spec(rows, width, cb, n, True) for _, rows, width, cb in blocked]
    in_specs += [_full_spec(a.shape) for a in full]
    in_specs += [_row_spec(rows, width, 0, n, True) for _, rows, width in douts]
    args = [a for a, _, _, _ in blocked] + list(full) + [a for a, _, _ in douts]
    scratch = []
    if has_state:
        st_shape = sprev.shape[1:]
        in_specs.append(pl.BlockSpec((1,) + st_shape, lambda c: (n - 1 - c, 0, 0)))
        args.append(sprev)
        scratch.append(pltpu.VMEM(st_shape, F32))
    out_specs = [_row_spec(rows, width, 0, n, True) for _, rows, width, _ in blocked]
    out_specs += [_full_spec(a.shape) for a in full]
    out_shape = [jax.ShapeDtypeStruct((n * rows, width), dt) for (_, rows, width, _), dt in zip(blocked, dx_dtypes)]
    out_shape += [jax.ShapeDtypeStruct(a.shape, F32) for a in full]
    res = pl.pallas_call(body, name=name, grid=(n,), in_specs=in_specs, out_specs=out_specs, out_shape=out_shape,
                         scratch_shapes=scratch, compiler_params=_cparams(("arbitrary",)))(*args)
    return res[:nb], res[nb:]


def _tile(n, target, unit):
    if n <= target:
        return n
    best = None
    for t in range(unit, target + 1, unit):
        if n % t == 0:
            best = t
    assert best is not None, (n, target, unit)
    return best


def _mm(name, a, b, dims="nn", out_dtype=F32, tm=2048, tn=512, tk=2048, after=None):
    if dims == "nn":
        (M, K), (_, N) = a.shape, b.shape
    elif dims == "nt":
        (M, K), (N, _) = a.shape, b.shape
    else:
        (K, M), (_, N) = a.shape, b.shape
    tm, tn, tk = _tile(M, tm, LANES), _tile(N, tn, LANES), _tile(K, tk, LANES)
    nk = K // tk
    extra = [] if after is None else [after]

    def body(*refs):
        a_ref, b_ref = refs[:2]
        o_ref, acc_ref = refs[-2:]
        part = _dot(a_ref[...], b_ref[...], dims)
        if nk == 1:
            o_ref[...] = part.astype(o_ref.dtype)
            return
        k = pl.program_id(2)

        @pl.when(k == 0)
        def _():
            acc_ref[...] = part

        @pl.when((k > 0) & (k < nk - 1))
        def _():
            acc_ref[...] += part

        @pl.when(k == nk - 1)
        def _():
            o_ref[...] = (acc_ref[...] + part).astype(o_ref.dtype)

    if dims == "tn":
        a_spec = pl.BlockSpec((tk, tm), lambda j, i, k: (k, i))
    else:
        a_spec = pl.BlockSpec((tm, tk), lambda j, i, k: (i, k))
    if dims == "nt":
        b_spec = pl.BlockSpec((tn, tk), lambda j, i, k: (j, k))
    else:
        b_spec = pl.BlockSpec((tk, tn), lambda j, i, k: (k, j))
    return pl.pallas_call(
        body, name=name, grid=(N // tn, M // tm, nk), in_specs=[a_spec, b_spec] + [ANY] * len(extra),
        out_specs=pl.BlockSpec((tm, tn), lambda j, i, k: (i, j)), out_shape=jax.ShapeDtypeStruct((M, N), out_dtype),
        scratch_shapes=[pltpu.VMEM((tm, tn) if nk > 1 else (8, LANES), F32)],
        compiler_params=_cparams(("parallel", "parallel", "arbitrary")))(a, b, *extra)


CONV_CB = 256


def _shift_down(x, k):
    if k == 0:
        return x
    return jnp.where(_iota2(x.shape, 0) >= k, pltpu.roll(x, k, 0), 0.0)


def _shift_up(x, k):
    if k == 0:
        return x
    t = x.shape[0]
    return jnp.where(_iota2(x.shape, 0) < t - k, pltpu.roll(x, t - k, 0), 0.0)


def _conv_pre(x, w, b):
    kk = w.shape[0]
    pre = x * w[kk - 1:kk, :]
    for k in range(kk - 1):
        pre = pre + _shift_down(x, kk - 1 - k) * w[k:k + 1, :]
    return pre if b is None else pre + b


EDGE = 16


def _conv_pre_rot(x, w, b):
    kk = w.shape[0]
    pre = x * w[kk - 1:kk, :]
    for k in range(kk - 1):
        pre = pre + pltpu.roll(x, kk - 1 - k, 0) * w[k:k + 1, :]
    return pre if b is None else pre + b


def _conv_t_local(d, w):
    kk = w.shape[0]
    out = d * w[kk - 1:kk, :]
    for k in range(kk - 1):
        out = out + _shift_up(d, kk - 1 - k) * w[k:k + 1, :]
    return out


def _col_sum(a):
    return jnp.sum(a, axis=0, keepdims=True)


def _conv_bwd_rot(x_ref, w, dpre, dpre_head, dx_ref, dw_ref, db_ref):
    T = dpre.shape[0]
    kk = w.shape[0]
    x = x_ref[...]
    x_head, x_tail = x_ref[0:EDGE, :], x_ref[T - EDGE:T, :]
    wrong_head = dpre[0:EDGE]
    dx = dpre * w[kk - 1:kk, :]
    for k in range(kk - 1):
        dx = dx + pltpu.roll(dpre, T - (kk - 1 - k), 0) * w[k:k + 1, :]
    dx_ref[...] = dx.astype(dx_ref.dtype)
    top = jnp.concatenate([dpre_head, dpre[EDGE:2 * EDGE]], axis=0)
    dx_ref[0:EDGE, :] = _conv_t_local(top, w)[0:EDGE].astype(dx_ref.dtype)
    dx_ref[T - EDGE:T, :] = _conv_t_local(dpre[T - EDGE:T], w).astype(dx_ref.dtype)
    ends = jnp.concatenate([x_tail, x_head], axis=0)
    dw_ref[kk - 1:kk, :] = _col_sum(dpre * x) + _col_sum((dpre_head - wrong_head) * x_head)
    for k in range(kk - 1):
        s = kk - 1 - k
        rotated_head = pltpu.roll(ends, s, 0)[EDGE:2 * EDGE]
        dw_ref[k:k + 1, :] = (_col_sum(dpre * pltpu.roll(x, s, 0)) - _col_sum(wrong_head * rotated_head)
                              + _col_sum(dpre_head * _shift_down(x_head, s)))
    if db_ref is not None:
        db_ref[...] = _col_sum(dpre) + _col_sum(dpre_head - wrong_head)


def _dsilu(pre):
    sg = jax.nn.sigmoid(pre)
    return sg * (1.0 + pre * (1.0 - sg))


def _conv_silu_fwd(name, src, col0, w, b):
    T = src.shape[0]
    kk, C = w.shape
    cb = CONV_CB
    off = col0 // cb

    def body(*refs):
        x_ref, w_ref, o_ref = refs[0], refs[1], refs[-1]
        b_val = refs[2][...] if b is not None else None
        o_ref[...] = jax.nn.silu(_conv_pre_rot(x_ref[...], w_ref[...], b_val))
        o_ref[0:EDGE, :] = jax.nn.silu(_conv_pre(x_ref[0:EDGE, :], w_ref[...], b_val))

    in_specs = [pl.BlockSpec((T, cb), lambda j: (0, off + j)), pl.BlockSpec((kk, cb), lambda j: (0, j))]
    args = [src, w]
    if b is not None:
        in_specs.append(pl.BlockSpec((1, cb), lambda j: (0, j)))
        args.append(b)
    return pl.pallas_call(body, name=name, grid=(C // cb,), in_specs=in_specs,
                          out_specs=pl.BlockSpec((T, cb), lambda j: (0, j)), out_shape=jax.ShapeDtypeStruct((T, C), F32),
                          compiler_params=_cparams(("parallel",)))(*args)


def _conv_silu_bwd(name, src, col0, w, b, dy, dx_dtype):
    T = src.shape[0]
    kk, C = w.shape
    cb = CONV_CB
    off = col0 // cb
    has_b = b is not None

    def body(*refs):
        x_ref, w_ref = refs[:2]
        pos = 2
        b_val = None
        if has_b:
            b_val = refs[pos][...]; pos += 1
        dy_ref = refs[pos]; pos += 1
        dx_ref, dw_ref = refs[pos], refs[pos + 1]
        db_ref = refs[pos + 2] if has_b else None
        wv = w_ref[...]
        dpre = dy_ref[...] * _dsilu(_conv_pre_rot(x_ref[...], wv, b_val))
        dpre_head = dy_ref[0:EDGE, :] * _dsilu(_conv_pre(x_ref[0:EDGE, :], wv, b_val))
        _conv_bwd_rot(x_ref, wv, dpre, dpre_head, dx_ref, dw_ref, db_ref)

    in_specs = [pl.BlockSpec((T, cb), lambda j: (0, off + j)), pl.BlockSpec((kk, cb), lambda j: (0, j))]
    args = [src, w]
    if has_b:
        in_specs.append(pl.BlockSpec((1, cb), lambda j: (0, j)))
        args.append(b)
    in_specs.append(pl.BlockSpec((T, cb), lambda j: (0, j)))
    args.append(dy)
    out_specs = [pl.BlockSpec((T, cb), lambda j: (0, j)), pl.BlockSpec((kk, cb), lambda j: (0, j))]
    out_shape = [jax.ShapeDtypeStruct((T, C), dx_dtype), jax.ShapeDtypeStruct((kk, C), F32)]
    if has_b:
        out_specs.append(pl.BlockSpec((1, cb), lambda j: (0, j)))
        out_shape.append(jax.ShapeDtypeStruct((1, C), F32))
    return pl.pallas_call(body, name=name, grid=(C // cb,), in_specs=in_specs, out_specs=out_specs, out_shape=out_shape,
                          compiler_params=_cparams(("parallel",)))(*args)


def _ffn_glu_fwd(name, up, w, b, out_dtype=F32):
    T = up.shape[0]
    kk = w.shape[0]
    cb = CONV_CB
    width = up.shape[1] // 2
    nblk = width // cb

    def body(g_ref, u_ref, wg_ref, wu_ref, bg_ref, bu_ref, o_ref):
        g = _conv_pre_rot(g_ref[...], wg_ref[...], bg_ref[...])
        u = _conv_pre_rot(u_ref[...], wu_ref[...], bu_ref[...])
        o_ref[...] = (jax.nn.silu(g) * u).astype(o_ref.dtype)
        g = _conv_pre(g_ref[0:EDGE, :], wg_ref[...], bg_ref[...])
        u = _conv_pre(u_ref[0:EDGE, :], wu_ref[...], bu_ref[...])
        o_ref[0:EDGE, :] = (jax.nn.silu(g) * u).astype(o_ref.dtype)

    lo, hi = (lambda j: (0, j)), (lambda j: (0, nblk + j))
    in_specs = [pl.BlockSpec((T, cb), lo), pl.BlockSpec((T, cb), hi), pl.BlockSpec((kk, cb), lo), pl.BlockSpec((kk, cb), hi),
                pl.BlockSpec((1, cb), lo), pl.BlockSpec((1, cb), hi)]
    return pl.pallas_call(body, name=name, grid=(nblk,), in_specs=in_specs, out_specs=pl.BlockSpec((T, cb), lo),
                          out_shape=jax.ShapeDtypeStruct((T, width), out_dtype),
                          compiler_params=_cparams(("parallel",)))(up, up, w, w, b, b)


def _ffn_glu_bwd(name, up, w, b, dact, dx_dtype):
    T = up.shape[0]
    kk = w.shape[0]
    cb = CONV_CB
    width = up.shape[1] // 2
    nblk = width // cb

    def body(g_ref, u_ref, wg_ref, wu_ref, bg_ref, bu_ref, d_ref, dg_ref, du_ref, dwg_ref, dwu_ref, dbg_ref, dbu_ref):
        wg, wu = wg_ref[...], wu_ref[...]
        g = _conv_pre_rot(g_ref[...], wg, bg_ref[...])
        u = _conv_pre_rot(u_ref[...], wu, bu_ref[...])
        d = d_ref[...].astype(F32)
        g_head = _conv_pre(g_ref[0:EDGE, :], wg, bg_ref[...])
        u_head = _conv_pre(u_ref[0:EDGE, :], wu, bu_ref[...])
        d_head = d_ref[0:EDGE, :].astype(F32)
        _conv_bwd_rot(g_ref, wg, d * u * _dsilu(g), d_head * u_head * _dsilu(g_head), dg_ref, dwg_ref, dbg_ref)
        _conv_bwd_rot(u_ref, wu, d * jax.nn.silu(g), d_head * jax.nn.silu(g_head), du_ref, dwu_ref, dbu_ref)

    lo, hi = (lambda j: (0, j)), (lambda j: (0, nblk + j))
    in_specs = [pl.BlockSpec((T, cb), lo), pl.BlockSpec((T, cb), hi), pl.BlockSpec((kk, cb), lo), pl.BlockSpec((kk, cb), hi),
                pl.BlockSpec((1, cb), lo), pl.BlockSpec((1, cb), hi), pl.BlockSpec((T, cb), lo)]
    out_specs = [pl.BlockSpec((T, cb), lo)] * 2 + [pl.BlockSpec((kk, cb), lo)] * 2 + [pl.BlockSpec((1, cb), lo)] * 2
    out_shape = ([jax.ShapeDtypeStruct((T, width), dx_dtype)] * 2 + [jax.ShapeDtypeStruct((kk, width), F32)] * 2
                 + [jax.ShapeDtypeStruct((1, width), F32)] * 2)
    return pl.pallas_call(body, name=name, grid=(nblk,), in_specs=in_specs, out_specs=out_specs, out_shape=out_shape,
                          compiler_params=_cparams(("parallel",)))(up, up, w, w, b, b, dact)


def _loss_head(y, target):
    T, D = y.shape
    tb = _tile(T, 256, 8)

    def body(y_ref, t_ref, dy_ref, l_ref):
        @pl.when(pl.program_id(0) == 0)
        def _():
            l_ref[...] = jnp.zeros_like(l_ref)

        err = y_ref[...] - t_ref[...]
        dy_ref[...] = err * (1.0 / D)
        l_ref[...] += jnp.sum(err * err, axis=0, keepdims=True) * (0.5 / D)

    spec = pl.BlockSpec((tb, D), lambda i: (i, 0))
    return pl.pallas_call(body, name="loss_head", grid=(T // tb,), in_specs=[spec, spec],
                          out_specs=[spec, pl.BlockSpec((1, D), lambda i: (0, 0))],
                          out_shape=[jax.ShapeDtypeStruct((T, D), F32), jax.ShapeDtypeStruct((1, D), F32)],
                          compiler_params=_cparams(("arbitrary",)))(y, target)


def _adamw_math(w, g, m, v):
    m = ADAM_B1 * m + (1.0 - ADAM_B1) * g
    v = ADAM_B2 * v + (1.0 - ADAM_B2) * jnp.square(g)
    m_hat = m / (1.0 - ADAM_B1 ** ADAM_STEP)
    v_hat = v / (1.0 - ADAM_B2 ** ADAM_STEP)
    return -ADAM_LR * (m_hat / (jnp.sqrt(v_hat) + ADAM_EPS) + ADAM_WD * w), m, v


def _adamw(name, w, g, m, v, after=None):
    A, R, C = w.shape
    if C % LANES == 0:
        rb, cb = _slab(R, C)
    else:
        rb, cb = _tile(R, max(8, SLAB_BYTES // 2 // (C * 4) // 8 * 8), 8), C
    extra = [] if after is None else [after]

    def body(w_ref, g_ref, m_ref, v_ref, *rest):
        d_ref, mo_ref, vo_ref = rest[-3:]
        d, mn, vn = _adamw_math(w_ref[...], g_ref[...], m_ref[...], v_ref[...])
        d_ref[...] = d
        mo_ref[...] = mn
        vo_ref[...] = vn

    spec = pl.BlockSpec((1, rb, cb), lambda a, r, q: (a, r, q))
    return pl.pallas_call(body, name=name, grid=(A, R // rb, C // cb), in_specs=[spec] * 4 + [ANY] * len(extra),
                          out_specs=[spec] * 3, out_shape=[jax.ShapeDtypeStruct(w.shape, F32)] * 3,
                          compiler_params=_cparams(("parallel", "parallel", "parallel")))(w, g, m, v, *extra)


def _adamw_small(parts, w, m, v):
    def body(p_ref, w_ref, m_ref, v_ref, g_ref, d_ref, mo_ref, vo_ref):
        g = p_ref[0]
        for i in range(1, N_DEV):
            g = g + p_ref[i]
        d, mn, vn = _adamw_math(w_ref[...], g, m_ref[...], v_ref[...])
        g_ref[...] = g
        d_ref[...] = d
        mo_ref[...] = mn
        vo_ref[...] = vn

    return pl.pallas_call(body, name="adamw_small", out_shape=[jax.ShapeDtypeStruct(w.shape, F32)] * 4,
                          compiler_params=_cparams())(parts, w, m, v)


def _add_blocks(name, a, b, out_dtype=F32):
    n, R, W = a.shape
    rb = _tile(R, 512, 8)

    def body(a_ref, b_ref, o_ref):
        o_ref[...] = (a_ref[...].astype(F32) + b_ref[...].astype(F32)).astype(o_ref.dtype)

    spec = pl.BlockSpec((1, rb, W), lambda i, r: (i, r, 0))
    return pl.pallas_call(body, name=name, grid=(n, R // rb), in_specs=[spec, spec], out_specs=spec,
                          out_shape=jax.ShapeDtypeStruct(a.shape, out_dtype),
                          compiler_params=_cparams(("parallel", "parallel")))(a, b)


SLAB_BYTES = 5 << 19


def _slab(R, W):
    if R % 16 == 0:
        return _tile(R, max(16, SLAB_BYTES // (4 * W) // 16 * 16), 16), W
    assert W % LANES == 0, (R, W)
    return R, _tile(W, max(LANES, SLAB_BYTES // (4 * R) // LANES * LANES), LANES)


def _pair_add(name, g, other, c, chip):
    _, R, W = g.shape
    rb, cb = _slab(R, W)

    def body(s_ref, a_ref, b_ref, send_ref, own_ref):
        s = a_ref[0] + b_ref[0]
        send_ref[0] = s.astype(send_ref.dtype)

        @pl.when(pl.program_id(2) == s_ref[1])
        def _():
            own_ref[...] = s

    grid_spec = pltpu.PrefetchScalarGridSpec(
        num_scalar_prefetch=1, grid=(R // rb, W // cb, 4),
        in_specs=[pl.BlockSpec((1, rb, cb), lambda r, q, p, s_ref: (2 * p + s_ref[0], r, q)),
                  pl.BlockSpec((1, rb, cb), lambda r, q, p, s_ref: (p, r, q))],
        out_specs=[pl.BlockSpec((1, rb, cb), lambda r, q, p, s_ref: (p, r, q)),
                   pl.BlockSpec((rb, cb), lambda r, q, p, s_ref: (r, q))])
    scalars = jnp.stack([c, chip]).astype(jnp.int32)
    return pl.pallas_call(body, name=name, grid_spec=grid_spec,
                          out_shape=[jax.ShapeDtypeStruct((4, R, W), MXU_DTYPE), jax.ShapeDtypeStruct((R, W), F32)],
                          compiler_params=_cparams(("parallel", "parallel", "arbitrary")))(scalars, g, other)


def _sum4(name, own, parts):
    R, W = own.shape
    rb, cb = _slab(R, W)

    def body(o_ref, p_ref, out_ref):
        out_ref[...] = ((o_ref[...] + p_ref[0].astype(F32)) + p_ref[1].astype(F32)) + p_ref[2].astype(F32)

    return pl.pallas_call(body, name=name, grid=(R // rb, W // cb),
                          in_specs=[pl.BlockSpec((rb, cb), lambda r, q: (r, q)), pl.BlockSpec((3, rb, cb), lambda r, q: (0, r, q))],
                          out_specs=pl.BlockSpec((rb, cb), lambda r, q: (r, q)), out_shape=jax.ShapeDtypeStruct((R, W), F32),
                          compiler_params=_cparams(("parallel", "parallel")))(own, parts)


MESH = pl.DeviceIdType.MESH
ANY = pl.BlockSpec(memory_space=pl.ANY)


def _place():
    return lax.axis_index("x"), lax.axis_index("y"), lax.axis_index("c")


def _other_chips(x, y):
    return [(1 - x, y), (x, 1 - y), (1 - x, 1 - y)]


def _all_gather(name, blocks):
    n = len(blocks)

    def body(*refs):
        x_refs, out_refs = refs[:n], refs[n:2 * n]
        send_sems, recv_sems, local_sems = refs[2 * n:]
        x, y, c = _place()
        me, sibling = (x, y, c), (x, y, 1 - c)
        chips = _other_chips(x, y)

        def slot(a, px, py, pc):
            return out_refs[a].at[4 * px + 2 * py + pc]

        def copy(a, k, blk, to, src=None):
            return pltpu.make_async_remote_copy(src_ref=slot(a, *blk) if src is None else src, dst_ref=slot(a, *blk),
                                                send_sem=send_sems.at[a, k], recv_sem=recv_sems.at[a, k],
                                                device_id=to, device_id_type=MESH)

        mine = [pltpu.make_async_copy(x_refs[a], slot(a, *me), local_sems.at[a]) for a in range(n)]
        for cp in mine:
            cp.start()
        first = []
        for j, chip in enumerate(chips):
            first += [copy(a, 1 + j, me, (*chip, c), src=x_refs[a]) for a in range(n)]
        first += [copy(a, 0, me, sibling, src=x_refs[a]) for a in range(n)]
        for cp in first:
            cp.start()
        passed = []
        for j, chip in enumerate(chips):
            for a in range(n):
                copy(a, 1 + j, (*chip, c), me).wait_recv()
                passed.append(copy(a, 4 + j, (*chip, c), sibling))
                passed[-1].start()
        for a in range(n):
            copy(a, 0, sibling, me).wait_recv()
        for j, chip in enumerate(chips):
            for a in range(n):
                copy(a, 4 + j, (*chip, 1 - c), me).wait_recv()
        for cp in first + passed:
            cp.wait_send()
        for cp in mine:
            cp.wait()

    return pl.pallas_call(body, name=name, in_specs=[ANY] * n, out_specs=[ANY] * n,
                          out_shape=[jax.ShapeDtypeStruct((N_DEV,) + b.shape, b.dtype) for b in blocks],
                          scratch_shapes=[pltpu.SemaphoreType.DMA((n, 7)), pltpu.SemaphoreType.DMA((n, 7)),
                                          pltpu.SemaphoreType.DMA((n,))])(*blocks)


def _routes_to_sibling(x, y, c):
    return [(2 * p + (1 - c), p, (x, y, 1 - c)) for p in range(4)]


def _routes_to_chips(x, y, c):
    return [(2 * px + py, j, (px, py, c)) for j, (px, py) in enumerate(_other_chips(x, y))]


def _routes_block_to_chips(x, y, c):
    me = 4 * x + 2 * y + c
    return [(me, me, (px, py, c)) for px, py in _other_chips(x, y)]


def _routes_blocks_to_sibling(x, y, c):
    return [(4 * px + 2 * py + c, 4 * px + 2 * py + c, (x, y, 1 - c)) for px, py in [(x, y)] + _other_chips(x, y)]


def _route_copies(routes, src_refs, land_refs, send_sems, recv_sems):
    x, y, c = _place()
    copies = []
    for a, (src, land) in enumerate(zip(src_refs, land_refs)):
        plan = routes(x, y, c)
        for k, (s, d, target) in enumerate(plan):
            i = a * len(plan) + k
            copies.append(pltpu.make_async_remote_copy(src_ref=src.at[s], dst_ref=land.at[d], send_sem=send_sems.at[i],
                                                       recv_sem=recv_sems.at[i], device_id=target, device_id_type=MESH))
    return copies


def _exchange(name, routes, n_routes, srcs, land_slots):
    n = len(srcs)

    def body(*refs):
        copies = _route_copies(routes, refs[:n], refs[n:2 * n], refs[2 * n], refs[2 * n + 1])
        for cp in copies:
            cp.start()
        for cp in copies:
            cp.wait_recv()
        for cp in copies:
            cp.wait_send()

    return pl.pallas_call(body, name=name, in_specs=[ANY] * n, out_specs=[ANY] * n,
                          out_shape=[jax.ShapeDtypeStruct((land_slots,) + s.shape[1:], s.dtype) for s in srcs],
                          scratch_shapes=[pltpu.SemaphoreType.DMA((n * n_routes,)), pltpu.SemaphoreType.DMA((n * n_routes,))])(*srcs)


HBM_SPEC = pl.BlockSpec(memory_space=pltpu.HBM)
SEM_SPEC = pl.BlockSpec(memory_space=pltpu.SEMAPHORE)
DATAFLOW = pltpu.SideEffectType.DATAFLOW_SIDE_EFFECTING


def _exchange_start(name, routes, n_routes, srcs, lands, after=None):
    n = len(srcs)
    in_place = lands is None
    bufs = list(srcs) + ([] if in_place else list(lands))
    nb = len(bufs)
    extra = [] if after is None else [after]

    def body(*refs):
        src_refs = refs[:n]
        land_refs = src_refs if in_place else refs[n:nb]
        send_sems, recv_sems = refs[nb + len(extra)], refs[nb + len(extra) + 1]
        token = refs[-1]
        for cp in _route_copies(routes, src_refs, land_refs, send_sems, recv_sems):
            cp.start()
        token[...] = jnp.zeros_like(token)

    sems = [pltpu.SemaphoreType.DMA((n * n_routes,)), pltpu.SemaphoreType.DMA((n * n_routes,))]
    out = pl.pallas_call(
        body, name=name, in_specs=[HBM_SPEC] * nb + [ANY] * len(extra),
        out_shape=sems + [pltpu.HBM(b.shape, b.dtype) for b in bufs] + [jax.ShapeDtypeStruct((8, LANES), F32)],
        out_specs=[SEM_SPEC, SEM_SPEC] + [HBM_SPEC] * nb + [pl.BlockSpec(memory_space=pltpu.VMEM)],
        input_output_aliases={i: 2 + i for i in range(nb)},
        compiler_params=pltpu.CompilerParams(has_side_effects=DATAFLOW))(
        *[pltpu.with_memory_space_constraint(b, pltpu.HBM) for b in bufs], *extra)
    return (out[0], out[1], list(out[2:2 + nb])), out[-1]


def _exchange_wait(name, routes, n_routes, n, started, after):
    send_sems, recv_sems, bufs = started
    nb = len(bufs)
    in_place = nb == n

    def body(*refs):
        src_refs = refs[:n]
        land_refs = src_refs if in_place else refs[n:nb]
        for cp in _route_copies(routes, src_refs, land_refs, refs[nb], refs[nb + 1]):
            cp.wait_send()
            cp.wait_recv()

    out = pl.pallas_call(
        body, name=name, in_specs=[HBM_SPEC] * nb + [SEM_SPEC, SEM_SPEC, ANY],
        out_shape=[pltpu.HBM(b.shape, b.dtype) for b in bufs], out_specs=[HBM_SPEC] * nb,
        input_output_aliases={i: i for i in range(nb)},
        compiler_params=pltpu.CompilerParams(has_side_effects=DATAFLOW))(*bufs, send_sems, recv_sems, after)
    return list(out[:n]) if in_place else list(out[n:])


def _pair_sums(tag, gs, from_sibling):
    x, y, c = _place()
    return [_pair_add(f"rs_add_{tag}_{i}", g, o, c, 2 * x + y) for i, (g, o) in enumerate(zip(gs, from_sibling))]


def _reduce_scatter(tag, gs):
    sums = _pair_sums(tag, gs, _exchange(f"rs_swap_{tag}", _routes_to_sibling, 4, gs, 4))
    got = _exchange(f"rs_chips_{tag}", _routes_to_chips, 3, [s[0] for s in sums], 3)
    return [_sum4(f"rs_sum_{tag}_{i}", s[1], q) for i, (s, q) in enumerate(zip(sums, got))]


def _reduce_scatter_begin(tag, gs):
    lands = [lax.empty((4,) + g.shape[1:], g.dtype) for g in gs]
    swap, token = _exchange_start(f"rs_swap_{tag}_start", _routes_to_sibling, 4, gs, lands)
    return dict(tag=tag, gs=gs, swap=swap), token


def _reduce_scatter_middle(state, after):
    tag, gs = state["tag"], state["gs"]
    from_sibling = _exchange_wait(f"rs_swap_{tag}_wait", _routes_to_sibling, 4, len(gs), state["swap"], after)
    state["sums"] = _pair_sums(tag, gs, from_sibling)
    partials = [s[0] for s in state["sums"]]
    lands = [lax.empty((3,) + p.shape[1:], p.dtype) for p in partials]
    state["chips"], token = _exchange_start(f"rs_chips_{tag}_start", _routes_to_chips, 3, partials, lands)
    return token


def _reduce_scatter_end(state, after):
    tag = state["tag"]
    got = _exchange_wait(f"rs_chips_{tag}_wait", _routes_to_chips, 3, len(state["gs"]), state["chips"], after)
    return [_sum4(f"rs_sum_{tag}_{i}", s[1], q) for i, (s, q) in enumerate(zip(state["sums"], got))]


def _all_gather_begin(tag, blocks, after):
    dev = 4 * lax.axis_index("x") + 2 * lax.axis_index("y") + lax.axis_index("c")
    zones = [lax.dynamic_update_slice_in_dim(lax.empty((N_DEV,) + b.shape, b.dtype), b[None], dev, axis=0) for b in blocks]
    chips, token = _exchange_start(f"gather_{tag}_chips_start", _routes_block_to_chips, 3, zones, None, after)
    return dict(tag=tag, n=len(blocks), chips=chips), token


def _all_gather_middle(state, after):
    tag, n = state["tag"], state["n"]
    zones = _exchange_wait(f"gather_{tag}_chips_wait", _routes_block_to_chips, 3, n, state["chips"], after)
    state["sibling"], token = _exchange_start(f"gather_{tag}_sibling_start", _routes_blocks_to_sibling, 4, zones, None)
    return token


def _all_gather_end(state, after):
    return _exchange_wait(f"gather_{state['tag']}_sibling_wait", _routes_blocks_to_sibling, 4, state["n"], state["sibling"], after)


PACK_UNIT = 8 * LANES


def _packed_size(shape):
    return -(-math.prod(shape) // PACK_UNIT) * PACK_UNIT


def _pack(arrays, dtype):
    parts = []
    for a in arrays:
        flat = a.reshape(-1).astype(dtype)
        parts.append(jnp.pad(flat, (0, _packed_size(a.shape) - flat.shape[0])))
    return jnp.concatenate(parts).reshape(-1, LANES)


def _unpack(flat, shapes, lead=()):
    flat = flat.reshape(lead + (-1,))
    out, pos = [], 0
    for s in shapes:
        out.append(flat[..., pos:pos + math.prod(s)].reshape(lead + tuple(s)))
        pos += _packed_size(s)
    return out


def _ffn_pad_rows(a):
    n = a.shape[0] // FFN_HALF
    a = jnp.pad(a.reshape(n, FFN_HALF, a.shape[1]), ((0, 0), (0, FFN_HALF_PAD - FFN_HALF), (0, 0)))
    return a.reshape(n * FFN_HALF_PAD, a.shape[2])


def _ffn_unpad_rows(a):
    n = a.shape[0] // FFN_HALF_PAD
    return a.reshape(n, FFN_HALF_PAD, a.shape[1])[:, :FFN_HALF].reshape(n * FFN_HALF, a.shape[1])


def _ffn_pad_cols(a):
    n = a.shape[1] // FFN_HALF
    a = jnp.pad(a.reshape(a.shape[0], n, FFN_HALF), ((0, 0), (0, 0), (0, FFN_HALF_PAD - FFN_HALF)))
    return a.reshape(a.shape[0], n * FFN_HALF_PAD)


def _ffn_unpad_cols(a):
    n = a.shape[1] // FFN_HALF_PAD
    return a.reshape(a.shape[0], n, FFN_HALF_PAD)[:, :, :FFN_HALF].reshape(a.shape[0], n * FFN_HALF)


def _shard_to_send(name, shard):
    if name == "w_in":
        shard = shard.T
    elif name == "ffn_w_up":
        shard = _ffn_pad_rows(shard.T)
    return shard.astype(MXU_DTYPE)


def _whole_from_gathered(name, g):
    if name == "w_in":
        return _pad_in_proj_rows(g.reshape(IN_DIM, g.shape[2]))
    if name in ("w_br_gdn", "w_br_gla"):
        return jnp.transpose(g, (1, 0, 2)).reshape(g.shape[1], N_DEV * g.shape[2])
    if name == "ffn_w_down":
        return jnp.pad(g, ((0, 0), (0, FFN_HALF_PAD - FFN_HALF), (0, 0))).reshape(FFN_PAD, g.shape[2])
    return g.reshape(N_DEV * g.shape[1], g.shape[2])


def _slots_from_whole(name, gw):
    if name == "w_in":
        return _unpad_in_proj_rows(gw).reshape(N_DEV, IN_DIM // N_DEV, gw.shape[1])
    if name in ("w_br_gdn", "w_br_gla"):
        return jnp.transpose(gw.reshape(gw.shape[0], N_DEV, gw.shape[1] // N_DEV), (1, 0, 2))
    return gw.reshape(N_DEV, gw.shape[0] // N_DEV, gw.shape[1])


def _shard_from_slot(name, s):
    if name == "ffn_w_up":
        return _ffn_unpad_rows(s)
    if name == "ffn_w_down":
        return s[:FFN_HALF]
    return s


def _in_proj_pieces():
    starts, pos = {}, 0
    for n, width in IN_SPLITS:
        starts[n] = (pos, width)
        pos += width
    return [(starts[ref][0], off + lane, starts[ref][1]) for _, off, _, pieces in PAD_SEGS for ref, lane in pieces]


def _pad_in_proj_rows(w):
    rows, at = [], 0
    for src, dst, n in sorted(_in_proj_pieces(), key=lambda p: p[1]):
        if dst > at:
            rows.append(jnp.zeros((dst - at, w.shape[1]), w.dtype))
        rows.append(w[src:src + n])
        at = dst + n
    rows.append(jnp.zeros((IN_PAD - at, w.shape[1]), w.dtype))
    return jnp.concatenate(rows, axis=0)


def _unpad_in_proj_rows(wp):
    return jnp.concatenate([wp[dst:dst + n] for _, dst, n in sorted(_in_proj_pieces())], axis=0)


def _lane_pad(a, width=LANES):
    return jnp.pad(a, ((0, 0), (0, width - a.shape[1])))


def _seg_blk(h, name, rows):
    off, width = SEG[name]
    return (h, rows, width, off // width)


def _ln_both(xs_, ps_):
    (y,) = _ln_fn(xs_, ps_)
    return (y, y)


def _behind(param, hooks, stage, *seen):
    if hooks is None or stage not in hooks:
        return param
    token = hooks[stage](*seen)
    return param if token is None else param + token[0:1, 0:1]


def _layer_fwd(l, x, x_mx, W, sp, hooks=None):
    T = x.shape[0]
    n64, ngla, ntok = T // SSD_CHUNK, T // GLA_BLOCK, T // 256
    h = _mm(f"in_proj_{l}", x_mx, W["w_in"], "nt")
    xbc = _conv_silu_fwd(f"ssd_conv_{l}", h, SEG["xbc"][0], sp["ssd_conv_w"], sp["ssd_conv_b"])
    gqkv = _conv_silu_fwd(f"gdn_conv_{l}", h, SEG["gqkv"][0], sp["gdn_conv_w"], None)

    ssd_in = [(xbc, SSD_CHUNK, SSD_XBC, 0), _seg_blk(h, "dt", SSD_CHUNK), _seg_blk(h, "z", SSD_CHUNK)]
    ssd_p = [sp["ssd_dt_bias"], sp["ssd_a_log"], sp["ssd_d"], sp["ssd_norm_w"]]
    o_ssd, ssd_states = _chain_fwd(f"ssd_fwd_{l}", _ssd_chunk, n64, ssd_in, ssd_p, [(SSD_CHUNK, SSD_INNER, MXU_DTYPE)],
                                   (SSD_STATE, SSD_INNER))
    o_gdn, gdn_saved = _gdn_forward(str(l), gqkv, h, dict(sp, gdn_a_log=_behind(sp["gdn_a_log"], hooks, "ssd", o_ssd)))
    gla_in = [_seg_blk(h, "lqkv", GLA_BLOCK), _seg_blk(h, "lglr", GLA_BLOCK), _seg_blk(h, "lr", GLA_BLOCK)]
    gla_p = [jnp.pad(sp["gla_gate_w2"], ((0, LANES - GLA_RANK), (0, 0))), sp["gla_gate_b"], sp["gla_norm_w"]]
    o_gla, gla_states = _chain_fwd(f"gla_fwd_{l}", _gla_block, ngla, gla_in, gla_p, [(GLA_BLOCK, GLA_V, MXU_DTYPE)],
                                   (GLA_VAL_DIM, GLA_K))
    ln1_p = [_behind(sp["ln1_g"], hooks, "mixed", o_gdn), sp["ln1_b"]]
    y_ssd = _mm(f"br_ssd_{l}", o_ssd, W["w_br_ssd"])
    y_gdn = _mm(f"br_gdn_{l}", o_gdn, W["w_br_gdn"])
    y_gla = _mm(f"br_gla_{l}", o_gla, W["w_br_gla"])
    merge_in = [_seg_blk(h, "gates", 256), (y_ssd, 256, D_MODEL, 0), (y_gdn, 256, D_MODEL, 0), (y_gla, 256, D_MODEL, 0)]
    (mix,) = _chain_fwd(f"merge_{l}", _merge_fn, ntok, merge_in, [], [(256, D_MODEL, MXU_DTYPE)])
    r1 = _mm(f"out_proj_{l}", mix, W["w_out"])
    both = [(256, D_MODEL, F32), (256, D_MODEL, MXU_DTYPE)]
    x1, x1_mx = _chain_fwd(f"ln1_{l}", _ln_both, ntok, [(x, 256, D_MODEL, 0), (r1, 256, D_MODEL, 0)], ln1_p, both)
    up = _mm(f"ffn_up_{l}", x1_mx, W["ffn_w_up"], "nt")
    act = _ffn_glu_fwd(f"ffn_glu_{l}", up, sp["ffn_conv_w_pad"], sp["ffn_conv_b_pad"], MXU_DTYPE)
    ln2_p = [_behind(sp["ln2_g"], hooks, "ffn_act", act), sp["ln2_b"]]
    r2 = _mm(f"ffn_down_{l}", act, W["ffn_w_down"])
    x2, x2_mx = _chain_fwd(f"ln2_{l}", _ln_both, ntok, [(x1, 256, D_MODEL, 0), (r2, 256, D_MODEL, 0)], ln2_p, both)
    saved = dict(x=x, x_mx=x_mx, h=h, xbc=xbc, gqkv=gqkv, ssd_in=ssd_in, ssd_p=ssd_p, ssd_states=ssd_states,
                 gdn=gdn_saved, gla_in=gla_in, gla_p=gla_p, gla_states=gla_states, o_ssd=o_ssd,
                 o_gdn=o_gdn, o_gla=o_gla, merge_in=merge_in, mix=mix, r1=r1, ln1_p=ln1_p, x1=x1, x1_mx=x1_mx, up=up, act=act,
                 r2=r2, ln2_p=ln2_p)
    return x2, x2_mx, saved


def _layer_bwd(l, dx2, W, sp, sv, hooks=None):
    T = dx2.shape[0]
    n64, ngla, ntok = T // SSD_CHUNK, T // GLA_BLOCK, T // 256
    bf = MXU_DTYPE
    gw, gs = {}, {}
    ln2_p = [_behind(sv["ln2_p"][0], hooks, "start"), sv["ln2_p"][1]]
    (dx1_a, dr2), (gs["ln2_g"], gs["ln2_b"]) = _chain_bwd(
        f"ln2_bwd_{l}", _ln_fn, ntok, [(sv["x1"], 256, D_MODEL, 0), (sv["r2"], 256, D_MODEL, 0)], ln2_p,
        [(dx2, 256, D_MODEL)], dx_dtypes=[F32, bf])
    gw["ffn_w_down"] = _mm(f"ffn_down_dw_{l}", sv["act"], dr2, "tn")
    dact = _mm(f"ffn_down_dx_{l}", dr2, W["ffn_w_down"], "nt")
    dg, du, dwg, dwu, dbg, dbu = _ffn_glu_bwd(f"ffn_glu_bwd_{l}", sv["up"], sp["ffn_conv_w_pad"], sp["ffn_conv_b_pad"], dact, bf)
    gs["ffn_conv_w"] = _ffn_unpad_cols(jnp.concatenate([dwg, dwu], axis=1))
    gs["ffn_conv_b"] = _ffn_unpad_cols(jnp.concatenate([dbg, dbu], axis=1))
    dup = jnp.concatenate([dg, du], axis=1)
    gw["ffn_w_up"] = _mm(f"ffn_up_dw_{l}", dup, sv["x1_mx"], "tn", tn=1024)
    dx1_b = _mm(f"ffn_up_dx_{l}", dup, W["ffn_w_up"], "nn", tn=1024, tk=1024)
    ln1_p = [_behind(sv["ln1_p"][0], hooks, "ffn", dx1_b), sv["ln1_p"][1]]
    (dx_a, dr1), (gs["ln1_g"], gs["ln1_b"]) = _chain_bwd(
        f"ln1_bwd_{l}", _ln_sum_fn, ntok, [(sv["x"], 256, D_MODEL, 0), (sv["r1"], 256, D_MODEL, 0)], ln1_p,
        [(dx1_a, 256, D_MODEL), (dx1_b, 256, D_MODEL)], dx_dtypes=[F32, bf])
    gw["w_out"] = _mm(f"out_proj_dw_{l}", sv["mix"], dr1, "tn")
    dmix = _mm(f"out_proj_dx_{l}", dr1, W["w_out"], "nt")
    (dgates, dy_ssd, dy_gdn, dy_gla), _ = _chain_bwd(f"merge_bwd_{l}", _merge_fn, ntok, sv["merge_in"], [],
                                                     [(dmix, 256, D_MODEL)], dx_dtypes=[bf, bf, bf, bf])
    gw["w_br_ssd"] = _mm(f"br_ssd_dw_{l}", sv["o_ssd"], dy_ssd, "tn")
    gw["w_br_gdn"] = _mm(f"br_gdn_dw_{l}", sv["o_gdn"], dy_gdn, "tn")
    gw["w_br_gla"] = _mm(f"br_gla_dw_{l}", sv["o_gla"], dy_gla, "tn")
    do_ssd = _mm(f"br_ssd_dx_{l}", dy_ssd, W["w_br_ssd"], "nt")
    do_gdn = _mm(f"br_gdn_dx_{l}", dy_gdn, W["w_br_gdn"], "nt")
    do_gla = _mm(f"br_gla_dx_{l}", dy_gla, W["w_br_gla"], "nt")

    ssd_p = [_behind(sv["ssd_p"][0], hooks, "branches", do_gla, gw)] + list(sv["ssd_p"][1:])
    (dxbc, ddt, dz), dps = _chain_bwd(f"ssd_bwd_{l}", _ssd_chunk, n64, sv["ssd_in"], ssd_p,
                                      [(do_ssd, SSD_CHUNK, SSD_INNER)], sprev=sv["ssd_states"], dx_dtypes=[F32, bf, bf])
    gs["ssd_dt_bias"], gs["ssd_a_log"], gs["ssd_d"], gs["ssd_norm_w"] = dps
    gdn_sv = dict(sv["gdn"], scan_p=[_behind(sv["gdn"]["scan_p"][0], hooks, "ssd", dz)])
    dgqkv, dgab, dgg, gs["gdn_a_log"], gs["gdn_dt_bias"], gs["gdn_norm_w"] = _gdn_backward(str(l), do_gdn, gdn_sv, bf)
    (dlqkv, dlglr, dlr), dps = _chain_bwd(f"gla_bwd_{l}", _gla_block, ngla, sv["gla_in"], sv["gla_p"],
                                          [(do_gla, GLA_BLOCK, GLA_V)], sprev=sv["gla_states"], dx_dtypes=[bf, bf, bf])
    gs["gla_gate_w2"], gs["gla_gate_b"], gs["gla_norm_w"] = dps[0][:GLA_RANK], dps[1], dps[2]
    dxbc_pre, gs["ssd_conv_w"], gs["ssd_conv_b"] = _conv_silu_bwd(
        f"ssd_conv_bwd_{l}", sv["h"], SEG["xbc"][0], sp["ssd_conv_w"], sp["ssd_conv_b"], dxbc, bf)
    dgqkv_pre, gs["gdn_conv_w"] = _conv_silu_bwd(f"gdn_conv_bwd_{l}", sv["h"], SEG["gqkv"][0], sp["gdn_conv_w"], None, dgqkv, bf)
    pieces = dict(gates=dgates, xbc=dxbc_pre, gqkv=dgqkv_pre, z=dz, lqkv=dlqkv, gg=dgg, lr=dlr, dt=ddt, gab=dgab, lglr=dlglr)
    cols = [pieces[name] for name, _, _, _ in PAD_SEGS]
    cols.append(jnp.zeros((T, IN_PAD - PAD_SEGS[-1][1] - PAD_SEGS[-1][2]), bf))
    dh = jnp.concatenate(cols, axis=1)
    gw["w_in"] = _mm(f"in_proj_dw_{l}", dh, sv["x_mx"], "tn", tn=1024)
    behind = hooks["w_in_grad"](gw) if hooks is not None and "w_in_grad" in hooks else None
    dx_b = _mm(f"in_proj_dx_{l}", dh, W["w_in"], "nn", tm=1024, tn=1024, tk=IN_PAD // 4, after=behind)
    dx = _add_blocks(f"dx_add_{l}", dx_a[None], dx_b[None])[0]
    return dx, gw, gs


def _ln_sum_fn(xs_, ps_):
    (y,) = _ln_fn(xs_, ps_)
    return (y, y)


def _small_2d(name, a):
    return a.reshape(1, -1) if a.ndim == 1 else a


def kernel(x, w_in, ssd_conv_w, ssd_conv_b, ssd_dt_bias, ssd_a_log, ssd_d, ssd_norm_w, gdn_conv_w, gdn_a_log, gdn_dt_bias, gdn_norm_w, gla_gate_w2, gla_gate_b, gla_norm_w, w_br_ssd, w_br_gdn, w_br_gla, w_out, ln1_g, ln1_b, ffn_w_up, ffn_conv_w, ffn_conv_b, ffn_w_down, ln2_g, ln2_b, loss_target, m_w_in, m_ssd_conv_w, m_ssd_conv_b, m_ssd_dt_bias, m_ssd_a_log, m_ssd_d, m_ssd_norm_w, m_gdn_conv_w, m_gdn_a_log, m_gdn_dt_bias, m_gdn_norm_w, m_gla_gate_w2, m_gla_gate_b, m_gla_norm_w, m_w_br_ssd, m_w_br_gdn, m_w_br_gla, m_w_out, m_ln1_g, m_ln1_b, m_ffn_w_up, m_ffn_conv_w, m_ffn_conv_b, m_ffn_w_down, m_ln2_g, m_ln2_b, v_w_in, v_ssd_conv_w, v_ssd_conv_b, v_ssd_dt_bias, v_ssd_a_log, v_ssd_d, v_ssd_norm_w, v_gdn_conv_w, v_gdn_a_log, v_gdn_dt_bias, v_gdn_norm_w, v_gla_gate_w2, v_gla_gate_b, v_gla_norm_w, v_w_br_ssd, v_w_br_gdn, v_w_br_gla, v_w_out, v_ln1_g, v_ln1_b, v_ffn_w_up, v_ffn_conv_w, v_ffn_conv_b, v_ffn_w_down, v_ln2_g, v_ln2_b):
    args = locals()
    w = {n: args[n] for n in WEIGHTS}
    m = {n: args["m_" + n] for n in WEIGHTS}
    v = {n: args["v_" + n] for n in WEIGHTS}
    dev = 4 * lax.axis_index("x") + 2 * lax.axis_index("y") + lax.axis_index("c")
    xl = x[0]
    tgt = loss_target[0]

    late = BIG[1:]

    def send(names, l):
        return [_shard_to_send(n, w[n][l]) for n in names]

    def whole_weights(names, got):
        return {n: _whole_from_gathered(n, g) for n, g in zip(names, got)}

    got0 = _all_gather("gather_first", send(BIG[:1], 0) + [w[n] for n in SMALL_SHARDED])
    gather0, token0 = _all_gather_begin("w_0", send(late, 0), got0[0])
    W = [whole_weights(BIG[:1], got0[:1]), None]
    whole = dict(w)
    for n, s in zip(SMALL_SHARDED, got0[1:]):
        whole[n] = jnp.transpose(s, (1, 2, 0, 3)).reshape(s.shape[1], s.shape[2], N_DEV * s.shape[3])
    SP = [{n: _small_2d(n, whole[n][l]) for n in SMALL} for l in range(DEPTH)]
    for sp in SP:
        sp["ffn_conv_w_pad"] = _ffn_pad_cols(sp["ffn_conv_w"])
        sp["ffn_conv_b_pad"] = _ffn_pad_cols(sp["ffn_conv_b"])

    held = {}

    def late_weights_cross(o_ssd):
        token = _all_gather_middle(gather0, o_ssd)
        held["gather1"], token1 = _all_gather_begin("w_1", send(BIG, 1), o_ssd)
        return token + token1

    def late_weights_arrive(mixed):
        W[0].update(whole_weights(late, _all_gather_end(gather0, mixed)))

    fwd_hooks = {"ssd": late_weights_cross, "mixed": late_weights_arrive,
                 "ffn_act": lambda act: _all_gather_middle(held["gather1"], act)}
    saved = [None] * DEPTH
    act, act_mx, saved[0] = _layer_fwd(0, xl, (xl + token0[0, 0]).astype(MXU_DTYPE), W[0], SP[0], hooks=fwd_hooks)
    W[1] = whole_weights(BIG, _all_gather_end(held["gather1"], act))
    act, act_mx, saved[1] = _layer_fwd(1, act, act_mx, W[1], SP[1])
    dy, loss_parts = _loss_head(act, tgt)
    loss = lax.psum(jnp.sum(loss_parts), ("x", "y", "c"))

    def slots_of(names, gw):
        return [_slots_from_whole(n, gw[n]) for n in names]

    grads = {}
    GS = [None] * DEPTH
    dy, gw, GS[1] = _layer_bwd(1, dy, W[1], SP[1], saved[1])
    reduce1, reduce1_token = _reduce_scatter_begin("1", slots_of(BIG, gw))

    def late_grads_leave(seen, gw0):
        held["reduce0"], token = _reduce_scatter_begin("0", slots_of(late, gw0))
        return token

    def w_in_grad_leaves(gw0):
        held["reduce_first"], token = _reduce_scatter_begin("first", slots_of(BIG[:1], gw0))
        return token

    bwd_hooks = {"start": lambda: reduce1_token, "ffn": lambda seen: _reduce_scatter_middle(reduce1, seen),
                 "branches": late_grads_leave, "ssd": lambda seen: _reduce_scatter_middle(held["reduce0"], seen),
                 "w_in_grad": w_in_grad_leaves}
    dy, gw, GS[0] = _layer_bwd(0, dy, W[0], SP[0], saved[0], hooks=bwd_hooks)
    first_token = _reduce_scatter_middle(held["reduce_first"], dy)
    red1 = _reduce_scatter_end(reduce1, dy)
    red0_late = _reduce_scatter_end(held["reduce0"], dy)
    grad_x = dy[None]
    kept_t = ("w_in", "ffn_w_up")
    grads_k = {n: jnp.stack([_shard_from_slot(n, red0_late[i]), _shard_from_slot(n, red1[i + 1])]) for i, n in enumerate(late)}

    small_shapes = [whole[n].shape for n in SMALL]
    gs_flat = _pack([jnp.stack([GS[l][n].reshape(whole[n].shape[1:]) for l in range(DEPTH)]) for n in SMALL], F32)
    (gs_all,) = _all_gather("gather_small_grads", [gs_flat])

    def mine(n, a):
        if n in SMALL_SHARDED:
            cs = a.shape[-1] // N_DEV
            return lax.dynamic_slice_in_dim(a, dev * cs, cs, axis=a.ndim - 1)
        return a

    m_whole, v_whole = {}, {}
    for n in SMALL:
        reps = (1, 1, N_DEV) if n in SMALL_SHARDED else (1,) * m[n].ndim
        m_whole[n], v_whole[n] = jnp.tile(m[n], reps), jnp.tile(v[n], reps)
    outs = _adamw_small(gs_all, _pack([whole[n] for n in SMALL], F32) + first_token[0:1, 0:1], _pack([m_whole[n] for n in SMALL], F32),
                        _pack([v_whole[n] for n in SMALL], F32))
    g_s, d_s, m_s, v_s = [_unpack(o, small_shapes) for o in outs]
    delta, new_m, new_v = {}, {}, {}
    for i, n in enumerate(SMALL):
        grads[n], delta[n], new_m[n], new_v[n] = mine(n, g_s[i]), mine(n, d_s[i]), mine(n, m_s[i]), mine(n, v_s[i])
    for n in late + BIG[:1]:
        if n == "w_in":
            done = sum(new_v[k].reshape(-1)[0:1] for k in late + SMALL[:1])
            (first0,) = _reduce_scatter_end(held["reduce_first"], done)
            grads_k[n] = jnp.stack([first0, red1[0]])
        view = (lambda a: jnp.transpose(a, (0, 2, 1))) if n in kept_t else (lambda a: a)
        outs = _adamw(f"adamw_{n}", view(w[n]), grads_k[n], view(m[n]), view(v[n]), after=None if n == "w_in" else first_token)
        grads[n], delta[n], new_m[n], new_v[n] = view(grads_k[n]), view(outs[0]), view(outs[1]), view(outs[2])

    return (loss, grad_x, *[grads[n] for n in WEIGHTS], *[delta[n] for n in WEIGHTS], *[new_m[n] for n in WEIGHTS],
            *[new_v[n] for n in WEIGHTS])
```

```python
import functools
import math

import jax
import jax.numpy as jnp
from jax import lax
from jax.experimental import pallas as pl
from jax.experimental.pallas import tpu as pltpu

F32 = jnp.float32
MXU_DTYPE = jnp.bfloat16
HI = lax.Precision.HIGHEST

N_DEV = 8
D_MODEL = 1024
DEPTH = 2
SSD_HEADS, SSD_HEAD_DIM, SSD_INNER, SSD_GROUPS, SSD_STATE, SSD_CHUNK = 16, 64, 1024, 2, 128, 64
SSD_XBC = SSD_INNER + 2 * SSD_GROUPS * SSD_STATE
GDN_HEADS, GDN_HEAD_DIM, GDN_WIDTH, GDN_CHUNK = 4, 128, 512, 64
GLA_HEADS, GLA_KEY_DIM, GLA_VAL_DIM, GLA_K, GLA_V, GLA_RANK, GLA_CHUNK = 4, 64, 128, 256, 512, 16, 16
GLA_BLOCK = 128
GLA_NORMALIZER = 16.0
FFN_DIM = 2816
FFN_HALF = FFN_DIM // 8
FFN_HALF_PAD = 384
FFN_UP_PAD = 16 * FFN_HALF_PAD
FFN_PAD = FFN_UP_PAD // 2
ALPHA = (2 * DEPTH) ** 0.25
LN_EPS = 1e-5
RMS_EPS = 1e-6
ADAM_LR, ADAM_B1, ADAM_B2, ADAM_EPS, ADAM_WD, ADAM_STEP = 0.001, 0.9, 0.999, 1e-08, 0.01, 10
LANES = 128
NEG_BIG = -1e30
VMEM_LIMIT = 56 * 1024 * 1024

IN_SPLITS = (("z", 1024), ("xbc", 1536), ("dt", 16), ("gqkv", 1536), ("ga", 4), ("gb", 4), ("gg", 512),
             ("lqkv", 1024), ("lglr", 16), ("lr", 512), ("gates", 3072))
IN_DIM = sum(w for _, w in IN_SPLITS)
PAD_SEGS = (("gates", 0, 3072, (("gates", 0),)), ("xbc", 3072, 1536, (("xbc", 0),)),
            ("gqkv", 4608, 1536, (("gqkv", 0),)), ("z", 6144, 1024, (("z", 0),)),
            ("lqkv", 7168, 1024, (("lqkv", 0),)), ("gg", 8192, 512, (("gg", 0),)), ("lr", 8704, 512, (("lr", 0),)),
            ("dt", 9216, 128, (("dt", 0),)), ("gab", 9344, 128, (("ga", 0), ("gb", 4))), ("lglr", 9472, 128, (("lglr", 0),)))
IN_PAD = 9728
SEG = {name: (off, width) for name, off, width, _ in PAD_SEGS}

BIG = ("w_in", "w_br_ssd", "w_br_gdn", "w_br_gla", "w_out", "ffn_w_up", "ffn_w_down")
COL_SHARDED = ("w_in", "w_br_gdn", "w_br_gla", "ffn_w_up")
SMALL_SHARDED = ("ssd_conv_w", "gdn_conv_w", "gla_gate_w2", "ffn_conv_w")
WEIGHTS = ("w_in", "ssd_conv_w", "ssd_conv_b", "ssd_dt_bias", "ssd_a_log", "ssd_d", "ssd_norm_w", "gdn_conv_w",
           "gdn_a_log", "gdn_dt_bias", "gdn_norm_w", "gla_gate_w2", "gla_gate_b", "gla_norm_w", "w_br_ssd", "w_br_gdn",
           "w_br_gla", "w_out", "ln1_g", "ln1_b", "ffn_w_up", "ffn_conv_w", "ffn_conv_b", "ffn_w_down", "ln2_g", "ln2_b")
SMALL = tuple(n for n in WEIGHTS if n not in BIG)
FLAT_W = 512


def _cparams(sem=None):
    kw = dict(vmem_limit_bytes=VMEM_LIMIT)
    if sem is not None:
        kw["dimension_semantics"] = sem
    return pltpu.CompilerParams(**kw)


_DIMS = {"nn": (((1,), (0,)), ((), ())), "nt": (((1,), (1,)), ((), ())), "tn": (((0,), (0,)), ((), ()))}


def _dot(a, b, dims="nn"):
    if MXU_DTYPE == F32:
        return lax.dot_general(a.astype(F32), b.astype(F32), _DIMS[dims], precision=HI, preferred_element_type=F32)
    return lax.dot_general(a.astype(MXU_DTYPE), b.astype(MXU_DTYPE), _DIMS[dims], preferred_element_type=F32)


def _dot_hi(a, b, dims="nn"):
    return lax.dot_general(a.astype(F32), b.astype(F32), _DIMS[dims], precision=HI, preferred_element_type=F32)


def _iota2(shape, axis):
    return lax.broadcasted_iota(jnp.int32, shape, axis)


def _tril(n, strict=False):
    r, c = _iota2((n, n), 0), _iota2((n, n), 1)
    return (r > c) if strict else (r >= c)


def _raw_dot(a, b, dims):
    return lax.dot_general(a, b, _DIMS[dims], preferred_element_type=F32)


def _dot_x3(a, b, dims="nn"):
    if MXU_DTYPE == F32:
        return _dot_hi(a, b, dims)
    ah, bh = a.astype(jnp.bfloat16), b.astype(jnp.bfloat16)
    al, bl = (a - ah.astype(F32)).astype(jnp.bfloat16), (b - bh.astype(F32)).astype(jnp.bfloat16)
    return _raw_dot(ah, bh, dims) + (_raw_dot(ah, bl, dims) + _raw_dot(al, bh, dims))


def _exact_dot(mask, b, dims, mask_first):
    if MXU_DTYPE == F32:
        return _dot_hi(mask, b, dims) if mask_first else _dot_hi(b, mask, dims)
    m = mask.astype(jnp.bfloat16)
    b1 = b.astype(jnp.bfloat16)
    r1 = b - b1.astype(F32)
    b2 = r1.astype(jnp.bfloat16)
    b3 = (r1 - b2.astype(F32)).astype(jnp.bfloat16)
    if mask_first:
        return _raw_dot(m, b1, dims) + (_raw_dot(m, b2, dims) + _raw_dot(m, b3, dims))
    return _raw_dot(b1, m, dims) + (_raw_dot(b2, m, dims) + _raw_dot(b3, m, dims))


@jax.custom_vjp
def _mask_left(mask, b):
    return _exact_dot(mask, b, "nn", True)


_mask_left.defvjp(lambda mask, b: (_mask_left(mask, b), mask),
                  lambda mask, d: (jnp.zeros_like(mask), _exact_dot(mask, d, "tn", True)))


@jax.custom_vjp
def _mask_right(a, mask):
    return _exact_dot(mask, a, "nn", False)


_mask_right.defvjp(lambda a, mask: (_mask_right(a, mask), mask),
                   lambda mask, d: (_exact_dot(mask, d, "nt", False), jnp.zeros_like(mask)))


@jax.custom_vjp
def _unit_lower_inverses(mats):
    n = mats[0].shape[0]
    eye = (_iota2((n, n), 0) == _iota2((n, n), 1)).astype(F32)
    xs = [eye - a for a in mats]
    ps = list(mats)
    k = 2
    while k < n:
        ps = [_dot_x3(p, p) for p in ps]
        xs = [x + _dot_x3(x, p) for x, p in zip(xs, ps)]
        k *= 2
    return xs


def _unit_lower_inverses_fwd(mats):
    ts = _unit_lower_inverses(mats)
    return ts, ts


def _unit_lower_inverses_bwd(ts, dts):
    mids = [_dot_x3(t, d, "tn") for t, d in zip(ts, dts)]
    return ([-_dot_x3(m, t, "nt") for m, t in zip(mids, ts)],)


_unit_lower_inverses.defvjp(_unit_lower_inverses_fwd, _unit_lower_inverses_bwd)


def _ssd_chunk(xs_, ps_, s_t):
    xbc, dtraw, z = xs_
    dt_bias, a_log, d_skip, norm_w = ps_
    L = xbc.shape[0]
    H, P, N, G = SSD_HEADS, SSD_HEAD_DIM, SSD_STATE, SSD_GROUPS
    W = SSD_INNER // G
    xs = xbc[:, :SSD_INNER]
    bm = xbc[:, SSD_INNER:SSD_INNER + G * N]
    cm = xbc[:, SSD_INNER + G * N:]
    dt = jax.nn.softplus(dtraw[:, :H] + dt_bias)
    a = dt * (-jnp.exp(a_log))
    causal = _tril(L)
    a_cs = _mask_left(causal.astype(F32), a)
    expand = (_iota2((H, SSD_INNER), 1) // P == _iota2((H, SSD_INNER), 0)).astype(F32)
    wide = _mask_right(jnp.concatenate([a_cs, dt, jnp.broadcast_to(d_skip, (L, H))], axis=0), expand)
    a_cs_x, dt_x, d_x = wide[:L], wide[L:2 * L], wide[2 * L:]
    a_end_x = a_cs_x[L - 1:L, :]
    a_cs_t, dt_t = a_cs.T, dt.T
    cb = [_dot(cm[:, g * N:(g + 1) * N], bm[:, g * N:(g + 1) * N], "nt") for g in range(G)]
    decay = [jnp.exp(jnp.where(causal, a_cs[:, h:h + 1] - a_cs_t[h:h + 1, :], NEG_BIG)) * dt_t[h:h + 1, :] for h in range(H)]
    ws = [cb[h // (H // G)] * decay[h] for h in range(H)]
    y = jnp.concatenate([_dot(ws[h], xs[:, h * P:(h + 1) * P]) for h in range(H)], axis=1)
    y_in = jnp.concatenate([_dot(cm[:, g * N:(g + 1) * N], s_t[:, g * W:(g + 1) * W]) for g in range(G)], axis=1)
    y = y + y_in * jnp.exp(a_cs_x) + d_x * xs
    xw = xs * (jnp.exp(a_end_x - a_cs_x) * dt_x)
    st = jnp.concatenate([_dot(bm[:, g * N:(g + 1) * N], xw[:, g * W:(g + 1) * W], "tn") for g in range(G)], axis=1)
    s_new = s_t * jnp.exp(a_end_x) + st
    yg = y * jax.nn.silu(z)
    outs = []
    for g in range(G):
        part = yg[:, g * W:(g + 1) * W]
        outs.append(part * lax.rsqrt(jnp.mean(part * part, axis=1, keepdims=True) + RMS_EPS))
    return (jnp.concatenate(outs, axis=1) * norm_w,), s_new


GDN_PREP_CHUNKS = 4


def _gdn_prep(xs_, ps_):
    qkv, ab = xs_
    a_log, dt_bias = ps_
    B = qkv.shape[0]
    H, D, L = GDN_HEADS, GDN_HEAD_DIM, GDN_CHUNK
    g_all = -jnp.exp(a_log) * jax.nn.softplus(ab + dt_bias)
    row, col = _iota2((B, B), 0), _iota2((B, B), 1)
    g_cs = _mask_left((((row // L) == (col // L)) & (row >= col)).astype(F32), g_all)
    g_cs_t = g_cs.T
    beta_all = jax.nn.sigmoid(ab)
    incl, strict = _tril(L), _tril(L, strict=True)
    qs, ks, vs = [], [], []
    for h in range(H):
        q = qkv[:, h * D:(h + 1) * D]
        k = qkv[:, GDN_WIDTH + h * D:GDN_WIDTH + (h + 1) * D]
        qs.append(q * lax.rsqrt(jnp.sum(q * q, axis=1, keepdims=True) + RMS_EPS) * (D ** -0.5))
        ks.append(k * lax.rsqrt(jnp.sum(k * k, axis=1, keepdims=True) + RMS_EPS))
        vs.append(qkv[:, 2 * GDN_WIDTH + h * D:2 * GDN_WIDTH + (h + 1) * D])
    pairs = [(c, h) for c in range(B // L) for h in range(H)]
    rows = {c: slice(c * L, (c + 1) * L) for c in range(B // L)}
    q_ = {(c, h): qs[h][rows[c]] for c, h in pairs}
    k_ = {(c, h): ks[h][rows[c]] for c, h in pairs}
    col_ = {(c, h): g_cs[rows[c], h:h + 1] for c, h in pairs}
    beta_ = {(c, h): beta_all[rows[c], H + h:H + h + 1] for c, h in pairs}
    gamma = {p: jnp.exp(jnp.where(incl, col_[p] - g_cs_t[p[1]:p[1] + 1, rows[p[0]]], NEG_BIG)) for p in pairs}
    kb = {p: k_[p] * beta_[p] for p in pairs}
    a_mat = [jnp.where(strict, _dot(kb[p], k_[p], "nt") * gamma[p], 0.0) for p in pairs]
    attn = {p: jnp.where(incl, _dot(q_[p], k_[p], "nt") * gamma[p], 0.0) for p in pairs}
    t_mat = dict(zip(pairs, _unit_lower_inverses(a_mat)))
    u = {p: _dot(t_mat[p], vs[p[1]][rows[p[0]]] * beta_[p]) for p in pairs}
    w = {p: _dot(t_mat[p], kb[p] * jnp.exp(col_[p])) for p in pairs}
    qd = {p: q_[p] * jnp.exp(col_[p]) for p in pairs}
    kd = {p: k_[p] * jnp.exp(col_[p][L - 1:L, :] - col_[p]) for p in pairs}

    def whole(parts):
        return jnp.concatenate([jnp.concatenate([parts[(c, h)] for h in range(H)], axis=1) for c in range(B // L)], axis=0)

    return (whole(u), whole(w), whole(qd), whole(kd), whole(attn), g_cs)


def _gdn_scan(xs_, ps_, s):
    u, w, qd, kd, attn, g_cs, gate = xs_
    (norm_w,) = ps_
    L = u.shape[0]
    H, D = GDN_HEADS, GDN_HEAD_DIM
    heads = range(H)
    lanes = [slice(h * D, (h + 1) * D) for h in heads]
    s_h = [s[lanes[h], :] for h in heads]
    v_new = [u[:, lanes[h]] - _dot(w[:, lanes[h]], s_h[h]) for h in heads]
    o = [_dot(qd[:, lanes[h]], s_h[h]) + _dot(attn[:, h * L:(h + 1) * L], v_new[h]) for h in heads]
    decay = [jnp.exp(g_cs[L - 1:L, h:h + 1]) for h in heads]
    s_new = [s_h[h] * decay[h] + _dot(kd[:, lanes[h]], v_new[h], "tn") for h in heads]
    o = [o[h] * lax.rsqrt(jnp.mean(o[h] * o[h], axis=1, keepdims=True) + RMS_EPS) * norm_w * jax.nn.silu(gate[:, lanes[h]])
         for h in heads]
    return (jnp.concatenate(o, axis=1),), jnp.concatenate(s_new, axis=0)


def _gdn_forward(tag, gqkv, h, sp):
    T = gqkv.shape[0]
    blk = GDN_PREP_CHUNKS * GDN_CHUNK
    prep_in = [(gqkv, blk, 3 * GDN_WIDTH, 0), _seg_blk(h, "gab", blk)]
    prep_p = [_lane_pad(sp["gdn_a_log"]), _lane_pad(sp["gdn_dt_bias"])]
    mx = MXU_DTYPE
    prep = _chain_fwd(f"gdn_prep_{tag}", _gdn_prep, T // blk, prep_in, prep_p,
                      [(blk, GDN_WIDTH, F32), (blk, GDN_WIDTH, mx), (blk, GDN_WIDTH, mx), (blk, GDN_WIDTH, mx),
                       (blk, GDN_HEADS * GDN_CHUNK, mx), (blk, LANES, F32)])
    widths = [GDN_WIDTH] * 4 + [GDN_HEADS * GDN_CHUNK, LANES]
    scan_in = [(a, GDN_CHUNK, wd, 0) for a, wd in zip(prep, widths)] + [_seg_blk(h, "gg", GDN_CHUNK)]
    scan_p = [sp["gdn_norm_w"]]
    o, states = _chain_fwd(f"gdn_scan_{tag}", _gdn_scan, T // GDN_CHUNK, scan_in, scan_p, [(GDN_CHUNK, GDN_WIDTH, mx)],
                           (GDN_WIDTH, GDN_HEAD_DIM))
    return o, dict(prep_in=prep_in, prep_p=prep_p, scan_in=scan_in, scan_p=scan_p, states=states, widths=widths)


def _gdn_backward(tag, do, sv, dx_dtype):
    T = do.shape[0]
    blk = GDN_PREP_CHUNKS * GDN_CHUNK
    dscan, (dnorm,) = _chain_bwd(f"gdn_scan_bwd_{tag}", _gdn_scan, T // GDN_CHUNK, sv["scan_in"], sv["scan_p"],
                                 [(do, GDN_CHUNK, GDN_WIDTH)], sprev=sv["states"], dx_dtypes=[F32] * 6 + [dx_dtype])
    douts = [(d, blk, wd) for d, wd in zip(dscan[:6], sv["widths"])]
    (dgqkv, dgab), (da_log, ddt_bias) = _chain_bwd(f"gdn_prep_bwd_{tag}", _gdn_prep, T // blk, sv["prep_in"], sv["prep_p"],
                                                   douts, dx_dtypes=[F32, dx_dtype])
    return dgqkv, dgab, dscan[6], da_log[:, :GDN_HEADS], ddt_bias[:, :GDN_HEADS], dnorm


def _gla_block(xs_, ps_, s_t):
    qkv, glr, r = xs_
    w2, gate_b, norm_w = ps_
    B = qkv.shape[0]
    H, K, V, C = GLA_HEADS, GLA_KEY_DIM, GLA_VAL_DIM, GLA_CHUNK
    q = qkv[:, :GLA_K] * (K ** -0.5)
    k = qkv[:, GLA_K:2 * GLA_K]
    v = qkv[:, 2 * GLA_K:]
    gk = jax.nn.log_sigmoid(_dot(glr, w2) + gate_b) / GLA_NORMALIZER
    row, col = _iota2((B, B), 0), _iota2((B, B), 1)
    same = (row // C) == (col // C)
    mask = same & (row >= col)
    b_cs = _mask_left(mask.astype(F32), gk)
    b_end = _mask_left((col == (row // C) * C + (C - 1)).astype(F32), b_cs)
    q_e = q * jnp.exp(b_cs)
    k_e = k * jnp.exp(-b_cs)
    k_d = k * jnp.exp(b_end - b_cs)
    intra = []
    for h in range(H):
        a_mat = jnp.where(mask, _dot(q_e[:, h * K:(h + 1) * K], k_e[:, h * K:(h + 1) * K], "nt"), 0.0)
        intra.append(_dot(a_mat, v[:, h * V:(h + 1) * V]))
    o = jnp.concatenate(intra, axis=1)
    chunks = [slice(j * C, (j + 1) * C) for j in range(B // C)]
    fresh = [jnp.concatenate([_dot(v[sl, h * V:(h + 1) * V], k_d[sl, h * K:(h + 1) * K], "tn") for h in range(H)], axis=1)
             for sl in chunks]
    entering = []
    for j, sl in enumerate(chunks):
        entering.append(s_t)
        s_t = s_t * jnp.exp(b_end[j * C:j * C + 1, :]) + fresh[j]
    inter = [jnp.concatenate([_dot(q_e[sl, h * K:(h + 1) * K], entering[j][:, h * K:(h + 1) * K], "nt") for h in range(H)],
                             axis=1) for j, sl in enumerate(chunks)]
    o = o + jnp.concatenate(inter, axis=0)
    outs = []
    for h in range(H):
        oh = o[:, h * V:(h + 1) * V]
        oh = oh * lax.rsqrt(jnp.mean(oh * oh, axis=1, keepdims=True) + RMS_EPS) * norm_w
        outs.append(oh * jax.nn.silu(r[:, h * V:(h + 1) * V]))
    return (jnp.concatenate(outs, axis=1),), s_t


def _merge_fn(xs_, ps_):
    gates, y_ssd, y_gdn, y_gla = xs_
    d = D_MODEL
    return (jax.nn.sigmoid(gates[:, :d]) * y_ssd + jax.nn.sigmoid(gates[:, d:2 * d]) * y_gdn
            + jax.nn.sigmoid(gates[:, 2 * d:]) * y_gla,)


def _ln_fn(xs_, ps_):
    x, r = xs_
    g, b = ps_
    t = ALPHA * x + r
    mu = jnp.mean(t, axis=1, keepdims=True)
    var = jnp.mean(jnp.square(t - mu), axis=1, keepdims=True)
    return ((t - mu) * lax.rsqrt(var + LN_EPS) * g + b,)


def _row_spec(rows, width, colblk, n, reverse):
    if reverse:
        return pl.BlockSpec((rows, width), lambda c: (n - 1 - c, colblk))
    return pl.BlockSpec((rows, width), lambda c: (c, colblk))


def _full_spec(shape):
    zeros = (0,) * len(shape)
    return pl.BlockSpec(shape, lambda c: zeros)


def _chain_fwd(name, fn, n, blocked, full, out_defs, state_shape=None):
    nb, nf, no = len(blocked), len(full), len(out_defs)

    def body(*refs):
        xs = [r[...].astype(F32) for r in refs[:nb]]
        ps = [r[...] for r in refs[nb:nb + nf]]
        o_refs = refs[nb + nf:nb + nf + no]
        if state_shape is None:
            outs = fn(xs, ps)
        else:
            sprev_ref, s_ref = refs[nb + nf + no:]

            @pl.when(pl.program_id(0) == 0)
            def _():
                s_ref[...] = jnp.zeros_like(s_ref)

            s = s_ref[...]
            sprev_ref[0] = s
            outs, s_new = fn(xs, ps, s)
            s_ref[...] = s_new
        for r, o in zip(o_refs, outs):
            r[...] = o.astype(r.dtype)

    in_specs = [_row_spec(rows, width, cb, n, False) for _, rows, width, cb in blocked]
    in_specs += [_full_spec(a.shape) for a in full]
    out_specs = [_row_spec(rows, width, 0, n, False) for rows, width, _ in out_defs]
    out_shape = [jax.ShapeDtypeStruct((n * rows, width), dt) for rows, width, dt in out_defs]
    scratch = []
    if state_shape is not None:
        out_specs.append(pl.BlockSpec((1,) + state_shape, lambda c: (c, 0, 0)))
        out_shape.append(jax.ShapeDtypeStruct((n,) + state_shape, F32))
        scratch.append(pltpu.VMEM(state_shape, F32))
    return pl.pallas_call(body, name=name, grid=(n,), in_specs=in_specs, out_specs=out_specs, out_shape=out_shape,
                          scratch_shapes=scratch, compiler_params=_cparams(("arbitrary",)))(
        *[a for a, _, _, _ in blocked], *full)


def _chain_bwd(name, fn, n, blocked, full, douts, sprev=None, dx_dtypes=None):
    nb, nf, nd = len(blocked), len(full), len(douts)
    has_state = sprev is not None
    dx_dtypes = dx_dtypes or [F32] * nb

    def body(*refs):
        pos = 0
        b_refs = refs[pos:pos + nb]; pos += nb
        f_refs = refs[pos:pos + nf]; pos += nf
        d_refs = refs[pos:pos + nd]; pos += nd
        if has_state:
            sprev_ref = refs[pos]; pos += 1
        dx_refs = refs[pos:pos + nb]; pos += nb
        dp_refs = refs[pos:pos + nf]; pos += nf
        if has_state:
            ds_ref = refs[pos]

        @pl.when(pl.program_id(0) == 0)
        def _():
            for r in dp_refs:
                r[...] = jnp.zeros_like(r)
            if has_state:
                ds_ref[...] = jnp.zeros_like(ds_ref)

        xs = [r[...].astype(F32) for r in b_refs]
        ps = [r[...] for r in f_refs]
        dys = tuple(r[...].astype(F32) for r in d_refs)
        if has_state:
            _, vjp = jax.vjp(fn, xs, ps, sprev_ref[0])
            dxs, dps, ds = vjp((dys, ds_ref[...]))
            ds_ref[...] = ds
        else:
            _, vjp = jax.vjp(fn, xs, ps)
            dxs, dps = vjp(dys)
        for r, d in zip(dx_refs, dxs):
            r[...] = d.astype(r.dtype)
        for r, d in zip(dp_refs, dps):
            r[...] += d

    in_specs = [_row_spec(rows, width, cb, n, True) for _, rows, width, cb in blocked]
    in_specs += [_full_spec(a.shape) for a in full]
    in_specs += [_row_spec(rows, width, 0, n, True) for _, rows, width in douts]
    args = [a for a, _, _, _ in blocked] + list(full) + [a for a, _, _ in douts]
    scratch = []
    if has_state:
        st_shape = sprev.shape[1:]
        in_specs.append(pl.BlockSpec((1,) + st_shape, lambda c: (n - 1 - c, 0, 0)))
        args.append(sprev)
        scratch.append(pltpu.VMEM(st_shape, F32))
    out_specs = [_row_spec(rows, width, 0, n, True) for _, rows, width, _ in blocked]
    out_specs += [_full_spec(a.shape) for a in full]
    out_shape = [jax.ShapeDtypeStruct((n * rows, width), dt) for (_, rows, width, _), dt in zip(blocked, dx_dtypes)]
    out_shape += [jax.ShapeDtypeStruct(a.shape, F32) for a in full]
    res = pl.pallas_call(body, name=name, grid=(n,), in_specs=in_specs, out_specs=out_specs, out_shape=out_shape,
                         scratch_shapes=scratch, compiler_params=_cparams(("arbitrary",)))(*args)
    return res[:nb], res[nb:]


def _tile(n, target, unit):
    if n <= target:
        return n
    best = None
    for t in range(unit, target + 1, unit):
        if n % t == 0:
            best = t
    assert best is not None, (n, target, unit)
    return best


def _mm(name, a, b, dims="nn", out_dtype=F32, tm=2048, tn=512, tk=2048, after=None):
    if dims == "nn":
        (M, K), (_, N) = a.shape, b.shape
    elif dims == "nt":
        (M, K), (N, _) = a.shape, b.shape
    else:
        (K, M), (_, N) = a.shape, b.shape
    tm, tn, tk = _tile(M, tm, LANES), _tile(N, tn, LANES), _tile(K, tk, LANES)
    nk = K // tk
    extra = [] if after is None else [after]

    def body(*refs):
        a_ref, b_ref = refs[:2]
        o_ref, acc_ref = refs[-2:]
        part = _dot(a_ref[...], b_ref[...], dims)
        if nk == 1:
            o_ref[...] = part.astype(o_ref.dtype)
            return
        k = pl.program_id(2)

        @pl.when(k == 0)
        def _():
            acc_ref[...] = part

        @pl.when((k > 0) & (k < nk - 1))
        def _():
            acc_ref[...] += part

        @pl.when(k == nk - 1)
        def _():
            o_ref[...] = (acc_ref[...] + part).astype(o_ref.dtype)

    if dims == "tn":
        a_spec = pl.BlockSpec((tk, tm), lambda j, i, k: (k, i))
    else:
        a_spec = pl.BlockSpec((tm, tk), lambda j, i, k: (i, k))
    if dims == "nt":
        b_spec = pl.BlockSpec((tn, tk), lambda j, i, k: (j, k))
    else:
        b_spec = pl.BlockSpec((tk, tn), lambda j, i, k: (k, j))
    return pl.pallas_call(
        body, name=name, grid=(N // tn, M // tm, nk), in_specs=[a_spec, b_spec] + [ANY] * len(extra),
        out_specs=pl.BlockSpec((tm, tn), lambda j, i, k: (i, j)), out_shape=jax.ShapeDtypeStruct((M, N), out_dtype),
        scratch_shapes=[pltpu.VMEM((tm, tn) if nk > 1 else (8, LANES), F32)],
        compiler_params=_cparams(("parallel", "parallel", "arbitrary")))(a, b, *extra)


CONV_CB = 256


def _shift_down(x, k):
    if k == 0:
        return x
    return jnp.where(_iota2(x.shape, 0) >= k, pltpu.roll(x, k, 0), 0.0)


def _shift_up(x, k):
    if k == 0:
        return x
    t = x.shape[0]
    return jnp.where(_iota2(x.shape, 0) < t - k, pltpu.roll(x, t - k, 0), 0.0)


def _conv_pre(x, w, b):
    kk = w.shape[0]
    pre = x * w[kk - 1:kk, :]
    for k in range(kk - 1):
        pre = pre + _shift_down(x, kk - 1 - k) * w[k:k + 1, :]
    return pre if b is None else pre + b


EDGE = 16


def _conv_pre_rot(x, w, b):
    kk = w.shape[0]
    pre = x * w[kk - 1:kk, :]
    for k in range(kk - 1):
        pre = pre + pltpu.roll(x, kk - 1 - k, 0) * w[k:k + 1, :]
    return pre if b is None else pre + b


def _conv_t_local(d, w):
    kk = w.shape[0]
    out = d * w[kk - 1:kk, :]
    for k in range(kk - 1):
        out = out + _shift_up(d, kk - 1 - k) * w[k:k + 1, :]
    return out


def _col_sum(a):
    return jnp.sum(a, axis=0, keepdims=True)


def _conv_bwd_rot(x_ref, w, dpre, dpre_head, dx_ref, dw_ref, db_ref):
    T = dpre.shape[0]
    kk = w.shape[0]
    x = x_ref[...]
    x_head, x_tail = x_ref[0:EDGE, :], x_ref[T - EDGE:T, :]
    wrong_head = dpre[0:EDGE]
    dx = dpre * w[kk - 1:kk, :]
    for k in range(kk - 1):
        dx = dx + pltpu.roll(dpre, T - (kk - 1 - k), 0) * w[k:k + 1, :]
    dx_ref[...] = dx.astype(dx_ref.dtype)
    top = jnp.concatenate([dpre_head, dpre[EDGE:2 * EDGE]], axis=0)
    dx_ref[0:EDGE, :] = _conv_t_local(top, w)[0:EDGE].astype(dx_ref.dtype)
    dx_ref[T - EDGE:T, :] = _conv_t_local(dpre[T - EDGE:T], w).astype(dx_ref.dtype)
    ends = jnp.concatenate([x_tail, x_head], axis=0)
    dw_ref[kk - 1:kk, :] = _col_sum(dpre * x) + _col_sum((dpre_head - wrong_head) * x_head)
    for k in range(kk - 1):
        s = kk - 1 - k
        rotated_head = pltpu.roll(ends, s, 0)[EDGE:2 * EDGE]
        dw_ref[k:k + 1, :] = (_col_sum(dpre * pltpu.roll(x, s, 0)) - _col_sum(wrong_head * rotated_head)
                              + _col_sum(dpre_head * _shift_down(x_head, s)))
    if db_ref is not None:
        db_ref[...] = _col_sum(dpre) + _col_sum(dpre_head - wrong_head)


def _dsilu(pre):
    sg = jax.nn.sigmoid(pre)
    return sg * (1.0 + pre * (1.0 - sg))


def _conv_silu_fwd(name, src, col0, w, b):
    T = src.shape[0]
    kk, C = w.shape
    cb = CONV_CB
    off = col0 // cb

    def body(*refs):
        x_ref, w_ref, o_ref = refs[0], refs[1], refs[-1]
        b_val = refs[2][...] if b is not None else None
        o_ref[...] = jax.nn.silu(_conv_pre_rot(x_ref[...], w_ref[...], b_val))
        o_ref[0:EDGE, :] = jax.nn.silu(_conv_pre(x_ref[0:EDGE, :], w_ref[...], b_val))

    in_specs = [pl.BlockSpec((T, cb), lambda j: (0, off + j)), pl.BlockSpec((kk, cb), lambda j: (0, j))]
    args = [src, w]
    if b is not None:
        in_specs.append(pl.BlockSpec((1, cb), lambda j: (0, j)))
        args.append(b)
    return pl.pallas_call(body, name=name, grid=(C // cb,), in_specs=in_specs,
                          out_specs=pl.BlockSpec((T, cb), lambda j: (0, j)), out_shape=jax.ShapeDtypeStruct((T, C), F32),
                          compiler_params=_cparams(("parallel",)))(*args)


def _conv_silu_bwd(name, src, col0, w, b, dy, dx_dtype):
    T = src.shape[0]
    kk, C = w.shape
    cb = CONV_CB
    off = col0 // cb
    has_b = b is not None

    def body(*refs):
        x_ref, w_ref = refs[:2]
        pos = 2
        b_val = None
        if has_b:
            b_val = refs[pos][...]; pos += 1
        dy_ref = refs[pos]; pos += 1
        dx_ref, dw_ref = refs[pos], refs[pos + 1]
        db_ref = refs[pos + 2] if has_b else None
        wv = w_ref[...]
        dpre = dy_ref[...] * _dsilu(_conv_pre_rot(x_ref[...], wv, b_val))
        dpre_head = dy_ref[0:EDGE, :] * _dsilu(_conv_pre(x_ref[0:EDGE, :], wv, b_val))
        _conv_bwd_rot(x_ref, wv, dpre, dpre_head, dx_ref, dw_ref, db_ref)

    in_specs = [pl.BlockSpec((T, cb), lambda j: (0, off + j)), pl.BlockSpec((kk, cb), lambda j: (0, j))]
    args = [src, w]
    if has_b:
        in_specs.append(pl.BlockSpec((1, cb), lambda j: (0, j)))
        args.append(b)
    in_specs.append(pl.BlockSpec((T, cb), lambda j: (0, j)))
    args.append(dy)
    out_specs = [pl.BlockSpec((T, cb), lambda j: (0, j)), pl.BlockSpec((kk, cb), lambda j: (0, j))]
    out_shape = [jax.ShapeDtypeStruct((T, C), dx_dtype), jax.ShapeDtypeStruct((kk, C), F32)]
    if has_b:
        out_specs.append(pl.BlockSpec((1, cb), lambda j: (0, j)))
        out_shape.append(jax.ShapeDtypeStruct((1, C), F32))
    return pl.pallas_call(body, name=name, grid=(C // cb,), in_specs=in_specs, out_specs=out_specs, out_shape=out_shape,
                          compiler_params=_cparams(("parallel",)))(*args)


def _ffn_glu_fwd(name, up, w, b, out_dtype=F32):
    T = up.shape[0]
    kk = w.shape[0]
    cb = CONV_CB
    width = up.shape[1] // 2
    nblk = width // cb

    def body(g_ref, u_ref, wg_ref, wu_ref, bg_ref, bu_ref, o_ref):
        g = _conv_pre_rot(g_ref[...], wg_ref[...], bg_ref[...])
        u = _conv_pre_rot(u_ref[...], wu_ref[...], bu_ref[...])
        o_ref[...] = (jax.nn.silu(g) * u).astype(o_ref.dtype)
        g = _conv_pre(g_ref[0:EDGE, :], wg_ref[...], bg_ref[...])
        u = _conv_pre(u_ref[0:EDGE, :], wu_ref[...], bu_ref[...])
        o_ref[0:EDGE, :] = (jax.nn.silu(g) * u).astype(o_ref.dtype)

    lo, hi = (lambda j: (0, j)), (lambda j: (0, nblk + j))
    in_specs = [pl.BlockSpec((T, cb), lo), pl.BlockSpec((T, cb), hi), pl.BlockSpec((kk, cb), lo), pl.BlockSpec((kk, cb), hi),
                pl.BlockSpec((1, cb), lo), pl.BlockSpec((1, cb), hi)]
    return pl.pallas_call(body, name=name, grid=(nblk,), in_specs=in_specs, out_specs=pl.BlockSpec((T, cb), lo),
                          out_shape=jax.ShapeDtypeStruct((T, width), out_dtype),
                          compiler_params=_cparams(("parallel",)))(up, up, w, w, b, b)


def _ffn_glu_bwd(name, up, w, b, dact, dx_dtype):
    T = up.shape[0]
    kk = w.shape[0]
    cb = CONV_CB
    width = up.shape[1] // 2
    nblk = width // cb

    def body(g_ref, u_ref, wg_ref, wu_ref, bg_ref, bu_ref, d_ref, dg_ref, du_ref, dwg_ref, dwu_ref, dbg_ref, dbu_ref):
        wg, wu = wg_ref[...], wu_ref[...]
        g = _conv_pre_rot(g_ref[...], wg, bg_ref[...])
        u = _conv_pre_rot(u_ref[...], wu, bu_ref[...])
        d = d_ref[...].astype(F32)
        g_head = _conv_pre(g_ref[0:EDGE, :], wg, bg_ref[...])
        u_head = _conv_pre(u_ref[0:EDGE, :], wu, bu_ref[...])
        d_head = d_ref[0:EDGE, :].astype(F32)
        _conv_bwd_rot(g_ref, wg, d * u * _dsilu(g), d_head * u_head * _dsilu(g_head), dg_ref, dwg_ref, dbg_ref)
        _conv_bwd_rot(u_ref, wu, d * jax.nn.silu(g), d_head * jax.nn.silu(g_head), du_ref, dwu_ref, dbu_ref)

    lo, hi = (lambda j: (0, j)), (lambda j: (0, nblk + j))
    in_specs = [pl.BlockSpec((T, cb), lo), pl.BlockSpec((T, cb), hi), pl.BlockSpec((kk, cb), lo), pl.BlockSpec((kk, cb), hi),
                pl.BlockSpec((1, cb), lo), pl.BlockSpec((1, cb), hi), pl.BlockSpec((T, cb), lo)]
    out_specs = [pl.BlockSpec((T, cb), lo)] * 2 + [pl.BlockSpec((kk, cb), lo)] * 2 + [pl.BlockSpec((1, cb), lo)] * 2
    out_shape = ([jax.ShapeDtypeStruct((T, width), dx_dtype)] * 2 + [jax.ShapeDtypeStruct((kk, width), F32)] * 2
                 + [jax.ShapeDtypeStruct((1, width), F32)] * 2)
    return pl.pallas_call(body, name=name, grid=(nblk,), in_specs=in_specs, out_specs=out_specs, out_shape=out_shape,
                          compiler_params=_cparams(("parallel",)))(up, up, w, w, b, b, dact)


def _loss_head(y, target):
    T, D = y.shape
    tb = _tile(T, 256, 8)

    def body(y_ref, t_ref, dy_ref, l_ref):
        @pl.when(pl.program_id(0) == 0)
        def _():
            l_ref[...] = jnp.zeros_like(l_ref)

        err = y_ref[...] - t_ref[...]
        dy_ref[...] = err * (1.0 / D)
        l_ref[...] += jnp.sum(err * err, axis=0, keepdims=True) * (0.5 / D)

    spec = pl.BlockSpec((tb, D), lambda i: (i, 0))
    return pl.pallas_call(body, name="loss_head", grid=(T // tb,), in_specs=[spec, spec],
                          out_specs=[spec, pl.BlockSpec((1, D), lambda i: (0, 0))],
                          out_shape=[jax.ShapeDtypeStruct((T, D), F32), jax.ShapeDtypeStruct((1, D), F32)],
                          compiler_params=_cparams(("arbitrary",)))(y, target)


def _adamw_math(w, g, m, v):
    m = ADAM_B1 * m + (1.0 - ADAM_B1) * g
    v = ADAM_B2 * v + (1.0 - ADAM_B2) * jnp.square(g)
    m_hat = m / (1.0 - ADAM_B1 ** ADAM_STEP)
    v_hat = v / (1.0 - ADAM_B2 ** ADAM_STEP)
    return -ADAM_LR * (m_hat / (jnp.sqrt(v_hat) + ADAM_EPS) + ADAM_WD * w), m, v


def _adamw(name, w, g, m, v, after=None):
    A, R, C = w.shape
    if C % LANES == 0:
        rb, cb = _slab(R, C)
    else:
        rb, cb = _tile(R, max(8, SLAB_BYTES // 2 // (C * 4) // 8 * 8), 8), C
    extra = [] if after is None else [after]

    def body(w_ref, g_ref, m_ref, v_ref, *rest):
        d_ref, mo_ref, vo_ref = rest[-3:]
        d, mn, vn = _adamw_math(w_ref[...], g_ref[...], m_ref[...], v_ref[...])
        d_ref[...] = d
        mo_ref[...] = mn
        vo_ref[...] = vn

    spec = pl.BlockSpec((1, rb, cb), lambda a, r, q: (a, r, q))
    return pl.pallas_call(body, name=name, grid=(A, R // rb, C // cb), in_specs=[spec] * 4 + [ANY] * len(extra),
                          out_specs=[spec] * 3, out_shape=[jax.ShapeDtypeStruct(w.shape, F32)] * 3,
                          compiler_params=_cparams(("parallel", "parallel", "parallel")))(w, g, m, v, *extra)


def _adamw_small(parts, w, m, v):
    def body(p_ref, w_ref, m_ref, v_ref, g_ref, d_ref, mo_ref, vo_ref):
        g = p_ref[0]
        for i in range(1, N_DEV):
            g = g + p_ref[i]
        d, mn, vn = _adamw_math(w_ref[...], g, m_ref[...], v_ref[...])
        g_ref[...] = g
        d_ref[...] = d
        mo_ref[...] = mn
        vo_ref[...] = vn

    return pl.pallas_call(body, name="adamw_small", out_shape=[jax.ShapeDtypeStruct(w.shape, F32)] * 4,
                          compiler_params=_cparams())(parts, w, m, v)


def _add_blocks(name, a, b, out_dtype=F32):
    n, R, W = a.shape
    rb = _tile(R, 512, 8)

    def body(a_ref, b_ref, o_ref):
        o_ref[...] = (a_ref[...].astype(F32) + b_ref[...].astype(F32)).astype(o_ref.dtype)

    spec = pl.BlockSpec((1, rb, W), lambda i, r: (i, r, 0))
    return pl.pallas_call(body, name=name, grid=(n, R // rb), in_specs=[spec, spec], out_specs=spec,
                          out_shape=jax.ShapeDtypeStruct(a.shape, out_dtype),
                          compiler_params=_cparams(("parallel", "parallel")))(a, b)


SLAB_BYTES = 5 << 19


def _slab(R, W):
    if R % 16 == 0:
        return _tile(R, max(16, SLAB_BYTES // (4 * W) // 16 * 16), 16), W
    assert W % LANES == 0, (R, W)
    return R, _tile(W, max(LANES, SLAB_BYTES // (4 * R) // LANES * LANES), LANES)


def _pair_add(name, g, other, c, chip):
    _, R, W = g.shape
    rb, cb = _slab(R, W)

    def body(s_ref, a_ref, b_ref, send_ref, own_ref):
        s = a_ref[0] + b_ref[0]
        send_ref[0] = s.astype(send_ref.dtype)

        @pl.when(pl.program_id(2) == s_ref[1])
        def _():
            own_ref[...] = s

    grid_spec = pltpu.PrefetchScalarGridSpec(
        num_scalar_prefetch=1, grid=(R // rb, W // cb, 4),
        in_specs=[pl.BlockSpec((1, rb, cb), lambda r, q, p, s_ref: (2 * p + s_ref[0], r, q)),
                  pl.BlockSpec((1, rb, cb), lambda r, q, p, s_ref: (p, r, q))],
        out_specs=[pl.BlockSpec((1, rb, cb), lambda r, q, p, s_ref: (p, r, q)),
                   pl.BlockSpec((rb, cb), lambda r, q, p, s_ref: (r, q))])
    scalars = jnp.stack([c, chip]).astype(jnp.int32)
    return pl.pallas_call(body, name=name, grid_spec=grid_spec,
                          out_shape=[jax.ShapeDtypeStruct((4, R, W), MXU_DTYPE), jax.ShapeDtypeStruct((R, W), F32)],
                          compiler_params=_cparams(("parallel", "parallel", "arbitrary")))(scalars, g, other)


def _sum4(name, own, parts):
    R, W = own.shape
    rb, cb = _slab(R, W)

    def body(o_ref, p_ref, out_ref):
        out_ref[...] = ((o_ref[...] + p_ref[0].astype(F32)) + p_ref[1].astype(F32)) + p_ref[2].astype(F32)

    return pl.pallas_call(body, name=name, grid=(R // rb, W // cb),
                          in_specs=[pl.BlockSpec((rb, cb), lambda r, q: (r, q)), pl.BlockSpec((3, rb, cb), lambda r, q: (0, r, q))],
                          out_specs=pl.BlockSpec((rb, cb), lambda r, q: (r, q)), out_shape=jax.ShapeDtypeStruct((R, W), F32),
                          compiler_params=_cparams(("parallel", "parallel")))(own, parts)


MESH = pl.DeviceIdType.MESH
ANY = pl.BlockSpec(memory_space=pl.ANY)


def _place():
    return lax.axis_index("x"), lax.axis_index("y"), lax.axis_index("c")


def _other_chips(x, y):
    return [(1 - x, y), (x, 1 - y), (1 - x, 1 - y)]


def _all_gather(name, blocks):
    n = len(blocks)

    def body(*refs):
        x_refs, out_refs = refs[:n], refs[n:2 * n]
        send_sems, recv_sems, local_sems = refs[2 * n:]
        x, y, c = _place()
        me, sibling = (x, y, c), (x, y, 1 - c)
        chips = _other_chips(x, y)

        def slot(a, px, py, pc):
            return out_refs[a].at[4 * px + 2 * py + pc]

        def copy(a, k, blk, to, src=None):
            return pltpu.make_async_remote_copy(src_ref=slot(a, *blk) if src is None else src, dst_ref=slot(a, *blk),
                                                send_sem=send_sems.at[a, k], recv_sem=recv_sems.at[a, k],
                                                device_id=to, device_id_type=MESH)

        mine = [pltpu.make_async_copy(x_refs[a], slot(a, *me), local_sems.at[a]) for a in range(n)]
        for cp in mine:
            cp.start()
        first = []
        for j, chip in enumerate(chips):
            first += [copy(a, 1 + j, me, (*chip, c), src=x_refs[a]) for a in range(n)]
        first += [copy(a, 0, me, sibling, src=x_refs[a]) for a in range(n)]
        for cp in first:
            cp.start()
        passed = []
        for j, chip in enumerate(chips):
            for a in range(n):
                copy(a, 1 + j, (*chip, c), me).wait_recv()
                passed.append(copy(a, 4 + j, (*chip, c), sibling))
                passed[-1].start()
        for a in range(n):
            copy(a, 0, sibling, me).wait_recv()
        for j, chip in enumerate(chips):
            for a in range(n):
                copy(a, 4 + j, (*chip, 1 - c), me).wait_recv()
        for cp in first + passed:
            cp.wait_send()
        for cp in mine:
            cp.wait()

    return pl.pallas_call(body, name=name, in_specs=[ANY] * n, out_specs=[ANY] * n,
                          out_shape=[jax.ShapeDtypeStruct((N_DEV,) + b.shape, b.dtype) for b in blocks],
                          scratch_shapes=[pltpu.SemaphoreType.DMA((n, 7)), pltpu.SemaphoreType.DMA((n, 7)),
                                          pltpu.SemaphoreType.DMA((n,))])(*blocks)


def _routes_to_sibling(x, y, c):
    return [(2 * p + (1 - c), p, (x, y, 1 - c)) for p in range(4)]


def _routes_to_chips(x, y, c):
    return [(2 * px + py, j, (px, py, c)) for j, (px, py) in enumerate(_other_chips(x, y))]


def _routes_block_to_chips(x, y, c):
    me = 4 * x + 2 * y + c
    return [(me, me, (px, py, c)) for px, py in _other_chips(x, y)]


def _routes_blocks_to_sibling(x, y, c):
    return [(4 * px + 2 * py + c, 4 * px + 2 * py + c, (x, y, 1 - c)) for px, py in [(x, y)] + _other_chips(x, y)]


def _route_copies(routes, src_refs, land_refs, send_sems, recv_sems):
    x, y, c = _place()
    copies = []
    for a, (src, land) in enumerate(zip(src_refs, land_refs)):
        plan = routes(x, y, c)
        for k, (s, d, target) in enumerate(plan):
            i = a * len(plan) + k
            copies.append(pltpu.make_async_remote_copy(src_ref=src.at[s], dst_ref=land.at[d], send_sem=send_sems.at[i],
                                                       recv_sem=recv_sems.at[i], device_id=target, device_id_type=MESH))
    return copies


def _exchange(name, routes, n_routes, srcs, land_slots):
    n = len(srcs)

    def body(*refs):
        copies = _route_copies(routes, refs[:n], refs[n:2 * n], refs[2 * n], refs[2 * n + 1])
        for cp in copies:
            cp.start()
        for cp in copies:
            cp.wait_recv()
        for cp in copies:
            cp.wait_send()

    return pl.pallas_call(body, name=name, in_specs=[ANY] * n, out_specs=[ANY] * n,
                          out_shape=[jax.ShapeDtypeStruct((land_slots,) + s.shape[1:], s.dtype) for s in srcs],
                          scratch_shapes=[pltpu.SemaphoreType.DMA((n * n_routes,)), pltpu.SemaphoreType.DMA((n * n_routes,))])(*srcs)


HBM_SPEC = pl.BlockSpec(memory_space=pltpu.HBM)
SEM_SPEC = pl.BlockSpec(memory_space=pltpu.SEMAPHORE)
DATAFLOW = pltpu.SideEffectType.DATAFLOW_SIDE_EFFECTING


def _exchange_start(name, routes, n_routes, srcs, lands, after=None):
    n = len(srcs)
    in_place = lands is None
    bufs = list(srcs) + ([] if in_place else list(lands))
    nb = len(bufs)
    extra = [] if after is None else [after]

    def body(*refs):
        src_refs = refs[:n]
        land_refs = src_refs if in_place else refs[n:nb]
        send_sems, recv_sems = refs[nb + len(extra)], refs[nb + len(extra) + 1]
        token = refs[-1]
        for cp in _route_copies(routes, src_refs, land_refs, send_sems, recv_sems):
            cp.start()
        token[...] = jnp.zeros_like(token)

    sems = [pltpu.SemaphoreType.DMA((n * n_routes,)), pltpu.SemaphoreType.DMA((n * n_routes,))]
    out = pl.pallas_call(
        body, name=name, in_specs=[HBM_SPEC] * nb + [ANY] * len(extra),
        out_shape=sems + [pltpu.HBM(b.shape, b.dtype) for b in bufs] + [jax.ShapeDtypeStruct((8, LANES), F32)],
        out_specs=[SEM_SPEC, SEM_SPEC] + [HBM_SPEC] * nb + [pl.BlockSpec(memory_space=pltpu.VMEM)],
        input_output_aliases={i: 2 + i for i in range(nb)},
        compiler_params=pltpu.CompilerParams(has_side_effects=DATAFLOW))(
        *[pltpu.with_memory_space_constraint(b, pltpu.HBM) for b in bufs], *extra)
    return (out[0], out[1], list(out[2:2 + nb])), out[-1]


def _exchange_wait(name, routes, n_routes, n, started, after):
    send_sems, recv_sems, bufs = started
    nb = len(bufs)
    in_place = nb == n

    def body(*refs):
        src_refs = refs[:n]
        land_refs = src_refs if in_place else refs[n:nb]
        for cp in _route_copies(routes, src_refs, land_refs, refs[nb], refs[nb + 1]):
            cp.wait_send()
            cp.wait_recv()

    out = pl.pallas_call(
        body, name=name, in_specs=[HBM_SPEC] * nb + [SEM_SPEC, SEM_SPEC, ANY],
        out_shape=[pltpu.HBM(b.shape, b.dtype) for b in bufs], out_specs=[HBM_SPEC] * nb,
        input_output_aliases={i: i for i in range(nb)},
        compiler_params=pltpu.CompilerParams(has_side_effects=DATAFLOW))(*bufs, send_sems, recv_sems, after)
    return list(out[:n]) if in_place else list(out[n:])


def _pair_sums(tag, gs, from_sibling):
    x, y, c = _place()
    return [_pair_add(f"rs_add_{tag}_{i}", g, o, c, 2 * x + y) for i, (g, o) in enumerate(zip(gs, from_sibling))]


def _reduce_scatter(tag, gs):
    sums = _pair_sums(tag, gs, _exchange(f"rs_swap_{tag}", _routes_to_sibling, 4, gs, 4))
    got = _exchange(f"rs_chips_{tag}", _routes_to_chips, 3, [s[0] for s in sums], 3)
    return [_sum4(f"rs_sum_{tag}_{i}", s[1], q) for i, (s, q) in enumerate(zip(sums, got))]


def _reduce_scatter_begin(tag, gs):
    lands = [lax.empty((4,) + g.shape[1:], g.dtype) for g in gs]
    swap, token = _exchange_start(f"rs_swap_{tag}_start", _routes_to_sibling, 4, gs, lands)
    return dict(tag=tag, gs=gs, swap=swap), token


def _reduce_scatter_middle(state, after):
    tag, gs = state["tag"], state["gs"]
    from_sibling = _exchange_wait(f"rs_swap_{tag}_wait", _routes_to_sibling, 4, len(gs), state["swap"], after)
    state["sums"] = _pair_sums(tag, gs, from_sibling)
    partials = [s[0] for s in state["sums"]]
    lands = [lax.empty((3,) + p.shape[1:], p.dtype) for p in partials]
    state["chips"], token = _exchange_start(f"rs_chips_{tag}_start", _routes_to_chips, 3, partials, lands)
    return token


def _reduce_scatter_end(state, after):
    tag = state["tag"]
    got = _exchange_wait(f"rs_chips_{tag}_wait", _routes_to_chips, 3, len(state["gs"]), state["chips"], after)
    return [_sum4(f"rs_sum_{tag}_{i}", s[1], q) for i, (s, q) in enumerate(zip(state["sums"], got))]


def _all_gather_begin(tag, blocks, after):
    dev = 4 * lax.axis_index("x") + 2 * lax.axis_index("y") + lax.axis_index("c")
    zones = [lax.dynamic_update_slice_in_dim(lax.empty((N_DEV,) + b.shape, b.dtype), b[None], dev, axis=0) for b in blocks]
    chips, token = _exchange_start(f"gather_{tag}_chips_start", _routes_block_to_chips, 3, zones, None, after)
    return dict(tag=tag, n=len(blocks), chips=chips), token


def _all_gather_middle(state, after):
    tag, n = state["tag"], state["n"]
    zones = _exchange_wait(f"gather_{tag}_chips_wait", _routes_block_to_chips, 3, n, state["chips"], after)
    state["sibling"], token = _exchange_start(f"gather_{tag}_sibling_start", _routes_blocks_to_sibling, 4, zones, None)
    return token


def _all_gather_end(state, after):
    return _exchange_wait(f"gather_{state['tag']}_sibling_wait", _routes_blocks_to_sibling, 4, state["n"], state["sibling"], after)


PACK_UNIT = 8 * LANES


def _packed_size(shape):
    return -(-math.prod(shape) // PACK_UNIT) * PACK_UNIT


def _pack(arrays, dtype):
    parts = []
    for a in arrays:
        flat = a.reshape(-1).astype(dtype)
        parts.append(jnp.pad(flat, (0, _packed_size(a.shape) - flat.shape[0])))
    return jnp.concatenate(parts).reshape(-1, LANES)


def _unpack(flat, shapes, lead=()):
    flat = flat.reshape(lead + (-1,))
    out, pos = [], 0
    for s in shapes:
        out.append(flat[..., pos:pos + math.prod(s)].reshape(lead + tuple(s)))
        pos += _packed_size(s)
    return out


def _ffn_pad_rows(a):
    n = a.shape[0] // FFN_HALF
    a = jnp.pad(a.reshape(n, FFN_HALF, a.shape[1]), ((0, 0), (0, FFN_HALF_PAD - FFN_HALF), (0, 0)))
    return a.reshape(n * FFN_HALF_PAD, a.shape[2])


def _ffn_unpad_rows(a):
    n = a.shape[0] // FFN_HALF_PAD
    return a.reshape(n, FFN_HALF_PAD, a.shape[1])[:, :FFN_HALF].reshape(n * FFN_HALF, a.shape[1])


def _ffn_pad_cols(a):
    n = a.shape[1] // FFN_HALF
    a = jnp.pad(a.reshape(a.shape[0], n, FFN_HALF), ((0, 0), (0, 0), (0, FFN_HALF_PAD - FFN_HALF)))
    return a.reshape(a.shape[0], n * FFN_HALF_PAD)


def _ffn_unpad_cols(a):
    n = a.shape[1] // FFN_HALF_PAD
    return a.reshape(a.shape[0], n, FFN_HALF_PAD)[:, :, :FFN_HALF].reshape(a.shape[0], n * FFN_HALF)


def _shard_to_send(name, shard):
    if name in ("w_in", "w_br_gdn", "w_br_gla"):
        shard = shard.T
    elif name == "ffn_w_up":
        shard = _ffn_pad_rows(shard.T)
    return shard.astype(MXU_DTYPE)


KEPT_TRANSPOSED = ("w_in", "w_br_gdn", "w_br_gla", "ffn_w_up")


def _whole_from_gathered(name, g):
    if name == "w_in":
        return _in_proj_from_shards(g)
    if name == "ffn_w_down":
        return jnp.pad(g, ((0, 0), (0, FFN_HALF_PAD - FFN_HALF), (0, 0))).reshape(FFN_PAD, g.shape[2])
    return g.reshape(N_DEV * g.shape[1], g.shape[2])


def _slots_from_whole(name, gw):
    if name == "w_in":
        return _in_proj_to_slots(gw)
    return gw.reshape(N_DEV, gw.shape[0] // N_DEV, gw.shape[1])


def _shard_from_slot(name, s):
    if name == "ffn_w_up":
        return _ffn_unpad_rows(s)
    if name == "ffn_w_down":
        return s[:FFN_HALF]
    return s


def _in_proj_pieces():
    starts, pos = {}, 0
    for n, width in IN_SPLITS:
        starts[n] = (pos, width)
        pos += width
    return [(starts[ref][0], off + lane, starts[ref][1]) for _, off, _, pieces in PAD_SEGS for ref, lane in pieces]


def _in_proj_moves():
    cs = IN_DIM // N_DEV
    moves = []
    for src, dst, n in sorted(_in_proj_pieces()):
        at = src
        while at < src + n:
            d = at // cs
            end = min(src + n, (d + 1) * cs)
            moves.append((d, at - d * cs, dst + at - src, end - at))
            at = end
    return moves


RELAYOUT_LANES = 128


def _in_proj_from_shards(g):
    _, cs, D = g.shape

    def body(g_ref, o_ref):
        o_ref[...] = jnp.zeros_like(o_ref)
        for d, i0, r0, n in _in_proj_moves():
            o_ref[r0:r0 + n, :] = g_ref[d, i0:i0 + n, :]

    cb = RELAYOUT_LANES
    return pl.pallas_call(body, name="w_in_rows_in", grid=(D // cb,),
                          in_specs=[pl.BlockSpec((N_DEV, cs, cb), lambda j: (0, 0, j))],
                          out_specs=pl.BlockSpec((IN_PAD, cb), lambda j: (0, j)),
                          out_shape=jax.ShapeDtypeStruct((IN_PAD, D), g.dtype), compiler_params=_cparams(("parallel",)))(g)


def _in_proj_to_slots(gw):
    D = gw.shape[1]
    cs = IN_DIM // N_DEV

    def body(x_ref, o_ref):
        for d, i0, r0, n in _in_proj_moves():
            o_ref[d, i0:i0 + n, :] = x_ref[r0:r0 + n, :]

    cb = RELAYOUT_LANES
    return pl.pallas_call(body, name="w_in_rows_out", grid=(D // cb,),
                          in_specs=[pl.BlockSpec((IN_PAD, cb), lambda j: (0, j))],
                          out_specs=pl.BlockSpec((N_DEV, cs, cb), lambda j: (0, 0, j)),
                          out_shape=jax.ShapeDtypeStruct((N_DEV, cs, D), gw.dtype), compiler_params=_cparams(("parallel",)))(gw)


def _pad_in_proj_rows(w):
    rows, at = [], 0
    for src, dst, n in sorted(_in_proj_pieces(), key=lambda p: p[1]):
        if dst > at:
            rows.append(jnp.zeros((dst - at, w.shape[1]), w.dtype))
        rows.append(w[src:src + n])
        at = dst + n
    rows.append(jnp.zeros((IN_PAD - at, w.shape[1]), w.dtype))
    return jnp.concatenate(rows, axis=0)


def _unpad_in_proj_rows(wp):
    return jnp.concatenate([wp[dst:dst + n] for _, dst, n in sorted(_in_proj_pieces())], axis=0)


def _lane_pad(a, width=LANES):
    return jnp.pad(a, ((0, 0), (0, width - a.shape[1])))


def _seg_blk(h, name, rows):
    off, width = SEG[name]
    return (h, rows, width, off // width)


def _ln_both(xs_, ps_):
    (y,) = _ln_fn(xs_, ps_)
    return (y, y)


def _behind(param, hooks, stage, *seen):
    if hooks is None or stage not in hooks:
        return param
    token = hooks[stage](*seen)
    return param if token is None else param + token[0:1, 0:1]


def _layer_fwd(l, x, x_mx, W, sp, hooks=None):
    T = x.shape[0]
    n64, ngla, ntok = T // SSD_CHUNK, T // GLA_BLOCK, T // 256
    h = _mm(f"in_proj_{l}", x_mx, W["w_in"], "nt")
    xbc = _conv_silu_fwd(f"ssd_conv_{l}", h, SEG["xbc"][0], sp["ssd_conv_w"], sp["ssd_conv_b"])
    gqkv = _conv_silu_fwd(f"gdn_conv_{l}", h, SEG["gqkv"][0], sp["gdn_conv_w"], None)

    ssd_in = [(xbc, SSD_CHUNK, SSD_XBC, 0), _seg_blk(h, "dt", SSD_CHUNK), _seg_blk(h, "z", SSD_CHUNK)]
    ssd_p = [sp["ssd_dt_bias"], sp["ssd_a_log"], sp["ssd_d"], sp["ssd_norm_w"]]
    o_ssd, ssd_states = _chain_fwd(f"ssd_fwd_{l}", _ssd_chunk, n64, ssd_in, ssd_p, [(SSD_CHUNK, SSD_INNER, MXU_DTYPE)],
                                   (SSD_STATE, SSD_INNER))
    o_gdn, gdn_saved = _gdn_forward(str(l), gqkv, h, dict(sp, gdn_a_log=_behind(sp["gdn_a_log"], hooks, "ssd", o_ssd)))
    gla_in = [_seg_blk(h, "lqkv", GLA_BLOCK), _seg_blk(h, "lglr", GLA_BLOCK), _seg_blk(h, "lr", GLA_BLOCK)]
    gla_p = [jnp.pad(sp["gla_gate_w2"], ((0, LANES - GLA_RANK), (0, 0))), sp["gla_gate_b"], sp["gla_norm_w"]]
    o_gla, gla_states = _chain_fwd(f"gla_fwd_{l}", _gla_block, ngla, gla_in, gla_p, [(GLA_BLOCK, GLA_V, MXU_DTYPE)],
                                   (GLA_VAL_DIM, GLA_K))
    ln1_p = [_behind(sp["ln1_g"], hooks, "mixed", o_gdn), sp["ln1_b"]]
    y_ssd = _mm(f"br_ssd_{l}", o_ssd, W["w_br_ssd"])
    y_gdn = _mm(f"br_gdn_{l}", o_gdn, W["w_br_gdn"], "nt")
    y_gla = _mm(f"br_gla_{l}", o_gla, W["w_br_gla"], "nt")
    merge_in = [_seg_blk(h, "gates", 256), (y_ssd, 256, D_MODEL, 0), (y_gdn, 256, D_MODEL, 0), (y_gla, 256, D_MODEL, 0)]
    (mix,) = _chain_fwd(f"merge_{l}", _merge_fn, ntok, merge_in, [], [(256, D_MODEL, MXU_DTYPE)])
    r1 = _mm(f"out_proj_{l}", mix, W["w_out"])
    both = [(256, D_MODEL, F32), (256, D_MODEL, MXU_DTYPE)]
    x1, x1_mx = _chain_fwd(f"ln1_{l}", _ln_both, ntok, [(x, 256, D_MODEL, 0), (r1, 256, D_MODEL, 0)], ln1_p, both)
    up = _mm(f"ffn_up_{l}", x1_mx, W["ffn_w_up"], "nt")
    act = _ffn_glu_fwd(f"ffn_glu_{l}", up, sp["ffn_conv_w_pad"], sp["ffn_conv_b_pad"], MXU_DTYPE)
    ln2_p = [_behind(sp["ln2_g"], hooks, "ffn_act", act), sp["ln2_b"]]
    r2 = _mm(f"ffn_down_{l}", act, W["ffn_w_down"])
    x2, x2_mx = _chain_fwd(f"ln2_{l}", _ln_both, ntok, [(x1, 256, D_MODEL, 0), (r2, 256, D_MODEL, 0)], ln2_p, both)
    saved = dict(x=x, x_mx=x_mx, h=h, xbc=xbc, gqkv=gqkv, ssd_in=ssd_in, ssd_p=ssd_p, ssd_states=ssd_states,
                 gdn=gdn_saved, gla_in=gla_in, gla_p=gla_p, gla_states=gla_states, o_ssd=o_ssd,
                 o_gdn=o_gdn, o_gla=o_gla, merge_in=merge_in, mix=mix, r1=r1, ln1_p=ln1_p, x1=x1, x1_mx=x1_mx, up=up, act=act,
                 r2=r2, ln2_p=ln2_p)
    return x2, x2_mx, saved


def _layer_bwd(l, dx2, W, sp, sv, hooks=None):
    T = dx2.shape[0]
    n64, ngla, ntok = T // SSD_CHUNK, T // GLA_BLOCK, T // 256
    bf = MXU_DTYPE
    gw, gs = {}, {}
    ln2_p = [_behind(sv["ln2_p"][0], hooks, "start"), sv["ln2_p"][1]]
    (dx1_a, dr2), (gs["ln2_g"], gs["ln2_b"]) = _chain_bwd(
        f"ln2_bwd_{l}", _ln_fn, ntok, [(sv["x1"], 256, D_MODEL, 0), (sv["r2"], 256, D_MODEL, 0)], ln2_p,
        [(dx2, 256, D_MODEL)], dx_dtypes=[F32, bf])
    gw["ffn_w_down"] = _mm(f"ffn_down_dw_{l}", sv["act"], dr2, "tn")
    dact = _mm(f"ffn_down_dx_{l}", dr2, W["ffn_w_down"], "nt")
    dg, du, dwg, dwu, dbg, dbu = _ffn_glu_bwd(f"ffn_glu_bwd_{l}", sv["up"], sp["ffn_conv_w_pad"], sp["ffn_conv_b_pad"], dact, bf)
    gs["ffn_conv_w"] = _ffn_unpad_cols(jnp.concatenate([dwg, dwu], axis=1))
    gs["ffn_conv_b"] = _ffn_unpad_cols(jnp.concatenate([dbg, dbu], axis=1))
    dup = jnp.concatenate([dg, du], axis=1)
    gw["ffn_w_up"] = _mm(f"ffn_up_dw_{l}", dup, sv["x1_mx"], "tn", tn=1024)
    dx1_b = _mm(f"ffn_up_dx_{l}", dup, W["ffn_w_up"], "nn", tn=1024, tk=1024)
    ln1_p = [_behind(sv["ln1_p"][0], hooks, "ffn", dx1_b), sv["ln1_p"][1]]
    (dx_a, dr1), (gs["ln1_g"], gs["ln1_b"]) = _chain_bwd(
        f"ln1_bwd_{l}", _ln_sum_fn, ntok, [(sv["x"], 256, D_MODEL, 0), (sv["r1"], 256, D_MODEL, 0)], ln1_p,
        [(dx1_a, 256, D_MODEL), (dx1_b, 256, D_MODEL)], dx_dtypes=[F32, bf])
    gw["w_out"] = _mm(f"out_proj_dw_{l}", sv["mix"], dr1, "tn")
    dmix = _mm(f"out_proj_dx_{l}", dr1, W["w_out"], "nt")
    (dgates, dy_ssd, dy_gdn, dy_gla), _ = _chain_bwd(f"merge_bwd_{l}", _merge_fn, ntok, sv["merge_in"], [],
                                                     [(dmix, 256, D_MODEL)], dx_dtypes=[bf, bf, bf, bf])
    gw["w_br_ssd"] = _mm(f"br_ssd_dw_{l}", sv["o_ssd"], dy_ssd, "tn")
    gw["w_br_gdn"] = _mm(f"br_gdn_dw_{l}", dy_gdn, sv["o_gdn"], "tn")
    gw["w_br_gla"] = _mm(f"br_gla_dw_{l}", dy_gla, sv["o_gla"], "tn")
    do_ssd = _mm(f"br_ssd_dx_{l}", dy_ssd, W["w_br_ssd"], "nt")
    do_gdn = _mm(f"br_gdn_dx_{l}", dy_gdn, W["w_br_gdn"], "nn")
    do_gla = _mm(f"br_gla_dx_{l}", dy_gla, W["w_br_gla"], "nn")

    ssd_p = [_behind(sv["ssd_p"][0], hooks, "branches", do_gla, gw)] + list(sv["ssd_p"][1:])
    (dxbc, ddt, dz), dps = _chain_bwd(f"ssd_bwd_{l}", _ssd_chunk, n64, sv["ssd_in"], ssd_p,
                                      [(do_ssd, SSD_CHUNK, SSD_INNER)], sprev=sv["ssd_states"], dx_dtypes=[F32, bf, bf])
    gs["ssd_dt_bias"], gs["ssd_a_log"], gs["ssd_d"], gs["ssd_norm_w"] = dps
    gdn_sv = dict(sv["gdn"], scan_p=[_behind(sv["gdn"]["scan_p"][0], hooks, "ssd", dz)])
    dgqkv, dgab, dgg, gs["gdn_a_log"], gs["gdn_dt_bias"], gs["gdn_norm_w"] = _gdn_backward(str(l), do_gdn, gdn_sv, bf)
    (dlqkv, dlglr, dlr), dps = _chain_bwd(f"gla_bwd_{l}", _gla_block, ngla, sv["gla_in"], sv["gla_p"],
                                          [(do_gla, GLA_BLOCK, GLA_V)], sprev=sv["gla_states"], dx_dtypes=[bf, bf, bf])
    gs["gla_gate_w2"], gs["gla_gate_b"], gs["gla_norm_w"] = dps[0][:GLA_RANK], dps[1], dps[2]
    dxbc_pre, gs["ssd_conv_w"], gs["ssd_conv_b"] = _conv_silu_bwd(
        f"ssd_conv_bwd_{l}", sv["h"], SEG["xbc"][0], sp["ssd_conv_w"], sp["ssd_conv_b"], dxbc, bf)
    dgqkv_pre, gs["gdn_conv_w"] = _conv_silu_bwd(f"gdn_conv_bwd_{l}", sv["h"], SEG["gqkv"][0], sp["gdn_conv_w"], None, dgqkv, bf)
    pieces = dict(gates=dgates, xbc=dxbc_pre, gqkv=dgqkv_pre, z=dz, lqkv=dlqkv, gg=dgg, lr=dlr, dt=ddt, gab=dgab, lglr=dlglr)
    cols = [pieces[name] for name, _, _, _ in PAD_SEGS]
    cols.append(jnp.zeros((T, IN_PAD - PAD_SEGS[-1][1] - PAD_SEGS[-1][2]), bf))
    dh = jnp.concatenate(cols, axis=1)
    gw["w_in"] = _mm(f"in_proj_dw_{l}", dh, sv["x_mx"], "tn", tn=1024)
    behind = hooks["w_in_grad"](gw) if hooks is not None and "w_in_grad" in hooks else None
    dx_b = _mm(f"in_proj_dx_{l}", dh, W["w_in"], "nn", tm=1024, tn=1024, tk=IN_PAD // 4, after=behind)
    dx = _add_blocks(f"dx_add_{l}", dx_a[None], dx_b[None])[0]
    return dx, gw, gs


def _ln_sum_fn(xs_, ps_):
    (y,) = _ln_fn(xs_, ps_)
    return (y, y)


def _small_2d(name, a):
    return a.reshape(1, -1) if a.ndim == 1 else a


def kernel(x, w_in, ssd_conv_w, ssd_conv_b, ssd_dt_bias, ssd_a_log, ssd_d, ssd_norm_w, gdn_conv_w, gdn_a_log, gdn_dt_bias, gdn_norm_w, gla_gate_w2, gla_gate_b, gla_norm_w, w_br_ssd, w_br_gdn, w_br_gla, w_out, ln1_g, ln1_b, ffn_w_up, ffn_conv_w, ffn_conv_b, ffn_w_down, ln2_g, ln2_b, loss_target, m_w_in, m_ssd_conv_w, m_ssd_conv_b, m_ssd_dt_bias, m_ssd_a_log, m_ssd_d, m_ssd_norm_w, m_gdn_conv_w, m_gdn_a_log, m_gdn_dt_bias, m_gdn_norm_w, m_gla_gate_w2, m_gla_gate_b, m_gla_norm_w, m_w_br_ssd, m_w_br_gdn, m_w_br_gla, m_w_out, m_ln1_g, m_ln1_b, m_ffn_w_up, m_ffn_conv_w, m_ffn_conv_b, m_ffn_w_down, m_ln2_g, m_ln2_b, v_w_in, v_ssd_conv_w, v_ssd_conv_b, v_ssd_dt_bias, v_ssd_a_log, v_ssd_d, v_ssd_norm_w, v_gdn_conv_w, v_gdn_a_log, v_gdn_dt_bias, v_gdn_norm_w, v_gla_gate_w2, v_gla_gate_b, v_gla_norm_w, v_w_br_ssd, v_w_br_gdn, v_w_br_gla, v_w_out, v_ln1_g, v_ln1_b, v_ffn_w_up, v_ffn_conv_w, v_ffn_conv_b, v_ffn_w_down, v_ln2_g, v_ln2_b):
    args = locals()
    w = {n: args[n] for n in WEIGHTS}
    m = {n: args["m_" + n] for n in WEIGHTS}
    v = {n: args["v_" + n] for n in WEIGHTS}
    dev = 4 * lax.axis_index("x") + 2 * lax.axis_index("y") + lax.axis_index("c")
    xl = x[0]
    tgt = loss_target[0]

    late = BIG[1:]

    def send(names, l):
        return [_shard_to_send(n, w[n][l]) for n in names]

    def whole_weights(names, got):
        return {n: _whole_from_gathered(n, g) for n, g in zip(names, got)}

    got0 = _all_gather("gather_first", send(BIG[:1], 0) + [w[n] for n in SMALL_SHARDED])
    gather0, token0 = _all_gather_begin("w_0", send(late, 0), got0[0])
    W = [whole_weights(BIG[:1], got0[:1]), None]
    whole = dict(w)
    for n, s in zip(SMALL_SHARDED, got0[1:]):
        whole[n] = jnp.transpose(s, (1, 2, 0, 3)).reshape(s.shape[1], s.shape[2], N_DEV * s.shape[3])
    SP = [{n: _small_2d(n, whole[n][l]) for n in SMALL} for l in range(DEPTH)]
    for sp in SP:
        sp["ffn_conv_w_pad"] = _ffn_pad_cols(sp["ffn_conv_w"])
        sp["ffn_conv_b_pad"] = _ffn_pad_cols(sp["ffn_conv_b"])

    held = {}

    def late_weights_cross(o_ssd):
        token = _all_gather_middle(gather0, o_ssd)
        held["gather1"], token1 = _all_gather_begin("w_1", send(BIG, 1), o_ssd)
        return token + token1

    def late_weights_arrive(mixed):
        W[0].update(whole_weights(late, _all_gather_end(gather0, mixed)))

    fwd_hooks = {"ssd": late_weights_cross, "mixed": late_weights_arrive,
                 "ffn_act": lambda act: _all_gather_middle(held["gather1"], act)}
    saved = [None] * DEPTH
    act, act_mx, saved[0] = _layer_fwd(0, xl, (xl + token0[0, 0]).astype(MXU_DTYPE), W[0], SP[0], hooks=fwd_hooks)
    W[1] = whole_weights(BIG, _all_gather_end(held["gather1"], act))
    act, act_mx, saved[1] = _layer_fwd(1, act, act_mx, W[1], SP[1])
    dy, loss_parts = _loss_head(act, tgt)
    loss = lax.psum(jnp.sum(loss_parts), ("x", "y", "c"))

    def slots_of(names, gw):
        return [_slots_from_whole(n, gw[n]) for n in names]

    grads = {}
    GS = [None] * DEPTH
    dy, gw, GS[1] = _layer_bwd(1, dy, W[1], SP[1], saved[1])
    reduce1, reduce1_token = _reduce_scatter_begin("1", slots_of(BIG, gw))

    def late_grads_leave(seen, gw0):
        held["reduce0"], token = _reduce_scatter_begin("0", slots_of(late, gw0))
        return token

    def w_in_grad_leaves(gw0):
        held["reduce_first"], token = _reduce_scatter_begin("first", slots_of(BIG[:1], gw0))
        return token

    bwd_hooks = {"start": lambda: reduce1_token, "ffn": lambda seen: _reduce_scatter_middle(reduce1, seen),
                 "branches": late_grads_leave, "ssd": lambda seen: _reduce_scatter_middle(held["reduce0"], seen),
                 "w_in_grad": w_in_grad_leaves}
    dy, gw, GS[0] = _layer_bwd(0, dy, W[0], SP[0], saved[0], hooks=bwd_hooks)
    first_token = _reduce_scatter_middle(held["reduce_first"], dy)
    red1 = _reduce_scatter_end(reduce1, dy)
    red0_late = _reduce_scatter_end(held["reduce0"], dy)
    grad_x = dy[None]
    kept_t = KEPT_TRANSPOSED
    grads_k = {n: jnp.stack([_shard_from_slot(n, red0_late[i]), _shard_from_slot(n, red1[i + 1])]) for i, n in enumerate(late)}

    small_shapes = [whole[n].shape for n in SMALL]
    gs_flat = _pack([jnp.stack([GS[l][n].reshape(whole[n].shape[1:]) for l in range(DEPTH)]) for n in SMALL], F32)
    (gs_all,) = _all_gather("gather_small_grads", [gs_flat])

    def mine(n, a):
        if n in SMALL_SHARDED:
            cs = a.shape[-1] // N_DEV
            return lax.dynamic_slice_in_dim(a, dev * cs, cs, axis=a.ndim - 1)
        return a

    m_whole, v_whole = {}, {}
    for n in SMALL:
        reps = (1, 1, N_DEV) if n in SMALL_SHARDED else (1,) * m[n].ndim
        m_whole[n], v_whole[n] = jnp.tile(m[n], reps), jnp.tile(v[n], reps)
    outs = _adamw_small(gs_all, _pack([whole[n] for n in SMALL], F32) + first_token[0:1, 0:1], _pack([m_whole[n] for n in SMALL], F32),
                        _pack([v_whole[n] for n in SMALL], F32))
    g_s, d_s, m_s, v_s = [_unpack(o, small_shapes) for o in outs]
    delta, new_m, new_v = {}, {}, {}
    for i, n in enumerate(SMALL):
        grads[n], delta[n], new_m[n], new_v[n] = mine(n, g_s[i]), mine(n, d_s[i]), mine(n, m_s[i]), mine(n, v_s[i])
    for n in late + BIG[:1]:
        if n == "w_in":
            done = sum(new_v[k].reshape(-1)[0:1] for k in late + SMALL[:1])
            (first0,) = _reduce_scatter_end(held["reduce_first"], done)
            grads_k[n] = jnp.stack([first0, red1[0]])
        view = (lambda a: jnp.transpose(a, (0, 2, 1))) if n in kept_t else (lambda a: a)
        outs = _adamw(f"adamw_{n}", view(w[n]), grads_k[n], view(m[n]), view(v[n]), after=None if n == "w_in" else first_token)
        grads[n], delta[n], new_m[n], new_v[n] = view(grads_k[n]), view(outs[0]), view(outs[1]), view(outs[2])

    return (loss, grad_x, *[grads[n] for n in WEIGHTS], *[delta[n] for n in WEIGHTS], *[new_m[n] for n in WEIGHTS],
            *[new_v[n] for n in WEIGHTS])
```

```python
import functools
import math

import jax
import jax.numpy as jnp
from jax import lax
from jax.experimental import pallas as pl
from jax.experimental.pallas import tpu as pltpu

F32 = jnp.float32
MXU_DTYPE = jnp.bfloat16
HI = lax.Precision.HIGHEST

N_DEV = 8
D_MODEL = 1024
DEPTH = 2
SSD_HEADS, SSD_HEAD_DIM, SSD_INNER, SSD_GROUPS, SSD_STATE, SSD_CHUNK = 16, 64, 1024, 2, 128, 64
SSD_XBC = SSD_INNER + 2 * SSD_GROUPS * SSD_STATE
GDN_HEADS, GDN_HEAD_DIM, GDN_WIDTH, GDN_CHUNK = 4, 128, 512, 64
GLA_HEADS, GLA_KEY_DIM, GLA_VAL_DIM, GLA_K, GLA_V, GLA_RANK, GLA_CHUNK = 4, 64, 128, 256, 512, 16, 16
GLA_BLOCK = 128
GLA_NORMALIZER = 16.0
FFN_DIM = 2816
FFN_HALF = FFN_DIM // 8
FFN_HALF_PAD = 384
FFN_UP_PAD = 16 * FFN_HALF_PAD
FFN_PAD = FFN_UP_PAD // 2
ALPHA = (2 * DEPTH) ** 0.25
LN_EPS = 1e-5
RMS_EPS = 1e-6
ADAM_LR, ADAM_B1, ADAM_B2, ADAM_EPS, ADAM_WD, ADAM_STEP = 0.001, 0.9, 0.999, 1e-08, 0.01, 10
LANES = 128
NEG_BIG = -1e30
VMEM_LIMIT = 56 * 1024 * 1024

IN_SPLITS = (("z", 1024), ("xbc", 1536), ("dt", 16), ("gqkv", 1536), ("ga", 4), ("gb", 4), ("gg", 512),
             ("lqkv", 1024), ("lglr", 16), ("lr", 512), ("gates", 3072))
IN_DIM = sum(w for _, w in IN_SPLITS)
PAD_SEGS = (("gates", 0, 3072, (("gates", 0),)), ("xbc", 3072, 1536, (("xbc", 0),)),
            ("gqkv", 4608, 1536, (("gqkv", 0),)), ("z", 6144, 1024, (("z", 0),)),
            ("lqkv", 7168, 1024, (("lqkv", 0),)), ("gg", 8192, 512, (("gg", 0),)), ("lr", 8704, 512, (("lr", 0),)),
            ("dt", 9216, 128, (("dt", 0),)), ("gab", 9344, 128, (("ga", 0), ("gb", 4))), ("lglr", 9472, 128, (("lglr", 0),)))
IN_PAD = 9728
SEG = {name: (off, width) for name, off, width, _ in PAD_SEGS}

BIG = ("w_in", "w_br_ssd", "w_br_gdn", "w_br_gla", "w_out", "ffn_w_up", "ffn_w_down")
COL_SHARDED = ("w_in", "w_br_gdn", "w_br_gla", "ffn_w_up")
SMALL_SHARDED = ("ssd_conv_w", "gdn_conv_w", "gla_gate_w2", "ffn_conv_w")
WEIGHTS = ("w_in", "ssd_conv_w", "ssd_conv_b", "ssd_dt_bias", "ssd_a_log", "ssd_d", "ssd_norm_w", "gdn_conv_w",
           "gdn_a_log", "gdn_dt_bias", "gdn_norm_w", "gla_gate_w2", "gla_gate_b", "gla_norm_w", "w_br_ssd", "w_br_gdn",
           "w_br_gla", "w_out", "ln1_g", "ln1_b", "ffn_w_up", "ffn_conv_w", "ffn_conv_b", "ffn_w_down", "ln2_g", "ln2_b")
SMALL = tuple(n for n in WEIGHTS if n not in BIG)
FLAT_W = 512


def _cparams(sem=None):
    kw = dict(vmem_limit_bytes=VMEM_LIMIT)
    if sem is not None:
        kw["dimension_semantics"] = sem
    return pltpu.CompilerParams(**kw)


_DIMS = {"nn": (((1,), (0,)), ((), ())), "nt": (((1,), (1,)), ((), ())), "tn": (((0,), (0,)), ((), ()))}


def _dot(a, b, dims="nn"):
    if MXU_DTYPE == F32:
        return lax.dot_general(a.astype(F32), b.astype(F32), _DIMS[dims], precision=HI, preferred_element_type=F32)
    return lax.dot_general(a.astype(MXU_DTYPE), b.astype(MXU_DTYPE), _DIMS[dims], preferred_element_type=F32)


def _dot_hi(a, b, dims="nn"):
    return lax.dot_general(a.astype(F32), b.astype(F32), _DIMS[dims], precision=HI, preferred_element_type=F32)


def _iota2(shape, axis):
    return lax.broadcasted_iota(jnp.int32, shape, axis)


def _tril(n, strict=False):
    r, c = _iota2((n, n), 0), _iota2((n, n), 1)
    return (r > c) if strict else (r >= c)


def _raw_dot(a, b, dims):
    return lax.dot_general(a, b, _DIMS[dims], preferred_element_type=F32)


def _dot_x3(a, b, dims="nn"):
    if MXU_DTYPE == F32:
        return _dot_hi(a, b, dims)
    ah, bh = a.astype(jnp.bfloat16), b.astype(jnp.bfloat16)
    al, bl = (a - ah.astype(F32)).astype(jnp.bfloat16), (b - bh.astype(F32)).astype(jnp.bfloat16)
    return _raw_dot(ah, bh, dims) + (_raw_dot(ah, bl, dims) + _raw_dot(al, bh, dims))


def _exact_dot(mask, b, dims, mask_first):
    if MXU_DTYPE == F32:
        return _dot_hi(mask, b, dims) if mask_first else _dot_hi(b, mask, dims)
    m = mask.astype(jnp.bfloat16)
    b1 = b.astype(jnp.bfloat16)
    r1 = b - b1.astype(F32)
    b2 = r1.astype(jnp.bfloat16)
    b3 = (r1 - b2.astype(F32)).astype(jnp.bfloat16)
    if mask_first:
        return _raw_dot(m, b1, dims) + (_raw_dot(m, b2, dims) + _raw_dot(m, b3, dims))
    return _raw_dot(b1, m, dims) + (_raw_dot(b2, m, dims) + _raw_dot(b3, m, dims))


@jax.custom_vjp
def _mask_left(mask, b):
    return _exact_dot(mask, b, "nn", True)


_mask_left.defvjp(lambda mask, b: (_mask_left(mask, b), mask),
                  lambda mask, d: (jnp.zeros_like(mask), _exact_dot(mask, d, "tn", True)))


@jax.custom_vjp
def _mask_right(a, mask):
    return _exact_dot(mask, a, "nn", False)


_mask_right.defvjp(lambda a, mask: (_mask_right(a, mask), mask),
                   lambda mask, d: (_exact_dot(mask, d, "nt", False), jnp.zeros_like(mask)))


@jax.custom_vjp
def _unit_lower_inverses(mats):
    n = mats[0].shape[0]
    eye = (_iota2((n, n), 0) == _iota2((n, n), 1)).astype(F32)
    xs = [eye - a for a in mats]
    ps = list(mats)
    k = 2
    while k < n:
        ps = [_dot_x3(p, p) for p in ps]
        xs = [x + _dot_x3(x, p) for x, p in zip(xs, ps)]
        k *= 2
    return xs


def _unit_lower_inverses_fwd(mats):
    ts = _unit_lower_inverses(mats)
    return ts, ts


def _unit_lower_inverses_bwd(ts, dts):
    mids = [_dot_x3(t, d, "tn") for t, d in zip(ts, dts)]
    return ([-_dot_x3(m, t, "nt") for m, t in zip(mids, ts)],)


_unit_lower_inverses.defvjp(_unit_lower_inverses_fwd, _unit_lower_inverses_bwd)


def _ssd_chunk(xs_, ps_, s_t):
    xbc, dtraw, z = xs_
    dt_bias, a_log, d_skip, norm_w = ps_
    L = xbc.shape[0]
    H, P, N, G = SSD_HEADS, SSD_HEAD_DIM, SSD_STATE, SSD_GROUPS
    W = SSD_INNER // G
    xs = xbc[:, :SSD_INNER]
    bm = xbc[:, SSD_INNER:SSD_INNER + G * N]
    cm = xbc[:, SSD_INNER + G * N:]
    dt = jax.nn.softplus(dtraw[:, :H] + dt_bias)
    a = dt * (-jnp.exp(a_log))
    causal = _tril(L)
    a_cs = _mask_left(causal.astype(F32), a)
    expand = (_iota2((H, SSD_INNER), 1) // P == _iota2((H, SSD_INNER), 0)).astype(F32)
    wide = _mask_right(jnp.concatenate([a_cs, dt, jnp.broadcast_to(d_skip, (L, H))], axis=0), expand)
    a_cs_x, dt_x, d_x = wide[:L], wide[L:2 * L], wide[2 * L:]
    a_end_x = a_cs_x[L - 1:L, :]
    a_cs_t, dt_t = a_cs.T, dt.T
    cb = [_dot(cm[:, g * N:(g + 1) * N], bm[:, g * N:(g + 1) * N], "nt") for g in range(G)]
    decay = [jnp.exp(jnp.where(causal, a_cs[:, h:h + 1] - a_cs_t[h:h + 1, :], NEG_BIG)) * dt_t[h:h + 1, :] for h in range(H)]
    ws = [cb[h // (H // G)] * decay[h] for h in range(H)]
    y = jnp.concatenate([_dot(ws[h], xs[:, h * P:(h + 1) * P]) for h in range(H)], axis=1)
    y_in = jnp.concatenate([_dot(cm[:, g * N:(g + 1) * N], s_t[:, g * W:(g + 1) * W]) for g in range(G)], axis=1)
    y = y + y_in * jnp.exp(a_cs_x) + d_x * xs
    xw = xs * (jnp.exp(a_end_x - a_cs_x) * dt_x)
    st = jnp.concatenate([_dot(bm[:, g * N:(g + 1) * N], xw[:, g * W:(g + 1) * W], "tn") for g in range(G)], axis=1)
    s_new = s_t * jnp.exp(a_end_x) + st
    yg = y * jax.nn.silu(z)
    outs = []
    for g in range(G):
        part = yg[:, g * W:(g + 1) * W]
        outs.append(part * lax.rsqrt(jnp.mean(part * part, axis=1, keepdims=True) + RMS_EPS))
    return (jnp.concatenate(outs, axis=1) * norm_w,), s_new


GDN_PREP_CHUNKS = 4


def _gdn_prep(xs_, ps_):
    qkv, ab = xs_
    a_log, dt_bias = ps_
    B = qkv.shape[0]
    H, D, L = GDN_HEADS, GDN_HEAD_DIM, GDN_CHUNK
    g_all = -jnp.exp(a_log) * jax.nn.softplus(ab + dt_bias)
    row, col = _iota2((B, B), 0), _iota2((B, B), 1)
    g_cs = _mask_left((((row // L) == (col // L)) & (row >= col)).astype(F32), g_all)
    g_cs_t = g_cs.T
    beta_all = jax.nn.sigmoid(ab)
    incl, strict = _tril(L), _tril(L, strict=True)
    qs, ks, vs = [], [], []
    for h in range(H):
        q = qkv[:, h * D:(h + 1) * D]
        k = qkv[:, GDN_WIDTH + h * D:GDN_WIDTH + (h + 1) * D]
        qs.append(q * lax.rsqrt(jnp.sum(q * q, axis=1, keepdims=True) + RMS_EPS) * (D ** -0.5))
        ks.append(k * lax.rsqrt(jnp.sum(k * k, axis=1, keepdims=True) + RMS_EPS))
        vs.append(qkv[:, 2 * GDN_WIDTH + h * D:2 * GDN_WIDTH + (h + 1) * D])
    pairs = [(c, h) for c in range(B // L) for h in range(H)]
    rows = {c: slice(c * L, (c + 1) * L) for c in range(B // L)}
    q_ = {(c, h): qs[h][rows[c]] for c, h in pairs}
    k_ = {(c, h): ks[h][rows[c]] for c, h in pairs}
    col_ = {(c, h): g_cs[rows[c], h:h + 1] for c, h in pairs}
    beta_ = {(c, h): beta_all[rows[c], H + h:H + h + 1] for c, h in pairs}
    gamma = {p: jnp.exp(jnp.where(incl, col_[p] - g_cs_t[p[1]:p[1] + 1, rows[p[0]]], NEG_BIG)) for p in pairs}
    kb = {p: k_[p] * beta_[p] for p in pairs}
    a_mat = [jnp.where(strict, _dot(kb[p], k_[p], "nt") * gamma[p], 0.0) for p in pairs]
    attn = {p: jnp.where(incl, _dot(q_[p], k_[p], "nt") * gamma[p], 0.0) for p in pairs}
    t_mat = dict(zip(pairs, _unit_lower_inverses(a_mat)))
    u = {p: _dot(t_mat[p], vs[p[1]][rows[p[0]]] * beta_[p]) for p in pairs}
    w = {p: _dot(t_mat[p], kb[p] * jnp.exp(col_[p])) for p in pairs}
    qd = {p: q_[p] * jnp.exp(col_[p]) for p in pairs}
    kd = {p: k_[p] * jnp.exp(col_[p][L - 1:L, :] - col_[p]) for p in pairs}

    def whole(parts):
        return jnp.concatenate([jnp.concatenate([parts[(c, h)] for h in range(H)], axis=1) for c in range(B // L)], axis=0)

    return (whole(u), whole(w), whole(qd), whole(kd), whole(attn), g_cs)


def _gdn_scan(xs_, ps_, s):
    u, w, qd, kd, attn, g_cs, gate = xs_
    (norm_w,) = ps_
    L = u.shape[0]
    H, D = GDN_HEADS, GDN_HEAD_DIM
    heads = range(H)
    lanes = [slice(h * D, (h + 1) * D) for h in heads]
    s_h = [s[lanes[h], :] for h in heads]
    v_new = [u[:, lanes[h]] - _dot(w[:, lanes[h]], s_h[h]) for h in heads]
    o = [_dot(qd[:, lanes[h]], s_h[h]) + _dot(attn[:, h * L:(h + 1) * L], v_new[h]) for h in heads]
    decay = [jnp.exp(g_cs[L - 1:L, h:h + 1]) for h in heads]
    s_new = [s_h[h] * decay[h] + _dot(kd[:, lanes[h]], v_new[h], "tn") for h in heads]
    o = [o[h] * lax.rsqrt(jnp.mean(o[h] * o[h], axis=1, keepdims=True) + RMS_EPS) * norm_w * jax.nn.silu(gate[:, lanes[h]])
         for h in heads]
    return (jnp.concatenate(o, axis=1),), jnp.concatenate(s_new, axis=0)


def _gdn_forward(tag, gqkv, h, sp):
    T = gqkv.shape[0]
    blk = GDN_PREP_CHUNKS * GDN_CHUNK
    prep_in = [(gqkv, blk, 3 * GDN_WIDTH, 0), _seg_blk(h, "gab", blk)]
    prep_p = [_lane_pad(sp["gdn_a_log"]), _lane_pad(sp["gdn_dt_bias"])]
    mx = MXU_DTYPE
    prep = _chain_fwd(f"gdn_prep_{tag}", _gdn_prep, T // blk, prep_in, prep_p,
                      [(blk, GDN_WIDTH, F32), (blk, GDN_WIDTH, mx), (blk, GDN_WIDTH, mx), (blk, GDN_WIDTH, mx),
                       (blk, GDN_HEADS * GDN_CHUNK, mx), (blk, LANES, F32)])
    widths = [GDN_WIDTH] * 4 + [GDN_HEADS * GDN_CHUNK, LANES]
    scan_in = [(a, GDN_CHUNK, wd, 0) for a, wd in zip(prep, widths)] + [_seg_blk(h, "gg", GDN_CHUNK)]
    scan_p = [sp["gdn_norm_w"]]
    o, states = _chain_fwd(f"gdn_scan_{tag}", _gdn_scan, T // GDN_CHUNK, scan_in, scan_p, [(GDN_CHUNK, GDN_WIDTH, mx)],
                           (GDN_WIDTH, GDN_HEAD_DIM))
    return o, dict(prep_in=prep_in, prep_p=prep_p, scan_in=scan_in, scan_p=scan_p, states=states, widths=widths)


def _gdn_backward(tag, do, sv, dx_dtype):
    T = do.shape[0]
    blk = GDN_PREP_CHUNKS * GDN_CHUNK
    dscan, (dnorm,) = _chain_bwd(f"gdn_scan_bwd_{tag}", _gdn_scan, T // GDN_CHUNK, sv["scan_in"], sv["scan_p"],
                                 [(do, GDN_CHUNK, GDN_WIDTH)], sprev=sv["states"], dx_dtypes=[F32] * 6 + [dx_dtype])
    douts = [(d, blk, wd) for d, wd in zip(dscan[:6], sv["widths"])]
    (dgqkv, dgab), (da_log, ddt_bias) = _chain_bwd(f"gdn_prep_bwd_{tag}", _gdn_prep, T // blk, sv["prep_in"], sv["prep_p"],
                                                   douts, dx_dtypes=[F32, dx_dtype])
    return dgqkv, dgab, dscan[6], da_log[:, :GDN_HEADS], ddt_bias[:, :GDN_HEADS], dnorm


def _gla_block(xs_, ps_, s_t):
    qkv, glr, r = xs_
    w2, gate_b, norm_w = ps_
    B = qkv.shape[0]
    H, K, V, C = GLA_HEADS, GLA_KEY_DIM, GLA_VAL_DIM, GLA_CHUNK
    q = qkv[:, :GLA_K] * (K ** -0.5)
    k = qkv[:, GLA_K:2 * GLA_K]
    v = qkv[:, 2 * GLA_K:]
    gk = jax.nn.log_sigmoid(_dot(glr, w2) + gate_b) / GLA_NORMALIZER
    row, col = _iota2((B, B), 0), _iota2((B, B), 1)
    same = (row // C) == (col // C)
    mask = same & (row >= col)
    b_cs = _mask_left(mask.astype(F32), gk)
    b_end = _mask_left((col == (row // C) * C + (C - 1)).astype(F32), b_cs)
    q_e = q * jnp.exp(b_cs)
    k_e = k * jnp.exp(-b_cs)
    k_d = k * jnp.exp(b_end - b_cs)
    intra = []
    for h in range(H):
        a_mat = jnp.where(mask, _dot(q_e[:, h * K:(h + 1) * K], k_e[:, h * K:(h + 1) * K], "nt"), 0.0)
        intra.append(_dot(a_mat, v[:, h * V:(h + 1) * V]))
    o = jnp.concatenate(intra, axis=1)
    chunks = [slice(j * C, (j + 1) * C) for j in range(B // C)]
    fresh = [jnp.concatenate([_dot(v[sl, h * V:(h + 1) * V], k_d[sl, h * K:(h + 1) * K], "tn") for h in range(H)], axis=1)
             for sl in chunks]
    entering = []
    for j, sl in enumerate(chunks):
        entering.append(s_t)
        s_t = s_t * jnp.exp(b_end[j * C:j * C + 1, :]) + fresh[j]
    inter = [jnp.concatenate([_dot(q_e[sl, h * K:(h + 1) * K], entering[j][:, h * K:(h + 1) * K], "nt") for h in range(H)],
                             axis=1) for j, sl in enumerate(chunks)]
    o = o + jnp.concatenate(inter, axis=0)
    outs = []
    for h in range(H):
        oh = o[:, h * V:(h + 1) * V]
        oh = oh * lax.rsqrt(jnp.mean(oh * oh, axis=1, keepdims=True) + RMS_EPS) * norm_w
        outs.append(oh * jax.nn.silu(r[:, h * V:(h + 1) * V]))
    return (jnp.concatenate(outs, axis=1),), s_t


def _merge_fn(xs_, ps_):
    gates, y_ssd, y_gdn, y_gla = xs_
    d = D_MODEL
    return (jax.nn.sigmoid(gates[:, :d]) * y_ssd + jax.nn.sigmoid(gates[:, d:2 * d]) * y_gdn
            + jax.nn.sigmoid(gates[:, 2 * d:]) * y_gla,)


def _ln_fn(xs_, ps_):
    x, r = xs_
    g, b = ps_
    t = ALPHA * x + r
    mu = jnp.mean(t, axis=1, keepdims=True)
    var = jnp.mean(jnp.square(t - mu), axis=1, keepdims=True)
    return ((t - mu) * lax.rsqrt(var + LN_EPS) * g + b,)


def _row_spec(rows, width, colblk, n, reverse):
    if reverse:
        return pl.BlockSpec((rows, width), lambda c: (n - 1 - c, colblk))
    return pl.BlockSpec((rows, width), lambda c: (c, colblk))


def _full_spec(shape):
    zeros = (0,) * len(shape)
    return pl.BlockSpec(shape, lambda c: zeros)


def _chain_fwd(name, fn, n, blocked, full, out_defs, state_shape=None):
    nb, nf, no = len(blocked), len(full), len(out_defs)

    def body(*refs):
        xs = [r[...].astype(F32) for r in refs[:nb]]
        ps = [r[...] for r in refs[nb:nb + nf]]
        o_refs = refs[nb + nf:nb + nf + no]
        if state_shape is None:
            outs = fn(xs, ps)
        else:
            sprev_ref, s_ref = refs[nb + nf + no:]

            @pl.when(pl.program_id(0) == 0)
            def _():
                s_ref[...] = jnp.zeros_like(s_ref)

            s = s_ref[...]
            sprev_ref[0] = s
            outs, s_new = fn(xs, ps, s)
            s_ref[...] = s_new
        for r, o in zip(o_refs, outs):
            r[...] = o.astype(r.dtype)

    in_specs = [_row_spec(rows, width, cb, n, False) for _, rows, width, cb in blocked]
    in_specs += [_full_spec(a.shape) for a in full]
    out_specs = [_row_spec(rows, width, 0, n, False) for rows, width, _ in out_defs]
    out_shape = [jax.ShapeDtypeStruct((n * rows, width), dt) for rows, width, dt in out_defs]
    scratch = []
    if state_shape is not None:
        out_specs.append(pl.BlockSpec((1,) + state_shape, lambda c: (c, 0, 0)))
        out_shape.append(jax.ShapeDtypeStruct((n,) + state_shape, F32))
        scratch.append(pltpu.VMEM(state_shape, F32))
    return pl.pallas_call(body, name=name, grid=(n,), in_specs=in_specs, out_specs=out_specs, out_shape=out_shape,
                          scratch_shapes=scratch, compiler_params=_cparams(("arbitrary",)))(
        *[a for a, _, _, _ in blocked], *full)


def _chain_bwd(name, fn, n, blocked, full, douts, sprev=None, dx_dtypes=None):
    nb, nf, nd = len(blocked), len(full), len(douts)
    has_state = sprev is not None
    dx_dtypes = dx_dtypes or [F32] * nb

    def body(*refs):
        pos = 0
        b_refs = refs[pos:pos + nb]; pos += nb
        f_refs = refs[pos:pos + nf]; pos += nf
        d_refs = refs[pos:pos + nd]; pos += nd
        if has_state:
            sprev_ref = refs[pos]; pos += 1
        dx_refs = refs[pos:pos + nb]; pos += nb
        dp_refs = refs[pos:pos + nf]; pos += nf
        if has_state:
            ds_ref = refs[pos]

        @pl.when(pl.program_id(0) == 0)
        def _():
            for r in dp_refs:
                r[...] = jnp.zeros_like(r)
            if has_state:
                ds_ref[...] = jnp.zeros_like(ds_ref)

        xs = [r[...].astype(F32) for r in b_refs]
        ps = [r[...] for r in f_refs]
        dys = tuple(r[...].astype(F32) for r in d_refs)
        if has_state:
            _, vjp = jax.vjp(fn, xs, ps, sprev_ref[0])
            dxs, dps, ds = vjp((dys, ds_ref[...]))
            ds_ref[...] = ds
        else:
            _, vjp = jax.vjp(fn, xs, ps)
            dxs, dps = vjp(dys)
        for r, d in zip(dx_refs, dxs):
            r[...] = d.astype(r.dtype)
        for r, d in zip(dp_refs, dps):
            r[...] += d

    in_specs = [_row_spec(rows, width, cb, n, True) for _, rows, width, cb in blocked]
    in_specs += [_full_spec(a.shape) for a in full]
    in_specs += [_row_spec(rows, width, 0, n, True) for _, rows, width in douts]
    args = [a for a, _, _, _ in blocked] + list(full) + [a for a, _, _ in douts]
    scratch = []
    if has_state:
        st_shape = sprev.shape[1:]
        in_specs.append(pl.BlockSpec((1,) + st_shape, lambda c: (n - 1 - c, 0, 0)))
        args.append(sprev)
        scratch.append(pltpu.VMEM(st_shape, F32))
    out_specs = [_row_spec(rows, width, 0, n, True) for _, rows, width, _ in blocked]
    out_specs += [_full_spec(a.shape) for a in full]
    out_shape = [jax.ShapeDtypeStruct((n * rows, width), dt) for (_, rows, width, _), dt in zip(blocked, dx_dtypes)]
    out_shape += [jax.ShapeDtypeStruct(a.shape, F32) for a in full]
    res = pl.pallas_call(body, name=name, grid=(n,), in_specs=in_specs, out_specs=out_specs, out_shape=out_shape,
                         scratch_shapes=scratch, compiler_params=_cparams(("arbitrary",)))(*args)
    return res[:nb], res[nb:]


def _tile(n, target, unit):
    if n <= target:
        return n
    best = None
    for t in range(unit, target + 1, unit):
        if n % t == 0:
            best = t
    assert best is not None, (n, target, unit)
    return best


def _mm(name, a, b, dims="nn", out_dtype=F32, tm=2048, tn=512, tk=2048, after=None):
    if dims == "nn":
        (M, K), (_, N) = a.shape, b.shape
    elif dims == "nt":
        (M, K), (N, _) = a.shape, b.shape
    else:
        (K, M), (_, N) = a.shape, b.shape
    tm, tn, tk = _tile(M, tm, LANES), _tile(N, tn, LANES), _tile(K, tk, LANES)
    nk = K // tk
    extra = [] if after is None else [after]

    def body(*refs):
        a_ref, b_ref = refs[:2]
        o_ref, acc_ref = refs[-2:]
        part = _dot(a_ref[...], b_ref[...], dims)
        if nk == 1:
            o_ref[...] = part.astype(o_ref.dtype)
            return
        k = pl.program_id(2)

        @pl.when(k == 0)
        def _():
            acc_ref[...] = part

        @pl.when((k > 0) & (k < nk - 1))
        def _():
            acc_ref[...] += part

        @pl.when(k == nk - 1)
        def _():
            o_ref[...] = (acc_ref[...] + part).astype(o_ref.dtype)

    if dims == "tn":
        a_spec = pl.BlockSpec((tk, tm), lambda j, i, k: (k, i))
    else:
        a_spec = pl.BlockSpec((tm, tk), lambda j, i, k: (i, k))
    if dims == "nt":
        b_spec = pl.BlockSpec((tn, tk), lambda j, i, k: (j, k))
    else:
        b_spec = pl.BlockSpec((tk, tn), lambda j, i, k: (k, j))
    return pl.pallas_call(
        body, name=name, grid=(N // tn, M // tm, nk), in_specs=[a_spec, b_spec] + [ANY] * len(extra),
        out_specs=pl.BlockSpec((tm, tn), lambda j, i, k: (i, j)), out_shape=jax.ShapeDtypeStruct((M, N), out_dtype),
        scratch_shapes=[pltpu.VMEM((tm, tn) if nk > 1 else (8, LANES), F32)],
        compiler_params=_cparams(("parallel", "parallel", "arbitrary")))(a, b, *extra)


CONV_CB = 256


def _shift_down(x, k):
    if k == 0:
        return x
    return jnp.where(_iota2(x.shape, 0) >= k, pltpu.roll(x, k, 0), 0.0)


def _shift_up(x, k):
    if k == 0:
        return x
    t = x.shape[0]
    return jnp.where(_iota2(x.shape, 0) < t - k, pltpu.roll(x, t - k, 0), 0.0)


def _conv_pre(x, w, b):
    kk = w.shape[0]
    pre = x * w[kk - 1:kk, :]
    for k in range(kk - 1):
        pre = pre + _shift_down(x, kk - 1 - k) * w[k:k + 1, :]
    return pre if b is None else pre + b


EDGE = 16


def _conv_pre_rot(x, w, b):
    kk = w.shape[0]
    pre = x * w[kk - 1:kk, :]
    for k in range(kk - 1):
        pre = pre + pltpu.roll(x, kk - 1 - k, 0) * w[k:k + 1, :]
    return pre if b is None else pre + b


def _conv_t_local(d, w):
    kk = w.shape[0]
    out = d * w[kk - 1:kk, :]
    for k in range(kk - 1):
        out = out + _shift_up(d, kk - 1 - k) * w[k:k + 1, :]
    return out


def _col_sum(a):
    return jnp.sum(a, axis=0, keepdims=True)


def _conv_bwd_rot(x_ref, w, dpre, dpre_head, dx_ref, dw_ref, db_ref):
    T = dpre.shape[0]
    kk = w.shape[0]
    x = x_ref[...]
    x_head, x_tail = x_ref[0:EDGE, :], x_ref[T - EDGE:T, :]
    wrong_head = dpre[0:EDGE]
    dx = dpre * w[kk - 1:kk, :]
    for k in range(kk - 1):
        dx = dx + pltpu.roll(dpre, T - (kk - 1 - k), 0) * w[k:k + 1, :]
    dx_ref[...] = dx.astype(dx_ref.dtype)
    top = jnp.concatenate([dpre_head, dpre[EDGE:2 * EDGE]], axis=0)
    dx_ref[0:EDGE, :] = _conv_t_local(top, w)[0:EDGE].astype(dx_ref.dtype)
    dx_ref[T - EDGE:T, :] = _conv_t_local(dpre[T - EDGE:T], w).astype(dx_ref.dtype)
    ends = jnp.concatenate([x_tail, x_head], axis=0)
    dw_ref[kk - 1:kk, :] = _col_sum(dpre * x) + _col_sum((dpre_head - wrong_head) * x_head)
    for k in range(kk - 1):
        s = kk - 1 - k
        rotated_head = pltpu.roll(ends, s, 0)[EDGE:2 * EDGE]
        dw_ref[k:k + 1, :] = (_col_sum(dpre * pltpu.roll(x, s, 0)) - _col_sum(wrong_head * rotated_head)
                              + _col_sum(dpre_head * _shift_down(x_head, s)))
    if db_ref is not None:
        db_ref[...] = _col_sum(dpre) + _col_sum(dpre_head - wrong_head)


def _dsilu(pre):
    sg = jax.nn.sigmoid(pre)
    return sg * (1.0 + pre * (1.0 - sg))


def _conv_silu_fwd(name, src, col0, w, b):
    T = src.shape[0]
    kk, C = w.shape
    cb = CONV_CB
    off = col0 // cb

    def body(*refs):
        x_ref, w_ref, o_ref = refs[0], refs[1], refs[-1]
        b_val = refs[2][...] if b is not None else None
        o_ref[...] = jax.nn.silu(_conv_pre_rot(x_ref[...], w_ref[...], b_val))
        o_ref[0:EDGE, :] = jax.nn.silu(_conv_pre(x_ref[0:EDGE, :], w_ref[...], b_val))

    in_specs = [pl.BlockSpec((T, cb), lambda j: (0, off + j)), pl.BlockSpec((kk, cb), lambda j: (0, j))]
    args = [src, w]
    if b is not None:
        in_specs.append(pl.BlockSpec((1, cb), lambda j: (0, j)))
        args.append(b)
    return pl.pallas_call(body, name=name, grid=(C // cb,), in_specs=in_specs,
                          out_specs=pl.BlockSpec((T, cb), lambda j: (0, j)), out_shape=jax.ShapeDtypeStruct((T, C), F32),
                          compiler_params=_cparams(("parallel",)))(*args)


def _conv_silu_bwd(name, src, col0, w, b, dy, dx_dtype):
    T = src.shape[0]
    kk, C = w.shape
    cb = CONV_CB
    off = col0 // cb
    has_b = b is not None

    def body(*refs):
        x_ref, w_ref = refs[:2]
        pos = 2
        b_val = None
        if has_b:
            b_val = refs[pos][...]; pos += 1
        dy_ref = refs[pos]; pos += 1
        dx_ref, dw_ref = refs[pos], refs[pos + 1]
        db_ref = refs[pos + 2] if has_b else None
        wv = w_ref[...]
        dpre = dy_ref[...] * _dsilu(_conv_pre_rot(x_ref[...], wv, b_val))
        dpre_head = dy_ref[0:EDGE, :] * _dsilu(_conv_pre(x_ref[0:EDGE, :], wv, b_val))
        _conv_bwd_rot(x_ref, wv, dpre, dpre_head, dx_ref, dw_ref, db_ref)

    in_specs = [pl.BlockSpec((T, cb), lambda j: (0, off + j)), pl.BlockSpec((kk, cb), lambda j: (0, j))]
    args = [src, w]
    if has_b:
        in_specs.append(pl.BlockSpec((1, cb), lambda j: (0, j)))
        args.append(b)
    in_specs.append(pl.BlockSpec((T, cb), lambda j: (0, j)))
    args.append(dy)
    out_specs = [pl.BlockSpec((T, cb), lambda j: (0, j)), pl.BlockSpec((kk, cb), lambda j: (0, j))]
    out_shape = [jax.ShapeDtypeStruct((T, C), dx_dtype), jax.ShapeDtypeStruct((kk, C), F32)]
    if has_b:
        out_specs.append(pl.BlockSpec((1, cb), lambda j: (0, j)))
        out_shape.append(jax.ShapeDtypeStruct((1, C), F32))
    return pl.pallas_call(body, name=name, grid=(C // cb,), in_specs=in_specs, out_specs=out_specs, out_shape=out_shape,
                          compiler_params=_cparams(("parallel",)))(*args)


def _ffn_glu_fwd(name, up, w, b, out_dtype=F32):
    T = up.shape[0]
    kk = w.shape[0]
    cb = CONV_CB
    width = up.shape[1] // 2
    nblk = width // cb

    def body(g_ref, u_ref, wg_ref, wu_ref, bg_ref, bu_ref, o_ref):
        g = _conv_pre_rot(g_ref[...], wg_ref[...], bg_ref[...])
        u = _conv_pre_rot(u_ref[...], wu_ref[...], bu_ref[...])
        o_ref[...] = (jax.nn.silu(g) * u).astype(o_ref.dtype)
        g = _conv_pre(g_ref[0:EDGE, :], wg_ref[...], bg_ref[...])
        u = _conv_pre(u_ref[0:EDGE, :], wu_ref[...], bu_ref[...])
        o_ref[0:EDGE, :] = (jax.nn.silu(g) * u).astype(o_ref.dtype)

    lo, hi = (lambda j: (0, j)), (lambda j: (0, nblk + j))
    in_specs = [pl.BlockSpec((T, cb), lo), pl.BlockSpec((T, cb), hi), pl.BlockSpec((kk, cb), lo), pl.BlockSpec((kk, cb), hi),
                pl.BlockSpec((1, cb), lo), pl.BlockSpec((1, cb), hi)]
    return pl.pallas_call(body, name=name, grid=(nblk,), in_specs=in_specs, out_specs=pl.BlockSpec((T, cb), lo),
                          out_shape=jax.ShapeDtypeStruct((T, width), out_dtype),
                          compiler_params=_cparams(("parallel",)))(up, up, w, w, b, b)


def _ffn_glu_bwd(name, up, w, b, dact, dx_dtype):
    T = up.shape[0]
    kk = w.shape[0]
    cb = CONV_CB
    width = up.shape[1] // 2
    nblk = width // cb

    def body(g_ref, u_ref, wg_ref, wu_ref, bg_ref, bu_ref, d_ref, dg_ref, du_ref, dwg_ref, dwu_ref, dbg_ref, dbu_ref):
        wg, wu = wg_ref[...], wu_ref[...]
        g = _conv_pre_rot(g_ref[...], wg, bg_ref[...])
        u = _conv_pre_rot(u_ref[...], wu, bu_ref[...])
        d = d_ref[...].astype(F32)
        g_head = _conv_pre(g_ref[0:EDGE, :], wg, bg_ref[...])
        u_head = _conv_pre(u_ref[0:EDGE, :], wu, bu_ref[...])
        d_head = d_ref[0:EDGE, :].astype(F32)
        _conv_bwd_rot(g_ref, wg, d * u * _dsilu(g), d_head * u_head * _dsilu(g_head), dg_ref, dwg_ref, dbg_ref)
        _conv_bwd_rot(u_ref, wu, d * jax.nn.silu(g), d_head * jax.nn.silu(g_head), du_ref, dwu_ref, dbu_ref)

    lo, hi = (lambda j: (0, j)), (lambda j: (0, nblk + j))
    in_specs = [pl.BlockSpec((T, cb), lo), pl.BlockSpec((T, cb), hi), pl.BlockSpec((kk, cb), lo), pl.BlockSpec((kk, cb), hi),
                pl.BlockSpec((1, cb), lo), pl.BlockSpec((1, cb), hi), pl.BlockSpec((T, cb), lo)]
    out_specs = [pl.BlockSpec((T, cb), lo)] * 2 + [pl.BlockSpec((kk, cb), lo)] * 2 + [pl.BlockSpec((1, cb), lo)] * 2
    out_shape = ([jax.ShapeDtypeStruct((T, width), dx_dtype)] * 2 + [jax.ShapeDtypeStruct((kk, width), F32)] * 2
                 + [jax.ShapeDtypeStruct((1, width), F32)] * 2)
    return pl.pallas_call(body, name=name, grid=(nblk,), in_specs=in_specs, out_specs=out_specs, out_shape=out_shape,
                          compiler_params=_cparams(("parallel",)))(up, up, w, w, b, b, dact)


def _loss_head(y, target):
    T, D = y.shape
    tb = _tile(T, 256, 8)

    def body(y_ref, t_ref, dy_ref, l_ref):
        @pl.when(pl.program_id(0) == 0)
        def _():
            l_ref[...] = jnp.zeros_like(l_ref)

        err = y_ref[...] - t_ref[...]
        dy_ref[...] = err * (1.0 / D)
        l_ref[...] += jnp.sum(err * err, axis=0, keepdims=True) * (0.5 / D)

    spec = pl.BlockSpec((tb, D), lambda i: (i, 0))
    return pl.pallas_call(body, name="loss_head", grid=(T // tb,), in_specs=[spec, spec],
                          out_specs=[spec, pl.BlockSpec((1, D), lambda i: (0, 0))],
                          out_shape=[jax.ShapeDtypeStruct((T, D), F32), jax.ShapeDtypeStruct((1, D), F32)],
                          compiler_params=_cparams(("arbitrary",)))(y, target)


def _adamw_math(w, g, m, v):
    m = ADAM_B1 * m + (1.0 - ADAM_B1) * g
    v = ADAM_B2 * v + (1.0 - ADAM_B2) * jnp.square(g)
    m_hat = m / (1.0 - ADAM_B1 ** ADAM_STEP)
    v_hat = v / (1.0 - ADAM_B2 ** ADAM_STEP)
    return -ADAM_LR * (m_hat / (jnp.sqrt(v_hat) + ADAM_EPS) + ADAM_WD * w), m, v


def _adamw(name, w, g, m, v, after=None):
    A, R, C = w.shape
    if C % LANES == 0:
        rb, cb = _slab(R, C)
    else:
        rb, cb = _tile(R, max(8, SLAB_BYTES // 2 // (C * 4) // 8 * 8), 8), C
    extra = [] if after is None else [after]

    def body(w_ref, g_ref, m_ref, v_ref, *rest):
        d_ref, mo_ref, vo_ref = rest[-3:]
        d, mn, vn = _adamw_math(w_ref[...], g_ref[...], m_ref[...], v_ref[...])
        d_ref[...] = d
        mo_ref[...] = mn
        vo_ref[...] = vn

    spec = pl.BlockSpec((1, rb, cb), lambda a, r, q: (a, r, q))
    return pl.pallas_call(body, name=name, grid=(A, R // rb, C // cb), in_specs=[spec] * 4 + [ANY] * len(extra),
                          out_specs=[spec] * 3, out_shape=[jax.ShapeDtypeStruct(w.shape, F32)] * 3,
                          compiler_params=_cparams(("parallel", "parallel", "parallel")))(w, g, m, v, *extra)


def _adamw_small(parts, w, m, v):
    def body(p_ref, w_ref, m_ref, v_ref, g_ref, d_ref, mo_ref, vo_ref):
        g = p_ref[0]
        for i in range(1, N_DEV):
            g = g + p_ref[i]
        d, mn, vn = _adamw_math(w_ref[...], g, m_ref[...], v_ref[...])
        g_ref[...] = g
        d_ref[...] = d
        mo_ref[...] = mn
        vo_ref[...] = vn

    return pl.pallas_call(body, name="adamw_small", out_shape=[jax.ShapeDtypeStruct(w.shape, F32)] * 4,
                          compiler_params=_cparams())(parts, w, m, v)


def _add_blocks(name, a, b, out_dtype=F32):
    n, R, W = a.shape
    rb = _tile(R, 512, 8)

    def body(a_ref, b_ref, o_ref):
        o_ref[...] = (a_ref[...].astype(F32) + b_ref[...].astype(F32)).astype(o_ref.dtype)

    spec = pl.BlockSpec((1, rb, W), lambda i, r: (i, r, 0))
    return pl.pallas_call(body, name=name, grid=(n, R // rb), in_specs=[spec, spec], out_specs=spec,
                          out_shape=jax.ShapeDtypeStruct(a.shape, out_dtype),
                          compiler_params=_cparams(("parallel", "parallel")))(a, b)


SLAB_BYTES = 5 << 19


def _slab(R, W):
    if R % 16 == 0:
        return _tile(R, max(16, SLAB_BYTES // (4 * W) // 16 * 16), 16), W
    assert W % LANES == 0, (R, W)
    return R, _tile(W, max(LANES, SLAB_BYTES // (4 * R) // LANES * LANES), LANES)


def _pair_add(name, g, other, c, chip):
    _, R, W = g.shape
    rb, cb = _slab(R, W)

    def body(s_ref, a_ref, b_ref, send_ref, own_ref):
        s = a_ref[0] + b_ref[0]
        send_ref[0] = s.astype(send_ref.dtype)

        @pl.when(pl.program_id(2) == s_ref[1])
        def _():
            own_ref[...] = s

    grid_spec = pltpu.PrefetchScalarGridSpec(
        num_scalar_prefetch=1, grid=(R // rb, W // cb, 4),
        in_specs=[pl.BlockSpec((1, rb, cb), lambda r, q, p, s_ref: (2 * p + s_ref[0], r, q)),
                  pl.BlockSpec((1, rb, cb), lambda r, q, p, s_ref: (p, r, q))],
        out_specs=[pl.BlockSpec((1, rb, cb), lambda r, q, p, s_ref: (p, r, q)),
                   pl.BlockSpec((rb, cb), lambda r, q, p, s_ref: (r, q))])
    scalars = jnp.stack([c, chip]).astype(jnp.int32)
    return pl.pallas_call(body, name=name, grid_spec=grid_spec,
                          out_shape=[jax.ShapeDtypeStruct((4, R, W), MXU_DTYPE), jax.ShapeDtypeStruct((R, W), F32)],
                          compiler_params=_cparams(("parallel", "parallel", "arbitrary")))(scalars, g, other)


def _sum4(name, own, parts):
    R, W = own.shape
    rb, cb = _slab(R, W)

    def body(o_ref, p_ref, out_ref):
        out_ref[...] = ((o_ref[...] + p_ref[0].astype(F32)) + p_ref[1].astype(F32)) + p_ref[2].astype(F32)

    return pl.pallas_call(body, name=name, grid=(R // rb, W // cb),
                          in_specs=[pl.BlockSpec((rb, cb), lambda r, q: (r, q)), pl.BlockSpec((3, rb, cb), lambda r, q: (0, r, q))],
                          out_specs=pl.BlockSpec((rb, cb), lambda r, q: (r, q)), out_shape=jax.ShapeDtypeStruct((R, W), F32),
                          compiler_params=_cparams(("parallel", "parallel")))(own, parts)


MESH = pl.DeviceIdType.MESH
ANY = pl.BlockSpec(memory_space=pl.ANY)


def _place():
    return lax.axis_index("x"), lax.axis_index("y"), lax.axis_index("c")


def _other_chips(x, y):
    return [(1 - x, y), (x, 1 - y), (1 - x, 1 - y)]


def _all_gather(name, blocks):
    n = len(blocks)

    def body(*refs):
        x_refs, out_refs = refs[:n], refs[n:2 * n]
        send_sems, recv_sems, local_sems = refs[2 * n:]
        x, y, c = _place()
        me, sibling = (x, y, c), (x, y, 1 - c)
        chips = _other_chips(x, y)

        def slot(a, px, py, pc):
            return out_refs[a].at[4 * px + 2 * py + pc]

        def copy(a, k, blk, to, src=None):
            return pltpu.make_async_remote_copy(src_ref=slot(a, *blk) if src is None else src, dst_ref=slot(a, *blk),
                                                send_sem=send_sems.at[a, k], recv_sem=recv_sems.at[a, k],
                                                device_id=to, device_id_type=MESH)

        mine = [pltpu.make_async_copy(x_refs[a], slot(a, *me), local_sems.at[a]) for a in range(n)]
        for cp in mine:
            cp.start()
        first = []
        for j, chip in enumerate(chips):
            first += [copy(a, 1 + j, me, (*chip, c), src=x_refs[a]) for a in range(n)]
        first += [copy(a, 0, me, sibling, src=x_refs[a]) for a in range(n)]
        for cp in first:
            cp.start()
        passed = []
        for j, chip in enumerate(chips):
            for a in range(n):
                copy(a, 1 + j, (*chip, c), me).wait_recv()
                passed.append(copy(a, 4 + j, (*chip, c), sibling))
                passed[-1].start()
        for a in range(n):
            copy(a, 0, sibling, me).wait_recv()
        for j, chip in enumerate(chips):
            for a in range(n):
                copy(a, 4 + j, (*chip, 1 - c), me).wait_recv()
        for cp in first + passed:
            cp.wait_send()
        for cp in mine:
            cp.wait()

    return pl.pallas_call(body, name=name, in_specs=[ANY] * n, out_specs=[ANY] * n,
                          out_shape=[jax.ShapeDtypeStruct((N_DEV,) + b.shape, b.dtype) for b in blocks],
                          scratch_shapes=[pltpu.SemaphoreType.DMA((n, 7)), pltpu.SemaphoreType.DMA((n, 7)),
                                          pltpu.SemaphoreType.DMA((n,))])(*blocks)


def _routes_to_sibling(x, y, c):
    return [(2 * p + (1 - c), p, (x, y, 1 - c)) for p in range(4)]


def _routes_to_chips(x, y, c):
    return [(2 * px + py, j, (px, py, c)) for j, (px, py) in enumerate(_other_chips(x, y))]


def _routes_block_to_chips(x, y, c):
    me = 4 * x + 2 * y + c
    return [(me, me, (px, py, c)) for px, py in _other_chips(x, y)]


def _routes_blocks_to_sibling(x, y, c):
    return [(4 * px + 2 * py + c, 4 * px + 2 * py + c, (x, y, 1 - c)) for px, py in [(x, y)] + _other_chips(x, y)]


def _route_copies(routes, src_refs, land_refs, send_sems, recv_sems):
    x, y, c = _place()
    copies = []
    for a, (src, land) in enumerate(zip(src_refs, land_refs)):
        plan = routes(x, y, c)
        for k, (s, d, target) in enumerate(plan):
            i = a * len(plan) + k
            copies.append(pltpu.make_async_remote_copy(src_ref=src.at[s], dst_ref=land.at[d], send_sem=send_sems.at[i],
                                                       recv_sem=recv_sems.at[i], device_id=target, device_id_type=MESH))
    return copies


def _exchange(name, routes, n_routes, srcs, land_slots):
    n = len(srcs)

    def body(*refs):
        copies = _route_copies(routes, refs[:n], refs[n:2 * n], refs[2 * n], refs[2 * n + 1])
        for cp in copies:
            cp.start()
        for cp in copies:
            cp.wait_recv()
        for cp in copies:
            cp.wait_send()

    return pl.pallas_call(body, name=name, in_specs=[ANY] * n, out_specs=[ANY] * n,
                          out_shape=[jax.ShapeDtypeStruct((land_slots,) + s.shape[1:], s.dtype) for s in srcs],
                          scratch_shapes=[pltpu.SemaphoreType.DMA((n * n_routes,)), pltpu.SemaphoreType.DMA((n * n_routes,))])(*srcs)


HBM_SPEC = pl.BlockSpec(memory_space=pltpu.HBM)
SEM_SPEC = pl.BlockSpec(memory_space=pltpu.SEMAPHORE)
DATAFLOW = pltpu.SideEffectType.DATAFLOW_SIDE_EFFECTING


def _exchange_start(name, routes, n_routes, srcs, lands, after=None):
    n = len(srcs)
    in_place = lands is None
    bufs = list(srcs) + ([] if in_place else list(lands))
    nb = len(bufs)
    extra = [] if after is None else [after]

    def body(*refs):
        src_refs = refs[:n]
        land_refs = src_refs if in_place else refs[n:nb]
        send_sems, recv_sems = refs[nb + len(extra)], refs[nb + len(extra) + 1]
        token = refs[-1]
        for cp in _route_copies(routes, src_refs, land_refs, send_sems, recv_sems):
            cp.start()
        token[...] = jnp.zeros_like(token)

    sems = [pltpu.SemaphoreType.DMA((n * n_routes,)), pltpu.SemaphoreType.DMA((n * n_routes,))]
    out = pl.pallas_call(
        body, name=name, in_specs=[HBM_SPEC] * nb + [ANY] * len(extra),
        out_shape=sems + [pltpu.HBM(b.shape, b.dtype) for b in bufs] + [jax.ShapeDtypeStruct((8, LANES), F32)],
        out_specs=[SEM_SPEC, SEM_SPEC] + [HBM_SPEC] * nb + [pl.BlockSpec(memory_space=pltpu.VMEM)],
        input_output_aliases={i: 2 + i for i in range(nb)},
        compiler_params=pltpu.CompilerParams(has_side_effects=DATAFLOW))(
        *[pltpu.with_memory_space_constraint(b, pltpu.HBM) for b in bufs], *extra)
    return (out[0], out[1], list(out[2:2 + nb])), out[-1]


def _exchange_wait(name, routes, n_routes, n, started, after):
    send_sems, recv_sems, bufs = started
    nb = len(bufs)
    in_place = nb == n

    def body(*refs):
        src_refs = refs[:n]
        land_refs = src_refs if in_place else refs[n:nb]
        for cp in _route_copies(routes, src_refs, land_refs, refs[nb], refs[nb + 1]):
            cp.wait_send()
            cp.wait_recv()

    out = pl.pallas_call(
        body, name=name, in_specs=[HBM_SPEC] * nb + [SEM_SPEC, SEM_SPEC, ANY],
        out_shape=[pltpu.HBM(b.shape, b.dtype) for b in bufs], out_specs=[HBM_SPEC] * nb,
        input_output_aliases={i: i for i in range(nb)},
        compiler_params=pltpu.CompilerParams(has_side_effects=DATAFLOW))(*bufs, send_sems, recv_sems, after)
    return list(out[:n]) if in_place else (list(out[:n]), list(out[n:]))


def _pair_sums(tag, gs, from_sibling):
    x, y, c = _place()
    return [_pair_add(f"rs_add_{tag}_{i}", g, o, c, 2 * x + y) for i, (g, o) in enumerate(zip(gs, from_sibling))]


def _reduce_scatter(tag, gs):
    sums = _pair_sums(tag, gs, _exchange(f"rs_swap_{tag}", _routes_to_sibling, 4, gs, 4))
    got = _exchange(f"rs_chips_{tag}", _routes_to_chips, 3, [s[0] for s in sums], 3)
    return [_sum4(f"rs_sum_{tag}_{i}", s[1], q) for i, (s, q) in enumerate(zip(sums, got))]


def _reduce_scatter_begin(tag, gs):
    lands = [lax.empty((4,) + g.shape[1:], g.dtype) for g in gs]
    swap, token = _exchange_start(f"rs_swap_{tag}_start", _routes_to_sibling, 4, gs, lands)
    return dict(tag=tag, n=len(gs), swap=swap), token


def _reduce_scatter_middle(state, after):
    tag, n = state["tag"], state["n"]
    gs, from_sibling = _exchange_wait(f"rs_swap_{tag}_wait", _routes_to_sibling, 4, n, state["swap"], after)
    state["sums"] = _pair_sums(tag, gs, from_sibling)
    partials = [s[0] for s in state["sums"]]
    lands = [lax.empty((3,) + p.shape[1:], p.dtype) for p in partials]
    state["chips"], token = _exchange_start(f"rs_chips_{tag}_start", _routes_to_chips, 3, partials, lands)
    return token


def _reduce_scatter_end(state, after):
    tag = state["tag"]
    _, got = _exchange_wait(f"rs_chips_{tag}_wait", _routes_to_chips, 3, state["n"], state["chips"], after)
    return [_sum4(f"rs_sum_{tag}_{i}", s[1], q) for i, (s, q) in enumerate(zip(state["sums"], got))]


def _all_gather_begin(tag, blocks, after):
    dev = 4 * lax.axis_index("x") + 2 * lax.axis_index("y") + lax.axis_index("c")
    zones = [lax.dynamic_update_slice_in_dim(lax.empty((N_DEV,) + b.shape, b.dtype), b[None], dev, axis=0) for b in blocks]
    chips, token = _exchange_start(f"gather_{tag}_chips_start", _routes_block_to_chips, 3, zones, None, after)
    return dict(tag=tag, n=len(blocks), chips=chips), token


def _all_gather_middle(state, after):
    tag, n = state["tag"], state["n"]
    zones = _exchange_wait(f"gather_{tag}_chips_wait", _routes_block_to_chips, 3, n, state["chips"], after)
    state["sibling"], token = _exchange_start(f"gather_{tag}_sibling_start", _routes_blocks_to_sibling, 4, zones, None)
    return token


def _all_gather_end(state, after):
    return _exchange_wait(f"gather_{state['tag']}_sibling_wait", _routes_blocks_to_sibling, 4, state["n"], state["sibling"], after)


PACK_UNIT = 8 * LANES


def _packed_size(shape):
    return -(-math.prod(shape) // PACK_UNIT) * PACK_UNIT


def _pack(arrays, dtype):
    parts = []
    for a in arrays:
        flat = a.reshape(-1).astype(dtype)
        parts.append(jnp.pad(flat, (0, _packed_size(a.shape) - flat.shape[0])))
    return jnp.concatenate(parts).reshape(-1, LANES)


def _unpack(flat, shapes, lead=()):
    flat = flat.reshape(lead + (-1,))
    out, pos = [], 0
    for s in shapes:
        out.append(flat[..., pos:pos + math.prod(s)].reshape(lead + tuple(s)))
        pos += _packed_size(s)
    return out


def _ffn_pad_rows(a):
    n = a.shape[0] // FFN_HALF
    a = jnp.pad(a.reshape(n, FFN_HALF, a.shape[1]), ((0, 0), (0, FFN_HALF_PAD - FFN_HALF), (0, 0)))
    return a.reshape(n * FFN_HALF_PAD, a.shape[2])


def _ffn_unpad_rows(a):
    n = a.shape[0] // FFN_HALF_PAD
    return a.reshape(n, FFN_HALF_PAD, a.shape[1])[:, :FFN_HALF].reshape(n * FFN_HALF, a.shape[1])


def _ffn_pad_cols(a):
    n = a.shape[1] // FFN_HALF
    a = jnp.pad(a.reshape(a.shape[0], n, FFN_HALF), ((0, 0), (0, 0), (0, FFN_HALF_PAD - FFN_HALF)))
    return a.reshape(a.shape[0], n * FFN_HALF_PAD)


def _ffn_unpad_cols(a):
    n = a.shape[1] // FFN_HALF_PAD
    return a.reshape(a.shape[0], n, FFN_HALF_PAD)[:, :, :FFN_HALF].reshape(a.shape[0], n * FFN_HALF)


def _shard_to_send(name, shard):
    if name in ("w_in", "w_br_gdn", "w_br_gla"):
        shard = shard.T
    elif name == "ffn_w_up":
        shard = _ffn_pad_rows(shard.T)
    return shard.astype(MXU_DTYPE)


KEPT_TRANSPOSED = ("w_in", "w_br_gdn", "w_br_gla", "ffn_w_up")


def _whole_from_gathered(name, g):
    if name == "w_in":
        return _in_proj_from_shards(g)
    if name == "ffn_w_down":
        return jnp.pad(g, ((0, 0), (0, FFN_HALF_PAD - FFN_HALF), (0, 0))).reshape(FFN_PAD, g.shape[2])
    return g.reshape(N_DEV * g.shape[1], g.shape[2])


def _slots_from_whole(name, gw):
    if name == "w_in":
        return _in_proj_to_slots(gw)
    return gw.reshape(N_DEV, gw.shape[0] // N_DEV, gw.shape[1])


def _shard_from_slot(name, s):
    if name == "ffn_w_up":
        return _ffn_unpad_rows(s)
    if name == "ffn_w_down":
        return s[:FFN_HALF]
    return s


def _in_proj_pieces():
    starts, pos = {}, 0
    for n, width in IN_SPLITS:
        starts[n] = (pos, width)
        pos += width
    return [(starts[ref][0], off + lane, starts[ref][1]) for _, off, _, pieces in PAD_SEGS for ref, lane in pieces]


def _in_proj_moves():
    cs = IN_DIM // N_DEV
    moves = []
    for src, dst, n in sorted(_in_proj_pieces()):
        at = src
        while at < src + n:
            d = at // cs
            end = min(src + n, (d + 1) * cs)
            moves.append((d, at - d * cs, dst + at - src, end - at))
            at = end
    return moves


RELAYOUT_LANES = 128


def _in_proj_from_shards(g):
    _, cs, D = g.shape

    def body(g_ref, o_ref):
        o_ref[...] = jnp.zeros_like(o_ref)
        for d, i0, r0, n in _in_proj_moves():
            o_ref[r0:r0 + n, :] = g_ref[d, i0:i0 + n, :]

    cb = RELAYOUT_LANES
    return pl.pallas_call(body, name="w_in_rows_in", grid=(D // cb,),
                          in_specs=[pl.BlockSpec((N_DEV, cs, cb), lambda j: (0, 0, j))],
                          out_specs=pl.BlockSpec((IN_PAD, cb), lambda j: (0, j)),
                          out_shape=jax.ShapeDtypeStruct((IN_PAD, D), g.dtype), compiler_params=_cparams(("parallel",)))(g)


def _in_proj_to_slots(gw):
    D = gw.shape[1]
    cs = IN_DIM // N_DEV

    def body(x_ref, o_ref):
        for d, i0, r0, n in _in_proj_moves():
            o_ref[d, i0:i0 + n, :] = x_ref[r0:r0 + n, :]

    cb = RELAYOUT_LANES
    return pl.pallas_call(body, name="w_in_rows_out", grid=(D // cb,),
                          in_specs=[pl.BlockSpec((IN_PAD, cb), lambda j: (0, j))],
                          out_specs=pl.BlockSpec((N_DEV, cs, cb), lambda j: (0, 0, j)),
                          out_shape=jax.ShapeDtypeStruct((N_DEV, cs, D), gw.dtype), compiler_params=_cparams(("parallel",)))(gw)


def _pad_in_proj_rows(w):
    rows, at = [], 0
    for src, dst, n in sorted(_in_proj_pieces(), key=lambda p: p[1]):
        if dst > at:
            rows.append(jnp.zeros((dst - at, w.shape[1]), w.dtype))
        rows.append(w[src:src + n])
        at = dst + n
    rows.append(jnp.zeros((IN_PAD - at, w.shape[1]), w.dtype))
    return jnp.concatenate(rows, axis=0)


def _unpad_in_proj_rows(wp):
    return jnp.concatenate([wp[dst:dst + n] for _, dst, n in sorted(_in_proj_pieces())], axis=0)


def _lane_pad(a, width=LANES):
    return jnp.pad(a, ((0, 0), (0, width - a.shape[1])))


def _seg_blk(h, name, rows):
    off, width = SEG[name]
    return (h, rows, width, off // width)


def _ln_both(xs_, ps_):
    (y,) = _ln_fn(xs_, ps_)
    return (y, y)


def _behind(param, hooks, stage, *seen):
    if hooks is None or stage not in hooks:
        return param
    token = hooks[stage](*seen)
    return param if token is None else param + token[0:1, 0:1]


def _layer_fwd(l, x, x_mx, W, sp, hooks=None):
    T = x.shape[0]
    n64, ngla, ntok = T // SSD_CHUNK, T // GLA_BLOCK, T // 256
    h = _mm(f"in_proj_{l}", x_mx, W["w_in"], "nt")
    xbc = _conv_silu_fwd(f"ssd_conv_{l}", h, SEG["xbc"][0], sp["ssd_conv_w"], sp["ssd_conv_b"])
    gqkv = _conv_silu_fwd(f"gdn_conv_{l}", h, SEG["gqkv"][0], sp["gdn_conv_w"], None)

    ssd_in = [(xbc, SSD_CHUNK, SSD_XBC, 0), _seg_blk(h, "dt", SSD_CHUNK), _seg_blk(h, "z", SSD_CHUNK)]
    ssd_p = [sp["ssd_dt_bias"], sp["ssd_a_log"], sp["ssd_d"], sp["ssd_norm_w"]]
    o_ssd, ssd_states = _chain_fwd(f"ssd_fwd_{l}", _ssd_chunk, n64, ssd_in, ssd_p, [(SSD_CHUNK, SSD_INNER, MXU_DTYPE)],
                                   (SSD_STATE, SSD_INNER))
    o_gdn, gdn_saved = _gdn_forward(str(l), gqkv, h, dict(sp, gdn_a_log=_behind(sp["gdn_a_log"], hooks, "ssd", o_ssd)))
    gla_in = [_seg_blk(h, "lqkv", GLA_BLOCK), _seg_blk(h, "lglr", GLA_BLOCK), _seg_blk(h, "lr", GLA_BLOCK)]
    gla_p = [jnp.pad(sp["gla_gate_w2"], ((0, LANES - GLA_RANK), (0, 0))), sp["gla_gate_b"], sp["gla_norm_w"]]
    o_gla, gla_states = _chain_fwd(f"gla_fwd_{l}", _gla_block, ngla, gla_in, gla_p, [(GLA_BLOCK, GLA_V, MXU_DTYPE)],
                                   (GLA_VAL_DIM, GLA_K))
    ln1_p = [_behind(sp["ln1_g"], hooks, "mixed", o_gdn), sp["ln1_b"]]
    y_ssd = _mm(f"br_ssd_{l}", o_ssd, W["w_br_ssd"])
    y_gdn = _mm(f"br_gdn_{l}", o_gdn, W["w_br_gdn"], "nt")
    y_gla = _mm(f"br_gla_{l}", o_gla, W["w_br_gla"], "nt")
    merge_in = [_seg_blk(h, "gates", 256), (y_ssd, 256, D_MODEL, 0), (y_gdn, 256, D_MODEL, 0), (y_gla, 256, D_MODEL, 0)]
    (mix,) = _chain_fwd(f"merge_{l}", _merge_fn, ntok, merge_in, [], [(256, D_MODEL, MXU_DTYPE)])
    r1 = _mm(f"out_proj_{l}", mix, W["w_out"])
    both = [(256, D_MODEL, F32), (256, D_MODEL, MXU_DTYPE)]
    x1, x1_mx = _chain_fwd(f"ln1_{l}", _ln_both, ntok, [(x, 256, D_MODEL, 0), (r1, 256, D_MODEL, 0)], ln1_p, both)
    up = _mm(f"ffn_up_{l}", x1_mx, W["ffn_w_up"], "nt")
    act = _ffn_glu_fwd(f"ffn_glu_{l}", up, sp["ffn_conv_w_pad"], sp["ffn_conv_b_pad"], MXU_DTYPE)
    ln2_p = [_behind(sp["ln2_g"], hooks, "ffn_act", act), sp["ln2_b"]]
    r2 = _mm(f"ffn_down_{l}", act, W["ffn_w_down"])
    x2, x2_mx = _chain_fwd(f"ln2_{l}", _ln_both, ntok, [(x1, 256, D_MODEL, 0), (r2, 256, D_MODEL, 0)], ln2_p, both)
    saved = dict(x=x, x_mx=x_mx, h=h, xbc=xbc, gqkv=gqkv, ssd_in=ssd_in, ssd_p=ssd_p, ssd_states=ssd_states,
                 gdn=gdn_saved, gla_in=gla_in, gla_p=gla_p, gla_states=gla_states, o_ssd=o_ssd,
                 o_gdn=o_gdn, o_gla=o_gla, merge_in=merge_in, mix=mix, r1=r1, ln1_p=ln1_p, x1=x1, x1_mx=x1_mx, up=up, act=act,
                 r2=r2, ln2_p=ln2_p)
    return x2, x2_mx, saved


def _layer_bwd(l, dx2, W, sp, sv, hooks=None):
    T = dx2.shape[0]
    n64, ngla, ntok = T // SSD_CHUNK, T // GLA_BLOCK, T // 256
    bf = MXU_DTYPE
    gw, gs = {}, {}
    ln2_p = [_behind(sv["ln2_p"][0], hooks, "start"), sv["ln2_p"][1]]
    (dx1_a, dr2), (gs["ln2_g"], gs["ln2_b"]) = _chain_bwd(
        f"ln2_bwd_{l}", _ln_fn, ntok, [(sv["x1"], 256, D_MODEL, 0), (sv["r2"], 256, D_MODEL, 0)], ln2_p,
        [(dx2, 256, D_MODEL)], dx_dtypes=[F32, bf])
    gw["ffn_w_down"] = _mm(f"ffn_down_dw_{l}", sv["act"], dr2, "tn")
    dact = _mm(f"ffn_down_dx_{l}", dr2, W["ffn_w_down"], "nt")
    dg, du, dwg, dwu, dbg, dbu = _ffn_glu_bwd(f"ffn_glu_bwd_{l}", sv["up"], sp["ffn_conv_w_pad"], sp["ffn_conv_b_pad"], dact, bf)
    gs["ffn_conv_w"] = _ffn_unpad_cols(jnp.concatenate([dwg, dwu], axis=1))
    gs["ffn_conv_b"] = _ffn_unpad_cols(jnp.concatenate([dbg, dbu], axis=1))
    dup = jnp.concatenate([dg, du], axis=1)
    gw["ffn_w_up"] = _mm(f"ffn_up_dw_{l}", dup, sv["x1_mx"], "tn", tn=1024)
    dx1_b = _mm(f"ffn_up_dx_{l}", dup, W["ffn_w_up"], "nn", tn=1024, tk=1024)
    ln1_p = [_behind(sv["ln1_p"][0], hooks, "ffn", dx1_b), sv["ln1_p"][1]]
    (dx_a, dr1), (gs["ln1_g"], gs["ln1_b"]) = _chain_bwd(
        f"ln1_bwd_{l}", _ln_sum_fn, ntok, [(sv["x"], 256, D_MODEL, 0), (sv["r1"], 256, D_MODEL, 0)], ln1_p,
        [(dx1_a, 256, D_MODEL), (dx1_b, 256, D_MODEL)], dx_dtypes=[F32, bf])
    gw["w_out"] = _mm(f"out_proj_dw_{l}", sv["mix"], dr1, "tn")
    dmix = _mm(f"out_proj_dx_{l}", dr1, W["w_out"], "nt")
    (dgates, dy_ssd, dy_gdn, dy_gla), _ = _chain_bwd(f"merge_bwd_{l}", _merge_fn, ntok, sv["merge_in"], [],
                                                     [(dmix, 256, D_MODEL)], dx_dtypes=[bf, bf, bf, bf])
    gw["w_br_ssd"] = _mm(f"br_ssd_dw_{l}", sv["o_ssd"], dy_ssd, "tn")
    gw["w_br_gdn"] = _mm(f"br_gdn_dw_{l}", dy_gdn, sv["o_gdn"], "tn")
    gw["w_br_gla"] = _mm(f"br_gla_dw_{l}", dy_gla, sv["o_gla"], "tn")
    do_ssd = _mm(f"br_ssd_dx_{l}", dy_ssd, W["w_br_ssd"], "nt")
    do_gdn = _mm(f"br_gdn_dx_{l}", dy_gdn, W["w_br_gdn"], "nn")
    do_gla = _mm(f"br_gla_dx_{l}", dy_gla, W["w_br_gla"], "nn")

    ssd_p = [_behind(sv["ssd_p"][0], hooks, "branches", do_gla, gw)] + list(sv["ssd_p"][1:])
    (dxbc, ddt, dz), dps = _chain_bwd(f"ssd_bwd_{l}", _ssd_chunk, n64, sv["ssd_in"], ssd_p,
                                      [(do_ssd, SSD_CHUNK, SSD_INNER)], sprev=sv["ssd_states"], dx_dtypes=[F32, bf, bf])
    gs["ssd_dt_bias"], gs["ssd_a_log"], gs["ssd_d"], gs["ssd_norm_w"] = dps
    gdn_sv = dict(sv["gdn"], scan_p=[_behind(sv["gdn"]["scan_p"][0], hooks, "ssd", dz)])
    dgqkv, dgab, dgg, gs["gdn_a_log"], gs["gdn_dt_bias"], gs["gdn_norm_w"] = _gdn_backward(str(l), do_gdn, gdn_sv, bf)
    (dlqkv, dlglr, dlr), dps = _chain_bwd(f"gla_bwd_{l}", _gla_block, ngla, sv["gla_in"], sv["gla_p"],
                                          [(do_gla, GLA_BLOCK, GLA_V)], sprev=sv["gla_states"], dx_dtypes=[bf, bf, bf])
    gs["gla_gate_w2"], gs["gla_gate_b"], gs["gla_norm_w"] = dps[0][:GLA_RANK], dps[1], dps[2]
    dxbc_pre, gs["ssd_conv_w"], gs["ssd_conv_b"] = _conv_silu_bwd(
        f"ssd_conv_bwd_{l}", sv["h"], SEG["xbc"][0], sp["ssd_conv_w"], sp["ssd_conv_b"], dxbc, bf)
    dgqkv_pre, gs["gdn_conv_w"] = _conv_silu_bwd(f"gdn_conv_bwd_{l}", sv["h"], SEG["gqkv"][0], sp["gdn_conv_w"], None, dgqkv, bf)
    pieces = dict(gates=dgates, xbc=dxbc_pre, gqkv=dgqkv_pre, z=dz, lqkv=dlqkv, gg=dgg, lr=dlr, dt=ddt, gab=dgab, lglr=dlglr)
    cols = [pieces[name] for name, _, _, _ in PAD_SEGS]
    cols.append(jnp.zeros((T, IN_PAD - PAD_SEGS[-1][1] - PAD_SEGS[-1][2]), bf))
    dh = jnp.concatenate(cols, axis=1)
    gw["w_in"] = _mm(f"in_proj_dw_{l}", dh, sv["x_mx"], "tn", tn=1024)
    behind = hooks["w_in_grad"](gw) if hooks is not None and "w_in_grad" in hooks else None
    dx_b = _mm(f"in_proj_dx_{l}", dh, W["w_in"], "nn", tm=1024, tn=1024, tk=IN_PAD // 4, after=behind)
    dx = _add_blocks(f"dx_add_{l}", dx_a[None], dx_b[None])[0]
    return dx, gw, gs


def _ln_sum_fn(xs_, ps_):
    (y,) = _ln_fn(xs_, ps_)
    return (y, y)


def _small_2d(name, a):
    return a.reshape(1, -1) if a.ndim == 1 else a


def kernel(x, w_in, ssd_conv_w, ssd_conv_b, ssd_dt_bias, ssd_a_log, ssd_d, ssd_norm_w, gdn_conv_w, gdn_a_log, gdn_dt_bias, gdn_norm_w, gla_gate_w2, gla_gate_b, gla_norm_w, w_br_ssd, w_br_gdn, w_br_gla, w_out, ln1_g, ln1_b, ffn_w_up, ffn_conv_w, ffn_conv_b, ffn_w_down, ln2_g, ln2_b, loss_target, m_w_in, m_ssd_conv_w, m_ssd_conv_b, m_ssd_dt_bias, m_ssd_a_log, m_ssd_d, m_ssd_norm_w, m_gdn_conv_w, m_gdn_a_log, m_gdn_dt_bias, m_gdn_norm_w, m_gla_gate_w2, m_gla_gate_b, m_gla_norm_w, m_w_br_ssd, m_w_br_gdn, m_w_br_gla, m_w_out, m_ln1_g, m_ln1_b, m_ffn_w_up, m_ffn_conv_w, m_ffn_conv_b, m_ffn_w_down, m_ln2_g, m_ln2_b, v_w_in, v_ssd_conv_w, v_ssd_conv_b, v_ssd_dt_bias, v_ssd_a_log, v_ssd_d, v_ssd_norm_w, v_gdn_conv_w, v_gdn_a_log, v_gdn_dt_bias, v_gdn_norm_w, v_gla_gate_w2, v_gla_gate_b, v_gla_norm_w, v_w_br_ssd, v_w_br_gdn, v_w_br_gla, v_w_out, v_ln1_g, v_ln1_b, v_ffn_w_up, v_ffn_conv_w, v_ffn_conv_b, v_ffn_w_down, v_ln2_g, v_ln2_b):
    args = locals()
    w = {n: args[n] for n in WEIGHTS}
    m = {n: args["m_" + n] for n in WEIGHTS}
    v = {n: args["v_" + n] for n in WEIGHTS}
    dev = 4 * lax.axis_index("x") + 2 * lax.axis_index("y") + lax.axis_index("c")
    xl = x[0]
    tgt = loss_target[0]

    late = BIG[1:]

    def send(names, l):
        return [_shard_to_send(n, w[n][l]) for n in names]

    def whole_weights(names, got):
        return {n: _whole_from_gathered(n, g) for n, g in zip(names, got)}

    got0 = _all_gather("gather_first", send(BIG[:1], 0) + [w[n] for n in SMALL_SHARDED])
    gather0, token0 = _all_gather_begin("w_0", send(late, 0), got0[0])
    W = [whole_weights(BIG[:1], got0[:1]), None]
    whole = dict(w)
    for n, s in zip(SMALL_SHARDED, got0[1:]):
        whole[n] = jnp.transpose(s, (1, 2, 0, 3)).reshape(s.shape[1], s.shape[2], N_DEV * s.shape[3])
    SP = [{n: _small_2d(n, whole[n][l]) for n in SMALL} for l in range(DEPTH)]
    for sp in SP:
        sp["ffn_conv_w_pad"] = _ffn_pad_cols(sp["ffn_conv_w"])
        sp["ffn_conv_b_pad"] = _ffn_pad_cols(sp["ffn_conv_b"])

    held = {}

    def late_weights_cross(o_ssd):
        token = _all_gather_middle(gather0, o_ssd)
        held["gather1"], token1 = _all_gather_begin("w_1", send(BIG, 1), o_ssd)
        return token + token1

    def late_weights_arrive(mixed):
        W[0].update(whole_weights(late, _all_gather_end(gather0, mixed)))

    fwd_hooks = {"ssd": late_weights_cross, "mixed": late_weights_arrive,
                 "ffn_act": lambda act: _all_gather_middle(held["gather1"], act)}
    saved = [None] * DEPTH
    act, act_mx, saved[0] = _layer_fwd(0, xl, (xl + token0[0, 0]).astype(MXU_DTYPE), W[0], SP[0], hooks=fwd_hooks)
    W[1] = whole_weights(BIG, _all_gather_end(held["gather1"], act))
    act, act_mx, saved[1] = _layer_fwd(1, act, act_mx, W[1], SP[1])
    dy, loss_parts = _loss_head(act, tgt)
    loss = lax.psum(jnp.sum(loss_parts), ("x", "y", "c"))

    def slots_of(names, gw):
        return [_slots_from_whole(n, gw[n]) for n in names]

    grads = {}
    GS = [None] * DEPTH
    dy, gw, GS[1] = _layer_bwd(1, dy, W[1], SP[1], saved[1])
    reduce1, reduce1_token = _reduce_scatter_begin("1", slots_of(BIG, gw))

    def late_grads_leave(seen, gw0):
        held["reduce0"], token = _reduce_scatter_begin("0", slots_of(late, gw0))
        return token

    def w_in_grad_leaves(gw0):
        held["reduce_first"], token = _reduce_scatter_begin("first", slots_of(BIG[:1], gw0))
        return token

    bwd_hooks = {"start": lambda: reduce1_token, "ffn": lambda seen: _reduce_scatter_middle(reduce1, seen),
                 "branches": late_grads_leave, "ssd": lambda seen: _reduce_scatter_middle(held["reduce0"], seen),
                 "w_in_grad": w_in_grad_leaves}
    dy, gw, GS[0] = _layer_bwd(0, dy, W[0], SP[0], saved[0], hooks=bwd_hooks)
    first_token = _reduce_scatter_middle(held["reduce_first"], dy)
    red1 = _reduce_scatter_end(reduce1, dy)
    red0_late = _reduce_scatter_end(held["reduce0"], dy)
    grad_x = dy[None]
    kept_t = KEPT_TRANSPOSED
    grads_k = {n: jnp.stack([_shard_from_slot(n, red0_late[i]), _shard_from_slot(n, red1[i + 1])]) for i, n in enumerate(late)}

    small_shapes = [whole[n].shape for n in SMALL]
    gs_flat = _pack([jnp.stack([GS[l][n].reshape(whole[n].shape[1:]) for l in range(DEPTH)]) for n in SMALL], F32)
    (gs_all,) = _all_gather("gather_small_grads", [gs_flat])

    def mine(n, a):
        if n in SMALL_SHARDED:
            cs = a.shape[-1] // N_DEV
            return lax.dynamic_slice_in_dim(a, dev * cs, cs, axis=a.ndim - 1)
        return a

    m_whole, v_whole = {}, {}
    for n in SMALL:
        reps = (1, 1, N_DEV) if n in SMALL_SHARDED else (1,) * m[n].ndim
        m_whole[n], v_whole[n] = jnp.tile(m[n], reps), jnp.tile(v[n], reps)
    outs = _adamw_small(gs_all, _pack([whole[n] for n in SMALL], F32) + first_token[0:1, 0:1], _pack([m_whole[n] for n in SMALL], F32),
                        _pack([v_whole[n] for n in SMALL], F32))
    g_s, d_s, m_s, v_s = [_unpack(o, small_shapes) for o in outs]
    delta, new_m, new_v = {}, {}, {}
    for i, n in enumerate(SMALL):
        grads[n], delta[n], new_m[n], new_v[n] = mine(n, g_s[i]), mine(n, d_s[i]), mine(n, m_s[i]), mine(n, v_s[i])
    for n in late + BIG[:1]:
        if n == "w_in":
            done = sum(new_v[k].reshape(-1)[0:1] for k in late + SMALL[:1])
            (first0,) = _reduce_scatter_end(held["reduce_first"], done)
            grads_k[n] = jnp.stack([first0, red1[0]])
        view = (lambda a: jnp.transpose(a, (0, 2, 1))) if n in kept_t else (lambda a: a)
        outs = _adamw(f"adamw_{n}", view(w[n]), grads_k[n], view(m[n]), view(v[n]), after=None if n == "w_in" else first_token)
        grads[n], delta[n], new_m[n], new_v[n] = view(grads_k[n]), view(outs[0]), view(outs[1]), view(outs[2])

    return (loss, grad_x, *[grads[n] for n in WEIGHTS], *[delta[n] for n in WEIGHTS], *[new_m[n] for n in WEIGHTS],
            *[new_v[n] for n in WEIGHTS])
```

```python
import functools
import math

import jax
import jax.numpy as jnp
from jax import lax
from jax.experimental import pallas as pl
from jax.experimental.pallas import tpu as pltpu

F32 = jnp.float32
MXU_DTYPE = jnp.bfloat16
HI = lax.Precision.HIGHEST

N_DEV = 8
D_MODEL = 1024
DEPTH = 2
SSD_HEADS, SSD_HEAD_DIM, SSD_INNER, SSD_GROUPS, SSD_STATE, SSD_CHUNK = 16, 64, 1024, 2, 128, 64
SSD_XBC = SSD_INNER + 2 * SSD_GROUPS * SSD_STATE
GDN_HEADS, GDN_HEAD_DIM, GDN_WIDTH, GDN_CHUNK = 4, 128, 512, 64
GLA_HEADS, GLA_KEY_DIM, GLA_VAL_DIM, GLA_K, GLA_V, GLA_RANK, GLA_CHUNK = 4, 64, 128, 256, 512, 16, 16
GLA_BLOCK = 128
GLA_NORMALIZER = 16.0
FFN_DIM = 2816
FFN_HALF = FFN_DIM // 8
FFN_HALF_PAD = 384
FFN_UP_PAD = 16 * FFN_HALF_PAD
FFN_PAD = FFN_UP_PAD // 2
ALPHA = (2 * DEPTH) ** 0.25
LN_EPS = 1e-5
RMS_EPS = 1e-6
ADAM_LR, ADAM_B1, ADAM_B2, ADAM_EPS, ADAM_WD, ADAM_STEP = 0.001, 0.9, 0.999, 1e-08, 0.01, 10
LANES = 128
NEG_BIG = -1e30
VMEM_LIMIT = 56 * 1024 * 1024

IN_SPLITS = (("z", 1024), ("xbc", 1536), ("dt", 16), ("gqkv", 1536), ("ga", 4), ("gb", 4), ("gg", 512),
             ("lqkv", 1024), ("lglr", 16), ("lr", 512), ("gates", 3072))
IN_DIM = sum(w for _, w in IN_SPLITS)
PAD_SEGS = (("gates", 0, 3072, (("gates", 0),)), ("xbc", 3072, 1536, (("xbc", 0),)),
            ("gqkv", 4608, 1536, (("gqkv", 0),)), ("z", 6144, 1024, (("z", 0),)),
            ("lqkv", 7168, 1024, (("lqkv", 0),)), ("gg", 8192, 512, (("gg", 0),)), ("lr", 8704, 512, (("lr", 0),)),
            ("dt", 9216, 128, (("dt", 0),)), ("gab", 9344, 128, (("ga", 0), ("gb", 4))), ("lglr", 9472, 128, (("lglr", 0),)))
IN_PAD = 9728
SEG = {name: (off, width) for name, off, width, _ in PAD_SEGS}

BIG = ("w_in", "w_br_ssd", "w_br_gdn", "w_br_gla", "w_out", "ffn_w_up", "ffn_w_down")
COL_SHARDED = ("w_in", "w_br_gdn", "w_br_gla", "ffn_w_up")
SMALL_SHARDED = ("ssd_conv_w", "gdn_conv_w", "gla_gate_w2", "ffn_conv_w")
WEIGHTS = ("w_in", "ssd_conv_w", "ssd_conv_b", "ssd_dt_bias", "ssd_a_log", "ssd_d", "ssd_norm_w", "gdn_conv_w",
           "gdn_a_log", "gdn_dt_bias", "gdn_norm_w", "gla_gate_w2", "gla_gate_b", "gla_norm_w", "w_br_ssd", "w_br_gdn",
           "w_br_gla", "w_out", "ln1_g", "ln1_b", "ffn_w_up", "ffn_conv_w", "ffn_conv_b", "ffn_w_down", "ln2_g", "ln2_b")
SMALL = tuple(n for n in WEIGHTS if n not in BIG)
FLAT_W = 512


def _cparams(sem=None):
    kw = dict(vmem_limit_bytes=VMEM_LIMIT)
    if sem is not None:
        kw["dimension_semantics"] = sem
    return pltpu.CompilerParams(**kw)


_DIMS = {"nn": (((1,), (0,)), ((), ())), "nt": (((1,), (1,)), ((), ())), "tn": (((0,), (0,)), ((), ()))}


def _dot(a, b, dims="nn"):
    if MXU_DTYPE == F32:
        return lax.dot_general(a.astype(F32), b.astype(F32), _DIMS[dims], precision=HI, preferred_element_type=F32)
    return lax.dot_general(a.astype(MXU_DTYPE), b.astype(MXU_DTYPE), _DIMS[dims], preferred_element_type=F32)


def _dot_hi(a, b, dims="nn"):
    return lax.dot_general(a.astype(F32), b.astype(F32), _DIMS[dims], precision=HI, preferred_element_type=F32)


def _iota2(shape, axis):
    return lax.broadcasted_iota(jnp.int32, shape, axis)


def _tril(n, strict=False):
    r, c = _iota2((n, n), 0), _iota2((n, n), 1)
    return (r > c) if strict else (r >= c)


def _raw_dot(a, b, dims):
    return lax.dot_general(a, b, _DIMS[dims], preferred_element_type=F32)


def _dot_x3(a, b, dims="nn"):
    if MXU_DTYPE == F32:
        return _dot_hi(a, b, dims)
    ah, bh = a.astype(jnp.bfloat16), b.astype(jnp.bfloat16)
    al, bl = (a - ah.astype(F32)).astype(jnp.bfloat16), (b - bh.astype(F32)).astype(jnp.bfloat16)
    return _raw_dot(ah, bh, dims) + (_raw_dot(ah, bl, dims) + _raw_dot(al, bh, dims))


def _exact_dot(mask, b, dims, mask_first):
    if MXU_DTYPE == F32:
        return _dot_hi(mask, b, dims) if mask_first else _dot_hi(b, mask, dims)
    m = mask.astype(jnp.bfloat16)
    b1 = b.astype(jnp.bfloat16)
    r1 = b - b1.astype(F32)
    b2 = r1.astype(jnp.bfloat16)
    b3 = (r1 - b2.astype(F32)).astype(jnp.bfloat16)
    if mask_first:
        return _raw_dot(m, b1, dims) + (_raw_dot(m, b2, dims) + _raw_dot(m, b3, dims))
    return _raw_dot(b1, m, dims) + (_raw_dot(b2, m, dims) + _raw_dot(b3, m, dims))


@jax.custom_vjp
def _mask_left(mask, b):
    return _exact_dot(mask, b, "nn", True)


_mask_left.defvjp(lambda mask, b: (_mask_left(mask, b), mask),
                  lambda mask, d: (jnp.zeros_like(mask), _exact_dot(mask, d, "tn", True)))


@jax.custom_vjp
def _mask_right(a, mask):
    return _exact_dot(mask, a, "nn", False)


_mask_right.defvjp(lambda a, mask: (_mask_right(a, mask), mask),
                   lambda mask, d: (_exact_dot(mask, d, "nt", False), jnp.zeros_like(mask)))


@jax.custom_vjp
def _unit_lower_inverses(mats):
    n = mats[0].shape[0]
    eye = (_iota2((n, n), 0) == _iota2((n, n), 1)).astype(F32)
    xs = [eye - a for a in mats]
    ps = list(mats)
    k = 2
    while k < n:
        ps = [_dot_x3(p, p) for p in ps]
        xs = [x + _dot_x3(x, p) for x, p in zip(xs, ps)]
        k *= 2
    return xs


def _unit_lower_inverses_fwd(mats):
    ts = _unit_lower_inverses(mats)
    return ts, ts


def _unit_lower_inverses_bwd(ts, dts):
    mids = [_dot_x3(t, d, "tn") for t, d in zip(ts, dts)]
    return ([-_dot_x3(m, t, "nt") for m, t in zip(mids, ts)],)


_unit_lower_inverses.defvjp(_unit_lower_inverses_fwd, _unit_lower_inverses_bwd)


def _ssd_chunk(xs_, ps_, s_t):
    xbc, dtraw, z = xs_
    dt_bias, a_log, d_skip, norm_w = ps_
    L = xbc.shape[0]
    H, P, N, G = SSD_HEADS, SSD_HEAD_DIM, SSD_STATE, SSD_GROUPS
    W = SSD_INNER // G
    xs = xbc[:, :SSD_INNER]
    bm = xbc[:, SSD_INNER:SSD_INNER + G * N]
    cm = xbc[:, SSD_INNER + G * N:]
    dt = jax.nn.softplus(dtraw[:, :H] + dt_bias)
    a = dt * (-jnp.exp(a_log))
    causal = _tril(L)
    a_cs = _mask_left(causal.astype(F32), a)
    expand = (_iota2((H, SSD_INNER), 1) // P == _iota2((H, SSD_INNER), 0)).astype(F32)
    wide = _mask_right(jnp.concatenate([a_cs, dt, jnp.broadcast_to(d_skip, (L, H))], axis=0), expand)
    a_cs_x, dt_x, d_x = wide[:L], wide[L:2 * L], wide[2 * L:]
    a_end_x = a_cs_x[L - 1:L, :]
    a_cs_t, dt_t = a_cs.T, dt.T
    cb = [_dot(cm[:, g * N:(g + 1) * N], bm[:, g * N:(g + 1) * N], "nt") for g in range(G)]
    cb2 = [jnp.concatenate([c, c], axis=1) for c in cb]
    lane2 = _iota2((L, 2 * L), 1)
    left = lane2 < L
    causal2 = _iota2((L, 2 * L), 0) >= jnp.where(left, lane2, lane2 - L)
    pairs = range(0, H, 2)
    col2 = [jnp.where(left, a_cs[:, h:h + 1], a_cs[:, h + 1:h + 2]) for h in pairs]
    row2 = [jnp.concatenate([a_cs_t[h:h + 1, :], a_cs_t[h + 1:h + 2, :]], axis=1) for h in pairs]
    dt2 = [jnp.concatenate([dt_t[h:h + 1, :], dt_t[h + 1:h + 2, :]], axis=1) for h in pairs]
    ws2 = [cb2[h // (H // G)] * (jnp.exp(jnp.where(causal2, col2[i] - row2[i], NEG_BIG)) * dt2[i]) for i, h in enumerate(pairs)]
    first = _iota2((L, 2 * P), 1) < P
    ys = []
    for i, h in enumerate(pairs):
        x2 = xs[:, h * P:(h + 2) * P]
        ys.append(_dot(ws2[i], jnp.concatenate([jnp.where(first, x2, 0.0), jnp.where(first, 0.0, x2)], axis=0)))
    y = jnp.concatenate(ys, axis=1)
    y_in = jnp.concatenate([_dot(cm[:, g * N:(g + 1) * N], s_t[:, g * W:(g + 1) * W]) for g in range(G)], axis=1)
    y = y + y_in * jnp.exp(a_cs_x) + d_x * xs
    xw = xs * (jnp.exp(a_end_x - a_cs_x) * dt_x)
    st = jnp.concatenate([_dot(bm[:, g * N:(g + 1) * N], xw[:, g * W:(g + 1) * W], "tn") for g in range(G)], axis=1)
    s_new = s_t * jnp.exp(a_end_x) + st
    yg = y * jax.nn.silu(z)
    outs = []
    for g in range(G):
        part = yg[:, g * W:(g + 1) * W]
        outs.append(part * lax.rsqrt(jnp.mean(part * part, axis=1, keepdims=True) + RMS_EPS))
    return (jnp.concatenate(outs, axis=1) * norm_w,), s_new


GDN_PREP_CHUNKS = 4


def _gdn_prep(xs_, ps_):
    qkv, ab = xs_
    a_log, dt_bias = ps_
    B = qkv.shape[0]
    H, D, L = GDN_HEADS, GDN_HEAD_DIM, GDN_CHUNK
    g_all = -jnp.exp(a_log) * jax.nn.softplus(ab + dt_bias)
    row, col = _iota2((B, B), 0), _iota2((B, B), 1)
    g_cs = _mask_left((((row // L) == (col // L)) & (row >= col)).astype(F32), g_all)
    g_cs_t = g_cs.T
    beta_all = jax.nn.sigmoid(ab)
    incl, strict = _tril(L), _tril(L, strict=True)
    qs, ks, vs = [], [], []
    for h in range(H):
        q = qkv[:, h * D:(h + 1) * D]
        k = qkv[:, GDN_WIDTH + h * D:GDN_WIDTH + (h + 1) * D]
        qs.append(q * lax.rsqrt(jnp.sum(q * q, axis=1, keepdims=True) + RMS_EPS) * (D ** -0.5))
        ks.append(k * lax.rsqrt(jnp.sum(k * k, axis=1, keepdims=True) + RMS_EPS))
        vs.append(qkv[:, 2 * GDN_WIDTH + h * D:2 * GDN_WIDTH + (h + 1) * D])
    pairs = [(c, h) for c in range(B // L) for h in range(H)]
    rows = {c: slice(c * L, (c + 1) * L) for c in range(B // L)}
    q_ = {(c, h): qs[h][rows[c]] for c, h in pairs}
    k_ = {(c, h): ks[h][rows[c]] for c, h in pairs}
    col_ = {(c, h): g_cs[rows[c], h:h + 1] for c, h in pairs}
    beta_ = {(c, h): beta_all[rows[c], H + h:H + h + 1] for c, h in pairs}
    gamma = {p: jnp.exp(jnp.where(incl, col_[p] - g_cs_t[p[1]:p[1] + 1, rows[p[0]]], NEG_BIG)) for p in pairs}
    kb = {p: k_[p] * beta_[p] for p in pairs}
    a_mat = [jnp.where(strict, _dot(kb[p], k_[p], "nt") * gamma[p], 0.0) for p in pairs]
    attn = {p: jnp.where(incl, _dot(q_[p], k_[p], "nt") * gamma[p], 0.0) for p in pairs}
    t_mat = dict(zip(pairs, _unit_lower_inverses(a_mat)))
    u = {p: _dot(t_mat[p], vs[p[1]][rows[p[0]]] * beta_[p]) for p in pairs}
    w = {p: _dot(t_mat[p], kb[p] * jnp.exp(col_[p])) for p in pairs}
    qd = {p: q_[p] * jnp.exp(col_[p]) for p in pairs}
    kd = {p: k_[p] * jnp.exp(col_[p][L - 1:L, :] - col_[p]) for p in pairs}

    def whole(parts):
        return jnp.concatenate([jnp.concatenate([parts[(c, h)] for h in range(H)], axis=1) for c in range(B // L)], axis=0)

    return (whole(u), whole(w), whole(qd), whole(kd), whole(attn), g_cs)


def _gdn_scan(xs_, ps_, s):
    u, w, qd, kd, attn, g_cs, gate = xs_
    (norm_w,) = ps_
    L = u.shape[0]
    H, D = GDN_HEADS, GDN_HEAD_DIM
    heads = range(H)
    lanes = [slice(h * D, (h + 1) * D) for h in heads]
    s_h = [s[lanes[h], :] for h in heads]
    v_new = [u[:, lanes[h]] - _dot(w[:, lanes[h]], s_h[h]) for h in heads]
    o = [_dot(qd[:, lanes[h]], s_h[h]) + _dot(attn[:, h * L:(h + 1) * L], v_new[h]) for h in heads]
    decay = [jnp.exp(g_cs[L - 1:L, h:h + 1]) for h in heads]
    s_new = [s_h[h] * decay[h] + _dot(kd[:, lanes[h]], v_new[h], "tn") for h in heads]
    o = [o[h] * lax.rsqrt(jnp.mean(o[h] * o[h], axis=1, keepdims=True) + RMS_EPS) * norm_w * jax.nn.silu(gate[:, lanes[h]])
         for h in heads]
    return (jnp.concatenate(o, axis=1),), jnp.concatenate(s_new, axis=0)


def _gdn_forward(tag, gqkv, h, sp):
    T = gqkv.shape[0]
    blk = GDN_PREP_CHUNKS * GDN_CHUNK
    prep_in = [(gqkv, blk, 3 * GDN_WIDTH, 0), _seg_blk(h, "gab", blk)]
    prep_p = [_lane_pad(sp["gdn_a_log"]), _lane_pad(sp["gdn_dt_bias"])]
    mx = MXU_DTYPE
    prep = _chain_fwd(f"gdn_prep_{tag}", _gdn_prep, T // blk, prep_in, prep_p,
                      [(blk, GDN_WIDTH, F32), (blk, GDN_WIDTH, mx), (blk, GDN_WIDTH, mx), (blk, GDN_WIDTH, mx),
                       (blk, GDN_HEADS * GDN_CHUNK, mx), (blk, LANES, F32)])
    widths = [GDN_WIDTH] * 4 + [GDN_HEADS * GDN_CHUNK, LANES]
    scan_in = [(a, GDN_CHUNK, wd, 0) for a, wd in zip(prep, widths)] + [_seg_blk(h, "gg", GDN_CHUNK)]
    scan_p = [sp["gdn_norm_w"]]
    o, states = _chain_fwd(f"gdn_scan_{tag}", _gdn_scan, T // GDN_CHUNK, scan_in, scan_p, [(GDN_CHUNK, GDN_WIDTH, mx)],
                           (GDN_WIDTH, GDN_HEAD_DIM))
    return o, dict(prep_in=prep_in, prep_p=prep_p, scan_in=scan_in, scan_p=scan_p, states=states, widths=widths)


def _gdn_backward(tag, do, sv, dx_dtype):
    T = do.shape[0]
    blk = GDN_PREP_CHUNKS * GDN_CHUNK
    dscan, (dnorm,) = _chain_bwd(f"gdn_scan_bwd_{tag}", _gdn_scan, T // GDN_CHUNK, sv["scan_in"], sv["scan_p"],
                                 [(do, GDN_CHUNK, GDN_WIDTH)], sprev=sv["states"], dx_dtypes=[F32] * 6 + [dx_dtype])
    douts = [(d, blk, wd) for d, wd in zip(dscan[:6], sv["widths"])]
    (dgqkv, dgab), (da_log, ddt_bias) = _chain_bwd(f"gdn_prep_bwd_{tag}", _gdn_prep, T // blk, sv["prep_in"], sv["prep_p"],
                                                   douts, dx_dtypes=[F32, dx_dtype])
    return dgqkv, dgab, dscan[6], da_log[:, :GDN_HEADS], ddt_bias[:, :GDN_HEADS], dnorm


def _gla_block(xs_, ps_, s_t):
    qkv, glr, r = xs_
    w2, gate_b, norm_w = ps_
    B = qkv.shape[0]
    H, K, V, C = GLA_HEADS, GLA_KEY_DIM, GLA_VAL_DIM, GLA_CHUNK
    q = qkv[:, :GLA_K] * (K ** -0.5)
    k = qkv[:, GLA_K:2 * GLA_K]
    v = qkv[:, 2 * GLA_K:]
    gk = jax.nn.log_sigmoid(_dot(glr, w2) + gate_b) / GLA_NORMALIZER
    row, col = _iota2((B, B), 0), _iota2((B, B), 1)
    same = (row // C) == (col // C)
    mask = same & (row >= col)
    b_cs = _mask_left(mask.astype(F32), gk)
    b_end = _mask_left((col == (row // C) * C + (C - 1)).astype(F32), b_cs)
    q_e = q * jnp.exp(b_cs)
    k_e = k * jnp.exp(-b_cs)
    k_d = k * jnp.exp(b_end - b_cs)
    intra = []
    for h in range(H):
        a_mat = jnp.where(mask, _dot(q_e[:, h * K:(h + 1) * K], k_e[:, h * K:(h + 1) * K], "nt"), 0.0)
        intra.append(_dot(a_mat, v[:, h * V:(h + 1) * V]))
    o = jnp.concatenate(intra, axis=1)
    chunks = [slice(j * C, (j + 1) * C) for j in range(B // C)]
    fresh = [jnp.concatenate([_dot(v[sl, h * V:(h + 1) * V], k_d[sl, h * K:(h + 1) * K], "tn") for h in range(H)], axis=1)
             for sl in chunks]
    entering = []
    for j, sl in enumerate(chunks):
        entering.append(s_t)
        s_t = s_t * jnp.exp(b_end[j * C:j * C + 1, :]) + fresh[j]
    inter = [jnp.concatenate([_dot(q_e[sl, h * K:(h + 1) * K], entering[j][:, h * K:(h + 1) * K], "nt") for h in range(H)],
                             axis=1) for j, sl in enumerate(chunks)]
    o = o + jnp.concatenate(inter, axis=0)
    outs = []
    for h in range(H):
        oh = o[:, h * V:(h + 1) * V]
        oh = oh * lax.rsqrt(jnp.mean(oh * oh, axis=1, keepdims=True) + RMS_EPS) * norm_w
        outs.append(oh * jax.nn.silu(r[:, h * V:(h + 1) * V]))
    return (jnp.concatenate(outs, axis=1),), s_t


def _merge_fn(xs_, ps_):
    gates, y_ssd, y_gdn, y_gla = xs_
    d = D_MODEL
    return (jax.nn.sigmoid(gates[:, :d]) * y_ssd + jax.nn.sigmoid(gates[:, d:2 * d]) * y_gdn
            + jax.nn.sigmoid(gates[:, 2 * d:]) * y_gla,)


def _ln_fn(xs_, ps_):
    x, r = xs_
    g, b = ps_
    t = ALPHA * x + r
    mu = jnp.mean(t, axis=1, keepdims=True)
    var = jnp.mean(jnp.square(t - mu), axis=1, keepdims=True)
    return ((t - mu) * lax.rsqrt(var + LN_EPS) * g + b,)


def _row_spec(rows, width, colblk, n, reverse):
    if reverse:
        return pl.BlockSpec((rows, width), lambda c: (n - 1 - c, colblk))
    return pl.BlockSpec((rows, width), lambda c: (c, colblk))


def _full_spec(shape):
    zeros = (0,) * len(shape)
    return pl.BlockSpec(shape, lambda c: zeros)


def _chain_fwd(name, fn, n, blocked, full, out_defs, state_shape=None):
    nb, nf, no = len(blocked), len(full), len(out_defs)

    def body(*refs):
        xs = [r[...].astype(F32) for r in refs[:nb]]
        ps = [r[...] for r in refs[nb:nb + nf]]
        o_refs = refs[nb + nf:nb + nf + no]
        if state_shape is None:
            outs = fn(xs, ps)
        else:
            sprev_ref, s_ref = refs[nb + nf + no:]

            @pl.when(pl.program_id(0) == 0)
            def _():
                s_ref[...] = jnp.zeros_like(s_ref)

            s = s_ref[...]
            sprev_ref[0] = s
            outs, s_new = fn(xs, ps, s)
            s_ref[...] = s_new
        for r, o in zip(o_refs, outs):
            r[...] = o.astype(r.dtype)

    in_specs = [_row_spec(rows, width, cb, n, False) for _, rows, width, cb in blocked]
    in_specs += [_full_spec(a.shape) for a in full]
    out_specs = [_row_spec(rows, width, 0, n, False) for rows, width, _ in out_defs]
    out_shape = [jax.ShapeDtypeStruct((n * rows, width), dt) for rows, width, dt in out_defs]
    scratch = []
    if state_shape is not None:
        out_specs.append(pl.BlockSpec((1,) + state_shape, lambda c: (c, 0, 0)))
        out_shape.append(jax.ShapeDtypeStruct((n,) + state_shape, F32))
        scratch.append(pltpu.VMEM(state_shape, F32))
    return pl.pallas_call(body, name=name, grid=(n,), in_specs=in_specs, out_specs=out_specs, out_shape=out_shape,
                          scratch_shapes=scratch, compiler_params=_cparams(("arbitrary",)))(
        *[a for a, _, _, _ in blocked], *full)


def _chain_bwd(name, fn, n, blocked, full, douts, sprev=None, dx_dtypes=None):
    nb, nf, nd = len(blocked), len(full), len(douts)
    has_state = sprev is not None
    dx_dtypes = dx_dtypes or [F32] * nb

    def body(*refs):
        pos = 0
        b_refs = refs[pos:pos + nb]; pos += nb
        f_refs = refs[pos:pos + nf]; pos += nf
        d_refs = refs[pos:pos + nd]; pos += nd
        if has_state:
            sprev_ref = refs[pos]; pos += 1
        dx_refs = refs[pos:pos + nb]; pos += nb
        dp_refs = refs[pos:pos + nf]; pos += nf
        if has_state:
            ds_ref = refs[pos]

        @pl.when(pl.program_id(0) == 0)
        def _():
            for r in dp_refs:
                r[...] = jnp.zeros_like(r)
            if has_state:
                ds_ref[...] = jnp.zeros_like(ds_ref)

        xs = [r[...].astype(F32) for r in b_refs]
        ps = [r[...] for r in f_refs]
        dys = tuple(r[...].astype(F32) for r in d_refs)
        if has_state:
            _, vjp = jax.vjp(fn, xs, ps, sprev_ref[0])
            dxs, dps, ds = vjp((dys, ds_ref[...]))
            ds_ref[...] = ds
        else:
            _, vjp = jax.vjp(fn, xs, ps)
            dxs, dps = vjp(dys)
        for r, d in zip(dx_refs, dxs):
            r[...] = d.astype(r.dtype)
        for r, d in zip(dp_refs, dps):
            r[...] += d

    in_specs = [_row_spec(rows, width, cb, n, True) for _, rows, width, cb in blocked]
    in_specs += [_full_spec(a.shape) for a in full]
    in_specs += [_row_spec(rows, width, 0, n, True) for _, rows, width in douts]
    args = [a for a, _, _, _ in blocked] + list(full) + [a for a, _, _ in douts]
    scratch = []
    if has_state:
        st_shape = sprev.shape[1:]
        in_specs.append(pl.BlockSpec((1,) + st_shape, lambda c: (n - 1 - c, 0, 0)))
        args.append(sprev)
        scratch.append(pltpu.VMEM(st_shape, F32))
    out_specs = [_row_spec(rows, width, 0, n, True) for _, rows, width, _ in blocked]
    out_specs += [_full_spec(a.shape) for a in full]
    out_shape = [jax.ShapeDtypeStruct((n * rows, width), dt) for (_, rows, width, _), dt in zip(blocked, dx_dtypes)]
    out_shape += [jax.ShapeDtypeStruct(a.shape, F32) for a in full]
    res = pl.pallas_call(body, name=name, grid=(n,), in_specs=in_specs, out_specs=out_specs, out_shape=out_shape,
                         scratch_shapes=scratch, compiler_params=_cparams(("arbitrary",)))(*args)
    return res[:nb], res[nb:]


def _tile(n, target, unit):
    if n <= target:
        return n
    best = None
    for t in range(unit, target + 1, unit):
        if n % t == 0:
            best = t
    assert best is not None, (n, target, unit)
    return best


def _mm(name, a, b, dims="nn", out_dtype=F32, tm=2048, tn=512, tk=2048, after=None):
    if dims == "nn":
        (M, K), (_, N) = a.shape, b.shape
    elif dims == "nt":
        (M, K), (N, _) = a.shape, b.shape
    else:
        (K, M), (_, N) = a.shape, b.shape
    tm, tn, tk = _tile(M, tm, LANES), _tile(N, tn, LANES), _tile(K, tk, LANES)
    nk = K // tk
    extra = [] if after is None else [after]

    def body(*refs):
        a_ref, b_ref = refs[:2]
        o_ref, acc_ref = refs[-2:]
        part = _dot(a_ref[...], b_ref[...], dims)
        if nk == 1:
            o_ref[...] = part.astype(o_ref.dtype)
            return
        k = pl.program_id(2)

        @pl.when(k == 0)
        def _():
            acc_ref[...] = part

        @pl.when((k > 0) & (k < nk - 1))
        def _():
            acc_ref[...] += part

        @pl.when(k == nk - 1)
        def _():
            o_ref[...] = (acc_ref[...] + part).astype(o_ref.dtype)

    if dims == "tn":
        a_spec = pl.BlockSpec((tk, tm), lambda j, i, k: (k, i))
    else:
        a_spec = pl.BlockSpec((tm, tk), lambda j, i, k: (i, k))
    if dims == "nt":
        b_spec = pl.BlockSpec((tn, tk), lambda j, i, k: (j, k))
    else:
        b_spec = pl.BlockSpec((tk, tn), lambda j, i, k: (k, j))
    return pl.pallas_call(
        body, name=name, grid=(N // tn, M // tm, nk), in_specs=[a_spec, b_spec] + [ANY] * len(extra),
        out_specs=pl.BlockSpec((tm, tn), lambda j, i, k: (i, j)), out_shape=jax.ShapeDtypeStruct((M, N), out_dtype),
        scratch_shapes=[pltpu.VMEM((tm, tn) if nk > 1 else (8, LANES), F32)],
        compiler_params=_cparams(("parallel", "parallel", "arbitrary")))(a, b, *extra)


CONV_CB = 256


def _shift_down(x, k):
    if k == 0:
        return x
    return jnp.where(_iota2(x.shape, 0) >= k, pltpu.roll(x, k, 0), 0.0)


def _shift_up(x, k):
    if k == 0:
        return x
    t = x.shape[0]
    return jnp.where(_iota2(x.shape, 0) < t - k, pltpu.roll(x, t - k, 0), 0.0)


def _conv_pre(x, w, b):
    kk = w.shape[0]
    pre = x * w[kk - 1:kk, :]
    for k in range(kk - 1):
        pre = pre + _shift_down(x, kk - 1 - k) * w[k:k + 1, :]
    return pre if b is None else pre + b


EDGE = 16


def _conv_pre_rot(x, w, b):
    kk = w.shape[0]
    pre = x * w[kk - 1:kk, :]
    for k in range(kk - 1):
        pre = pre + pltpu.roll(x, kk - 1 - k, 0) * w[k:k + 1, :]
    return pre if b is None else pre + b


def _conv_t_local(d, w):
    kk = w.shape[0]
    out = d * w[kk - 1:kk, :]
    for k in range(kk - 1):
        out = out + _shift_up(d, kk - 1 - k) * w[k:k + 1, :]
    return out


def _col_sum(a):
    return jnp.sum(a, axis=0, keepdims=True)


def _conv_bwd_rot(x_ref, w, dpre, dpre_head, dx_ref, dw_ref, db_ref):
    T = dpre.shape[0]
    kk = w.shape[0]
    x = x_ref[...]
    x_head, x_tail = x_ref[0:EDGE, :], x_ref[T - EDGE:T, :]
    wrong_head = dpre[0:EDGE]
    dx = dpre * w[kk - 1:kk, :]
    for k in range(kk - 1):
        dx = dx + pltpu.roll(dpre, T - (kk - 1 - k), 0) * w[k:k + 1, :]
    dx_ref[...] = dx.astype(dx_ref.dtype)
    top = jnp.concatenate([dpre_head, dpre[EDGE:2 * EDGE]], axis=0)
    dx_ref[0:EDGE, :] = _conv_t_local(top, w)[0:EDGE].astype(dx_ref.dtype)
    dx_ref[T - EDGE:T, :] = _conv_t_local(dpre[T - EDGE:T], w).astype(dx_ref.dtype)
    ends = jnp.concatenate([x_tail, x_head], axis=0)
    dw_ref[kk - 1:kk, :] = _col_sum(dpre * x) + _col_sum((dpre_head - wrong_head) * x_head)
    for k in range(kk - 1):
        s = kk - 1 - k
        rotated_head = pltpu.roll(ends, s, 0)[EDGE:2 * EDGE]
        dw_ref[k:k + 1, :] = (_col_sum(dpre * pltpu.roll(x, s, 0)) - _col_sum(wrong_head * rotated_head)
                              + _col_sum(dpre_head * _shift_down(x_head, s)))
    if db_ref is not None:
        db_ref[...] = _col_sum(dpre) + _col_sum(dpre_head - wrong_head)


def _dsilu(pre):
    sg = jax.nn.sigmoid(pre)
    return sg * (1.0 + pre * (1.0 - sg))


def _conv_silu_fwd(name, src, col0, w, b):
    T = src.shape[0]
    kk, C = w.shape
    cb = CONV_CB
    off = col0 // cb

    def body(*refs):
        x_ref, w_ref, o_ref = refs[0], refs[1], refs[-1]
        b_val = refs[2][...] if b is not None else None
        o_ref[...] = jax.nn.silu(_conv_pre_rot(x_ref[...], w_ref[...], b_val))
        o_ref[0:EDGE, :] = jax.nn.silu(_conv_pre(x_ref[0:EDGE, :], w_ref[...], b_val))

    in_specs = [pl.BlockSpec((T, cb), lambda j: (0, off + j)), pl.BlockSpec((kk, cb), lambda j: (0, j))]
    args = [src, w]
    if b is not None:
        in_specs.append(pl.BlockSpec((1, cb), lambda j: (0, j)))
        args.append(b)
    return pl.pallas_call(body, name=name, grid=(C // cb,), in_specs=in_specs,
                          out_specs=pl.BlockSpec((T, cb), lambda j: (0, j)), out_shape=jax.ShapeDtypeStruct((T, C), F32),
                          compiler_params=_cparams(("parallel",)))(*args)


def _conv_silu_bwd(name, src, col0, w, b, dy, dx_dtype):
    T = src.shape[0]
    kk, C = w.shape
    cb = CONV_CB
    off = col0 // cb
    has_b = b is not None

    def body(*refs):
        x_ref, w_ref = refs[:2]
        pos = 2
        b_val = None
        if has_b:
            b_val = refs[pos][...]; pos += 1
        dy_ref = refs[pos]; pos += 1
        dx_ref, dw_ref = refs[pos], refs[pos + 1]
        db_ref = refs[pos + 2] if has_b else None
        wv = w_ref[...]
        dpre = dy_ref[...] * _dsilu(_conv_pre_rot(x_ref[...], wv, b_val))
        dpre_head = dy_ref[0:EDGE, :] * _dsilu(_conv_pre(x_ref[0:EDGE, :], wv, b_val))
        _conv_bwd_rot(x_ref, wv, dpre, dpre_head, dx_ref, dw_ref, db_ref)

    in_specs = [pl.BlockSpec((T, cb), lambda j: (0, off + j)), pl.BlockSpec((kk, cb), lambda j: (0, j))]
    args = [src, w]
    if has_b:
        in_specs.append(pl.BlockSpec((1, cb), lambda j: (0, j)))
        args.append(b)
    in_specs.append(pl.BlockSpec((T, cb), lambda j: (0, j)))
    args.append(dy)
    out_specs = [pl.BlockSpec((T, cb), lambda j: (0, j)), pl.BlockSpec((kk, cb), lambda j: (0, j))]
    out_shape = [jax.ShapeDtypeStruct((T, C), dx_dtype), jax.ShapeDtypeStruct((kk, C), F32)]
    if has_b:
        out_specs.append(pl.BlockSpec((1, cb), lambda j: (0, j)))
        out_shape.append(jax.ShapeDtypeStruct((1, C), F32))
    return pl.pallas_call(body, name=name, grid=(C // cb,), in_specs=in_specs, out_specs=out_specs, out_shape=out_shape,
                          compiler_params=_cparams(("parallel",)))(*args)


def _ffn_glu_fwd(name, up, w, b, out_dtype=F32):
    T = up.shape[0]
    kk = w.shape[0]
    cb = CONV_CB
    width = up.shape[1] // 2
    nblk = width // cb

    def body(g_ref, u_ref, wg_ref, wu_ref, bg_ref, bu_ref, o_ref):
        g = _conv_pre_rot(g_ref[...], wg_ref[...], bg_ref[...])
        u = _conv_pre_rot(u_ref[...], wu_ref[...], bu_ref[...])
        o_ref[...] = (jax.nn.silu(g) * u).astype(o_ref.dtype)
        g = _conv_pre(g_ref[0:EDGE, :], wg_ref[...], bg_ref[...])
        u = _conv_pre(u_ref[0:EDGE, :], wu_ref[...], bu_ref[...])
        o_ref[0:EDGE, :] = (jax.nn.silu(g) * u).astype(o_ref.dtype)

    lo, hi = (lambda j: (0, j)), (lambda j: (0, nblk + j))
    in_specs = [pl.BlockSpec((T, cb), lo), pl.BlockSpec((T, cb), hi), pl.BlockSpec((kk, cb), lo), pl.BlockSpec((kk, cb), hi),
                pl.BlockSpec((1, cb), lo), pl.BlockSpec((1, cb), hi)]
    return pl.pallas_call(body, name=name, grid=(nblk,), in_specs=in_specs, out_specs=pl.BlockSpec((T, cb), lo),
                          out_shape=jax.ShapeDtypeStruct((T, width), out_dtype),
                          compiler_params=_cparams(("parallel",)))(up, up, w, w, b, b)


def _ffn_glu_bwd(name, up, w, b, dact, dx_dtype):
    T = up.shape[0]
    kk = w.shape[0]
    cb = CONV_CB
    width = up.shape[1] // 2
    nblk = width // cb

    def body(g_ref, u_ref, wg_ref, wu_ref, bg_ref, bu_ref, d_ref, dg_ref, du_ref, dwg_ref, dwu_ref, dbg_ref, dbu_ref):
        wg, wu = wg_ref[...], wu_ref[...]
        g = _conv_pre_rot(g_ref[...], wg, bg_ref[...])
        u = _conv_pre_rot(u_ref[...], wu, bu_ref[...])
        d = d_ref[...].astype(F32)
        g_head = _conv_pre(g_ref[0:EDGE, :], wg, bg_ref[...])
        u_head = _conv_pre(u_ref[0:EDGE, :], wu, bu_ref[...])
        d_head = d_ref[0:EDGE, :].astype(F32)
        _conv_bwd_rot(g_ref, wg, d * u * _dsilu(g), d_head * u_head * _dsilu(g_head), dg_ref, dwg_ref, dbg_ref)
        _conv_bwd_rot(u_ref, wu, d * jax.nn.silu(g), d_head * jax.nn.silu(g_head), du_ref, dwu_ref, dbu_ref)

    lo, hi = (lambda j: (0, j)), (lambda j: (0, nblk + j))
    in_specs = [pl.BlockSpec((T, cb), lo), pl.BlockSpec((T, cb), hi), pl.BlockSpec((kk, cb), lo), pl.BlockSpec((kk, cb), hi),
                pl.BlockSpec((1, cb), lo), pl.BlockSpec((1, cb), hi), pl.BlockSpec((T, cb), lo)]
    out_specs = [pl.BlockSpec((T, cb), lo)] * 2 + [pl.BlockSpec((kk, cb), lo)] * 2 + [pl.BlockSpec((1, cb), lo)] * 2
    out_shape = ([jax.ShapeDtypeStruct((T, width), dx_dtype)] * 2 + [jax.ShapeDtypeStruct((kk, width), F32)] * 2
                 + [jax.ShapeDtypeStruct((1, width), F32)] * 2)
    return pl.pallas_call(body, name=name, grid=(nblk,), in_specs=in_specs, out_specs=out_specs, out_shape=out_shape,
                          compiler_params=_cparams(("parallel",)))(up, up, w, w, b, b, dact)


def _loss_head(y, target):
    T, D = y.shape
    tb = _tile(T, 256, 8)

    def body(y_ref, t_ref, dy_ref, l_ref):
        @pl.when(pl.program_id(0) == 0)
        def _():
            l_ref[...] = jnp.zeros_like(l_ref)

        err = y_ref[...] - t_ref[...]
        dy_ref[...] = err * (1.0 / D)
        l_ref[...] += jnp.sum(err * err, axis=0, keepdims=True) * (0.5 / D)

    spec = pl.BlockSpec((tb, D), lambda i: (i, 0))
    return pl.pallas_call(body, name="loss_head", grid=(T // tb,), in_specs=[spec, spec],
                          out_specs=[spec, pl.BlockSpec((1, D), lambda i: (0, 0))],
                          out_shape=[jax.ShapeDtypeStruct((T, D), F32), jax.ShapeDtypeStruct((1, D), F32)],
                          compiler_params=_cparams(("arbitrary",)))(y, target)


def _adamw_math(w, g, m, v):
    m = ADAM_B1 * m + (1.0 - ADAM_B1) * g
    v = ADAM_B2 * v + (1.0 - ADAM_B2) * jnp.square(g)
    m_hat = m / (1.0 - ADAM_B1 ** ADAM_STEP)
    v_hat = v / (1.0 - ADAM_B2 ** ADAM_STEP)
    return -ADAM_LR * (m_hat / (jnp.sqrt(v_hat) + ADAM_EPS) + ADAM_WD * w), m, v


def _adamw(name, w, g, m, v, after=None):
    A, R, C = w.shape
    if C % LANES == 0:
        rb, cb = _slab(R, C)
    else:
        rb, cb = _tile(R, max(8, SLAB_BYTES // 2 // (C * 4) // 8 * 8), 8), C
    extra = [] if after is None else [after]

    def body(w_ref, g_ref, m_ref, v_ref, *rest):
        d_ref, mo_ref, vo_ref = rest[-3:]
        d, mn, vn = _adamw_math(w_ref[...], g_ref[...], m_ref[...], v_ref[...])
        d_ref[...] = d
        mo_ref[...] = mn
        vo_ref[...] = vn

    spec = pl.BlockSpec((1, rb, cb), lambda a, r, q: (a, r, q))
    return pl.pallas_call(body, name=name, grid=(A, R // rb, C // cb), in_specs=[spec] * 4 + [ANY] * len(extra),
                          out_specs=[spec] * 3, out_shape=[jax.ShapeDtypeStruct(w.shape, F32)] * 3,
                          compiler_params=_cparams(("parallel", "parallel", "parallel")))(w, g, m, v, *extra)


def _adamw_small(parts, w, m, v):
    def body(p_ref, w_ref, m_ref, v_ref, g_ref, d_ref, mo_ref, vo_ref):
        g = p_ref[0]
        for i in range(1, N_DEV):
            g = g + p_ref[i]
        d, mn, vn = _adamw_math(w_ref[...], g, m_ref[...], v_ref[...])
        g_ref[...] = g
        d_ref[...] = d
        mo_ref[...] = mn
        vo_ref[...] = vn

    return pl.pallas_call(body, name="adamw_small", out_shape=[jax.ShapeDtypeStruct(w.shape, F32)] * 4,
                          compiler_params=_cparams())(parts, w, m, v)


def _add_blocks(name, a, b, out_dtype=F32):
    n, R, W = a.shape
    rb = _tile(R, 512, 8)

    def body(a_ref, b_ref, o_ref):
        o_ref[...] = (a_ref[...].astype(F32) + b_ref[...].astype(F32)).astype(o_ref.dtype)

    spec = pl.BlockSpec((1, rb, W), lambda i, r: (i, r, 0))
    return pl.pallas_call(body, name=name, grid=(n, R // rb), in_specs=[spec, spec], out_specs=spec,
                          out_shape=jax.ShapeDtypeStruct(a.shape, out_dtype),
                          compiler_params=_cparams(("parallel", "parallel")))(a, b)


SLAB_BYTES = 5 << 19


def _slab(R, W):
    if R % 16 == 0:
        return _tile(R, max(16, SLAB_BYTES // (4 * W) // 16 * 16), 16), W
    assert W % LANES == 0, (R, W)
    return R, _tile(W, max(LANES, SLAB_BYTES // (4 * R) // LANES * LANES), LANES)


def _pair_add(name, g, other, c, chip):
    _, R, W = g.shape
    rb, cb = _slab(R, W)

    def body(s_ref, a_ref, b_ref, send_ref, own_ref):
        s = a_ref[0] + b_ref[0]
        send_ref[0] = s.astype(send_ref.dtype)

        @pl.when(pl.program_id(2) == s_ref[1])
        def _():
            own_ref[...] = s

    grid_spec = pltpu.PrefetchScalarGridSpec(
        num_scalar_prefetch=1, grid=(R // rb, W // cb, 4),
        in_specs=[pl.BlockSpec((1, rb, cb), lambda r, q, p, s_ref: (2 * p + s_ref[0], r, q)),
                  pl.BlockSpec((1, rb, cb), lambda r, q, p, s_ref: (p, r, q))],
        out_specs=[pl.BlockSpec((1, rb, cb), lambda r, q, p, s_ref: (p, r, q)),
                   pl.BlockSpec((rb, cb), lambda r, q, p, s_ref: (r, q))])
    scalars = jnp.stack([c, chip]).astype(jnp.int32)
    return pl.pallas_call(body, name=name, grid_spec=grid_spec,
                          out_shape=[jax.ShapeDtypeStruct((4, R, W), MXU_DTYPE), jax.ShapeDtypeStruct((R, W), F32)],
                          compiler_params=_cparams(("parallel", "parallel", "arbitrary")))(scalars, g, other)


def _sum4(name, own, parts):
    R, W = own.shape
    rb, cb = _slab(R, W)

    def body(o_ref, p_ref, out_ref):
        out_ref[...] = ((o_ref[...] + p_ref[0].astype(F32)) + p_ref[1].astype(F32)) + p_ref[2].astype(F32)

    return pl.pallas_call(body, name=name, grid=(R // rb, W // cb),
                          in_specs=[pl.BlockSpec((rb, cb), lambda r, q: (r, q)), pl.BlockSpec((3, rb, cb), lambda r, q: (0, r, q))],
                          out_specs=pl.BlockSpec((rb, cb), lambda r, q: (r, q)), out_shape=jax.ShapeDtypeStruct((R, W), F32),
                          compiler_params=_cparams(("parallel", "parallel")))(own, parts)


MESH = pl.DeviceIdType.MESH
ANY = pl.BlockSpec(memory_space=pl.ANY)


def _place():
    return lax.axis_index("x"), lax.axis_index("y"), lax.axis_index("c")


def _other_chips(x, y):
    return [(1 - x, y), (x, 1 - y), (1 - x, 1 - y)]


def _all_gather(name, blocks):
    n = len(blocks)

    def body(*refs):
        x_refs, out_refs = refs[:n], refs[n:2 * n]
        send_sems, recv_sems, local_sems = refs[2 * n:]
        x, y, c = _place()
        me, sibling = (x, y, c), (x, y, 1 - c)
        chips = _other_chips(x, y)

        def slot(a, px, py, pc):
            return out_refs[a].at[4 * px + 2 * py + pc]

        def copy(a, k, blk, to, src=None):
            return pltpu.make_async_remote_copy(src_ref=slot(a, *blk) if src is None else src, dst_ref=slot(a, *blk),
                                                send_sem=send_sems.at[a, k], recv_sem=recv_sems.at[a, k],
                                                device_id=to, device_id_type=MESH)

        mine = [pltpu.make_async_copy(x_refs[a], slot(a, *me), local_sems.at[a]) for a in range(n)]
        for cp in mine:
            cp.start()
        first = []
        for j, chip in enumerate(chips):
            first += [copy(a, 1 + j, me, (*chip, c), src=x_refs[a]) for a in range(n)]
        first += [copy(a, 0, me, sibling, src=x_refs[a]) for a in range(n)]
        for cp in first:
            cp.start()
        passed = []
        for j, chip in enumerate(chips):
            for a in range(n):
                copy(a, 1 + j, (*chip, c), me).wait_recv()
                passed.append(copy(a, 4 + j, (*chip, c), sibling))
                passed[-1].start()
        for a in range(n):
            copy(a, 0, sibling, me).wait_recv()
        for j, chip in enumerate(chips):
            for a in range(n):
                copy(a, 4 + j, (*chip, 1 - c), me).wait_recv()
        for cp in first + passed:
            cp.wait_send()
        for cp in mine:
            cp.wait()

    return pl.pallas_call(body, name=name, in_specs=[ANY] * n, out_specs=[ANY] * n,
                          out_shape=[jax.ShapeDtypeStruct((N_DEV,) + b.shape, b.dtype) for b in blocks],
                          scratch_shapes=[pltpu.SemaphoreType.DMA((n, 7)), pltpu.SemaphoreType.DMA((n, 7)),
                                          pltpu.SemaphoreType.DMA((n,))])(*blocks)


def _routes_to_sibling(x, y, c):
    return [(2 * p + (1 - c), p, (x, y, 1 - c)) for p in range(4)]


def _routes_to_chips(x, y, c):
    return [(2 * px + py, j, (px, py, c)) for j, (px, py) in enumerate(_other_chips(x, y))]


def _routes_block_to_chips(x, y, c):
    me = 4 * x + 2 * y + c
    return [(me, me, (px, py, c)) for px, py in _other_chips(x, y)]


def _routes_blocks_to_sibling(x, y, c):
    return [(4 * px + 2 * py + c, 4 * px + 2 * py + c, (x, y, 1 - c)) for px, py in [(x, y)] + _other_chips(x, y)]


def _route_copies(routes, src_refs, land_refs, send_sems, recv_sems):
    x, y, c = _place()
    copies = []
    for a, (src, land) in enumerate(zip(src_refs, land_refs)):
        plan = routes(x, y, c)
        for k, (s, d, target) in enumerate(plan):
            i = a * len(plan) + k
            copies.append(pltpu.make_async_remote_copy(src_ref=src.at[s], dst_ref=land.at[d], send_sem=send_sems.at[i],
                                                       recv_sem=recv_sems.at[i], device_id=target, device_id_type=MESH))
    return copies


def _exchange(name, routes, n_routes, srcs, land_slots):
    n = len(srcs)

    def body(*refs):
        copies = _route_copies(routes, refs[:n], refs[n:2 * n], refs[2 * n], refs[2 * n + 1])
        for cp in copies:
            cp.start()
        for cp in copies:
            cp.wait_recv()
        for cp in copies:
            cp.wait_send()

    return pl.pallas_call(body, name=name, in_specs=[ANY] * n, out_specs=[ANY] * n,
                          out_shape=[jax.ShapeDtypeStruct((land_slots,) + s.shape[1:], s.dtype) for s in srcs],
                          scratch_shapes=[pltpu.SemaphoreType.DMA((n * n_routes,)), pltpu.SemaphoreType.DMA((n * n_routes,))])(*srcs)


HBM_SPEC = pl.BlockSpec(memory_space=pltpu.HBM)
SEM_SPEC = pl.BlockSpec(memory_space=pltpu.SEMAPHORE)
DATAFLOW = pltpu.SideEffectType.DATAFLOW_SIDE_EFFECTING


def _exchange_start(name, routes, n_routes, srcs, lands, after=None):
    n = len(srcs)
    in_place = lands is None
    bufs = list(srcs) + ([] if in_place else list(lands))
    nb = len(bufs)
    extra = [] if after is None else [after]

    def body(*refs):
        src_refs = refs[:n]
        land_refs = src_refs if in_place else refs[n:nb]
        send_sems, recv_sems = refs[nb + len(extra)], refs[nb + len(extra) + 1]
        token = refs[-1]
        for cp in _route_copies(routes, src_refs, land_refs, send_sems, recv_sems):
            cp.start()
        token[...] = jnp.zeros_like(token)

    sems = [pltpu.SemaphoreType.DMA((n * n_routes,)), pltpu.SemaphoreType.DMA((n * n_routes,))]
    out = pl.pallas_call(
        body, name=name, in_specs=[HBM_SPEC] * nb + [ANY] * len(extra),
        out_shape=sems + [pltpu.HBM(b.shape, b.dtype) for b in bufs] + [jax.ShapeDtypeStruct((8, LANES), F32)],
        out_specs=[SEM_SPEC, SEM_SPEC] + [HBM_SPEC] * nb + [pl.BlockSpec(memory_space=pltpu.VMEM)],
        input_output_aliases={i: 2 + i for i in range(nb)},
        compiler_params=pltpu.CompilerParams(has_side_effects=DATAFLOW))(
        *[pltpu.with_memory_space_constraint(b, pltpu.HBM) for b in bufs], *extra)
    return (out[0], out[1], list(out[2:2 + nb])), out[-1]


def _exchange_wait(name, routes, n_routes, n, started, after):
    send_sems, recv_sems, bufs = started
    nb = len(bufs)
    in_place = nb == n

    def body(*refs):
        src_refs = refs[:n]
        land_refs = src_refs if in_place else refs[n:nb]
        for cp in _route_copies(routes, src_refs, land_refs, refs[nb], refs[nb + 1]):
            cp.wait_send()
            cp.wait_recv()

    out = pl.pallas_call(
        body, name=name, in_specs=[HBM_SPEC] * nb + [SEM_SPEC, SEM_SPEC, ANY],
        out_shape=[pltpu.HBM(b.shape, b.dtype) for b in bufs], out_specs=[HBM_SPEC] * nb,
        input_output_aliases={i: i for i in range(nb)},
        compiler_params=pltpu.CompilerParams(has_side_effects=DATAFLOW))(*bufs, send_sems, recv_sems, after)
    return list(out[:n]) if in_place else (list(out[:n]), list(out[n:]))


def _pair_sums(tag, gs, from_sibling):
    x, y, c = _place()
    return [_pair_add(f"rs_add_{tag}_{i}", g, o, c, 2 * x + y) for i, (g, o) in enumerate(zip(gs, from_sibling))]


def _reduce_scatter(tag, gs):
    sums = _pair_sums(tag, gs, _exchange(f"rs_swap_{tag}", _routes_to_sibling, 4, gs, 4))
    got = _exchange(f"rs_chips_{tag}", _routes_to_chips, 3, [s[0] for s in sums], 3)
    return [_sum4(f"rs_sum_{tag}_{i}", s[1], q) for i, (s, q) in enumerate(zip(sums, got))]


def _reduce_scatter_begin(tag, gs):
    lands = [lax.empty((4,) + g.shape[1:], g.dtype) for g in gs]
    swap, token = _exchange_start(f"rs_swap_{tag}_start", _routes_to_sibling, 4, gs, lands)
    return dict(tag=tag, n=len(gs), swap=swap), token


def _reduce_scatter_middle(state, after):
    tag, n = state["tag"], state["n"]
    gs, from_sibling = _exchange_wait(f"rs_swap_{tag}_wait", _routes_to_sibling, 4, n, state["swap"], after)
    state["sums"] = _pair_sums(tag, gs, from_sibling)
    partials = [s[0] for s in state["sums"]]
    lands = [lax.empty((3,) + p.shape[1:], p.dtype) for p in partials]
    state["chips"], token = _exchange_start(f"rs_chips_{tag}_start", _routes_to_chips, 3, partials, lands)
    return token


def _reduce_scatter_end(state, after):
    tag = state["tag"]
    _, got = _exchange_wait(f"rs_chips_{tag}_wait", _routes_to_chips, 3, state["n"], state["chips"], after)
    return [_sum4(f"rs_sum_{tag}_{i}", s[1], q) for i, (s, q) in enumerate(zip(state["sums"], got))]


def _all_gather_begin(tag, blocks, after):
    dev = 4 * lax.axis_index("x") + 2 * lax.axis_index("y") + lax.axis_index("c")
    zones = [lax.dynamic_update_slice_in_dim(lax.empty((N_DEV,) + b.shape, b.dtype), b[None], dev, axis=0) for b in blocks]
    chips, token = _exchange_start(f"gather_{tag}_chips_start", _routes_block_to_chips, 3, zones, None, after)
    return dict(tag=tag, n=len(blocks), chips=chips), token


def _all_gather_middle(state, after):
    tag, n = state["tag"], state["n"]
    zones = _exchange_wait(f"gather_{tag}_chips_wait", _routes_block_to_chips, 3, n, state["chips"], after)
    state["sibling"], token = _exchange_start(f"gather_{tag}_sibling_start", _routes_blocks_to_sibling, 4, zones, None)
    return token


def _all_gather_end(state, after):
    return _exchange_wait(f"gather_{state['tag']}_sibling_wait", _routes_blocks_to_sibling, 4, state["n"], state["sibling"], after)


PACK_UNIT = 8 * LANES


def _packed_size(shape):
    return -(-math.prod(shape) // PACK_UNIT) * PACK_UNIT


def _pack(arrays, dtype):
    parts = []
    for a in arrays:
        flat = a.reshape(-1).astype(dtype)
        parts.append(jnp.pad(flat, (0, _packed_size(a.shape) - flat.shape[0])))
    return jnp.concatenate(parts).reshape(-1, LANES)


def _unpack(flat, shapes, lead=()):
    flat = flat.reshape(lead + (-1,))
    out, pos = [], 0
    for s in shapes:
        out.append(flat[..., pos:pos + math.prod(s)].reshape(lead + tuple(s)))
        pos += _packed_size(s)
    return out


def _ffn_pad_rows(a):
    n = a.shape[0] // FFN_HALF
    a = jnp.pad(a.reshape(n, FFN_HALF, a.shape[1]), ((0, 0), (0, FFN_HALF_PAD - FFN_HALF), (0, 0)))
    return a.reshape(n * FFN_HALF_PAD, a.shape[2])


def _ffn_unpad_rows(a):
    n = a.shape[0] // FFN_HALF_PAD
    return a.reshape(n, FFN_HALF_PAD, a.shape[1])[:, :FFN_HALF].reshape(n * FFN_HALF, a.shape[1])


def _ffn_pad_cols(a):
    n = a.shape[1] // FFN_HALF
    a = jnp.pad(a.reshape(a.shape[0], n, FFN_HALF), ((0, 0), (0, 0), (0, FFN_HALF_PAD - FFN_HALF)))
    return a.reshape(a.shape[0], n * FFN_HALF_PAD)


def _ffn_unpad_cols(a):
    n = a.shape[1] // FFN_HALF_PAD
    return a.reshape(a.shape[0], n, FFN_HALF_PAD)[:, :, :FFN_HALF].reshape(a.shape[0], n * FFN_HALF)


def _shard_to_send(name, shard):
    if name in ("w_in", "w_br_gdn", "w_br_gla"):
        shard = shard.T
    elif name == "ffn_w_up":
        shard = _ffn_pad_rows(shard.T)
    return shard.astype(MXU_DTYPE)


KEPT_TRANSPOSED = ("w_in", "w_br_gdn", "w_br_gla", "ffn_w_up")


def _whole_from_gathered(name, g):
    if name == "w_in":
        return _in_proj_from_shards(g)
    if name == "ffn_w_down":
        return jnp.pad(g, ((0, 0), (0, FFN_HALF_PAD - FFN_HALF), (0, 0))).reshape(FFN_PAD, g.shape[2])
    return g.reshape(N_DEV * g.shape[1], g.shape[2])


def _slots_from_whole(name, gw):
    if name == "w_in":
        return _in_proj_to_slots(gw)
    return gw.reshape(N_DEV, gw.shape[0] // N_DEV, gw.shape[1])


def _shard_from_slot(name, s):
    if name == "ffn_w_up":
        return _ffn_unpad_rows(s)
    if name == "ffn_w_down":
        return s[:FFN_HALF]
    return s


def _in_proj_pieces():
    starts, pos = {}, 0
    for n, width in IN_SPLITS:
        starts[n] = (pos, width)
        pos += width
    return [(starts[ref][0], off + lane, starts[ref][1]) for _, off, _, pieces in PAD_SEGS for ref, lane in pieces]


def _in_proj_moves():
    cs = IN_DIM // N_DEV
    moves = []
    for src, dst, n in sorted(_in_proj_pieces()):
        at = src
        while at < src + n:
            d = at // cs
            end = min(src + n, (d + 1) * cs)
            moves.append((d, at - d * cs, dst + at - src, end - at))
            at = end
    return moves


RELAYOUT_LANES = 128


def _in_proj_from_shards(g):
    _, cs, D = g.shape

    def body(g_ref, o_ref):
        o_ref[...] = jnp.zeros_like(o_ref)
        for d, i0, r0, n in _in_proj_moves():
            o_ref[r0:r0 + n, :] = g_ref[d, i0:i0 + n, :]

    cb = RELAYOUT_LANES
    return pl.pallas_call(body, name="w_in_rows_in", grid=(D // cb,),
                          in_specs=[pl.BlockSpec((N_DEV, cs, cb), lambda j: (0, 0, j))],
                          out_specs=pl.BlockSpec((IN_PAD, cb), lambda j: (0, j)),
                          out_shape=jax.ShapeDtypeStruct((IN_PAD, D), g.dtype), compiler_params=_cparams(("parallel",)))(g)


def _in_proj_to_slots(gw):
    D = gw.shape[1]
    cs = IN_DIM // N_DEV

    def body(x_ref, o_ref):
        for d, i0, r0, n in _in_proj_moves():
            o_ref[d, i0:i0 + n, :] = x_ref[r0:r0 + n, :]

    cb = RELAYOUT_LANES
    return pl.pallas_call(body, name="w_in_rows_out", grid=(D // cb,),
                          in_specs=[pl.BlockSpec((IN_PAD, cb), lambda j: (0, j))],
                          out_specs=pl.BlockSpec((N_DEV, cs, cb), lambda j: (0, 0, j)),
                          out_shape=jax.ShapeDtypeStruct((N_DEV, cs, D), gw.dtype), compiler_params=_cparams(("parallel",)))(gw)


def _pad_in_proj_rows(w):
    rows, at = [], 0
    for src, dst, n in sorted(_in_proj_pieces(), key=lambda p: p[1]):
        if dst > at:
            rows.append(jnp.zeros((dst - at, w.shape[1]), w.dtype))
        rows.append(w[src:src + n])
        at = dst + n
    rows.append(jnp.zeros((IN_PAD - at, w.shape[1]), w.dtype))
    return jnp.concatenate(rows, axis=0)


def _unpad_in_proj_rows(wp):
    return jnp.concatenate([wp[dst:dst + n] for _, dst, n in sorted(_in_proj_pieces())], axis=0)


def _lane_pad(a, width=LANES):
    return jnp.pad(a, ((0, 0), (0, width - a.shape[1])))


def _seg_blk(h, name, rows):
    off, width = SEG[name]
    return (h, rows, width, off // width)


def _ln_both(xs_, ps_):
    (y,) = _ln_fn(xs_, ps_)
    return (y, y)


def _behind(param, hooks, stage, *seen):
    if hooks is None or stage not in hooks:
        return param
    token = hooks[stage](*seen)
    return param if token is None else param + token[0:1, 0:1]


def _layer_fwd(l, x, x_mx, W, sp, hooks=None):
    T = x.shape[0]
    n64, ngla, ntok = T // SSD_CHUNK, T // GLA_BLOCK, T // 256
    h = _mm(f"in_proj_{l}", x_mx, W["w_in"], "nt")
    xbc = _conv_silu_fwd(f"ssd_conv_{l}", h, SEG["xbc"][0], sp["ssd_conv_w"], sp["ssd_conv_b"])
    gqkv = _conv_silu_fwd(f"gdn_conv_{l}", h, SEG["gqkv"][0], sp["gdn_conv_w"], None)

    ssd_in = [(xbc, SSD_CHUNK, SSD_XBC, 0), _seg_blk(h, "dt", SSD_CHUNK), _seg_blk(h, "z", SSD_CHUNK)]
    ssd_p = [sp["ssd_dt_bias"], sp["ssd_a_log"], sp["ssd_d"], sp["ssd_norm_w"]]
    o_ssd, ssd_states = _chain_fwd(f"ssd_fwd_{l}", _ssd_chunk, n64, ssd_in, ssd_p, [(SSD_CHUNK, SSD_INNER, MXU_DTYPE)],
                                   (SSD_STATE, SSD_INNER))
    o_gdn, gdn_saved = _gdn_forward(str(l), gqkv, h, dict(sp, gdn_a_log=_behind(sp["gdn_a_log"], hooks, "ssd", o_ssd)))
    gla_in = [_seg_blk(h, "lqkv", GLA_BLOCK), _seg_blk(h, "lglr", GLA_BLOCK), _seg_blk(h, "lr", GLA_BLOCK)]
    gla_p = [jnp.pad(sp["gla_gate_w2"], ((0, LANES - GLA_RANK), (0, 0))), sp["gla_gate_b"], sp["gla_norm_w"]]
    o_gla, gla_states = _chain_fwd(f"gla_fwd_{l}", _gla_block, ngla, gla_in, gla_p, [(GLA_BLOCK, GLA_V, MXU_DTYPE)],
                                   (GLA_VAL_DIM, GLA_K))
    ln1_p = [_behind(sp["ln1_g"], hooks, "mixed", o_gdn), sp["ln1_b"]]
    y_ssd = _mm(f"br_ssd_{l}", o_ssd, W["w_br_ssd"])
    y_gdn = _mm(f"br_gdn_{l}", o_gdn, W["w_br_gdn"], "nt")
    y_gla = _mm(f"br_gla_{l}", o_gla, W["w_br_gla"], "nt")
    merge_in = [_seg_blk(h, "gates", 256), (y_ssd, 256, D_MODEL, 0), (y_gdn, 256, D_MODEL, 0), (y_gla, 256, D_MODEL, 0)]
    (mix,) = _chain_fwd(f"merge_{l}", _merge_fn, ntok, merge_in, [], [(256, D_MODEL, MXU_DTYPE)])
    r1 = _mm(f"out_proj_{l}", mix, W["w_out"])
    both = [(256, D_MODEL, F32), (256, D_MODEL, MXU_DTYPE)]
    x1, x1_mx = _chain_fwd(f"ln1_{l}", _ln_both, ntok, [(x, 256, D_MODEL, 0), (r1, 256, D_MODEL, 0)], ln1_p, both)
    up = _mm(f"ffn_up_{l}", x1_mx, W["ffn_w_up"], "nt")
    act = _ffn_glu_fwd(f"ffn_glu_{l}", up, sp["ffn_conv_w_pad"], sp["ffn_conv_b_pad"], MXU_DTYPE)
    ln2_p = [_behind(sp["ln2_g"], hooks, "ffn_act", act), sp["ln2_b"]]
    r2 = _mm(f"ffn_down_{l}", act, W["ffn_w_down"])
    x2, x2_mx = _chain_fwd(f"ln2_{l}", _ln_both, ntok, [(x1, 256, D_MODEL, 0), (r2, 256, D_MODEL, 0)], ln2_p, both)
    saved = dict(x=x, x_mx=x_mx, h=h, xbc=xbc, gqkv=gqkv, ssd_in=ssd_in, ssd_p=ssd_p, ssd_states=ssd_states,
                 gdn=gdn_saved, gla_in=gla_in, gla_p=gla_p, gla_states=gla_states, o_ssd=o_ssd,
                 o_gdn=o_gdn, o_gla=o_gla, merge_in=merge_in, mix=mix, r1=r1, ln1_p=ln1_p, x1=x1, x1_mx=x1_mx, up=up, act=act,
                 r2=r2, ln2_p=ln2_p)
    return x2, x2_mx, saved


def _layer_bwd(l, dx2, W, sp, sv, hooks=None):
    T = dx2.shape[0]
    n64, ngla, ntok = T // SSD_CHUNK, T // GLA_BLOCK, T // 256
    bf = MXU_DTYPE
    gw, gs = {}, {}
    ln2_p = [_behind(sv["ln2_p"][0], hooks, "start"), sv["ln2_p"][1]]
    (dx1_a, dr2), (gs["ln2_g"], gs["ln2_b"]) = _chain_bwd(
        f"ln2_bwd_{l}", _ln_fn, ntok, [(sv["x1"], 256, D_MODEL, 0), (sv["r2"], 256, D_MODEL, 0)], ln2_p,
        [(dx2, 256, D_MODEL)], dx_dtypes=[F32, bf])
    gw["ffn_w_down"] = _mm(f"ffn_down_dw_{l}", sv["act"], dr2, "tn")
    dact = _mm(f"ffn_down_dx_{l}", dr2, W["ffn_w_down"], "nt")
    dg, du, dwg, dwu, dbg, dbu = _ffn_glu_bwd(f"ffn_glu_bwd_{l}", sv["up"], sp["ffn_conv_w_pad"], sp["ffn_conv_b_pad"], dact, bf)
    gs["ffn_conv_w"] = _ffn_unpad_cols(jnp.concatenate([dwg, dwu], axis=1))
    gs["ffn_conv_b"] = _ffn_unpad_cols(jnp.concatenate([dbg, dbu], axis=1))
    dup = jnp.concatenate([dg, du], axis=1)
    gw["ffn_w_up"] = _mm(f"ffn_up_dw_{l}", dup, sv["x1_mx"], "tn", tn=1024)
    dx1_b = _mm(f"ffn_up_dx_{l}", dup, W["ffn_w_up"], "nn", tn=1024, tk=1024)
    ln1_p = [_behind(sv["ln1_p"][0], hooks, "ffn", dx1_b), sv["ln1_p"][1]]
    (dx_a, dr1), (gs["ln1_g"], gs["ln1_b"]) = _chain_bwd(
        f"ln1_bwd_{l}", _ln_sum_fn, ntok, [(sv["x"], 256, D_MODEL, 0), (sv["r1"], 256, D_MODEL, 0)], ln1_p,
        [(dx1_a, 256, D_MODEL), (dx1_b, 256, D_MODEL)], dx_dtypes=[F32, bf])
    gw["w_out"] = _mm(f"out_proj_dw_{l}", sv["mix"], dr1, "tn")
    dmix = _mm(f"out_proj_dx_{l}", dr1, W["w_out"], "nt")
    (dgates, dy_ssd, dy_gdn, dy_gla), _ = _chain_bwd(f"merge_bwd_{l}", _merge_fn, ntok, sv["merge_in"], [],
                                                     [(dmix, 256, D_MODEL)], dx_dtypes=[bf, bf, bf, bf])
    gw["w_br_ssd"] = _mm(f"br_ssd_dw_{l}", sv["o_ssd"], dy_ssd, "tn")
    gw["w_br_gdn"] = _mm(f"br_gdn_dw_{l}", dy_gdn, sv["o_gdn"], "tn")
    gw["w_br_gla"] = _mm(f"br_gla_dw_{l}", dy_gla, sv["o_gla"], "tn")
    do_ssd = _mm(f"br_ssd_dx_{l}", dy_ssd, W["w_br_ssd"], "nt")
    do_gdn = _mm(f"br_gdn_dx_{l}", dy_gdn, W["w_br_gdn"], "nn")
    do_gla = _mm(f"br_gla_dx_{l}", dy_gla, W["w_br_gla"], "nn")

    ssd_p = [_behind(sv["ssd_p"][0], hooks, "branches", do_gla, gw)] + list(sv["ssd_p"][1:])
    (dxbc, ddt, dz), dps = _chain_bwd(f"ssd_bwd_{l}", _ssd_chunk, n64, sv["ssd_in"], ssd_p,
                                      [(do_ssd, SSD_CHUNK, SSD_INNER)], sprev=sv["ssd_states"], dx_dtypes=[F32, bf, bf])
    gs["ssd_dt_bias"], gs["ssd_a_log"], gs["ssd_d"], gs["ssd_norm_w"] = dps
    gdn_sv = dict(sv["gdn"], scan_p=[_behind(sv["gdn"]["scan_p"][0], hooks, "ssd", dz)])
    dgqkv, dgab, dgg, gs["gdn_a_log"], gs["gdn_dt_bias"], gs["gdn_norm_w"] = _gdn_backward(str(l), do_gdn, gdn_sv, bf)
    (dlqkv, dlglr, dlr), dps = _chain_bwd(f"gla_bwd_{l}", _gla_block, ngla, sv["gla_in"], sv["gla_p"],
                                          [(do_gla, GLA_BLOCK, GLA_V)], sprev=sv["gla_states"], dx_dtypes=[bf, bf, bf])
    gs["gla_gate_w2"], gs["gla_gate_b"], gs["gla_norm_w"] = dps[0][:GLA_RANK], dps[1], dps[2]
    dxbc_pre, gs["ssd_conv_w"], gs["ssd_conv_b"] = _conv_silu_bwd(
        f"ssd_conv_bwd_{l}", sv["h"], SEG["xbc"][0], sp["ssd_conv_w"], sp["ssd_conv_b"], dxbc, bf)
    dgqkv_pre, gs["gdn_conv_w"] = _conv_silu_bwd(f"gdn_conv_bwd_{l}", sv["h"], SEG["gqkv"][0], sp["gdn_conv_w"], None, dgqkv, bf)
    pieces = dict(gates=dgates, xbc=dxbc_pre, gqkv=dgqkv_pre, z=dz, lqkv=dlqkv, gg=dgg, lr=dlr, dt=ddt, gab=dgab, lglr=dlglr)
    cols = [pieces[name] for name, _, _, _ in PAD_SEGS]
    cols.append(jnp.zeros((T, IN_PAD - PAD_SEGS[-1][1] - PAD_SEGS[-1][2]), bf))
    dh = jnp.concatenate(cols, axis=1)
    gw["w_in"] = _mm(f"in_proj_dw_{l}", dh, sv["x_mx"], "tn", tn=1024)
    behind = hooks["w_in_grad"](gw) if hooks is not None and "w_in_grad" in hooks else None
    dx_b = _mm(f"in_proj_dx_{l}", dh, W["w_in"], "nn", tm=1024, tn=1024, tk=IN_PAD // 4, after=behind)
    dx = _add_blocks(f"dx_add_{l}", dx_a[None], dx_b[None])[0]
    return dx, gw, gs


def _ln_sum_fn(xs_, ps_):
    (y,) = _ln_fn(xs_, ps_)
    return (y, y)


def _small_2d(name, a):
    return a.reshape(1, -1) if a.ndim == 1 else a


def kernel(x, w_in, ssd_conv_w, ssd_conv_b, ssd_dt_bias, ssd_a_log, ssd_d, ssd_norm_w, gdn_conv_w, gdn_a_log, gdn_dt_bias, gdn_norm_w, gla_gate_w2, gla_gate_b, gla_norm_w, w_br_ssd, w_br_gdn, w_br_gla, w_out, ln1_g, ln1_b, ffn_w_up, ffn_conv_w, ffn_conv_b, ffn_w_down, ln2_g, ln2_b, loss_target, m_w_in, m_ssd_conv_w, m_ssd_conv_b, m_ssd_dt_bias, m_ssd_a_log, m_ssd_d, m_ssd_norm_w, m_gdn_conv_w, m_gdn_a_log, m_gdn_dt_bias, m_gdn_norm_w, m_gla_gate_w2, m_gla_gate_b, m_gla_norm_w, m_w_br_ssd, m_w_br_gdn, m_w_br_gla, m_w_out, m_ln1_g, m_ln1_b, m_ffn_w_up, m_ffn_conv_w, m_ffn_conv_b, m_ffn_w_down, m_ln2_g, m_ln2_b, v_w_in, v_ssd_conv_w, v_ssd_conv_b, v_ssd_dt_bias, v_ssd_a_log, v_ssd_d, v_ssd_norm_w, v_gdn_conv_w, v_gdn_a_log, v_gdn_dt_bias, v_gdn_norm_w, v_gla_gate_w2, v_gla_gate_b, v_gla_norm_w, v_w_br_ssd, v_w_br_gdn, v_w_br_gla, v_w_out, v_ln1_g, v_ln1_b, v_ffn_w_up, v_ffn_conv_w, v_ffn_conv_b, v_ffn_w_down, v_ln2_g, v_ln2_b):
    args = locals()
    w = {n: args[n] for n in WEIGHTS}
    m = {n: args["m_" + n] for n in WEIGHTS}
    v = {n: args["v_" + n] for n in WEIGHTS}
    dev = 4 * lax.axis_index("x") + 2 * lax.axis_index("y") + lax.axis_index("c")
    xl = x[0]
    tgt = loss_target[0]

    late = BIG[1:]

    def send(names, l):
        return [_shard_to_send(n, w[n][l]) for n in names]

    def whole_weights(names, got):
        return {n: _whole_from_gathered(n, g) for n, g in zip(names, got)}

    got0 = _all_gather("gather_first", send(BIG[:1], 0) + [w[n] for n in SMALL_SHARDED])
    gather0, token0 = _all_gather_begin("w_0", send(late, 0), got0[0])
    W = [whole_weights(BIG[:1], got0[:1]), None]
    whole = dict(w)
    for n, s in zip(SMALL_SHARDED, got0[1:]):
        whole[n] = jnp.transpose(s, (1, 2, 0, 3)).reshape(s.shape[1], s.shape[2], N_DEV * s.shape[3])
    SP = [{n: _small_2d(n, whole[n][l]) for n in SMALL} for l in range(DEPTH)]
    for sp in SP:
        sp["ffn_conv_w_pad"] = _ffn_pad_cols(sp["ffn_conv_w"])
        sp["ffn_conv_b_pad"] = _ffn_pad_cols(sp["ffn_conv_b"])

    held = {}

    def late_weights_cross(o_ssd):
        token = _all_gather_middle(gather0, o_ssd)
        held["gather1"], token1 = _all_gather_begin("w_1", send(BIG, 1), o_ssd)
        return token + token1

    def late_weights_arrive(mixed):
        W[0].update(whole_weights(late, _all_gather_end(gather0, mixed)))

    fwd_hooks = {"ssd": late_weights_cross, "mixed": late_weights_arrive,
                 "ffn_act": lambda act: _all_gather_middle(held["gather1"], act)}
    saved = [None] * DEPTH
    act, act_mx, saved[0] = _layer_fwd(0, xl, (xl + token0[0, 0]).astype(MXU_DTYPE), W[0], SP[0], hooks=fwd_hooks)
    W[1] = whole_weights(BIG, _all_gather_end(held["gather1"], act))
    act, act_mx, saved[1] = _layer_fwd(1, act, act_mx, W[1], SP[1])
    dy, loss_parts = _loss_head(act, tgt)
    loss = lax.psum(jnp.sum(loss_parts), ("x", "y", "c"))

    def slots_of(names, gw):
        return [_slots_from_whole(n, gw[n]) for n in names]

    grads = {}
    GS = [None] * DEPTH
    dy, gw, GS[1] = _layer_bwd(1, dy, W[1], SP[1], saved[1])
    reduce1, reduce1_token = _reduce_scatter_begin("1", slots_of(BIG, gw))

    def late_grads_leave(seen, gw0):
        held["reduce0"], token = _reduce_scatter_begin("0", slots_of(late, gw0))
        return token

    def w_in_grad_leaves(gw0):
        held["reduce_first"], token = _reduce_scatter_begin("first", slots_of(BIG[:1], gw0))
        return token

    bwd_hooks = {"start": lambda: reduce1_token, "ffn": lambda seen: _reduce_scatter_middle(reduce1, seen),
                 "branches": late_grads_leave, "ssd": lambda seen: _reduce_scatter_middle(held["reduce0"], seen),
                 "w_in_grad": w_in_grad_leaves}
    dy, gw, GS[0] = _layer_bwd(0, dy, W[0], SP[0], saved[0], hooks=bwd_hooks)
    small_shapes = [whole[n].shape for n in SMALL]
    gs_flat = _pack([jnp.stack([GS[l][n].reshape(whole[n].shape[1:]) for l in range(DEPTH)]) for n in SMALL], F32)
    (gs_all,) = _all_gather("gather_small_grads", [gs_flat])
    first_token = _reduce_scatter_middle(held["reduce_first"], gs_all)
    red1 = _reduce_scatter_end(reduce1, dy)
    red0_late = _reduce_scatter_end(held["reduce0"], dy)
    grad_x = dy[None]
    kept_t = KEPT_TRANSPOSED
    grads_k = {n: jnp.stack([_shard_from_slot(n, red0_late[i]), _shard_from_slot(n, red1[i + 1])]) for i, n in enumerate(late)}

    def mine(n, a):
        if n in SMALL_SHARDED:
            cs = a.shape[-1] // N_DEV
            return lax.dynamic_slice_in_dim(a, dev * cs, cs, axis=a.ndim - 1)
        return a

    m_whole, v_whole = {}, {}
    for n in SMALL:
        reps = (1, 1, N_DEV) if n in SMALL_SHARDED else (1,) * m[n].ndim
        m_whole[n], v_whole[n] = jnp.tile(m[n], reps), jnp.tile(v[n], reps)
    outs = _adamw_small(gs_all, _pack([whole[n] for n in SMALL], F32) + first_token[0:1, 0:1], _pack([m_whole[n] for n in SMALL], F32),
                        _pack([v_whole[n] for n in SMALL], F32))
    g_s, d_s, m_s, v_s = [_unpack(o, small_shapes) for o in outs]
    delta, new_m, new_v = {}, {}, {}
    for i, n in enumerate(SMALL):
        grads[n], delta[n], new_m[n], new_v[n] = mine(n, g_s[i]), mine(n, d_s[i]), mine(n, m_s[i]), mine(n, v_s[i])
    for n in late + BIG[:1]:
        if n == "w_in":
            done = sum(new_v[k].reshape(-1)[0:1] for k in late + SMALL[:1])
            (first0,) = _reduce_scatter_end(held["reduce_first"], done)
            grads_k[n] = jnp.stack([first0, red1[0]])
        view = (lambda a: jnp.transpose(a, (0, 2, 1))) if n in kept_t else (lambda a: a)
        outs = _adamw(f"adamw_{n}", view(w[n]), grads_k[n], view(m[n]), view(v[n]), after=None if n == "w_in" else first_token)
        grads[n], delta[n], new_m[n], new_v[n] = view(grads_k[n]), view(outs[0]), view(outs[1]), view(outs[2])

    return (loss, grad_x, *[grads[n] for n in WEIGHTS], *[delta[n] for n in WEIGHTS], *[new_m[n] for n in WEIGHTS],
            *[new_v[n] for n in WEIGHTS])
```

```python
import functools
import math

import jax
import jax.numpy as jnp
from jax import lax
from jax.experimental import pallas as pl
from jax.experimental.pallas import tpu as pltpu

F32 = jnp.float32
MXU_DTYPE = jnp.bfloat16
HI = lax.Precision.HIGHEST

N_DEV = 8
D_MODEL = 1024
DEPTH = 2
SSD_HEADS, SSD_HEAD_DIM, SSD_INNER, SSD_GROUPS, SSD_STATE, SSD_CHUNK = 16, 64, 1024, 2, 128, 64
SSD_XBC = SSD_INNER + 2 * SSD_GROUPS * SSD_STATE
GDN_HEADS, GDN_HEAD_DIM, GDN_WIDTH, GDN_CHUNK = 4, 128, 512, 64
GLA_HEADS, GLA_KEY_DIM, GLA_VAL_DIM, GLA_K, GLA_V, GLA_RANK, GLA_CHUNK = 4, 64, 128, 256, 512, 16, 16
GLA_BLOCK = 128
GLA_NORMALIZER = 16.0
FFN_DIM = 2816
FFN_HALF = FFN_DIM // 8
FFN_HALF_PAD = 384
FFN_UP_PAD = 16 * FFN_HALF_PAD
FFN_PAD = FFN_UP_PAD // 2
ALPHA = (2 * DEPTH) ** 0.25
LN_EPS = 1e-5
RMS_EPS = 1e-6
ADAM_LR, ADAM_B1, ADAM_B2, ADAM_EPS, ADAM_WD, ADAM_STEP = 0.001, 0.9, 0.999, 1e-08, 0.01, 10
LANES = 128
NEG_BIG = -1e30
VMEM_LIMIT = 56 * 1024 * 1024

IN_SPLITS = (("z", 1024), ("xbc", 1536), ("dt", 16), ("gqkv", 1536), ("ga", 4), ("gb", 4), ("gg", 512),
             ("lqkv", 1024), ("lglr", 16), ("lr", 512), ("gates", 3072))
IN_DIM = sum(w for _, w in IN_SPLITS)
PAD_SEGS = (("gates", 0, 3072, (("gates", 0),)), ("xbc", 3072, 1536, (("xbc", 0),)),
            ("gqkv", 4608, 1536, (("gqkv", 0),)), ("z", 6144, 1024, (("z", 0),)),
            ("lqkv", 7168, 1024, (("lqkv", 0),)), ("gg", 8192, 512, (("gg", 0),)), ("lr", 8704, 512, (("lr", 0),)),
            ("dt", 9216, 128, (("dt", 0),)), ("gab", 9344, 128, (("ga", 0), ("gb", 4))), ("lglr", 9472, 128, (("lglr", 0),)))
IN_PAD = 9728
SEG = {name: (off, width) for name, off, width, _ in PAD_SEGS}

BIG = ("w_in", "w_br_ssd", "w_br_gdn", "w_br_gla", "w_out", "ffn_w_up", "ffn_w_down")
COL_SHARDED = ("w_in", "w_br_gdn", "w_br_gla", "ffn_w_up")
SMALL_SHARDED = ("ssd_conv_w", "gdn_conv_w", "gla_gate_w2", "ffn_conv_w")
WEIGHTS = ("w_in", "ssd_conv_w", "ssd_conv_b", "ssd_dt_bias", "ssd_a_log", "ssd_d", "ssd_norm_w", "gdn_conv_w",
           "gdn_a_log", "gdn_dt_bias", "gdn_norm_w", "gla_gate_w2", "gla_gate_b", "gla_norm_w", "w_br_ssd", "w_br_gdn",
           "w_br_gla", "w_out", "ln1_g", "ln1_b", "ffn_w_up", "ffn_conv_w", "ffn_conv_b", "ffn_w_down", "ln2_g", "ln2_b")
SMALL = tuple(n for n in WEIGHTS if n not in BIG)
FLAT_W = 512


def _cparams(sem=None):
    kw = dict(vmem_limit_bytes=VMEM_LIMIT)
    if sem is not None:
        kw["dimension_semantics"] = sem
    return pltpu.CompilerParams(**kw)


_DIMS = {"nn": (((1,), (0,)), ((), ())), "nt": (((1,), (1,)), ((), ())), "tn": (((0,), (0,)), ((), ()))}


def _dot(a, b, dims="nn"):
    if MXU_DTYPE == F32:
        return lax.dot_general(a.astype(F32), b.astype(F32), _DIMS[dims], precision=HI, preferred_element_type=F32)
    return lax.dot_general(a.astype(MXU_DTYPE), b.astype(MXU_DTYPE), _DIMS[dims], preferred_element_type=F32)


def _dot_hi(a, b, dims="nn"):
    return lax.dot_general(a.astype(F32), b.astype(F32), _DIMS[dims], precision=HI, preferred_element_type=F32)


def _iota2(shape, axis):
    return lax.broadcasted_iota(jnp.int32, shape, axis)


def _tril(n, strict=False):
    r, c = _iota2((n, n), 0), _iota2((n, n), 1)
    return (r > c) if strict else (r >= c)


def _raw_dot(a, b, dims):
    return lax.dot_general(a, b, _DIMS[dims], preferred_element_type=F32)


def _dot_x3(a, b, dims="nn"):
    if MXU_DTYPE == F32:
        return _dot_hi(a, b, dims)
    ah, bh = a.astype(jnp.bfloat16), b.astype(jnp.bfloat16)
    al, bl = (a - ah.astype(F32)).astype(jnp.bfloat16), (b - bh.astype(F32)).astype(jnp.bfloat16)
    return _raw_dot(ah, bh, dims) + (_raw_dot(ah, bl, dims) + _raw_dot(al, bh, dims))


def _exact_dot(mask, b, dims, mask_first):
    if MXU_DTYPE == F32:
        return _dot_hi(mask, b, dims) if mask_first else _dot_hi(b, mask, dims)
    m = mask.astype(jnp.bfloat16)
    b1 = b.astype(jnp.bfloat16)
    r1 = b - b1.astype(F32)
    b2 = r1.astype(jnp.bfloat16)
    b3 = (r1 - b2.astype(F32)).astype(jnp.bfloat16)
    if mask_first:
        return _raw_dot(m, b1, dims) + (_raw_dot(m, b2, dims) + _raw_dot(m, b3, dims))
    return _raw_dot(b1, m, dims) + (_raw_dot(b2, m, dims) + _raw_dot(b3, m, dims))


@jax.custom_vjp
def _mask_left(mask, b):
    return _exact_dot(mask, b, "nn", True)


_mask_left.defvjp(lambda mask, b: (_mask_left(mask, b), mask),
                  lambda mask, d: (jnp.zeros_like(mask), _exact_dot(mask, d, "tn", True)))


@jax.custom_vjp
def _mask_right(a, mask):
    return _exact_dot(mask, a, "nn", False)


_mask_right.defvjp(lambda a, mask: (_mask_right(a, mask), mask),
                   lambda mask, d: (_exact_dot(mask, d, "nt", False), jnp.zeros_like(mask)))


@jax.custom_vjp
def _unit_lower_inverses(mats):
    n = mats[0].shape[0]
    eye = (_iota2((n, n), 0) == _iota2((n, n), 1)).astype(F32)
    xs = [eye - a for a in mats]
    ps = list(mats)
    k = 2
    while k < n:
        ps = [_dot_x3(p, p) for p in ps]
        xs = [x + _dot_x3(x, p) for x, p in zip(xs, ps)]
        k *= 2
    return xs


def _unit_lower_inverses_fwd(mats):
    ts = _unit_lower_inverses(mats)
    return ts, ts


def _unit_lower_inverses_bwd(ts, dts):
    mids = [_dot_x3(t, d, "tn") for t, d in zip(ts, dts)]
    return ([-_dot_x3(m, t, "nt") for m, t in zip(mids, ts)],)


_unit_lower_inverses.defvjp(_unit_lower_inverses_fwd, _unit_lower_inverses_bwd)


def _ssd_chunk(xs_, ps_, s_t):
    xbc, dtraw, z = xs_
    dt_bias, a_log, d_skip, norm_w = ps_
    L = xbc.shape[0]
    H, P, N, G = SSD_HEADS, SSD_HEAD_DIM, SSD_STATE, SSD_GROUPS
    W = SSD_INNER // G
    xs = xbc[:, :SSD_INNER]
    bm = xbc[:, SSD_INNER:SSD_INNER + G * N]
    cm = xbc[:, SSD_INNER + G * N:]
    dt = jax.nn.softplus(dtraw[:, :H] + dt_bias)
    a = dt * (-jnp.exp(a_log))
    causal = _tril(L)
    a_cs = _mask_left(causal.astype(F32), a)
    expand = (_iota2((H, SSD_INNER), 1) // P == _iota2((H, SSD_INNER), 0)).astype(F32)
    wide = _mask_right(jnp.concatenate([a_cs, dt, jnp.broadcast_to(d_skip, (L, H))], axis=0), expand)
    a_cs_x, dt_x, d_x = wide[:L], wide[L:2 * L], wide[2 * L:]
    a_end_x = a_cs_x[L - 1:L, :]
    a_cs_t, dt_t = a_cs.T, dt.T
    cb = [_dot(cm[:, g * N:(g + 1) * N], bm[:, g * N:(g + 1) * N], "nt") for g in range(G)]
    cb2 = [jnp.concatenate([c, c], axis=1) for c in cb]
    lane2 = _iota2((L, 2 * L), 1)
    left = lane2 < L
    causal2 = _iota2((L, 2 * L), 0) >= jnp.where(left, lane2, lane2 - L)
    pairs = range(0, H, 2)
    col2 = [jnp.where(left, a_cs[:, h:h + 1], a_cs[:, h + 1:h + 2]) for h in pairs]
    row2 = [jnp.concatenate([a_cs_t[h:h + 1, :], a_cs_t[h + 1:h + 2, :]], axis=1) for h in pairs]
    dt2 = [jnp.concatenate([dt_t[h:h + 1, :], dt_t[h + 1:h + 2, :]], axis=1) for h in pairs]
    ws2 = [cb2[h // (H // G)] * (jnp.exp(jnp.where(causal2, col2[i] - row2[i], NEG_BIG)) * dt2[i]) for i, h in enumerate(pairs)]
    first = _iota2((L, 2 * P), 1) < P
    ys = []
    for i, h in enumerate(pairs):
        x2 = xs[:, h * P:(h + 2) * P]
        ys.append(_dot(ws2[i], jnp.concatenate([jnp.where(first, x2, 0.0), jnp.where(first, 0.0, x2)], axis=0)))
    y = jnp.concatenate(ys, axis=1)
    y_in = jnp.concatenate([_dot(cm[:, g * N:(g + 1) * N], s_t[:, g * W:(g + 1) * W]) for g in range(G)], axis=1)
    y = y + y_in * jnp.exp(a_cs_x) + d_x * xs
    xw = xs * (jnp.exp(a_end_x - a_cs_x) * dt_x)
    st = jnp.concatenate([_dot(bm[:, g * N:(g + 1) * N], xw[:, g * W:(g + 1) * W], "tn") for g in range(G)], axis=1)
    s_new = s_t * jnp.exp(a_end_x) + st
    yg = y * jax.nn.silu(z)
    outs = []
    for g in range(G):
        part = yg[:, g * W:(g + 1) * W]
        outs.append(part * lax.rsqrt(jnp.mean(part * part, axis=1, keepdims=True) + RMS_EPS))
    return (jnp.concatenate(outs, axis=1) * norm_w,), s_new


GDN_PREP_CHUNKS = 4


def _gdn_prep(xs_, ps_):
    qkv, ab = xs_
    a_log, dt_bias = ps_
    B = qkv.shape[0]
    H, D, L = GDN_HEADS, GDN_HEAD_DIM, GDN_CHUNK
    g_all = -jnp.exp(a_log) * jax.nn.softplus(ab + dt_bias)
    row, col = _iota2((B, B), 0), _iota2((B, B), 1)
    g_cs = _mask_left((((row // L) == (col // L)) & (row >= col)).astype(F32), g_all)
    g_cs_t = g_cs.T
    beta_all = jax.nn.sigmoid(ab)
    incl, strict = _tril(L), _tril(L, strict=True)
    qs, ks, vs = [], [], []
    for h in range(H):
        q = qkv[:, h * D:(h + 1) * D]
        k = qkv[:, GDN_WIDTH + h * D:GDN_WIDTH + (h + 1) * D]
        qs.append(q * lax.rsqrt(jnp.sum(q * q, axis=1, keepdims=True) + RMS_EPS) * (D ** -0.5))
        ks.append(k * lax.rsqrt(jnp.sum(k * k, axis=1, keepdims=True) + RMS_EPS))
        vs.append(qkv[:, 2 * GDN_WIDTH + h * D:2 * GDN_WIDTH + (h + 1) * D])
    pairs = [(c, h) for c in range(B // L) for h in range(H)]
    rows = {c: slice(c * L, (c + 1) * L) for c in range(B // L)}
    q_ = {(c, h): qs[h][rows[c]] for c, h in pairs}
    k_ = {(c, h): ks[h][rows[c]] for c, h in pairs}
    col_ = {(c, h): g_cs[rows[c], h:h + 1] for c, h in pairs}
    beta_ = {(c, h): beta_all[rows[c], H + h:H + h + 1] for c, h in pairs}
    gamma = {p: jnp.exp(jnp.where(incl, col_[p] - g_cs_t[p[1]:p[1] + 1, rows[p[0]]], NEG_BIG)) for p in pairs}
    kb = {p: k_[p] * beta_[p] for p in pairs}
    a_mat = [jnp.where(strict, _dot(kb[p], k_[p], "nt") * gamma[p], 0.0) for p in pairs]
    attn = {p: jnp.where(incl, _dot(q_[p], k_[p], "nt") * gamma[p], 0.0) for p in pairs}
    t_mat = dict(zip(pairs, _unit_lower_inverses(a_mat)))
    u = {p: _dot(t_mat[p], vs[p[1]][rows[p[0]]] * beta_[p]) for p in pairs}
    w = {p: _dot(t_mat[p], kb[p] * jnp.exp(col_[p])) for p in pairs}
    qd = {p: q_[p] * jnp.exp(col_[p]) for p in pairs}
    kd = {p: k_[p] * jnp.exp(col_[p][L - 1:L, :] - col_[p]) for p in pairs}

    def whole(parts):
        return jnp.concatenate([jnp.concatenate([parts[(c, h)] for h in range(H)], axis=1) for c in range(B // L)], axis=0)

    return (whole(u), whole(w), whole(qd), whole(kd), whole(attn), g_cs)


def _gdn_scan(xs_, ps_, s):
    u, w, qd, kd, attn, g_cs, gate = xs_
    (norm_w,) = ps_
    L = u.shape[0]
    H, D = GDN_HEADS, GDN_HEAD_DIM
    heads = range(H)
    lanes = [slice(h * D, (h + 1) * D) for h in heads]
    s_h = [s[lanes[h], :] for h in heads]
    v_new = [u[:, lanes[h]] - _dot(w[:, lanes[h]], s_h[h]) for h in heads]
    o = [_dot(qd[:, lanes[h]], s_h[h]) + _dot(attn[:, h * L:(h + 1) * L], v_new[h]) for h in heads]
    decay = [jnp.exp(g_cs[L - 1:L, h:h + 1]) for h in heads]
    s_new = [s_h[h] * decay[h] + _dot(kd[:, lanes[h]], v_new[h], "tn") for h in heads]
    o = [o[h] * lax.rsqrt(jnp.mean(o[h] * o[h], axis=1, keepdims=True) + RMS_EPS) * norm_w * jax.nn.silu(gate[:, lanes[h]])
         for h in heads]
    return (jnp.concatenate(o, axis=1),), jnp.concatenate(s_new, axis=0)


def _gdn_forward(tag, gqkv, h, sp):
    T = gqkv.shape[0]
    blk = GDN_PREP_CHUNKS * GDN_CHUNK
    prep_in = [(gqkv, blk, 3 * GDN_WIDTH, 0), _seg_blk(h, "gab", blk)]
    prep_p = [_lane_pad(sp["gdn_a_log"]), _lane_pad(sp["gdn_dt_bias"])]
    mx = MXU_DTYPE
    prep = _chain_fwd(f"gdn_prep_{tag}", _gdn_prep, T // blk, prep_in, prep_p,
                      [(blk, GDN_WIDTH, F32), (blk, GDN_WIDTH, mx), (blk, GDN_WIDTH, mx), (blk, GDN_WIDTH, mx),
                       (blk, GDN_HEADS * GDN_CHUNK, mx), (blk, LANES, F32)])
    widths = [GDN_WIDTH] * 4 + [GDN_HEADS * GDN_CHUNK, LANES]
    scan_in = [(a, GDN_CHUNK, wd, 0) for a, wd in zip(prep, widths)] + [_seg_blk(h, "gg", GDN_CHUNK)]
    scan_p = [sp["gdn_norm_w"]]
    o, states = _chain_fwd(f"gdn_scan_{tag}", _gdn_scan, T // GDN_CHUNK, scan_in, scan_p, [(GDN_CHUNK, GDN_WIDTH, mx)],
                           (GDN_WIDTH, GDN_HEAD_DIM))
    return o, dict(prep_in=prep_in, prep_p=prep_p, scan_in=scan_in, scan_p=scan_p, states=states, widths=widths)


def _gdn_backward(tag, do, sv, dx_dtype):
    T = do.shape[0]
    blk = GDN_PREP_CHUNKS * GDN_CHUNK
    dscan, (dnorm,) = _chain_bwd(f"gdn_scan_bwd_{tag}", _gdn_scan, T // GDN_CHUNK, sv["scan_in"], sv["scan_p"],
                                 [(do, GDN_CHUNK, GDN_WIDTH)], sprev=sv["states"], dx_dtypes=[F32] * 6 + [dx_dtype])
    douts = [(d, blk, wd) for d, wd in zip(dscan[:6], sv["widths"])]
    (dgqkv, dgab), (da_log, ddt_bias) = _chain_bwd(f"gdn_prep_bwd_{tag}", _gdn_prep, T // blk, sv["prep_in"], sv["prep_p"],
                                                   douts, dx_dtypes=[F32, dx_dtype])
    return dgqkv, dgab, dscan[6], da_log[:, :GDN_HEADS], ddt_bias[:, :GDN_HEADS], dnorm


def _gla_block(xs_, ps_, s_t):
    qkv, glr, r = xs_
    w2, gate_b, norm_w = ps_
    B = qkv.shape[0]
    H, K, V, C = GLA_HEADS, GLA_KEY_DIM, GLA_VAL_DIM, GLA_CHUNK
    q = qkv[:, :GLA_K] * (K ** -0.5)
    k = qkv[:, GLA_K:2 * GLA_K]
    v = qkv[:, 2 * GLA_K:]
    gk = jax.nn.log_sigmoid(_dot(glr, w2) + gate_b) / GLA_NORMALIZER
    row, col = _iota2((B, B), 0), _iota2((B, B), 1)
    same = (row // C) == (col // C)
    mask = same & (row >= col)
    b_cs = _mask_left(mask.astype(F32), gk)
    b_end = _mask_left((col == (row // C) * C + (C - 1)).astype(F32), b_cs)
    q_e = q * jnp.exp(b_cs)
    k_e = k * jnp.exp(-b_cs)
    k_d = k * jnp.exp(b_end - b_cs)
    intra = []
    for h in range(H):
        a_mat = jnp.where(mask, _dot(q_e[:, h * K:(h + 1) * K], k_e[:, h * K:(h + 1) * K], "nt"), 0.0)
        intra.append(_dot(a_mat, v[:, h * V:(h + 1) * V]))
    o = jnp.concatenate(intra, axis=1)
    chunks = [slice(j * C, (j + 1) * C) for j in range(B // C)]
    fresh = [jnp.concatenate([_dot(v[sl, h * V:(h + 1) * V], k_d[sl, h * K:(h + 1) * K], "tn") for h in range(H)], axis=1)
             for sl in chunks]
    entering = []
    for j, sl in enumerate(chunks):
        entering.append(s_t)
        s_t = s_t * jnp.exp(b_end[j * C:j * C + 1, :]) + fresh[j]
    inter = [jnp.concatenate([_dot(q_e[sl, h * K:(h + 1) * K], entering[j][:, h * K:(h + 1) * K], "nt") for h in range(H)],
                             axis=1) for j, sl in enumerate(chunks)]
    o = o + jnp.concatenate(inter, axis=0)
    outs = []
    for h in range(H):
        oh = o[:, h * V:(h + 1) * V]
        oh = oh * lax.rsqrt(jnp.mean(oh * oh, axis=1, keepdims=True) + RMS_EPS) * norm_w
        outs.append(oh * jax.nn.silu(r[:, h * V:(h + 1) * V]))
    return (jnp.concatenate(outs, axis=1),), s_t


def _merge_fn(xs_, ps_):
    gates, y_ssd, y_gdn, y_gla = xs_
    d = D_MODEL
    return (jax.nn.sigmoid(gates[:, :d]) * y_ssd + jax.nn.sigmoid(gates[:, d:2 * d]) * y_gdn
            + jax.nn.sigmoid(gates[:, 2 * d:]) * y_gla,)


def _ln_fn(xs_, ps_):
    x, r = xs_
    g, b = ps_
    t = ALPHA * x + r
    mu = jnp.mean(t, axis=1, keepdims=True)
    var = jnp.mean(jnp.square(t - mu), axis=1, keepdims=True)
    return ((t - mu) * lax.rsqrt(var + LN_EPS) * g + b,)


def _row_spec(rows, width, colblk, n, reverse):
    if reverse:
        return pl.BlockSpec((rows, width), lambda c: (n - 1 - c, colblk))
    return pl.BlockSpec((rows, width), lambda c: (c, colblk))


def _full_spec(shape):
    zeros = (0,) * len(shape)
    return pl.BlockSpec(shape, lambda c: zeros)


def _chain_fwd(name, fn, n, blocked, full, out_defs, state_shape=None):
    nb, nf, no = len(blocked), len(full), len(out_defs)

    def body(*refs):
        xs = [r[...].astype(F32) for r in refs[:nb]]
        ps = [r[...] for r in refs[nb:nb + nf]]
        o_refs = refs[nb + nf:nb + nf + no]
        if state_shape is None:
            outs = fn(xs, ps)
        else:
            sprev_ref, s_ref = refs[nb + nf + no:]

            @pl.when(pl.program_id(0) == 0)
            def _():
                s_ref[...] = jnp.zeros_like(s_ref)

            s = s_ref[...]
            sprev_ref[0] = s
            outs, s_new = fn(xs, ps, s)
            s_ref[...] = s_new
        for r, o in zip(o_refs, outs):
            r[...] = o.astype(r.dtype)

    in_specs = [_row_spec(rows, width, cb, n, False) for _, rows, width, cb in blocked]
    in_specs += [_full_spec(a.shape) for a in full]
    out_specs = [_row_spec(rows, width, 0, n, False) for rows, width, _ in out_defs]
    out_shape = [jax.ShapeDtypeStruct((n * rows, width), dt) for rows, width, dt in out_defs]
    scratch = []
    if state_shape is not None:
        out_specs.append(pl.BlockSpec((1,) + state_shape, lambda c: (c, 0, 0)))
        out_shape.append(jax.ShapeDtypeStruct((n,) + state_shape, F32))
        scratch.append(pltpu.VMEM(state_shape, F32))
    return pl.pallas_call(body, name=name, grid=(n,), in_specs=in_specs, out_specs=out_specs, out_shape=out_shape,
                          scratch_shapes=scratch, compiler_params=_cparams(("arbitrary",)))(
        *[a for a, _, _, _ in blocked], *full)


def _chain_bwd(name, fn, n, blocked, full, douts, sprev=None, dx_dtypes=None):
    nb, nf, nd = len(blocked), len(full), len(douts)
    has_state = sprev is not None
    dx_dtypes = dx_dtypes or [F32] * nb

    def body(*refs):
        pos = 0
        b_refs = refs[pos:pos + nb]; pos += nb
        f_refs = refs[pos:pos + nf]; pos += nf
        d_refs = refs[pos:pos + nd]; pos += nd
        if has_state:
            sprev_ref = refs[pos]; pos += 1
        dx_refs = refs[pos:pos + nb]; pos += nb
        dp_refs = refs[pos:pos + nf]; pos += nf
        if has_state:
            ds_ref = refs[pos]

        @pl.when(pl.program_id(0) == 0)
        def _():
            for r in dp_refs:
                r[...] = jnp.zeros_like(r)
            if has_state:
                ds_ref[...] = jnp.zeros_like(ds_ref)

        xs = [r[...].astype(F32) for r in b_refs]
        ps = [r[...] for r in f_refs]
        dys = tuple(r[...].astype(F32) for r in d_refs)
        if has_state:
            _, vjp = jax.vjp(fn, xs, ps, sprev_ref[0])
            dxs, dps, ds = vjp((dys, ds_ref[...]))
            ds_ref[...] = ds
        else:
            _, vjp = jax.vjp(fn, xs, ps)
            dxs, dps = vjp(dys)
        for r, d in zip(dx_refs, dxs):
            r[...] = d.astype(r.dtype)
        for r, d in zip(dp_refs, dps):
            r[...] += d

    in_specs = [_row_spec(rows, width, cb, n, True) for _, rows, width, cb in blocked]
    in_specs += [_full_spec(a.shape) for a in full]
    in_specs += [_row_spec(rows, width, 0, n, True) for _, rows, width in douts]
    args = [a for a, _, _, _ in blocked] + list(full) + [a for a, _, _ in douts]
    scratch = []
    if has_state:
        st_shape = sprev.shape[1:]
        in_specs.append(pl.BlockSpec((1,) + st_shape, lambda c: (n - 1 - c, 0, 0)))
        args.append(sprev)
        scratch.append(pltpu.VMEM(st_shape, F32))
    out_specs = [_row_spec(rows, width, 0, n, True) for _, rows, width, _ in blocked]
    out_specs += [_full_spec(a.shape) for a in full]
    out_shape = [jax.ShapeDtypeStruct((n * rows, width), dt) for (_, rows, width, _), dt in zip(blocked, dx_dtypes)]
    out_shape += [jax.ShapeDtypeStruct(a.shape, F32) for a in full]
    res = pl.pallas_call(body, name=name, grid=(n,), in_specs=in_specs, out_specs=out_specs, out_shape=out_shape,
                         scratch_shapes=scratch, compiler_params=_cparams(("arbitrary",)))(*args)
    return res[:nb], res[nb:]


def _tile(n, target, unit):
    if n <= target:
        return n
    best = None
    for t in range(unit, target + 1, unit):
        if n % t == 0:
            best = t
    assert best is not None, (n, target, unit)
    return best


def _mm(name, a, b, dims="nn", out_dtype=F32, tm=2048, tn=512, tk=2048, after=None):
    if dims == "nn":
        (M, K), (_, N) = a.shape, b.shape
    elif dims == "nt":
        (M, K), (N, _) = a.shape, b.shape
    else:
        (K, M), (_, N) = a.shape, b.shape
    tm, tn, tk = _tile(M, tm, LANES), _tile(N, tn, LANES), _tile(K, tk, LANES)
    nk = K // tk
    extra = [] if after is None else [after]

    def body(*refs):
        a_ref, b_ref = refs[:2]
        o_ref, acc_ref = refs[-2:]
        part = _dot(a_ref[...], b_ref[...], dims)
        if nk == 1:
            o_ref[...] = part.astype(o_ref.dtype)
            return
        k = pl.program_id(2)

        @pl.when(k == 0)
        def _():
            acc_ref[...] = part

        @pl.when((k > 0) & (k < nk - 1))
        def _():
            acc_ref[...] += part

        @pl.when(k == nk - 1)
        def _():
            o_ref[...] = (acc_ref[...] + part).astype(o_ref.dtype)

    if dims == "tn":
        a_spec = pl.BlockSpec((tk, tm), lambda j, i, k: (k, i))
    else:
        a_spec = pl.BlockSpec((tm, tk), lambda j, i, k: (i, k))
    if dims == "nt":
        b_spec = pl.BlockSpec((tn, tk), lambda j, i, k: (j, k))
    else:
        b_spec = pl.BlockSpec((tk, tn), lambda j, i, k: (k, j))
    return pl.pallas_call(
        body, name=name, grid=(N // tn, M // tm, nk), in_specs=[a_spec, b_spec] + [ANY] * len(extra),
        out_specs=pl.BlockSpec((tm, tn), lambda j, i, k: (i, j)), out_shape=jax.ShapeDtypeStruct((M, N), out_dtype),
        scratch_shapes=[pltpu.VMEM((tm, tn) if nk > 1 else (8, LANES), F32)],
        compiler_params=_cparams(("parallel", "parallel", "arbitrary")))(a, b, *extra)


CONV_CB = 256


def _shift_down(x, k):
    if k == 0:
        return x
    return jnp.where(_iota2(x.shape, 0) >= k, pltpu.roll(x, k, 0), 0.0)


def _shift_up(x, k):
    if k == 0:
        return x
    t = x.shape[0]
    return jnp.where(_iota2(x.shape, 0) < t - k, pltpu.roll(x, t - k, 0), 0.0)


def _conv_pre(x, w, b):
    kk = w.shape[0]
    pre = x * w[kk - 1:kk, :]
    for k in range(kk - 1):
        pre = pre + _shift_down(x, kk - 1 - k) * w[k:k + 1, :]
    return pre if b is None else pre + b


EDGE = 16


def _conv_pre_rot(x, w, b):
    kk = w.shape[0]
    pre = x * w[kk - 1:kk, :]
    for k in range(kk - 1):
        pre = pre + pltpu.roll(x, kk - 1 - k, 0) * w[k:k + 1, :]
    return pre if b is None else pre + b


def _conv_t_local(d, w):
    kk = w.shape[0]
    out = d * w[kk - 1:kk, :]
    for k in range(kk - 1):
        out = out + _shift_up(d, kk - 1 - k) * w[k:k + 1, :]
    return out


def _col_sum(a):
    return jnp.sum(a, axis=0, keepdims=True)


def _conv_bwd_rot(x_ref, w, dpre, dpre_head, dx_ref, dw_ref, db_ref):
    T = dpre.shape[0]
    kk = w.shape[0]
    x = x_ref[...]
    x_head, x_tail = x_ref[0:EDGE, :], x_ref[T - EDGE:T, :]
    wrong_head = dpre[0:EDGE]
    dx = dpre * w[kk - 1:kk, :]
    for k in range(kk - 1):
        dx = dx + pltpu.roll(dpre, T - (kk - 1 - k), 0) * w[k:k + 1, :]
    dx_ref[...] = dx.astype(dx_ref.dtype)
    top = jnp.concatenate([dpre_head, dpre[EDGE:2 * EDGE]], axis=0)
    dx_ref[0:EDGE, :] = _conv_t_local(top, w)[0:EDGE].astype(dx_ref.dtype)
    dx_ref[T - EDGE:T, :] = _conv_t_local(dpre[T - EDGE:T], w).astype(dx_ref.dtype)
    ends = jnp.concatenate([x_tail, x_head], axis=0)
    dw_ref[kk - 1:kk, :] = _col_sum(dpre * x) + _col_sum((dpre_head - wrong_head) * x_head)
    for k in range(kk - 1):
        s = kk - 1 - k
        rotated_head = pltpu.roll(ends, s, 0)[EDGE:2 * EDGE]
        dw_ref[k:k + 1, :] = (_col_sum(dpre * pltpu.roll(x, s, 0)) - _col_sum(wrong_head * rotated_head)
                              + _col_sum(dpre_head * _shift_down(x_head, s)))
    if db_ref is not None:
        db_ref[...] = _col_sum(dpre) + _col_sum(dpre_head - wrong_head)


def _dsilu(pre):
    sg = jax.nn.sigmoid(pre)
    return sg * (1.0 + pre * (1.0 - sg))


def _conv_silu_fwd(name, src, col0, w, b):
    T = src.shape[0]
    kk, C = w.shape
    cb = CONV_CB
    off = col0 // cb

    def body(*refs):
        x_ref, w_ref, o_ref = refs[0], refs[1], refs[-1]
        b_val = refs[2][...] if b is not None else None
        o_ref[...] = jax.nn.silu(_conv_pre_rot(x_ref[...], w_ref[...], b_val))
        o_ref[0:EDGE, :] = jax.nn.silu(_conv_pre(x_ref[0:EDGE, :], w_ref[...], b_val))

    in_specs = [pl.BlockSpec((T, cb), lambda j: (0, off + j)), pl.BlockSpec((kk, cb), lambda j: (0, j))]
    args = [src, w]
    if b is not None:
        in_specs.append(pl.BlockSpec((1, cb), lambda j: (0, j)))
        args.append(b)
    return pl.pallas_call(body, name=name, grid=(C // cb,), in_specs=in_specs,
                          out_specs=pl.BlockSpec((T, cb), lambda j: (0, j)), out_shape=jax.ShapeDtypeStruct((T, C), F32),
                          compiler_params=_cparams(("parallel",)))(*args)


def _conv_silu_bwd(name, src, col0, w, b, dy, dx_dtype):
    T = src.shape[0]
    kk, C = w.shape
    cb = CONV_CB
    off = col0 // cb
    has_b = b is not None

    def body(*refs):
        x_ref, w_ref = refs[:2]
        pos = 2
        b_val = None
        if has_b:
            b_val = refs[pos][...]; pos += 1
        dy_ref = refs[pos]; pos += 1
        dx_ref, dw_ref = refs[pos], refs[pos + 1]
        db_ref = refs[pos + 2] if has_b else None
        wv = w_ref[...]
        dpre = dy_ref[...] * _dsilu(_conv_pre_rot(x_ref[...], wv, b_val))
        dpre_head = dy_ref[0:EDGE, :] * _dsilu(_conv_pre(x_ref[0:EDGE, :], wv, b_val))
        _conv_bwd_rot(x_ref, wv, dpre, dpre_head, dx_ref, dw_ref, db_ref)

    in_specs = [pl.BlockSpec((T, cb), lambda j: (0, off + j)), pl.BlockSpec((kk, cb), lambda j: (0, j))]
    args = [src, w]
    if has_b:
        in_specs.append(pl.BlockSpec((1, cb), lambda j: (0, j)))
        args.append(b)
    in_specs.append(pl.BlockSpec((T, cb), lambda j: (0, j)))
    args.append(dy)
    out_specs = [pl.BlockSpec((T, cb), lambda j: (0, j)), pl.BlockSpec((kk, cb), lambda j: (0, j))]
    out_shape = [jax.ShapeDtypeStruct((T, C), dx_dtype), jax.ShapeDtypeStruct((kk, C), F32)]
    if has_b:
        out_specs.append(pl.BlockSpec((1, cb), lambda j: (0, j)))
        out_shape.append(jax.ShapeDtypeStruct((1, C), F32))
    return pl.pallas_call(body, name=name, grid=(C // cb,), in_specs=in_specs, out_specs=out_specs, out_shape=out_shape,
                          compiler_params=_cparams(("parallel",)))(*args)


def _ffn_glu_fwd(name, up, w, b, out_dtype=F32):
    T = up.shape[0]
    kk = w.shape[0]
    cb = CONV_CB
    width = up.shape[1] // 2
    nblk = width // cb

    def body(g_ref, u_ref, wg_ref, wu_ref, bg_ref, bu_ref, o_ref):
        g = _conv_pre_rot(g_ref[...], wg_ref[...], bg_ref[...])
        u = _conv_pre_rot(u_ref[...], wu_ref[...], bu_ref[...])
        o_ref[...] = (jax.nn.silu(g) * u).astype(o_ref.dtype)
        g = _conv_pre(g_ref[0:EDGE, :], wg_ref[...], bg_ref[...])
        u = _conv_pre(u_ref[0:EDGE, :], wu_ref[...], bu_ref[...])
        o_ref[0:EDGE, :] = (jax.nn.silu(g) * u).astype(o_ref.dtype)

    lo, hi = (lambda j: (0, j)), (lambda j: (0, nblk + j))
    in_specs = [pl.BlockSpec((T, cb), lo), pl.BlockSpec((T, cb), hi), pl.BlockSpec((kk, cb), lo), pl.BlockSpec((kk, cb), hi),
                pl.BlockSpec((1, cb), lo), pl.BlockSpec((1, cb), hi)]
    return pl.pallas_call(body, name=name, grid=(nblk,), in_specs=in_specs, out_specs=pl.BlockSpec((T, cb), lo),
                          out_shape=jax.ShapeDtypeStruct((T, width), out_dtype),
                          compiler_params=_cparams(("parallel",)))(up, up, w, w, b, b)


def _ffn_glu_bwd(name, up, w, b, dact, dx_dtype):
    T = up.shape[0]
    kk = w.shape[0]
    cb = CONV_CB
    width = up.shape[1] // 2
    nblk = width // cb

    def body(g_ref, u_ref, wg_ref, wu_ref, bg_ref, bu_ref, d_ref, dg_ref, du_ref, dwg_ref, dwu_ref, dbg_ref, dbu_ref):
        wg, wu = wg_ref[...], wu_ref[...]
        g = _conv_pre_rot(g_ref[...], wg, bg_ref[...])
        u = _conv_pre_rot(u_ref[...], wu, bu_ref[...])
        d = d_ref[...].astype(F32)
        g_head = _conv_pre(g_ref[0:EDGE, :], wg, bg_ref[...])
        u_head = _conv_pre(u_ref[0:EDGE, :], wu, bu_ref[...])
        d_head = d_ref[0:EDGE, :].astype(F32)
        sg, sg_head = jax.nn.sigmoid(g), jax.nn.sigmoid(g_head)
        _conv_bwd_rot(g_ref, wg, d * u * (sg * (1.0 + g * (1.0 - sg))),
                      d_head * u_head * (sg_head * (1.0 + g_head * (1.0 - sg_head))), dg_ref, dwg_ref, dbg_ref)
        _conv_bwd_rot(u_ref, wu, d * (g * sg), d_head * (g_head * sg_head), du_ref, dwu_ref, dbu_ref)

    lo, hi = (lambda j: (0, j)), (lambda j: (0, nblk + j))
    in_specs = [pl.BlockSpec((T, cb), lo), pl.BlockSpec((T, cb), hi), pl.BlockSpec((kk, cb), lo), pl.BlockSpec((kk, cb), hi),
                pl.BlockSpec((1, cb), lo), pl.BlockSpec((1, cb), hi), pl.BlockSpec((T, cb), lo)]
    out_specs = [pl.BlockSpec((T, cb), lo)] * 2 + [pl.BlockSpec((kk, cb), lo)] * 2 + [pl.BlockSpec((1, cb), lo)] * 2
    out_shape = ([jax.ShapeDtypeStruct((T, width), dx_dtype)] * 2 + [jax.ShapeDtypeStruct((kk, width), F32)] * 2
                 + [jax.ShapeDtypeStruct((1, width), F32)] * 2)
    return pl.pallas_call(body, name=name, grid=(nblk,), in_specs=in_specs, out_specs=out_specs, out_shape=out_shape,
                          compiler_params=_cparams(("parallel",)))(up, up, w, w, b, b, dact)


def _loss_head(y, target):
    T, D = y.shape
    tb = _tile(T, 256, 8)

    def body(y_ref, t_ref, dy_ref, l_ref):
        @pl.when(pl.program_id(0) == 0)
        def _():
            l_ref[...] = jnp.zeros_like(l_ref)

        err = y_ref[...] - t_ref[...]
        dy_ref[...] = err * (1.0 / D)
        l_ref[...] += jnp.sum(err * err, axis=0, keepdims=True) * (0.5 / D)

    spec = pl.BlockSpec((tb, D), lambda i: (i, 0))
    return pl.pallas_call(body, name="loss_head", grid=(T // tb,), in_specs=[spec, spec],
                          out_specs=[spec, pl.BlockSpec((1, D), lambda i: (0, 0))],
                          out_shape=[jax.ShapeDtypeStruct((T, D), F32), jax.ShapeDtypeStruct((1, D), F32)],
                          compiler_params=_cparams(("arbitrary",)))(y, target)


def _adamw_math(w, g, m, v):
    m = ADAM_B1 * m + (1.0 - ADAM_B1) * g
    v = ADAM_B2 * v + (1.0 - ADAM_B2) * jnp.square(g)
    m_hat = m / (1.0 - ADAM_B1 ** ADAM_STEP)
    v_hat = v / (1.0 - ADAM_B2 ** ADAM_STEP)
    return -ADAM_LR * (m_hat / (jnp.sqrt(v_hat) + ADAM_EPS) + ADAM_WD * w), m, v


def _adamw(name, w, g, m, v, after=None):
    A, R, C = w.shape
    if C % LANES == 0:
        rb, cb = _slab(R, C)
    else:
        rb, cb = _tile(R, max(8, SLAB_BYTES // 2 // (C * 4) // 8 * 8), 8), C
    extra = [] if after is None else [after]

    def body(w_ref, g_ref, m_ref, v_ref, *rest):
        d_ref, mo_ref, vo_ref = rest[-3:]
        d, mn, vn = _adamw_math(w_ref[...], g_ref[...], m_ref[...], v_ref[...])
        d_ref[...] = d
        mo_ref[...] = mn
        vo_ref[...] = vn

    spec = pl.BlockSpec((1, rb, cb), lambda a, r, q: (a, r, q))
    return pl.pallas_call(body, name=name, grid=(A, R // rb, C // cb), in_specs=[spec] * 4 + [ANY] * len(extra),
                          out_specs=[spec] * 3, out_shape=[jax.ShapeDtypeStruct(w.shape, F32)] * 3,
                          compiler_params=_cparams(("parallel", "parallel", "parallel")))(w, g, m, v, *extra)


def _adamw_small(parts, w, m, v):
    def body(p_ref, w_ref, m_ref, v_ref, g_ref, d_ref, mo_ref, vo_ref):
        g = p_ref[0]
        for i in range(1, N_DEV):
            g = g + p_ref[i]
        d, mn, vn = _adamw_math(w_ref[...], g, m_ref[...], v_ref[...])
        g_ref[...] = g
        d_ref[...] = d
        mo_ref[...] = mn
        vo_ref[...] = vn

    return pl.pallas_call(body, name="adamw_small", out_shape=[jax.ShapeDtypeStruct(w.shape, F32)] * 4,
                          compiler_params=_cparams())(parts, w, m, v)


def _add_blocks(name, a, b, out_dtype=F32):
    n, R, W = a.shape
    rb = _tile(R, 512, 8)

    def body(a_ref, b_ref, o_ref):
        o_ref[...] = (a_ref[...].astype(F32) + b_ref[...].astype(F32)).astype(o_ref.dtype)

    spec = pl.BlockSpec((1, rb, W), lambda i, r: (i, r, 0))
    return pl.pallas_call(body, name=name, grid=(n, R // rb), in_specs=[spec, spec], out_specs=spec,
                          out_shape=jax.ShapeDtypeStruct(a.shape, out_dtype),
                          compiler_params=_cparams(("parallel", "parallel")))(a, b)


SLAB_BYTES = 5 << 19


def _slab(R, W):
    if R % 16 == 0:
        return _tile(R, max(16, SLAB_BYTES // (4 * W) // 16 * 16), 16), W
    assert W % LANES == 0, (R, W)
    return R, _tile(W, max(LANES, SLAB_BYTES // (4 * R) // LANES * LANES), LANES)


def _pair_add(name, g, other, c, chip):
    _, R, W = g.shape
    rb, cb = _slab(R, W)

    def body(s_ref, a_ref, b_ref, send_ref, own_ref):
        s = a_ref[0] + b_ref[0]
        send_ref[0] = s.astype(send_ref.dtype)

        @pl.when(pl.program_id(2) == s_ref[1])
        def _():
            own_ref[...] = s

    grid_spec = pltpu.PrefetchScalarGridSpec(
        num_scalar_prefetch=1, grid=(R // rb, W // cb, 4),
        in_specs=[pl.BlockSpec((1, rb, cb), lambda r, q, p, s_ref: (2 * p + s_ref[0], r, q)),
                  pl.BlockSpec((1, rb, cb), lambda r, q, p, s_ref: (p, r, q))],
        out_specs=[pl.BlockSpec((1, rb, cb), lambda r, q, p, s_ref: (p, r, q)),
                   pl.BlockSpec((rb, cb), lambda r, q, p, s_ref: (r, q))])
    scalars = jnp.stack([c, chip]).astype(jnp.int32)
    return pl.pallas_call(body, name=name, grid_spec=grid_spec,
                          out_shape=[jax.ShapeDtypeStruct((4, R, W), MXU_DTYPE), jax.ShapeDtypeStruct((R, W), F32)],
                          compiler_params=_cparams(("parallel", "parallel", "arbitrary")))(scalars, g, other)


def _sum4(name, own, parts):
    R, W = own.shape
    rb, cb = _slab(R, W)

    def body(o_ref, p_ref, out_ref):
        out_ref[...] = ((o_ref[...] + p_ref[0].astype(F32)) + p_ref[1].astype(F32)) + p_ref[2].astype(F32)

    return pl.pallas_call(body, name=name, grid=(R // rb, W // cb),
                          in_specs=[pl.BlockSpec((rb, cb), lambda r, q: (r, q)), pl.BlockSpec((3, rb, cb), lambda r, q: (0, r, q))],
                          out_specs=pl.BlockSpec((rb, cb), lambda r, q: (r, q)), out_shape=jax.ShapeDtypeStruct((R, W), F32),
                          compiler_params=_cparams(("parallel", "parallel")))(own, parts)


MESH = pl.DeviceIdType.MESH
ANY = pl.BlockSpec(memory_space=pl.ANY)


def _place():
    return lax.axis_index("x"), lax.axis_index("y"), lax.axis_index("c")


def _other_chips(x, y):
    return [(1 - x, y), (x, 1 - y), (1 - x, 1 - y)]


def _all_gather(name, blocks):
    n = len(blocks)

    def body(*refs):
        x_refs, out_refs = refs[:n], refs[n:2 * n]
        send_sems, recv_sems, local_sems = refs[2 * n:]
        x, y, c = _place()
        me, sibling = (x, y, c), (x, y, 1 - c)
        chips = _other_chips(x, y)

        def slot(a, px, py, pc):
            return out_refs[a].at[4 * px + 2 * py + pc]

        def copy(a, k, blk, to, src=None):
            return pltpu.make_async_remote_copy(src_ref=slot(a, *blk) if src is None else src, dst_ref=slot(a, *blk),
                                                send_sem=send_sems.at[a, k], recv_sem=recv_sems.at[a, k],
                                                device_id=to, device_id_type=MESH)

        mine = [pltpu.make_async_copy(x_refs[a], slot(a, *me), local_sems.at[a]) for a in range(n)]
        for cp in mine:
            cp.start()
        first = []
        for j, chip in enumerate(chips):
            first += [copy(a, 1 + j, me, (*chip, c), src=x_refs[a]) for a in range(n)]
        first += [copy(a, 0, me, sibling, src=x_refs[a]) for a in range(n)]
        for cp in first:
            cp.start()
        passed = []
        for j, chip in enumerate(chips):
            for a in range(n):
                copy(a, 1 + j, (*chip, c), me).wait_recv()
                passed.append(copy(a, 4 + j, (*chip, c), sibling))
                passed[-1].start()
        for a in range(n):
            copy(a, 0, sibling, me).wait_recv()
        for j, chip in enumerate(chips):
            for a in range(n):
                copy(a, 4 + j, (*chip, 1 - c), me).wait_recv()
        for cp in first + passed:
            cp.wait_send()
        for cp in mine:
            cp.wait()

    return pl.pallas_call(body, name=name, in_specs=[ANY] * n, out_specs=[ANY] * n,
                          out_shape=[jax.ShapeDtypeStruct((N_DEV,) + b.shape, b.dtype) for b in blocks],
                          scratch_shapes=[pltpu.SemaphoreType.DMA((n, 7)), pltpu.SemaphoreType.DMA((n, 7)),
                                          pltpu.SemaphoreType.DMA((n,))])(*blocks)


def _routes_to_sibling(x, y, c):
    return [(2 * p + (1 - c), p, (x, y, 1 - c)) for p in range(4)]


def _routes_to_chips(x, y, c):
    return [(2 * px + py, j, (px, py, c)) for j, (px, py) in enumerate(_other_chips(x, y))]


def _routes_block_to_chips(x, y, c):
    me = 4 * x + 2 * y + c
    return [(me, me, (px, py, c)) for px, py in _other_chips(x, y)]


def _routes_blocks_to_sibling(x, y, c):
    return [(4 * px + 2 * py + c, 4 * px + 2 * py + c, (x, y, 1 - c)) for px, py in [(x, y)] + _other_chips(x, y)]


def _route_copies(routes, src_refs, land_refs, send_sems, recv_sems):
    x, y, c = _place()
    copies = []
    for a, (src, land) in enumerate(zip(src_refs, land_refs)):
        plan = routes(x, y, c)
        for k, (s, d, target) in enumerate(plan):
            i = a * len(plan) + k
            copies.append(pltpu.make_async_remote_copy(src_ref=src.at[s], dst_ref=land.at[d], send_sem=send_sems.at[i],
                                                       recv_sem=recv_sems.at[i], device_id=target, device_id_type=MESH))
    return copies


def _exchange(name, routes, n_routes, srcs, land_slots):
    n = len(srcs)

    def body(*refs):
        copies = _route_copies(routes, refs[:n], refs[n:2 * n], refs[2 * n], refs[2 * n + 1])
        for cp in copies:
            cp.start()
        for cp in copies:
            cp.wait_recv()
        for cp in copies:
            cp.wait_send()

    return pl.pallas_call(body, name=name, in_specs=[ANY] * n, out_specs=[ANY] * n,
                          out_shape=[jax.ShapeDtypeStruct((land_slots,) + s.shape[1:], s.dtype) for s in srcs],
                          scratch_shapes=[pltpu.SemaphoreType.DMA((n * n_routes,)), pltpu.SemaphoreType.DMA((n * n_routes,))])(*srcs)


HBM_SPEC = pl.BlockSpec(memory_space=pltpu.HBM)
SEM_SPEC = pl.BlockSpec(memory_space=pltpu.SEMAPHORE)
DATAFLOW = pltpu.SideEffectType.DATAFLOW_SIDE_EFFECTING


def _exchange_start(name, routes, n_routes, srcs, lands, after=None):
    n = len(srcs)
    in_place = lands is None
    bufs = list(srcs) + ([] if in_place else list(lands))
    nb = len(bufs)
    extra = [] if after is None else [after]

    def body(*refs):
        src_refs = refs[:n]
        land_refs = src_refs if in_place else refs[n:nb]
        send_sems, recv_sems = refs[nb + len(extra)], refs[nb + len(extra) + 1]
        token = refs[-1]
        for cp in _route_copies(routes, src_refs, land_refs, send_sems, recv_sems):
            cp.start()
        token[...] = jnp.zeros_like(token)

    sems = [pltpu.SemaphoreType.DMA((n * n_routes,)), pltpu.SemaphoreType.DMA((n * n_routes,))]
    out = pl.pallas_call(
        body, name=name, in_specs=[HBM_SPEC] * nb + [ANY] * len(extra),
        out_shape=sems + [pltpu.HBM(b.shape, b.dtype) for b in bufs] + [jax.ShapeDtypeStruct((8, LANES), F32)],
        out_specs=[SEM_SPEC, SEM_SPEC] + [HBM_SPEC] * nb + [pl.BlockSpec(memory_space=pltpu.VMEM)],
        input_output_aliases={i: 2 + i for i in range(nb)},
        compiler_params=pltpu.CompilerParams(has_side_effects=DATAFLOW))(
        *[pltpu.with_memory_space_constraint(b, pltpu.HBM) for b in bufs], *extra)
    return (out[0], out[1], list(out[2:2 + nb])), out[-1]


def _exchange_wait(name, routes, n_routes, n, started, after):
    send_sems, recv_sems, bufs = started
    nb = len(bufs)
    in_place = nb == n

    def body(*refs):
        src_refs = refs[:n]
        land_refs = src_refs if in_place else refs[n:nb]
        for cp in _route_copies(routes, src_refs, land_refs, refs[nb], refs[nb + 1]):
            cp.wait_send()
            cp.wait_recv()

    out = pl.pallas_call(
        body, name=name, in_specs=[HBM_SPEC] * nb + [SEM_SPEC, SEM_SPEC, ANY],
        out_shape=[pltpu.HBM(b.shape, b.dtype) for b in bufs], out_specs=[HBM_SPEC] * nb,
        input_output_aliases={i: i for i in range(nb)},
        compiler_params=pltpu.CompilerParams(has_side_effects=DATAFLOW))(*bufs, send_sems, recv_sems, after)
    return list(out[:n]) if in_place else (list(out[:n]), list(out[n:]))


def _pair_sums(tag, gs, from_sibling):
    x, y, c = _place()
    return [_pair_add(f"rs_add_{tag}_{i}", g, o, c, 2 * x + y) for i, (g, o) in enumerate(zip(gs, from_sibling))]


def _reduce_scatter(tag, gs):
    sums = _pair_sums(tag, gs, _exchange(f"rs_swap_{tag}", _routes_to_sibling, 4, gs, 4))
    got = _exchange(f"rs_chips_{tag}", _routes_to_chips, 3, [s[0] for s in sums], 3)
    return [_sum4(f"rs_sum_{tag}_{i}", s[1], q) for i, (s, q) in enumerate(zip(sums, got))]


def _reduce_scatter_begin(tag, gs):
    lands = [lax.empty((4,) + g.shape[1:], g.dtype) for g in gs]
    swap, token = _exchange_start(f"rs_swap_{tag}_start", _routes_to_sibling, 4, gs, lands)
    return dict(tag=tag, n=len(gs), swap=swap), token


def _reduce_scatter_middle(state, after):
    tag, n = state["tag"], state["n"]
    gs, from_sibling = _exchange_wait(f"rs_swap_{tag}_wait", _routes_to_sibling, 4, n, state["swap"], after)
    state["sums"] = _pair_sums(tag, gs, from_sibling)
    partials = [s[0] for s in state["sums"]]
    lands = [lax.empty((3,) + p.shape[1:], p.dtype) for p in partials]
    state["chips"], token = _exchange_start(f"rs_chips_{tag}_start", _routes_to_chips, 3, partials, lands)
    return token


def _reduce_scatter_end(state, after):
    tag = state["tag"]
    _, got = _exchange_wait(f"rs_chips_{tag}_wait", _routes_to_chips, 3, state["n"], state["chips"], after)
    return [_sum4(f"rs_sum_{tag}_{i}", s[1], q) for i, (s, q) in enumerate(zip(state["sums"], got))]


def _all_gather_begin(tag, blocks, after):
    dev = 4 * lax.axis_index("x") + 2 * lax.axis_index("y") + lax.axis_index("c")
    zones = [lax.dynamic_update_slice_in_dim(lax.empty((N_DEV,) + b.shape, b.dtype), b[None], dev, axis=0) for b in blocks]
    chips, token = _exchange_start(f"gather_{tag}_chips_start", _routes_block_to_chips, 3, zones, None, after)
    return dict(tag=tag, n=len(blocks), chips=chips), token


def _all_gather_middle(state, after):
    tag, n = state["tag"], state["n"]
    zones = _exchange_wait(f"gather_{tag}_chips_wait", _routes_block_to_chips, 3, n, state["chips"], after)
    state["sibling"], token = _exchange_start(f"gather_{tag}_sibling_start", _routes_blocks_to_sibling, 4, zones, None)
    return token


def _all_gather_end(state, after):
    return _exchange_wait(f"gather_{state['tag']}_sibling_wait", _routes_blocks_to_sibling, 4, state["n"], state["sibling"], after)


PACK_UNIT = 8 * LANES


def _packed_size(shape):
    return -(-math.prod(shape) // PACK_UNIT) * PACK_UNIT


def _pack(arrays, dtype):
    parts = []
    for a in arrays:
        flat = a.reshape(-1).astype(dtype)
        parts.append(jnp.pad(flat, (0, _packed_size(a.shape) - flat.shape[0])))
    return jnp.concatenate(parts).reshape(-1, LANES)


def _unpack(flat, shapes, lead=()):
    out, row = [], 0
    for s in shapes:
        rows = _packed_size(s) // LANES
        piece = flat[..., row:row + rows, :].reshape(lead + (rows * LANES,))
        out.append(piece[..., :math.prod(s)].reshape(lead + tuple(s)))
        row += rows
    return out


def _ffn_pad_rows(a):
    n = a.shape[0] // FFN_HALF
    a = jnp.pad(a.reshape(n, FFN_HALF, a.shape[1]), ((0, 0), (0, FFN_HALF_PAD - FFN_HALF), (0, 0)))
    return a.reshape(n * FFN_HALF_PAD, a.shape[2])


def _ffn_unpad_rows(a):
    n = a.shape[0] // FFN_HALF_PAD
    return a.reshape(n, FFN_HALF_PAD, a.shape[1])[:, :FFN_HALF].reshape(n * FFN_HALF, a.shape[1])


def _ffn_pad_cols(a):
    n = a.shape[1] // FFN_HALF
    a = jnp.pad(a.reshape(a.shape[0], n, FFN_HALF), ((0, 0), (0, 0), (0, FFN_HALF_PAD - FFN_HALF)))
    return a.reshape(a.shape[0], n * FFN_HALF_PAD)


def _ffn_unpad_cols(a):
    n = a.shape[1] // FFN_HALF_PAD
    return a.reshape(a.shape[0], n, FFN_HALF_PAD)[:, :, :FFN_HALF].reshape(a.shape[0], n * FFN_HALF)


def _shard_to_send(name, shard):
    if name in ("w_in", "w_br_gdn", "w_br_gla"):
        shard = shard.T
    elif name == "ffn_w_up":
        shard = _ffn_pad_rows(shard.T)
    return shard.astype(MXU_DTYPE)


KEPT_TRANSPOSED = ("w_in", "w_br_gdn", "w_br_gla", "ffn_w_up")


def _whole_from_gathered(name, g):
    if name == "w_in":
        return _in_proj_from_shards(g)
    if name == "ffn_w_down":
        return jnp.pad(g, ((0, 0), (0, FFN_HALF_PAD - FFN_HALF), (0, 0))).reshape(FFN_PAD, g.shape[2])
    return g.reshape(N_DEV * g.shape[1], g.shape[2])


def _slots_from_whole(name, gw):
    if name == "w_in":
        return _in_proj_to_slots(gw)
    return gw.reshape(N_DEV, gw.shape[0] // N_DEV, gw.shape[1])


def _shard_from_slot(name, s):
    if name == "ffn_w_up":
        return _ffn_unpad_rows(s)
    if name == "ffn_w_down":
        return s[:FFN_HALF]
    return s


def _in_proj_pieces():
    starts, pos = {}, 0
    for n, width in IN_SPLITS:
        starts[n] = (pos, width)
        pos += width
    return [(starts[ref][0], off + lane, starts[ref][1]) for _, off, _, pieces in PAD_SEGS for ref, lane in pieces]


def _in_proj_moves():
    cs = IN_DIM // N_DEV
    moves = []
    for src, dst, n in sorted(_in_proj_pieces()):
        at = src
        while at < src + n:
            d = at // cs
            end = min(src + n, (d + 1) * cs)
            moves.append((d, at - d * cs, dst + at - src, end - at))
            at = end
    return moves


RELAYOUT_LANES = 128


def _in_proj_from_shards(g):
    _, cs, D = g.shape

    def body(g_ref, o_ref):
        o_ref[...] = jnp.zeros_like(o_ref)
        for d, i0, r0, n in _in_proj_moves():
            o_ref[r0:r0 + n, :] = g_ref[d, i0:i0 + n, :]

    cb = RELAYOUT_LANES
    return pl.pallas_call(body, name="w_in_rows_in", grid=(D // cb,),
                          in_specs=[pl.BlockSpec((N_DEV, cs, cb), lambda j: (0, 0, j))],
                          out_specs=pl.BlockSpec((IN_PAD, cb), lambda j: (0, j)),
                          out_shape=jax.ShapeDtypeStruct((IN_PAD, D), g.dtype), compiler_params=_cparams(("parallel",)))(g)


def _in_proj_to_slots(gw):
    D = gw.shape[1]
    cs = IN_DIM // N_DEV

    def body(x_ref, o_ref):
        for d, i0, r0, n in _in_proj_moves():
            o_ref[d, i0:i0 + n, :] = x_ref[r0:r0 + n, :]

    cb = RELAYOUT_LANES
    return pl.pallas_call(body, name="w_in_rows_out", grid=(D // cb,),
                          in_specs=[pl.BlockSpec((IN_PAD, cb), lambda j: (0, j))],
                          out_specs=pl.BlockSpec((N_DEV, cs, cb), lambda j: (0, 0, j)),
                          out_shape=jax.ShapeDtypeStruct((N_DEV, cs, D), gw.dtype), compiler_params=_cparams(("parallel",)))(gw)


def _pad_in_proj_rows(w):
    rows, at = [], 0
    for src, dst, n in sorted(_in_proj_pieces(), key=lambda p: p[1]):
        if dst > at:
            rows.append(jnp.zeros((dst - at, w.shape[1]), w.dtype))
        rows.append(w[src:src + n])
        at = dst + n
    rows.append(jnp.zeros((IN_PAD - at, w.shape[1]), w.dtype))
    return jnp.concatenate(rows, axis=0)


def _unpad_in_proj_rows(wp):
    return jnp.concatenate([wp[dst:dst + n] for _, dst, n in sorted(_in_proj_pieces())], axis=0)


def _lane_pad(a, width=LANES):
    return jnp.pad(a, ((0, 0), (0, width - a.shape[1])))


def _seg_blk(h, name, rows):
    off, width = SEG[name]
    return (h, rows, width, off // width)


def _ln_both(xs_, ps_):
    (y,) = _ln_fn(xs_, ps_)
    return (y, y)


def _behind(param, hooks, stage, *seen):
    if hooks is None or stage not in hooks:
        return param
    token = hooks[stage](*seen)
    return param if token is None else param + token[0:1, 0:1]


def _layer_fwd(l, x, x_mx, W, sp, hooks=None):
    T = x.shape[0]
    n64, ngla, ntok = T // SSD_CHUNK, T // GLA_BLOCK, T // 256
    h = _mm(f"in_proj_{l}", x_mx, W["w_in"], "nt")
    xbc = _conv_silu_fwd(f"ssd_conv_{l}", h, SEG["xbc"][0], sp["ssd_conv_w"], sp["ssd_conv_b"])
    gqkv = _conv_silu_fwd(f"gdn_conv_{l}", h, SEG["gqkv"][0], sp["gdn_conv_w"], None)

    ssd_in = [(xbc, SSD_CHUNK, SSD_XBC, 0), _seg_blk(h, "dt", SSD_CHUNK), _seg_blk(h, "z", SSD_CHUNK)]
    ssd_p = [sp["ssd_dt_bias"], sp["ssd_a_log"], sp["ssd_d"], sp["ssd_norm_w"]]
    o_ssd, ssd_states = _chain_fwd(f"ssd_fwd_{l}", _ssd_chunk, n64, ssd_in, ssd_p, [(SSD_CHUNK, SSD_INNER, MXU_DTYPE)],
                                   (SSD_STATE, SSD_INNER))
    o_gdn, gdn_saved = _gdn_forward(str(l), gqkv, h, dict(sp, gdn_a_log=_behind(sp["gdn_a_log"], hooks, "ssd", o_ssd)))
    gla_in = [_seg_blk(h, "lqkv", GLA_BLOCK), _seg_blk(h, "lglr", GLA_BLOCK), _seg_blk(h, "lr", GLA_BLOCK)]
    gla_p = [jnp.pad(sp["gla_gate_w2"], ((0, LANES - GLA_RANK), (0, 0))), sp["gla_gate_b"], sp["gla_norm_w"]]
    o_gla, gla_states = _chain_fwd(f"gla_fwd_{l}", _gla_block, ngla, gla_in, gla_p, [(GLA_BLOCK, GLA_V, MXU_DTYPE)],
                                   (GLA_VAL_DIM, GLA_K))
    ln1_p = [_behind(sp["ln1_g"], hooks, "mixed", o_gdn), sp["ln1_b"]]
    y_ssd = _mm(f"br_ssd_{l}", o_ssd, W["w_br_ssd"])
    y_gdn = _mm(f"br_gdn_{l}", o_gdn, W["w_br_gdn"], "nt")
    y_gla = _mm(f"br_gla_{l}", o_gla, W["w_br_gla"], "nt")
    merge_in = [_seg_blk(h, "gates", 256), (y_ssd, 256, D_MODEL, 0), (y_gdn, 256, D_MODEL, 0), (y_gla, 256, D_MODEL, 0)]
    (mix,) = _chain_fwd(f"merge_{l}", _merge_fn, ntok, merge_in, [], [(256, D_MODEL, MXU_DTYPE)])
    r1 = _mm(f"out_proj_{l}", mix, W["w_out"])
    both = [(256, D_MODEL, F32), (256, D_MODEL, MXU_DTYPE)]
    x1, x1_mx = _chain_fwd(f"ln1_{l}", _ln_both, ntok, [(x, 256, D_MODEL, 0), (r1, 256, D_MODEL, 0)], ln1_p, both)
    up = _mm(f"ffn_up_{l}", x1_mx, W["ffn_w_up"], "nt")
    act = _ffn_glu_fwd(f"ffn_glu_{l}", up, sp["ffn_conv_w_pad"], sp["ffn_conv_b_pad"], MXU_DTYPE)
    ln2_p = [_behind(sp["ln2_g"], hooks, "ffn_act", act), sp["ln2_b"]]
    r2 = _mm(f"ffn_down_{l}", act, W["ffn_w_down"], tn=1024, tk=1024)
    x2, x2_mx = _chain_fwd(f"ln2_{l}", _ln_both, ntok, [(x1, 256, D_MODEL, 0), (r2, 256, D_MODEL, 0)], ln2_p, both)
    saved = dict(x=x, x_mx=x_mx, h=h, xbc=xbc, gqkv=gqkv, ssd_in=ssd_in, ssd_p=ssd_p, ssd_states=ssd_states,
                 gdn=gdn_saved, gla_in=gla_in, gla_p=gla_p, gla_states=gla_states, o_ssd=o_ssd,
                 o_gdn=o_gdn, o_gla=o_gla, merge_in=merge_in, mix=mix, r1=r1, ln1_p=ln1_p, x1=x1, x1_mx=x1_mx, up=up, act=act,
                 r2=r2, ln2_p=ln2_p)
    return x2, x2_mx, saved


def _layer_bwd(l, dx2, W, sp, sv, hooks=None):
    T = dx2.shape[0]
    n64, ngla, ntok = T // SSD_CHUNK, T // GLA_BLOCK, T // 256
    bf = MXU_DTYPE
    gw, gs = {}, {}
    ln2_p = [_behind(sv["ln2_p"][0], hooks, "start"), sv["ln2_p"][1]]
    (dx1_a, dr2), (gs["ln2_g"], gs["ln2_b"]) = _chain_bwd(
        f"ln2_bwd_{l}", _ln_fn, ntok, [(sv["x1"], 256, D_MODEL, 0), (sv["r2"], 256, D_MODEL, 0)], ln2_p,
        [(dx2, 256, D_MODEL)], dx_dtypes=[F32, bf])
    gw["ffn_w_down"] = _mm(f"ffn_down_dw_{l}", sv["act"], dr2, "tn", tn=1024)
    dact = _mm(f"ffn_down_dx_{l}", dr2, W["ffn_w_down"], "nt")
    dg, du, dwg, dwu, dbg, dbu = _ffn_glu_bwd(f"ffn_glu_bwd_{l}", sv["up"], sp["ffn_conv_w_pad"], sp["ffn_conv_b_pad"], dact, bf)
    gs["ffn_conv_w"] = _ffn_unpad_cols(jnp.concatenate([dwg, dwu], axis=1))
    gs["ffn_conv_b"] = _ffn_unpad_cols(jnp.concatenate([dbg, dbu], axis=1))
    dup = jnp.concatenate([dg, du], axis=1)
    gw["ffn_w_up"] = _mm(f"ffn_up_dw_{l}", dup, sv["x1_mx"], "tn", tn=1024)
    dx1_b = _mm(f"ffn_up_dx_{l}", dup, W["ffn_w_up"], "nn", tn=1024, tk=1024)
    ln1_p = [_behind(sv["ln1_p"][0], hooks, "ffn", dx1_b), sv["ln1_p"][1]]
    (dx_a, dr1), (gs["ln1_g"], gs["ln1_b"]) = _chain_bwd(
        f"ln1_bwd_{l}", _ln_sum_fn, ntok, [(sv["x"], 256, D_MODEL, 0), (sv["r1"], 256, D_MODEL, 0)], ln1_p,
        [(dx1_a, 256, D_MODEL), (dx1_b, 256, D_MODEL)], dx_dtypes=[F32, bf])
    gw["w_out"] = _mm(f"out_proj_dw_{l}", sv["mix"], dr1, "tn")
    dmix = _mm(f"out_proj_dx_{l}", dr1, W["w_out"], "nt")
    (dgates, dy_ssd, dy_gdn, dy_gla), _ = _chain_bwd(f"merge_bwd_{l}", _merge_fn, ntok, sv["merge_in"], [],
                                                     [(dmix, 256, D_MODEL)], dx_dtypes=[bf, bf, bf, bf])
    gw["w_br_ssd"] = _mm(f"br_ssd_dw_{l}", sv["o_ssd"], dy_ssd, "tn")
    gw["w_br_gdn"] = _mm(f"br_gdn_dw_{l}", dy_gdn, sv["o_gdn"], "tn")
    gw["w_br_gla"] = _mm(f"br_gla_dw_{l}", dy_gla, sv["o_gla"], "tn")
    do_ssd = _mm(f"br_ssd_dx_{l}", dy_ssd, W["w_br_ssd"], "nt")
    do_gdn = _mm(f"br_gdn_dx_{l}", dy_gdn, W["w_br_gdn"], "nn")
    do_gla = _mm(f"br_gla_dx_{l}", dy_gla, W["w_br_gla"], "nn")

    ssd_p = [_behind(sv["ssd_p"][0], hooks, "branches", do_gla, gw)] + list(sv["ssd_p"][1:])
    (dxbc, ddt, dz), dps = _chain_bwd(f"ssd_bwd_{l}", _ssd_chunk, n64, sv["ssd_in"], ssd_p,
                                      [(do_ssd, SSD_CHUNK, SSD_INNER)], sprev=sv["ssd_states"], dx_dtypes=[F32, bf, bf])
    gs["ssd_dt_bias"], gs["ssd_a_log"], gs["ssd_d"], gs["ssd_norm_w"] = dps
    gdn_sv = dict(sv["gdn"], scan_p=[_behind(sv["gdn"]["scan_p"][0], hooks, "ssd", dz)])
    dgqkv, dgab, dgg, gs["gdn_a_log"], gs["gdn_dt_bias"], gs["gdn_norm_w"] = _gdn_backward(str(l), do_gdn, gdn_sv, bf)
    (dlqkv, dlglr, dlr), dps = _chain_bwd(f"gla_bwd_{l}", _gla_block, ngla, sv["gla_in"], sv["gla_p"],
                                          [(do_gla, GLA_BLOCK, GLA_V)], sprev=sv["gla_states"], dx_dtypes=[bf, bf, bf])
    gs["gla_gate_w2"], gs["gla_gate_b"], gs["gla_norm_w"] = dps[0][:GLA_RANK], dps[1], dps[2]
    dxbc_pre, gs["ssd_conv_w"], gs["ssd_conv_b"] = _conv_silu_bwd(
        f"ssd_conv_bwd_{l}", sv["h"], SEG["xbc"][0], sp["ssd_conv_w"], sp["ssd_conv_b"], dxbc, bf)
    dgqkv_pre, gs["gdn_conv_w"] = _conv_silu_bwd(f"gdn_conv_bwd_{l}", sv["h"], SEG["gqkv"][0], sp["gdn_conv_w"], None, dgqkv, bf)
    pieces = dict(gates=dgates, xbc=dxbc_pre, gqkv=dgqkv_pre, z=dz, lqkv=dlqkv, gg=dgg, lr=dlr, dt=ddt, gab=dgab, lglr=dlglr)
    cols = [pieces[name] for name, _, _, _ in PAD_SEGS]
    cols.append(jnp.zeros((T, IN_PAD - PAD_SEGS[-1][1] - PAD_SEGS[-1][2]), bf))
    dh = jnp.concatenate(cols, axis=1)
    gw["w_in"] = _mm(f"in_proj_dw_{l}", dh, sv["x_mx"], "tn", tn=1024)
    behind = hooks["w_in_grad"](gw) if hooks is not None and "w_in_grad" in hooks else None
    dx_b = _mm(f"in_proj_dx_{l}", dh, W["w_in"], "nn", tm=1024, tn=1024, tk=IN_PAD // 4, after=behind)
    dx = _add_blocks(f"dx_add_{l}", dx_a[None], dx_b[None])[0]
    return dx, gw, gs


def _ln_sum_fn(xs_, ps_):
    (y,) = _ln_fn(xs_, ps_)
    return (y, y)


def _small_2d(name, a):
    return a.reshape(1, -1) if a.ndim == 1 else a


def kernel(x, w_in, ssd_conv_w, ssd_conv_b, ssd_dt_bias, ssd_a_log, ssd_d, ssd_norm_w, gdn_conv_w, gdn_a_log, gdn_dt_bias, gdn_norm_w, gla_gate_w2, gla_gate_b, gla_norm_w, w_br_ssd, w_br_gdn, w_br_gla, w_out, ln1_g, ln1_b, ffn_w_up, ffn_conv_w, ffn_conv_b, ffn_w_down, ln2_g, ln2_b, loss_target, m_w_in, m_ssd_conv_w, m_ssd_conv_b, m_ssd_dt_bias, m_ssd_a_log, m_ssd_d, m_ssd_norm_w, m_gdn_conv_w, m_gdn_a_log, m_gdn_dt_bias, m_gdn_norm_w, m_gla_gate_w2, m_gla_gate_b, m_gla_norm_w, m_w_br_ssd, m_w_br_gdn, m_w_br_gla, m_w_out, m_ln1_g, m_ln1_b, m_ffn_w_up, m_ffn_conv_w, m_ffn_conv_b, m_ffn_w_down, m_ln2_g, m_ln2_b, v_w_in, v_ssd_conv_w, v_ssd_conv_b, v_ssd_dt_bias, v_ssd_a_log, v_ssd_d, v_ssd_norm_w, v_gdn_conv_w, v_gdn_a_log, v_gdn_dt_bias, v_gdn_norm_w, v_gla_gate_w2, v_gla_gate_b, v_gla_norm_w, v_w_br_ssd, v_w_br_gdn, v_w_br_gla, v_w_out, v_ln1_g, v_ln1_b, v_ffn_w_up, v_ffn_conv_w, v_ffn_conv_b, v_ffn_w_down, v_ln2_g, v_ln2_b):
    args = locals()
    w = {n: args[n] for n in WEIGHTS}
    m = {n: args["m_" + n] for n in WEIGHTS}
    v = {n: args["v_" + n] for n in WEIGHTS}
    dev = 4 * lax.axis_index("x") + 2 * lax.axis_index("y") + lax.axis_index("c")
    xl = x[0]
    tgt = loss_target[0]

    late = BIG[1:]

    def send(names, l):
        return [_shard_to_send(n, w[n][l]) for n in names]

    def whole_weights(names, got):
        return {n: _whole_from_gathered(n, g) for n, g in zip(names, got)}

    got0 = _all_gather("gather_first", send(BIG[:1], 0) + [w[n] for n in SMALL_SHARDED])
    gather0, token0 = _all_gather_begin("w_0", send(late, 0), got0[0])
    W = [whole_weights(BIG[:1], got0[:1]), None]
    whole = dict(w)
    for n, s in zip(SMALL_SHARDED, got0[1:]):
        whole[n] = jnp.transpose(s, (1, 2, 0, 3)).reshape(s.shape[1], s.shape[2], N_DEV * s.shape[3])
    SP = [{n: _small_2d(n, whole[n][l]) for n in SMALL} for l in range(DEPTH)]
    for sp in SP:
        sp["ffn_conv_w_pad"] = _ffn_pad_cols(sp["ffn_conv_w"])
        sp["ffn_conv_b_pad"] = _ffn_pad_cols(sp["ffn_conv_b"])

    held = {}

    def late_weights_cross(o_ssd):
        token = _all_gather_middle(gather0, o_ssd)
        held["gather1"], token1 = _all_gather_begin("w_1", send(BIG, 1), o_ssd)
        return token + token1

    def late_weights_arrive(mixed):
        W[0].update(whole_weights(late, _all_gather_end(gather0, mixed)))

    fwd_hooks = {"ssd": late_weights_cross, "mixed": late_weights_arrive,
                 "ffn_act": lambda act: _all_gather_middle(held["gather1"], act)}
    saved = [None] * DEPTH
    act, act_mx, saved[0] = _layer_fwd(0, xl, (xl + token0[0, 0]).astype(MXU_DTYPE), W[0], SP[0], hooks=fwd_hooks)
    W[1] = whole_weights(BIG, _all_gather_end(held["gather1"], act))
    act, act_mx, saved[1] = _layer_fwd(1, act, act_mx, W[1], SP[1])
    dy, loss_parts = _loss_head(act, tgt)
    loss = lax.psum(jnp.sum(loss_parts), ("x", "y", "c"))

    def slots_of(names, gw):
        return [_slots_from_whole(n, gw[n]) for n in names]

    grads = {}
    GS = [None] * DEPTH
    dy, gw, GS[1] = _layer_bwd(1, dy, W[1], SP[1], saved[1])
    reduce1, reduce1_token = _reduce_scatter_begin("1", slots_of(BIG, gw))

    def late_grads_leave(seen, gw0):
        held["reduce0"], token = _reduce_scatter_begin("0", slots_of(late, gw0))
        return token

    def w_in_grad_leaves(gw0):
        held["reduce_first"], token = _reduce_scatter_begin("first", slots_of(BIG[:1], gw0))
        return token

    bwd_hooks = {"start": lambda: reduce1_token, "ffn": lambda seen: _reduce_scatter_middle(reduce1, seen),
                 "branches": late_grads_leave, "ssd": lambda seen: _reduce_scatter_middle(held["reduce0"], seen),
                 "w_in_grad": w_in_grad_leaves}
    dy, gw, GS[0] = _layer_bwd(0, dy, W[0], SP[0], saved[0], hooks=bwd_hooks)
    small_shapes = [whole[n].shape for n in SMALL]
    gs_flat = _pack([jnp.stack([GS[l][n].reshape(whole[n].shape[1:]) for l in range(DEPTH)]) for n in SMALL], F32)
    (gs_all,) = _all_gather("gather_small_grads", [gs_flat])
    first_token = _reduce_scatter_middle(held["reduce_first"], gs_all)
    red1 = _reduce_scatter_end(reduce1, dy)
    red0_late = _reduce_scatter_end(held["reduce0"], dy)
    grad_x = dy[None]
    kept_t = KEPT_TRANSPOSED
    grads_k = {n: jnp.stack([_shard_from_slot(n, red0_late[i]), _shard_from_slot(n, red1[i + 1])]) for i, n in enumerate(late)}

    def mine(n, a):
        if n in SMALL_SHARDED:
            cs = a.shape[-1] // N_DEV
            return lax.dynamic_slice_in_dim(a, dev * cs, cs, axis=a.ndim - 1)
        return a

    m_whole, v_whole = {}, {}
    for n in SMALL:
        reps = (1, 1, N_DEV) if n in SMALL_SHARDED else (1,) * m[n].ndim
        m_whole[n], v_whole[n] = jnp.tile(m[n], reps), jnp.tile(v[n], reps)
    outs = _adamw_small(gs_all, _pack([whole[n] for n in SMALL], F32) + first_token[0:1, 0:1], _pack([m_whole[n] for n in SMALL], F32),
                        _pack([v_whole[n] for n in SMALL], F32))
    g_s, d_s, m_s, v_s = [_unpack(o, small_shapes) for o in outs]
    delta, new_m, new_v = {}, {}, {}
    for i, n in enumerate(SMALL):
        grads[n], delta[n], new_m[n], new_v[n] = mine(n, g_s[i]), mine(n, d_s[i]), mine(n, m_s[i]), mine(n, v_s[i])
    for n in late + BIG[:1]:
        if n == "w_in":
            done = sum(new_v[k].reshape(-1)[0:1] for k in late + SMALL[:1])
            (first0,) = _reduce_scatter_end(held["reduce_first"], done)
            grads_k[n] = jnp.stack([first0, red1[0]])
        view = (lambda a: jnp.transpose(a, (0, 2, 1))) if n in kept_t else (lambda a: a)
        outs = _adamw(f"adamw_{n}", view(w[n]), grads_k[n], view(m[n]), view(v[n]), after=None if n == "w_in" else first_token)
        grads[n], delta[n], new_m[n], new_v[n] = view(grads_k[n]), view(outs[0]), view(outs[1]), view(outs[2])

    return (loss, grad_x, *[grads[n] for n in WEIGHTS], *[delta[n] for n in WEIGHTS], *[new_m[n] for n in WEIGHTS],
            *[new_v[n] for n in WEIGHTS])
```

```python
import functools
import math

import jax
import jax.numpy as jnp
from jax import lax
from jax.experimental import pallas as pl
from jax.experimental.pallas import tpu as pltpu

F32 = jnp.float32
MXU_DTYPE = jnp.bfloat16
HI = lax.Precision.HIGHEST

N_DEV = 8
D_MODEL = 1024
DEPTH = 2
SSD_HEADS, SSD_HEAD_DIM, SSD_INNER, SSD_GROUPS, SSD_STATE, SSD_CHUNK = 16, 64, 1024, 2, 128, 64
SSD_XBC = SSD_INNER + 2 * SSD_GROUPS * SSD_STATE
GDN_HEADS, GDN_HEAD_DIM, GDN_WIDTH, GDN_CHUNK = 4, 128, 512, 64
GLA_HEADS, GLA_KEY_DIM, GLA_VAL_DIM, GLA_K, GLA_V, GLA_RANK, GLA_CHUNK = 4, 64, 128, 256, 512, 16, 16
GLA_BLOCK = 128
GLA_NORMALIZER = 16.0
FFN_DIM = 2816
FFN_HALF = FFN_DIM // 8
FFN_HALF_PAD = 384
FFN_UP_PAD = 16 * FFN_HALF_PAD
FFN_PAD = FFN_UP_PAD // 2
ALPHA = (2 * DEPTH) ** 0.25
LN_EPS = 1e-5
RMS_EPS = 1e-6
ADAM_LR, ADAM_B1, ADAM_B2, ADAM_EPS, ADAM_WD, ADAM_STEP = 0.001, 0.9, 0.999, 1e-08, 0.01, 10
LANES = 128
NEG_BIG = -1e30
VMEM_LIMIT = 56 * 1024 * 1024

IN_SPLITS = (("z", 1024), ("xbc", 1536), ("dt", 16), ("gqkv", 1536), ("ga", 4), ("gb", 4), ("gg", 512),
             ("lqkv", 1024), ("lglr", 16), ("lr", 512), ("gates", 3072))
IN_DIM = sum(w for _, w in IN_SPLITS)
PAD_SEGS = (("gates", 0, 3072, (("gates", 0),)), ("xbc", 3072, 1536, (("xbc", 0),)),
            ("gqkv", 4608, 1536, (("gqkv", 0),)), ("z", 6144, 1024, (("z", 0),)),
            ("lqkv", 7168, 1024, (("lqkv", 0),)), ("gg", 8192, 512, (("gg", 0),)), ("lr", 8704, 512, (("lr", 0),)),
            ("dt", 9216, 128, (("dt", 0),)), ("gab", 9344, 128, (("ga", 0), ("gb", 4))), ("lglr", 9472, 128, (("lglr", 0),)))
IN_PAD = 9728
SEG = {name: (off, width) for name, off, width, _ in PAD_SEGS}

BIG = ("w_in", "w_br_ssd", "w_br_gdn", "w_br_gla", "w_out", "ffn_w_up", "ffn_w_down")
COL_SHARDED = ("w_in", "w_br_gdn", "w_br_gla", "ffn_w_up")
SMALL_SHARDED = ("ssd_conv_w", "gdn_conv_w", "gla_gate_w2", "ffn_conv_w")
WEIGHTS = ("w_in", "ssd_conv_w", "ssd_conv_b", "ssd_dt_bias", "ssd_a_log", "ssd_d", "ssd_norm_w", "gdn_conv_w",
           "gdn_a_log", "gdn_dt_bias", "gdn_norm_w", "gla_gate_w2", "gla_gate_b", "gla_norm_w", "w_br_ssd", "w_br_gdn",
           "w_br_gla", "w_out", "ln1_g", "ln1_b", "ffn_w_up", "ffn_conv_w", "ffn_conv_b", "ffn_w_down", "ln2_g", "ln2_b")
SMALL = tuple(n for n in WEIGHTS if n not in BIG)
FLAT_W = 512


def _cparams(sem=None):
    kw = dict(vmem_limit_bytes=VMEM_LIMIT)
    if sem is not None:
        kw["dimension_semantics"] = sem
    return pltpu.CompilerParams(**kw)


_DIMS = {"nn": (((1,), (0,)), ((), ())), "nt": (((1,), (1,)), ((), ())), "tn": (((0,), (0,)), ((), ()))}


def _dot(a, b, dims="nn"):
    if MXU_DTYPE == F32:
        return lax.dot_general(a.astype(F32), b.astype(F32), _DIMS[dims], precision=HI, preferred_element_type=F32)
    return lax.dot_general(a.astype(MXU_DTYPE), b.astype(MXU_DTYPE), _DIMS[dims], preferred_element_type=F32)


def _dot_hi(a, b, dims="nn"):
    return lax.dot_general(a.astype(F32), b.astype(F32), _DIMS[dims], precision=HI, preferred_element_type=F32)


def _iota2(shape, axis):
    return lax.broadcasted_iota(jnp.int32, shape, axis)


def _tril(n, strict=False):
    r, c = _iota2((n, n), 0), _iota2((n, n), 1)
    return (r > c) if strict else (r >= c)


def _raw_dot(a, b, dims):
    return lax.dot_general(a, b, _DIMS[dims], preferred_element_type=F32)


def _dot_x3(a, b, dims="nn"):
    if MXU_DTYPE == F32:
        return _dot_hi(a, b, dims)
    ah, bh = a.astype(jnp.bfloat16), b.astype(jnp.bfloat16)
    al, bl = (a - ah.astype(F32)).astype(jnp.bfloat16), (b - bh.astype(F32)).astype(jnp.bfloat16)
    return _raw_dot(ah, bh, dims) + (_raw_dot(ah, bl, dims) + _raw_dot(al, bh, dims))


def _exact_dot(mask, b, dims, mask_first):
    if MXU_DTYPE == F32:
        return _dot_hi(mask, b, dims) if mask_first else _dot_hi(b, mask, dims)
    m = mask.astype(jnp.bfloat16)
    b1 = b.astype(jnp.bfloat16)
    r1 = b - b1.astype(F32)
    b2 = r1.astype(jnp.bfloat16)
    b3 = (r1 - b2.astype(F32)).astype(jnp.bfloat16)
    if mask_first:
        return _raw_dot(m, b1, dims) + (_raw_dot(m, b2, dims) + _raw_dot(m, b3, dims))
    return _raw_dot(b1, m, dims) + (_raw_dot(b2, m, dims) + _raw_dot(b3, m, dims))


@jax.custom_vjp
def _mask_left(mask, b):
    return _exact_dot(mask, b, "nn", True)


_mask_left.defvjp(lambda mask, b: (_mask_left(mask, b), mask),
                  lambda mask, d: (jnp.zeros_like(mask), _exact_dot(mask, d, "tn", True)))


@jax.custom_vjp
def _mask_right(a, mask):
    return _exact_dot(mask, a, "nn", False)


_mask_right.defvjp(lambda a, mask: (_mask_right(a, mask), mask),
                   lambda mask, d: (_exact_dot(mask, d, "nt", False), jnp.zeros_like(mask)))


@jax.custom_vjp
def _unit_lower_inverses(mats):
    n = mats[0].shape[0]
    eye = (_iota2((n, n), 0) == _iota2((n, n), 1)).astype(F32)
    xs = [eye - a for a in mats]
    ps = list(mats)
    k = 2
    while k < n:
        ps = [_dot_x3(p, p) for p in ps]
        xs = [x + _dot_x3(x, p) for x, p in zip(xs, ps)]
        k *= 2
    return xs


def _unit_lower_inverses_fwd(mats):
    ts = _unit_lower_inverses(mats)
    return ts, ts


def _unit_lower_inverses_bwd(ts, dts):
    mids = [_dot_x3(t, d, "tn") for t, d in zip(ts, dts)]
    return ([-_dot_x3(m, t, "nt") for m, t in zip(mids, ts)],)


_unit_lower_inverses.defvjp(_unit_lower_inverses_fwd, _unit_lower_inverses_bwd)


def _ssd_chunk(xs_, ps_, s_t):
    xbc, dtraw, z = xs_
    dt_bias, a_log, d_skip, norm_w = ps_
    L = xbc.shape[0]
    H, P, N, G = SSD_HEADS, SSD_HEAD_DIM, SSD_STATE, SSD_GROUPS
    W = SSD_INNER // G
    xs = xbc[:, :SSD_INNER]
    bm = xbc[:, SSD_INNER:SSD_INNER + G * N]
    cm = xbc[:, SSD_INNER + G * N:]
    dt = jax.nn.softplus(dtraw[:, :H] + dt_bias)
    a = dt * (-jnp.exp(a_log))
    causal = _tril(L)
    a_cs = _mask_left(causal.astype(F32), a)
    expand = (_iota2((H, SSD_INNER), 1) // P == _iota2((H, SSD_INNER), 0)).astype(F32)
    wide = _mask_right(jnp.concatenate([a_cs, dt, jnp.broadcast_to(d_skip, (L, H))], axis=0), expand)
    a_cs_x, dt_x, d_x = wide[:L], wide[L:2 * L], wide[2 * L:]
    a_end_x = a_cs_x[L - 1:L, :]
    a_cs_t, dt_t = a_cs.T, dt.T
    cb = [_dot(cm[:, g * N:(g + 1) * N], bm[:, g * N:(g + 1) * N], "nt") for g in range(G)]
    cb2 = [jnp.concatenate([c, c], axis=1) for c in cb]
    lane2 = _iota2((L, 2 * L), 1)
    left = lane2 < L
    causal2 = _iota2((L, 2 * L), 0) >= jnp.where(left, lane2, lane2 - L)
    pairs = range(0, H, 2)
    col2 = [jnp.where(left, a_cs[:, h:h + 1], a_cs[:, h + 1:h + 2]) for h in pairs]
    row2 = [jnp.concatenate([a_cs_t[h:h + 1, :], a_cs_t[h + 1:h + 2, :]], axis=1) for h in pairs]
    dt2 = [jnp.concatenate([dt_t[h:h + 1, :], dt_t[h + 1:h + 2, :]], axis=1) for h in pairs]
    ws2 = [cb2[h // (H // G)] * (jnp.exp(jnp.where(causal2, col2[i] - row2[i], NEG_BIG)) * dt2[i]) for i, h in enumerate(pairs)]
    first = _iota2((L, 2 * P), 1) < P
    ys = []
    for i, h in enumerate(pairs):
        x2 = xs[:, h * P:(h + 2) * P]
        ys.append(_dot(ws2[i], jnp.concatenate([jnp.where(first, x2, 0.0), jnp.where(first, 0.0, x2)], axis=0)))
    y = jnp.concatenate(ys, axis=1)
    y_in = jnp.concatenate([_dot(cm[:, g * N:(g + 1) * N], s_t[:, g * W:(g + 1) * W]) for g in range(G)], axis=1)
    y = y + y_in * jnp.exp(a_cs_x) + d_x * xs
    xw = xs * (jnp.exp(a_end_x - a_cs_x) * dt_x)
    st = jnp.concatenate([_dot(bm[:, g * N:(g + 1) * N], xw[:, g * W:(g + 1) * W], "tn") for g in range(G)], axis=1)
    s_new = s_t * jnp.exp(a_end_x) + st
    yg = y * jax.nn.silu(z)
    outs = []
    for g in range(G):
        part = yg[:, g * W:(g + 1) * W]
        outs.append(part * lax.rsqrt(jnp.mean(part * part, axis=1, keepdims=True) + RMS_EPS))
    return (jnp.concatenate(outs, axis=1) * norm_w,), s_new


GDN_PREP_CHUNKS = 8


def _gdn_prep(xs_, ps_):
    qkv, ab = xs_
    a_log, dt_bias = ps_
    B = qkv.shape[0]
    H, D, L = GDN_HEADS, GDN_HEAD_DIM, GDN_CHUNK
    g_all = -jnp.exp(a_log) * jax.nn.softplus(ab + dt_bias)
    row, col = _iota2((B, B), 0), _iota2((B, B), 1)
    g_cs = _mask_left((((row // L) == (col // L)) & (row >= col)).astype(F32), g_all)
    g_cs_t = g_cs.T
    beta_all = jax.nn.sigmoid(ab)
    incl, strict = _tril(L), _tril(L, strict=True)
    qs, ks, vs = [], [], []
    for h in range(H):
        q = qkv[:, h * D:(h + 1) * D]
        k = qkv[:, GDN_WIDTH + h * D:GDN_WIDTH + (h + 1) * D]
        qs.append(q * lax.rsqrt(jnp.sum(q * q, axis=1, keepdims=True) + RMS_EPS) * (D ** -0.5))
        ks.append(k * lax.rsqrt(jnp.sum(k * k, axis=1, keepdims=True) + RMS_EPS))
        vs.append(qkv[:, 2 * GDN_WIDTH + h * D:2 * GDN_WIDTH + (h + 1) * D])
    pairs = [(c, h) for c in range(B // L) for h in range(H)]
    rows = {c: slice(c * L, (c + 1) * L) for c in range(B // L)}
    q_ = {(c, h): qs[h][rows[c]] for c, h in pairs}
    k_ = {(c, h): ks[h][rows[c]] for c, h in pairs}
    col_ = {(c, h): g_cs[rows[c], h:h + 1] for c, h in pairs}
    beta_ = {(c, h): beta_all[rows[c], H + h:H + h + 1] for c, h in pairs}
    gamma = {p: jnp.exp(jnp.where(incl, col_[p] - g_cs_t[p[1]:p[1] + 1, rows[p[0]]], NEG_BIG)) for p in pairs}
    kb = {p: k_[p] * beta_[p] for p in pairs}
    a_mat = [jnp.where(strict, _dot(kb[p], k_[p], "nt") * gamma[p], 0.0) for p in pairs]
    attn = {p: jnp.where(incl, _dot(q_[p], k_[p], "nt") * gamma[p], 0.0) for p in pairs}
    t_mat = dict(zip(pairs, _unit_lower_inverses(a_mat)))
    u = {p: _dot(t_mat[p], vs[p[1]][rows[p[0]]] * beta_[p]) for p in pairs}
    w = {p: _dot(t_mat[p], kb[p] * jnp.exp(col_[p])) for p in pairs}
    qd = {p: q_[p] * jnp.exp(col_[p]) for p in pairs}
    kd = {p: k_[p] * jnp.exp(col_[p][L - 1:L, :] - col_[p]) for p in pairs}

    def whole(parts):
        return jnp.concatenate([jnp.concatenate([parts[(c, h)] for h in range(H)], axis=1) for c in range(B // L)], axis=0)

    return (whole(u), whole(w), whole(qd), whole(kd), whole(attn), g_cs)


def _gdn_scan(xs_, ps_, s):
    u, w, qd, kd, attn, g_cs, gate = xs_
    (norm_w,) = ps_
    L = u.shape[0]
    H, D = GDN_HEADS, GDN_HEAD_DIM
    heads = range(H)
    lanes = [slice(h * D, (h + 1) * D) for h in heads]
    s_h = [s[lanes[h], :] for h in heads]
    v_new = [u[:, lanes[h]] - _dot(w[:, lanes[h]], s_h[h]) for h in heads]
    o = [_dot(qd[:, lanes[h]], s_h[h]) + _dot(attn[:, h * L:(h + 1) * L], v_new[h]) for h in heads]
    decay = [jnp.exp(g_cs[L - 1:L, h:h + 1]) for h in heads]
    s_new = [s_h[h] * decay[h] + _dot(kd[:, lanes[h]], v_new[h], "tn") for h in heads]
    o = [o[h] * lax.rsqrt(jnp.mean(o[h] * o[h], axis=1, keepdims=True) + RMS_EPS) * norm_w * jax.nn.silu(gate[:, lanes[h]])
         for h in heads]
    return (jnp.concatenate(o, axis=1),), jnp.concatenate(s_new, axis=0)


def _gdn_forward(tag, gqkv, h, sp):
    T = gqkv.shape[0]
    blk = min(GDN_PREP_CHUNKS * GDN_CHUNK, T)
    prep_in = [(gqkv, blk, 3 * GDN_WIDTH, 0), _seg_blk(h, "gab", blk)]
    prep_p = [_lane_pad(sp["gdn_a_log"]), _lane_pad(sp["gdn_dt_bias"])]
    mx = MXU_DTYPE
    prep = _chain_fwd(f"gdn_prep_{tag}", _gdn_prep, T // blk, prep_in, prep_p,
                      [(blk, GDN_WIDTH, F32), (blk, GDN_WIDTH, mx), (blk, GDN_WIDTH, mx), (blk, GDN_WIDTH, mx),
                       (blk, GDN_HEADS * GDN_CHUNK, mx), (blk, LANES, F32)])
    widths = [GDN_WIDTH] * 4 + [GDN_HEADS * GDN_CHUNK, LANES]
    scan_in = [(a, GDN_CHUNK, wd, 0) for a, wd in zip(prep, widths)] + [_seg_blk(h, "gg", GDN_CHUNK)]
    scan_p = [sp["gdn_norm_w"]]
    o, states = _chain_fwd(f"gdn_scan_{tag}", _gdn_scan, T // GDN_CHUNK, scan_in, scan_p, [(GDN_CHUNK, GDN_WIDTH, mx)],
                           (GDN_WIDTH, GDN_HEAD_DIM))
    return o, dict(prep_in=prep_in, prep_p=prep_p, scan_in=scan_in, scan_p=scan_p, states=states, widths=widths)


def _gdn_backward(tag, do, sv, dx_dtype):
    T = do.shape[0]
    blk = min(GDN_PREP_CHUNKS * GDN_CHUNK, T)
    dscan, (dnorm,) = _chain_bwd(f"gdn_scan_bwd_{tag}", _gdn_scan, T // GDN_CHUNK, sv["scan_in"], sv["scan_p"],
                                 [(do, GDN_CHUNK, GDN_WIDTH)], sprev=sv["states"], dx_dtypes=[F32] * 6 + [dx_dtype])
    douts = [(d, blk, wd) for d, wd in zip(dscan[:6], sv["widths"])]
    (dgqkv, dgab), (da_log, ddt_bias) = _chain_bwd(f"gdn_prep_bwd_{tag}", _gdn_prep, T // blk, sv["prep_in"], sv["prep_p"],
                                                   douts, dx_dtypes=[F32, dx_dtype])
    return dgqkv, dgab, dscan[6], da_log[:, :GDN_HEADS], ddt_bias[:, :GDN_HEADS], dnorm


def _gla_block(xs_, ps_, s_t):
    qkv, glr, r = xs_
    w2, gate_b, norm_w = ps_
    B = qkv.shape[0]
    H, K, V, C = GLA_HEADS, GLA_KEY_DIM, GLA_VAL_DIM, GLA_CHUNK
    q = qkv[:, :GLA_K] * (K ** -0.5)
    k = qkv[:, GLA_K:2 * GLA_K]
    v = qkv[:, 2 * GLA_K:]
    gk = jax.nn.log_sigmoid(_dot(glr, w2) + gate_b) / GLA_NORMALIZER
    row, col = _iota2((B, B), 0), _iota2((B, B), 1)
    same = (row // C) == (col // C)
    mask = same & (row >= col)
    b_cs = _mask_left(mask.astype(F32), gk)
    b_end = _mask_left((col == (row // C) * C + (C - 1)).astype(F32), b_cs)
    q_e = q * jnp.exp(b_cs)
    k_e = k * jnp.exp(-b_cs)
    k_d = k * jnp.exp(b_end - b_cs)
    intra = []
    for h in range(H):
        a_mat = jnp.where(mask, _dot(q_e[:, h * K:(h + 1) * K], k_e[:, h * K:(h + 1) * K], "nt"), 0.0)
        intra.append(_dot(a_mat, v[:, h * V:(h + 1) * V]))
    o = jnp.concatenate(intra, axis=1)
    chunks = [slice(j * C, (j + 1) * C) for j in range(B // C)]
    fresh = [jnp.concatenate([_dot(v[sl, h * V:(h + 1) * V], k_d[sl, h * K:(h + 1) * K], "tn") for h in range(H)], axis=1)
             for sl in chunks]
    entering = []
    for j, sl in enumerate(chunks):
        entering.append(s_t)
        s_t = s_t * jnp.exp(b_end[j * C:j * C + 1, :]) + fresh[j]
    inter = [jnp.concatenate([_dot(q_e[sl, h * K:(h + 1) * K], entering[j][:, h * K:(h + 1) * K], "nt") for h in range(H)],
                             axis=1) for j, sl in enumerate(chunks)]
    o = o + jnp.concatenate(inter, axis=0)
    outs = []
    for h in range(H):
        oh = o[:, h * V:(h + 1) * V]
        oh = oh * lax.rsqrt(jnp.mean(oh * oh, axis=1, keepdims=True) + RMS_EPS) * norm_w
        outs.append(oh * jax.nn.silu(r[:, h * V:(h + 1) * V]))
    return (jnp.concatenate(outs, axis=1),), s_t


def _merge_fn(xs_, ps_):
    gates, y_ssd, y_gdn, y_gla = xs_
    d = D_MODEL
    return (jax.nn.sigmoid(gates[:, :d]) * y_ssd + jax.nn.sigmoid(gates[:, d:2 * d]) * y_gdn
            + jax.nn.sigmoid(gates[:, 2 * d:]) * y_gla,)


def _ln_fn(xs_, ps_):
    x, r = xs_
    g, b = ps_
    t = ALPHA * x + r
    mu = jnp.mean(t, axis=1, keepdims=True)
    var = jnp.mean(jnp.square(t - mu), axis=1, keepdims=True)
    return ((t - mu) * lax.rsqrt(var + LN_EPS) * g + b,)


def _row_spec(rows, width, colblk, n, reverse):
    if reverse:
        return pl.BlockSpec((rows, width), lambda c: (n - 1 - c, colblk))
    return pl.BlockSpec((rows, width), lambda c: (c, colblk))


def _full_spec(shape):
    zeros = (0,) * len(shape)
    return pl.BlockSpec(shape, lambda c: zeros)


def _chain_fwd(name, fn, n, blocked, full, out_defs, state_shape=None):
    nb, nf, no = len(blocked), len(full), len(out_defs)

    def body(*refs):
        xs = [r[...].astype(F32) for r in refs[:nb]]
        ps = [r[...] for r in refs[nb:nb + nf]]
        o_refs = refs[nb + nf:nb + nf + no]
        if state_shape is None:
            outs = fn(xs, ps)
        else:
            sprev_ref, s_ref = refs[nb + nf + no:]

            @pl.when(pl.program_id(0) == 0)
            def _():
                s_ref[...] = jnp.zeros_like(s_ref)

            s = s_ref[...]
            sprev_ref[0] = s
            outs, s_new = fn(xs, ps, s)
            s_ref[...] = s_new
        for r, o in zip(o_refs, outs):
            r[...] = o.astype(r.dtype)

    in_specs = [_row_spec(rows, width, cb, n, False) for _, rows, width, cb in blocked]
    in_specs += [_full_spec(a.shape) for a in full]
    out_specs = [_row_spec(rows, width, 0, n, False) for rows, width, _ in out_defs]
    out_shape = [jax.ShapeDtypeStruct((n * rows, width), dt) for rows, width, dt in out_defs]
    scratch = []
    if state_shape is not None:
        out_specs.append(pl.BlockSpec((1,) + state_shape, lambda c: (c, 0, 0)))
        out_shape.append(jax.ShapeDtypeStruct((n,) + state_shape, F32))
        scratch.append(pltpu.VMEM(state_shape, F32))
    return pl.pallas_call(body, name=name, grid=(n,), in_specs=in_specs, out_specs=out_specs, out_shape=out_shape,
                          scratch_shapes=scratch, compiler_params=_cparams(("arbitrary",)))(
        *[a for a, _, _, _ in blocked], *full)


def _chain_bwd(name, fn, n, blocked, full, douts, sprev=None, dx_dtypes=None):
    nb, nf, nd = len(blocked), len(full), len(douts)
    has_state = sprev is not None
    dx_dtypes = dx_dtypes or [F32] * nb

    def body(*refs):
        pos = 0
        b_refs = refs[pos:pos + nb]; pos += nb
        f_refs = refs[pos:pos + nf]; pos += nf
        d_refs = refs[pos:pos + nd]; pos += nd
        if has_state:
            sprev_ref = refs[pos]; pos += 1
        dx_refs = refs[pos:pos + nb]; pos += nb
        dp_refs = refs[pos:pos + nf]; pos += nf
        if has_state:
            ds_ref = refs[pos]

        @pl.when(pl.program_id(0) == 0)
        def _():
            for r in dp_refs:
                r[...] = jnp.zeros_like(r)
            if has_state:
                ds_ref[...] = jnp.zeros_like(ds_ref)

        xs = [r[...].astype(F32) for r in b_refs]
        ps = [r[...] for r in f_refs]
        dys = tuple(r[...].astype(F32) for r in d_refs)
        if has_state:
            _, vjp = jax.vjp(fn, xs, ps, sprev_ref[0])
            dxs, dps, ds = vjp((dys, ds_ref[...]))
            ds_ref[...] = ds
        else:
            _, vjp = jax.vjp(fn, xs, ps)
            dxs, dps = vjp(dys)
        for r, d in zip(dx_refs, dxs):
            r[...] = d.astype(r.dtype)
        for r, d in zip(dp_refs, dps):
            r[...] += d

    in_specs = [_row_spec(rows, width, cb, n, True) for _, rows, width, cb in blocked]
    in_specs += [_full_spec(a.shape) for a in full]
    in_specs += [_row_spec(rows, width, 0, n, True) for _, rows, width in douts]
    args = [a for a, _, _, _ in blocked] + list(full) + [a for a, _, _ in douts]
    scratch = []
    if has_state:
        st_shape = sprev.shape[1:]
        in_specs.append(pl.BlockSpec((1,) + st_shape, lambda c: (n - 1 - c, 0, 0)))
        args.append(sprev)
        scratch.append(pltpu.VMEM(st_shape, F32))
    out_specs = [_row_spec(rows, width, 0, n, True) for _, rows, width, _ in blocked]
    out_specs += [_full_spec(a.shape) for a in full]
    out_shape = [jax.ShapeDtypeStruct((n * rows, width), dt) for (_, rows, width, _), dt in zip(blocked, dx_dtypes)]
    out_shape += [jax.ShapeDtypeStruct(a.shape, F32) for a in full]
    res = pl.pallas_call(body, name=name, grid=(n,), in_specs=in_specs, out_specs=out_specs, out_shape=out_shape,
                         scratch_shapes=scratch, compiler_params=_cparams(("arbitrary",)))(*args)
    return res[:nb], res[nb:]


def _tile(n, target, unit):
    if n <= target:
        return n
    best = None
    for t in range(unit, target + 1, unit):
        if n % t == 0:
            best = t
    assert best is not None, (n, target, unit)
    return best


def _mm(name, a, b, dims="nn", out_dtype=F32, tm=2048, tn=512, tk=2048, after=None):
    if dims == "nn":
        (M, K), (_, N) = a.shape, b.shape
    elif dims == "nt":
        (M, K), (N, _) = a.shape, b.shape
    else:
        (K, M), (_, N) = a.shape, b.shape
    tm, tn, tk = _tile(M, tm, LANES), _tile(N, tn, LANES), _tile(K, tk, LANES)
    nk = K // tk
    extra = [] if after is None else [after]

    def body(*refs):
        a_ref, b_ref = refs[:2]
        o_ref, acc_ref = refs[-2:]
        part = _dot(a_ref[...], b_ref[...], dims)
        if nk == 1:
            o_ref[...] = part.astype(o_ref.dtype)
            return
        k = pl.program_id(2)

        @pl.when(k == 0)
        def _():
            acc_ref[...] = part

        @pl.when((k > 0) & (k < nk - 1))
        def _():
            acc_ref[...] += part

        @pl.when(k == nk - 1)
        def _():
            o_ref[...] = (acc_ref[...] + part).astype(o_ref.dtype)

    if dims == "tn":
        a_spec = pl.BlockSpec((tk, tm), lambda j, i, k: (k, i))
    else:
        a_spec = pl.BlockSpec((tm, tk), lambda j, i, k: (i, k))
    if dims == "nt":
        b_spec = pl.BlockSpec((tn, tk), lambda j, i, k: (j, k))
    else:
        b_spec = pl.BlockSpec((tk, tn), lambda j, i, k: (k, j))
    return pl.pallas_call(
        body, name=name, grid=(N // tn, M // tm, nk), in_specs=[a_spec, b_spec] + [ANY] * len(extra),
        out_specs=pl.BlockSpec((tm, tn), lambda j, i, k: (i, j)), out_shape=jax.ShapeDtypeStruct((M, N), out_dtype),
        scratch_shapes=[pltpu.VMEM((tm, tn) if nk > 1 else (8, LANES), F32)],
        compiler_params=_cparams(("parallel", "parallel", "arbitrary")))(a, b, *extra)


CONV_CB = 256


def _shift_down(x, k):
    if k == 0:
        return x
    return jnp.where(_iota2(x.shape, 0) >= k, pltpu.roll(x, k, 0), 0.0)


def _shift_up(x, k):
    if k == 0:
        return x
    t = x.shape[0]
    return jnp.where(_iota2(x.shape, 0) < t - k, pltpu.roll(x, t - k, 0), 0.0)


def _conv_pre(x, w, b):
    kk = w.shape[0]
    pre = x * w[kk - 1:kk, :]
    for k in range(kk - 1):
        pre = pre + _shift_down(x, kk - 1 - k) * w[k:k + 1, :]
    return pre if b is None else pre + b


EDGE = 16


def _conv_pre_rot(x, w, b):
    kk = w.shape[0]
    pre = x * w[kk - 1:kk, :]
    for k in range(kk - 1):
        pre = pre + pltpu.roll(x, kk - 1 - k, 0) * w[k:k + 1, :]
    return pre if b is None else pre + b


def _conv_t_local(d, w):
    kk = w.shape[0]
    out = d * w[kk - 1:kk, :]
    for k in range(kk - 1):
        out = out + _shift_up(d, kk - 1 - k) * w[k:k + 1, :]
    return out


def _col_sum(a):
    return jnp.sum(a, axis=0, keepdims=True)


def _conv_bwd_rot(x_ref, w, dpre, dpre_head, dx_ref, dw_ref, db_ref):
    T = dpre.shape[0]
    kk = w.shape[0]
    x = x_ref[...]
    x_head, x_tail = x_ref[0:EDGE, :], x_ref[T - EDGE:T, :]
    wrong_head = dpre[0:EDGE]
    dx = dpre * w[kk - 1:kk, :]
    for k in range(kk - 1):
        dx = dx + pltpu.roll(dpre, T - (kk - 1 - k), 0) * w[k:k + 1, :]
    dx_ref[...] = dx.astype(dx_ref.dtype)
    top = jnp.concatenate([dpre_head, dpre[EDGE:2 * EDGE]], axis=0)
    dx_ref[0:EDGE, :] = _conv_t_local(top, w)[0:EDGE].astype(dx_ref.dtype)
    dx_ref[T - EDGE:T, :] = _conv_t_local(dpre[T - EDGE:T], w).astype(dx_ref.dtype)
    ends = jnp.concatenate([x_tail, x_head], axis=0)
    dw_ref[kk - 1:kk, :] = _col_sum(dpre * x) + _col_sum((dpre_head - wrong_head) * x_head)
    for k in range(kk - 1):
        s = kk - 1 - k
        rotated_head = pltpu.roll(ends, s, 0)[EDGE:2 * EDGE]
        dw_ref[k:k + 1, :] = (_col_sum(dpre * pltpu.roll(x, s, 0)) - _col_sum(wrong_head * rotated_head)
                              + _col_sum(dpre_head * _shift_down(x_head, s)))
    if db_ref is not None:
        db_ref[...] = _col_sum(dpre) + _col_sum(dpre_head - wrong_head)


def _dsilu(pre):
    sg = jax.nn.sigmoid(pre)
    return sg * (1.0 + pre * (1.0 - sg))


def _conv_silu_fwd(name, src, col0, w, b):
    T = src.shape[0]
    kk, C = w.shape
    cb = CONV_CB
    off = col0 // cb

    def body(*refs):
        x_ref, w_ref, o_ref = refs[0], refs[1], refs[-1]
        b_val = refs[2][...] if b is not None else None
        o_ref[...] = jax.nn.silu(_conv_pre_rot(x_ref[...], w_ref[...], b_val))
        o_ref[0:EDGE, :] = jax.nn.silu(_conv_pre(x_ref[0:EDGE, :], w_ref[...], b_val))

    in_specs = [pl.BlockSpec((T, cb), lambda j: (0, off + j)), pl.BlockSpec((kk, cb), lambda j: (0, j))]
    args = [src, w]
    if b is not None:
        in_specs.append(pl.BlockSpec((1, cb), lambda j: (0, j)))
        args.append(b)
    return pl.pallas_call(body, name=name, grid=(C // cb,), in_specs=in_specs,
                          out_specs=pl.BlockSpec((T, cb), lambda j: (0, j)), out_shape=jax.ShapeDtypeStruct((T, C), F32),
                          compiler_params=_cparams(("parallel",)))(*args)


def _conv_silu_bwd(name, src, col0, w, b, dy, dx_dtype):
    T = src.shape[0]
    kk, C = w.shape
    cb = CONV_CB
    off = col0 // cb
    has_b = b is not None

    def body(*refs):
        x_ref, w_ref = refs[:2]
        pos = 2
        b_val = None
        if has_b:
            b_val = refs[pos][...]; pos += 1
        dy_ref = refs[pos]; pos += 1
        dx_ref, dw_ref = refs[pos], refs[pos + 1]
        db_ref = refs[pos + 2] if has_b else None
        wv = w_ref[...]
        dpre = dy_ref[...] * _dsilu(_conv_pre_rot(x_ref[...], wv, b_val))
        dpre_head = dy_ref[0:EDGE, :] * _dsilu(_conv_pre(x_ref[0:EDGE, :], wv, b_val))
        _conv_bwd_rot(x_ref, wv, dpre, dpre_head, dx_ref, dw_ref, db_ref)

    in_specs = [pl.BlockSpec((T, cb), lambda j: (0, off + j)), pl.BlockSpec((kk, cb), lambda j: (0, j))]
    args = [src, w]
    if has_b:
        in_specs.append(pl.BlockSpec((1, cb), lambda j: (0, j)))
        args.append(b)
    in_specs.append(pl.BlockSpec((T, cb), lambda j: (0, j)))
    args.append(dy)
    out_specs = [pl.BlockSpec((T, cb), lambda j: (0, j)), pl.BlockSpec((kk, cb), lambda j: (0, j))]
    out_shape = [jax.ShapeDtypeStruct((T, C), dx_dtype), jax.ShapeDtypeStruct((kk, C), F32)]
    if has_b:
        out_specs.append(pl.BlockSpec((1, cb), lambda j: (0, j)))
        out_shape.append(jax.ShapeDtypeStruct((1, C), F32))
    return pl.pallas_call(body, name=name, grid=(C // cb,), in_specs=in_specs, out_specs=out_specs, out_shape=out_shape,
                          compiler_params=_cparams(("parallel",)))(*args)


def _ffn_glu_fwd(name, up, w, b, out_dtype=F32):
    T = up.shape[0]
    kk = w.shape[0]
    cb = CONV_CB
    width = up.shape[1] // 2
    nblk = width // cb

    def body(g_ref, u_ref, wg_ref, wu_ref, bg_ref, bu_ref, o_ref):
        g = _conv_pre_rot(g_ref[...], wg_ref[...], bg_ref[...])
        u = _conv_pre_rot(u_ref[...], wu_ref[...], bu_ref[...])
        o_ref[...] = (jax.nn.silu(g) * u).astype(o_ref.dtype)
        g = _conv_pre(g_ref[0:EDGE, :], wg_ref[...], bg_ref[...])
        u = _conv_pre(u_ref[0:EDGE, :], wu_ref[...], bu_ref[...])
        o_ref[0:EDGE, :] = (jax.nn.silu(g) * u).astype(o_ref.dtype)

    lo, hi = (lambda j: (0, j)), (lambda j: (0, nblk + j))
    in_specs = [pl.BlockSpec((T, cb), lo), pl.BlockSpec((T, cb), hi), pl.BlockSpec((kk, cb), lo), pl.BlockSpec((kk, cb), hi),
                pl.BlockSpec((1, cb), lo), pl.BlockSpec((1, cb), hi)]
    return pl.pallas_call(body, name=name, grid=(nblk,), in_specs=in_specs, out_specs=pl.BlockSpec((T, cb), lo),
                          out_shape=jax.ShapeDtypeStruct((T, width), out_dtype),
                          compiler_params=_cparams(("parallel",)))(up, up, w, w, b, b)


def _ffn_glu_bwd(name, up, w, b, dact, dx_dtype):
    T = up.shape[0]
    kk = w.shape[0]
    cb = CONV_CB
    width = up.shape[1] // 2
    nblk = width // cb

    def body(g_ref, u_ref, wg_ref, wu_ref, bg_ref, bu_ref, d_ref, dg_ref, du_ref, dwg_ref, dwu_ref, dbg_ref, dbu_ref):
        wg, wu = wg_ref[...], wu_ref[...]
        g = _conv_pre_rot(g_ref[...], wg, bg_ref[...])
        u = _conv_pre_rot(u_ref[...], wu, bu_ref[...])
        d = d_ref[...].astype(F32)
        g_head = _conv_pre(g_ref[0:EDGE, :], wg, bg_ref[...])
        u_head = _conv_pre(u_ref[0:EDGE, :], wu, bu_ref[...])
        d_head = d_ref[0:EDGE, :].astype(F32)
        sg, sg_head = jax.nn.sigmoid(g), jax.nn.sigmoid(g_head)
        _conv_bwd_rot(g_ref, wg, d * u * (sg * (1.0 + g * (1.0 - sg))),
                      d_head * u_head * (sg_head * (1.0 + g_head * (1.0 - sg_head))), dg_ref, dwg_ref, dbg_ref)
        _conv_bwd_rot(u_ref, wu, d * (g * sg), d_head * (g_head * sg_head), du_ref, dwu_ref, dbu_ref)

    lo, hi = (lambda j: (0, j)), (lambda j: (0, nblk + j))
    in_specs = [pl.BlockSpec((T, cb), lo), pl.BlockSpec((T, cb), hi), pl.BlockSpec((kk, cb), lo), pl.BlockSpec((kk, cb), hi),
                pl.BlockSpec((1, cb), lo), pl.BlockSpec((1, cb), hi), pl.BlockSpec((T, cb), lo)]
    out_specs = [pl.BlockSpec((T, cb), lo)] * 2 + [pl.BlockSpec((kk, cb), lo)] * 2 + [pl.BlockSpec((1, cb), lo)] * 2
    out_shape = ([jax.ShapeDtypeStruct((T, width), dx_dtype)] * 2 + [jax.ShapeDtypeStruct((kk, width), F32)] * 2
                 + [jax.ShapeDtypeStruct((1, width), F32)] * 2)
    return pl.pallas_call(body, name=name, grid=(nblk,), in_specs=in_specs, out_specs=out_specs, out_shape=out_shape,
                          compiler_params=_cparams(("parallel",)))(up, up, w, w, b, b, dact)


def _loss_head(y, target):
    T, D = y.shape
    tb = _tile(T, 256, 8)

    def body(y_ref, t_ref, dy_ref, l_ref):
        @pl.when(pl.program_id(0) == 0)
        def _():
            l_ref[...] = jnp.zeros_like(l_ref)

        err = y_ref[...] - t_ref[...]
        dy_ref[...] = err * (1.0 / D)
        l_ref[...] += jnp.sum(err * err, axis=0, keepdims=True) * (0.5 / D)

    spec = pl.BlockSpec((tb, D), lambda i: (i, 0))
    return pl.pallas_call(body, name="loss_head", grid=(T // tb,), in_specs=[spec, spec],
                          out_specs=[spec, pl.BlockSpec((1, D), lambda i: (0, 0))],
                          out_shape=[jax.ShapeDtypeStruct((T, D), F32), jax.ShapeDtypeStruct((1, D), F32)],
                          compiler_params=_cparams(("arbitrary",)))(y, target)


def _adamw_math(w, g, m, v):
    m = ADAM_B1 * m + (1.0 - ADAM_B1) * g
    v = ADAM_B2 * v + (1.0 - ADAM_B2) * jnp.square(g)
    m_hat = m / (1.0 - ADAM_B1 ** ADAM_STEP)
    v_hat = v / (1.0 - ADAM_B2 ** ADAM_STEP)
    return -ADAM_LR * (m_hat / (jnp.sqrt(v_hat) + ADAM_EPS) + ADAM_WD * w), m, v


def _adamw(name, w, g, m, v, after=None):
    A, R, C = w.shape
    if C % LANES == 0:
        rb, cb = _slab(R, C)
    else:
        rb, cb = _tile(R, max(8, SLAB_BYTES // 2 // (C * 4) // 8 * 8), 8), C
    extra = [] if after is None else [after]

    def body(w_ref, g_ref, m_ref, v_ref, *rest):
        d_ref, mo_ref, vo_ref = rest[-3:]
        d, mn, vn = _adamw_math(w_ref[...], g_ref[...], m_ref[...], v_ref[...])
        d_ref[...] = d
        mo_ref[...] = mn
        vo_ref[...] = vn

    spec = pl.BlockSpec((1, rb, cb), lambda a, r, q: (a, r, q))
    return pl.pallas_call(body, name=name, grid=(A, R // rb, C // cb), in_specs=[spec] * 4 + [ANY] * len(extra),
                          out_specs=[spec] * 3, out_shape=[jax.ShapeDtypeStruct(w.shape, F32)] * 3,
                          compiler_params=_cparams(("parallel", "parallel", "parallel")))(w, g, m, v, *extra)


def _adamw_small(parts, w, m, v):
    def body(p_ref, w_ref, m_ref, v_ref, g_ref, d_ref, mo_ref, vo_ref):
        g = p_ref[0]
        for i in range(1, N_DEV):
            g = g + p_ref[i]
        d, mn, vn = _adamw_math(w_ref[...], g, m_ref[...], v_ref[...])
        g_ref[...] = g
        d_ref[...] = d
        mo_ref[...] = mn
        vo_ref[...] = vn

    return pl.pallas_call(body, name="adamw_small", out_shape=[jax.ShapeDtypeStruct(w.shape, F32)] * 4,
                          compiler_params=_cparams())(parts, w, m, v)


def _add_blocks(name, a, b, out_dtype=F32):
    n, R, W = a.shape
    rb = _tile(R, 512, 8)

    def body(a_ref, b_ref, o_ref):
        o_ref[...] = (a_ref[...].astype(F32) + b_ref[...].astype(F32)).astype(o_ref.dtype)

    spec = pl.BlockSpec((1, rb, W), lambda i, r: (i, r, 0))
    return pl.pallas_call(body, name=name, grid=(n, R // rb), in_specs=[spec, spec], out_specs=spec,
                          out_shape=jax.ShapeDtypeStruct(a.shape, out_dtype),
                          compiler_params=_cparams(("parallel", "parallel")))(a, b)


SLAB_BYTES = 5 << 19


def _slab(R, W):
    if R % 16 == 0:
        return _tile(R, max(16, SLAB_BYTES // (4 * W) // 16 * 16), 16), W
    assert W % LANES == 0, (R, W)
    return R, _tile(W, max(LANES, SLAB_BYTES // (4 * R) // LANES * LANES), LANES)


def _pair_add(name, g, other, c, chip):
    _, R, W = g.shape
    rb, cb = _slab(R, W)

    def body(s_ref, a_ref, b_ref, send_ref, own_ref):
        s = a_ref[0] + b_ref[0]
        send_ref[0] = s.astype(send_ref.dtype)

        @pl.when(pl.program_id(2) == s_ref[1])
        def _():
            own_ref[...] = s

    grid_spec = pltpu.PrefetchScalarGridSpec(
        num_scalar_prefetch=1, grid=(R // rb, W // cb, 4),
        in_specs=[pl.BlockSpec((1, rb, cb), lambda r, q, p, s_ref: (2 * p + s_ref[0], r, q)),
                  pl.BlockSpec((1, rb, cb), lambda r, q, p, s_ref: (p, r, q))],
        out_specs=[pl.BlockSpec((1, rb, cb), lambda r, q, p, s_ref: (p, r, q)),
                   pl.BlockSpec((rb, cb), lambda r, q, p, s_ref: (r, q))])
    scalars = jnp.stack([c, chip]).astype(jnp.int32)
    return pl.pallas_call(body, name=name, grid_spec=grid_spec,
                          out_shape=[jax.ShapeDtypeStruct((4, R, W), MXU_DTYPE), jax.ShapeDtypeStruct((R, W), F32)],
                          compiler_params=_cparams(("parallel", "parallel", "arbitrary")))(scalars, g, other)


def _sum4(name, own, parts):
    R, W = own.shape
    rb, cb = _slab(R, W)

    def body(o_ref, p_ref, out_ref):
        out_ref[...] = ((o_ref[...] + p_ref[0].astype(F32)) + p_ref[1].astype(F32)) + p_ref[2].astype(F32)

    return pl.pallas_call(body, name=name, grid=(R // rb, W // cb),
                          in_specs=[pl.BlockSpec((rb, cb), lambda r, q: (r, q)), pl.BlockSpec((3, rb, cb), lambda r, q: (0, r, q))],
                          out_specs=pl.BlockSpec((rb, cb), lambda r, q: (r, q)), out_shape=jax.ShapeDtypeStruct((R, W), F32),
                          compiler_params=_cparams(("parallel", "parallel")))(own, parts)


MESH = pl.DeviceIdType.MESH
ANY = pl.BlockSpec(memory_space=pl.ANY)


def _place():
    return lax.axis_index("x"), lax.axis_index("y"), lax.axis_index("c")


def _other_chips(x, y):
    return [(1 - x, y), (x, 1 - y), (1 - x, 1 - y)]


def _all_gather(name, blocks):
    n = len(blocks)

    def body(*refs):
        x_refs, out_refs = refs[:n], refs[n:2 * n]
        send_sems, recv_sems, local_sems = refs[2 * n:]
        x, y, c = _place()
        me, sibling = (x, y, c), (x, y, 1 - c)
        chips = _other_chips(x, y)

        def slot(a, px, py, pc):
            return out_refs[a].at[4 * px + 2 * py + pc]

        def copy(a, k, blk, to, src=None):
            return pltpu.make_async_remote_copy(src_ref=slot(a, *blk) if src is None else src, dst_ref=slot(a, *blk),
                                                send_sem=send_sems.at[a, k], recv_sem=recv_sems.at[a, k],
                                                device_id=to, device_id_type=MESH)

        mine = [pltpu.make_async_copy(x_refs[a], slot(a, *me), local_sems.at[a]) for a in range(n)]
        for cp in mine:
            cp.start()
        first = []
        for j, chip in enumerate(chips):
            first += [copy(a, 1 + j, me, (*chip, c), src=x_refs[a]) for a in range(n)]
        first += [copy(a, 0, me, sibling, src=x_refs[a]) for a in range(n)]
        for cp in first:
            cp.start()
        passed = []
        for j, chip in enumerate(chips):
            for a in range(n):
                copy(a, 1 + j, (*chip, c), me).wait_recv()
                passed.append(copy(a, 4 + j, (*chip, c), sibling))
                passed[-1].start()
        for a in range(n):
            copy(a, 0, sibling, me).wait_recv()
        for j, chip in enumerate(chips):
            for a in range(n):
                copy(a, 4 + j, (*chip, 1 - c), me).wait_recv()
        for cp in first + passed:
            cp.wait_send()
        for cp in mine:
            cp.wait()

    return pl.pallas_call(body, name=name, in_specs=[ANY] * n, out_specs=[ANY] * n,
                          out_shape=[jax.ShapeDtypeStruct((N_DEV,) + b.shape, b.dtype) for b in blocks],
                          scratch_shapes=[pltpu.SemaphoreType.DMA((n, 7)), pltpu.SemaphoreType.DMA((n, 7)),
                                          pltpu.SemaphoreType.DMA((n,))])(*blocks)


def _routes_to_sibling(x, y, c):
    return [(2 * p + (1 - c), p, (x, y, 1 - c)) for p in range(4)]


def _routes_to_chips(x, y, c):
    return [(2 * px + py, j, (px, py, c)) for j, (px, py) in enumerate(_other_chips(x, y))]


def _routes_block_to_chips(x, y, c):
    me = 4 * x + 2 * y + c
    return [(me, me, (px, py, c)) for px, py in _other_chips(x, y)]


def _routes_blocks_to_sibling(x, y, c):
    return [(4 * px + 2 * py + c, 4 * px + 2 * py + c, (x, y, 1 - c)) for px, py in [(x, y)] + _other_chips(x, y)]


def _route_copies(routes, src_refs, land_refs, send_sems, recv_sems):
    x, y, c = _place()
    copies = []
    for a, (src, land) in enumerate(zip(src_refs, land_refs)):
        plan = routes(x, y, c)
        for k, (s, d, target) in enumerate(plan):
            i = a * len(plan) + k
            copies.append(pltpu.make_async_remote_copy(src_ref=src.at[s], dst_ref=land.at[d], send_sem=send_sems.at[i],
                                                       recv_sem=recv_sems.at[i], device_id=target, device_id_type=MESH))
    return copies


def _exchange(name, routes, n_routes, srcs, land_slots):
    n = len(srcs)

    def body(*refs):
        copies = _route_copies(routes, refs[:n], refs[n:2 * n], refs[2 * n], refs[2 * n + 1])
        for cp in copies:
            cp.start()
        for cp in copies:
            cp.wait_recv()
        for cp in copies:
            cp.wait_send()

    return pl.pallas_call(body, name=name, in_specs=[ANY] * n, out_specs=[ANY] * n,
                          out_shape=[jax.ShapeDtypeStruct((land_slots,) + s.shape[1:], s.dtype) for s in srcs],
                          scratch_shapes=[pltpu.SemaphoreType.DMA((n * n_routes,)), pltpu.SemaphoreType.DMA((n * n_routes,))])(*srcs)


HBM_SPEC = pl.BlockSpec(memory_space=pltpu.HBM)
SEM_SPEC = pl.BlockSpec(memory_space=pltpu.SEMAPHORE)
DATAFLOW = pltpu.SideEffectType.DATAFLOW_SIDE_EFFECTING


def _exchange_start(name, routes, n_routes, srcs, lands, after=None):
    n = len(srcs)
    in_place = lands is None
    bufs = list(srcs) + ([] if in_place else list(lands))
    nb = len(bufs)
    extra = [] if after is None else [after]

    def body(*refs):
        src_refs = refs[:n]
        land_refs = src_refs if in_place else refs[n:nb]
        send_sems, recv_sems = refs[nb + len(extra)], refs[nb + len(extra) + 1]
        token = refs[-1]
        for cp in _route_copies(routes, src_refs, land_refs, send_sems, recv_sems):
            cp.start()
        token[...] = jnp.zeros_like(token)

    sems = [pltpu.SemaphoreType.DMA((n * n_routes,)), pltpu.SemaphoreType.DMA((n * n_routes,))]
    out = pl.pallas_call(
        body, name=name, in_specs=[HBM_SPEC] * nb + [ANY] * len(extra),
        out_shape=sems + [pltpu.HBM(b.shape, b.dtype) for b in bufs] + [jax.ShapeDtypeStruct((8, LANES), F32)],
        out_specs=[SEM_SPEC, SEM_SPEC] + [HBM_SPEC] * nb + [pl.BlockSpec(memory_space=pltpu.VMEM)],
        input_output_aliases={i: 2 + i for i in range(nb)},
        compiler_params=pltpu.CompilerParams(has_side_effects=DATAFLOW))(
        *[pltpu.with_memory_space_constraint(b, pltpu.HBM) for b in bufs], *extra)
    return (out[0], out[1], list(out[2:2 + nb])), out[-1]


def _exchange_wait(name, routes, n_routes, n, started, after):
    send_sems, recv_sems, bufs = started
    nb = len(bufs)
    in_place = nb == n

    def body(*refs):
        src_refs = refs[:n]
        land_refs = src_refs if in_place else refs[n:nb]
        for cp in _route_copies(routes, src_refs, land_refs, refs[nb], refs[nb + 1]):
            cp.wait_send()
            cp.wait_recv()

    out = pl.pallas_call(
        body, name=name, in_specs=[HBM_SPEC] * nb + [SEM_SPEC, SEM_SPEC, ANY],
        out_shape=[pltpu.HBM(b.shape, b.dtype) for b in bufs], out_specs=[HBM_SPEC] * nb,
        input_output_aliases={i: i for i in range(nb)},
        compiler_params=pltpu.CompilerParams(has_side_effects=DATAFLOW))(*bufs, send_sems, recv_sems, after)
    return list(out[:n]) if in_place else (list(out[:n]), list(out[n:]))


def _pair_sums(tag, gs, from_sibling):
    x, y, c = _place()
    return [_pair_add(f"rs_add_{tag}_{i}", g, o, c, 2 * x + y) for i, (g, o) in enumerate(zip(gs, from_sibling))]


def _reduce_scatter(tag, gs):
    sums = _pair_sums(tag, gs, _exchange(f"rs_swap_{tag}", _routes_to_sibling, 4, gs, 4))
    got = _exchange(f"rs_chips_{tag}", _routes_to_chips, 3, [s[0] for s in sums], 3)
    return [_sum4(f"rs_sum_{tag}_{i}", s[1], q) for i, (s, q) in enumerate(zip(sums, got))]


def _reduce_scatter_begin(tag, gs):
    lands = [lax.empty((4,) + g.shape[1:], g.dtype) for g in gs]
    swap, token = _exchange_start(f"rs_swap_{tag}_start", _routes_to_sibling, 4, gs, lands)
    return dict(tag=tag, n=len(gs), swap=swap), token


def _reduce_scatter_middle(state, after):
    tag, n = state["tag"], state["n"]
    gs, from_sibling = _exchange_wait(f"rs_swap_{tag}_wait", _routes_to_sibling, 4, n, state["swap"], after)
    state["sums"] = _pair_sums(tag, gs, from_sibling)
    partials = [s[0] for s in state["sums"]]
    lands = [lax.empty((3,) + p.shape[1:], p.dtype) for p in partials]
    state["chips"], token = _exchange_start(f"rs_chips_{tag}_start", _routes_to_chips, 3, partials, lands)
    return token


def _reduce_scatter_end(state, after):
    tag = state["tag"]
    _, got = _exchange_wait(f"rs_chips_{tag}_wait", _routes_to_chips, 3, state["n"], state["chips"], after)
    return [_sum4(f"rs_sum_{tag}_{i}", s[1], q) for i, (s, q) in enumerate(zip(state["sums"], got))]


def _all_gather_begin(tag, blocks, after):
    dev = 4 * lax.axis_index("x") + 2 * lax.axis_index("y") + lax.axis_index("c")
    zones = [lax.dynamic_update_slice_in_dim(lax.empty((N_DEV,) + b.shape, b.dtype), b[None], dev, axis=0) for b in blocks]
    chips, token = _exchange_start(f"gather_{tag}_chips_start", _routes_block_to_chips, 3, zones, None, after)
    return dict(tag=tag, n=len(blocks), chips=chips), token


def _all_gather_middle(state, after):
    tag, n = state["tag"], state["n"]
    zones = _exchange_wait(f"gather_{tag}_chips_wait", _routes_block_to_chips, 3, n, state["chips"], after)
    state["sibling"], token = _exchange_start(f"gather_{tag}_sibling_start", _routes_blocks_to_sibling, 4, zones, None)
    return token


def _all_gather_end(state, after):
    return _exchange_wait(f"gather_{state['tag']}_sibling_wait", _routes_blocks_to_sibling, 4, state["n"], state["sibling"], after)


PACK_UNIT = 8 * LANES


def _packed_size(shape):
    return -(-math.prod(shape) // PACK_UNIT) * PACK_UNIT


def _pack(arrays, dtype):
    parts = []
    for a in arrays:
        flat = a.reshape(-1).astype(dtype)
        parts.append(jnp.pad(flat, (0, _packed_size(a.shape) - flat.shape[0])))
    return jnp.concatenate(parts).reshape(-1, LANES)


def _unpack(flat, shapes, lead=()):
    out, row = [], 0
    for s in shapes:
        rows = _packed_size(s) // LANES
        piece = flat[..., row:row + rows, :].reshape(lead + (rows * LANES,))
        out.append(piece[..., :math.prod(s)].reshape(lead + tuple(s)))
        row += rows
    return out


def _ffn_pad_rows(a):
    n = a.shape[0] // FFN_HALF
    a = jnp.pad(a.reshape(n, FFN_HALF, a.shape[1]), ((0, 0), (0, FFN_HALF_PAD - FFN_HALF), (0, 0)))
    return a.reshape(n * FFN_HALF_PAD, a.shape[2])


def _ffn_unpad_rows(a):
    n = a.shape[0] // FFN_HALF_PAD
    return a.reshape(n, FFN_HALF_PAD, a.shape[1])[:, :FFN_HALF].reshape(n * FFN_HALF, a.shape[1])


def _ffn_pad_cols(a):
    n = a.shape[1] // FFN_HALF
    a = jnp.pad(a.reshape(a.shape[0], n, FFN_HALF), ((0, 0), (0, 0), (0, FFN_HALF_PAD - FFN_HALF)))
    return a.reshape(a.shape[0], n * FFN_HALF_PAD)


def _ffn_unpad_cols(a):
    n = a.shape[1] // FFN_HALF_PAD
    return a.reshape(a.shape[0], n, FFN_HALF_PAD)[:, :, :FFN_HALF].reshape(a.shape[0], n * FFN_HALF)


def _shard_to_send(name, shard):
    if name in ("w_in", "w_br_gdn", "w_br_gla"):
        shard = shard.T
    elif name == "ffn_w_up":
        shard = _ffn_pad_rows(shard.T)
    return shard.astype(MXU_DTYPE)


KEPT_TRANSPOSED = ("w_in", "w_br_gdn", "w_br_gla", "ffn_w_up")


def _whole_from_gathered(name, g):
    if name == "w_in":
        return _in_proj_from_shards(g)
    if name == "ffn_w_down":
        return jnp.pad(g, ((0, 0), (0, FFN_HALF_PAD - FFN_HALF), (0, 0))).reshape(FFN_PAD, g.shape[2])
    return g.reshape(N_DEV * g.shape[1], g.shape[2])


def _slots_from_whole(name, gw):
    if name == "w_in":
        return _in_proj_to_slots(gw)
    return gw.reshape(N_DEV, gw.shape[0] // N_DEV, gw.shape[1])


def _shard_from_slot(name, s):
    if name == "ffn_w_up":
        return _ffn_unpad_rows(s)
    if name == "ffn_w_down":
        return s[:FFN_HALF]
    return s


def _in_proj_pieces():
    starts, pos = {}, 0
    for n, width in IN_SPLITS:
        starts[n] = (pos, width)
        pos += width
    return [(starts[ref][0], off + lane, starts[ref][1]) for _, off, _, pieces in PAD_SEGS for ref, lane in pieces]


def _in_proj_moves():
    cs = IN_DIM // N_DEV
    moves = []
    for src, dst, n in sorted(_in_proj_pieces()):
        at = src
        while at < src + n:
            d = at // cs
            end = min(src + n, (d + 1) * cs)
            moves.append((d, at - d * cs, dst + at - src, end - at))
            at = end
    return moves


RELAYOUT_LANES = 128


def _in_proj_from_shards(g):
    _, cs, D = g.shape

    def body(g_ref, o_ref):
        o_ref[...] = jnp.zeros_like(o_ref)
        for d, i0, r0, n in _in_proj_moves():
            o_ref[r0:r0 + n, :] = g_ref[d, i0:i0 + n, :]

    cb = RELAYOUT_LANES
    return pl.pallas_call(body, name="w_in_rows_in", grid=(D // cb,),
                          in_specs=[pl.BlockSpec((N_DEV, cs, cb), lambda j: (0, 0, j))],
                          out_specs=pl.BlockSpec((IN_PAD, cb), lambda j: (0, j)),
                          out_shape=jax.ShapeDtypeStruct((IN_PAD, D), g.dtype), compiler_params=_cparams(("parallel",)))(g)


def _in_proj_to_slots(gw):
    D = gw.shape[1]
    cs = IN_DIM // N_DEV

    def body(x_ref, o_ref):
        for d, i0, r0, n in _in_proj_moves():
            o_ref[d, i0:i0 + n, :] = x_ref[r0:r0 + n, :]

    cb = RELAYOUT_LANES
    return pl.pallas_call(body, name="w_in_rows_out", grid=(D // cb,),
                          in_specs=[pl.BlockSpec((IN_PAD, cb), lambda j: (0, j))],
                          out_specs=pl.BlockSpec((N_DEV, cs, cb), lambda j: (0, 0, j)),
                          out_shape=jax.ShapeDtypeStruct((N_DEV, cs, D), gw.dtype), compiler_params=_cparams(("parallel",)))(gw)


def _pad_in_proj_rows(w):
    rows, at = [], 0
    for src, dst, n in sorted(_in_proj_pieces(), key=lambda p: p[1]):
        if dst > at:
            rows.append(jnp.zeros((dst - at, w.shape[1]), w.dtype))
        rows.append(w[src:src + n])
        at = dst + n
    rows.append(jnp.zeros((IN_PAD - at, w.shape[1]), w.dtype))
    return jnp.concatenate(rows, axis=0)


def _unpad_in_proj_rows(wp):
    return jnp.concatenate([wp[dst:dst + n] for _, dst, n in sorted(_in_proj_pieces())], axis=0)


def _lane_pad(a, width=LANES):
    return jnp.pad(a, ((0, 0), (0, width - a.shape[1])))


def _seg_blk(h, name, rows):
    off, width = SEG[name]
    return (h, rows, width, off // width)


def _ln_both(xs_, ps_):
    (y,) = _ln_fn(xs_, ps_)
    return (y, y)


def _behind(param, hooks, stage, *seen):
    if hooks is None or stage not in hooks:
        return param
    token = hooks[stage](*seen)
    return param if token is None else param + token[0:1, 0:1]


def _layer_fwd(l, x, x_mx, W, sp, hooks=None):
    T = x.shape[0]
    n64, ngla, ntok = T // SSD_CHUNK, T // GLA_BLOCK, T // 256
    h = _mm(f"in_proj_{l}", x_mx, W["w_in"], "nt")
    xbc = _conv_silu_fwd(f"ssd_conv_{l}", h, SEG["xbc"][0], sp["ssd_conv_w"], sp["ssd_conv_b"])
    gqkv = _conv_silu_fwd(f"gdn_conv_{l}", h, SEG["gqkv"][0], sp["gdn_conv_w"], None)

    ssd_in = [(xbc, SSD_CHUNK, SSD_XBC, 0), _seg_blk(h, "dt", SSD_CHUNK), _seg_blk(h, "z", SSD_CHUNK)]
    ssd_p = [sp["ssd_dt_bias"], sp["ssd_a_log"], sp["ssd_d"], sp["ssd_norm_w"]]
    o_ssd, ssd_states = _chain_fwd(f"ssd_fwd_{l}", _ssd_chunk, n64, ssd_in, ssd_p, [(SSD_CHUNK, SSD_INNER, MXU_DTYPE)],
                                   (SSD_STATE, SSD_INNER))
    o_gdn, gdn_saved = _gdn_forward(str(l), gqkv, h, dict(sp, gdn_a_log=_behind(sp["gdn_a_log"], hooks, "ssd", o_ssd)))
    gla_in = [_seg_blk(h, "lqkv", GLA_BLOCK), _seg_blk(h, "lglr", GLA_BLOCK), _seg_blk(h, "lr", GLA_BLOCK)]
    gla_p = [jnp.pad(sp["gla_gate_w2"], ((0, LANES - GLA_RANK), (0, 0))), sp["gla_gate_b"], sp["gla_norm_w"]]
    o_gla, gla_states = _chain_fwd(f"gla_fwd_{l}", _gla_block, ngla, gla_in, gla_p, [(GLA_BLOCK, GLA_V, MXU_DTYPE)],
                                   (GLA_VAL_DIM, GLA_K))
    ln1_p = [_behind(sp["ln1_g"], hooks, "mixed", o_gdn), sp["ln1_b"]]
    y_ssd = _mm(f"br_ssd_{l}", o_ssd, W["w_br_ssd"])
    y_gdn = _mm(f"br_gdn_{l}", o_gdn, W["w_br_gdn"], "nt")
    y_gla = _mm(f"br_gla_{l}", o_gla, W["w_br_gla"], "nt")
    merge_in = [_seg_blk(h, "gates", 256), (y_ssd, 256, D_MODEL, 0), (y_gdn, 256, D_MODEL, 0), (y_gla, 256, D_MODEL, 0)]
    (mix,) = _chain_fwd(f"merge_{l}", _merge_fn, ntok, merge_in, [], [(256, D_MODEL, MXU_DTYPE)])
    r1 = _mm(f"out_proj_{l}", mix, W["w_out"])
    both = [(256, D_MODEL, F32), (256, D_MODEL, MXU_DTYPE)]
    x1, x1_mx = _chain_fwd(f"ln1_{l}", _ln_both, ntok, [(x, 256, D_MODEL, 0), (r1, 256, D_MODEL, 0)], ln1_p, both)
    up = _mm(f"ffn_up_{l}", x1_mx, W["ffn_w_up"], "nt")
    act = _ffn_glu_fwd(f"ffn_glu_{l}", up, sp["ffn_conv_w_pad"], sp["ffn_conv_b_pad"], MXU_DTYPE)
    ln2_p = [_behind(sp["ln2_g"], hooks, "ffn_act", act), sp["ln2_b"]]
    r2 = _mm(f"ffn_down_{l}", act, W["ffn_w_down"], tn=1024, tk=1024)
    x2, x2_mx = _chain_fwd(f"ln2_{l}", _ln_both, ntok, [(x1, 256, D_MODEL, 0), (r2, 256, D_MODEL, 0)], ln2_p, both)
    saved = dict(x=x, x_mx=x_mx, h=h, xbc=xbc, gqkv=gqkv, ssd_in=ssd_in, ssd_p=ssd_p, ssd_states=ssd_states,
                 gdn=gdn_saved, gla_in=gla_in, gla_p=gla_p, gla_states=gla_states, o_ssd=o_ssd,
                 o_gdn=o_gdn, o_gla=o_gla, merge_in=merge_in, mix=mix, r1=r1, ln1_p=ln1_p, x1=x1, x1_mx=x1_mx, up=up, act=act,
                 r2=r2, ln2_p=ln2_p)
    return x2, x2_mx, saved


def _layer_bwd(l, dx2, W, sp, sv, hooks=None):
    T = dx2.shape[0]
    n64, ngla, ntok = T // SSD_CHUNK, T // GLA_BLOCK, T // 256
    bf = MXU_DTYPE
    gw, gs = {}, {}
    ln2_p = [_behind(sv["ln2_p"][0], hooks, "start"), sv["ln2_p"][1]]
    (dx1_a, dr2), (gs["ln2_g"], gs["ln2_b"]) = _chain_bwd(
        f"ln2_bwd_{l}", _ln_fn, ntok, [(sv["x1"], 256, D_MODEL, 0), (sv["r2"], 256, D_MODEL, 0)], ln2_p,
        [(dx2, 256, D_MODEL)], dx_dtypes=[F32, bf])
    gw["ffn_w_down"] = _mm(f"ffn_down_dw_{l}", sv["act"], dr2, "tn", tn=1024)
    dact = _mm(f"ffn_down_dx_{l}", dr2, W["ffn_w_down"], "nt")
    dg, du, dwg, dwu, dbg, dbu = _ffn_glu_bwd(f"ffn_glu_bwd_{l}", sv["up"], sp["ffn_conv_w_pad"], sp["ffn_conv_b_pad"], dact, bf)
    gs["ffn_conv_w"] = _ffn_unpad_cols(jnp.concatenate([dwg, dwu], axis=1))
    gs["ffn_conv_b"] = _ffn_unpad_cols(jnp.concatenate([dbg, dbu], axis=1))
    dup = jnp.concatenate([dg, du], axis=1)
    gw["ffn_w_up"] = _mm(f"ffn_up_dw_{l}", dup, sv["x1_mx"], "tn", tn=1024)
    dx1_b = _mm(f"ffn_up_dx_{l}", dup, W["ffn_w_up"], "nn", tn=1024, tk=1024)
    ln1_p = [_behind(sv["ln1_p"][0], hooks, "ffn", dx1_b), sv["ln1_p"][1]]
    (dx_a, dr1), (gs["ln1_g"], gs["ln1_b"]) = _chain_bwd(
        f"ln1_bwd_{l}", _ln_sum_fn, ntok, [(sv["x"], 256, D_MODEL, 0), (sv["r1"], 256, D_MODEL, 0)], ln1_p,
        [(dx1_a, 256, D_MODEL), (dx1_b, 256, D_MODEL)], dx_dtypes=[F32, bf])
    gw["w_out"] = _mm(f"out_proj_dw_{l}", sv["mix"], dr1, "tn")
    dmix = _mm(f"out_proj_dx_{l}", dr1, W["w_out"], "nt")
    (dgates, dy_ssd, dy_gdn, dy_gla), _ = _chain_bwd(f"merge_bwd_{l}", _merge_fn, ntok, sv["merge_in"], [],
                                                     [(dmix, 256, D_MODEL)], dx_dtypes=[bf, bf, bf, bf])
    gw["w_br_ssd"] = _mm(f"br_ssd_dw_{l}", sv["o_ssd"], dy_ssd, "tn")
    gw["w_br_gdn"] = _mm(f"br_gdn_dw_{l}", dy_gdn, sv["o_gdn"], "tn")
    gw["w_br_gla"] = _mm(f"br_gla_dw_{l}", dy_gla, sv["o_gla"], "tn")
    do_ssd = _mm(f"br_ssd_dx_{l}", dy_ssd, W["w_br_ssd"], "nt")
    do_gdn = _mm(f"br_gdn_dx_{l}", dy_gdn, W["w_br_gdn"], "nn")
    do_gla = _mm(f"br_gla_dx_{l}", dy_gla, W["w_br_gla"], "nn")

    ssd_p = [_behind(sv["ssd_p"][0], hooks, "branches", do_gla, gw)] + list(sv["ssd_p"][1:])
    (dxbc, ddt, dz), dps = _chain_bwd(f"ssd_bwd_{l}", _ssd_chunk, n64, sv["ssd_in"], ssd_p,
                                      [(do_ssd, SSD_CHUNK, SSD_INNER)], sprev=sv["ssd_states"], dx_dtypes=[F32, bf, bf])
    gs["ssd_dt_bias"], gs["ssd_a_log"], gs["ssd_d"], gs["ssd_norm_w"] = dps
    gdn_sv = dict(sv["gdn"], scan_p=[_behind(sv["gdn"]["scan_p"][0], hooks, "ssd", dz)])
    dgqkv, dgab, dgg, gs["gdn_a_log"], gs["gdn_dt_bias"], gs["gdn_norm_w"] = _gdn_backward(str(l), do_gdn, gdn_sv, bf)
    (dlqkv, dlglr, dlr), dps = _chain_bwd(f"gla_bwd_{l}", _gla_block, ngla, sv["gla_in"], sv["gla_p"],
                                          [(do_gla, GLA_BLOCK, GLA_V)], sprev=sv["gla_states"], dx_dtypes=[bf, bf, bf])
    gs["gla_gate_w2"], gs["gla_gate_b"], gs["gla_norm_w"] = dps[0][:GLA_RANK], dps[1], dps[2]
    dxbc_pre, gs["ssd_conv_w"], gs["ssd_conv_b"] = _conv_silu_bwd(
        f"ssd_conv_bwd_{l}", sv["h"], SEG["xbc"][0], sp["ssd_conv_w"], sp["ssd_conv_b"], dxbc, bf)
    dgqkv_pre, gs["gdn_conv_w"] = _conv_silu_bwd(f"gdn_conv_bwd_{l}", sv["h"], SEG["gqkv"][0], sp["gdn_conv_w"], None, dgqkv, bf)
    pieces = dict(gates=dgates, xbc=dxbc_pre, gqkv=dgqkv_pre, z=dz, lqkv=dlqkv, gg=dgg, lr=dlr, dt=ddt, gab=dgab, lglr=dlglr)
    cols = [pieces[name] for name, _, _, _ in PAD_SEGS]
    cols.append(jnp.zeros((T, IN_PAD - PAD_SEGS[-1][1] - PAD_SEGS[-1][2]), bf))
    dh = jnp.concatenate(cols, axis=1)
    gw["w_in"] = _mm(f"in_proj_dw_{l}", dh, sv["x_mx"], "tn", tn=1024)
    behind = hooks["w_in_grad"](gw) if hooks is not None and "w_in_grad" in hooks else None
    dx_b = _mm(f"in_proj_dx_{l}", dh, W["w_in"], "nn", tm=1024, tn=1024, tk=IN_PAD // 4, after=behind)
    dx = _add_blocks(f"dx_add_{l}", dx_a[None], dx_b[None])[0]
    return dx, gw, gs


def _ln_sum_fn(xs_, ps_):
    (y,) = _ln_fn(xs_, ps_)
    return (y, y)


def _small_2d(name, a):
    return a.reshape(1, -1) if a.ndim == 1 else a


def kernel(x, w_in, ssd_conv_w, ssd_conv_b, ssd_dt_bias, ssd_a_log, ssd_d, ssd_norm_w, gdn_conv_w, gdn_a_log, gdn_dt_bias, gdn_norm_w, gla_gate_w2, gla_gate_b, gla_norm_w, w_br_ssd, w_br_gdn, w_br_gla, w_out, ln1_g, ln1_b, ffn_w_up, ffn_conv_w, ffn_conv_b, ffn_w_down, ln2_g, ln2_b, loss_target, m_w_in, m_ssd_conv_w, m_ssd_conv_b, m_ssd_dt_bias, m_ssd_a_log, m_ssd_d, m_ssd_norm_w, m_gdn_conv_w, m_gdn_a_log, m_gdn_dt_bias, m_gdn_norm_w, m_gla_gate_w2, m_gla_gate_b, m_gla_norm_w, m_w_br_ssd, m_w_br_gdn, m_w_br_gla, m_w_out, m_ln1_g, m_ln1_b, m_ffn_w_up, m_ffn_conv_w, m_ffn_conv_b, m_ffn_w_down, m_ln2_g, m_ln2_b, v_w_in, v_ssd_conv_w, v_ssd_conv_b, v_ssd_dt_bias, v_ssd_a_log, v_ssd_d, v_ssd_norm_w, v_gdn_conv_w, v_gdn_a_log, v_gdn_dt_bias, v_gdn_norm_w, v_gla_gate_w2, v_gla_gate_b, v_gla_norm_w, v_w_br_ssd, v_w_br_gdn, v_w_br_gla, v_w_out, v_ln1_g, v_ln1_b, v_ffn_w_up, v_ffn_conv_w, v_ffn_conv_b, v_ffn_w_down, v_ln2_g, v_ln2_b):
    args = locals()
    w = {n: args[n] for n in WEIGHTS}
    m = {n: args["m_" + n] for n in WEIGHTS}
    v = {n: args["v_" + n] for n in WEIGHTS}
    dev = 4 * lax.axis_index("x") + 2 * lax.axis_index("y") + lax.axis_index("c")
    xl = x[0]
    tgt = loss_target[0]

    late = BIG[1:]

    def send(names, l):
        return [_shard_to_send(n, w[n][l]) for n in names]

    def whole_weights(names, got):
        return {n: _whole_from_gathered(n, g) for n, g in zip(names, got)}

    got0 = _all_gather("gather_first", send(BIG[:1], 0) + [w[n] for n in SMALL_SHARDED])
    gather0, token0 = _all_gather_begin("w_0", send(late, 0), got0[0])
    W = [whole_weights(BIG[:1], got0[:1]), None]
    whole = dict(w)
    for n, s in zip(SMALL_SHARDED, got0[1:]):
        whole[n] = jnp.transpose(s, (1, 2, 0, 3)).reshape(s.shape[1], s.shape[2], N_DEV * s.shape[3])
    SP = [{n: _small_2d(n, whole[n][l]) for n in SMALL} for l in range(DEPTH)]
    for sp in SP:
        sp["ffn_conv_w_pad"] = _ffn_pad_cols(sp["ffn_conv_w"])
        sp["ffn_conv_b_pad"] = _ffn_pad_cols(sp["ffn_conv_b"])

    held = {}

    def late_weights_cross(o_ssd):
        token = _all_gather_middle(gather0, o_ssd)
        held["gather1"], token1 = _all_gather_begin("w_1", send(BIG, 1), o_ssd)
        return token + token1

    def late_weights_arrive(mixed):
        W[0].update(whole_weights(late, _all_gather_end(gather0, mixed)))

    fwd_hooks = {"ssd": late_weights_cross, "mixed": late_weights_arrive,
                 "ffn_act": lambda act: _all_gather_middle(held["gather1"], act)}
    saved = [None] * DEPTH
    act, act_mx, saved[0] = _layer_fwd(0, xl, (xl + token0[0, 0]).astype(MXU_DTYPE), W[0], SP[0], hooks=fwd_hooks)
    W[1] = whole_weights(BIG, _all_gather_end(held["gather1"], act))
    act, act_mx, saved[1] = _layer_fwd(1, act, act_mx, W[1], SP[1])
    dy, loss_parts = _loss_head(act, tgt)
    loss = lax.psum(jnp.sum(loss_parts), ("x", "y", "c"))

    def slots_of(names, gw):
        return [_slots_from_whole(n, gw[n]) for n in names]

    grads = {}
    GS = [None] * DEPTH
    dy, gw, GS[1] = _layer_bwd(1, dy, W[1], SP[1], saved[1])
    reduce1, reduce1_token = _reduce_scatter_begin("1", slots_of(BIG, gw))

    def late_grads_leave(seen, gw0):
        held["reduce0"], token = _reduce_scatter_begin("0", slots_of(late, gw0))
        return token

    def w_in_grad_leaves(gw0):
        held["reduce_first"], token = _reduce_scatter_begin("first", slots_of(BIG[:1], gw0))
        return token

    bwd_hooks = {"start": lambda: reduce1_token, "ffn": lambda seen: _reduce_scatter_middle(reduce1, seen),
                 "branches": late_grads_leave, "ssd": lambda seen: _reduce_scatter_middle(held["reduce0"], seen),
                 "w_in_grad": w_in_grad_leaves}
    dy, gw, GS[0] = _layer_bwd(0, dy, W[0], SP[0], saved[0], hooks=bwd_hooks)
    small_shapes = [whole[n].shape for n in SMALL]
    gs_flat = _pack([jnp.stack([GS[l][n].reshape(whole[n].shape[1:]) for l in range(DEPTH)]) for n in SMALL], F32)
    (gs_all,) = _all_gather("gather_small_grads", [gs_flat])
    first_token = _reduce_scatter_middle(held["reduce_first"], gs_all)
    red1 = _reduce_scatter_end(reduce1, dy)
    red0_late = _reduce_scatter_end(held["reduce0"], dy)
    grad_x = dy[None]
    kept_t = KEPT_TRANSPOSED
    grads_k = {n: jnp.stack([_shard_from_slot(n, red0_late[i]), _shard_from_slot(n, red1[i + 1])]) for i, n in enumerate(late)}

    def mine(n, a):
        if n in SMALL_SHARDED:
            cs = a.shape[-1] // N_DEV
            return lax.dynamic_slice_in_dim(a, dev * cs, cs, axis=a.ndim - 1)
        return a

    m_whole, v_whole = {}, {}
    for n in SMALL:
        reps = (1, 1, N_DEV) if n in SMALL_SHARDED else (1,) * m[n].ndim
        m_whole[n], v_whole[n] = jnp.tile(m[n], reps), jnp.tile(v[n], reps)
    outs = _adamw_small(gs_all, _pack([whole[n] for n in SMALL], F32) + first_token[0:1, 0:1], _pack([m_whole[n] for n in SMALL], F32),
                        _pack([v_whole[n] for n in SMALL], F32))
    g_s, d_s, m_s, v_s = [_unpack(o, small_shapes) for o in outs]
    delta, new_m, new_v = {}, {}, {}
    for i, n in enumerate(SMALL):
        grads[n], delta[n], new_m[n], new_v[n] = mine(n, g_s[i]), mine(n, d_s[i]), mine(n, m_s[i]), mine(n, v_s[i])
    for n in late + BIG[:1]:
        if n == "w_in":
            done = sum(new_v[k].reshape(-1)[0:1] for k in late + SMALL[:1])
            (first0,) = _reduce_scatter_end(held["reduce_first"], done)
            grads_k[n] = jnp.stack([first0, red1[0]])
        view = (lambda a: jnp.transpose(a, (0, 2, 1))) if n in kept_t else (lambda a: a)
        outs = _adamw(f"adamw_{n}", view(w[n]), grads_k[n], view(m[n]), view(v[n]), after=None if n == "w_in" else first_token)
        grads[n], delta[n], new_m[n], new_v[n] = view(grads_k[n]), view(outs[0]), view(outs[1]), view(outs[2])

    return (loss, grad_x, *[grads[n] for n in WEIGHTS], *[delta[n] for n in WEIGHTS], *[new_m[n] for n in WEIGHTS],
            *[new_v[n] for n in WEIGHTS])
```

```python
import functools
import math

import jax
import jax.numpy as jnp
from jax import lax
from jax.experimental import pallas as pl
from jax.experimental.pallas import tpu as pltpu

F32 = jnp.float32
MXU_DTYPE = jnp.bfloat16
HI = lax.Precision.HIGHEST

N_DEV = 8
D_MODEL = 1024
DEPTH = 2
SSD_HEADS, SSD_HEAD_DIM, SSD_INNER, SSD_GROUPS, SSD_STATE, SSD_CHUNK = 16, 64, 1024, 2, 128, 64
SSD_XBC = SSD_INNER + 2 * SSD_GROUPS * SSD_STATE
GDN_HEADS, GDN_HEAD_DIM, GDN_WIDTH, GDN_CHUNK = 4, 128, 512, 64
GLA_HEADS, GLA_KEY_DIM, GLA_VAL_DIM, GLA_K, GLA_V, GLA_RANK, GLA_CHUNK = 4, 64, 128, 256, 512, 16, 16
GLA_BLOCK = 128
GLA_NORMALIZER = 16.0
FFN_DIM = 2816
FFN_HALF = FFN_DIM // 8
FFN_HALF_PAD = 384
FFN_UP_PAD = 16 * FFN_HALF_PAD
FFN_PAD = FFN_UP_PAD // 2
ALPHA = (2 * DEPTH) ** 0.25
LN_EPS = 1e-5
RMS_EPS = 1e-6
ADAM_LR, ADAM_B1, ADAM_B2, ADAM_EPS, ADAM_WD, ADAM_STEP = 0.001, 0.9, 0.999, 1e-08, 0.01, 10
LANES = 128
NEG_BIG = -1e30
VMEM_LIMIT = 56 * 1024 * 1024

IN_SPLITS = (("z", 1024), ("xbc", 1536), ("dt", 16), ("gqkv", 1536), ("ga", 4), ("gb", 4), ("gg", 512),
             ("lqkv", 1024), ("lglr", 16), ("lr", 512), ("gates", 3072))
IN_DIM = sum(w for _, w in IN_SPLITS)
PAD_SEGS = (("gates", 0, 3072, (("gates", 0),)), ("xbc", 3072, 1536, (("xbc", 0),)),
            ("gqkv", 4608, 1536, (("gqkv", 0),)), ("z", 6144, 1024, (("z", 0),)),
            ("lqkv", 7168, 1024, (("lqkv", 0),)), ("gg", 8192, 512, (("gg", 0),)), ("lr", 8704, 512, (("lr", 0),)),
            ("dt", 9216, 128, (("dt", 0),)), ("gab", 9344, 128, (("ga", 0), ("gb", 4))), ("lglr", 9472, 128, (("lglr", 0),)))
IN_PAD = 9728
SEG = {name: (off, width) for name, off, width, _ in PAD_SEGS}

BIG = ("w_in", "w_br_ssd", "w_br_gdn", "w_br_gla", "w_out", "ffn_w_up", "ffn_w_down")
COL_SHARDED = ("w_in", "w_br_gdn", "w_br_gla", "ffn_w_up")
SMALL_SHARDED = ("ssd_conv_w", "gdn_conv_w", "gla_gate_w2", "ffn_conv_w")
WEIGHTS = ("w_in", "ssd_conv_w", "ssd_conv_b", "ssd_dt_bias", "ssd_a_log", "ssd_d", "ssd_norm_w", "gdn_conv_w",
           "gdn_a_log", "gdn_dt_bias", "gdn_norm_w", "gla_gate_w2", "gla_gate_b", "gla_norm_w", "w_br_ssd", "w_br_gdn",
           "w_br_gla", "w_out", "ln1_g", "ln1_b", "ffn_w_up", "ffn_conv_w", "ffn_conv_b", "ffn_w_down", "ln2_g", "ln2_b")
SMALL = tuple(n for n in WEIGHTS if n not in BIG)
FLAT_W = 512


def _cparams(sem=None):
    kw = dict(vmem_limit_bytes=VMEM_LIMIT)
    if sem is not None:
        kw["dimension_semantics"] = sem
    return pltpu.CompilerParams(**kw)


_DIMS = {"nn": (((1,), (0,)), ((), ())), "nt": (((1,), (1,)), ((), ())), "tn": (((0,), (0,)), ((), ()))}


def _dot(a, b, dims="nn"):
    if MXU_DTYPE == F32:
        return lax.dot_general(a.astype(F32), b.astype(F32), _DIMS[dims], precision=HI, preferred_element_type=F32)
    return lax.dot_general(a.astype(MXU_DTYPE), b.astype(MXU_DTYPE), _DIMS[dims], preferred_element_type=F32)


def _dot_hi(a, b, dims="nn"):
    return lax.dot_general(a.astype(F32), b.astype(F32), _DIMS[dims], precision=HI, preferred_element_type=F32)


def _iota2(shape, axis):
    return lax.broadcasted_iota(jnp.int32, shape, axis)


def _tril(n, strict=False):
    r, c = _iota2((n, n), 0), _iota2((n, n), 1)
    return (r > c) if strict else (r >= c)


def _raw_dot(a, b, dims):
    return lax.dot_general(a, b, _DIMS[dims], preferred_element_type=F32)


def _dot_x3(a, b, dims="nn"):
    if MXU_DTYPE == F32:
        return _dot_hi(a, b, dims)
    ah, bh = a.astype(jnp.bfloat16), b.astype(jnp.bfloat16)
    al, bl = (a - ah.astype(F32)).astype(jnp.bfloat16), (b - bh.astype(F32)).astype(jnp.bfloat16)
    return _raw_dot(ah, bh, dims) + (_raw_dot(ah, bl, dims) + _raw_dot(al, bh, dims))


def _exact_dot(mask, b, dims, mask_first):
    if MXU_DTYPE == F32:
        return _dot_hi(mask, b, dims) if mask_first else _dot_hi(b, mask, dims)
    m = mask.astype(jnp.bfloat16)
    b1 = b.astype(jnp.bfloat16)
    r1 = b - b1.astype(F32)
    b2 = r1.astype(jnp.bfloat16)
    b3 = (r1 - b2.astype(F32)).astype(jnp.bfloat16)
    if mask_first:
        return _raw_dot(m, b1, dims) + (_raw_dot(m, b2, dims) + _raw_dot(m, b3, dims))
    return _raw_dot(b1, m, dims) + (_raw_dot(b2, m, dims) + _raw_dot(b3, m, dims))


@jax.custom_vjp
def _mask_left(mask, b):
    return _exact_dot(mask, b, "nn", True)


_mask_left.defvjp(lambda mask, b: (_mask_left(mask, b), mask),
                  lambda mask, d: (jnp.zeros_like(mask), _exact_dot(mask, d, "tn", True)))


@jax.custom_vjp
def _mask_right(a, mask):
    return _exact_dot(mask, a, "nn", False)


_mask_right.defvjp(lambda a, mask: (_mask_right(a, mask), mask),
                   lambda mask, d: (_exact_dot(mask, d, "nt", False), jnp.zeros_like(mask)))


@jax.custom_vjp
def _unit_lower_inverses(mats):
    n = mats[0].shape[0]
    eye = (_iota2((n, n), 0) == _iota2((n, n), 1)).astype(F32)
    xs = [eye - a for a in mats]
    ps = list(mats)
    k = 2
    while k < n:
        ps = [_dot_x3(p, p) for p in ps]
        xs = [x + _dot_x3(x, p) for x, p in zip(xs, ps)]
        k *= 2
    return xs


def _unit_lower_inverses_fwd(mats):
    ts = _unit_lower_inverses(mats)
    return ts, ts


def _unit_lower_inverses_bwd(ts, dts):
    mids = [_dot_x3(t, d, "tn") for t, d in zip(ts, dts)]
    return ([-_dot_x3(m, t, "nt") for m, t in zip(mids, ts)],)


_unit_lower_inverses.defvjp(_unit_lower_inverses_fwd, _unit_lower_inverses_bwd)


SSD_BLOCK = 4 * SSD_CHUNK


def _ssd_block(xs_, ps_, s_t):
    xbc, dtraw, z = xs_
    chunks = [slice(c * SSD_CHUNK, (c + 1) * SSD_CHUNK) for c in range(xbc.shape[0] // SSD_CHUNK)]
    free = [_ssd_chunk_free(xbc[sl], dtraw[sl], ps_) for sl in chunks]
    outs = []
    for sl, f in zip(chunks, free):
        o, s_t = _ssd_chunk_finish(f, z[sl], ps_[3], s_t)
        outs.append(o)
    return (jnp.concatenate(outs, axis=0),), s_t


def _ssd_chunk_free(xbc, dtraw, ps_):
    dt_bias, a_log, d_skip, _ = ps_
    L = xbc.shape[0]
    H, P, N, G = SSD_HEADS, SSD_HEAD_DIM, SSD_STATE, SSD_GROUPS
    W = SSD_INNER // G
    xs = xbc[:, :SSD_INNER]
    bm = xbc[:, SSD_INNER:SSD_INNER + G * N]
    cm = xbc[:, SSD_INNER + G * N:]
    dt = jax.nn.softplus(dtraw[:, :H] + dt_bias)
    a = dt * (-jnp.exp(a_log))
    causal = _tril(L)
    a_cs = _mask_left(causal.astype(F32), a)
    expand = (_iota2((H, SSD_INNER), 1) // P == _iota2((H, SSD_INNER), 0)).astype(F32)
    wide = _mask_right(jnp.concatenate([a_cs, dt, jnp.broadcast_to(d_skip, (L, H))], axis=0), expand)
    a_cs_x, dt_x, d_x = wide[:L], wide[L:2 * L], wide[2 * L:]
    a_end_x = a_cs_x[L - 1:L, :]
    a_cs_t, dt_t = a_cs.T, dt.T
    cb = [_dot(cm[:, g * N:(g + 1) * N], bm[:, g * N:(g + 1) * N], "nt") for g in range(G)]
    cb2 = [jnp.concatenate([c, c], axis=1) for c in cb]
    lane2 = _iota2((L, 2 * L), 1)
    left = lane2 < L
    causal2 = _iota2((L, 2 * L), 0) >= jnp.where(left, lane2, lane2 - L)
    pairs = range(0, H, 2)
    col2 = [jnp.where(left, a_cs[:, h:h + 1], a_cs[:, h + 1:h + 2]) for h in pairs]
    row2 = [jnp.concatenate([a_cs_t[h:h + 1, :], a_cs_t[h + 1:h + 2, :]], axis=1) for h in pairs]
    dt2 = [jnp.concatenate([dt_t[h:h + 1, :], dt_t[h + 1:h + 2, :]], axis=1) for h in pairs]
    ws2 = [cb2[h // (H // G)] * (jnp.exp(jnp.where(causal2, col2[i] - row2[i], NEG_BIG)) * dt2[i]) for i, h in enumerate(pairs)]
    first = _iota2((L, 2 * P), 1) < P
    ys = []
    for i, h in enumerate(pairs):
        x2 = xs[:, h * P:(h + 2) * P]
        ys.append(_dot(ws2[i], jnp.concatenate([jnp.where(first, x2, 0.0), jnp.where(first, 0.0, x2)], axis=0)))
    y = jnp.concatenate(ys, axis=1) + d_x * xs
    xw = xs * (jnp.exp(a_end_x - a_cs_x) * dt_x)
    st = jnp.concatenate([_dot(bm[:, g * N:(g + 1) * N], xw[:, g * W:(g + 1) * W], "tn") for g in range(G)], axis=1)
    return dict(y=y, cm=cm, st=st, carry=jnp.exp(a_cs_x), keep=jnp.exp(a_end_x))


def _ssd_chunk_finish(free, z, norm_w, s_t):
    N, G = SSD_STATE, SSD_GROUPS
    W = SSD_INNER // G
    cm = free["cm"]
    y_in = jnp.concatenate([_dot(cm[:, g * N:(g + 1) * N], s_t[:, g * W:(g + 1) * W]) for g in range(G)], axis=1)
    y = free["y"] + y_in * free["carry"]
    s_new = s_t * free["keep"] + free["st"]
    yg = y * jax.nn.silu(z)
    outs = []
    for g in range(G):
        part = yg[:, g * W:(g + 1) * W]
        outs.append(part * lax.rsqrt(jnp.mean(part * part, axis=1, keepdims=True) + RMS_EPS))
    return jnp.concatenate(outs, axis=1) * norm_w, s_new


GDN_PREP_CHUNKS = 8


def _gdn_prep(xs_, ps_):
    qkv, ab = xs_
    a_log, dt_bias = ps_
    B = qkv.shape[0]
    H, D, L = GDN_HEADS, GDN_HEAD_DIM, GDN_CHUNK
    g_all = -jnp.exp(a_log) * jax.nn.softplus(ab + dt_bias)
    row, col = _iota2((B, B), 0), _iota2((B, B), 1)
    g_cs = _mask_left((((row // L) == (col // L)) & (row >= col)).astype(F32), g_all)
    g_cs_t = g_cs.T
    beta_all = jax.nn.sigmoid(ab)
    incl, strict = _tril(L), _tril(L, strict=True)
    qs, ks, vs = [], [], []
    for h in range(H):
        q = qkv[:, h * D:(h + 1) * D]
        k = qkv[:, GDN_WIDTH + h * D:GDN_WIDTH + (h + 1) * D]
        qs.append(q * lax.rsqrt(jnp.sum(q * q, axis=1, keepdims=True) + RMS_EPS) * (D ** -0.5))
        ks.append(k * lax.rsqrt(jnp.sum(k * k, axis=1, keepdims=True) + RMS_EPS))
        vs.append(qkv[:, 2 * GDN_WIDTH + h * D:2 * GDN_WIDTH + (h + 1) * D])
    pairs = [(c, h) for c in range(B // L) for h in range(H)]
    rows = {c: slice(c * L, (c + 1) * L) for c in range(B // L)}
    q_ = {(c, h): qs[h][rows[c]] for c, h in pairs}
    k_ = {(c, h): ks[h][rows[c]] for c, h in pairs}
    col_ = {(c, h): g_cs[rows[c], h:h + 1] for c, h in pairs}
    beta_ = {(c, h): beta_all[rows[c], H + h:H + h + 1] for c, h in pairs}
    gamma = {p: jnp.exp(jnp.where(incl, col_[p] - g_cs_t[p[1]:p[1] + 1, rows[p[0]]], NEG_BIG)) for p in pairs}
    kb = {p: k_[p] * beta_[p] for p in pairs}
    a_mat = [jnp.where(strict, _dot(kb[p], k_[p], "nt") * gamma[p], 0.0) for p in pairs]
    attn = {p: jnp.where(incl, _dot(q_[p], k_[p], "nt") * gamma[p], 0.0) for p in pairs}
    t_mat = dict(zip(pairs, _unit_lower_inverses(a_mat)))
    u = {p: _dot(t_mat[p], vs[p[1]][rows[p[0]]] * beta_[p]) for p in pairs}
    w = {p: _dot(t_mat[p], kb[p] * jnp.exp(col_[p])) for p in pairs}
    qd = {p: q_[p] * jnp.exp(col_[p]) for p in pairs}
    kd = {p: k_[p] * jnp.exp(col_[p][L - 1:L, :] - col_[p]) for p in pairs}

    def whole(parts):
        return jnp.concatenate([jnp.concatenate([parts[(c, h)] for h in range(H)], axis=1) for c in range(B // L)], axis=0)

    return (whole(u), whole(w), whole(qd), whole(kd), whole(attn), g_cs)


def _gdn_scan(xs_, ps_, s):
    u, w, qd, kd, attn, g_cs, gate = xs_
    (norm_w,) = ps_
    L = u.shape[0]
    H, D = GDN_HEADS, GDN_HEAD_DIM
    heads = range(H)
    lanes = [slice(h * D, (h + 1) * D) for h in heads]
    s_h = [s[lanes[h], :] for h in heads]
    v_new = [u[:, lanes[h]] - _dot(w[:, lanes[h]], s_h[h]) for h in heads]
    o = [_dot(qd[:, lanes[h]], s_h[h]) + _dot(attn[:, h * L:(h + 1) * L], v_new[h]) for h in heads]
    decay = [jnp.exp(g_cs[L - 1:L, h:h + 1]) for h in heads]
    s_new = [s_h[h] * decay[h] + _dot(kd[:, lanes[h]], v_new[h], "tn") for h in heads]
    o = [o[h] * lax.rsqrt(jnp.mean(o[h] * o[h], axis=1, keepdims=True) + RMS_EPS) * norm_w * jax.nn.silu(gate[:, lanes[h]])
         for h in heads]
    return (jnp.concatenate(o, axis=1),), jnp.concatenate(s_new, axis=0)


def _gdn_forward(tag, gqkv, h, sp):
    T = gqkv.shape[0]
    blk = min(GDN_PREP_CHUNKS * GDN_CHUNK, T)
    prep_in = [(gqkv, blk, 3 * GDN_WIDTH, 0), _seg_blk(h, "gab", blk)]
    prep_p = [_lane_pad(sp["gdn_a_log"]), _lane_pad(sp["gdn_dt_bias"])]
    mx = MXU_DTYPE
    prep = _chain_fwd(f"gdn_prep_{tag}", _gdn_prep, T // blk, prep_in, prep_p,
                      [(blk, GDN_WIDTH, F32), (blk, GDN_WIDTH, mx), (blk, GDN_WIDTH, mx), (blk, GDN_WIDTH, mx),
                       (blk, GDN_HEADS * GDN_CHUNK, mx), (blk, LANES, F32)])
    widths = [GDN_WIDTH] * 4 + [GDN_HEADS * GDN_CHUNK, LANES]
    scan_in = [(a, GDN_CHUNK, wd, 0) for a, wd in zip(prep, widths)] + [_seg_blk(h, "gg", GDN_CHUNK)]
    scan_p = [sp["gdn_norm_w"]]
    o, states = _chain_fwd(f"gdn_scan_{tag}", _gdn_scan, T // GDN_CHUNK, scan_in, scan_p, [(GDN_CHUNK, GDN_WIDTH, mx)],
                           (GDN_WIDTH, GDN_HEAD_DIM))
    return o, dict(prep_in=prep_in, prep_p=prep_p, scan_in=scan_in, scan_p=scan_p, states=states, widths=widths)


def _gdn_backward(tag, do, sv, dx_dtype):
    T = do.shape[0]
    blk = min(GDN_PREP_CHUNKS * GDN_CHUNK, T)
    dscan, (dnorm,) = _chain_bwd(f"gdn_scan_bwd_{tag}", _gdn_scan, T // GDN_CHUNK, sv["scan_in"], sv["scan_p"],
                                 [(do, GDN_CHUNK, GDN_WIDTH)], sprev=sv["states"], dx_dtypes=[F32] * 6 + [dx_dtype])
    douts = [(d, blk, wd) for d, wd in zip(dscan[:6], sv["widths"])]
    (dgqkv, dgab), (da_log, ddt_bias) = _chain_bwd(f"gdn_prep_bwd_{tag}", _gdn_prep, T // blk, sv["prep_in"], sv["prep_p"],
                                                   douts, dx_dtypes=[F32, dx_dtype])
    return dgqkv, dgab, dscan[6], da_log[:, :GDN_HEADS], ddt_bias[:, :GDN_HEADS], dnorm


def _gla_block(xs_, ps_, s_t):
    qkv, glr, r = xs_
    w2, gate_b, norm_w = ps_
    B = qkv.shape[0]
    H, K, V, C = GLA_HEADS, GLA_KEY_DIM, GLA_VAL_DIM, GLA_CHUNK
    q = qkv[:, :GLA_K] * (K ** -0.5)
    k = qkv[:, GLA_K:2 * GLA_K]
    v = qkv[:, 2 * GLA_K:]
    gk = jax.nn.log_sigmoid(_dot(glr, w2) + gate_b) / GLA_NORMALIZER
    row, col = _iota2((B, B), 0), _iota2((B, B), 1)
    same = (row // C) == (col // C)
    mask = same & (row >= col)
    b_cs = _mask_left(mask.astype(F32), gk)
    b_end = _mask_left((col == (row // C) * C + (C - 1)).astype(F32), b_cs)
    q_e = q * jnp.exp(b_cs)
    k_e = k * jnp.exp(-b_cs)
    k_d = k * jnp.exp(b_end - b_cs)
    intra = []
    for h in range(H):
        a_mat = jnp.where(mask, _dot(q_e[:, h * K:(h + 1) * K], k_e[:, h * K:(h + 1) * K], "nt"), 0.0)
        intra.append(_dot(a_mat, v[:, h * V:(h + 1) * V]))
    o = jnp.concatenate(intra, axis=1)
    chunks = [slice(j * C, (j + 1) * C) for j in range(B // C)]
    fresh = [jnp.concatenate([_dot(v[sl, h * V:(h + 1) * V], k_d[sl, h * K:(h + 1) * K], "tn") for h in range(H)], axis=1)
             for sl in chunks]
    entering = []
    for j, sl in enumerate(chunks):
        entering.append(s_t)
        s_t = s_t * jnp.exp(b_end[j * C:j * C + 1, :]) + fresh[j]
    inter = [jnp.concatenate([_dot(q_e[sl, h * K:(h + 1) * K], entering[j][:, h * K:(h + 1) * K], "nt") for h in range(H)],
                             axis=1) for j, sl in enumerate(chunks)]
    o = o + jnp.concatenate(inter, axis=0)
    outs = []
    for h in range(H):
        oh = o[:, h * V:(h + 1) * V]
        oh = oh * lax.rsqrt(jnp.mean(oh * oh, axis=1, keepdims=True) + RMS_EPS) * norm_w
        outs.append(oh * jax.nn.silu(r[:, h * V:(h + 1) * V]))
    return (jnp.concatenate(outs, axis=1),), s_t


def _merge_fn(xs_, ps_):
    gates, y_ssd, y_gdn, y_gla = xs_
    d = D_MODEL
    return (jax.nn.sigmoid(gates[:, :d]) * y_ssd + jax.nn.sigmoid(gates[:, d:2 * d]) * y_gdn
            + jax.nn.sigmoid(gates[:, 2 * d:]) * y_gla,)


def _ln_fn(xs_, ps_):
    x, r = xs_
    g, b = ps_
    t = ALPHA * x + r
    mu = jnp.mean(t, axis=1, keepdims=True)
    var = jnp.mean(jnp.square(t - mu), axis=1, keepdims=True)
    return ((t - mu) * lax.rsqrt(var + LN_EPS) * g + b,)


def _row_spec(rows, width, colblk, n, reverse):
    if reverse:
        return pl.BlockSpec((rows, width), lambda c: (n - 1 - c, colblk))
    return pl.BlockSpec((rows, width), lambda c: (c, colblk))


def _full_spec(shape):
    zeros = (0,) * len(shape)
    return pl.BlockSpec(shape, lambda c: zeros)


def _chain_fwd(name, fn, n, blocked, full, out_defs, state_shape=None):
    nb, nf, no = len(blocked), len(full), len(out_defs)

    def body(*refs):
        xs = [r[...].astype(F32) for r in refs[:nb]]
        ps = [r[...] for r in refs[nb:nb + nf]]
        o_refs = refs[nb + nf:nb + nf + no]
        if state_shape is None:
            outs = fn(xs, ps)
        else:
            sprev_ref, s_ref = refs[nb + nf + no:]

            @pl.when(pl.program_id(0) == 0)
            def _():
                s_ref[...] = jnp.zeros_like(s_ref)

            s = s_ref[...]
            sprev_ref[0] = s
            outs, s_new = fn(xs, ps, s)
            s_ref[...] = s_new
        for r, o in zip(o_refs, outs):
            r[...] = o.astype(r.dtype)

    in_specs = [_row_spec(rows, width, cb, n, False) for _, rows, width, cb in blocked]
    in_specs += [_full_spec(a.shape) for a in full]
    out_specs = [_row_spec(rows, width, 0, n, False) for rows, width, _ in out_defs]
    out_shape = [jax.ShapeDtypeStruct((n * rows, width), dt) for rows, width, dt in out_defs]
    scratch = []
    if state_shape is not None:
        out_specs.append(pl.BlockSpec((1,) + state_shape, lambda c: (c, 0, 0)))
        out_shape.append(jax.ShapeDtypeStruct((n,) + state_shape, F32))
        scratch.append(pltpu.VMEM(state_shape, F32))
    return pl.pallas_call(body, name=name, grid=(n,), in_specs=in_specs, out_specs=out_specs, out_shape=out_shape,
                          scratch_shapes=scratch, compiler_params=_cparams(("arbitrary",)))(
        *[a for a, _, _, _ in blocked], *full)


def _chain_bwd(name, fn, n, blocked, full, douts, sprev=None, dx_dtypes=None):
    nb, nf, nd = len(blocked), len(full), len(douts)
    has_state = sprev is not None
    dx_dtypes = dx_dtypes or [F32] * nb

    def body(*refs):
        pos = 0
        b_refs = refs[pos:pos + nb]; pos += nb
        f_refs = refs[pos:pos + nf]; pos += nf
        d_refs = refs[pos:pos + nd]; pos += nd
        if has_state:
            sprev_ref = refs[pos]; pos += 1
        dx_refs = refs[pos:pos + nb]; pos += nb
        dp_refs = refs[pos:pos + nf]; pos += nf
        if has_state:
            ds_ref = refs[pos]

        @pl.when(pl.program_id(0) == 0)
        def _():
            for r in dp_refs:
                r[...] = jnp.zeros_like(r)
            if has_state:
                ds_ref[...] = jnp.zeros_like(ds_ref)

        xs = [r[...].astype(F32) for r in b_refs]
        ps = [r[...] for r in f_refs]
        dys = tuple(r[...].astype(F32) for r in d_refs)
        if has_state:
            _, vjp = jax.vjp(fn, xs, ps, sprev_ref[0])
            dxs, dps, ds = vjp((dys, ds_ref[...]))
            ds_ref[...] = ds
        else:
            _, vjp = jax.vjp(fn, xs, ps)
            dxs, dps = vjp(dys)
        for r, d in zip(dx_refs, dxs):
            r[...] = d.astype(r.dtype)
        for r, d in zip(dp_refs, dps):
            r[...] += d

    in_specs = [_row_spec(rows, width, cb, n, True) for _, rows, width, cb in blocked]
    in_specs += [_full_spec(a.shape) for a in full]
    in_specs += [_row_spec(rows, width, 0, n, True) for _, rows, width in douts]
    args = [a for a, _, _, _ in blocked] + list(full) + [a for a, _, _ in douts]
    scratch = []
    if has_state:
        st_shape = sprev.shape[1:]
        in_specs.append(pl.BlockSpec((1,) + st_shape, lambda c: (n - 1 - c, 0, 0)))
        args.append(sprev)
        scratch.append(pltpu.VMEM(st_shape, F32))
    out_specs = [_row_spec(rows, width, 0, n, True) for _, rows, width, _ in blocked]
    out_specs += [_full_spec(a.shape) for a in full]
    out_shape = [jax.ShapeDtypeStruct((n * rows, width), dt) for (_, rows, width, _), dt in zip(blocked, dx_dtypes)]
    out_shape += [jax.ShapeDtypeStruct(a.shape, F32) for a in full]
    res = pl.pallas_call(body, name=name, grid=(n,), in_specs=in_specs, out_specs=out_specs, out_shape=out_shape,
                         scratch_shapes=scratch, compiler_params=_cparams(("arbitrary",)))(*args)
    return res[:nb], res[nb:]


def _tile(n, target, unit):
    if n <= target:
        return n
    best = None
    for t in range(unit, target + 1, unit):
        if n % t == 0:
            best = t
    assert best is not None, (n, target, unit)
    return best


def _mm(name, a, b, dims="nn", out_dtype=F32, tm=2048, tn=512, tk=2048, after=None):
    if dims == "nn":
        (M, K), (_, N) = a.shape, b.shape
    elif dims == "nt":
        (M, K), (N, _) = a.shape, b.shape
    else:
        (K, M), (_, N) = a.shape, b.shape
    tm, tn, tk = _tile(M, tm, LANES), _tile(N, tn, LANES), _tile(K, tk, LANES)
    nk = K // tk
    extra = [] if after is None else [after]

    def body(*refs):
        a_ref, b_ref = refs[:2]
        o_ref, acc_ref = refs[-2:]
        part = _dot(a_ref[...], b_ref[...], dims)
        if nk == 1:
            o_ref[...] = part.astype(o_ref.dtype)
            return
        k = pl.program_id(2)

        @pl.when(k == 0)
        def _():
            acc_ref[...] = part

        @pl.when((k > 0) & (k < nk - 1))
        def _():
            acc_ref[...] += part

        @pl.when(k == nk - 1)
        def _():
            o_ref[...] = (acc_ref[...] + part).astype(o_ref.dtype)

    if dims == "tn":
        a_spec = pl.BlockSpec((tk, tm), lambda j, i, k: (k, i))
    else:
        a_spec = pl.BlockSpec((tm, tk), lambda j, i, k: (i, k))
    if dims == "nt":
        b_spec = pl.BlockSpec((tn, tk), lambda j, i, k: (j, k))
    else:
        b_spec = pl.BlockSpec((tk, tn), lambda j, i, k: (k, j))
    return pl.pallas_call(
        body, name=name, grid=(N // tn, M // tm, nk), in_specs=[a_spec, b_spec] + [ANY] * len(extra),
        out_specs=pl.BlockSpec((tm, tn), lambda j, i, k: (i, j)), out_shape=jax.ShapeDtypeStruct((M, N), out_dtype),
        scratch_shapes=[pltpu.VMEM((tm, tn) if nk > 1 else (8, LANES), F32)],
        compiler_params=_cparams(("parallel", "parallel", "arbitrary")))(a, b, *extra)


CONV_CB = 256


def _shift_down(x, k):
    if k == 0:
        return x
    return jnp.where(_iota2(x.shape, 0) >= k, pltpu.roll(x, k, 0), 0.0)


def _shift_up(x, k):
    if k == 0:
        return x
    t = x.shape[0]
    return jnp.where(_iota2(x.shape, 0) < t - k, pltpu.roll(x, t - k, 0), 0.0)


def _conv_pre(x, w, b):
    kk = w.shape[0]
    pre = x * w[kk - 1:kk, :]
    for k in range(kk - 1):
        pre = pre + _shift_down(x, kk - 1 - k) * w[k:k + 1, :]
    return pre if b is None else pre + b


EDGE = 16


def _conv_pre_rot(x, w, b):
    kk = w.shape[0]
    pre = x * w[kk - 1:kk, :]
    for k in range(kk - 1):
        pre = pre + pltpu.roll(x, kk - 1 - k, 0) * w[k:k + 1, :]
    return pre if b is None else pre + b


def _conv_t_local(d, w):
    kk = w.shape[0]
    out = d * w[kk - 1:kk, :]
    for k in range(kk - 1):
        out = out + _shift_up(d, kk - 1 - k) * w[k:k + 1, :]
    return out


def _col_sum(a):
    return jnp.sum(a, axis=0, keepdims=True)


def _conv_bwd_rot(x_ref, w, dpre, dpre_head, dx_ref, dw_ref, db_ref):
    T = dpre.shape[0]
    kk = w.shape[0]
    x = x_ref[...]
    x_head, x_tail = x_ref[0:EDGE, :], x_ref[T - EDGE:T, :]
    wrong_head = dpre[0:EDGE]
    dx = dpre * w[kk - 1:kk, :]
    for k in range(kk - 1):
        dx = dx + pltpu.roll(dpre, T - (kk - 1 - k), 0) * w[k:k + 1, :]
    dx_ref[...] = dx.astype(dx_ref.dtype)
    top = jnp.concatenate([dpre_head, dpre[EDGE:2 * EDGE]], axis=0)
    dx_ref[0:EDGE, :] = _conv_t_local(top, w)[0:EDGE].astype(dx_ref.dtype)
    dx_ref[T - EDGE:T, :] = _conv_t_local(dpre[T - EDGE:T], w).astype(dx_ref.dtype)
    ends = jnp.concatenate([x_tail, x_head], axis=0)
    dw_ref[kk - 1:kk, :] = _col_sum(dpre * x) + _col_sum((dpre_head - wrong_head) * x_head)
    for k in range(kk - 1):
        s = kk - 1 - k
        rotated_head = pltpu.roll(ends, s, 0)[EDGE:2 * EDGE]
        dw_ref[k:k + 1, :] = (_col_sum(dpre * pltpu.roll(x, s, 0)) - _col_sum(wrong_head * rotated_head)
                              + _col_sum(dpre_head * _shift_down(x_head, s)))
    if db_ref is not None:
        db_ref[...] = _col_sum(dpre) + _col_sum(dpre_head - wrong_head)


def _dsilu(pre):
    sg = jax.nn.sigmoid(pre)
    return sg * (1.0 + pre * (1.0 - sg))


def _conv_silu_fwd(name, src, col0, w, b):
    T = src.shape[0]
    kk, C = w.shape
    cb = CONV_CB
    off = col0 // cb

    def body(*refs):
        x_ref, w_ref, o_ref = refs[0], refs[1], refs[-1]
        b_val = refs[2][...] if b is not None else None
        o_ref[...] = jax.nn.silu(_conv_pre_rot(x_ref[...], w_ref[...], b_val))
        o_ref[0:EDGE, :] = jax.nn.silu(_conv_pre(x_ref[0:EDGE, :], w_ref[...], b_val))

    in_specs = [pl.BlockSpec((T, cb), lambda j: (0, off + j)), pl.BlockSpec((kk, cb), lambda j: (0, j))]
    args = [src, w]
    if b is not None:
        in_specs.append(pl.BlockSpec((1, cb), lambda j: (0, j)))
        args.append(b)
    return pl.pallas_call(body, name=name, grid=(C // cb,), in_specs=in_specs,
                          out_specs=pl.BlockSpec((T, cb), lambda j: (0, j)), out_shape=jax.ShapeDtypeStruct((T, C), F32),
                          compiler_params=_cparams(("parallel",)))(*args)


def _conv_silu_bwd(name, src, col0, w, b, dy, dx_dtype):
    T = src.shape[0]
    kk, C = w.shape
    cb = CONV_CB
    off = col0 // cb
    has_b = b is not None

    def body(*refs):
        x_ref, w_ref = refs[:2]
        pos = 2
        b_val = None
        if has_b:
            b_val = refs[pos][...]; pos += 1
        dy_ref = refs[pos]; pos += 1
        dx_ref, dw_ref = refs[pos], refs[pos + 1]
        db_ref = refs[pos + 2] if has_b else None
        wv = w_ref[...]
        dpre = dy_ref[...] * _dsilu(_conv_pre_rot(x_ref[...], wv, b_val))
        dpre_head = dy_ref[0:EDGE, :] * _dsilu(_conv_pre(x_ref[0:EDGE, :], wv, b_val))
        _conv_bwd_rot(x_ref, wv, dpre, dpre_head, dx_ref, dw_ref, db_ref)

    in_specs = [pl.BlockSpec((T, cb), lambda j: (0, off + j)), pl.BlockSpec((kk, cb), lambda j: (0, j))]
    args = [src, w]
    if has_b:
        in_specs.append(pl.BlockSpec((1, cb), lambda j: (0, j)))
        args.append(b)
    in_specs.append(pl.BlockSpec((T, cb), lambda j: (0, j)))
    args.append(dy)
    out_specs = [pl.BlockSpec((T, cb), lambda j: (0, j)), pl.BlockSpec((kk, cb), lambda j: (0, j))]
    out_shape = [jax.ShapeDtypeStruct((T, C), dx_dtype), jax.ShapeDtypeStruct((kk, C), F32)]
    if has_b:
        out_specs.append(pl.BlockSpec((1, cb), lambda j: (0, j)))
        out_shape.append(jax.ShapeDtypeStruct((1, C), F32))
    return pl.pallas_call(body, name=name, grid=(C // cb,), in_specs=in_specs, out_specs=out_specs, out_shape=out_shape,
                          compiler_params=_cparams(("parallel",)))(*args)


def _ffn_glu_fwd(name, up, w, b, out_dtype=F32):
    T = up.shape[0]
    kk = w.shape[0]
    cb = CONV_CB
    width = up.shape[1] // 2
    nblk = width // cb

    def body(g_ref, u_ref, wg_ref, wu_ref, bg_ref, bu_ref, o_ref):
        g = _conv_pre_rot(g_ref[...], wg_ref[...], bg_ref[...])
        u = _conv_pre_rot(u_ref[...], wu_ref[...], bu_ref[...])
        o_ref[...] = (jax.nn.silu(g) * u).astype(o_ref.dtype)
        g = _conv_pre(g_ref[0:EDGE, :], wg_ref[...], bg_ref[...])
        u = _conv_pre(u_ref[0:EDGE, :], wu_ref[...], bu_ref[...])
        o_ref[0:EDGE, :] = (jax.nn.silu(g) * u).astype(o_ref.dtype)

    lo, hi = (lambda j: (0, j)), (lambda j: (0, nblk + j))
    in_specs = [pl.BlockSpec((T, cb), lo), pl.BlockSpec((T, cb), hi), pl.BlockSpec((kk, cb), lo), pl.BlockSpec((kk, cb), hi),
                pl.BlockSpec((1, cb), lo), pl.BlockSpec((1, cb), hi)]
    return pl.pallas_call(body, name=name, grid=(nblk,), in_specs=in_specs, out_specs=pl.BlockSpec((T, cb), lo),
                          out_shape=jax.ShapeDtypeStruct((T, width), out_dtype),
                          compiler_params=_cparams(("parallel",)))(up, up, w, w, b, b)


def _ffn_glu_bwd(name, up, w, b, dact, dx_dtype):
    T = up.shape[0]
    kk = w.shape[0]
    cb = CONV_CB
    width = up.shape[1] // 2
    nblk = width // cb

    def body(g_ref, u_ref, wg_ref, wu_ref, bg_ref, bu_ref, d_ref, dg_ref, du_ref, dwg_ref, dwu_ref, dbg_ref, dbu_ref):
        wg, wu = wg_ref[...], wu_ref[...]
        g = _conv_pre_rot(g_ref[...], wg, bg_ref[...])
        u = _conv_pre_rot(u_ref[...], wu, bu_ref[...])
        d = d_ref[...].astype(F32)
        g_head = _conv_pre(g_ref[0:EDGE, :], wg, bg_ref[...])
        u_head = _conv_pre(u_ref[0:EDGE, :], wu, bu_ref[...])
        d_head = d_ref[0:EDGE, :].astype(F32)
        sg, sg_head = jax.nn.sigmoid(g), jax.nn.sigmoid(g_head)
        _conv_bwd_rot(g_ref, wg, d * u * (sg * (1.0 + g * (1.0 - sg))),
                      d_head * u_head * (sg_head * (1.0 + g_head * (1.0 - sg_head))), dg_ref, dwg_ref, dbg_ref)
        _conv_bwd_rot(u_ref, wu, d * (g * sg), d_head * (g_head * sg_head), du_ref, dwu_ref, dbu_ref)

    lo, hi = (lambda j: (0, j)), (lambda j: (0, nblk + j))
    in_specs = [pl.BlockSpec((T, cb), lo), pl.BlockSpec((T, cb), hi), pl.BlockSpec((kk, cb), lo), pl.BlockSpec((kk, cb), hi),
                pl.BlockSpec((1, cb), lo), pl.BlockSpec((1, cb), hi), pl.BlockSpec((T, cb), lo)]
    out_specs = [pl.BlockSpec((T, cb), lo)] * 2 + [pl.BlockSpec((kk, cb), lo)] * 2 + [pl.BlockSpec((1, cb), lo)] * 2
    out_shape = ([jax.ShapeDtypeStruct((T, width), dx_dtype)] * 2 + [jax.ShapeDtypeStruct((kk, width), F32)] * 2
                 + [jax.ShapeDtypeStruct((1, width), F32)] * 2)
    return pl.pallas_call(body, name=name, grid=(nblk,), in_specs=in_specs, out_specs=out_specs, out_shape=out_shape,
                          compiler_params=_cparams(("parallel",)))(up, up, w, w, b, b, dact)


def _loss_head(y, target):
    T, D = y.shape
    tb = _tile(T, 256, 8)

    def body(y_ref, t_ref, dy_ref, l_ref):
        @pl.when(pl.program_id(0) == 0)
        def _():
            l_ref[...] = jnp.zeros_like(l_ref)

        err = y_ref[...] - t_ref[...]
        dy_ref[...] = err * (1.0 / D)
        l_ref[...] += jnp.sum(err * err, axis=0, keepdims=True) * (0.5 / D)

    spec = pl.BlockSpec((tb, D), lambda i: (i, 0))
    return pl.pallas_call(body, name="loss_head", grid=(T // tb,), in_specs=[spec, spec],
                          out_specs=[spec, pl.BlockSpec((1, D), lambda i: (0, 0))],
                          out_shape=[jax.ShapeDtypeStruct((T, D), F32), jax.ShapeDtypeStruct((1, D), F32)],
                          compiler_params=_cparams(("arbitrary",)))(y, target)


def _adamw_math(w, g, m, v):
    m = ADAM_B1 * m + (1.0 - ADAM_B1) * g
    v = ADAM_B2 * v + (1.0 - ADAM_B2) * jnp.square(g)
    m_hat = m / (1.0 - ADAM_B1 ** ADAM_STEP)
    v_hat = v / (1.0 - ADAM_B2 ** ADAM_STEP)
    return -ADAM_LR * (m_hat / (jnp.sqrt(v_hat) + ADAM_EPS) + ADAM_WD * w), m, v


def _adamw(name, w, g, m, v, after=None):
    A, R, C = w.shape
    if C % LANES == 0:
        rb, cb = _slab(R, C)
    else:
        rb, cb = _tile(R, max(8, SLAB_BYTES // 2 // (C * 4) // 8 * 8), 8), C
    extra = [] if after is None else [after]

    def body(w_ref, g_ref, m_ref, v_ref, *rest):
        d_ref, mo_ref, vo_ref = rest[-3:]
        d, mn, vn = _adamw_math(w_ref[...], g_ref[...], m_ref[...], v_ref[...])
        d_ref[...] = d
        mo_ref[...] = mn
        vo_ref[...] = vn

    spec = pl.BlockSpec((1, rb, cb), lambda a, r, q: (a, r, q))
    return pl.pallas_call(body, name=name, grid=(A, R // rb, C // cb), in_specs=[spec] * 4 + [ANY] * len(extra),
                          out_specs=[spec] * 3, out_shape=[jax.ShapeDtypeStruct(w.shape, F32)] * 3,
                          compiler_params=_cparams(("parallel", "parallel", "parallel")))(w, g, m, v, *extra)


def _adamw_small(parts, w, m, v):
    def body(p_ref, w_ref, m_ref, v_ref, g_ref, d_ref, mo_ref, vo_ref):
        g = p_ref[0]
        for i in range(1, N_DEV):
            g = g + p_ref[i]
        d, mn, vn = _adamw_math(w_ref[...], g, m_ref[...], v_ref[...])
        g_ref[...] = g
        d_ref[...] = d
        mo_ref[...] = mn
        vo_ref[...] = vn

    return pl.pallas_call(body, name="adamw_small", out_shape=[jax.ShapeDtypeStruct(w.shape, F32)] * 4,
                          compiler_params=_cparams())(parts, w, m, v)


def _add_blocks(name, a, b, out_dtype=F32):
    n, R, W = a.shape
    rb = _tile(R, 512, 8)

    def body(a_ref, b_ref, o_ref):
        o_ref[...] = (a_ref[...].astype(F32) + b_ref[...].astype(F32)).astype(o_ref.dtype)

    spec = pl.BlockSpec((1, rb, W), lambda i, r: (i, r, 0))
    return pl.pallas_call(body, name=name, grid=(n, R // rb), in_specs=[spec, spec], out_specs=spec,
                          out_shape=jax.ShapeDtypeStruct(a.shape, out_dtype),
                          compiler_params=_cparams(("parallel", "parallel")))(a, b)


SLAB_BYTES = 5 << 19


def _slab(R, W):
    if R % 16 == 0:
        return _tile(R, max(16, SLAB_BYTES // (4 * W) // 16 * 16), 16), W
    assert W % LANES == 0, (R, W)
    return R, _tile(W, max(LANES, SLAB_BYTES // (4 * R) // LANES * LANES), LANES)


def _pair_add(name, g, other, c, chip):
    _, R, W = g.shape
    rb, cb = _slab(R, W)

    def body(s_ref, a_ref, b_ref, send_ref, own_ref):
        s = a_ref[0] + b_ref[0]
        send_ref[0] = s.astype(send_ref.dtype)

        @pl.when(pl.program_id(2) == s_ref[1])
        def _():
            own_ref[...] = s

    grid_spec = pltpu.PrefetchScalarGridSpec(
        num_scalar_prefetch=1, grid=(R // rb, W // cb, 4),
        in_specs=[pl.BlockSpec((1, rb, cb), lambda r, q, p, s_ref: (2 * p + s_ref[0], r, q)),
                  pl.BlockSpec((1, rb, cb), lambda r, q, p, s_ref: (p, r, q))],
        out_specs=[pl.BlockSpec((1, rb, cb), lambda r, q, p, s_ref: (p, r, q)),
                   pl.BlockSpec((rb, cb), lambda r, q, p, s_ref: (r, q))])
    scalars = jnp.stack([c, chip]).astype(jnp.int32)
    return pl.pallas_call(body, name=name, grid_spec=grid_spec,
                          out_shape=[jax.ShapeDtypeStruct((4, R, W), MXU_DTYPE), jax.ShapeDtypeStruct((R, W), F32)],
                          compiler_params=_cparams(("parallel", "parallel", "arbitrary")))(scalars, g, other)


def _sum4(name, own, parts):
    R, W = own.shape
    rb, cb = _slab(R, W)

    def body(o_ref, p_ref, out_ref):
        out_ref[...] = ((o_ref[...] + p_ref[0].astype(F32)) + p_ref[1].astype(F32)) + p_ref[2].astype(F32)

    return pl.pallas_call(body, name=name, grid=(R // rb, W // cb),
                          in_specs=[pl.BlockSpec((rb, cb), lambda r, q: (r, q)), pl.BlockSpec((3, rb, cb), lambda r, q: (0, r, q))],
                          out_specs=pl.BlockSpec((rb, cb), lambda r, q: (r, q)), out_shape=jax.ShapeDtypeStruct((R, W), F32),
                          compiler_params=_cparams(("parallel", "parallel")))(own, parts)


MESH = pl.DeviceIdType.MESH
ANY = pl.BlockSpec(memory_space=pl.ANY)


def _place():
    return lax.axis_index("x"), lax.axis_index("y"), lax.axis_index("c")


def _other_chips(x, y):
    return [(1 - x, y), (x, 1 - y), (1 - x, 1 - y)]


def _all_gather(name, blocks):
    n = len(blocks)

    def body(*refs):
        x_refs, out_refs = refs[:n], refs[n:2 * n]
        send_sems, recv_sems, local_sems = refs[2 * n:]
        x, y, c = _place()
        me, sibling = (x, y, c), (x, y, 1 - c)
        chips = _other_chips(x, y)

        def slot(a, px, py, pc):
            return out_refs[a].at[4 * px + 2 * py + pc]

        def copy(a, k, blk, to, src=None):
            return pltpu.make_async_remote_copy(src_ref=slot(a, *blk) if src is None else src, dst_ref=slot(a, *blk),
                                                send_sem=send_sems.at[a, k], recv_sem=recv_sems.at[a, k],
                                                device_id=to, device_id_type=MESH)

        mine = [pltpu.make_async_copy(x_refs[a], slot(a, *me), local_sems.at[a]) for a in range(n)]
        for cp in mine:
            cp.start()
        first = []
        for j, chip in enumerate(chips):
            first += [copy(a, 1 + j, me, (*chip, c), src=x_refs[a]) for a in range(n)]
        first += [copy(a, 0, me, sibling, src=x_refs[a]) for a in range(n)]
        for cp in first:
            cp.start()
        passed = []
        for j, chip in enumerate(chips):
            for a in range(n):
                copy(a, 1 + j, (*chip, c), me).wait_recv()
                passed.append(copy(a, 4 + j, (*chip, c), sibling))
                passed[-1].start()
        for a in range(n):
            copy(a, 0, sibling, me).wait_recv()
        for j, chip in enumerate(chips):
            for a in range(n):
                copy(a, 4 + j, (*chip, 1 - c), me).wait_recv()
        for cp in first + passed:
            cp.wait_send()
        for cp in mine:
            cp.wait()

    return pl.pallas_call(body, name=name, in_specs=[ANY] * n, out_specs=[ANY] * n,
                          out_shape=[jax.ShapeDtypeStruct((N_DEV,) + b.shape, b.dtype) for b in blocks],
                          scratch_shapes=[pltpu.SemaphoreType.DMA((n, 7)), pltpu.SemaphoreType.DMA((n, 7)),
                                          pltpu.SemaphoreType.DMA((n,))])(*blocks)


def _routes_to_sibling(x, y, c):
    return [(2 * p + (1 - c), p, (x, y, 1 - c)) for p in range(4)]


def _routes_to_chips(x, y, c):
    return [(2 * px + py, j, (px, py, c)) for j, (px, py) in enumerate(_other_chips(x, y))]


def _routes_block_to_chips(x, y, c):
    me = 4 * x + 2 * y + c
    return [(me, me, (px, py, c)) for px, py in _other_chips(x, y)]


def _routes_blocks_to_sibling(x, y, c):
    return [(4 * px + 2 * py + c, 4 * px + 2 * py + c, (x, y, 1 - c)) for px, py in [(x, y)] + _other_chips(x, y)]


def _route_copies(routes, src_refs, land_refs, send_sems, recv_sems):
    x, y, c = _place()
    copies = []
    for a, (src, land) in enumerate(zip(src_refs, land_refs)):
        plan = routes(x, y, c)
        for k, (s, d, target) in enumerate(plan):
            i = a * len(plan) + k
            copies.append(pltpu.make_async_remote_copy(src_ref=src.at[s], dst_ref=land.at[d], send_sem=send_sems.at[i],
                                                       recv_sem=recv_sems.at[i], device_id=target, device_id_type=MESH))
    return copies


def _exchange(name, routes, n_routes, srcs, land_slots):
    n = len(srcs)

    def body(*refs):
        copies = _route_copies(routes, refs[:n], refs[n:2 * n], refs[2 * n], refs[2 * n + 1])
        for cp in copies:
            cp.start()
        for cp in copies:
            cp.wait_recv()
        for cp in copies:
            cp.wait_send()

    return pl.pallas_call(body, name=name, in_specs=[ANY] * n, out_specs=[ANY] * n,
                          out_shape=[jax.ShapeDtypeStruct((land_slots,) + s.shape[1:], s.dtype) for s in srcs],
                          scratch_shapes=[pltpu.SemaphoreType.DMA((n * n_routes,)), pltpu.SemaphoreType.DMA((n * n_routes,))])(*srcs)


HBM_SPEC = pl.BlockSpec(memory_space=pltpu.HBM)
SEM_SPEC = pl.BlockSpec(memory_space=pltpu.SEMAPHORE)
DATAFLOW = pltpu.SideEffectType.DATAFLOW_SIDE_EFFECTING


def _exchange_start(name, routes, n_routes, srcs, lands, after=None):
    n = len(srcs)
    in_place = lands is None
    bufs = list(srcs) + ([] if in_place else list(lands))
    nb = len(bufs)
    extra = [] if after is None else [after]

    def body(*refs):
        src_refs = refs[:n]
        land_refs = src_refs if in_place else refs[n:nb]
        send_sems, recv_sems = refs[nb + len(extra)], refs[nb + len(extra) + 1]
        token = refs[-1]
        for cp in _route_copies(routes, src_refs, land_refs, send_sems, recv_sems):
            cp.start()
        token[...] = jnp.zeros_like(token)

    sems = [pltpu.SemaphoreType.DMA((n * n_routes,)), pltpu.SemaphoreType.DMA((n * n_routes,))]
    out = pl.pallas_call(
        body, name=name, in_specs=[HBM_SPEC] * nb + [ANY] * len(extra),
        out_shape=sems + [pltpu.HBM(b.shape, b.dtype) for b in bufs] + [jax.ShapeDtypeStruct((8, LANES), F32)],
        out_specs=[SEM_SPEC, SEM_SPEC] + [HBM_SPEC] * nb + [pl.BlockSpec(memory_space=pltpu.VMEM)],
        input_output_aliases={i: 2 + i for i in range(nb)},
        compiler_params=pltpu.CompilerParams(has_side_effects=DATAFLOW))(
        *[pltpu.with_memory_space_constraint(b, pltpu.HBM) for b in bufs], *extra)
    return (out[0], out[1], list(out[2:2 + nb])), out[-1]


def _exchange_wait(name, routes, n_routes, n, started, after):
    send_sems, recv_sems, bufs = started
    nb = len(bufs)
    in_place = nb == n

    def body(*refs):
        src_refs = refs[:n]
        land_refs = src_refs if in_place else refs[n:nb]
        for cp in _route_copies(routes, src_refs, land_refs, refs[nb], refs[nb + 1]):
            cp.wait_send()
            cp.wait_recv()

    out = pl.pallas_call(
        body, name=name, in_specs=[HBM_SPEC] * nb + [SEM_SPEC, SEM_SPEC, ANY],
        out_shape=[pltpu.HBM(b.shape, b.dtype) for b in bufs], out_specs=[HBM_SPEC] * nb,
        input_output_aliases={i: i for i in range(nb)},
        compiler_params=pltpu.CompilerParams(has_side_effects=DATAFLOW))(*bufs, send_sems, recv_sems, after)
    return list(out[:n]) if in_place else (list(out[:n]), list(out[n:]))


def _pair_sums(tag, gs, from_sibling):
    x, y, c = _place()
    return [_pair_add(f"rs_add_{tag}_{i}", g, o, c, 2 * x + y) for i, (g, o) in enumerate(zip(gs, from_sibling))]


def _reduce_scatter(tag, gs):
    sums = _pair_sums(tag, gs, _exchange(f"rs_swap_{tag}", _routes_to_sibling, 4, gs, 4))
    got = _exchange(f"rs_chips_{tag}", _routes_to_chips, 3, [s[0] for s in sums], 3)
    return [_sum4(f"rs_sum_{tag}_{i}", s[1], q) for i, (s, q) in enumerate(zip(sums, got))]


def _reduce_scatter_begin(tag, gs):
    lands = [lax.empty((4,) + g.shape[1:], g.dtype) for g in gs]
    swap, token = _exchange_start(f"rs_swap_{tag}_start", _routes_to_sibling, 4, gs, lands)
    return dict(tag=tag, n=len(gs), swap=swap), token


def _reduce_scatter_middle(state, after):
    tag, n = state["tag"], state["n"]
    gs, from_sibling = _exchange_wait(f"rs_swap_{tag}_wait", _routes_to_sibling, 4, n, state["swap"], after)
    state["sums"] = _pair_sums(tag, gs, from_sibling)
    partials = [s[0] for s in state["sums"]]
    lands = [lax.empty((3,) + p.shape[1:], p.dtype) for p in partials]
    state["chips"], token = _exchange_start(f"rs_chips_{tag}_start", _routes_to_chips, 3, partials, lands)
    return token


def _reduce_scatter_end(state, after):
    tag = state["tag"]
    _, got = _exchange_wait(f"rs_chips_{tag}_wait", _routes_to_chips, 3, state["n"], state["chips"], after)
    return [_sum4(f"rs_sum_{tag}_{i}", s[1], q) for i, (s, q) in enumerate(zip(state["sums"], got))]


def _all_gather_begin(tag, blocks, after):
    dev = 4 * lax.axis_index("x") + 2 * lax.axis_index("y") + lax.axis_index("c")
    zones = [lax.dynamic_update_slice_in_dim(lax.empty((N_DEV,) + b.shape, b.dtype), b[None], dev, axis=0) for b in blocks]
    chips, token = _exchange_start(f"gather_{tag}_chips_start", _routes_block_to_chips, 3, zones, None, after)
    return dict(tag=tag, n=len(blocks), chips=chips), token


def _all_gather_middle(state, after):
    tag, n = state["tag"], state["n"]
    zones = _exchange_wait(f"gather_{tag}_chips_wait", _routes_block_to_chips, 3, n, state["chips"], after)
    state["sibling"], token = _exchange_start(f"gather_{tag}_sibling_start", _routes_blocks_to_sibling, 4, zones, None)
    return token


def _all_gather_end(state, after):
    return _exchange_wait(f"gather_{state['tag']}_sibling_wait", _routes_blocks_to_sibling, 4, state["n"], state["sibling"], after)


PACK_UNIT = 8 * LANES


def _packed_size(shape):
    return -(-math.prod(shape) // PACK_UNIT) * PACK_UNIT


def _pack(arrays, dtype):
    parts = []
    for a in arrays:
        flat = a.reshape(-1).astype(dtype)
        parts.append(jnp.pad(flat, (0, _packed_size(a.shape) - flat.shape[0])))
    return jnp.concatenate(parts).reshape(-1, LANES)


def _unpack(flat, shapes, lead=()):
    out, row = [], 0
    for s in shapes:
        rows = _packed_size(s) // LANES
        piece = flat[..., row:row + rows, :].reshape(lead + (rows * LANES,))
        out.append(piece[..., :math.prod(s)].reshape(lead + tuple(s)))
        row += rows
    return out


def _ffn_pad_rows(a):
    n = a.shape[0] // FFN_HALF
    a = jnp.pad(a.reshape(n, FFN_HALF, a.shape[1]), ((0, 0), (0, FFN_HALF_PAD - FFN_HALF), (0, 0)))
    return a.reshape(n * FFN_HALF_PAD, a.shape[2])


def _ffn_unpad_rows(a):
    n = a.shape[0] // FFN_HALF_PAD
    return a.reshape(n, FFN_HALF_PAD, a.shape[1])[:, :FFN_HALF].reshape(n * FFN_HALF, a.shape[1])


def _ffn_pad_cols(a):
    n = a.shape[1] // FFN_HALF
    a = jnp.pad(a.reshape(a.shape[0], n, FFN_HALF), ((0, 0), (0, 0), (0, FFN_HALF_PAD - FFN_HALF)))
    return a.reshape(a.shape[0], n * FFN_HALF_PAD)


def _ffn_unpad_cols(a):
    n = a.shape[1] // FFN_HALF_PAD
    return a.reshape(a.shape[0], n, FFN_HALF_PAD)[:, :, :FFN_HALF].reshape(a.shape[0], n * FFN_HALF)


def _shard_to_send(name, shard):
    if name in ("w_in", "w_br_gdn", "w_br_gla"):
        shard = shard.T
    elif name == "ffn_w_up":
        shard = _ffn_pad_rows(shard.T)
    return shard.astype(MXU_DTYPE)


KEPT_TRANSPOSED = ("w_in", "w_br_gdn", "w_br_gla", "ffn_w_up")


def _whole_from_gathered(name, g):
    if name == "w_in":
        return _in_proj_from_shards(g)
    if name == "ffn_w_down":
        return jnp.pad(g, ((0, 0), (0, FFN_HALF_PAD - FFN_HALF), (0, 0))).reshape(FFN_PAD, g.shape[2])
    return g.reshape(N_DEV * g.shape[1], g.shape[2])


def _slots_from_whole(name, gw):
    if name == "w_in":
        return _in_proj_to_slots(gw)
    return gw.reshape(N_DEV, gw.shape[0] // N_DEV, gw.shape[1])


def _shard_from_slot(name, s):
    if name == "ffn_w_up":
        return _ffn_unpad_rows(s)
    if name == "ffn_w_down":
        return s[:FFN_HALF]
    return s


def _in_proj_pieces():
    starts, pos = {}, 0
    for n, width in IN_SPLITS:
        starts[n] = (pos, width)
        pos += width
    return [(starts[ref][0], off + lane, starts[ref][1]) for _, off, _, pieces in PAD_SEGS for ref, lane in pieces]


def _in_proj_moves():
    cs = IN_DIM // N_DEV
    moves = []
    for src, dst, n in sorted(_in_proj_pieces()):
        at = src
        while at < src + n:
            d = at // cs
            end = min(src + n, (d + 1) * cs)
            moves.append((d, at - d * cs, dst + at - src, end - at))
            at = end
    return moves


RELAYOUT_LANES = 128


def _in_proj_from_shards(g):
    _, cs, D = g.shape

    def body(g_ref, o_ref):
        o_ref[...] = jnp.zeros_like(o_ref)
        for d, i0, r0, n in _in_proj_moves():
            o_ref[r0:r0 + n, :] = g_ref[d, i0:i0 + n, :]

    cb = RELAYOUT_LANES
    return pl.pallas_call(body, name="w_in_rows_in", grid=(D // cb,),
                          in_specs=[pl.BlockSpec((N_DEV, cs, cb), lambda j: (0, 0, j))],
                          out_specs=pl.BlockSpec((IN_PAD, cb), lambda j: (0, j)),
                          out_shape=jax.ShapeDtypeStruct((IN_PAD, D), g.dtype), compiler_params=_cparams(("parallel",)))(g)


def _in_proj_to_slots(gw):
    D = gw.shape[1]
    cs = IN_DIM // N_DEV

    def body(x_ref, o_ref):
        for d, i0, r0, n in _in_proj_moves():
            o_ref[d, i0:i0 + n, :] = x_ref[r0:r0 + n, :]

    cb = RELAYOUT_LANES
    return pl.pallas_call(body, name="w_in_rows_out", grid=(D // cb,),
                          in_specs=[pl.BlockSpec((IN_PAD, cb), lambda j: (0, j))],
                          out_specs=pl.BlockSpec((N_DEV, cs, cb), lambda j: (0, 0, j)),
                          out_shape=jax.ShapeDtypeStruct((N_DEV, cs, D), gw.dtype), compiler_params=_cparams(("parallel",)))(gw)


def _pad_in_proj_rows(w):
    rows, at = [], 0
    for src, dst, n in sorted(_in_proj_pieces(), key=lambda p: p[1]):
        if dst > at:
            rows.append(jnp.zeros((dst - at, w.shape[1]), w.dtype))
        rows.append(w[src:src + n])
        at = dst + n
    rows.append(jnp.zeros((IN_PAD - at, w.shape[1]), w.dtype))
    return jnp.concatenate(rows, axis=0)


def _unpad_in_proj_rows(wp):
    return jnp.concatenate([wp[dst:dst + n] for _, dst, n in sorted(_in_proj_pieces())], axis=0)


def _lane_pad(a, width=LANES):
    return jnp.pad(a, ((0, 0), (0, width - a.shape[1])))


def _seg_blk(h, name, rows):
    off, width = SEG[name]
    return (h, rows, width, off // width)


def _ln_both(xs_, ps_):
    (y,) = _ln_fn(xs_, ps_)
    return (y, y)


def _behind(param, hooks, stage, *seen):
    if hooks is None or stage not in hooks:
        return param
    token = hooks[stage](*seen)
    return param if token is None else param + token[0:1, 0:1]


def _layer_fwd(l, x, x_mx, W, sp, hooks=None):
    T = x.shape[0]
    n64, ngla, ntok = T // SSD_CHUNK, T // GLA_BLOCK, T // 256
    h = _mm(f"in_proj_{l}", x_mx, W["w_in"], "nt")
    xbc = _conv_silu_fwd(f"ssd_conv_{l}", h, SEG["xbc"][0], sp["ssd_conv_w"], sp["ssd_conv_b"])
    gqkv = _conv_silu_fwd(f"gdn_conv_{l}", h, SEG["gqkv"][0], sp["gdn_conv_w"], None)

    ssd_in = [(xbc, SSD_BLOCK, SSD_XBC, 0), _seg_blk(h, "dt", SSD_BLOCK), _seg_blk(h, "z", SSD_BLOCK)]
    ssd_p = [sp["ssd_dt_bias"], sp["ssd_a_log"], sp["ssd_d"], sp["ssd_norm_w"]]
    o_ssd, ssd_states = _chain_fwd(f"ssd_fwd_{l}", _ssd_block, T // SSD_BLOCK, ssd_in, ssd_p,
                                   [(SSD_BLOCK, SSD_INNER, MXU_DTYPE)], (SSD_STATE, SSD_INNER))
    o_gdn, gdn_saved = _gdn_forward(str(l), gqkv, h, dict(sp, gdn_a_log=_behind(sp["gdn_a_log"], hooks, "ssd", o_ssd)))
    gla_in = [_seg_blk(h, "lqkv", GLA_BLOCK), _seg_blk(h, "lglr", GLA_BLOCK), _seg_blk(h, "lr", GLA_BLOCK)]
    gla_p = [jnp.pad(sp["gla_gate_w2"], ((0, LANES - GLA_RANK), (0, 0))), sp["gla_gate_b"], sp["gla_norm_w"]]
    o_gla, gla_states = _chain_fwd(f"gla_fwd_{l}", _gla_block, ngla, gla_in, gla_p, [(GLA_BLOCK, GLA_V, MXU_DTYPE)],
                                   (GLA_VAL_DIM, GLA_K))
    ln1_p = [_behind(sp["ln1_g"], hooks, "mixed", o_gdn), sp["ln1_b"]]
    y_ssd = _mm(f"br_ssd_{l}", o_ssd, W["w_br_ssd"])
    y_gdn = _mm(f"br_gdn_{l}", o_gdn, W["w_br_gdn"], "nt")
    y_gla = _mm(f"br_gla_{l}", o_gla, W["w_br_gla"], "nt")
    merge_in = [_seg_blk(h, "gates", 256), (y_ssd, 256, D_MODEL, 0), (y_gdn, 256, D_MODEL, 0), (y_gla, 256, D_MODEL, 0)]
    (mix,) = _chain_fwd(f"merge_{l}", _merge_fn, ntok, merge_in, [], [(256, D_MODEL, MXU_DTYPE)])
    r1 = _mm(f"out_proj_{l}", mix, W["w_out"])
    both = [(256, D_MODEL, F32), (256, D_MODEL, MXU_DTYPE)]
    x1, x1_mx = _chain_fwd(f"ln1_{l}", _ln_both, ntok, [(x, 256, D_MODEL, 0), (r1, 256, D_MODEL, 0)], ln1_p, both)
    up = _mm(f"ffn_up_{l}", x1_mx, W["ffn_w_up"], "nt")
    act = _ffn_glu_fwd(f"ffn_glu_{l}", up, sp["ffn_conv_w_pad"], sp["ffn_conv_b_pad"], MXU_DTYPE)
    ln2_p = [_behind(sp["ln2_g"], hooks, "ffn_act", act), sp["ln2_b"]]
    r2 = _mm(f"ffn_down_{l}", act, W["ffn_w_down"], tn=1024, tk=1024)
    x2, x2_mx = _chain_fwd(f"ln2_{l}", _ln_both, ntok, [(x1, 256, D_MODEL, 0), (r2, 256, D_MODEL, 0)], ln2_p, both)
    saved = dict(x=x, x_mx=x_mx, h=h, xbc=xbc, gqkv=gqkv, ssd_in=ssd_in, ssd_p=ssd_p, ssd_states=ssd_states,
                 gdn=gdn_saved, gla_in=gla_in, gla_p=gla_p, gla_states=gla_states, o_ssd=o_ssd,
                 o_gdn=o_gdn, o_gla=o_gla, merge_in=merge_in, mix=mix, r1=r1, ln1_p=ln1_p, x1=x1, x1_mx=x1_mx, up=up, act=act,
                 r2=r2, ln2_p=ln2_p)
    return x2, x2_mx, saved


def _layer_bwd(l, dx2, W, sp, sv, hooks=None):
    T = dx2.shape[0]
    n64, ngla, ntok = T // SSD_CHUNK, T // GLA_BLOCK, T // 256
    bf = MXU_DTYPE
    gw, gs = {}, {}
    ln2_p = [_behind(sv["ln2_p"][0], hooks, "start"), sv["ln2_p"][1]]
    (dx1_a, dr2), (gs["ln2_g"], gs["ln2_b"]) = _chain_bwd(
        f"ln2_bwd_{l}", _ln_fn, ntok, [(sv["x1"], 256, D_MODEL, 0), (sv["r2"], 256, D_MODEL, 0)], ln2_p,
        [(dx2, 256, D_MODEL)], dx_dtypes=[F32, bf])
    gw["ffn_w_down"] = _mm(f"ffn_down_dw_{l}", sv["act"], dr2, "tn", tn=1024)
    dact = _mm(f"ffn_down_dx_{l}", dr2, W["ffn_w_down"], "nt")
    dg, du, dwg, dwu, dbg, dbu = _ffn_glu_bwd(f"ffn_glu_bwd_{l}", sv["up"], sp["ffn_conv_w_pad"], sp["ffn_conv_b_pad"], dact, bf)
    gs["ffn_conv_w"] = _ffn_unpad_cols(jnp.concatenate([dwg, dwu], axis=1))
    gs["ffn_conv_b"] = _ffn_unpad_cols(jnp.concatenate([dbg, dbu], axis=1))
    dup = jnp.concatenate([dg, du], axis=1)
    gw["ffn_w_up"] = _mm(f"ffn_up_dw_{l}", dup, sv["x1_mx"], "tn", tn=1024)
    dx1_b = _mm(f"ffn_up_dx_{l}", dup, W["ffn_w_up"], "nn", tn=1024, tk=1024)
    ln1_p = [_behind(sv["ln1_p"][0], hooks, "ffn", dx1_b), sv["ln1_p"][1]]
    (dx_a, dr1), (gs["ln1_g"], gs["ln1_b"]) = _chain_bwd(
        f"ln1_bwd_{l}", _ln_sum_fn, ntok, [(sv["x"], 256, D_MODEL, 0), (sv["r1"], 256, D_MODEL, 0)], ln1_p,
        [(dx1_a, 256, D_MODEL), (dx1_b, 256, D_MODEL)], dx_dtypes=[F32, bf])
    gw["w_out"] = _mm(f"out_proj_dw_{l}", sv["mix"], dr1, "tn")
    dmix = _mm(f"out_proj_dx_{l}", dr1, W["w_out"], "nt")
    (dgates, dy_ssd, dy_gdn, dy_gla), _ = _chain_bwd(f"merge_bwd_{l}", _merge_fn, ntok, sv["merge_in"], [],
                                                     [(dmix, 256, D_MODEL)], dx_dtypes=[bf, bf, bf, bf])
    gw["w_br_ssd"] = _mm(f"br_ssd_dw_{l}", sv["o_ssd"], dy_ssd, "tn")
    gw["w_br_gdn"] = _mm(f"br_gdn_dw_{l}", dy_gdn, sv["o_gdn"], "tn")
    gw["w_br_gla"] = _mm(f"br_gla_dw_{l}", dy_gla, sv["o_gla"], "tn")
    do_ssd = _mm(f"br_ssd_dx_{l}", dy_ssd, W["w_br_ssd"], "nt")
    do_gdn = _mm(f"br_gdn_dx_{l}", dy_gdn, W["w_br_gdn"], "nn")
    do_gla = _mm(f"br_gla_dx_{l}", dy_gla, W["w_br_gla"], "nn")

    ssd_p = [_behind(sv["ssd_p"][0], hooks, "branches", do_gla, gw)] + list(sv["ssd_p"][1:])
    (dxbc, ddt, dz), dps = _chain_bwd(f"ssd_bwd_{l}", _ssd_block, T // SSD_BLOCK, sv["ssd_in"], ssd_p,
                                      [(do_ssd, SSD_BLOCK, SSD_INNER)], sprev=sv["ssd_states"], dx_dtypes=[F32, bf, bf])
    gs["ssd_dt_bias"], gs["ssd_a_log"], gs["ssd_d"], gs["ssd_norm_w"] = dps
    gdn_sv = dict(sv["gdn"], scan_p=[_behind(sv["gdn"]["scan_p"][0], hooks, "ssd", dz)])
    dgqkv, dgab, dgg, gs["gdn_a_log"], gs["gdn_dt_bias"], gs["gdn_norm_w"] = _gdn_backward(str(l), do_gdn, gdn_sv, bf)
    (dlqkv, dlglr, dlr), dps = _chain_bwd(f"gla_bwd_{l}", _gla_block, ngla, sv["gla_in"], sv["gla_p"],
                                          [(do_gla, GLA_BLOCK, GLA_V)], sprev=sv["gla_states"], dx_dtypes=[bf, bf, bf])
    gs["gla_gate_w2"], gs["gla_gate_b"], gs["gla_norm_w"] = dps[0][:GLA_RANK], dps[1], dps[2]
    dxbc_pre, gs["ssd_conv_w"], gs["ssd_conv_b"] = _conv_silu_bwd(
        f"ssd_conv_bwd_{l}", sv["h"], SEG["xbc"][0], sp["ssd_conv_w"], sp["ssd_conv_b"], dxbc, bf)
    dgqkv_pre, gs["gdn_conv_w"] = _conv_silu_bwd(f"gdn_conv_bwd_{l}", sv["h"], SEG["gqkv"][0], sp["gdn_conv_w"], None, dgqkv, bf)
    pieces = dict(gates=dgates, xbc=dxbc_pre, gqkv=dgqkv_pre, z=dz, lqkv=dlqkv, gg=dgg, lr=dlr, dt=ddt, gab=dgab, lglr=dlglr)
    cols = [pieces[name] for name, _, _, _ in PAD_SEGS]
    cols.append(jnp.zeros((T, IN_PAD - PAD_SEGS[-1][1] - PAD_SEGS[-1][2]), bf))
    dh = jnp.concatenate(cols, axis=1)
    gw["w_in"] = _mm(f"in_proj_dw_{l}", dh, sv["x_mx"], "tn", tn=1024)
    behind = hooks["w_in_grad"](gw) if hooks is not None and "w_in_grad" in hooks else None
    dx_b = _mm(f"in_proj_dx_{l}", dh, W["w_in"], "nn", tm=1024, tn=1024, tk=IN_PAD // 4, after=behind)
    dx = _add_blocks(f"dx_add_{l}", dx_a[None], dx_b[None])[0]
    return dx, gw, gs


def _ln_sum_fn(xs_, ps_):
    (y,) = _ln_fn(xs_, ps_)
    return (y, y)


def _small_2d(name, a):
    return a.reshape(1, -1) if a.ndim == 1 else a


def kernel(x, w_in, ssd_conv_w, ssd_conv_b, ssd_dt_bias, ssd_a_log, ssd_d, ssd_norm_w, gdn_conv_w, gdn_a_log, gdn_dt_bias, gdn_norm_w, gla_gate_w2, gla_gate_b, gla_norm_w, w_br_ssd, w_br_gdn, w_br_gla, w_out, ln1_g, ln1_b, ffn_w_up, ffn_conv_w, ffn_conv_b, ffn_w_down, ln2_g, ln2_b, loss_target, m_w_in, m_ssd_conv_w, m_ssd_conv_b, m_ssd_dt_bias, m_ssd_a_log, m_ssd_d, m_ssd_norm_w, m_gdn_conv_w, m_gdn_a_log, m_gdn_dt_bias, m_gdn_norm_w, m_gla_gate_w2, m_gla_gate_b, m_gla_norm_w, m_w_br_ssd, m_w_br_gdn, m_w_br_gla, m_w_out, m_ln1_g, m_ln1_b, m_ffn_w_up, m_ffn_conv_w, m_ffn_conv_b, m_ffn_w_down, m_ln2_g, m_ln2_b, v_w_in, v_ssd_conv_w, v_ssd_conv_b, v_ssd_dt_bias, v_ssd_a_log, v_ssd_d, v_ssd_norm_w, v_gdn_conv_w, v_gdn_a_log, v_gdn_dt_bias, v_gdn_norm_w, v_gla_gate_w2, v_gla_gate_b, v_gla_norm_w, v_w_br_ssd, v_w_br_gdn, v_w_br_gla, v_w_out, v_ln1_g, v_ln1_b, v_ffn_w_up, v_ffn_conv_w, v_ffn_conv_b, v_ffn_w_down, v_ln2_g, v_ln2_b):
    args = locals()
    w = {n: args[n] for n in WEIGHTS}
    m = {n: args["m_" + n] for n in WEIGHTS}
    v = {n: args["v_" + n] for n in WEIGHTS}
    dev = 4 * lax.axis_index("x") + 2 * lax.axis_index("y") + lax.axis_index("c")
    xl = x[0]
    tgt = loss_target[0]

    late = BIG[1:]

    def send(names, l):
        return [_shard_to_send(n, w[n][l]) for n in names]

    def whole_weights(names, got):
        return {n: _whole_from_gathered(n, g) for n, g in zip(names, got)}

    got0 = _all_gather("gather_first", send(BIG[:1], 0) + [w[n] for n in SMALL_SHARDED])
    gather0, token0 = _all_gather_begin("w_0", send(late, 0), got0[0])
    W = [whole_weights(BIG[:1], got0[:1]), None]
    whole = dict(w)
    for n, s in zip(SMALL_SHARDED, got0[1:]):
        whole[n] = jnp.transpose(s, (1, 2, 0, 3)).reshape(s.shape[1], s.shape[2], N_DEV * s.shape[3])
    SP = [{n: _small_2d(n, whole[n][l]) for n in SMALL} for l in range(DEPTH)]
    for sp in SP:
        sp["ffn_conv_w_pad"] = _ffn_pad_cols(sp["ffn_conv_w"])
        sp["ffn_conv_b_pad"] = _ffn_pad_cols(sp["ffn_conv_b"])

    held = {}

    def late_weights_cross(o_ssd):
        token = _all_gather_middle(gather0, o_ssd)
        held["gather1"], token1 = _all_gather_begin("w_1", send(BIG, 1), o_ssd)
        return token + token1

    def late_weights_arrive(mixed):
        W[0].update(whole_weights(late, _all_gather_end(gather0, mixed)))

    fwd_hooks = {"ssd": late_weights_cross, "mixed": late_weights_arrive,
                 "ffn_act": lambda act: _all_gather_middle(held["gather1"], act)}
    saved = [None] * DEPTH
    act, act_mx, saved[0] = _layer_fwd(0, xl, (xl + token0[0, 0]).astype(MXU_DTYPE), W[0], SP[0], hooks=fwd_hooks)
    W[1] = whole_weights(BIG, _all_gather_end(held["gather1"], act))
    act, act_mx, saved[1] = _layer_fwd(1, act, act_mx, W[1], SP[1])
    dy, loss_parts = _loss_head(act, tgt)
    loss = lax.psum(jnp.sum(loss_parts), ("x", "y", "c"))

    def slots_of(names, gw):
        return [_slots_from_whole(n, gw[n]) for n in names]

    grads = {}
    GS = [None] * DEPTH
    dy, gw, GS[1] = _layer_bwd(1, dy, W[1], SP[1], saved[1])
    reduce1, reduce1_token = _reduce_scatter_begin("1", slots_of(BIG, gw))

    def late_grads_leave(seen, gw0):
        held["reduce0"], token = _reduce_scatter_begin("0", slots_of(late, gw0))
        return token

    def w_in_grad_leaves(gw0):
        held["reduce_first"], token = _reduce_scatter_begin("first", slots_of(BIG[:1], gw0))
        return token

    bwd_hooks = {"start": lambda: reduce1_token, "ffn": lambda seen: _reduce_scatter_middle(reduce1, seen),
                 "branches": late_grads_leave, "ssd": lambda seen: _reduce_scatter_middle(held["reduce0"], seen),
                 "w_in_grad": w_in_grad_leaves}
    dy, gw, GS[0] = _layer_bwd(0, dy, W[0], SP[0], saved[0], hooks=bwd_hooks)
    small_shapes = [whole[n].shape for n in SMALL]
    gs_flat = _pack([jnp.stack([GS[l][n].reshape(whole[n].shape[1:]) for l in range(DEPTH)]) for n in SMALL], F32)
    (gs_all,) = _all_gather("gather_small_grads", [gs_flat])
    first_token = _reduce_scatter_middle(held["reduce_first"], gs_all)
    red1 = _reduce_scatter_end(reduce1, dy)
    red0_late = _reduce_scatter_end(held["reduce0"], dy)
    grad_x = dy[None]
    kept_t = KEPT_TRANSPOSED
    grads_k = {n: jnp.stack([_shard_from_slot(n, red0_late[i]), _shard_from_slot(n, red1[i + 1])]) for i, n in enumerate(late)}

    def mine(n, a):
        if n in SMALL_SHARDED:
            cs = a.shape[-1] // N_DEV
            return lax.dynamic_slice_in_dim(a, dev * cs, cs, axis=a.ndim - 1)
        return a

    m_whole, v_whole = {}, {}
    for n in SMALL:
        reps = (1, 1, N_DEV) if n in SMALL_SHARDED else (1,) * m[n].ndim
        m_whole[n], v_whole[n] = jnp.tile(m[n], reps), jnp.tile(v[n], reps)
    outs = _adamw_small(gs_all, _pack([whole[n] for n in SMALL], F32) + first_token[0:1, 0:1], _pack([m_whole[n] for n in SMALL], F32),
                        _pack([v_whole[n] for n in SMALL], F32))
    g_s, d_s, m_s, v_s = [_unpack(o, small_shapes) for o in outs]
    delta, new_m, new_v = {}, {}, {}
    for i, n in enumerate(SMALL):
        grads[n], delta[n], new_m[n], new_v[n] = mine(n, g_s[i]), mine(n, d_s[i]), mine(n, m_s[i]), mine(n, v_s[i])
    for n in late + BIG[:1]:
        if n == "w_in":
            done = sum(new_v[k].reshape(-1)[0:1] for k in late + SMALL[:1])
            (first0,) = _reduce_scatter_end(held["reduce_first"], done)
            grads_k[n] = jnp.stack([first0, red1[0]])
        view = (lambda a: jnp.transpose(a, (0, 2, 1))) if n in kept_t else (lambda a: a)
        outs = _adamw(f"adamw_{n}", view(w[n]), grads_k[n], view(m[n]), view(v[n]), after=None if n == "w_in" else first_token)
        grads[n], delta[n], new_m[n], new_v[n] = view(grads_k[n]), view(outs[0]), view(outs[1]), view(outs[2])

    return (loss, grad_x, *[grads[n] for n in WEIGHTS], *[delta[n] for n in WEIGHTS], *[new_m[n] for n in WEIGHTS],
            *[new_v[n] for n in WEIGHTS])
```

```python
import functools
import math

import jax
import jax.numpy as jnp
from jax import lax
from jax.experimental import pallas as pl
from jax.experimental.pallas import tpu as pltpu

F32 = jnp.float32
MXU_DTYPE = jnp.bfloat16
HI = lax.Precision.HIGHEST

N_DEV = 8
D_MODEL = 1024
DEPTH = 2
SSD_HEADS, SSD_HEAD_DIM, SSD_INNER, SSD_GROUPS, SSD_STATE, SSD_CHUNK = 16, 64, 1024, 2, 128, 64
SSD_XBC = SSD_INNER + 2 * SSD_GROUPS * SSD_STATE
GDN_HEADS, GDN_HEAD_DIM, GDN_WIDTH, GDN_CHUNK = 4, 128, 512, 64
GLA_HEADS, GLA_KEY_DIM, GLA_VAL_DIM, GLA_K, GLA_V, GLA_RANK, GLA_CHUNK = 4, 64, 128, 256, 512, 16, 16
GLA_BLOCK = 128
GLA_NORMALIZER = 16.0
FFN_DIM = 2816
FFN_HALF = FFN_DIM // 8
FFN_HALF_PAD = 384
FFN_UP_PAD = 16 * FFN_HALF_PAD
FFN_PAD = FFN_UP_PAD // 2
ALPHA = (2 * DEPTH) ** 0.25
LN_EPS = 1e-5
RMS_EPS = 1e-6
ADAM_LR, ADAM_B1, ADAM_B2, ADAM_EPS, ADAM_WD, ADAM_STEP = 0.001, 0.9, 0.999, 1e-08, 0.01, 10
LANES = 128
NEG_BIG = -1e30
VMEM_LIMIT = 56 * 1024 * 1024

IN_SPLITS = (("z", 1024), ("xbc", 1536), ("dt", 16), ("gqkv", 1536), ("ga", 4), ("gb", 4), ("gg", 512),
             ("lqkv", 1024), ("lglr", 16), ("lr", 512), ("gates", 3072))
IN_DIM = sum(w for _, w in IN_SPLITS)
PAD_SEGS = (("gates", 0, 3072, (("gates", 0),)), ("xbc", 3072, 1536, (("xbc", 0),)),
            ("gqkv", 4608, 1536, (("gqkv", 0),)), ("z", 6144, 1024, (("z", 0),)),
            ("lqkv", 7168, 1024, (("lqkv", 0),)), ("gg", 8192, 512, (("gg", 0),)), ("lr", 8704, 512, (("lr", 0),)),
            ("dt", 9216, 128, (("dt", 0),)), ("gab", 9344, 128, (("ga", 0), ("gb", 4))), ("lglr", 9472, 128, (("lglr", 0),)))
IN_PAD = 9728
SEG = {name: (off, width) for name, off, width, _ in PAD_SEGS}

BIG = ("w_in", "w_br_ssd", "w_br_gdn", "w_br_gla", "w_out", "ffn_w_up", "ffn_w_down")
COL_SHARDED = ("w_in", "w_br_gdn", "w_br_gla", "ffn_w_up")
SMALL_SHARDED = ("ssd_conv_w", "gdn_conv_w", "gla_gate_w2", "ffn_conv_w")
WEIGHTS = ("w_in", "ssd_conv_w", "ssd_conv_b", "ssd_dt_bias", "ssd_a_log", "ssd_d", "ssd_norm_w", "gdn_conv_w",
           "gdn_a_log", "gdn_dt_bias", "gdn_norm_w", "gla_gate_w2", "gla_gate_b", "gla_norm_w", "w_br_ssd", "w_br_gdn",
           "w_br_gla", "w_out", "ln1_g", "ln1_b", "ffn_w_up", "ffn_conv_w", "ffn_conv_b", "ffn_w_down", "ln2_g", "ln2_b")
SMALL = tuple(n for n in WEIGHTS if n not in BIG)
FLAT_W = 512


def _cparams(sem=None):
    kw = dict(vmem_limit_bytes=VMEM_LIMIT)
    if sem is not None:
        kw["dimension_semantics"] = sem
    return pltpu.CompilerParams(**kw)


_DIMS = {"nn": (((1,), (0,)), ((), ())), "nt": (((1,), (1,)), ((), ())), "tn": (((0,), (0,)), ((), ()))}


def _dot(a, b, dims="nn"):
    if MXU_DTYPE == F32:
        return lax.dot_general(a.astype(F32), b.astype(F32), _DIMS[dims], precision=HI, preferred_element_type=F32)
    return lax.dot_general(a.astype(MXU_DTYPE), b.astype(MXU_DTYPE), _DIMS[dims], preferred_element_type=F32)


def _dot_hi(a, b, dims="nn"):
    return lax.dot_general(a.astype(F32), b.astype(F32), _DIMS[dims], precision=HI, preferred_element_type=F32)


def _iota2(shape, axis):
    return lax.broadcasted_iota(jnp.int32, shape, axis)


def _tril(n, strict=False):
    r, c = _iota2((n, n), 0), _iota2((n, n), 1)
    return (r > c) if strict else (r >= c)


def _raw_dot(a, b, dims):
    return lax.dot_general(a, b, _DIMS[dims], preferred_element_type=F32)


def _dot_x3(a, b, dims="nn"):
    if MXU_DTYPE == F32:
        return _dot_hi(a, b, dims)
    ah, bh = a.astype(jnp.bfloat16), b.astype(jnp.bfloat16)
    al, bl = (a - ah.astype(F32)).astype(jnp.bfloat16), (b - bh.astype(F32)).astype(jnp.bfloat16)
    return _raw_dot(ah, bh, dims) + (_raw_dot(ah, bl, dims) + _raw_dot(al, bh, dims))


def _exact_dot(mask, b, dims, mask_first):
    if MXU_DTYPE == F32:
        return _dot_hi(mask, b, dims) if mask_first else _dot_hi(b, mask, dims)
    m = mask.astype(jnp.bfloat16)
    b1 = b.astype(jnp.bfloat16)
    r1 = b - b1.astype(F32)
    b2 = r1.astype(jnp.bfloat16)
    b3 = (r1 - b2.astype(F32)).astype(jnp.bfloat16)
    if mask_first:
        return _raw_dot(m, b1, dims) + (_raw_dot(m, b2, dims) + _raw_dot(m, b3, dims))
    return _raw_dot(b1, m, dims) + (_raw_dot(b2, m, dims) + _raw_dot(b3, m, dims))


@jax.custom_vjp
def _mask_left(mask, b):
    return _exact_dot(mask, b, "nn", True)


_mask_left.defvjp(lambda mask, b: (_mask_left(mask, b), mask),
                  lambda mask, d: (jnp.zeros_like(mask), _exact_dot(mask, d, "tn", True)))


@jax.custom_vjp
def _mask_right(a, mask):
    return _exact_dot(mask, a, "nn", False)


_mask_right.defvjp(lambda a, mask: (_mask_right(a, mask), mask),
                   lambda mask, d: (_exact_dot(mask, d, "nt", False), jnp.zeros_like(mask)))


@jax.custom_vjp
def _unit_lower_inverses(mats):
    n = mats[0].shape[0]
    eye = (_iota2((n, n), 0) == _iota2((n, n), 1)).astype(F32)
    xs = [eye - a for a in mats]
    ps = list(mats)
    k = 2
    while k < n:
        ps = [_dot_x3(p, p) for p in ps]
        xs = [x + _dot_x3(x, p) for x, p in zip(xs, ps)]
        k *= 2
    return xs


def _unit_lower_inverses_fwd(mats):
    ts = _unit_lower_inverses(mats)
    return ts, ts


def _unit_lower_inverses_bwd(ts, dts):
    mids = [_dot_x3(t, d, "tn") for t, d in zip(ts, dts)]
    return ([-_dot_x3(m, t, "nt") for m, t in zip(mids, ts)],)


_unit_lower_inverses.defvjp(_unit_lower_inverses_fwd, _unit_lower_inverses_bwd)


def _ssd_chunk(xs_, ps_, s_t):
    xbc, dtraw, z = xs_
    dt_bias, a_log, d_skip, norm_w = ps_
    L = xbc.shape[0]
    H, P, N, G = SSD_HEADS, SSD_HEAD_DIM, SSD_STATE, SSD_GROUPS
    W = SSD_INNER // G
    xs = xbc[:, :SSD_INNER]
    bm = xbc[:, SSD_INNER:SSD_INNER + G * N]
    cm = xbc[:, SSD_INNER + G * N:]
    dt = jax.nn.softplus(dtraw[:, :H] + dt_bias)
    a = dt * (-jnp.exp(a_log))
    causal = _tril(L)
    a_cs = _mask_left(causal.astype(F32), a)
    expand = (_iota2((H, SSD_INNER), 1) // P == _iota2((H, SSD_INNER), 0)).astype(F32)
    wide = _mask_right(jnp.concatenate([a_cs, dt, jnp.broadcast_to(d_skip, (L, H))], axis=0), expand)
    a_cs_x, dt_x, d_x = wide[:L], wide[L:2 * L], wide[2 * L:]
    a_end_x = a_cs_x[L - 1:L, :]
    a_cs_t, dt_t = a_cs.T, dt.T
    cb = [_dot(cm[:, g * N:(g + 1) * N], bm[:, g * N:(g + 1) * N], "nt") for g in range(G)]
    cb2 = [jnp.concatenate([c, c], axis=1) for c in cb]
    lane2 = _iota2((L, 2 * L), 1)
    left = lane2 < L
    causal2 = _iota2((L, 2 * L), 0) >= jnp.where(left, lane2, lane2 - L)
    pairs = range(0, H, 2)
    col2 = [jnp.where(left, a_cs[:, h:h + 1], a_cs[:, h + 1:h + 2]) for h in pairs]
    row2 = [jnp.concatenate([a_cs_t[h:h + 1, :], a_cs_t[h + 1:h + 2, :]], axis=1) for h in pairs]
    dt2 = [jnp.concatenate([dt_t[h:h + 1, :], dt_t[h + 1:h + 2, :]], axis=1) for h in pairs]
    ws2 = [cb2[h // (H // G)] * (jnp.exp(jnp.where(causal2, col2[i] - row2[i], NEG_BIG)) * dt2[i]) for i, h in enumerate(pairs)]
    first = _iota2((L, 2 * P), 1) < P
    ys = []
    for i, h in enumerate(pairs):
        x2 = xs[:, h * P:(h + 2) * P]
        ys.append(_dot(ws2[i], jnp.concatenate([jnp.where(first, x2, 0.0), jnp.where(first, 0.0, x2)], axis=0)))
    y = jnp.concatenate(ys, axis=1)
    y_in = jnp.concatenate([_dot(cm[:, g * N:(g + 1) * N], s_t[:, g * W:(g + 1) * W]) for g in range(G)], axis=1)
    y = y + y_in * jnp.exp(a_cs_x) + d_x * xs
    xw = xs * (jnp.exp(a_end_x - a_cs_x) * dt_x)
    st = jnp.concatenate([_dot(bm[:, g * N:(g + 1) * N], xw[:, g * W:(g + 1) * W], "tn") for g in range(G)], axis=1)
    s_new = s_t * jnp.exp(a_end_x) + st
    yg = y * jax.nn.silu(z)
    outs = []
    for g in range(G):
        part = yg[:, g * W:(g + 1) * W]
        outs.append(part * lax.rsqrt(jnp.mean(part * part, axis=1, keepdims=True) + RMS_EPS))
    return (jnp.concatenate(outs, axis=1) * norm_w,), s_new


GDN_PREP_CHUNKS = 8


def _gdn_prep(xs_, ps_):
    qkv, ab = xs_
    a_log, dt_bias = ps_
    B = qkv.shape[0]
    H, D, L = GDN_HEADS, GDN_HEAD_DIM, GDN_CHUNK
    g_all = -jnp.exp(a_log) * jax.nn.softplus(ab + dt_bias)
    row, col = _iota2((B, B), 0), _iota2((B, B), 1)
    g_cs = _mask_left((((row // L) == (col // L)) & (row >= col)).astype(F32), g_all)
    g_cs_t = g_cs.T
    beta_all = jax.nn.sigmoid(ab)
    incl, strict = _tril(L), _tril(L, strict=True)
    qs, ks, vs = [], [], []
    for h in range(H):
        q = qkv[:, h * D:(h + 1) * D]
        k = qkv[:, GDN_WIDTH + h * D:GDN_WIDTH + (h + 1) * D]
        qs.append(q * lax.rsqrt(jnp.sum(q * q, axis=1, keepdims=True) + RMS_EPS) * (D ** -0.5))
        ks.append(k * lax.rsqrt(jnp.sum(k * k, axis=1, keepdims=True) + RMS_EPS))
        vs.append(qkv[:, 2 * GDN_WIDTH + h * D:2 * GDN_WIDTH + (h + 1) * D])
    pairs = [(c, h) for c in range(B // L) for h in range(H)]
    rows = {c: slice(c * L, (c + 1) * L) for c in range(B // L)}
    q_ = {(c, h): qs[h][rows[c]] for c, h in pairs}
    k_ = {(c, h): ks[h][rows[c]] for c, h in pairs}
    col_ = {(c, h): g_cs[rows[c], h:h + 1] for c, h in pairs}
    beta_ = {(c, h): beta_all[rows[c], H + h:H + h + 1] for c, h in pairs}
    gamma = {p: jnp.exp(jnp.where(incl, col_[p] - g_cs_t[p[1]:p[1] + 1, rows[p[0]]], NEG_BIG)) for p in pairs}
    kb = {p: k_[p] * beta_[p] for p in pairs}
    a_mat = [jnp.where(strict, _dot(kb[p], k_[p], "nt") * gamma[p], 0.0) for p in pairs]
    attn = {p: jnp.where(incl, _dot(q_[p], k_[p], "nt") * gamma[p], 0.0) for p in pairs}
    t_mat = dict(zip(pairs, _unit_lower_inverses(a_mat)))
    u = {p: _dot(t_mat[p], vs[p[1]][rows[p[0]]] * beta_[p]) for p in pairs}
    w = {p: _dot(t_mat[p], kb[p] * jnp.exp(col_[p])) for p in pairs}
    qd = {p: q_[p] * jnp.exp(col_[p]) for p in pairs}
    kd = {p: k_[p] * jnp.exp(col_[p][L - 1:L, :] - col_[p]) for p in pairs}

    def whole(parts):
        return jnp.concatenate([jnp.concatenate([parts[(c, h)] for h in range(H)], axis=1) for c in range(B // L)], axis=0)

    return (whole(u), whole(w), whole(qd), whole(kd), whole(attn), g_cs)


def _gdn_scan(xs_, ps_, s):
    u, w, qd, kd, attn, g_cs, gate = xs_
    (norm_w,) = ps_
    L = u.shape[0]
    H, D = GDN_HEADS, GDN_HEAD_DIM
    heads = range(H)
    lanes = [slice(h * D, (h + 1) * D) for h in heads]
    s_h = [s[lanes[h], :] for h in heads]
    v_new = [u[:, lanes[h]] - _dot(w[:, lanes[h]], s_h[h]) for h in heads]
    o = [_dot(qd[:, lanes[h]], s_h[h]) + _dot(attn[:, h * L:(h + 1) * L], v_new[h]) for h in heads]
    decay = [jnp.exp(g_cs[L - 1:L, h:h + 1]) for h in heads]
    s_new = [s_h[h] * decay[h] + _dot(kd[:, lanes[h]], v_new[h], "tn") for h in heads]
    o = [o[h] * lax.rsqrt(jnp.mean(o[h] * o[h], axis=1, keepdims=True) + RMS_EPS) * norm_w * jax.nn.silu(gate[:, lanes[h]])
         for h in heads]
    return (jnp.concatenate(o, axis=1),), jnp.concatenate(s_new, axis=0)


def _gdn_forward(tag, gqkv, h, sp):
    T = gqkv.shape[0]
    blk = min(GDN_PREP_CHUNKS * GDN_CHUNK, T)
    prep_in = [(gqkv, blk, 3 * GDN_WIDTH, 0), _seg_blk(h, "gab", blk)]
    prep_p = [_lane_pad(sp["gdn_a_log"]), _lane_pad(sp["gdn_dt_bias"])]
    mx = MXU_DTYPE
    prep = _chain_fwd(f"gdn_prep_{tag}", _gdn_prep, T // blk, prep_in, prep_p,
                      [(blk, GDN_WIDTH, F32), (blk, GDN_WIDTH, mx), (blk, GDN_WIDTH, mx), (blk, GDN_WIDTH, mx),
                       (blk, GDN_HEADS * GDN_CHUNK, mx), (blk, LANES, F32)])
    widths = [GDN_WIDTH] * 4 + [GDN_HEADS * GDN_CHUNK, LANES]
    scan_in = [(a, GDN_CHUNK, wd, 0) for a, wd in zip(prep, widths)] + [_seg_blk(h, "gg", GDN_CHUNK)]
    scan_p = [sp["gdn_norm_w"]]
    o, states = _chain_fwd(f"gdn_scan_{tag}", _gdn_scan, T // GDN_CHUNK, scan_in, scan_p, [(GDN_CHUNK, GDN_WIDTH, mx)],
                           (GDN_WIDTH, GDN_HEAD_DIM))
    return o, dict(prep_in=prep_in, prep_p=prep_p, scan_in=scan_in, scan_p=scan_p, states=states, widths=widths)


def _gdn_backward(tag, do, sv, dx_dtype):
    T = do.shape[0]
    blk = min(GDN_PREP_CHUNKS * GDN_CHUNK, T)
    dscan, (dnorm,) = _chain_bwd(f"gdn_scan_bwd_{tag}", _gdn_scan, T // GDN_CHUNK, sv["scan_in"], sv["scan_p"],
                                 [(do, GDN_CHUNK, GDN_WIDTH)], sprev=sv["states"], dx_dtypes=[F32] * 6 + [dx_dtype])
    douts = [(d, blk, wd) for d, wd in zip(dscan[:6], sv["widths"])]
    (dgqkv, dgab), (da_log, ddt_bias) = _chain_bwd(f"gdn_prep_bwd_{tag}", _gdn_prep, T // blk, sv["prep_in"], sv["prep_p"],
                                                   douts, dx_dtypes=[F32, dx_dtype])
    return dgqkv, dgab, dscan[6], da_log[:, :GDN_HEADS], ddt_bias[:, :GDN_HEADS], dnorm


def _gla_block(xs_, ps_, s_t):
    qkv, glr, r = xs_
    w2, gate_b, norm_w = ps_
    B = qkv.shape[0]
    H, K, V, C = GLA_HEADS, GLA_KEY_DIM, GLA_VAL_DIM, GLA_CHUNK
    q = qkv[:, :GLA_K] * (K ** -0.5)
    k = qkv[:, GLA_K:2 * GLA_K]
    v = qkv[:, 2 * GLA_K:]
    gk = jax.nn.log_sigmoid(_dot(glr, w2) + gate_b) / GLA_NORMALIZER
    row, col = _iota2((B, B), 0), _iota2((B, B), 1)
    same = (row // C) == (col // C)
    mask = same & (row >= col)
    b_cs = _mask_left(mask.astype(F32), gk)
    b_end = _mask_left((col == (row // C) * C + (C - 1)).astype(F32), b_cs)
    q_e = q * jnp.exp(b_cs)
    k_e = k * jnp.exp(-b_cs)
    k_d = k * jnp.exp(b_end - b_cs)
    intra = []
    for h in range(H):
        a_mat = jnp.where(mask, _dot(q_e[:, h * K:(h + 1) * K], k_e[:, h * K:(h + 1) * K], "nt"), 0.0)
        intra.append(_dot(a_mat, v[:, h * V:(h + 1) * V]))
    o = jnp.concatenate(intra, axis=1)
    chunks = [slice(j * C, (j + 1) * C) for j in range(B // C)]
    fresh = [jnp.concatenate([_dot(v[sl, h * V:(h + 1) * V], k_d[sl, h * K:(h + 1) * K], "tn") for h in range(H)], axis=1)
             for sl in chunks]
    entering = []
    for j, sl in enumerate(chunks):
        entering.append(s_t)
        s_t = s_t * jnp.exp(b_end[j * C:j * C + 1, :]) + fresh[j]
    inter = [jnp.concatenate([_dot(q_e[sl, h * K:(h + 1) * K], entering[j][:, h * K:(h + 1) * K], "nt") for h in range(H)],
                             axis=1) for j, sl in enumerate(chunks)]
    o = o + jnp.concatenate(inter, axis=0)
    outs = []
    for h in range(H):
        oh = o[:, h * V:(h + 1) * V]
        oh = oh * lax.rsqrt(jnp.mean(oh * oh, axis=1, keepdims=True) + RMS_EPS) * norm_w
        outs.append(oh * jax.nn.silu(r[:, h * V:(h + 1) * V]))
    return (jnp.concatenate(outs, axis=1),), s_t


def _merge_fn(xs_, ps_):
    gates, y_ssd, y_gdn, y_gla = xs_
    d = D_MODEL
    return (jax.nn.sigmoid(gates[:, :d]) * y_ssd + jax.nn.sigmoid(gates[:, d:2 * d]) * y_gdn
            + jax.nn.sigmoid(gates[:, 2 * d:]) * y_gla,)


def _ln_fn(xs_, ps_):
    x, r = xs_
    g, b = ps_
    t = ALPHA * x + r
    mu = jnp.mean(t, axis=1, keepdims=True)
    var = jnp.mean(jnp.square(t - mu), axis=1, keepdims=True)
    return ((t - mu) * lax.rsqrt(var + LN_EPS) * g + b,)


def _row_spec(rows, width, colblk, n, reverse):
    if reverse:
        return pl.BlockSpec((rows, width), lambda c: (n - 1 - c, colblk))
    return pl.BlockSpec((rows, width), lambda c: (c, colblk))


def _full_spec(shape):
    zeros = (0,) * len(shape)
    return pl.BlockSpec(shape, lambda c: zeros)


def _chain_fwd(name, fn, n, blocked, full, out_defs, state_shape=None):
    nb, nf, no = len(blocked), len(full), len(out_defs)

    def body(*refs):
        xs = [r[...].astype(F32) for r in refs[:nb]]
        ps = [r[...] for r in refs[nb:nb + nf]]
        o_refs = refs[nb + nf:nb + nf + no]
        if state_shape is None:
            outs = fn(xs, ps)
        else:
            sprev_ref, s_ref = refs[nb + nf + no:]

            @pl.when(pl.program_id(0) == 0)
            def _():
                s_ref[...] = jnp.zeros_like(s_ref)

            s = s_ref[...]
            sprev_ref[0] = s
            outs, s_new = fn(xs, ps, s)
            s_ref[...] = s_new
        for r, o in zip(o_refs, outs):
            r[...] = o.astype(r.dtype)

    in_specs = [_row_spec(rows, width, cb, n, False) for _, rows, width, cb in blocked]
    in_specs += [_full_spec(a.shape) for a in full]
    out_specs = [_row_spec(rows, width, 0, n, False) for rows, width, _ in out_defs]
    out_shape = [jax.ShapeDtypeStruct((n * rows, width), dt) for rows, width, dt in out_defs]
    scratch = []
    if state_shape is not None:
        out_specs.append(pl.BlockSpec((1,) + state_shape, lambda c: (c, 0, 0)))
        out_shape.append(jax.ShapeDtypeStruct((n,) + state_shape, F32))
        scratch.append(pltpu.VMEM(state_shape, F32))
    return pl.pallas_call(body, name=name, grid=(n,), in_specs=in_specs, out_specs=out_specs, out_shape=out_shape,
                          scratch_shapes=scratch, compiler_params=_cparams(("arbitrary",)))(
        *[a for a, _, _, _ in blocked], *full)


def _chain_bwd(name, fn, n, blocked, full, douts, sprev=None, dx_dtypes=None):
    nb, nf, nd = len(blocked), len(full), len(douts)
    has_state = sprev is not None
    dx_dtypes = dx_dtypes or [F32] * nb

    def body(*refs):
        pos = 0
        b_refs = refs[pos:pos + nb]; pos += nb
        f_refs = refs[pos:pos + nf]; pos += nf
        d_refs = refs[pos:pos + nd]; pos += nd
        if has_state:
            sprev_ref = refs[pos]; pos += 1
        dx_refs = refs[pos:pos + nb]; pos += nb
        dp_refs = refs[pos:pos + nf]; pos += nf
        if has_state:
            ds_ref = refs[pos]

        @pl.when(pl.program_id(0) == 0)
        def _():
            for r in dp_refs:
                r[...] = jnp.zeros_like(r)
            if has_state:
                ds_ref[...] = jnp.zeros_like(ds_ref)

        xs = [r[...].astype(F32) for r in b_refs]
        ps = [r[...] for r in f_refs]
        dys = tuple(r[...].astype(F32) for r in d_refs)
        if has_state:
            _, vjp = jax.vjp(fn, xs, ps, sprev_ref[0])
            dxs, dps, ds = vjp((dys, ds_ref[...]))
            ds_ref[...] = ds
        else:
            _, vjp = jax.vjp(fn, xs, ps)
            dxs, dps = vjp(dys)
        for r, d in zip(dx_refs, dxs):
            r[...] = d.astype(r.dtype)
        for r, d in zip(dp_refs, dps):
            r[...] += d

    in_specs = [_row_spec(rows, width, cb, n, True) for _, rows, width, cb in blocked]
    in_specs += [_full_spec(a.shape) for a in full]
    in_specs += [_row_spec(rows, width, 0, n, True) for _, rows, width in douts]
    args = [a for a, _, _, _ in blocked] + list(full) + [a for a, _, _ in douts]
    scratch = []
    if has_state:
        st_shape = sprev.shape[1:]
        in_specs.append(pl.BlockSpec((1,) + st_shape, lambda c: (n - 1 - c, 0, 0)))
        args.append(sprev)
        scratch.append(pltpu.VMEM(st_shape, F32))
    out_specs = [_row_spec(rows, width, 0, n, True) for _, rows, width, _ in blocked]
    out_specs += [_full_spec(a.shape) for a in full]
    out_shape = [jax.ShapeDtypeStruct((n * rows, width), dt) for (_, rows, width, _), dt in zip(blocked, dx_dtypes)]
    out_shape += [jax.ShapeDtypeStruct(a.shape, F32) for a in full]
    res = pl.pallas_call(body, name=name, grid=(n,), in_specs=in_specs, out_specs=out_specs, out_shape=out_shape,
                         scratch_shapes=scratch, compiler_params=_cparams(("arbitrary",)))(*args)
    return res[:nb], res[nb:]


def _tile(n, target, unit):
    if n <= target:
        return n
    best = None
    for t in range(unit, target + 1, unit):
        if n % t == 0:
            best = t
    assert best is not None, (n, target, unit)
    return best


def _mm(name, a, b, dims="nn", out_dtype=F32, tm=2048, tn=512, tk=2048, after=None):
    if dims == "nn":
        (M, K), (_, N) = a.shape, b.shape
    elif dims == "nt":
        (M, K), (N, _) = a.shape, b.shape
    else:
        (K, M), (_, N) = a.shape, b.shape
    tm, tn, tk = _tile(M, tm, LANES), _tile(N, tn, LANES), _tile(K, tk, LANES)
    nk = K // tk
    extra = [] if after is None else [after]

    def body(*refs):
        a_ref, b_ref = refs[:2]
        o_ref, acc_ref = refs[-2:]
        part = _dot(a_ref[...], b_ref[...], dims)
        if nk == 1:
            o_ref[...] = part.astype(o_ref.dtype)
            return
        k = pl.program_id(2)

        @pl.when(k == 0)
        def _():
            acc_ref[...] = part

        @pl.when((k > 0) & (k < nk - 1))
        def _():
            acc_ref[...] += part

        @pl.when(k == nk - 1)
        def _():
            o_ref[...] = (acc_ref[...] + part).astype(o_ref.dtype)

    if dims == "tn":
        a_spec = pl.BlockSpec((tk, tm), lambda j, i, k: (k, i))
    else:
        a_spec = pl.BlockSpec((tm, tk), lambda j, i, k: (i, k))
    if dims == "nt":
        b_spec = pl.BlockSpec((tn, tk), lambda j, i, k: (j, k))
    else:
        b_spec = pl.BlockSpec((tk, tn), lambda j, i, k: (k, j))
    return pl.pallas_call(
        body, name=name, grid=(N // tn, M // tm, nk), in_specs=[a_spec, b_spec] + [ANY] * len(extra),
        out_specs=pl.BlockSpec((tm, tn), lambda j, i, k: (i, j)), out_shape=jax.ShapeDtypeStruct((M, N), out_dtype),
        scratch_shapes=[pltpu.VMEM((tm, tn) if nk > 1 else (8, LANES), F32)],
        compiler_params=_cparams(("parallel", "parallel", "arbitrary")))(a, b, *extra)


CONV_CB = 256


def _shift_down(x, k):
    if k == 0:
        return x
    return jnp.where(_iota2(x.shape, 0) >= k, pltpu.roll(x, k, 0), 0.0)


def _shift_up(x, k):
    if k == 0:
        return x
    t = x.shape[0]
    return jnp.where(_iota2(x.shape, 0) < t - k, pltpu.roll(x, t - k, 0), 0.0)


def _conv_pre(x, w, b):
    kk = w.shape[0]
    pre = x * w[kk - 1:kk, :]
    for k in range(kk - 1):
        pre = pre + _shift_down(x, kk - 1 - k) * w[k:k + 1, :]
    return pre if b is None else pre + b


EDGE = 16


def _rotations(x, kk):
    return {s: pltpu.roll(x, s, 0) for s in range(1, kk)}


def _conv_pre_rot(x, w, b, rot=None):
    kk = w.shape[0]
    rot = _rotations(x, kk) if rot is None else rot
    pre = x * w[kk - 1:kk, :]
    for k in range(kk - 1):
        pre = pre + rot[kk - 1 - k] * w[k:k + 1, :]
    return pre if b is None else pre + b


def _conv_t_local(d, w):
    kk = w.shape[0]
    out = d * w[kk - 1:kk, :]
    for k in range(kk - 1):
        out = out + _shift_up(d, kk - 1 - k) * w[k:k + 1, :]
    return out


def _col_sum(a):
    return jnp.sum(a, axis=0, keepdims=True)


def _conv_bwd_rot(x_ref, x, rot, w, dpre, dpre_head, dx_ref, dw_ref, db_ref):
    T = dpre.shape[0]
    kk = w.shape[0]
    x_head, x_tail = x_ref[0:EDGE, :], x_ref[T - EDGE:T, :]
    wrong_head = dpre[0:EDGE]
    dx = dpre * w[kk - 1:kk, :]
    for k in range(kk - 1):
        dx = dx + pltpu.roll(dpre, T - (kk - 1 - k), 0) * w[k:k + 1, :]
    dx_ref[...] = dx.astype(dx_ref.dtype)
    top = jnp.concatenate([dpre_head, dpre[EDGE:2 * EDGE]], axis=0)
    dx_ref[0:EDGE, :] = _conv_t_local(top, w)[0:EDGE].astype(dx_ref.dtype)
    dx_ref[T - EDGE:T, :] = _conv_t_local(dpre[T - EDGE:T], w).astype(dx_ref.dtype)
    ends = jnp.concatenate([x_tail, x_head], axis=0)
    dw_ref[kk - 1:kk, :] = _col_sum(dpre * x) + _col_sum((dpre_head - wrong_head) * x_head)
    for k in range(kk - 1):
        s = kk - 1 - k
        rotated_head = pltpu.roll(ends, s, 0)[EDGE:2 * EDGE]
        dw_ref[k:k + 1, :] = (_col_sum(dpre * rot[s]) - _col_sum(wrong_head * rotated_head)
                              + _col_sum(dpre_head * _shift_down(x_head, s)))
    if db_ref is not None:
        db_ref[...] = _col_sum(dpre) + _col_sum(dpre_head - wrong_head)


def _dsilu(pre):
    sg = jax.nn.sigmoid(pre)
    return sg * (1.0 + pre * (1.0 - sg))


def _conv_silu_fwd(name, src, col0, w, b):
    T = src.shape[0]
    kk, C = w.shape
    cb = CONV_CB
    off = col0 // cb

    def body(*refs):
        x_ref, w_ref, o_ref = refs[0], refs[1], refs[-1]
        b_val = refs[2][...] if b is not None else None
        o_ref[...] = jax.nn.silu(_conv_pre_rot(x_ref[...], w_ref[...], b_val))
        o_ref[0:EDGE, :] = jax.nn.silu(_conv_pre(x_ref[0:EDGE, :], w_ref[...], b_val))

    in_specs = [pl.BlockSpec((T, cb), lambda j: (0, off + j)), pl.BlockSpec((kk, cb), lambda j: (0, j))]
    args = [src, w]
    if b is not None:
        in_specs.append(pl.BlockSpec((1, cb), lambda j: (0, j)))
        args.append(b)
    return pl.pallas_call(body, name=name, grid=(C // cb,), in_specs=in_specs,
                          out_specs=pl.BlockSpec((T, cb), lambda j: (0, j)), out_shape=jax.ShapeDtypeStruct((T, C), F32),
                          compiler_params=_cparams(("parallel",)))(*args)


def _conv_silu_bwd(name, src, col0, w, b, dy, dx_dtype):
    T = src.shape[0]
    kk, C = w.shape
    cb = CONV_CB
    off = col0 // cb
    has_b = b is not None

    def body(*refs):
        x_ref, w_ref = refs[:2]
        pos = 2
        b_val = None
        if has_b:
            b_val = refs[pos][...]; pos += 1
        dy_ref = refs[pos]; pos += 1
        dx_ref, dw_ref = refs[pos], refs[pos + 1]
        db_ref = refs[pos + 2] if has_b else None
        wv, x = w_ref[...], x_ref[...]
        rot = _rotations(x, kk)
        dpre = dy_ref[...] * _dsilu(_conv_pre_rot(x, wv, b_val, rot))
        dpre_head = dy_ref[0:EDGE, :] * _dsilu(_conv_pre(x_ref[0:EDGE, :], wv, b_val))
        _conv_bwd_rot(x_ref, x, rot, wv, dpre, dpre_head, dx_ref, dw_ref, db_ref)

    in_specs = [pl.BlockSpec((T, cb), lambda j: (0, off + j)), pl.BlockSpec((kk, cb), lambda j: (0, j))]
    args = [src, w]
    if has_b:
        in_specs.append(pl.BlockSpec((1, cb), lambda j: (0, j)))
        args.append(b)
    in_specs.append(pl.BlockSpec((T, cb), lambda j: (0, j)))
    args.append(dy)
    out_specs = [pl.BlockSpec((T, cb), lambda j: (0, j)), pl.BlockSpec((kk, cb), lambda j: (0, j))]
    out_shape = [jax.ShapeDtypeStruct((T, C), dx_dtype), jax.ShapeDtypeStruct((kk, C), F32)]
    if has_b:
        out_specs.append(pl.BlockSpec((1, cb), lambda j: (0, j)))
        out_shape.append(jax.ShapeDtypeStruct((1, C), F32))
    return pl.pallas_call(body, name=name, grid=(C // cb,), in_specs=in_specs, out_specs=out_specs, out_shape=out_shape,
                          compiler_params=_cparams(("parallel",)))(*args)


def _ffn_glu_fwd(name, up, w, b, out_dtype=F32):
    T = up.shape[0]
    kk = w.shape[0]
    cb = CONV_CB
    width = up.shape[1] // 2
    nblk = width // cb

    def body(g_ref, u_ref, wg_ref, wu_ref, bg_ref, bu_ref, o_ref):
        g = _conv_pre_rot(g_ref[...], wg_ref[...], bg_ref[...])
        u = _conv_pre_rot(u_ref[...], wu_ref[...], bu_ref[...])
        o_ref[...] = (jax.nn.silu(g) * u).astype(o_ref.dtype)
        g = _conv_pre(g_ref[0:EDGE, :], wg_ref[...], bg_ref[...])
        u = _conv_pre(u_ref[0:EDGE, :], wu_ref[...], bu_ref[...])
        o_ref[0:EDGE, :] = (jax.nn.silu(g) * u).astype(o_ref.dtype)

    lo, hi = (lambda j: (0, j)), (lambda j: (0, nblk + j))
    in_specs = [pl.BlockSpec((T, cb), lo), pl.BlockSpec((T, cb), hi), pl.BlockSpec((kk, cb), lo), pl.BlockSpec((kk, cb), hi),
                pl.BlockSpec((1, cb), lo), pl.BlockSpec((1, cb), hi)]
    return pl.pallas_call(body, name=name, grid=(nblk,), in_specs=in_specs, out_specs=pl.BlockSpec((T, cb), lo),
                          out_shape=jax.ShapeDtypeStruct((T, width), out_dtype),
                          compiler_params=_cparams(("parallel",)))(up, up, w, w, b, b)


def _ffn_glu_bwd(name, up, w, b, dact, dx_dtype):
    T = up.shape[0]
    kk = w.shape[0]
    cb = CONV_CB
    width = up.shape[1] // 2
    nblk = width // cb

    def body(g_ref, u_ref, wg_ref, wu_ref, bg_ref, bu_ref, d_ref, dg_ref, du_ref, dwg_ref, dwu_ref, dbg_ref, dbu_ref):
        wg, wu, xg, xu = wg_ref[...], wu_ref[...], g_ref[...], u_ref[...]
        rot_g, rot_u = _rotations(xg, kk), _rotations(xu, kk)
        g = _conv_pre_rot(xg, wg, bg_ref[...], rot_g)
        u = _conv_pre_rot(xu, wu, bu_ref[...], rot_u)
        d = d_ref[...].astype(F32)
        g_head = _conv_pre(g_ref[0:EDGE, :], wg, bg_ref[...])
        u_head = _conv_pre(u_ref[0:EDGE, :], wu, bu_ref[...])
        d_head = d_ref[0:EDGE, :].astype(F32)
        sg, sg_head = jax.nn.sigmoid(g), jax.nn.sigmoid(g_head)
        _conv_bwd_rot(g_ref, xg, rot_g, wg, d * u * (sg * (1.0 + g * (1.0 - sg))),
                      d_head * u_head * (sg_head * (1.0 + g_head * (1.0 - sg_head))), dg_ref, dwg_ref, dbg_ref)
        _conv_bwd_rot(u_ref, xu, rot_u, wu, d * (g * sg), d_head * (g_head * sg_head), du_ref, dwu_ref, dbu_ref)

    lo, hi = (lambda j: (0, j)), (lambda j: (0, nblk + j))
    in_specs = [pl.BlockSpec((T, cb), lo), pl.BlockSpec((T, cb), hi), pl.BlockSpec((kk, cb), lo), pl.BlockSpec((kk, cb), hi),
                pl.BlockSpec((1, cb), lo), pl.BlockSpec((1, cb), hi), pl.BlockSpec((T, cb), lo)]
    out_specs = [pl.BlockSpec((T, cb), lo)] * 2 + [pl.BlockSpec((kk, cb), lo)] * 2 + [pl.BlockSpec((1, cb), lo)] * 2
    out_shape = ([jax.ShapeDtypeStruct((T, width), dx_dtype)] * 2 + [jax.ShapeDtypeStruct((kk, width), F32)] * 2
                 + [jax.ShapeDtypeStruct((1, width), F32)] * 2)
    return pl.pallas_call(body, name=name, grid=(nblk,), in_specs=in_specs, out_specs=out_specs, out_shape=out_shape,
                          compiler_params=_cparams(("parallel",)))(up, up, w, w, b, b, dact)


def _loss_head(y, target):
    T, D = y.shape
    tb = _tile(T, 256, 8)

    def body(y_ref, t_ref, dy_ref, l_ref):
        @pl.when(pl.program_id(0) == 0)
        def _():
            l_ref[...] = jnp.zeros_like(l_ref)

        err = y_ref[...] - t_ref[...]
        dy_ref[...] = err * (1.0 / D)
        l_ref[...] += jnp.sum(err * err, axis=0, keepdims=True) * (0.5 / D)

    spec = pl.BlockSpec((tb, D), lambda i: (i, 0))
    return pl.pallas_call(body, name="loss_head", grid=(T // tb,), in_specs=[spec, spec],
                          out_specs=[spec, pl.BlockSpec((1, D), lambda i: (0, 0))],
                          out_shape=[jax.ShapeDtypeStruct((T, D), F32), jax.ShapeDtypeStruct((1, D), F32)],
                          compiler_params=_cparams(("arbitrary",)))(y, target)


def _adamw_math(w, g, m, v):
    m = ADAM_B1 * m + (1.0 - ADAM_B1) * g
    v = ADAM_B2 * v + (1.0 - ADAM_B2) * jnp.square(g)
    m_hat = m / (1.0 - ADAM_B1 ** ADAM_STEP)
    v_hat = v / (1.0 - ADAM_B2 ** ADAM_STEP)
    return -ADAM_LR * (m_hat / (jnp.sqrt(v_hat) + ADAM_EPS) + ADAM_WD * w), m, v


def _adamw(name, w, g, m, v, after=None):
    A, R, C = w.shape
    if C % LANES == 0:
        rb, cb = _slab(R, C)
    else:
        rb, cb = _tile(R, max(8, SLAB_BYTES // 2 // (C * 4) // 8 * 8), 8), C
    extra = [] if after is None else [after]

    def body(w_ref, g_ref, m_ref, v_ref, *rest):
        d_ref, mo_ref, vo_ref = rest[-3:]
        d, mn, vn = _adamw_math(w_ref[...], g_ref[...], m_ref[...], v_ref[...])
        d_ref[...] = d
        mo_ref[...] = mn
        vo_ref[...] = vn

    spec = pl.BlockSpec((1, rb, cb), lambda a, r, q: (a, r, q))
    return pl.pallas_call(body, name=name, grid=(A, R // rb, C // cb), in_specs=[spec] * 4 + [ANY] * len(extra),
                          out_specs=[spec] * 3, out_shape=[jax.ShapeDtypeStruct(w.shape, F32)] * 3,
                          compiler_params=_cparams(("parallel", "parallel", "parallel")))(w, g, m, v, *extra)


def _adamw_small(parts, w, m, v):
    def body(p_ref, w_ref, m_ref, v_ref, g_ref, d_ref, mo_ref, vo_ref):
        g = p_ref[0]
        for i in range(1, N_DEV):
            g = g + p_ref[i]
        d, mn, vn = _adamw_math(w_ref[...], g, m_ref[...], v_ref[...])
        g_ref[...] = g
        d_ref[...] = d
        mo_ref[...] = mn
        vo_ref[...] = vn

    return pl.pallas_call(body, name="adamw_small", out_shape=[jax.ShapeDtypeStruct(w.shape, F32)] * 4,
                          compiler_params=_cparams())(parts, w, m, v)


def _add_blocks(name, a, b, out_dtype=F32):
    n, R, W = a.shape
    rb = _tile(R, 512, 8)

    def body(a_ref, b_ref, o_ref):
        o_ref[...] = (a_ref[...].astype(F32) + b_ref[...].astype(F32)).astype(o_ref.dtype)

    spec = pl.BlockSpec((1, rb, W), lambda i, r: (i, r, 0))
    return pl.pallas_call(body, name=name, grid=(n, R // rb), in_specs=[spec, spec], out_specs=spec,
                          out_shape=jax.ShapeDtypeStruct(a.shape, out_dtype),
                          compiler_params=_cparams(("parallel", "parallel")))(a, b)


SLAB_BYTES = 5 << 19


def _slab(R, W):
    if R % 16 == 0:
        return _tile(R, max(16, SLAB_BYTES // (4 * W) // 16 * 16), 16), W
    assert W % LANES == 0, (R, W)
    return R, _tile(W, max(LANES, SLAB_BYTES // (4 * R) // LANES * LANES), LANES)


def _pair_add(name, g, other, c, chip):
    _, R, W = g.shape
    rb, cb = _slab(R, W)

    def body(s_ref, a_ref, b_ref, send_ref, own_ref):
        s = a_ref[0] + b_ref[0]
        send_ref[0] = s.astype(send_ref.dtype)

        @pl.when(pl.program_id(2) == s_ref[1])
        def _():
            own_ref[...] = s

    grid_spec = pltpu.PrefetchScalarGridSpec(
        num_scalar_prefetch=1, grid=(R // rb, W // cb, 4),
        in_specs=[pl.BlockSpec((1, rb, cb), lambda r, q, p, s_ref: (2 * p + s_ref[0], r, q)),
                  pl.BlockSpec((1, rb, cb), lambda r, q, p, s_ref: (p, r, q))],
        out_specs=[pl.BlockSpec((1, rb, cb), lambda r, q, p, s_ref: (p, r, q)),
                   pl.BlockSpec((rb, cb), lambda r, q, p, s_ref: (r, q))])
    scalars = jnp.stack([c, chip]).astype(jnp.int32)
    return pl.pallas_call(body, name=name, grid_spec=grid_spec,
                          out_shape=[jax.ShapeDtypeStruct((4, R, W), MXU_DTYPE), jax.ShapeDtypeStruct((R, W), F32)],
                          compiler_params=_cparams(("parallel", "parallel", "arbitrary")))(scalars, g, other)


def _sum4(name, own, parts):
    R, W = own.shape
    rb, cb = _slab(R, W)

    def body(o_ref, p_ref, out_ref):
        out_ref[...] = ((o_ref[...] + p_ref[0].astype(F32)) + p_ref[1].astype(F32)) + p_ref[2].astype(F32)

    return pl.pallas_call(body, name=name, grid=(R // rb, W // cb),
                          in_specs=[pl.BlockSpec((rb, cb), lambda r, q: (r, q)), pl.BlockSpec((3, rb, cb), lambda r, q: (0, r, q))],
                          out_specs=pl.BlockSpec((rb, cb), lambda r, q: (r, q)), out_shape=jax.ShapeDtypeStruct((R, W), F32),
                          compiler_params=_cparams(("parallel", "parallel")))(own, parts)


MESH = pl.DeviceIdType.MESH
ANY = pl.BlockSpec(memory_space=pl.ANY)


def _place():
    return lax.axis_index("x"), lax.axis_index("y"), lax.axis_index("c")


def _other_chips(x, y):
    return [(1 - x, y), (x, 1 - y), (1 - x, 1 - y)]


def _all_gather(name, blocks):
    n = len(blocks)

    def body(*refs):
        x_refs, out_refs = refs[:n], refs[n:2 * n]
        send_sems, recv_sems, local_sems = refs[2 * n:]
        x, y, c = _place()
        me, sibling = (x, y, c), (x, y, 1 - c)
        chips = _other_chips(x, y)

        def slot(a, px, py, pc):
            return out_refs[a].at[4 * px + 2 * py + pc]

        def copy(a, k, blk, to, src=None):
            return pltpu.make_async_remote_copy(src_ref=slot(a, *blk) if src is None else src, dst_ref=slot(a, *blk),
                                                send_sem=send_sems.at[a, k], recv_sem=recv_sems.at[a, k],
                                                device_id=to, device_id_type=MESH)

        mine = [pltpu.make_async_copy(x_refs[a], slot(a, *me), local_sems.at[a]) for a in range(n)]
        for cp in mine:
            cp.start()
        first = []
        for j, chip in enumerate(chips):
            first += [copy(a, 1 + j, me, (*chip, c), src=x_refs[a]) for a in range(n)]
        first += [copy(a, 0, me, sibling, src=x_refs[a]) for a in range(n)]
        for cp in first:
            cp.start()
        passed = []
        for j, chip in enumerate(chips):
            for a in range(n):
                copy(a, 1 + j, (*chip, c), me).wait_recv()
                passed.append(copy(a, 4 + j, (*chip, c), sibling))
                passed[-1].start()
        for a in range(n):
            copy(a, 0, sibling, me).wait_recv()
        for j, chip in enumerate(chips):
            for a in range(n):
                copy(a, 4 + j, (*chip, 1 - c), me).wait_recv()
        for cp in first + passed:
            cp.wait_send()
        for cp in mine:
            cp.wait()

    return pl.pallas_call(body, name=name, in_specs=[ANY] * n, out_specs=[ANY] * n,
                          out_shape=[jax.ShapeDtypeStruct((N_DEV,) + b.shape, b.dtype) for b in blocks],
                          scratch_shapes=[pltpu.SemaphoreType.DMA((n, 7)), pltpu.SemaphoreType.DMA((n, 7)),
                                          pltpu.SemaphoreType.DMA((n,))])(*blocks)


def _routes_to_sibling(x, y, c):
    return [(2 * p + (1 - c), p, (x, y, 1 - c)) for p in range(4)]


def _routes_to_chips(x, y, c):
    return [(2 * px + py, j, (px, py, c)) for j, (px, py) in enumerate(_other_chips(x, y))]


def _routes_block_to_chips(x, y, c):
    me = 4 * x + 2 * y + c
    return [(me, me, (px, py, c)) for px, py in _other_chips(x, y)]


def _routes_blocks_to_sibling(x, y, c):
    return [(4 * px + 2 * py + c, 4 * px + 2 * py + c, (x, y, 1 - c)) for px, py in [(x, y)] + _other_chips(x, y)]


def _route_copies(routes, src_refs, land_refs, send_sems, recv_sems):
    x, y, c = _place()
    copies = []
    for a, (src, land) in enumerate(zip(src_refs, land_refs)):
        plan = routes(x, y, c)
        for k, (s, d, target) in enumerate(plan):
            i = a * len(plan) + k
            copies.append(pltpu.make_async_remote_copy(src_ref=src.at[s], dst_ref=land.at[d], send_sem=send_sems.at[i],
                                                       recv_sem=recv_sems.at[i], device_id=target, device_id_type=MESH))
    return copies


def _exchange(name, routes, n_routes, srcs, land_slots):
    n = len(srcs)

    def body(*refs):
        copies = _route_copies(routes, refs[:n], refs[n:2 * n], refs[2 * n], refs[2 * n + 1])
        for cp in copies:
            cp.start()
        for cp in copies:
            cp.wait_recv()
        for cp in copies:
            cp.wait_send()

    return pl.pallas_call(body, name=name, in_specs=[ANY] * n, out_specs=[ANY] * n,
                          out_shape=[jax.ShapeDtypeStruct((land_slots,) + s.shape[1:], s.dtype) for s in srcs],
                          scratch_shapes=[pltpu.SemaphoreType.DMA((n * n_routes,)), pltpu.SemaphoreType.DMA((n * n_routes,))])(*srcs)


HBM_SPEC = pl.BlockSpec(memory_space=pltpu.HBM)
SEM_SPEC = pl.BlockSpec(memory_space=pltpu.SEMAPHORE)
DATAFLOW = pltpu.SideEffectType.DATAFLOW_SIDE_EFFECTING


def _exchange_start(name, routes, n_routes, srcs, lands, after=None):
    n = len(srcs)
    in_place = lands is None
    bufs = list(srcs) + ([] if in_place else list(lands))
    nb = len(bufs)
    extra = [] if after is None else [after]

    def body(*refs):
        src_refs = refs[:n]
        land_refs = src_refs if in_place else refs[n:nb]
        send_sems, recv_sems = refs[nb + len(extra)], refs[nb + len(extra) + 1]
        token = refs[-1]
        for cp in _route_copies(routes, src_refs, land_refs, send_sems, recv_sems):
            cp.start()
        token[...] = jnp.zeros_like(token)

    sems = [pltpu.SemaphoreType.DMA((n * n_routes,)), pltpu.SemaphoreType.DMA((n * n_routes,))]
    out = pl.pallas_call(
        body, name=name, in_specs=[HBM_SPEC] * nb + [ANY] * len(extra),
        out_shape=sems + [pltpu.HBM(b.shape, b.dtype) for b in bufs] + [jax.ShapeDtypeStruct((8, LANES), F32)],
        out_specs=[SEM_SPEC, SEM_SPEC] + [HBM_SPEC] * nb + [pl.BlockSpec(memory_space=pltpu.VMEM)],
        input_output_aliases={i: 2 + i for i in range(nb)},
        compiler_params=pltpu.CompilerParams(has_side_effects=DATAFLOW))(
        *[pltpu.with_memory_space_constraint(b, pltpu.HBM) for b in bufs], *extra)
    return (out[0], out[1], list(out[2:2 + nb])), out[-1]


def _exchange_wait(name, routes, n_routes, n, started, after):
    send_sems, recv_sems, bufs = started
    nb = len(bufs)
    in_place = nb == n

    def body(*refs):
        src_refs = refs[:n]
        land_refs = src_refs if in_place else refs[n:nb]
        for cp in _route_copies(routes, src_refs, land_refs, refs[nb], refs[nb + 1]):
            cp.wait_send()
            cp.wait_recv()

    out = pl.pallas_call(
        body, name=name, in_specs=[HBM_SPEC] * nb + [SEM_SPEC, SEM_SPEC, ANY],
        out_shape=[pltpu.HBM(b.shape, b.dtype) for b in bufs], out_specs=[HBM_SPEC] * nb,
        input_output_aliases={i: i for i in range(nb)},
        compiler_params=pltpu.CompilerParams(has_side_effects=DATAFLOW))(*bufs, send_sems, recv_sems, after)
    return list(out[:n]) if in_place else (list(out[:n]), list(out[n:]))


def _pair_sums(tag, gs, from_sibling):
    x, y, c = _place()
    return [_pair_add(f"rs_add_{tag}_{i}", g, o, c, 2 * x + y) for i, (g, o) in enumerate(zip(gs, from_sibling))]


def _reduce_scatter(tag, gs):
    sums = _pair_sums(tag, gs, _exchange(f"rs_swap_{tag}", _routes_to_sibling, 4, gs, 4))
    got = _exchange(f"rs_chips_{tag}", _routes_to_chips, 3, [s[0] for s in sums], 3)
    return [_sum4(f"rs_sum_{tag}_{i}", s[1], q) for i, (s, q) in enumerate(zip(sums, got))]


def _reduce_scatter_begin(tag, gs):
    lands = [lax.empty((4,) + g.shape[1:], g.dtype) for g in gs]
    swap, token = _exchange_start(f"rs_swap_{tag}_start", _routes_to_sibling, 4, gs, lands)
    return dict(tag=tag, n=len(gs), swap=swap), token


def _reduce_scatter_middle(state, after):
    tag, n = state["tag"], state["n"]
    gs, from_sibling = _exchange_wait(f"rs_swap_{tag}_wait", _routes_to_sibling, 4, n, state["swap"], after)
    state["sums"] = _pair_sums(tag, gs, from_sibling)
    partials = [s[0] for s in state["sums"]]
    lands = [lax.empty((3,) + p.shape[1:], p.dtype) for p in partials]
    state["chips"], token = _exchange_start(f"rs_chips_{tag}_start", _routes_to_chips, 3, partials, lands)
    return token


def _reduce_scatter_end(state, after):
    tag = state["tag"]
    _, got = _exchange_wait(f"rs_chips_{tag}_wait", _routes_to_chips, 3, state["n"], state["chips"], after)
    return [_sum4(f"rs_sum_{tag}_{i}", s[1], q) for i, (s, q) in enumerate(zip(state["sums"], got))]


def _all_gather_begin(tag, blocks, after):
    dev = 4 * lax.axis_index("x") + 2 * lax.axis_index("y") + lax.axis_index("c")
    zones = [lax.dynamic_update_slice_in_dim(lax.empty((N_DEV,) + b.shape, b.dtype), b[None], dev, axis=0) for b in blocks]
    chips, token = _exchange_start(f"gather_{tag}_chips_start", _routes_block_to_chips, 3, zones, None, after)
    return dict(tag=tag, n=len(blocks), chips=chips), token


def _all_gather_middle(state, after):
    tag, n = state["tag"], state["n"]
    zones = _exchange_wait(f"gather_{tag}_chips_wait", _routes_block_to_chips, 3, n, state["chips"], after)
    state["sibling"], token = _exchange_start(f"gather_{tag}_sibling_start", _routes_blocks_to_sibling, 4, zones, None)
    return token


def _all_gather_end(state, after):
    return _exchange_wait(f"gather_{state['tag']}_sibling_wait", _routes_blocks_to_sibling, 4, state["n"], state["sibling"], after)


PACK_UNIT = 8 * LANES


def _packed_size(shape):
    return -(-math.prod(shape) // PACK_UNIT) * PACK_UNIT


def _pack(arrays, dtype):
    parts = []
    for a in arrays:
        flat = a.reshape(-1).astype(dtype)
        parts.append(jnp.pad(flat, (0, _packed_size(a.shape) - flat.shape[0])))
    return jnp.concatenate(parts).reshape(-1, LANES)


def _unpack(flat, shapes, lead=()):
    out, row = [], 0
    for s in shapes:
        rows = _packed_size(s) // LANES
        piece = flat[..., row:row + rows, :].reshape(lead + (rows * LANES,))
        out.append(piece[..., :math.prod(s)].reshape(lead + tuple(s)))
        row += rows
    return out


def _ffn_pad_rows(a):
    n = a.shape[0] // FFN_HALF
    a = jnp.pad(a.reshape(n, FFN_HALF, a.shape[1]), ((0, 0), (0, FFN_HALF_PAD - FFN_HALF), (0, 0)))
    return a.reshape(n * FFN_HALF_PAD, a.shape[2])


def _ffn_unpad_rows(a):
    n = a.shape[0] // FFN_HALF_PAD
    return a.reshape(n, FFN_HALF_PAD, a.shape[1])[:, :FFN_HALF].reshape(n * FFN_HALF, a.shape[1])


def _ffn_pad_cols(a):
    n = a.shape[1] // FFN_HALF
    a = jnp.pad(a.reshape(a.shape[0], n, FFN_HALF), ((0, 0), (0, 0), (0, FFN_HALF_PAD - FFN_HALF)))
    return a.reshape(a.shape[0], n * FFN_HALF_PAD)


def _ffn_unpad_cols(a):
    n = a.shape[1] // FFN_HALF_PAD
    return a.reshape(a.shape[0], n, FFN_HALF_PAD)[:, :, :FFN_HALF].reshape(a.shape[0], n * FFN_HALF)


def _shard_to_send(name, shard):
    if name in ("w_in", "w_br_gdn", "w_br_gla"):
        shard = shard.T
    elif name == "ffn_w_up":
        shard = _ffn_pad_rows(shard.T)
    return shard.astype(MXU_DTYPE)


KEPT_TRANSPOSED = ("w_in", "w_br_gdn", "w_br_gla", "ffn_w_up")


def _whole_from_gathered(name, g):
    if name == "w_in":
        return _in_proj_from_shards(g)
    if name == "ffn_w_down":
        return jnp.pad(g, ((0, 0), (0, FFN_HALF_PAD - FFN_HALF), (0, 0))).reshape(FFN_PAD, g.shape[2])
    return g.reshape(N_DEV * g.shape[1], g.shape[2])


def _slots_from_whole(name, gw):
    if name == "w_in":
        return _in_proj_to_slots(gw)
    return gw.reshape(N_DEV, gw.shape[0] // N_DEV, gw.shape[1])


def _shard_from_slot(name, s):
    if name == "ffn_w_up":
        return _ffn_unpad_rows(s)
    if name == "ffn_w_down":
        return s[:FFN_HALF]
    return s


def _in_proj_pieces():
    starts, pos = {}, 0
    for n, width in IN_SPLITS:
        starts[n] = (pos, width)
        pos += width
    return [(starts[ref][0], off + lane, starts[ref][1]) for _, off, _, pieces in PAD_SEGS for ref, lane in pieces]


def _in_proj_moves():
    cs = IN_DIM // N_DEV
    moves = []
    for src, dst, n in sorted(_in_proj_pieces()):
        at = src
        while at < src + n:
            d = at // cs
            end = min(src + n, (d + 1) * cs)
            moves.append((d, at - d * cs, dst + at - src, end - at))
            at = end
    return moves


RELAYOUT_LANES = 128


def _in_proj_from_shards(g):
    _, cs, D = g.shape

    def body(g_ref, o_ref):
        o_ref[...] = jnp.zeros_like(o_ref)
        for d, i0, r0, n in _in_proj_moves():
            o_ref[r0:r0 + n, :] = g_ref[d, i0:i0 + n, :]

    cb = RELAYOUT_LANES
    return pl.pallas_call(body, name="w_in_rows_in", grid=(D // cb,),
                          in_specs=[pl.BlockSpec((N_DEV, cs, cb), lambda j: (0, 0, j))],
                          out_specs=pl.BlockSpec((IN_PAD, cb), lambda j: (0, j)),
                          out_shape=jax.ShapeDtypeStruct((IN_PAD, D), g.dtype), compiler_params=_cparams(("parallel",)))(g)


def _in_proj_to_slots(gw):
    D = gw.shape[1]
    cs = IN_DIM // N_DEV

    def body(x_ref, o_ref):
        for d, i0, r0, n in _in_proj_moves():
            o_ref[d, i0:i0 + n, :] = x_ref[r0:r0 + n, :]

    cb = RELAYOUT_LANES
    return pl.pallas_call(body, name="w_in_rows_out", grid=(D // cb,),
                          in_specs=[pl.BlockSpec((IN_PAD, cb), lambda j: (0, j))],
                          out_specs=pl.BlockSpec((N_DEV, cs, cb), lambda j: (0, 0, j)),
                          out_shape=jax.ShapeDtypeStruct((N_DEV, cs, D), gw.dtype), compiler_params=_cparams(("parallel",)))(gw)


def _pad_in_proj_rows(w):
    rows, at = [], 0
    for src, dst, n in sorted(_in_proj_pieces(), key=lambda p: p[1]):
        if dst > at:
            rows.append(jnp.zeros((dst - at, w.shape[1]), w.dtype))
        rows.append(w[src:src + n])
        at = dst + n
    rows.append(jnp.zeros((IN_PAD - at, w.shape[1]), w.dtype))
    return jnp.concatenate(rows, axis=0)


def _unpad_in_proj_rows(wp):
    return jnp.concatenate([wp[dst:dst + n] for _, dst, n in sorted(_in_proj_pieces())], axis=0)


def _lane_pad(a, width=LANES):
    return jnp.pad(a, ((0, 0), (0, width - a.shape[1])))


def _seg_blk(h, name, rows):
    off, width = SEG[name]
    return (h, rows, width, off // width)


def _ln_both(xs_, ps_):
    (y,) = _ln_fn(xs_, ps_)
    return (y, y)


def _behind(param, hooks, stage, *seen):
    if hooks is None or stage not in hooks:
        return param
    token = hooks[stage](*seen)
    return param if token is None else param + token[0:1, 0:1]


def _layer_fwd(l, x, x_mx, W, sp, hooks=None):
    T = x.shape[0]
    n64, ngla, ntok = T // SSD_CHUNK, T // GLA_BLOCK, T // 256
    h = _mm(f"in_proj_{l}", x_mx, W["w_in"], "nt")
    xbc = _conv_silu_fwd(f"ssd_conv_{l}", h, SEG["xbc"][0], sp["ssd_conv_w"], sp["ssd_conv_b"])
    gqkv = _conv_silu_fwd(f"gdn_conv_{l}", h, SEG["gqkv"][0], sp["gdn_conv_w"], None)

    ssd_in = [(xbc, SSD_CHUNK, SSD_XBC, 0), _seg_blk(h, "dt", SSD_CHUNK), _seg_blk(h, "z", SSD_CHUNK)]
    ssd_p = [sp["ssd_dt_bias"], sp["ssd_a_log"], sp["ssd_d"], sp["ssd_norm_w"]]
    o_ssd, ssd_states = _chain_fwd(f"ssd_fwd_{l}", _ssd_chunk, n64, ssd_in, ssd_p, [(SSD_CHUNK, SSD_INNER, MXU_DTYPE)],
                                   (SSD_STATE, SSD_INNER))
    o_gdn, gdn_saved = _gdn_forward(str(l), gqkv, h, dict(sp, gdn_a_log=_behind(sp["gdn_a_log"], hooks, "ssd", o_ssd)))
    gla_in = [_seg_blk(h, "lqkv", GLA_BLOCK), _seg_blk(h, "lglr", GLA_BLOCK), _seg_blk(h, "lr", GLA_BLOCK)]
    gla_p = [jnp.pad(sp["gla_gate_w2"], ((0, LANES - GLA_RANK), (0, 0))), sp["gla_gate_b"], sp["gla_norm_w"]]
    o_gla, gla_states = _chain_fwd(f"gla_fwd_{l}", _gla_block, ngla, gla_in, gla_p, [(GLA_BLOCK, GLA_V, MXU_DTYPE)],
                                   (GLA_VAL_DIM, GLA_K))
    ln1_p = [_behind(sp["ln1_g"], hooks, "mixed", o_gdn), sp["ln1_b"]]
    y_ssd = _mm(f"br_ssd_{l}", o_ssd, W["w_br_ssd"])
    y_gdn = _mm(f"br_gdn_{l}", o_gdn, W["w_br_gdn"], "nt")
    y_gla = _mm(f"br_gla_{l}", o_gla, W["w_br_gla"], "nt")
    merge_in = [_seg_blk(h, "gates", 256), (y_ssd, 256, D_MODEL, 0), (y_gdn, 256, D_MODEL, 0), (y_gla, 256, D_MODEL, 0)]
    (mix,) = _chain_fwd(f"merge_{l}", _merge_fn, ntok, merge_in, [], [(256, D_MODEL, MXU_DTYPE)])
    r1 = _mm(f"out_proj_{l}", mix, W["w_out"])
    both = [(256, D_MODEL, F32), (256, D_MODEL, MXU_DTYPE)]
    x1, x1_mx = _chain_fwd(f"ln1_{l}", _ln_both, ntok, [(x, 256, D_MODEL, 0), (r1, 256, D_MODEL, 0)], ln1_p, both)
    up = _mm(f"ffn_up_{l}", x1_mx, W["ffn_w_up"], "nt")
    act = _ffn_glu_fwd(f"ffn_glu_{l}", up, sp["ffn_conv_w_pad"], sp["ffn_conv_b_pad"], MXU_DTYPE)
    ln2_p = [_behind(sp["ln2_g"], hooks, "ffn_act", act), sp["ln2_b"]]
    r2 = _mm(f"ffn_down_{l}", act, W["ffn_w_down"], tn=1024, tk=1024)
    x2, x2_mx = _chain_fwd(f"ln2_{l}", _ln_both, ntok, [(x1, 256, D_MODEL, 0), (r2, 256, D_MODEL, 0)], ln2_p, both)
    saved = dict(x=x, x_mx=x_mx, h=h, xbc=xbc, gqkv=gqkv, ssd_in=ssd_in, ssd_p=ssd_p, ssd_states=ssd_states,
                 gdn=gdn_saved, gla_in=gla_in, gla_p=gla_p, gla_states=gla_states, o_ssd=o_ssd,
                 o_gdn=o_gdn, o_gla=o_gla, merge_in=merge_in, mix=mix, r1=r1, ln1_p=ln1_p, x1=x1, x1_mx=x1_mx, up=up, act=act,
                 r2=r2, ln2_p=ln2_p)
    return x2, x2_mx, saved


def _layer_bwd(l, dx2, W, sp, sv, hooks=None):
    T = dx2.shape[0]
    n64, ngla, ntok = T // SSD_CHUNK, T // GLA_BLOCK, T // 256
    bf = MXU_DTYPE
    gw, gs = {}, {}
    ln2_p = [_behind(sv["ln2_p"][0], hooks, "start"), sv["ln2_p"][1]]
    (dx1_a, dr2), (gs["ln2_g"], gs["ln2_b"]) = _chain_bwd(
        f"ln2_bwd_{l}", _ln_fn, ntok, [(sv["x1"], 256, D_MODEL, 0), (sv["r2"], 256, D_MODEL, 0)], ln2_p,
        [(dx2, 256, D_MODEL)], dx_dtypes=[F32, bf])
    gw["ffn_w_down"] = _mm(f"ffn_down_dw_{l}", sv["act"], dr2, "tn", tn=1024)
    dact = _mm(f"ffn_down_dx_{l}", dr2, W["ffn_w_down"], "nt")
    dg, du, dwg, dwu, dbg, dbu = _ffn_glu_bwd(f"ffn_glu_bwd_{l}", sv["up"], sp["ffn_conv_w_pad"], sp["ffn_conv_b_pad"], dact, bf)
    gs["ffn_conv_w"] = _ffn_unpad_cols(jnp.concatenate([dwg, dwu], axis=1))
    gs["ffn_conv_b"] = _ffn_unpad_cols(jnp.concatenate([dbg, dbu], axis=1))
    dup = jnp.concatenate([dg, du], axis=1)
    gw["ffn_w_up"] = _mm(f"ffn_up_dw_{l}", dup, sv["x1_mx"], "tn", tn=1024)
    dx1_b = _mm(f"ffn_up_dx_{l}", dup, W["ffn_w_up"], "nn", tn=1024, tk=1024)
    ln1_p = [_behind(sv["ln1_p"][0], hooks, "ffn", dx1_b), sv["ln1_p"][1]]
    (dx_a, dr1), (gs["ln1_g"], gs["ln1_b"]) = _chain_bwd(
        f"ln1_bwd_{l}", _ln_sum_fn, ntok, [(sv["x"], 256, D_MODEL, 0), (sv["r1"], 256, D_MODEL, 0)], ln1_p,
        [(dx1_a, 256, D_MODEL), (dx1_b, 256, D_MODEL)], dx_dtypes=[F32, bf])
    gw["w_out"] = _mm(f"out_proj_dw_{l}", sv["mix"], dr1, "tn")
    dmix = _mm(f"out_proj_dx_{l}", dr1, W["w_out"], "nt")
    (dgates, dy_ssd, dy_gdn, dy_gla), _ = _chain_bwd(f"merge_bwd_{l}", _merge_fn, ntok, sv["merge_in"], [],
                                                     [(dmix, 256, D_MODEL)], dx_dtypes=[bf, bf, bf, bf])
    gw["w_br_ssd"] = _mm(f"br_ssd_dw_{l}", sv["o_ssd"], dy_ssd, "tn")
    gw["w_br_gdn"] = _mm(f"br_gdn_dw_{l}", dy_gdn, sv["o_gdn"], "tn")
    gw["w_br_gla"] = _mm(f"br_gla_dw_{l}", dy_gla, sv["o_gla"], "tn")
    do_ssd = _mm(f"br_ssd_dx_{l}", dy_ssd, W["w_br_ssd"], "nt")
    do_gdn = _mm(f"br_gdn_dx_{l}", dy_gdn, W["w_br_gdn"], "nn")
    do_gla = _mm(f"br_gla_dx_{l}", dy_gla, W["w_br_gla"], "nn")

    ssd_p = [_behind(sv["ssd_p"][0], hooks, "branches", do_gla, gw)] + list(sv["ssd_p"][1:])
    (dxbc, ddt, dz), dps = _chain_bwd(f"ssd_bwd_{l}", _ssd_chunk, n64, sv["ssd_in"], ssd_p,
                                      [(do_ssd, SSD_CHUNK, SSD_INNER)], sprev=sv["ssd_states"], dx_dtypes=[F32, bf, bf])
    gs["ssd_dt_bias"], gs["ssd_a_log"], gs["ssd_d"], gs["ssd_norm_w"] = dps
    gdn_sv = dict(sv["gdn"], scan_p=[_behind(sv["gdn"]["scan_p"][0], hooks, "ssd", dz)])
    dgqkv, dgab, dgg, gs["gdn_a_log"], gs["gdn_dt_bias"], gs["gdn_norm_w"] = _gdn_backward(str(l), do_gdn, gdn_sv, bf)
    (dlqkv, dlglr, dlr), dps = _chain_bwd(f"gla_bwd_{l}", _gla_block, ngla, sv["gla_in"], sv["gla_p"],
                                          [(do_gla, GLA_BLOCK, GLA_V)], sprev=sv["gla_states"], dx_dtypes=[bf, bf, bf])
    gs["gla_gate_w2"], gs["gla_gate_b"], gs["gla_norm_w"] = dps[0][:GLA_RANK], dps[1], dps[2]
    dxbc_pre, gs["ssd_conv_w"], gs["ssd_conv_b"] = _conv_silu_bwd(
        f"ssd_conv_bwd_{l}", sv["h"], SEG["xbc"][0], sp["ssd_conv_w"], sp["ssd_conv_b"], dxbc, bf)
    dgqkv_pre, gs["gdn_conv_w"] = _conv_silu_bwd(f"gdn_conv_bwd_{l}", sv["h"], SEG["gqkv"][0], sp["gdn_conv_w"], None, dgqkv, bf)
    pieces = dict(gates=dgates, xbc=dxbc_pre, gqkv=dgqkv_pre, z=dz, lqkv=dlqkv, gg=dgg, lr=dlr, dt=ddt, gab=dgab, lglr=dlglr)
    cols = [pieces[name] for name, _, _, _ in PAD_SEGS]
    cols.append(jnp.zeros((T, IN_PAD - PAD_SEGS[-1][1] - PAD_SEGS[-1][2]), bf))
    dh = jnp.concatenate(cols, axis=1)
    gw["w_in"] = _mm(f"in_proj_dw_{l}", dh, sv["x_mx"], "tn", tn=1024)
    behind = hooks["w_in_grad"](gw) if hooks is not None and "w_in_grad" in hooks else None
    dx_b = _mm(f"in_proj_dx_{l}", dh, W["w_in"], "nn", tm=1024, tn=1024, tk=IN_PAD // 4, after=behind)
    dx = _add_blocks(f"dx_add_{l}", dx_a[None], dx_b[None])[0]
    return dx, gw, gs


def _ln_sum_fn(xs_, ps_):
    (y,) = _ln_fn(xs_, ps_)
    return (y, y)


def _small_2d(name, a):
    return a.reshape(1, -1) if a.ndim == 1 else a


def kernel(x, w_in, ssd_conv_w, ssd_conv_b, ssd_dt_bias, ssd_a_log, ssd_d, ssd_norm_w, gdn_conv_w, gdn_a_log, gdn_dt_bias, gdn_norm_w, gla_gate_w2, gla_gate_b, gla_norm_w, w_br_ssd, w_br_gdn, w_br_gla, w_out, ln1_g, ln1_b, ffn_w_up, ffn_conv_w, ffn_conv_b, ffn_w_down, ln2_g, ln2_b, loss_target, m_w_in, m_ssd_conv_w, m_ssd_conv_b, m_ssd_dt_bias, m_ssd_a_log, m_ssd_d, m_ssd_norm_w, m_gdn_conv_w, m_gdn_a_log, m_gdn_dt_bias, m_gdn_norm_w, m_gla_gate_w2, m_gla_gate_b, m_gla_norm_w, m_w_br_ssd, m_w_br_gdn, m_w_br_gla, m_w_out, m_ln1_g, m_ln1_b, m_ffn_w_up, m_ffn_conv_w, m_ffn_conv_b, m_ffn_w_down, m_ln2_g, m_ln2_b, v_w_in, v_ssd_conv_w, v_ssd_conv_b, v_ssd_dt_bias, v_ssd_a_log, v_ssd_d, v_ssd_norm_w, v_gdn_conv_w, v_gdn_a_log, v_gdn_dt_bias, v_gdn_norm_w, v_gla_gate_w2, v_gla_gate_b, v_gla_norm_w, v_w_br_ssd, v_w_br_gdn, v_w_br_gla, v_w_out, v_ln1_g, v_ln1_b, v_ffn_w_up, v_ffn_conv_w, v_ffn_conv_b, v_ffn_w_down, v_ln2_g, v_ln2_b):
    args = locals()
    w = {n: args[n] for n in WEIGHTS}
    m = {n: args["m_" + n] for n in WEIGHTS}
    v = {n: args["v_" + n] for n in WEIGHTS}
    dev = 4 * lax.axis_index("x") + 2 * lax.axis_index("y") + lax.axis_index("c")
    xl = x[0]
    tgt = loss_target[0]

    late = BIG[1:]

    def send(names, l):
        return [_shard_to_send(n, w[n][l]) for n in names]

    def whole_weights(names, got):
        return {n: _whole_from_gathered(n, g) for n, g in zip(names, got)}

    got0 = _all_gather("gather_first", send(BIG[:1], 0) + [w[n] for n in SMALL_SHARDED])
    gather0, token0 = _all_gather_begin("w_0", send(late, 0), got0[0])
    W = [whole_weights(BIG[:1], got0[:1]), None]
    whole = dict(w)
    for n, s in zip(SMALL_SHARDED, got0[1:]):
        whole[n] = jnp.transpose(s, (1, 2, 0, 3)).reshape(s.shape[1], s.shape[2], N_DEV * s.shape[3])
    SP = [{n: _small_2d(n, whole[n][l]) for n in SMALL} for l in range(DEPTH)]
    for sp in SP:
        sp["ffn_conv_w_pad"] = _ffn_pad_cols(sp["ffn_conv_w"])
        sp["ffn_conv_b_pad"] = _ffn_pad_cols(sp["ffn_conv_b"])

    held = {}

    def late_weights_cross(o_ssd):
        token = _all_gather_middle(gather0, o_ssd)
        held["gather1"], token1 = _all_gather_begin("w_1", send(BIG, 1), o_ssd)
        return token + token1

    def late_weights_arrive(mixed):
        W[0].update(whole_weights(late, _all_gather_end(gather0, mixed)))

    fwd_hooks = {"ssd": late_weights_cross, "mixed": late_weights_arrive,
                 "ffn_act": lambda act: _all_gather_middle(held["gather1"], act)}
    saved = [None] * DEPTH
    act, act_mx, saved[0] = _layer_fwd(0, xl, (xl + token0[0, 0]).astype(MXU_DTYPE), W[0], SP[0], hooks=fwd_hooks)
    W[1] = whole_weights(BIG, _all_gather_end(held["gather1"], act))
    act, act_mx, saved[1] = _layer_fwd(1, act, act_mx, W[1], SP[1])
    dy, loss_parts = _loss_head(act, tgt)
    loss = lax.psum(jnp.sum(loss_parts), ("x", "y", "c"))

    def slots_of(names, gw):
        return [_slots_from_whole(n, gw[n]) for n in names]

    grads = {}
    GS = [None] * DEPTH
    dy, gw, GS[1] = _layer_bwd(1, dy, W[1], SP[1], saved[1])
    reduce1, reduce1_token = _reduce_scatter_begin("1", slots_of(BIG, gw))

    def late_grads_leave(seen, gw0):
        held["reduce0"], token = _reduce_scatter_begin("0", slots_of(late, gw0))
        return token

    def w_in_grad_leaves(gw0):
        held["reduce_first"], token = _reduce_scatter_begin("first", slots_of(BIG[:1], gw0))
        return token

    bwd_hooks = {"start": lambda: reduce1_token, "ffn": lambda seen: _reduce_scatter_middle(reduce1, seen),
                 "branches": late_grads_leave, "ssd": lambda seen: _reduce_scatter_middle(held["reduce0"], seen),
                 "w_in_grad": w_in_grad_leaves}
    dy, gw, GS[0] = _layer_bwd(0, dy, W[0], SP[0], saved[0], hooks=bwd_hooks)
    small_shapes = [whole[n].shape for n in SMALL]
    gs_flat = _pack([jnp.stack([GS[l][n].reshape(whole[n].shape[1:]) for l in range(DEPTH)]) for n in SMALL], F32)
    (gs_all,) = _all_gather("gather_small_grads", [gs_flat])
    first_token = _reduce_scatter_middle(held["reduce_first"], gs_all)
    red1 = _reduce_scatter_end(reduce1, dy)
    red0_late = _reduce_scatter_end(held["reduce0"], dy)
    grad_x = dy[None]
    kept_t = KEPT_TRANSPOSED
    grads_k = {n: jnp.stack([_shard_from_slot(n, red0_late[i]), _shard_from_slot(n, red1[i + 1])]) for i, n in enumerate(late)}

    def mine(n, a):
        if n in SMALL_SHARDED:
            cs = a.shape[-1] // N_DEV
            return lax.dynamic_slice_in_dim(a, dev * cs, cs, axis=a.ndim - 1)
        return a

    m_whole, v_whole = {}, {}
    for n in SMALL:
        reps = (1, 1, N_DEV) if n in SMALL_SHARDED else (1,) * m[n].ndim
        m_whole[n], v_whole[n] = jnp.tile(m[n], reps), jnp.tile(v[n], reps)
    outs = _adamw_small(gs_all, _pack([whole[n] for n in SMALL], F32) + first_token[0:1, 0:1], _pack([m_whole[n] for n in SMALL], F32),
                        _pack([v_whole[n] for n in SMALL], F32))
    g_s, d_s, m_s, v_s = [_unpack(o, small_shapes) for o in outs]
    delta, new_m, new_v = {}, {}, {}
    for i, n in enumerate(SMALL):
        grads[n], delta[n], new_m[n], new_v[n] = mine(n, g_s[i]), mine(n, d_s[i]), mine(n, m_s[i]), mine(n, v_s[i])
    for n in late + BIG[:1]:
        if n == "w_in":
            done = sum(new_v[k].reshape(-1)[0:1] for k in late + SMALL[:1])
            (first0,) = _reduce_scatter_end(held["reduce_first"], done)
            grads_k[n] = jnp.stack([first0, red1[0]])
        view = (lambda a: jnp.transpose(a, (0, 2, 1))) if n in kept_t else (lambda a: a)
        outs = _adamw(f"adamw_{n}", view(w[n]), grads_k[n], view(m[n]), view(v[n]), after=None if n == "w_in" else first_token)
        grads[n], delta[n], new_m[n], new_v[n] = view(grads_k[n]), view(outs[0]), view(outs[1]), view(outs[2])

    return (loss, grad_x, *[grads[n] for n in WEIGHTS], *[delta[n] for n in WEIGHTS], *[new_m[n] for n in WEIGHTS],
            *[new_v[n] for n in WEIGHTS])
```

```python
import functools
import math

import jax
import jax.numpy as jnp
from jax import lax
from jax.experimental import pallas as pl
from jax.experimental.pallas import tpu as pltpu

F32 = jnp.float32
MXU_DTYPE = jnp.bfloat16
HI = lax.Precision.HIGHEST

N_DEV = 8
D_MODEL = 1024
DEPTH = 2
SSD_HEADS, SSD_HEAD_DIM, SSD_INNER, SSD_GROUPS, SSD_STATE, SSD_CHUNK = 16, 64, 1024, 2, 128, 64
SSD_XBC = SSD_INNER + 2 * SSD_GROUPS * SSD_STATE
GDN_HEADS, GDN_HEAD_DIM, GDN_WIDTH, GDN_CHUNK = 4, 128, 512, 64
GLA_HEADS, GLA_KEY_DIM, GLA_VAL_DIM, GLA_K, GLA_V, GLA_RANK, GLA_CHUNK = 4, 64, 128, 256, 512, 16, 16
GLA_BLOCK = 128
GLA_NORMALIZER = 16.0
FFN_DIM = 2816
FFN_HALF = FFN_DIM // 8
FFN_HALF_PAD = 384
FFN_UP_PAD = 16 * FFN_HALF_PAD
FFN_PAD = FFN_UP_PAD // 2
ALPHA = (2 * DEPTH) ** 0.25
LN_EPS = 1e-5
RMS_EPS = 1e-6
ADAM_LR, ADAM_B1, ADAM_B2, ADAM_EPS, ADAM_WD, ADAM_STEP = 0.001, 0.9, 0.999, 1e-08, 0.01, 10
LANES = 128
NEG_BIG = -1e30
VMEM_LIMIT = 56 * 1024 * 1024

IN_SPLITS = (("z", 1024), ("xbc", 1536), ("dt", 16), ("gqkv", 1536), ("ga", 4), ("gb", 4), ("gg", 512),
             ("lqkv", 1024), ("lglr", 16), ("lr", 512), ("gates", 3072))
IN_DIM = sum(w for _, w in IN_SPLITS)
PAD_SEGS = (("gates", 0, 3072, (("gates", 0),)), ("xbc", 3072, 1536, (("xbc", 0),)),
            ("gqkv", 4608, 1536, (("gqkv", 0),)), ("z", 6144, 1024, (("z", 0),)),
            ("lqkv", 7168, 1024, (("lqkv", 0),)), ("gg", 8192, 512, (("gg", 0),)), ("lr", 8704, 512, (("lr", 0),)),
            ("dt", 9216, 128, (("dt", 0),)), ("gab", 9344, 128, (("ga", 0), ("gb", 4))), ("lglr", 9472, 128, (("lglr", 0),)))
IN_PAD = 9728
SEG = {name: (off, width) for name, off, width, _ in PAD_SEGS}

BIG = ("w_in", "w_br_ssd", "w_br_gdn", "w_br_gla", "w_out", "ffn_w_up", "ffn_w_down")
COL_SHARDED = ("w_in", "w_br_gdn", "w_br_gla", "ffn_w_up")
SMALL_SHARDED = ("ssd_conv_w", "gdn_conv_w", "gla_gate_w2", "ffn_conv_w")
WEIGHTS = ("w_in", "ssd_conv_w", "ssd_conv_b", "ssd_dt_bias", "ssd_a_log", "ssd_d", "ssd_norm_w", "gdn_conv_w",
           "gdn_a_log", "gdn_dt_bias", "gdn_norm_w", "gla_gate_w2", "gla_gate_b", "gla_norm_w", "w_br_ssd", "w_br_gdn",
           "w_br_gla", "w_out", "ln1_g", "ln1_b", "ffn_w_up", "ffn_conv_w", "ffn_conv_b", "ffn_w_down", "ln2_g", "ln2_b")
SMALL = tuple(n for n in WEIGHTS if n not in BIG)
FLAT_W = 512


def _cparams(sem=None):
    kw = dict(vmem_limit_bytes=VMEM_LIMIT)
    if sem is not None:
        kw["dimension_semantics"] = sem
    return pltpu.CompilerParams(**kw)


_DIMS = {"nn": (((1,), (0,)), ((), ())), "nt": (((1,), (1,)), ((), ())), "tn": (((0,), (0,)), ((), ()))}


def _dot(a, b, dims="nn"):
    if MXU_DTYPE == F32:
        return lax.dot_general(a.astype(F32), b.astype(F32), _DIMS[dims], precision=HI, preferred_element_type=F32)
    return lax.dot_general(a.astype(MXU_DTYPE), b.astype(MXU_DTYPE), _DIMS[dims], preferred_element_type=F32)


def _dot_hi(a, b, dims="nn"):
    return lax.dot_general(a.astype(F32), b.astype(F32), _DIMS[dims], precision=HI, preferred_element_type=F32)


def _iota2(shape, axis):
    return lax.broadcasted_iota(jnp.int32, shape, axis)


def _tril(n, strict=False):
    r, c = _iota2((n, n), 0), _iota2((n, n), 1)
    return (r > c) if strict else (r >= c)


def _raw_dot(a, b, dims):
    return lax.dot_general(a, b, _DIMS[dims], preferred_element_type=F32)


def _dot_x3(a, b, dims="nn"):
    if MXU_DTYPE == F32:
        return _dot_hi(a, b, dims)
    ah, bh = a.astype(jnp.bfloat16), b.astype(jnp.bfloat16)
    al, bl = (a - ah.astype(F32)).astype(jnp.bfloat16), (b - bh.astype(F32)).astype(jnp.bfloat16)
    return _raw_dot(ah, bh, dims) + (_raw_dot(ah, bl, dims) + _raw_dot(al, bh, dims))


def _exact_dot(mask, b, dims, mask_first):
    if MXU_DTYPE == F32:
        return _dot_hi(mask, b, dims) if mask_first else _dot_hi(b, mask, dims)
    m = mask.astype(jnp.bfloat16)
    b1 = b.astype(jnp.bfloat16)
    r1 = b - b1.astype(F32)
    b2 = r1.astype(jnp.bfloat16)
    b3 = (r1 - b2.astype(F32)).astype(jnp.bfloat16)
    if mask_first:
        return _raw_dot(m, b1, dims) + (_raw_dot(m, b2, dims) + _raw_dot(m, b3, dims))
    return _raw_dot(b1, m, dims) + (_raw_dot(b2, m, dims) + _raw_dot(b3, m, dims))


@jax.custom_vjp
def _mask_left(mask, b):
    return _exact_dot(mask, b, "nn", True)


_mask_left.defvjp(lambda mask, b: (_mask_left(mask, b), mask),
                  lambda mask, d: (jnp.zeros_like(mask), _exact_dot(mask, d, "tn", True)))


@jax.custom_vjp
def _mask_right(a, mask):
    return _exact_dot(mask, a, "nn", False)


_mask_right.defvjp(lambda a, mask: (_mask_right(a, mask), mask),
                   lambda mask, d: (_exact_dot(mask, d, "nt", False), jnp.zeros_like(mask)))


@jax.custom_vjp
def _unit_lower_inverses(mats):
    n = mats[0].shape[0]
    eye = (_iota2((n, n), 0) == _iota2((n, n), 1)).astype(F32)
    xs = [eye - a for a in mats]
    ps = list(mats)
    k = 2
    while k < n:
        ps = [_dot_x3(p, p) for p in ps]
        xs = [x + _dot_x3(x, p) for x, p in zip(xs, ps)]
        k *= 2
    return xs


def _unit_lower_inverses_fwd(mats):
    ts = _unit_lower_inverses(mats)
    return ts, ts


def _unit_lower_inverses_bwd(ts, dts):
    mids = [_dot_x3(t, d, "tn") for t, d in zip(ts, dts)]
    return ([-_dot_x3(m, t, "nt") for m, t in zip(mids, ts)],)


_unit_lower_inverses.defvjp(_unit_lower_inverses_fwd, _unit_lower_inverses_bwd)


def _ssd_chunk(xs_, ps_, s_t):
    xbc, dtraw, z = xs_
    dt_bias, a_log, d_skip, norm_w = ps_
    L = xbc.shape[0]
    H, P, N, G = SSD_HEADS, SSD_HEAD_DIM, SSD_STATE, SSD_GROUPS
    W = SSD_INNER // G
    xs = xbc[:, :SSD_INNER]
    bm = xbc[:, SSD_INNER:SSD_INNER + G * N]
    cm = xbc[:, SSD_INNER + G * N:]
    dt = jax.nn.softplus(dtraw[:, :H] + dt_bias)
    a = dt * (-jnp.exp(a_log))
    causal = _tril(L)
    a_cs = _mask_left(causal.astype(F32), a)
    expand = (_iota2((H, SSD_INNER), 1) // P == _iota2((H, SSD_INNER), 0)).astype(F32)
    wide = _mask_right(jnp.concatenate([a_cs, dt, jnp.broadcast_to(d_skip, (L, H))], axis=0), expand)
    a_cs_x, dt_x, d_x = wide[:L], wide[L:2 * L], wide[2 * L:]
    a_end_x = a_cs_x[L - 1:L, :]
    a_cs_t, dt_t = a_cs.T, dt.T
    cb = [_dot(cm[:, g * N:(g + 1) * N], bm[:, g * N:(g + 1) * N], "nt") for g in range(G)]
    cb2 = [jnp.concatenate([c, c], axis=1) for c in cb]
    lane2 = _iota2((L, 2 * L), 1)
    left = lane2 < L
    causal2 = _iota2((L, 2 * L), 0) >= jnp.where(left, lane2, lane2 - L)
    pairs = range(0, H, 2)
    col2 = [jnp.where(left, a_cs[:, h:h + 1], a_cs[:, h + 1:h + 2]) for h in pairs]
    row2 = [jnp.concatenate([a_cs_t[h:h + 1, :], a_cs_t[h + 1:h + 2, :]], axis=1) for h in pairs]
    dt2 = [jnp.concatenate([dt_t[h:h + 1, :], dt_t[h + 1:h + 2, :]], axis=1) for h in pairs]
    ws2 = [cb2[h // (H // G)] * (jnp.exp(jnp.where(causal2, col2[i] - row2[i], NEG_BIG)) * dt2[i]) for i, h in enumerate(pairs)]
    first = _iota2((L, 2 * P), 1) < P
    ys = []
    for i, h in enumerate(pairs):
        x2 = xs[:, h * P:(h + 2) * P]
        ys.append(_dot(ws2[i], jnp.concatenate([jnp.where(first, x2, 0.0), jnp.where(first, 0.0, x2)], axis=0)))
    y = jnp.concatenate(ys, axis=1)
    y_in = jnp.concatenate([_dot(cm[:, g * N:(g + 1) * N], s_t[:, g * W:(g + 1) * W]) for g in range(G)], axis=1)
    y = y + y_in * jnp.exp(a_cs_x) + d_x * xs
    xw = xs * (jnp.exp(a_end_x - a_cs_x) * dt_x)
    st = jnp.concatenate([_dot(bm[:, g * N:(g + 1) * N], xw[:, g * W:(g + 1) * W], "tn") for g in range(G)], axis=1)
    s_new = s_t * jnp.exp(a_end_x) + st
    yg = y * jax.nn.silu(z)
    outs = []
    for g in range(G):
        part = yg[:, g * W:(g + 1) * W]
        outs.append(part * lax.rsqrt(jnp.mean(part * part, axis=1, keepdims=True) + RMS_EPS))
    return (jnp.concatenate(outs, axis=1) * norm_w,), s_new


GDN_PREP_CHUNKS = 8


def _gdn_prep(xs_, ps_):
    qkv, ab = xs_
    a_log, dt_bias = ps_
    B = qkv.shape[0]
    H, D, L = GDN_HEADS, GDN_HEAD_DIM, GDN_CHUNK
    g_all = -jnp.exp(a_log) * jax.nn.softplus(ab + dt_bias)
    row, col = _iota2((B, B), 0), _iota2((B, B), 1)
    g_cs = _mask_left((((row // L) == (col // L)) & (row >= col)).astype(F32), g_all)
    g_cs_t = g_cs.T
    beta_all = jax.nn.sigmoid(ab)
    incl, strict = _tril(L), _tril(L, strict=True)
    qs, ks, vs = [], [], []
    for h in range(H):
        q = qkv[:, h * D:(h + 1) * D]
        k = qkv[:, GDN_WIDTH + h * D:GDN_WIDTH + (h + 1) * D]
        qs.append(q * lax.rsqrt(jnp.sum(q * q, axis=1, keepdims=True) + RMS_EPS) * (D ** -0.5))
        ks.append(k * lax.rsqrt(jnp.sum(k * k, axis=1, keepdims=True) + RMS_EPS))
        vs.append(qkv[:, 2 * GDN_WIDTH + h * D:2 * GDN_WIDTH + (h + 1) * D])
    pairs = [(c, h) for c in range(B // L) for h in range(H)]
    rows = {c: slice(c * L, (c + 1) * L) for c in range(B // L)}
    q_ = {(c, h): qs[h][rows[c]] for c, h in pairs}
    k_ = {(c, h): ks[h][rows[c]] for c, h in pairs}
    col_ = {(c, h): g_cs[rows[c], h:h + 1] for c, h in pairs}
    beta_ = {(c, h): beta_all[rows[c], H + h:H + h + 1] for c, h in pairs}
    gamma = {p: jnp.exp(jnp.where(incl, col_[p] - g_cs_t[p[1]:p[1] + 1, rows[p[0]]], NEG_BIG)) for p in pairs}
    kb = {p: k_[p] * beta_[p] for p in pairs}
    a_mat = [jnp.where(strict, _dot(kb[p], k_[p], "nt") * gamma[p], 0.0) for p in pairs]
    attn = {p: jnp.where(incl, _dot(q_[p], k_[p], "nt") * gamma[p], 0.0) for p in pairs}
    t_mat = dict(zip(pairs, _unit_lower_inverses(a_mat)))
    u = {p: _dot(t_mat[p], vs[p[1]][rows[p[0]]] * beta_[p]) for p in pairs}
    w = {p: _dot(t_mat[p], kb[p] * jnp.exp(col_[p])) for p in pairs}
    qd = {p: q_[p] * jnp.exp(col_[p]) for p in pairs}
    kd = {p: k_[p] * jnp.exp(col_[p][L - 1:L, :] - col_[p]) for p in pairs}

    def whole(parts):
        return jnp.concatenate([jnp.concatenate([parts[(c, h)] for h in range(H)], axis=1) for c in range(B // L)], axis=0)

    return (whole(u), whole(w), whole(qd), whole(kd), whole(attn), g_cs)


def _gdn_scan(xs_, ps_, s):
    u, w, qd, kd, attn, g_cs, gate = xs_
    (norm_w,) = ps_
    L = u.shape[0]
    H, D = GDN_HEADS, GDN_HEAD_DIM
    heads = range(H)
    lanes = [slice(h * D, (h + 1) * D) for h in heads]
    s_h = [s[lanes[h], :] for h in heads]
    v_new = [u[:, lanes[h]] - _dot(w[:, lanes[h]], s_h[h]) for h in heads]
    o = [_dot(qd[:, lanes[h]], s_h[h]) + _dot(attn[:, h * L:(h + 1) * L], v_new[h]) for h in heads]
    decay = [jnp.exp(g_cs[L - 1:L, h:h + 1]) for h in heads]
    s_new = [s_h[h] * decay[h] + _dot(kd[:, lanes[h]], v_new[h], "tn") for h in heads]
    o = [o[h] * lax.rsqrt(jnp.mean(o[h] * o[h], axis=1, keepdims=True) + RMS_EPS) * norm_w * jax.nn.silu(gate[:, lanes[h]])
         for h in heads]
    return (jnp.concatenate(o, axis=1),), jnp.concatenate(s_new, axis=0)


def _gdn_forward(tag, gqkv, h, sp):
    T = gqkv.shape[0]
    blk = min(GDN_PREP_CHUNKS * GDN_CHUNK, T)
    prep_in = [(gqkv, blk, 3 * GDN_WIDTH, 0), _seg_blk(h, "gab", blk)]
    prep_p = [_lane_pad(sp["gdn_a_log"]), _lane_pad(sp["gdn_dt_bias"])]
    mx = MXU_DTYPE
    prep = _chain_fwd(f"gdn_prep_{tag}", _gdn_prep, T // blk, prep_in, prep_p,
                      [(blk, GDN_WIDTH, F32), (blk, GDN_WIDTH, mx), (blk, GDN_WIDTH, mx), (blk, GDN_WIDTH, mx),
                       (blk, GDN_HEADS * GDN_CHUNK, mx), (blk, LANES, F32)])
    widths = [GDN_WIDTH] * 4 + [GDN_HEADS * GDN_CHUNK, LANES]
    scan_in = [(a, GDN_CHUNK, wd, 0) for a, wd in zip(prep, widths)] + [_seg_blk(h, "gg", GDN_CHUNK)]
    scan_p = [sp["gdn_norm_w"]]
    o, states = _chain_fwd(f"gdn_scan_{tag}", _gdn_scan, T // GDN_CHUNK, scan_in, scan_p, [(GDN_CHUNK, GDN_WIDTH, mx)],
                           (GDN_WIDTH, GDN_HEAD_DIM))
    return o, dict(prep_in=prep_in, prep_p=prep_p, scan_in=scan_in, scan_p=scan_p, states=states, widths=widths)


def _gdn_backward(tag, do, sv, dx_dtype):
    T = do.shape[0]
    blk = min(GDN_PREP_CHUNKS * GDN_CHUNK, T)
    dscan, (dnorm,) = _chain_bwd(f"gdn_scan_bwd_{tag}", _gdn_scan, T // GDN_CHUNK, sv["scan_in"], sv["scan_p"],
                                 [(do, GDN_CHUNK, GDN_WIDTH)], sprev=sv["states"], dx_dtypes=[F32] * 6 + [dx_dtype])
    douts = [(d, blk, wd) for d, wd in zip(dscan[:6], sv["widths"])]
    (dgqkv, dgab), (da_log, ddt_bias) = _chain_bwd(f"gdn_prep_bwd_{tag}", _gdn_prep, T // blk, sv["prep_in"], sv["prep_p"],
                                                   douts, dx_dtypes=[F32, dx_dtype])
    return dgqkv, dgab, dscan[6], da_log[:, :GDN_HEADS], ddt_bias[:, :GDN_HEADS], dnorm


def _gla_block(xs_, ps_, s_t):
    qkv, glr, r = xs_
    w2, gate_b, norm_w = ps_
    B = qkv.shape[0]
    H, K, V, C = GLA_HEADS, GLA_KEY_DIM, GLA_VAL_DIM, GLA_CHUNK
    q = qkv[:, :GLA_K] * (K ** -0.5)
    k = qkv[:, GLA_K:2 * GLA_K]
    v = qkv[:, 2 * GLA_K:]
    gk = jax.nn.log_sigmoid(_dot(glr, w2) + gate_b) / GLA_NORMALIZER
    row, col = _iota2((B, B), 0), _iota2((B, B), 1)
    same = (row // C) == (col // C)
    mask = same & (row >= col)
    b_cs = _mask_left(mask.astype(F32), gk)
    b_end = _mask_left((col == (row // C) * C + (C - 1)).astype(F32), b_cs)
    q_e = q * jnp.exp(b_cs)
    k_e = k * jnp.exp(-b_cs)
    k_d = k * jnp.exp(b_end - b_cs)
    intra = []
    for h in range(H):
        a_mat = jnp.where(mask, _dot(q_e[:, h * K:(h + 1) * K], k_e[:, h * K:(h + 1) * K], "nt"), 0.0)
        intra.append(_dot(a_mat, v[:, h * V:(h + 1) * V]))
    o = jnp.concatenate(intra, axis=1)
    chunks = [slice(j * C, (j + 1) * C) for j in range(B // C)]
    fresh = [jnp.concatenate([_dot(v[sl, h * V:(h + 1) * V], k_d[sl, h * K:(h + 1) * K], "tn") for h in range(H)], axis=1)
             for sl in chunks]
    entering = []
    for j, sl in enumerate(chunks):
        entering.append(s_t)
        s_t = s_t * jnp.exp(b_end[j * C:j * C + 1, :]) + fresh[j]
    inter = [jnp.concatenate([_dot(q_e[sl, h * K:(h + 1) * K], entering[j][:, h * K:(h + 1) * K], "nt") for h in range(H)],
                             axis=1) for j, sl in enumerate(chunks)]
    o = o + jnp.concatenate(inter, axis=0)
    outs = []
    for h in range(H):
        oh = o[:, h * V:(h + 1) * V]
        oh = oh * lax.rsqrt(jnp.mean(oh * oh, axis=1, keepdims=True) + RMS_EPS) * norm_w
        outs.append(oh * jax.nn.silu(r[:, h * V:(h + 1) * V]))
    return (jnp.concatenate(outs, axis=1),), s_t


def _merge_fn(xs_, ps_):
    gates, y_ssd, y_gdn, y_gla = xs_
    d = D_MODEL
    return (jax.nn.sigmoid(gates[:, :d]) * y_ssd + jax.nn.sigmoid(gates[:, d:2 * d]) * y_gdn
            + jax.nn.sigmoid(gates[:, 2 * d:]) * y_gla,)


def _ln_fn(xs_, ps_):
    x, r = xs_
    g, b = ps_
    t = ALPHA * x + r
    mu = jnp.mean(t, axis=1, keepdims=True)
    var = jnp.mean(jnp.square(t - mu), axis=1, keepdims=True)
    return ((t - mu) * lax.rsqrt(var + LN_EPS) * g + b,)


def _row_spec(rows, width, colblk, n, reverse):
    if reverse:
        return pl.BlockSpec((rows, width), lambda c: (n - 1 - c, colblk))
    return pl.BlockSpec((rows, width), lambda c: (c, colblk))


def _full_spec(shape):
    zeros = (0,) * len(shape)
    return pl.BlockSpec(shape, lambda c: zeros)


def _chain_fwd(name, fn, n, blocked, full, out_defs, state_shape=None):
    nb, nf, no = len(blocked), len(full), len(out_defs)

    def body(*refs):
        xs = [r[...].astype(F32) for r in refs[:nb]]
        ps = [r[...] for r in refs[nb:nb + nf]]
        o_refs = refs[nb + nf:nb + nf + no]
        if state_shape is None:
            outs = fn(xs, ps)
        else:
            sprev_ref, s_ref = refs[nb + nf + no:]

            @pl.when(pl.program_id(0) == 0)
            def _():
                s_ref[...] = jnp.zeros_like(s_ref)

            s = s_ref[...]
            sprev_ref[0] = s
            outs, s_new = fn(xs, ps, s)
            s_ref[...] = s_new
        for r, o in zip(o_refs, outs):
            r[...] = o.astype(r.dtype)

    in_specs = [_row_spec(rows, width, cb, n, False) for _, rows, width, cb in blocked]
    in_specs += [_full_spec(a.shape) for a in full]
    out_specs = [_row_spec(rows, width, 0, n, False) for rows, width, _ in out_defs]
    out_shape = [jax.ShapeDtypeStruct((n * rows, width), dt) for rows, width, dt in out_defs]
    scratch = []
    if state_shape is not None:
        out_specs.append(pl.BlockSpec((1,) + state_shape, lambda c: (c, 0, 0)))
        out_shape.append(jax.ShapeDtypeStruct((n,) + state_shape, F32))
        scratch.append(pltpu.VMEM(state_shape, F32))
    return pl.pallas_call(body, name=name, grid=(n,), in_specs=in_specs, out_specs=out_specs, out_shape=out_shape,
                          scratch_shapes=scratch, compiler_params=_cparams(("arbitrary",)))(
        *[a for a, _, _, _ in blocked], *full)


def _chain_bwd(name, fn, n, blocked, full, douts, sprev=None, dx_dtypes=None):
    nb, nf, nd = len(blocked), len(full), len(douts)
    has_state = sprev is not None
    dx_dtypes = dx_dtypes or [F32] * nb

    def body(*refs):
        pos = 0
        b_refs = refs[pos:pos + nb]; pos += nb
        f_refs = refs[pos:pos + nf]; pos += nf
        d_refs = refs[pos:pos + nd]; pos += nd
        if has_state:
            sprev_ref = refs[pos]; pos += 1
        dx_refs = refs[pos:pos + nb]; pos += nb
        dp_refs = refs[pos:pos + nf]; pos += nf
        if has_state:
            ds_ref = refs[pos]

        @pl.when(pl.program_id(0) == 0)
        def _():
            for r in dp_refs:
                r[...] = jnp.zeros_like(r)
            if has_state:
                ds_ref[...] = jnp.zeros_like(ds_ref)

        xs = [r[...].astype(F32) for r in b_refs]
        ps = [r[...] for r in f_refs]
        dys = tuple(r[...].astype(F32) for r in d_refs)
        if has_state:
            _, vjp = jax.vjp(fn, xs, ps, sprev_ref[0])
            dxs, dps, ds = vjp((dys, ds_ref[...]))
            ds_ref[...] = ds
        else:
            _, vjp = jax.vjp(fn, xs, ps)
            dxs, dps = vjp(dys)
        for r, d in zip(dx_refs, dxs):
            r[...] = d.astype(r.dtype)
        for r, d in zip(dp_refs, dps):
            r[...] += d

    in_specs = [_row_spec(rows, width, cb, n, True) for _, rows, width, cb in blocked]
    in_specs += [_full_spec(a.shape) for a in full]
    in_specs += [_row_spec(rows, width, 0, n, True) for _, rows, width in douts]
    args = [a for a, _, _, _ in blocked] + list(full) + [a for a, _, _ in douts]
    scratch = []
    if has_state:
        st_shape = sprev.shape[1:]
        in_specs.append(pl.BlockSpec((1,) + st_shape, lambda c: (n - 1 - c, 0, 0)))
        args.append(sprev)
        scratch.append(pltpu.VMEM(st_shape, F32))
    out_specs = [_row_spec(rows, width, 0, n, True) for _, rows, width, _ in blocked]
    out_specs += [_full_spec(a.shape) for a in full]
    out_shape = [jax.ShapeDtypeStruct((n * rows, width), dt) for (_, rows, width, _), dt in zip(blocked, dx_dtypes)]
    out_shape += [jax.ShapeDtypeStruct(a.shape, F32) for a in full]
    res = pl.pallas_call(body, name=name, grid=(n,), in_specs=in_specs, out_specs=out_specs, out_shape=out_shape,
                         scratch_shapes=scratch, compiler_params=_cparams(("arbitrary",)))(*args)
    return res[:nb], res[nb:]


def _tile(n, target, unit):
    if n <= target:
        return n
    best = None
    for t in range(unit, target + 1, unit):
        if n % t == 0:
            best = t
    assert best is not None, (n, target, unit)
    return best


def _mm(name, a, b, dims="nn", out_dtype=F32, tm=2048, tn=512, tk=2048, after=None, plus=None):
    if dims == "nn":
        (M, K), (_, N) = a.shape, b.shape
    elif dims == "nt":
        (M, K), (N, _) = a.shape, b.shape
    else:
        (K, M), (_, N) = a.shape, b.shape
    tm, tn, tk = _tile(M, tm, LANES), _tile(N, tn, LANES), _tile(K, tk, LANES)
    nk = K // tk
    extra = [] if after is None else [after]
    addend = [] if plus is None else [plus]

    def body(*refs):
        a_ref, b_ref = refs[:2]
        o_ref, acc_ref = refs[-2:]
        part = _dot(a_ref[...], b_ref[...], dims)

        def write_out(total):
            if plus is not None:
                total = total + refs[2][...]
            o_ref[...] = total.astype(o_ref.dtype)

        if nk == 1:
            write_out(part)
            return
        k = pl.program_id(2)

        @pl.when(k == 0)
        def _():
            acc_ref[...] = part

        @pl.when((k > 0) & (k < nk - 1))
        def _():
            acc_ref[...] += part

        @pl.when(k == nk - 1)
        def _():
            write_out(acc_ref[...] + part)

    if dims == "tn":
        a_spec = pl.BlockSpec((tk, tm), lambda j, i, k: (k, i))
    else:
        a_spec = pl.BlockSpec((tm, tk), lambda j, i, k: (i, k))
    if dims == "nt":
        b_spec = pl.BlockSpec((tn, tk), lambda j, i, k: (j, k))
    else:
        b_spec = pl.BlockSpec((tk, tn), lambda j, i, k: (k, j))
    return pl.pallas_call(
        body, name=name, grid=(N // tn, M // tm, nk),
        in_specs=[a_spec, b_spec] + [pl.BlockSpec((tm, tn), lambda j, i, k: (i, j))] * len(addend) + [ANY] * len(extra),
        out_specs=pl.BlockSpec((tm, tn), lambda j, i, k: (i, j)), out_shape=jax.ShapeDtypeStruct((M, N), out_dtype),
        scratch_shapes=[pltpu.VMEM((tm, tn) if nk > 1 else (8, LANES), F32)],
        compiler_params=_cparams(("parallel", "parallel", "arbitrary")))(a, b, *addend, *extra)


CONV_CB = 256


def _shift_down(x, k):
    if k == 0:
        return x
    return jnp.where(_iota2(x.shape, 0) >= k, pltpu.roll(x, k, 0), 0.0)


def _shift_up(x, k):
    if k == 0:
        return x
    t = x.shape[0]
    return jnp.where(_iota2(x.shape, 0) < t - k, pltpu.roll(x, t - k, 0), 0.0)


def _conv_pre(x, w, b):
    kk = w.shape[0]
    pre = x * w[kk - 1:kk, :]
    for k in range(kk - 1):
        pre = pre + _shift_down(x, kk - 1 - k) * w[k:k + 1, :]
    return pre if b is None else pre + b


EDGE = 16


def _rotations(x, kk):
    return {s: pltpu.roll(x, s, 0) for s in range(1, kk)}


def _conv_pre_rot(x, w, b, rot=None):
    kk = w.shape[0]
    rot = _rotations(x, kk) if rot is None else rot
    pre = x * w[kk - 1:kk, :]
    for k in range(kk - 1):
        pre = pre + rot[kk - 1 - k] * w[k:k + 1, :]
    return pre if b is None else pre + b


def _conv_t_local(d, w):
    kk = w.shape[0]
    out = d * w[kk - 1:kk, :]
    for k in range(kk - 1):
        out = out + _shift_up(d, kk - 1 - k) * w[k:k + 1, :]
    return out


def _col_sum(a):
    return jnp.sum(a, axis=0, keepdims=True)


def _conv_bwd_rot(x_ref, x, rot, w, dpre, dpre_head, dx_ref, dw_ref, db_ref):
    T = dpre.shape[0]
    kk = w.shape[0]
    x_head, x_tail = x_ref[0:EDGE, :], x_ref[T - EDGE:T, :]
    wrong_head = dpre[0:EDGE]
    dx = dpre * w[kk - 1:kk, :]
    for k in range(kk - 1):
        dx = dx + pltpu.roll(dpre, T - (kk - 1 - k), 0) * w[k:k + 1, :]
    dx_ref[...] = dx.astype(dx_ref.dtype)
    top = jnp.concatenate([dpre_head, dpre[EDGE:2 * EDGE]], axis=0)
    dx_ref[0:EDGE, :] = _conv_t_local(top, w)[0:EDGE].astype(dx_ref.dtype)
    dx_ref[T - EDGE:T, :] = _conv_t_local(dpre[T - EDGE:T], w).astype(dx_ref.dtype)
    ends = jnp.concatenate([x_tail, x_head], axis=0)
    dw_ref[kk - 1:kk, :] = _col_sum(dpre * x) + _col_sum((dpre_head - wrong_head) * x_head)
    for k in range(kk - 1):
        s = kk - 1 - k
        rotated_head = pltpu.roll(ends, s, 0)[EDGE:2 * EDGE]
        dw_ref[k:k + 1, :] = (_col_sum(dpre * rot[s]) - _col_sum(wrong_head * rotated_head)
                              + _col_sum(dpre_head * _shift_down(x_head, s)))
    if db_ref is not None:
        db_ref[...] = _col_sum(dpre) + _col_sum(dpre_head - wrong_head)


def _dsilu(pre):
    sg = jax.nn.sigmoid(pre)
    return sg * (1.0 + pre * (1.0 - sg))


def _conv_silu_fwd(name, src, col0, w, b):
    T = src.shape[0]
    kk, C = w.shape
    cb = CONV_CB
    off = col0 // cb

    def body(*refs):
        x_ref, w_ref, o_ref = refs[0], refs[1], refs[-1]
        b_val = refs[2][...] if b is not None else None
        o_ref[...] = jax.nn.silu(_conv_pre_rot(x_ref[...], w_ref[...], b_val))
        o_ref[0:EDGE, :] = jax.nn.silu(_conv_pre(x_ref[0:EDGE, :], w_ref[...], b_val))

    in_specs = [pl.BlockSpec((T, cb), lambda j: (0, off + j)), pl.BlockSpec((kk, cb), lambda j: (0, j))]
    args = [src, w]
    if b is not None:
        in_specs.append(pl.BlockSpec((1, cb), lambda j: (0, j)))
        args.append(b)
    return pl.pallas_call(body, name=name, grid=(C // cb,), in_specs=in_specs,
                          out_specs=pl.BlockSpec((T, cb), lambda j: (0, j)), out_shape=jax.ShapeDtypeStruct((T, C), F32),
                          compiler_params=_cparams(("parallel",)))(*args)


def _conv_silu_bwd(name, src, col0, w, b, dy, dx_dtype):
    T = src.shape[0]
    kk, C = w.shape
    cb = CONV_CB
    off = col0 // cb
    has_b = b is not None

    def body(*refs):
        x_ref, w_ref = refs[:2]
        pos = 2
        b_val = None
        if has_b:
            b_val = refs[pos][...]; pos += 1
        dy_ref = refs[pos]; pos += 1
        dx_ref, dw_ref = refs[pos], refs[pos + 1]
        db_ref = refs[pos + 2] if has_b else None
        wv, x = w_ref[...], x_ref[...]
        rot = _rotations(x, kk)
        dpre = dy_ref[...] * _dsilu(_conv_pre_rot(x, wv, b_val, rot))
        dpre_head = dy_ref[0:EDGE, :] * _dsilu(_conv_pre(x_ref[0:EDGE, :], wv, b_val))
        _conv_bwd_rot(x_ref, x, rot, wv, dpre, dpre_head, dx_ref, dw_ref, db_ref)

    in_specs = [pl.BlockSpec((T, cb), lambda j: (0, off + j)), pl.BlockSpec((kk, cb), lambda j: (0, j))]
    args = [src, w]
    if has_b:
        in_specs.append(pl.BlockSpec((1, cb), lambda j: (0, j)))
        args.append(b)
    in_specs.append(pl.BlockSpec((T, cb), lambda j: (0, j)))
    args.append(dy)
    out_specs = [pl.BlockSpec((T, cb), lambda j: (0, j)), pl.BlockSpec((kk, cb), lambda j: (0, j))]
    out_shape = [jax.ShapeDtypeStruct((T, C), dx_dtype), jax.ShapeDtypeStruct((kk, C), F32)]
    if has_b:
        out_specs.append(pl.BlockSpec((1, cb), lambda j: (0, j)))
        out_shape.append(jax.ShapeDtypeStruct((1, C), F32))
    return pl.pallas_call(body, name=name, grid=(C // cb,), in_specs=in_specs, out_specs=out_specs, out_shape=out_shape,
                          compiler_params=_cparams(("parallel",)))(*args)


def _ffn_glu_fwd(name, up, w, b, out_dtype=F32):
    T = up.shape[0]
    kk = w.shape[0]
    cb = CONV_CB
    width = up.shape[1] // 2
    nblk = width // cb

    def body(g_ref, u_ref, wg_ref, wu_ref, bg_ref, bu_ref, o_ref):
        g = _conv_pre_rot(g_ref[...], wg_ref[...], bg_ref[...])
        u = _conv_pre_rot(u_ref[...], wu_ref[...], bu_ref[...])
        o_ref[...] = (jax.nn.silu(g) * u).astype(o_ref.dtype)
        g = _conv_pre(g_ref[0:EDGE, :], wg_ref[...], bg_ref[...])
        u = _conv_pre(u_ref[0:EDGE, :], wu_ref[...], bu_ref[...])
        o_ref[0:EDGE, :] = (jax.nn.silu(g) * u).astype(o_ref.dtype)

    lo, hi = (lambda j: (0, j)), (lambda j: (0, nblk + j))
    in_specs = [pl.BlockSpec((T, cb), lo), pl.BlockSpec((T, cb), hi), pl.BlockSpec((kk, cb), lo), pl.BlockSpec((kk, cb), hi),
                pl.BlockSpec((1, cb), lo), pl.BlockSpec((1, cb), hi)]
    return pl.pallas_call(body, name=name, grid=(nblk,), in_specs=in_specs, out_specs=pl.BlockSpec((T, cb), lo),
                          out_shape=jax.ShapeDtypeStruct((T, width), out_dtype),
                          compiler_params=_cparams(("parallel",)))(up, up, w, w, b, b)


def _ffn_glu_bwd(name, up, w, b, dact, dx_dtype):
    T = up.shape[0]
    kk = w.shape[0]
    cb = CONV_CB
    width = up.shape[1] // 2
    nblk = width // cb

    def body(g_ref, u_ref, wg_ref, wu_ref, bg_ref, bu_ref, d_ref, dg_ref, du_ref, dwg_ref, dwu_ref, dbg_ref, dbu_ref):
        wg, wu, xg, xu = wg_ref[...], wu_ref[...], g_ref[...], u_ref[...]
        rot_g, rot_u = _rotations(xg, kk), _rotations(xu, kk)
        g = _conv_pre_rot(xg, wg, bg_ref[...], rot_g)
        u = _conv_pre_rot(xu, wu, bu_ref[...], rot_u)
        d = d_ref[...].astype(F32)
        g_head = _conv_pre(g_ref[0:EDGE, :], wg, bg_ref[...])
        u_head = _conv_pre(u_ref[0:EDGE, :], wu, bu_ref[...])
        d_head = d_ref[0:EDGE, :].astype(F32)
        sg, sg_head = jax.nn.sigmoid(g), jax.nn.sigmoid(g_head)
        _conv_bwd_rot(g_ref, xg, rot_g, wg, d * u * (sg * (1.0 + g * (1.0 - sg))),
                      d_head * u_head * (sg_head * (1.0 + g_head * (1.0 - sg_head))), dg_ref, dwg_ref, dbg_ref)
        _conv_bwd_rot(u_ref, xu, rot_u, wu, d * (g * sg), d_head * (g_head * sg_head), du_ref, dwu_ref, dbu_ref)

    lo, hi = (lambda j: (0, j)), (lambda j: (0, nblk + j))
    in_specs = [pl.BlockSpec((T, cb), lo), pl.BlockSpec((T, cb), hi), pl.BlockSpec((kk, cb), lo), pl.BlockSpec((kk, cb), hi),
                pl.BlockSpec((1, cb), lo), pl.BlockSpec((1, cb), hi), pl.BlockSpec((T, cb), lo)]
    out_specs = [pl.BlockSpec((T, cb), lo)] * 2 + [pl.BlockSpec((kk, cb), lo)] * 2 + [pl.BlockSpec((1, cb), lo)] * 2
    out_shape = ([jax.ShapeDtypeStruct((T, width), dx_dtype)] * 2 + [jax.ShapeDtypeStruct((kk, width), F32)] * 2
                 + [jax.ShapeDtypeStruct((1, width), F32)] * 2)
    return pl.pallas_call(body, name=name, grid=(nblk,), in_specs=in_specs, out_specs=out_specs, out_shape=out_shape,
                          compiler_params=_cparams(("parallel",)))(up, up, w, w, b, b, dact)


def _loss_head(y, target):
    T, D = y.shape
    tb = _tile(T, 256, 8)

    def body(y_ref, t_ref, dy_ref, l_ref):
        @pl.when(pl.program_id(0) == 0)
        def _():
            l_ref[...] = jnp.zeros_like(l_ref)

        err = y_ref[...] - t_ref[...]
        dy_ref[...] = err * (1.0 / D)
        l_ref[...] += jnp.sum(err * err, axis=0, keepdims=True) * (0.5 / D)

    spec = pl.BlockSpec((tb, D), lambda i: (i, 0))
    return pl.pallas_call(body, name="loss_head", grid=(T // tb,), in_specs=[spec, spec],
                          out_specs=[spec, pl.BlockSpec((1, D), lambda i: (0, 0))],
                          out_shape=[jax.ShapeDtypeStruct((T, D), F32), jax.ShapeDtypeStruct((1, D), F32)],
                          compiler_params=_cparams(("arbitrary",)))(y, target)


def _adamw_math(w, g, m, v):
    m = ADAM_B1 * m + (1.0 - ADAM_B1) * g
    v = ADAM_B2 * v + (1.0 - ADAM_B2) * jnp.square(g)
    m_hat = m / (1.0 - ADAM_B1 ** ADAM_STEP)
    v_hat = v / (1.0 - ADAM_B2 ** ADAM_STEP)
    return -ADAM_LR * (m_hat / (jnp.sqrt(v_hat) + ADAM_EPS) + ADAM_WD * w), m, v


def _adamw(name, w, g, m, v, after=None):
    A, R, C = w.shape
    if C % LANES == 0:
        rb, cb = _slab(R, C)
    else:
        rb, cb = _tile(R, max(8, SLAB_BYTES // 2 // (C * 4) // 8 * 8), 8), C
    extra = [] if after is None else [after]

    def body(w_ref, g_ref, m_ref, v_ref, *rest):
        d_ref, mo_ref, vo_ref = rest[-3:]
        d, mn, vn = _adamw_math(w_ref[...], g_ref[...], m_ref[...], v_ref[...])
        d_ref[...] = d
        mo_ref[...] = mn
        vo_ref[...] = vn

    spec = pl.BlockSpec((1, rb, cb), lambda a, r, q: (a, r, q))
    return pl.pallas_call(body, name=name, grid=(A, R // rb, C // cb), in_specs=[spec] * 4 + [ANY] * len(extra),
                          out_specs=[spec] * 3, out_shape=[jax.ShapeDtypeStruct(w.shape, F32)] * 3,
                          compiler_params=_cparams(("parallel", "parallel", "parallel")))(w, g, m, v, *extra)


def _adamw_small(parts, w, m, v):
    def body(p_ref, w_ref, m_ref, v_ref, g_ref, d_ref, mo_ref, vo_ref):
        g = p_ref[0]
        for i in range(1, N_DEV):
            g = g + p_ref[i]
        d, mn, vn = _adamw_math(w_ref[...], g, m_ref[...], v_ref[...])
        g_ref[...] = g
        d_ref[...] = d
        mo_ref[...] = mn
        vo_ref[...] = vn

    return pl.pallas_call(body, name="adamw_small", out_shape=[jax.ShapeDtypeStruct(w.shape, F32)] * 4,
                          compiler_params=_cparams())(parts, w, m, v)


def _add_blocks(name, a, b, out_dtype=F32):
    n, R, W = a.shape
    rb = _tile(R, 512, 8)

    def body(a_ref, b_ref, o_ref):
        o_ref[...] = (a_ref[...].astype(F32) + b_ref[...].astype(F32)).astype(o_ref.dtype)

    spec = pl.BlockSpec((1, rb, W), lambda i, r: (i, r, 0))
    return pl.pallas_call(body, name=name, grid=(n, R // rb), in_specs=[spec, spec], out_specs=spec,
                          out_shape=jax.ShapeDtypeStruct(a.shape, out_dtype),
                          compiler_params=_cparams(("parallel", "parallel")))(a, b)


SLAB_BYTES = 5 << 19


def _slab(R, W):
    if R % 16 == 0:
        return _tile(R, max(16, SLAB_BYTES // (4 * W) // 16 * 16), 16), W
    assert W % LANES == 0, (R, W)
    return R, _tile(W, max(LANES, SLAB_BYTES // (4 * R) // LANES * LANES), LANES)


def _pair_add(name, g, other, c, chip):
    _, R, W = g.shape
    rb, cb = _slab(R, W)

    def body(s_ref, a_ref, b_ref, send_ref, own_ref):
        s = a_ref[0] + b_ref[0]
        send_ref[0] = s.astype(send_ref.dtype)

        @pl.when(pl.program_id(2) == s_ref[1])
        def _():
            own_ref[...] = s

    grid_spec = pltpu.PrefetchScalarGridSpec(
        num_scalar_prefetch=1, grid=(R // rb, W // cb, 4),
        in_specs=[pl.BlockSpec((1, rb, cb), lambda r, q, p, s_ref: (2 * p + s_ref[0], r, q)),
                  pl.BlockSpec((1, rb, cb), lambda r, q, p, s_ref: (p, r, q))],
        out_specs=[pl.BlockSpec((1, rb, cb), lambda r, q, p, s_ref: (p, r, q)),
                   pl.BlockSpec((rb, cb), lambda r, q, p, s_ref: (r, q))])
    scalars = jnp.stack([c, chip]).astype(jnp.int32)
    return pl.pallas_call(body, name=name, grid_spec=grid_spec,
                          out_shape=[jax.ShapeDtypeStruct((4, R, W), MXU_DTYPE), jax.ShapeDtypeStruct((R, W), F32)],
                          compiler_params=_cparams(("parallel", "parallel", "arbitrary")))(scalars, g, other)


def _sum4(name, own, parts):
    R, W = own.shape
    rb, cb = _slab(R, W)

    def body(o_ref, p_ref, out_ref):
        out_ref[...] = ((o_ref[...] + p_ref[0].astype(F32)) + p_ref[1].astype(F32)) + p_ref[2].astype(F32)

    return pl.pallas_call(body, name=name, grid=(R // rb, W // cb),
                          in_specs=[pl.BlockSpec((rb, cb), lambda r, q: (r, q)), pl.BlockSpec((3, rb, cb), lambda r, q: (0, r, q))],
                          out_specs=pl.BlockSpec((rb, cb), lambda r, q: (r, q)), out_shape=jax.ShapeDtypeStruct((R, W), F32),
                          compiler_params=_cparams(("parallel", "parallel")))(own, parts)


MESH = pl.DeviceIdType.MESH
ANY = pl.BlockSpec(memory_space=pl.ANY)


def _place():
    return lax.axis_index("x"), lax.axis_index("y"), lax.axis_index("c")


def _other_chips(x, y):
    return [(1 - x, y), (x, 1 - y), (1 - x, 1 - y)]


def _all_gather(name, blocks):
    n = len(blocks)

    def body(*refs):
        x_refs, out_refs = refs[:n], refs[n:2 * n]
        send_sems, recv_sems, local_sems = refs[2 * n:]
        x, y, c = _place()
        me, sibling = (x, y, c), (x, y, 1 - c)
        chips = _other_chips(x, y)

        def slot(a, px, py, pc):
            return out_refs[a].at[4 * px + 2 * py + pc]

        def copy(a, k, blk, to, src=None):
            return pltpu.make_async_remote_copy(src_ref=slot(a, *blk) if src is None else src, dst_ref=slot(a, *blk),
                                                send_sem=send_sems.at[a, k], recv_sem=recv_sems.at[a, k],
                                                device_id=to, device_id_type=MESH)

        mine = [pltpu.make_async_copy(x_refs[a], slot(a, *me), local_sems.at[a]) for a in range(n)]
        for cp in mine:
            cp.start()
        first = []
        for j, chip in enumerate(chips):
            first += [copy(a, 1 + j, me, (*chip, c), src=x_refs[a]) for a in range(n)]
        first += [copy(a, 0, me, sibling, src=x_refs[a]) for a in range(n)]
        for cp in first:
            cp.start()
        passed = []
        for j, chip in enumerate(chips):
            for a in range(n):
                copy(a, 1 + j, (*chip, c), me).wait_recv()
                passed.append(copy(a, 4 + j, (*chip, c), sibling))
                passed[-1].start()
        for a in range(n):
            copy(a, 0, sibling, me).wait_recv()
        for j, chip in enumerate(chips):
            for a in range(n):
                copy(a, 4 + j, (*chip, 1 - c), me).wait_recv()
        for cp in first + passed:
            cp.wait_send()
        for cp in mine:
            cp.wait()

    return pl.pallas_call(body, name=name, in_specs=[ANY] * n, out_specs=[ANY] * n,
                          out_shape=[jax.ShapeDtypeStruct((N_DEV,) + b.shape, b.dtype) for b in blocks],
                          scratch_shapes=[pltpu.SemaphoreType.DMA((n, 7)), pltpu.SemaphoreType.DMA((n, 7)),
                                          pltpu.SemaphoreType.DMA((n,))])(*blocks)


def _routes_to_sibling(x, y, c):
    return [(2 * p + (1 - c), p, (x, y, 1 - c)) for p in range(4)]


def _routes_to_chips(x, y, c):
    return [(2 * px + py, j, (px, py, c)) for j, (px, py) in enumerate(_other_chips(x, y))]


def _routes_block_to_chips(x, y, c):
    me = 4 * x + 2 * y + c
    return [(me, me, (px, py, c)) for px, py in _other_chips(x, y)]


def _routes_blocks_to_sibling(x, y, c):
    return [(4 * px + 2 * py + c, 4 * px + 2 * py + c, (x, y, 1 - c)) for px, py in [(x, y)] + _other_chips(x, y)]


def _route_copies(routes, src_refs, land_refs, send_sems, recv_sems):
    x, y, c = _place()
    copies = []
    for a, (src, land) in enumerate(zip(src_refs, land_refs)):
        plan = routes(x, y, c)
        for k, (s, d, target) in enumerate(plan):
            i = a * len(plan) + k
            copies.append(pltpu.make_async_remote_copy(src_ref=src.at[s], dst_ref=land.at[d], send_sem=send_sems.at[i],
                                                       recv_sem=recv_sems.at[i], device_id=target, device_id_type=MESH))
    return copies


def _exchange(name, routes, n_routes, srcs, land_slots):
    n = len(srcs)

    def body(*refs):
        copies = _route_copies(routes, refs[:n], refs[n:2 * n], refs[2 * n], refs[2 * n + 1])
        for cp in copies:
            cp.start()
        for cp in copies:
            cp.wait_recv()
        for cp in copies:
            cp.wait_send()

    return pl.pallas_call(body, name=name, in_specs=[ANY] * n, out_specs=[ANY] * n,
                          out_shape=[jax.ShapeDtypeStruct((land_slots,) + s.shape[1:], s.dtype) for s in srcs],
                          scratch_shapes=[pltpu.SemaphoreType.DMA((n * n_routes,)), pltpu.SemaphoreType.DMA((n * n_routes,))])(*srcs)


HBM_SPEC = pl.BlockSpec(memory_space=pltpu.HBM)
SEM_SPEC = pl.BlockSpec(memory_space=pltpu.SEMAPHORE)
DATAFLOW = pltpu.SideEffectType.DATAFLOW_SIDE_EFFECTING


def _exchange_start(name, routes, n_routes, srcs, lands, after=None):
    n = len(srcs)
    in_place = lands is None
    bufs = list(srcs) + ([] if in_place else list(lands))
    nb = len(bufs)
    extra = [] if after is None else [after]

    def body(*refs):
        src_refs = refs[:n]
        land_refs = src_refs if in_place else refs[n:nb]
        send_sems, recv_sems = refs[nb + len(extra)], refs[nb + len(extra) + 1]
        token = refs[-1]
        for cp in _route_copies(routes, src_refs, land_refs, send_sems, recv_sems):
            cp.start()
        token[...] = jnp.zeros_like(token)

    sems = [pltpu.SemaphoreType.DMA((n * n_routes,)), pltpu.SemaphoreType.DMA((n * n_routes,))]
    out = pl.pallas_call(
        body, name=name, in_specs=[HBM_SPEC] * nb + [ANY] * len(extra),
        out_shape=sems + [pltpu.HBM(b.shape, b.dtype) for b in bufs] + [jax.ShapeDtypeStruct((8, LANES), F32)],
        out_specs=[SEM_SPEC, SEM_SPEC] + [HBM_SPEC] * nb + [pl.BlockSpec(memory_space=pltpu.VMEM)],
        input_output_aliases={i: 2 + i for i in range(nb)},
        compiler_params=pltpu.CompilerParams(has_side_effects=DATAFLOW))(
        *[pltpu.with_memory_space_constraint(b, pltpu.HBM) for b in bufs], *extra)
    return (out[0], out[1], list(out[2:2 + nb])), out[-1]


def _exchange_wait(name, routes, n_routes, n, started, after):
    send_sems, recv_sems, bufs = started
    nb = len(bufs)
    in_place = nb == n

    def body(*refs):
        src_refs = refs[:n]
        land_refs = src_refs if in_place else refs[n:nb]
        for cp in _route_copies(routes, src_refs, land_refs, refs[nb], refs[nb + 1]):
            cp.wait_send()
            cp.wait_recv()

    out = pl.pallas_call(
        body, name=name, in_specs=[HBM_SPEC] * nb + [SEM_SPEC, SEM_SPEC, ANY],
        out_shape=[pltpu.HBM(b.shape, b.dtype) for b in bufs], out_specs=[HBM_SPEC] * nb,
        input_output_aliases={i: i for i in range(nb)},
        compiler_params=pltpu.CompilerParams(has_side_effects=DATAFLOW))(*bufs, send_sems, recv_sems, after)
    return list(out[:n]) if in_place else (list(out[:n]), list(out[n:]))


def _pair_sums(tag, gs, from_sibling):
    x, y, c = _place()
    return [_pair_add(f"rs_add_{tag}_{i}", g, o, c, 2 * x + y) for i, (g, o) in enumerate(zip(gs, from_sibling))]


def _reduce_scatter(tag, gs):
    sums = _pair_sums(tag, gs, _exchange(f"rs_swap_{tag}", _routes_to_sibling, 4, gs, 4))
    got = _exchange(f"rs_chips_{tag}", _routes_to_chips, 3, [s[0] for s in sums], 3)
    return [_sum4(f"rs_sum_{tag}_{i}", s[1], q) for i, (s, q) in enumerate(zip(sums, got))]


def _reduce_scatter_begin(tag, gs):
    lands = [lax.empty((4,) + g.shape[1:], g.dtype) for g in gs]
    swap, token = _exchange_start(f"rs_swap_{tag}_start", _routes_to_sibling, 4, gs, lands)
    return dict(tag=tag, n=len(gs), swap=swap), token


def _reduce_scatter_middle(state, after):
    tag, n = state["tag"], state["n"]
    gs, from_sibling = _exchange_wait(f"rs_swap_{tag}_wait", _routes_to_sibling, 4, n, state["swap"], after)
    state["sums"] = _pair_sums(tag, gs, from_sibling)
    partials = [s[0] for s in state["sums"]]
    lands = [lax.empty((3,) + p.shape[1:], p.dtype) for p in partials]
    state["chips"], token = _exchange_start(f"rs_chips_{tag}_start", _routes_to_chips, 3, partials, lands)
    return token


def _reduce_scatter_end(state, after):
    tag = state["tag"]
    _, got = _exchange_wait(f"rs_chips_{tag}_wait", _routes_to_chips, 3, state["n"], state["chips"], after)
    return [_sum4(f"rs_sum_{tag}_{i}", s[1], q) for i, (s, q) in enumerate(zip(state["sums"], got))]


def _all_gather_begin(tag, blocks, after):
    dev = 4 * lax.axis_index("x") + 2 * lax.axis_index("y") + lax.axis_index("c")
    zones = [lax.dynamic_update_slice_in_dim(lax.empty((N_DEV,) + b.shape, b.dtype), b[None], dev, axis=0) for b in blocks]
    chips, token = _exchange_start(f"gather_{tag}_chips_start", _routes_block_to_chips, 3, zones, None, after)
    return dict(tag=tag, n=len(blocks), chips=chips), token


def _all_gather_middle(state, after):
    tag, n = state["tag"], state["n"]
    zones = _exchange_wait(f"gather_{tag}_chips_wait", _routes_block_to_chips, 3, n, state["chips"], after)
    state["sibling"], token = _exchange_start(f"gather_{tag}_sibling_start", _routes_blocks_to_sibling, 4, zones, None)
    return token


def _all_gather_end(state, after):
    return _exchange_wait(f"gather_{state['tag']}_sibling_wait", _routes_blocks_to_sibling, 4, state["n"], state["sibling"], after)


PACK_UNIT = 8 * LANES


def _packed_size(shape):
    return -(-math.prod(shape) // PACK_UNIT) * PACK_UNIT


def _pack(arrays, dtype):
    parts = []
    for a in arrays:
        flat = a.reshape(-1).astype(dtype)
        parts.append(jnp.pad(flat, (0, _packed_size(a.shape) - flat.shape[0])))
    return jnp.concatenate(parts).reshape(-1, LANES)


def _unpack(flat, shapes, lead=()):
    out, row = [], 0
    for s in shapes:
        rows = _packed_size(s) // LANES
        piece = flat[..., row:row + rows, :].reshape(lead + (rows * LANES,))
        out.append(piece[..., :math.prod(s)].reshape(lead + tuple(s)))
        row += rows
    return out


def _ffn_pad_rows(a):
    n = a.shape[0] // FFN_HALF
    a = jnp.pad(a.reshape(n, FFN_HALF, a.shape[1]), ((0, 0), (0, FFN_HALF_PAD - FFN_HALF), (0, 0)))
    return a.reshape(n * FFN_HALF_PAD, a.shape[2])


def _ffn_unpad_rows(a):
    n = a.shape[0] // FFN_HALF_PAD
    return a.reshape(n, FFN_HALF_PAD, a.shape[1])[:, :FFN_HALF].reshape(n * FFN_HALF, a.shape[1])


def _ffn_pad_cols(a):
    n = a.shape[1] // FFN_HALF
    a = jnp.pad(a.reshape(a.shape[0], n, FFN_HALF), ((0, 0), (0, 0), (0, FFN_HALF_PAD - FFN_HALF)))
    return a.reshape(a.shape[0], n * FFN_HALF_PAD)


def _ffn_unpad_cols(a):
    n = a.shape[1] // FFN_HALF_PAD
    return a.reshape(a.shape[0], n, FFN_HALF_PAD)[:, :, :FFN_HALF].reshape(a.shape[0], n * FFN_HALF)


def _shard_to_send(name, shard):
    if name in ("w_in", "w_br_gdn", "w_br_gla"):
        shard = shard.T
    elif name == "ffn_w_up":
        shard = _ffn_pad_rows(shard.T)
    return shard.astype(MXU_DTYPE)


KEPT_TRANSPOSED = ("w_in", "w_br_gdn", "w_br_gla", "ffn_w_up")


def _whole_from_gathered(name, g):
    if name == "w_in":
        return _in_proj_from_shards(g)
    if name == "ffn_w_down":
        return jnp.pad(g, ((0, 0), (0, FFN_HALF_PAD - FFN_HALF), (0, 0))).reshape(FFN_PAD, g.shape[2])
    return g.reshape(N_DEV * g.shape[1], g.shape[2])


def _slots_from_whole(name, gw):
    if name == "w_in":
        return _in_proj_to_slots(gw)
    return gw.reshape(N_DEV, gw.shape[0] // N_DEV, gw.shape[1])


def _shard_from_slot(name, s):
    if name == "ffn_w_up":
        return _ffn_unpad_rows(s)
    if name == "ffn_w_down":
        return s[:FFN_HALF]
    return s


def _in_proj_pieces():
    starts, pos = {}, 0
    for n, width in IN_SPLITS:
        starts[n] = (pos, width)
        pos += width
    return [(starts[ref][0], off + lane, starts[ref][1]) for _, off, _, pieces in PAD_SEGS for ref, lane in pieces]


def _in_proj_moves():
    cs = IN_DIM // N_DEV
    moves = []
    for src, dst, n in sorted(_in_proj_pieces()):
        at = src
        while at < src + n:
            d = at // cs
            end = min(src + n, (d + 1) * cs)
            moves.append((d, at - d * cs, dst + at - src, end - at))
            at = end
    return moves


RELAYOUT_LANES = 128


def _in_proj_from_shards(g):
    _, cs, D = g.shape

    def body(g_ref, o_ref):
        o_ref[...] = jnp.zeros_like(o_ref)
        for d, i0, r0, n in _in_proj_moves():
            o_ref[r0:r0 + n, :] = g_ref[d, i0:i0 + n, :]

    cb = RELAYOUT_LANES
    return pl.pallas_call(body, name="w_in_rows_in", grid=(D // cb,),
                          in_specs=[pl.BlockSpec((N_DEV, cs, cb), lambda j: (0, 0, j))],
                          out_specs=pl.BlockSpec((IN_PAD, cb), lambda j: (0, j)),
                          out_shape=jax.ShapeDtypeStruct((IN_PAD, D), g.dtype), compiler_params=_cparams(("parallel",)))(g)


def _in_proj_to_slots(gw):
    D = gw.shape[1]
    cs = IN_DIM // N_DEV

    def body(x_ref, o_ref):
        for d, i0, r0, n in _in_proj_moves():
            o_ref[d, i0:i0 + n, :] = x_ref[r0:r0 + n, :]

    cb = RELAYOUT_LANES
    return pl.pallas_call(body, name="w_in_rows_out", grid=(D // cb,),
                          in_specs=[pl.BlockSpec((IN_PAD, cb), lambda j: (0, j))],
                          out_specs=pl.BlockSpec((N_DEV, cs, cb), lambda j: (0, 0, j)),
                          out_shape=jax.ShapeDtypeStruct((N_DEV, cs, D), gw.dtype), compiler_params=_cparams(("parallel",)))(gw)


def _pad_in_proj_rows(w):
    rows, at = [], 0
    for src, dst, n in sorted(_in_proj_pieces(), key=lambda p: p[1]):
        if dst > at:
            rows.append(jnp.zeros((dst - at, w.shape[1]), w.dtype))
        rows.append(w[src:src + n])
        at = dst + n
    rows.append(jnp.zeros((IN_PAD - at, w.shape[1]), w.dtype))
    return jnp.concatenate(rows, axis=0)


def _unpad_in_proj_rows(wp):
    return jnp.concatenate([wp[dst:dst + n] for _, dst, n in sorted(_in_proj_pieces())], axis=0)


def _lane_pad(a, width=LANES):
    return jnp.pad(a, ((0, 0), (0, width - a.shape[1])))


def _seg_blk(h, name, rows):
    off, width = SEG[name]
    return (h, rows, width, off // width)


def _ln_both(xs_, ps_):
    (y,) = _ln_fn(xs_, ps_)
    return (y, y)


def _behind(param, hooks, stage, *seen):
    if hooks is None or stage not in hooks:
        return param
    token = hooks[stage](*seen)
    return param if token is None else param + token[0:1, 0:1]


def _layer_fwd(l, x, x_mx, W, sp, hooks=None):
    T = x.shape[0]
    n64, ngla, ntok = T // SSD_CHUNK, T // GLA_BLOCK, T // 256
    h = _mm(f"in_proj_{l}", x_mx, W["w_in"], "nt")
    xbc = _conv_silu_fwd(f"ssd_conv_{l}", h, SEG["xbc"][0], sp["ssd_conv_w"], sp["ssd_conv_b"])
    gqkv = _conv_silu_fwd(f"gdn_conv_{l}", h, SEG["gqkv"][0], sp["gdn_conv_w"], None)

    ssd_in = [(xbc, SSD_CHUNK, SSD_XBC, 0), _seg_blk(h, "dt", SSD_CHUNK), _seg_blk(h, "z", SSD_CHUNK)]
    ssd_p = [sp["ssd_dt_bias"], sp["ssd_a_log"], sp["ssd_d"], sp["ssd_norm_w"]]
    o_ssd, ssd_states = _chain_fwd(f"ssd_fwd_{l}", _ssd_chunk, n64, ssd_in, ssd_p, [(SSD_CHUNK, SSD_INNER, MXU_DTYPE)],
                                   (SSD_STATE, SSD_INNER))
    o_gdn, gdn_saved = _gdn_forward(str(l), gqkv, h, dict(sp, gdn_a_log=_behind(sp["gdn_a_log"], hooks, "ssd", o_ssd)))
    gla_in = [_seg_blk(h, "lqkv", GLA_BLOCK), _seg_blk(h, "lglr", GLA_BLOCK), _seg_blk(h, "lr", GLA_BLOCK)]
    gla_p = [jnp.pad(sp["gla_gate_w2"], ((0, LANES - GLA_RANK), (0, 0))), sp["gla_gate_b"], sp["gla_norm_w"]]
    o_gla, gla_states = _chain_fwd(f"gla_fwd_{l}", _gla_block, ngla, gla_in, gla_p, [(GLA_BLOCK, GLA_V, MXU_DTYPE)],
                                   (GLA_VAL_DIM, GLA_K))
    ln1_p = [_behind(sp["ln1_g"], hooks, "mixed", o_gdn), sp["ln1_b"]]
    y_ssd = _mm(f"br_ssd_{l}", o_ssd, W["w_br_ssd"])
    y_gdn = _mm(f"br_gdn_{l}", o_gdn, W["w_br_gdn"], "nt")
    y_gla = _mm(f"br_gla_{l}", o_gla, W["w_br_gla"], "nt")
    merge_in = [_seg_blk(h, "gates", 256), (y_ssd, 256, D_MODEL, 0), (y_gdn, 256, D_MODEL, 0), (y_gla, 256, D_MODEL, 0)]
    (mix,) = _chain_fwd(f"merge_{l}", _merge_fn, ntok, merge_in, [], [(256, D_MODEL, MXU_DTYPE)])
    r1 = _mm(f"out_proj_{l}", mix, W["w_out"])
    both = [(256, D_MODEL, F32), (256, D_MODEL, MXU_DTYPE)]
    x1, x1_mx = _chain_fwd(f"ln1_{l}", _ln_both, ntok, [(x, 256, D_MODEL, 0), (r1, 256, D_MODEL, 0)], ln1_p, both)
    up = _mm(f"ffn_up_{l}", x1_mx, W["ffn_w_up"], "nt")
    act = _ffn_glu_fwd(f"ffn_glu_{l}", up, sp["ffn_conv_w_pad"], sp["ffn_conv_b_pad"], MXU_DTYPE)
    ln2_p = [_behind(sp["ln2_g"], hooks, "ffn_act", act), sp["ln2_b"]]
    r2 = _mm(f"ffn_down_{l}", act, W["ffn_w_down"], tn=1024, tk=1024)
    x2, x2_mx = _chain_fwd(f"ln2_{l}", _ln_both, ntok, [(x1, 256, D_MODEL, 0), (r2, 256, D_MODEL, 0)], ln2_p, both)
    saved = dict(x=x, x_mx=x_mx, h=h, xbc=xbc, gqkv=gqkv, ssd_in=ssd_in, ssd_p=ssd_p, ssd_states=ssd_states,
                 gdn=gdn_saved, gla_in=gla_in, gla_p=gla_p, gla_states=gla_states, o_ssd=o_ssd,
                 o_gdn=o_gdn, o_gla=o_gla, merge_in=merge_in, mix=mix, r1=r1, ln1_p=ln1_p, x1=x1, x1_mx=x1_mx, up=up, act=act,
                 r2=r2, ln2_p=ln2_p)
    return x2, x2_mx, saved


def _layer_bwd(l, dx2, W, sp, sv, hooks=None):
    T = dx2.shape[0]
    n64, ngla, ntok = T // SSD_CHUNK, T // GLA_BLOCK, T // 256
    bf = MXU_DTYPE
    gw, gs = {}, {}
    ln2_p = [_behind(sv["ln2_p"][0], hooks, "start"), sv["ln2_p"][1]]
    (dx1_a, dr2), (gs["ln2_g"], gs["ln2_b"]) = _chain_bwd(
        f"ln2_bwd_{l}", _ln_fn, ntok, [(sv["x1"], 256, D_MODEL, 0), (sv["r2"], 256, D_MODEL, 0)], ln2_p,
        [(dx2, 256, D_MODEL)], dx_dtypes=[F32, bf])
    gw["ffn_w_down"] = _mm(f"ffn_down_dw_{l}", sv["act"], dr2, "tn", tn=1024)
    dact = _mm(f"ffn_down_dx_{l}", dr2, W["ffn_w_down"], "nt")
    dg, du, dwg, dwu, dbg, dbu = _ffn_glu_bwd(f"ffn_glu_bwd_{l}", sv["up"], sp["ffn_conv_w_pad"], sp["ffn_conv_b_pad"], dact, bf)
    gs["ffn_conv_w"] = _ffn_unpad_cols(jnp.concatenate([dwg, dwu], axis=1))
    gs["ffn_conv_b"] = _ffn_unpad_cols(jnp.concatenate([dbg, dbu], axis=1))
    dup = jnp.concatenate([dg, du], axis=1)
    gw["ffn_w_up"] = _mm(f"ffn_up_dw_{l}", dup, sv["x1_mx"], "tn", tn=1024)
    dx1 = _mm(f"ffn_up_dx_{l}", dup, W["ffn_w_up"], "nn", tm=1024, tn=1024, tk=1024, plus=dx1_a)
    ln1_p = [_behind(sv["ln1_p"][0], hooks, "ffn", dx1), sv["ln1_p"][1]]
    (dx_a, dr1), (gs["ln1_g"], gs["ln1_b"]) = _chain_bwd(
        f"ln1_bwd_{l}", _ln_fn, ntok, [(sv["x"], 256, D_MODEL, 0), (sv["r1"], 256, D_MODEL, 0)], ln1_p,
        [(dx1, 256, D_MODEL)], dx_dtypes=[F32, bf])
    gw["w_out"] = _mm(f"out_proj_dw_{l}", sv["mix"], dr1, "tn")
    dmix = _mm(f"out_proj_dx_{l}", dr1, W["w_out"], "nt")
    (dgates, dy_ssd, dy_gdn, dy_gla), _ = _chain_bwd(f"merge_bwd_{l}", _merge_fn, ntok, sv["merge_in"], [],
                                                     [(dmix, 256, D_MODEL)], dx_dtypes=[bf, bf, bf, bf])
    gw["w_br_ssd"] = _mm(f"br_ssd_dw_{l}", sv["o_ssd"], dy_ssd, "tn")
    gw["w_br_gdn"] = _mm(f"br_gdn_dw_{l}", dy_gdn, sv["o_gdn"], "tn")
    gw["w_br_gla"] = _mm(f"br_gla_dw_{l}", dy_gla, sv["o_gla"], "tn")
    do_ssd = _mm(f"br_ssd_dx_{l}", dy_ssd, W["w_br_ssd"], "nt")
    do_gdn = _mm(f"br_gdn_dx_{l}", dy_gdn, W["w_br_gdn"], "nn")
    do_gla = _mm(f"br_gla_dx_{l}", dy_gla, W["w_br_gla"], "nn")

    ssd_p = [_behind(sv["ssd_p"][0], hooks, "branches", do_gla, gw)] + list(sv["ssd_p"][1:])
    (dxbc, ddt, dz), dps = _chain_bwd(f"ssd_bwd_{l}", _ssd_chunk, n64, sv["ssd_in"], ssd_p,
                                      [(do_ssd, SSD_CHUNK, SSD_INNER)], sprev=sv["ssd_states"], dx_dtypes=[F32, bf, bf])
    gs["ssd_dt_bias"], gs["ssd_a_log"], gs["ssd_d"], gs["ssd_norm_w"] = dps
    gdn_sv = dict(sv["gdn"], scan_p=[_behind(sv["gdn"]["scan_p"][0], hooks, "ssd", dz)])
    dgqkv, dgab, dgg, gs["gdn_a_log"], gs["gdn_dt_bias"], gs["gdn_norm_w"] = _gdn_backward(str(l), do_gdn, gdn_sv, bf)
    (dlqkv, dlglr, dlr), dps = _chain_bwd(f"gla_bwd_{l}", _gla_block, ngla, sv["gla_in"], sv["gla_p"],
                                          [(do_gla, GLA_BLOCK, GLA_V)], sprev=sv["gla_states"], dx_dtypes=[bf, bf, bf])
    gs["gla_gate_w2"], gs["gla_gate_b"], gs["gla_norm_w"] = dps[0][:GLA_RANK], dps[1], dps[2]
    dxbc_pre, gs["ssd_conv_w"], gs["ssd_conv_b"] = _conv_silu_bwd(
        f"ssd_conv_bwd_{l}", sv["h"], SEG["xbc"][0], sp["ssd_conv_w"], sp["ssd_conv_b"], dxbc, bf)
    dgqkv_pre, gs["gdn_conv_w"] = _conv_silu_bwd(f"gdn_conv_bwd_{l}", sv["h"], SEG["gqkv"][0], sp["gdn_conv_w"], None, dgqkv, bf)
    pieces = dict(gates=dgates, xbc=dxbc_pre, gqkv=dgqkv_pre, z=dz, lqkv=dlqkv, gg=dgg, lr=dlr, dt=ddt, gab=dgab, lglr=dlglr)
    cols = [pieces[name] for name, _, _, _ in PAD_SEGS]
    cols.append(jnp.zeros((T, IN_PAD - PAD_SEGS[-1][1] - PAD_SEGS[-1][2]), bf))
    dh = jnp.concatenate(cols, axis=1)
    gw["w_in"] = _mm(f"in_proj_dw_{l}", dh, sv["x_mx"], "tn", tn=1024)
    behind = hooks["w_in_grad"](gw) if hooks is not None and "w_in_grad" in hooks else None
    dx = _mm(f"in_proj_dx_{l}", dh, W["w_in"], "nn", tm=1024, tn=1024, tk=IN_PAD // 4, after=behind, plus=dx_a)
    return dx, gw, gs


def _ln_sum_fn(xs_, ps_):
    (y,) = _ln_fn(xs_, ps_)
    return (y, y)


def _small_2d(name, a):
    return a.reshape(1, -1) if a.ndim == 1 else a


def kernel(x, w_in, ssd_conv_w, ssd_conv_b, ssd_dt_bias, ssd_a_log, ssd_d, ssd_norm_w, gdn_conv_w, gdn_a_log, gdn_dt_bias, gdn_norm_w, gla_gate_w2, gla_gate_b, gla_norm_w, w_br_ssd, w_br_gdn, w_br_gla, w_out, ln1_g, ln1_b, ffn_w_up, ffn_conv_w, ffn_conv_b, ffn_w_down, ln2_g, ln2_b, loss_target, m_w_in, m_ssd_conv_w, m_ssd_conv_b, m_ssd_dt_bias, m_ssd_a_log, m_ssd_d, m_ssd_norm_w, m_gdn_conv_w, m_gdn_a_log, m_gdn_dt_bias, m_gdn_norm_w, m_gla_gate_w2, m_gla_gate_b, m_gla_norm_w, m_w_br_ssd, m_w_br_gdn, m_w_br_gla, m_w_out, m_ln1_g, m_ln1_b, m_ffn_w_up, m_ffn_conv_w, m_ffn_conv_b, m_ffn_w_down, m_ln2_g, m_ln2_b, v_w_in, v_ssd_conv_w, v_ssd_conv_b, v_ssd_dt_bias, v_ssd_a_log, v_ssd_d, v_ssd_norm_w, v_gdn_conv_w, v_gdn_a_log, v_gdn_dt_bias, v_gdn_norm_w, v_gla_gate_w2, v_gla_gate_b, v_gla_norm_w, v_w_br_ssd, v_w_br_gdn, v_w_br_gla, v_w_out, v_ln1_g, v_ln1_b, v_ffn_w_up, v_ffn_conv_w, v_ffn_conv_b, v_ffn_w_down, v_ln2_g, v_ln2_b):
    args = locals()
    w = {n: args[n] for n in WEIGHTS}
    m = {n: args["m_" + n] for n in WEIGHTS}
    v = {n: args["v_" + n] for n in WEIGHTS}
    dev = 4 * lax.axis_index("x") + 2 * lax.axis_index("y") + lax.axis_index("c")
    xl = x[0]
    tgt = loss_target[0]

    late = BIG[1:]

    def send(names, l):
        return [_shard_to_send(n, w[n][l]) for n in names]

    def whole_weights(names, got):
        return {n: _whole_from_gathered(n, g) for n, g in zip(names, got)}

    got0 = _all_gather("gather_first", send(BIG[:1], 0) + [w[n] for n in SMALL_SHARDED])
    gather0, token0 = _all_gather_begin("w_0", send(late, 0), got0[0])
    W = [whole_weights(BIG[:1], got0[:1]), None]
    whole = dict(w)
    for n, s in zip(SMALL_SHARDED, got0[1:]):
        whole[n] = jnp.transpose(s, (1, 2, 0, 3)).reshape(s.shape[1], s.shape[2], N_DEV * s.shape[3])
    SP = [{n: _small_2d(n, whole[n][l]) for n in SMALL} for l in range(DEPTH)]
    for sp in SP:
        sp["ffn_conv_w_pad"] = _ffn_pad_cols(sp["ffn_conv_w"])
        sp["ffn_conv_b_pad"] = _ffn_pad_cols(sp["ffn_conv_b"])

    held = {}

    def late_weights_cross(o_ssd):
        token = _all_gather_middle(gather0, o_ssd)
        held["gather1"], token1 = _all_gather_begin("w_1", send(BIG, 1), o_ssd)
        return token + token1

    def late_weights_arrive(mixed):
        W[0].update(whole_weights(late, _all_gather_end(gather0, mixed)))

    fwd_hooks = {"ssd": late_weights_cross, "mixed": late_weights_arrive,
                 "ffn_act": lambda act: _all_gather_middle(held["gather1"], act)}
    saved = [None] * DEPTH
    act, act_mx, saved[0] = _layer_fwd(0, xl, (xl + token0[0, 0]).astype(MXU_DTYPE), W[0], SP[0], hooks=fwd_hooks)
    W[1] = whole_weights(BIG, _all_gather_end(held["gather1"], act))
    act, act_mx, saved[1] = _layer_fwd(1, act, act_mx, W[1], SP[1])
    dy, loss_parts = _loss_head(act, tgt)
    loss = lax.psum(jnp.sum(loss_parts), ("x", "y", "c"))

    def slots_of(names, gw):
        return [_slots_from_whole(n, gw[n]) for n in names]

    grads = {}
    GS = [None] * DEPTH
    dy, gw, GS[1] = _layer_bwd(1, dy, W[1], SP[1], saved[1])
    reduce1, reduce1_token = _reduce_scatter_begin("1", slots_of(BIG, gw))

    def late_grads_leave(seen, gw0):
        held["reduce0"], token = _reduce_scatter_begin("0", slots_of(late, gw0))
        return token

    def w_in_grad_leaves(gw0):
        held["reduce_first"], token = _reduce_scatter_begin("first", slots_of(BIG[:1], gw0))
        return token

    bwd_hooks = {"start": lambda: reduce1_token, "ffn": lambda seen: _reduce_scatter_middle(reduce1, seen),
                 "branches": late_grads_leave, "ssd": lambda seen: _reduce_scatter_middle(held["reduce0"], seen),
                 "w_in_grad": w_in_grad_leaves}
    dy, gw, GS[0] = _layer_bwd(0, dy, W[0], SP[0], saved[0], hooks=bwd_hooks)
    small_shapes = [whole[n].shape for n in SMALL]
    gs_flat = _pack([jnp.stack([GS[l][n].reshape(whole[n].shape[1:]) for l in range(DEPTH)]) for n in SMALL], F32)
    (gs_all,) = _all_gather("gather_small_grads", [gs_flat])
    first_token = _reduce_scatter_middle(held["reduce_first"], gs_all)
    red1 = _reduce_scatter_end(reduce1, dy)
    red0_late = _reduce_scatter_end(held["reduce0"], dy)
    grad_x = dy[None]
    kept_t = KEPT_TRANSPOSED
    grads_k = {n: jnp.stack([_shard_from_slot(n, red0_late[i]), _shard_from_slot(n, red1[i + 1])]) for i, n in enumerate(late)}

    def mine(n, a):
        if n in SMALL_SHARDED:
            cs = a.shape[-1] // N_DEV
            return lax.dynamic_slice_in_dim(a, dev * cs, cs, axis=a.ndim - 1)
        return a

    m_whole, v_whole = {}, {}
    for n in SMALL:
        reps = (1, 1, N_DEV) if n in SMALL_SHARDED else (1,) * m[n].ndim
        m_whole[n], v_whole[n] = jnp.tile(m[n], reps), jnp.tile(v[n], reps)
    outs = _adamw_small(gs_all, _pack([whole[n] for n in SMALL], F32) + first_token[0:1, 0:1], _pack([m_whole[n] for n in SMALL], F32),
                        _pack([v_whole[n] for n in SMALL], F32))
    g_s, d_s, m_s, v_s = [_unpack(o, small_shapes) for o in outs]
    delta, new_m, new_v = {}, {}, {}
    for i, n in enumerate(SMALL):
        grads[n], delta[n], new_m[n], new_v[n] = mine(n, g_s[i]), mine(n, d_s[i]), mine(n, m_s[i]), mine(n, v_s[i])
    for n in late + BIG[:1]:
        if n == "w_in":
            done = sum(new_v[k].reshape(-1)[0:1] for k in late + SMALL[:1])
            (first0,) = _reduce_scatter_end(held["reduce_first"], done)
            grads_k[n] = jnp.stack([first0, red1[0]])
        view = (lambda a: jnp.transpose(a, (0, 2, 1))) if n in kept_t else (lambda a: a)
        outs = _adamw(f"adamw_{n}", view(w[n]), grads_k[n], view(m[n]), view(v[n]), after=None if n == "w_in" else first_token)
        grads[n], delta[n], new_m[n], new_v[n] = view(grads_k[n]), view(outs[0]), view(outs[1]), view(outs[2])

    return (loss, grad_x, *[grads[n] for n in WEIGHTS], *[delta[n] for n in WEIGHTS], *[new_m[n] for n in WEIGHTS],
            *[new_v[n] for n in WEIGHTS])
```
